```python
import math
import jax, jax.numpy as jnp
from jax import lax
import numpy as np

D_MODEL = 1024
BATCH = 8
SEQ = 8192
DEPTH = 1

N_META = 16
MIX_WIDTH = D_MODEL
SSM_WIDTH = MIX_WIDTH // 2
POOL_WIDTH = MIX_WIDTH - SSM_WIDTH
SSM_GROUP_CH = 16
SSM_GROUPS = SSM_WIDTH // SSM_GROUP_CH
SSM_STATE = 64
DT_MIN = 1e-3
DT_MAX = 1e-1
POOL_WINDOWS = (2, 4, 8, 16)
POOL_GROUPS = len(POOL_WINDOWS)
POOL_GROUP = POOL_WIDTH // POOL_GROUPS
D_FF = ((8 * D_MODEL // 3 + 127) // 128) * 128
RMS_EPS = 1e-6

kernel_name = "hymba_s5_poolformer_macaron_layer"


def rms_norm(x, g):
    xf = x.astype(jnp.float32)
    y = xf * lax.rsqrt(jnp.mean(xf * xf, axis=-1, keepdims=True) + RMS_EPS)
    return (y * g.astype(jnp.float32)).astype(x.dtype)


def swiglu(h, w_gate, w_up, w_down):
    return (jax.nn.silu(h @ w_gate) * (h @ w_up)) @ w_down


def _complex_scan_combine(e1, e2):
    a1r, a1i, b1r, b1i = e1
    a2r, a2i, b2r, b2i = e2
    ar = a2r * a1r - a2i * a1i
    ai = a2r * a1i + a2i * a1r
    a2r_b = a2r[:, None]
    a2i_b = a2i[:, None]
    br = a2r_b * b1r - a2i_b * b1i + b2r
    bi = a2r_b * b1i + a2i_b * b1r + b2i
    return (ar, ai, br, bi)


def s5_mixer(u, lam_re, lam_im, log_dt, b_re, b_im, c_re, c_im, d_skip, w_glu):
    Bt, L, _ = u.shape
    ug = u.astype(jnp.float32).reshape(Bt, L, SSM_GROUPS, SSM_GROUP_CH)
    lr = lam_re.astype(jnp.float32)
    li = lam_im.astype(jnp.float32)
    dt = jnp.exp(log_dt.astype(jnp.float32))[:, None]
    decay = jnp.exp(lr * dt)
    ang = li * dt
    a_re = decay * jnp.cos(ang)
    a_im = decay * jnp.sin(ang)
    nr = a_re - 1.0
    den = lr * lr + li * li
    q_re = (nr * lr + a_im * li) / den
    q_im = (a_im * lr - nr * li) / den
    br = b_re.astype(jnp.float32)
    bi = b_im.astype(jnp.float32)
    bb_re = q_re[..., None] * br - q_im[..., None] * bi
    bb_im = q_re[..., None] * bi + q_im[..., None] * br
    bu_re = jnp.einsum('blgh,gnh->lbgn', ug, bb_re)
    bu_im = jnp.einsum('blgh,gnh->lbgn', ug, bb_im)
    A_re = jnp.broadcast_to(a_re, (L,) + a_re.shape)
    A_im = jnp.broadcast_to(a_im, (L,) + a_im.shape)
    _, _, x_re, x_im = lax.associative_scan(_complex_scan_combine, (A_re, A_im, bu_re, bu_im), axis=0)
    y = (jnp.einsum('lbgn,ghn->blgh', x_re, c_re.astype(jnp.float32))
         - jnp.einsum('lbgn,ghn->blgh', x_im, c_im.astype(jnp.float32)))
    y = y + d_skip.astype(jnp.float32).reshape(SSM_GROUPS, SSM_GROUP_CH) * ug
    z = jnp.einsum('blgh,ghk->blgk', jax.nn.gelu(y), w_glu.astype(jnp.float32))
    out = z[..., :SSM_GROUP_CH] * jax.nn.sigmoid(z[..., SSM_GROUP_CH:])
    return out.reshape(Bt, L, SSM_WIDTH).astype(u.dtype)


def pool_mixer(u, pool_w, pool_scale):
    Bt, L, _ = u.shape
    uf = u.astype(jnp.float32)
    cs = jnp.cumsum(uf, axis=1)
    cs = jnp.concatenate([jnp.zeros((Bt, 1, POOL_WIDTH), jnp.float32), cs], axis=1)
    t1 = jnp.arange(1, L + 1, dtype=jnp.float32)
    outs = []
    for gi, w in enumerate(POOL_WINDOWS):
        lo_c, hi_c = gi * POOL_GROUP, (gi + 1) * POOL_GROUP
        c = cs[:, :, lo_c:hi_c]
        hi = c[:, 1:]
        lo = jnp.concatenate([jnp.zeros((Bt, w - 1, POOL_GROUP), jnp.float32), c[:, :L - w + 1]], axis=1)
        cnt = jnp.minimum(t1, float(w))[None, :, None]
        outs.append((hi - lo) / cnt - uf[:, :, lo_c:hi_c])
    pooled = jnp.stack(outs, axis=2)
    mixed = jnp.einsum('blgc,gcd->blgd', pooled, pool_w.astype(jnp.float32))
    mixed = mixed * pool_scale.astype(jnp.float32).reshape(POOL_GROUPS, POOL_GROUP)
    return mixed.reshape(Bt, L, POOL_WIDTH).astype(u.dtype)


def _fwd_setup_inputs(seed: int = 0) -> dict:
    key = jax.random.key(seed)
    ks = jax.random.split(key, 40)
    f32 = jnp.float32

    def nrm(k, shape, scale):
        return jax.random.normal(k, shape, f32) * scale

    def gain(k, shape):
        return 1.0 + 0.02 * jax.random.normal(k, shape, f32)

    Dp = DEPTH
    G, N, H = SSM_GROUPS, SSM_STATE, SSM_GROUP_CH
    n_idx = jnp.arange(N, dtype=f32)
    return {
        "x": nrm(ks[0], (BATCH, SEQ, D_MODEL), 1.0),
        "meta_tokens": nrm(ks[1], (N_META, D_MODEL), 1.0),
        "ffn1_pre_norm": gain(ks[2], (Dp, D_MODEL)),
        "ffn1_post_norm": gain(ks[3], (Dp, D_MODEL)),
        "ffn1_w_gate": nrm(ks[4], (Dp, D_MODEL, D_FF), D_MODEL ** -0.5),
        "ffn1_w_up": nrm(ks[5], (Dp, D_MODEL, D_FF), D_MODEL ** -0.5),
        "ffn1_w_down": nrm(ks[6], (Dp, D_FF, D_MODEL), D_FF ** -0.5),
        "mix_pre_norm": gain(ks[7], (Dp, D_MODEL)),
        "mix_post_norm": gain(ks[8], (Dp, D_MODEL)),
        "w_in": nrm(ks[9], (Dp, D_MODEL, MIX_WIDTH), D_MODEL ** -0.5),
        "ssm_lambda_re": -0.5 + 0.01 * jax.random.normal(ks[10], (Dp, G, N), f32),
        "ssm_lambda_im": math.pi * n_idx + 0.01 * jax.random.normal(ks[11], (Dp, G, N), f32),
        "ssm_log_dt": jax.random.uniform(ks[12], (Dp, G), f32, math.log(DT_MIN), math.log(DT_MAX)),
        "ssm_b_re": nrm(ks[13], (Dp, G, N, H), (2.0 * H) ** -0.5),
        "ssm_b_im": nrm(ks[14], (Dp, G, N, H), (2.0 * H) ** -0.5),
        "ssm_c_re": nrm(ks[15], (Dp, G, H, N), (2.0 * N) ** -0.5),
        "ssm_c_im": nrm(ks[16], (Dp, G, H, N), (2.0 * N) ** -0.5),
        "ssm_d": nrm(ks[17], (Dp, SSM_WIDTH), 1.0),
        "ssm_w_glu": nrm(ks[18], (Dp, G, H, 2 * H), H ** -0.5),
        "pool_w": nrm(ks[19], (Dp, POOL_GROUPS, POOL_GROUP, POOL_GROUP), POOL_GROUP ** -0.5),
        "pool_scale": gain(ks[20], (Dp, POOL_WIDTH)),
        "ssm_out_norm": gain(ks[21], (Dp, SSM_WIDTH)),
        "pool_out_norm": gain(ks[22], (Dp, POOL_WIDTH)),
        "w_out": nrm(ks[23], (Dp, MIX_WIDTH, D_MODEL), MIX_WIDTH ** -0.5),
        "ffn2_pre_norm": gain(ks[24], (Dp, D_MODEL)),
        "ffn2_post_norm": gain(ks[25], (Dp, D_MODEL)),
        "ffn2_w_gate": nrm(ks[26], (Dp, D_MODEL, D_FF), D_MODEL ** -0.5),
        "ffn2_w_up": nrm(ks[27], (Dp, D_MODEL, D_FF), D_MODEL ** -0.5),
        "ffn2_w_down": nrm(ks[28], (Dp, D_FF, D_MODEL), D_FF ** -0.5),
    }


def _fwd_reference(x, meta_tokens, ffn1_pre_norm, ffn1_post_norm, ffn1_w_gate, ffn1_w_up, ffn1_w_down,
              mix_pre_norm, mix_post_norm, w_in, ssm_lambda_re, ssm_lambda_im, ssm_log_dt,
              ssm_b_re, ssm_b_im, ssm_c_re, ssm_c_im, ssm_d, ssm_w_glu, pool_w, pool_scale,
              ssm_out_norm, pool_out_norm, w_out, ffn2_pre_norm, ffn2_post_norm,
              ffn2_w_gate, ffn2_w_up, ffn2_w_down):
    Bt = x.shape[0]
    meta = jnp.broadcast_to(meta_tokens.astype(x.dtype)[None], (Bt, N_META, D_MODEL))
    h = jnp.concatenate([meta, x], axis=1)
    for i in range(DEPTH):
        f = swiglu(rms_norm(h, ffn1_pre_norm[i]), ffn1_w_gate[i], ffn1_w_up[i], ffn1_w_down[i])
        h = h + 0.5 * rms_norm(f, ffn1_post_norm[i])
        proj = rms_norm(h, mix_pre_norm[i]) @ w_in[i]
        y_ssm = s5_mixer(proj[..., :SSM_WIDTH], ssm_lambda_re[i], ssm_lambda_im[i], ssm_log_dt[i],
                         ssm_b_re[i], ssm_b_im[i], ssm_c_re[i], ssm_c_im[i], ssm_d[i], ssm_w_glu[i])
        y_pool = pool_mixer(proj[..., SSM_WIDTH:], pool_w[i], pool_scale[i])
        mixed = jnp.concatenate([rms_norm(y_ssm, ssm_out_norm[i]),
                                 rms_norm(y_pool, pool_out_norm[i])], axis=-1) @ w_out[i]
        h = h + rms_norm(mixed, mix_post_norm[i])
        f = swiglu(rms_norm(h, ffn2_pre_norm[i]), ffn2_w_gate[i], ffn2_w_up[i], ffn2_w_down[i])
        h = h + 0.5 * rms_norm(f, ffn2_post_norm[i])
    return h[:, N_META:]


import jax as _jax
import jax.numpy as _jnp

TWIN_FORMAT = 'train_step'
FWD_PARAMS = ['x', 'meta_tokens', 'ffn1_pre_norm', 'ffn1_post_norm', 'ffn1_w_gate', 'ffn1_w_up', 'ffn1_w_down', 'mix_pre_norm', 'mix_post_norm', 'w_in', 'ssm_lambda_re', 'ssm_lambda_im', 'ssm_log_dt', 'ssm_b_re', 'ssm_b_im', 'ssm_c_re', 'ssm_c_im', 'ssm_d', 'ssm_w_glu', 'pool_w', 'pool_scale', 'ssm_out_norm', 'pool_out_norm', 'w_out', 'ffn2_pre_norm', 'ffn2_post_norm', 'ffn2_w_gate', 'ffn2_w_up', 'ffn2_w_down']
TWIN_WEIGHTS = ['meta_tokens', 'ffn1_pre_norm', 'ffn1_post_norm', 'ffn1_w_gate', 'ffn1_w_up', 'ffn1_w_down', 'mix_pre_norm', 'mix_post_norm', 'w_in', 'ssm_lambda_re', 'ssm_lambda_im', 'ssm_log_dt', 'ssm_b_re', 'ssm_b_im', 'ssm_c_re', 'ssm_c_im', 'ssm_d', 'ssm_w_glu', 'pool_w', 'pool_scale', 'ssm_out_norm', 'pool_out_norm', 'w_out', 'ffn2_pre_norm', 'ffn2_post_norm', 'ffn2_w_gate', 'ffn2_w_up', 'ffn2_w_down']
TWIN_DIFF_INPUT = 'x'
TWIN_INPUTS = ['x', 'meta_tokens', 'ffn1_pre_norm', 'ffn1_post_norm', 'ffn1_w_gate', 'ffn1_w_up', 'ffn1_w_down', 'mix_pre_norm', 'mix_post_norm', 'w_in', 'ssm_lambda_re', 'ssm_lambda_im', 'ssm_log_dt', 'ssm_b_re', 'ssm_b_im', 'ssm_c_re', 'ssm_c_im', 'ssm_d', 'ssm_w_glu', 'pool_w', 'pool_scale', 'ssm_out_norm', 'pool_out_norm', 'w_out', 'ffn2_pre_norm', 'ffn2_post_norm', 'ffn2_w_gate', 'ffn2_w_up', 'ffn2_w_down', 'loss_target', 'm_meta_tokens', 'm_ffn1_pre_norm', 'm_ffn1_post_norm', 'm_ffn1_w_gate', 'm_ffn1_w_up', 'm_ffn1_w_down', 'm_mix_pre_norm', 'm_mix_post_norm', 'm_w_in', 'm_ssm_lambda_re', 'm_ssm_lambda_im', 'm_ssm_log_dt', 'm_ssm_b_re', 'm_ssm_b_im', 'm_ssm_c_re', 'm_ssm_c_im', 'm_ssm_d', 'm_ssm_w_glu', 'm_pool_w', 'm_pool_scale', 'm_ssm_out_norm', 'm_pool_out_norm', 'm_w_out', 'm_ffn2_pre_norm', 'm_ffn2_post_norm', 'm_ffn2_w_gate', 'm_ffn2_w_up', 'm_ffn2_w_down', 'v_meta_tokens', 'v_ffn1_pre_norm', 'v_ffn1_post_norm', 'v_ffn1_w_gate', 'v_ffn1_w_up', 'v_ffn1_w_down', 'v_mix_pre_norm', 'v_mix_post_norm', 'v_w_in', 'v_ssm_lambda_re', 'v_ssm_lambda_im', 'v_ssm_log_dt', 'v_ssm_b_re', 'v_ssm_b_im', 'v_ssm_c_re', 'v_ssm_c_im', 'v_ssm_d', 'v_ssm_w_glu', 'v_pool_w', 'v_pool_scale', 'v_ssm_out_norm', 'v_pool_out_norm', 'v_w_out', 'v_ffn2_pre_norm', 'v_ffn2_post_norm', 'v_ffn2_w_gate', 'v_ffn2_w_up', 'v_ffn2_w_down']
TWIN_OUTPUTS = ['loss', 'grad_x', 'grad_meta_tokens', 'grad_ffn1_pre_norm', 'grad_ffn1_post_norm', 'grad_ffn1_w_gate', 'grad_ffn1_w_up', 'grad_ffn1_w_down', 'grad_mix_pre_norm', 'grad_mix_post_norm', 'grad_w_in', 'grad_ssm_lambda_re', 'grad_ssm_lambda_im', 'grad_ssm_log_dt', 'grad_ssm_b_re', 'grad_ssm_b_im', 'grad_ssm_c_re', 'grad_ssm_c_im', 'grad_ssm_d', 'grad_ssm_w_glu', 'grad_pool_w', 'grad_pool_scale', 'grad_ssm_out_norm', 'grad_pool_out_norm', 'grad_w_out', 'grad_ffn2_pre_norm', 'grad_ffn2_post_norm', 'grad_ffn2_w_gate', 'grad_ffn2_w_up', 'grad_ffn2_w_down', 'delta_meta_tokens', 'delta_ffn1_pre_norm', 'delta_ffn1_post_norm', 'delta_ffn1_w_gate', 'delta_ffn1_w_up', 'delta_ffn1_w_down', 'delta_mix_pre_norm', 'delta_mix_post_norm', 'delta_w_in', 'delta_ssm_lambda_re', 'delta_ssm_lambda_im', 'delta_ssm_log_dt', 'delta_ssm_b_re', 'delta_ssm_b_im', 'delta_ssm_c_re', 'delta_ssm_c_im', 'delta_ssm_d', 'delta_ssm_w_glu', 'delta_pool_w', 'delta_pool_scale', 'delta_ssm_out_norm', 'delta_pool_out_norm', 'delta_w_out', 'delta_ffn2_pre_norm', 'delta_ffn2_post_norm', 'delta_ffn2_w_gate', 'delta_ffn2_w_up', 'delta_ffn2_w_down', 'new_m_meta_tokens', 'new_m_ffn1_pre_norm', 'new_m_ffn1_post_norm', 'new_m_ffn1_w_gate', 'new_m_ffn1_w_up', 'new_m_ffn1_w_down', 'new_m_mix_pre_norm', 'new_m_mix_post_norm', 'new_m_w_in', 'new_m_ssm_lambda_re', 'new_m_ssm_lambda_im', 'new_m_ssm_log_dt', 'new_m_ssm_b_re', 'new_m_ssm_b_im', 'new_m_ssm_c_re', 'new_m_ssm_c_im', 'new_m_ssm_d', 'new_m_ssm_w_glu', 'new_m_pool_w', 'new_m_pool_scale', 'new_m_ssm_out_norm', 'new_m_pool_out_norm', 'new_m_w_out', 'new_m_ffn2_pre_norm', 'new_m_ffn2_post_norm', 'new_m_ffn2_w_gate', 'new_m_ffn2_w_up', 'new_m_ffn2_w_down', 'new_v_meta_tokens', 'new_v_ffn1_pre_norm', 'new_v_ffn1_post_norm', 'new_v_ffn1_w_gate', 'new_v_ffn1_w_up', 'new_v_ffn1_w_down', 'new_v_mix_pre_norm', 'new_v_mix_post_norm', 'new_v_w_in', 'new_v_ssm_lambda_re', 'new_v_ssm_lambda_im', 'new_v_ssm_log_dt', 'new_v_ssm_b_re', 'new_v_ssm_b_im', 'new_v_ssm_c_re', 'new_v_ssm_c_im', 'new_v_ssm_d', 'new_v_ssm_w_glu', 'new_v_pool_w', 'new_v_pool_scale', 'new_v_ssm_out_norm', 'new_v_pool_out_norm', 'new_v_w_out', 'new_v_ffn2_pre_norm', 'new_v_ffn2_post_norm', 'new_v_ffn2_w_gate', 'new_v_ffn2_w_up', 'new_v_ffn2_w_down']
TWIN_LEAF_KINDS = {'loss': 'loss', 'grad_x': 'grad_x', 'grad_meta_tokens': 'grad_w', 'grad_ffn1_pre_norm': 'grad_w', 'grad_ffn1_post_norm': 'grad_w', 'grad_ffn1_w_gate': 'grad_w', 'grad_ffn1_w_up': 'grad_w', 'grad_ffn1_w_down': 'grad_w', 'grad_mix_pre_norm': 'grad_w', 'grad_mix_post_norm': 'grad_w', 'grad_w_in': 'grad_w', 'grad_ssm_lambda_re': 'grad_w', 'grad_ssm_lambda_im': 'grad_w', 'grad_ssm_log_dt': 'grad_w', 'grad_ssm_b_re': 'grad_w', 'grad_ssm_b_im': 'grad_w', 'grad_ssm_c_re': 'grad_w', 'grad_ssm_c_im': 'grad_w', 'grad_ssm_d': 'grad_w', 'grad_ssm_w_glu': 'grad_w', 'grad_pool_w': 'grad_w', 'grad_pool_scale': 'grad_w', 'grad_ssm_out_norm': 'grad_w', 'grad_pool_out_norm': 'grad_w', 'grad_w_out': 'grad_w', 'grad_ffn2_pre_norm': 'grad_w', 'grad_ffn2_post_norm': 'grad_w', 'grad_ffn2_w_gate': 'grad_w', 'grad_ffn2_w_up': 'grad_w', 'grad_ffn2_w_down': 'grad_w', 'delta_meta_tokens': 'delta_w', 'delta_ffn1_pre_norm': 'delta_w', 'delta_ffn1_post_norm': 'delta_w', 'delta_ffn1_w_gate': 'delta_w', 'delta_ffn1_w_up': 'delta_w', 'delta_ffn1_w_down': 'delta_w', 'delta_mix_pre_norm': 'delta_w', 'delta_mix_post_norm': 'delta_w', 'delta_w_in': 'delta_w', 'delta_ssm_lambda_re': 'delta_w', 'delta_ssm_lambda_im': 'delta_w', 'delta_ssm_log_dt': 'delta_w', 'delta_ssm_b_re': 'delta_w', 'delta_ssm_b_im': 'delta_w', 'delta_ssm_c_re': 'delta_w', 'delta_ssm_c_im': 'delta_w', 'delta_ssm_d': 'delta_w', 'delta_ssm_w_glu': 'delta_w', 'delta_pool_w': 'delta_w', 'delta_pool_scale': 'delta_w', 'delta_ssm_out_norm': 'delta_w', 'delta_pool_out_norm': 'delta_w', 'delta_w_out': 'delta_w', 'delta_ffn2_pre_norm': 'delta_w', 'delta_ffn2_post_norm': 'delta_w', 'delta_ffn2_w_gate': 'delta_w', 'delta_ffn2_w_up': 'delta_w', 'delta_ffn2_w_down': 'delta_w', 'new_m_meta_tokens': 'new_m', 'new_m_ffn1_pre_norm': 'new_m', 'new_m_ffn1_post_norm': 'new_m', 'new_m_ffn1_w_gate': 'new_m', 'new_m_ffn1_w_up': 'new_m', 'new_m_ffn1_w_down': 'new_m', 'new_m_mix_pre_norm': 'new_m', 'new_m_mix_post_norm': 'new_m', 'new_m_w_in': 'new_m', 'new_m_ssm_lambda_re': 'new_m', 'new_m_ssm_lambda_im': 'new_m', 'new_m_ssm_log_dt': 'new_m', 'new_m_ssm_b_re': 'new_m', 'new_m_ssm_b_im': 'new_m', 'new_m_ssm_c_re': 'new_m', 'new_m_ssm_c_im': 'new_m', 'new_m_ssm_d': 'new_m', 'new_m_ssm_w_glu': 'new_m', 'new_m_pool_w': 'new_m', 'new_m_pool_scale': 'new_m', 'new_m_ssm_out_norm': 'new_m', 'new_m_pool_out_norm': 'new_m', 'new_m_w_out': 'new_m', 'new_m_ffn2_pre_norm': 'new_m', 'new_m_ffn2_post_norm': 'new_m', 'new_m_ffn2_w_gate': 'new_m', 'new_m_ffn2_w_up': 'new_m', 'new_m_ffn2_w_down': 'new_m', 'new_v_meta_tokens': 'new_v', 'new_v_ffn1_pre_norm': 'new_v', 'new_v_ffn1_post_norm': 'new_v', 'new_v_ffn1_w_gate': 'new_v', 'new_v_ffn1_w_up': 'new_v', 'new_v_ffn1_w_down': 'new_v', 'new_v_mix_pre_norm': 'new_v', 'new_v_mix_post_norm': 'new_v', 'new_v_w_in': 'new_v', 'new_v_ssm_lambda_re': 'new_v', 'new_v_ssm_lambda_im': 'new_v', 'new_v_ssm_log_dt': 'new_v', 'new_v_ssm_b_re': 'new_v', 'new_v_ssm_b_im': 'new_v', 'new_v_ssm_c_re': 'new_v', 'new_v_ssm_c_im': 'new_v', 'new_v_ssm_d': 'new_v', 'new_v_ssm_w_glu': 'new_v', 'new_v_pool_w': 'new_v', 'new_v_pool_scale': 'new_v', 'new_v_ssm_out_norm': 'new_v', 'new_v_pool_out_norm': 'new_v', 'new_v_w_out': 'new_v', 'new_v_ffn2_pre_norm': 'new_v', 'new_v_ffn2_post_norm': 'new_v', 'new_v_ffn2_w_gate': 'new_v', 'new_v_ffn2_w_up': 'new_v', 'new_v_ffn2_w_down': 'new_v'}


def _forward(args):
    return _fwd_reference(*[args[k] for k in FWD_PARAMS])


def _output_shape():
    out = _jax.eval_shape(lambda: _forward(_fwd_setup_inputs(0)))
    return out.shape, out.dtype

N_MICROBATCH = 1
ADAM_LR = 0.001
ADAM_B1 = 0.9
ADAM_B2 = 0.999
ADAM_EPS = 1e-08
ADAM_WD = 0.01
ADAM_STEP = 10
PER_EXAMPLE_BATCH_AXIS = {'x': 0, 'loss_target': 0}
SHARED_INPUTS = []
_WEIGHT_DTYPES = {'meta_tokens': _jnp.float32, 'ffn1_pre_norm': _jnp.float32, 'ffn1_post_norm': _jnp.float32, 'ffn1_w_gate': _jnp.float32, 'ffn1_w_up': _jnp.float32, 'ffn1_w_down': _jnp.float32, 'mix_pre_norm': _jnp.float32, 'mix_post_norm': _jnp.float32, 'w_in': _jnp.float32, 'ssm_lambda_re': _jnp.float32, 'ssm_lambda_im': _jnp.float32, 'ssm_log_dt': _jnp.float32, 'ssm_b_re': _jnp.float32, 'ssm_b_im': _jnp.float32, 'ssm_c_re': _jnp.float32, 'ssm_c_im': _jnp.float32, 'ssm_d': _jnp.float32, 'ssm_w_glu': _jnp.float32, 'pool_w': _jnp.float32, 'pool_scale': _jnp.float32, 'ssm_out_norm': _jnp.float32, 'pool_out_norm': _jnp.float32, 'w_out': _jnp.float32, 'ffn2_pre_norm': _jnp.float32, 'ffn2_post_norm': _jnp.float32, 'ffn2_w_gate': _jnp.float32, 'ffn2_w_up': _jnp.float32, 'ffn2_w_down': _jnp.float32}
MOMENT_SCALE = {'meta_tokens': 4.014640e-03, 'ffn1_pre_norm': 5.481770e-01, 'ffn1_post_norm': 1.584899e+01, 'ffn1_w_gate': 2.152898e-01, 'ffn1_w_up': 2.475822e-01, 'ffn1_w_down': 4.392054e-01, 'mix_pre_norm': 1.306658e+00, 'mix_post_norm': 6.469365e+01, 'w_in': 1.113297e+00, 'ssm_lambda_re': 3.024574e-02, 'ssm_lambda_im': 3.416739e-02, 'ssm_log_dt': 3.250022e+01, 'ssm_b_re': 1.825531e-02, 'ssm_b_im': 1.792078e-02, 'ssm_c_re': 3.529293e-02, 'ssm_c_im': 3.621341e-02, 'ssm_d': 1.837725e+00, 'ssm_w_glu': 1.430559e+00, 'pool_w': 1.279605e+00, 'pool_scale': 1.327140e+00, 'ssm_out_norm': 2.188418e+00, 'pool_out_norm': 1.324723e+00, 'w_out': 1.672332e+00, 'ffn2_pre_norm': 1.167804e+00, 'ffn2_post_norm': 1.590742e+01, 'ffn2_w_gate': 3.032857e-01, 'ffn2_w_up': 5.395423e-01, 'ffn2_w_down': 9.090840e-01}


def _to_microbatches(a, axis):
    t = _jnp.moveaxis(a, axis, 0)
    t = t.reshape((N_MICROBATCH, t.shape[0] // N_MICROBATCH) + t.shape[1:])
    return _jnp.moveaxis(t, 1, axis + 1)


def setup_inputs(seed: int = 0) -> dict:
    inp = _fwd_setup_inputs(seed)
    key = _jax.random.fold_in(_jax.random.key(seed), 7919)
    shape, _ = _output_shape()
    out = dict(inp)
    out["loss_target"] = _jax.random.normal(_jax.random.fold_in(key, 0), shape, _jnp.float32)
    for i, name in enumerate(TWIN_WEIGHTS):
        w = inp[name].astype(_jnp.float32)
        if MOMENT_SCALE is None:
            s = _jnp.sqrt(_jnp.mean(_jnp.square(w)) + 1e-30)
        else:
            s = MOMENT_SCALE[name]
        km, kv = _jax.random.split(_jax.random.fold_in(key, i + 1))
        out[name] = w
        out["m_" + name] = s * _jax.random.normal(km, w.shape, _jnp.float32)
        out["v_" + name] = (s * s) * _jax.random.uniform(kv, w.shape, _jnp.float32, 0.5, 1.5)
    if N_MICROBATCH > 1:
        for name, axis in PER_EXAMPLE_BATCH_AXIS.items():
            out[name] = _to_microbatches(out[name], axis)
    return {'x': out['x'], 'meta_tokens': out['meta_tokens'], 'ffn1_pre_norm': out['ffn1_pre_norm'], 'ffn1_post_norm': out['ffn1_post_norm'], 'ffn1_w_gate': out['ffn1_w_gate'], 'ffn1_w_up': out['ffn1_w_up'], 'ffn1_w_down': out['ffn1_w_down'], 'mix_pre_norm': out['mix_pre_norm'], 'mix_post_norm': out['mix_post_norm'], 'w_in': out['w_in'], 'ssm_lambda_re': out['ssm_lambda_re'], 'ssm_lambda_im': out['ssm_lambda_im'], 'ssm_log_dt': out['ssm_log_dt'], 'ssm_b_re': out['ssm_b_re'], 'ssm_b_im': out['ssm_b_im'], 'ssm_c_re': out['ssm_c_re'], 'ssm_c_im': out['ssm_c_im'], 'ssm_d': out['ssm_d'], 'ssm_w_glu': out['ssm_w_glu'], 'pool_w': out['pool_w'], 'pool_scale': out['pool_scale'], 'ssm_out_norm': out['ssm_out_norm'], 'pool_out_norm': out['pool_out_norm'], 'w_out': out['w_out'], 'ffn2_pre_norm': out['ffn2_pre_norm'], 'ffn2_post_norm': out['ffn2_post_norm'], 'ffn2_w_gate': out['ffn2_w_gate'], 'ffn2_w_up': out['ffn2_w_up'], 'ffn2_w_down': out['ffn2_w_down'], 'loss_target': out['loss_target'], 'm_meta_tokens': out['m_meta_tokens'], 'm_ffn1_pre_norm': out['m_ffn1_pre_norm'], 'm_ffn1_post_norm': out['m_ffn1_post_norm'], 'm_ffn1_w_gate': out['m_ffn1_w_gate'], 'm_ffn1_w_up': out['m_ffn1_w_up'], 'm_ffn1_w_down': out['m_ffn1_w_down'], 'm_mix_pre_norm': out['m_mix_pre_norm'], 'm_mix_post_norm': out['m_mix_post_norm'], 'm_w_in': out['m_w_in'], 'm_ssm_lambda_re': out['m_ssm_lambda_re'], 'm_ssm_lambda_im': out['m_ssm_lambda_im'], 'm_ssm_log_dt': out['m_ssm_log_dt'], 'm_ssm_b_re': out['m_ssm_b_re'], 'm_ssm_b_im': out['m_ssm_b_im'], 'm_ssm_c_re': out['m_ssm_c_re'], 'm_ssm_c_im': out['m_ssm_c_im'], 'm_ssm_d': out['m_ssm_d'], 'm_ssm_w_glu': out['m_ssm_w_glu'], 'm_pool_w': out['m_pool_w'], 'm_pool_scale': out['m_pool_scale'], 'm_ssm_out_norm': out['m_ssm_out_norm'], 'm_pool_out_norm': out['m_pool_out_norm'], 'm_w_out': out['m_w_out'], 'm_ffn2_pre_norm': out['m_ffn2_pre_norm'], 'm_ffn2_post_norm': out['m_ffn2_post_norm'], 'm_ffn2_w_gate': out['m_ffn2_w_gate'], 'm_ffn2_w_up': out['m_ffn2_w_up'], 'm_ffn2_w_down': out['m_ffn2_w_down'], 'v_meta_tokens': out['v_meta_tokens'], 'v_ffn1_pre_norm': out['v_ffn1_pre_norm'], 'v_ffn1_post_norm': out['v_ffn1_post_norm'], 'v_ffn1_w_gate': out['v_ffn1_w_gate'], 'v_ffn1_w_up': out['v_ffn1_w_up'], 'v_ffn1_w_down': out['v_ffn1_w_down'], 'v_mix_pre_norm': out['v_mix_pre_norm'], 'v_mix_post_norm': out['v_mix_post_norm'], 'v_w_in': out['v_w_in'], 'v_ssm_lambda_re': out['v_ssm_lambda_re'], 'v_ssm_lambda_im': out['v_ssm_lambda_im'], 'v_ssm_log_dt': out['v_ssm_log_dt'], 'v_ssm_b_re': out['v_ssm_b_re'], 'v_ssm_b_im': out['v_ssm_b_im'], 'v_ssm_c_re': out['v_ssm_c_re'], 'v_ssm_c_im': out['v_ssm_c_im'], 'v_ssm_d': out['v_ssm_d'], 'v_ssm_w_glu': out['v_ssm_w_glu'], 'v_pool_w': out['v_pool_w'], 'v_pool_scale': out['v_pool_scale'], 'v_ssm_out_norm': out['v_ssm_out_norm'], 'v_pool_out_norm': out['v_pool_out_norm'], 'v_w_out': out['v_w_out'], 'v_ffn2_pre_norm': out['v_ffn2_pre_norm'], 'v_ffn2_post_norm': out['v_ffn2_post_norm'], 'v_ffn2_w_gate': out['v_ffn2_w_gate'], 'v_ffn2_w_up': out['v_ffn2_w_up'], 'v_ffn2_w_down': out['v_ffn2_w_down']}


def _loss(weights, diff, rest, loss_target):
    with _jax.named_scope("forward"):
        args = {**rest, TWIN_DIFF_INPUT: diff, **{k: w.astype(_WEIGHT_DTYPES[k]) for k, w in weights.items()}}
        y = _forward(args)
    with _jax.named_scope("loss_head"):
        err = _jnp.square(y.astype(_jnp.float32) - loss_target)
        return 0.5 * _jnp.sum(_jnp.mean(err, axis=-1)) if err.ndim else 0.5 * err


def _adamw(w, g, m, v):
    m = ADAM_B1 * m + (1.0 - ADAM_B1) * g
    v = ADAM_B2 * v + (1.0 - ADAM_B2) * _jnp.square(g)
    m_hat = m / (1.0 - ADAM_B1 ** ADAM_STEP)
    v_hat = v / (1.0 - ADAM_B2 ** ADAM_STEP)
    delta = -ADAM_LR * (m_hat / (_jnp.sqrt(v_hat) + ADAM_EPS) + ADAM_WD * w)
    return delta, m, v


def reference(x, meta_tokens, ffn1_pre_norm, ffn1_post_norm, ffn1_w_gate, ffn1_w_up, ffn1_w_down, mix_pre_norm, mix_post_norm, w_in, ssm_lambda_re, ssm_lambda_im, ssm_log_dt, ssm_b_re, ssm_b_im, ssm_c_re, ssm_c_im, ssm_d, ssm_w_glu, pool_w, pool_scale, ssm_out_norm, pool_out_norm, w_out, ffn2_pre_norm, ffn2_post_norm, ffn2_w_gate, ffn2_w_up, ffn2_w_down, loss_target, m_meta_tokens, m_ffn1_pre_norm, m_ffn1_post_norm, m_ffn1_w_gate, m_ffn1_w_up, m_ffn1_w_down, m_mix_pre_norm, m_mix_post_norm, m_w_in, m_ssm_lambda_re, m_ssm_lambda_im, m_ssm_log_dt, m_ssm_b_re, m_ssm_b_im, m_ssm_c_re, m_ssm_c_im, m_ssm_d, m_ssm_w_glu, m_pool_w, m_pool_scale, m_ssm_out_norm, m_pool_out_norm, m_w_out, m_ffn2_pre_norm, m_ffn2_post_norm, m_ffn2_w_gate, m_ffn2_w_up, m_ffn2_w_down, v_meta_tokens, v_ffn1_pre_norm, v_ffn1_post_norm, v_ffn1_w_gate, v_ffn1_w_up, v_ffn1_w_down, v_mix_pre_norm, v_mix_post_norm, v_w_in, v_ssm_lambda_re, v_ssm_lambda_im, v_ssm_log_dt, v_ssm_b_re, v_ssm_b_im, v_ssm_c_re, v_ssm_c_im, v_ssm_d, v_ssm_w_glu, v_pool_w, v_pool_scale, v_ssm_out_norm, v_pool_out_norm, v_w_out, v_ffn2_pre_norm, v_ffn2_post_norm, v_ffn2_w_gate, v_ffn2_w_up, v_ffn2_w_down):
    given = dict(x=x, meta_tokens=meta_tokens, ffn1_pre_norm=ffn1_pre_norm, ffn1_post_norm=ffn1_post_norm, ffn1_w_gate=ffn1_w_gate, ffn1_w_up=ffn1_w_up, ffn1_w_down=ffn1_w_down, mix_pre_norm=mix_pre_norm, mix_post_norm=mix_post_norm, w_in=w_in, ssm_lambda_re=ssm_lambda_re, ssm_lambda_im=ssm_lambda_im, ssm_log_dt=ssm_log_dt, ssm_b_re=ssm_b_re, ssm_b_im=ssm_b_im, ssm_c_re=ssm_c_re, ssm_c_im=ssm_c_im, ssm_d=ssm_d, ssm_w_glu=ssm_w_glu, pool_w=pool_w, pool_scale=pool_scale, ssm_out_norm=ssm_out_norm, pool_out_norm=pool_out_norm, w_out=w_out, ffn2_pre_norm=ffn2_pre_norm, ffn2_post_norm=ffn2_post_norm, ffn2_w_gate=ffn2_w_gate, ffn2_w_up=ffn2_w_up, ffn2_w_down=ffn2_w_down, loss_target=loss_target, m_meta_tokens=m_meta_tokens, m_ffn1_pre_norm=m_ffn1_pre_norm, m_ffn1_post_norm=m_ffn1_post_norm, m_ffn1_w_gate=m_ffn1_w_gate, m_ffn1_w_up=m_ffn1_w_up, m_ffn1_w_down=m_ffn1_w_down, m_mix_pre_norm=m_mix_pre_norm, m_mix_post_norm=m_mix_post_norm, m_w_in=m_w_in, m_ssm_lambda_re=m_ssm_lambda_re, m_ssm_lambda_im=m_ssm_lambda_im, m_ssm_log_dt=m_ssm_log_dt, m_ssm_b_re=m_ssm_b_re, m_ssm_b_im=m_ssm_b_im, m_ssm_c_re=m_ssm_c_re, m_ssm_c_im=m_ssm_c_im, m_ssm_d=m_ssm_d, m_ssm_w_glu=m_ssm_w_glu, m_pool_w=m_pool_w, m_pool_scale=m_pool_scale, m_ssm_out_norm=m_ssm_out_norm, m_pool_out_norm=m_pool_out_norm, m_w_out=m_w_out, m_ffn2_pre_norm=m_ffn2_pre_norm, m_ffn2_post_norm=m_ffn2_post_norm, m_ffn2_w_gate=m_ffn2_w_gate, m_ffn2_w_up=m_ffn2_w_up, m_ffn2_w_down=m_ffn2_w_down, v_meta_tokens=v_meta_tokens, v_ffn1_pre_norm=v_ffn1_pre_norm, v_ffn1_post_norm=v_ffn1_post_norm, v_ffn1_w_gate=v_ffn1_w_gate, v_ffn1_w_up=v_ffn1_w_up, v_ffn1_w_down=v_ffn1_w_down, v_mix_pre_norm=v_mix_pre_norm, v_mix_post_norm=v_mix_post_norm, v_w_in=v_w_in, v_ssm_lambda_re=v_ssm_lambda_re, v_ssm_lambda_im=v_ssm_lambda_im, v_ssm_log_dt=v_ssm_log_dt, v_ssm_b_re=v_ssm_b_re, v_ssm_b_im=v_ssm_b_im, v_ssm_c_re=v_ssm_c_re, v_ssm_c_im=v_ssm_c_im, v_ssm_d=v_ssm_d, v_ssm_w_glu=v_ssm_w_glu, v_pool_w=v_pool_w, v_pool_scale=v_pool_scale, v_ssm_out_norm=v_ssm_out_norm, v_pool_out_norm=v_pool_out_norm, v_w_out=v_w_out, v_ffn2_pre_norm=v_ffn2_pre_norm, v_ffn2_post_norm=v_ffn2_post_norm, v_ffn2_w_gate=v_ffn2_w_gate, v_ffn2_w_up=v_ffn2_w_up, v_ffn2_w_down=v_ffn2_w_down)
    weights = {n: given[n] for n in TWIN_WEIGHTS}
    shared = {n: given[n] for n in SHARED_INPUTS}
    per_example = {n: given[n] for n in ['x']}
    grad_fn = _jax.value_and_grad(_loss, argnums=(0, 1))

    def one_microbatch(ex, loss_target):
        ex = dict(ex)
        diff = ex.pop(TWIN_DIFF_INPUT)
        return grad_fn(weights, diff, {**shared, **ex}, loss_target)

    if N_MICROBATCH == 1:
        loss, (grad_w, grad_x) = one_microbatch(per_example, given["loss_target"])
    else:
        def body(carry, xs):
            loss_sum, grad_sum = carry
            l_k, (gw_k, gx_k) = one_microbatch(xs[0], xs[1])
            with _jax.named_scope("update"):
                return (loss_sum + l_k, _jax.tree.map(_jnp.add, grad_sum, gw_k)), gx_k

        init = (_jnp.zeros((), _jnp.float32), _jax.tree.map(_jnp.zeros_like, weights))
        (loss, grad_w), grad_x = _jax.lax.scan(body, init, (per_example, given["loss_target"]))
    with _jax.named_scope("update"):
        delta_w, new_m, new_v = {}, {}, {}
        for n in TWIN_WEIGHTS:
            delta_w[n], new_m[n], new_v[n] = _adamw(weights[n], grad_w[n], given["m_" + n], given["v_" + n])
    return (loss, grad_x, *[grad_w[n] for n in TWIN_WEIGHTS], *[delta_w[n] for n in TWIN_WEIGHTS],
            *[new_m[n] for n in TWIN_WEIGHTS], *[new_v[n] for n in TWIN_WEIGHTS])
```

```python
import functools
import math

import jax
import jax.numpy as jnp
import numpy as np
from jax import lax
from jax.experimental import pallas as pl
from jax.experimental.pallas import tpu as pltpu

F32 = jnp.float32
_MXU = jnp.bfloat16
_ACT = jnp.bfloat16
_WIRE = jnp.bfloat16

N_DEV = 8
N_META = 16
RMS_EPS = 1e-6
SSM_GROUP_CH = 16
SSM_STATE = 64
LANES = 128
SUBLANES = 8
POOL_WINDOWS = (2, 4, 8, 16)
POOL_HALO = 16
ADAM_LR = 0.001
ADAM_B1 = 0.9
ADAM_B2 = 0.999
ADAM_EPS = 1e-08
ADAM_WD = 0.01
ADAM_STEP = 10
GELU_C0 = math.sqrt(2.0 / math.pi)
GELU_C1 = 0.044715
VMEM_LIMIT = 56 * 1024 * 1024

_NT = (((1,), (1,)), ((), ()))
_TN = (((0,), (0,)), ((), ()))


def _dot(a, b):
    return jnp.dot(a, b, preferred_element_type=F32)


def _dot_nt(a, b):
    return lax.dot_general(a, b, _NT, preferred_element_type=F32)


def _dot_tn(a, b):
    return lax.dot_general(a, b, _TN, preferred_element_type=F32)


def _rs(x):
    return lax.rsqrt(jnp.mean(x * x, axis=-1, keepdims=True) + RMS_EPS)


def _sigmoid(x):
    return 1.0 / (1.0 + jnp.exp(-x))


def _row_tile(n_rows, largest=432):
    for t in (432, 304, 48, 16):
        if t <= largest and n_rows % t == 0:
            return t
    raise ValueError(n_rows)


def _ff_chunk(d_ff):
    return d_ff // 2 if (d_ff // 2) % LANES == 0 else d_ff


def _params(n_axes=1):
    return pltpu.CompilerParams(dimension_semantics=("arbitrary",) * n_axes, vmem_limit_bytes=VMEM_LIMIT)


def _resident():
    return pl.BlockSpec(memory_space=pltpu.VMEM)


def _full(shape):
    nd = len(shape)
    return pl.BlockSpec(shape, lambda *_: (0,) * nd)


def _exchange(srcs, kinds, name):
    n = len(srcs)
    out_shape = []
    for s, kind in zip(srcs, kinds):
        shape = (N_DEV,) + s.shape if kind == "gather" else s.shape
        out_shape.append(jax.ShapeDtypeStruct(shape, s.dtype))

    def body(*refs):
        src, dst = refs[:n], refs[n:2 * n]
        send_sems, recv_sems, local_sems = refs[2 * n:]
        x, y, c = lax.axis_index("x"), lax.axis_index("y"), lax.axis_index("c")
        me = 4 * x + 2 * y + c
        copies = []
        for a in range(n):
            mine = src[a] if kinds[a] == "gather" else src[a].at[me]
            cp = pltpu.make_async_copy(mine, dst[a].at[me], local_sems.at[a])
            cp.start()
            copies.append(cp)
        for r in range(1, N_DEV):
            px = 1 - x if r & 4 else x
            py = 1 - y if r & 2 else y
            pc = 1 - c if r & 1 else c
            peer = 4 * px + 2 * py + pc
            for a in range(n):
                part = src[a] if kinds[a] == "gather" else src[a].at[peer]
                cp = pltpu.make_async_remote_copy(
                    src_ref=part, dst_ref=dst[a].at[me],
                    send_sem=send_sems.at[a * (N_DEV - 1) + r - 1], recv_sem=recv_sems.at[a * (N_DEV - 1) + r - 1],
                    device_id=(px, py, pc), device_id_type=pl.DeviceIdType.MESH)
                cp.start()
                copies.append(cp)
        for cp in copies:
            cp.wait()

    any_spec = pl.BlockSpec(memory_space=pl.ANY)
    return pl.pallas_call(
        body, name=name, out_shape=out_shape,
        in_specs=[any_spec] * n, out_specs=[any_spec] * n,
        scratch_shapes=[pltpu.SemaphoreType.DMA((n * (N_DEV - 1),)), pltpu.SemaphoreType.DMA((n * (N_DEV - 1),)),
                        pltpu.SemaphoreType.DMA((n,))],
    )(*srcs)


def _sum_slots(r, name):
    _, rows, cols = r.shape
    blk = rows
    for cand in (592, 512, 256, 128, 64, 32, 16):
        if rows % cand == 0:
            blk = cand
            break

    def body(r_ref, o_ref):
        acc = r_ref[0].astype(F32)
        for d in range(1, N_DEV):
            acc = acc + r_ref[d].astype(F32)
        o_ref[...] = acc

    return pl.pallas_call(
        body, name=name, grid=(rows // blk,), out_shape=jax.ShapeDtypeStruct((rows, cols), F32),
        in_specs=[pl.BlockSpec((N_DEV, blk, cols), lambda i: (0, i, 0))],
        out_specs=pl.BlockSpec((blk, cols), lambda i: (i, 0)), compiler_params=_params(),
    )(r)


def _ffn_fwd(h, g_pre, g_post, wgt, wut, wd, name):
    n_rows, d = h.shape
    d_ff = wd.shape[0]
    tm, fc = _row_tile(n_rows), _ff_chunk(d_ff)
    n_t, n_c = n_rows // tm, d_ff // fc

    def body(h_ref, gpre_ref, gpost_ref, wgt_ref, wut_ref, wd_ref, a_ref, b_ref, f_ref, ho_ref, n_scr, acc):
        c = pl.program_id(1)

        @pl.when(c == 0)
        def _():
            hv = h_ref[...]
            n_scr[...] = (hv * _rs(hv) * gpre_ref[...]).astype(_MXU)
            acc[...] = jnp.zeros_like(acc)

        rows = pl.ds(pl.multiple_of(c * fc, fc), fc)
        nv = n_scr[...]
        a = _dot_nt(nv, wgt_ref[rows, :])
        b = _dot_nt(nv, wut_ref[rows, :])
        a_ref[...] = a.astype(_ACT)
        b_ref[...] = b.astype(_ACT)
        s = a * _sigmoid(a) * b
        acc[...] += _dot(s.astype(_MXU), wd_ref[rows, :])

        @pl.when(c == n_c - 1)
        def _():
            f = acc[...]
            f_ref[...] = f
            ho_ref[...] = h_ref[...] + 0.5 * (f * _rs(f) * gpost_ref[...])

    row = pl.BlockSpec((tm, d), lambda i, c: (i, 0))
    chunk = pl.BlockSpec((tm, fc), lambda i, c: (i, c))
    return pl.pallas_call(
        body, name=name, grid=(n_t, n_c),
        out_shape=[jax.ShapeDtypeStruct((n_rows, d_ff), _ACT), jax.ShapeDtypeStruct((n_rows, d_ff), _ACT),
                   jax.ShapeDtypeStruct((n_rows, d), F32), jax.ShapeDtypeStruct((n_rows, d), F32)],
        in_specs=[row, _full((1, d)), _full((1, d)), _resident(), _resident(), _resident()],
        out_specs=[chunk, chunk, row, row],
        scratch_shapes=[pltpu.VMEM((tm, d), _MXU), pltpu.VMEM((tm, d), F32)],
        compiler_params=_params(2),
    )(h, g_pre, g_post, wgt, wut, wd)


def _ffn_bwd(dho, h, f, a, b, g_pre, g_post, wgt, wut, wd, name):
    n_rows, d = h.shape
    d_ff = wd.shape[0]
    tm, fc = _row_tile(n_rows, 304), _ff_chunk(d_ff)
    n_t, n_c = n_rows // tm, d_ff // fc

    def body(dho_ref, h_ref, f_ref, a_ref, b_ref, gpre_ref, gpost_ref, wgt_ref, wut_ref, wd_ref,
             dh_ref, da_ref, db_ref, s_ref, df_ref, n_ref, dgpre_ref, dgpost_ref, dn_acc):
        i, c = pl.program_id(0), pl.program_id(1)

        @pl.when((i == 0) & (c == 0))
        def _():
            dgpre_ref[...] = jnp.zeros_like(dgpre_ref)
            dgpost_ref[...] = jnp.zeros_like(dgpost_ref)

        @pl.when(c == 0)
        def _():
            fv = f_ref[...]
            rf = _rs(fv)
            fhat = fv * rf
            dy = 0.5 * dho_ref[...]
            dgpost_ref[...] += jnp.sum(dy * fhat, axis=0, keepdims=True)
            dfhat = dy * gpost_ref[...]
            df = rf * (dfhat - fhat * jnp.mean(dfhat * fhat, axis=-1, keepdims=True))
            df_ref[...] = df.astype(_MXU)
            hv = h_ref[...]
            n_ref[...] = (hv * _rs(hv) * gpre_ref[...]).astype(_MXU)
            dn_acc[...] = jnp.zeros_like(dn_acc)

        rows = pl.ds(pl.multiple_of(c * fc, fc), fc)
        ds = _dot_nt(df_ref[...], wd_ref[rows, :])
        av = a_ref[...].astype(F32)
        bv = b_ref[...].astype(F32)
        sg = _sigmoid(av)
        si = av * sg
        da = (ds * bv * (sg * (1.0 + av * (1.0 - sg)))).astype(_MXU)
        db = (ds * si).astype(_MXU)
        da_ref[...] = da
        db_ref[...] = db
        s_ref[...] = (si * bv).astype(_MXU)
        dn_acc[...] += _dot(da, wgt_ref[rows, :]) + _dot(db, wut_ref[rows, :])

        @pl.when(c == n_c - 1)
        def _():
            dn = dn_acc[...]
            hv = h_ref[...]
            r = _rs(hv)
            hhat = hv * r
            dgpre_ref[...] += jnp.sum(dn * hhat, axis=0, keepdims=True)
            dhh = dn * gpre_ref[...]
            dh_ref[...] = dho_ref[...] + r * (dhh - hhat * jnp.mean(dhh * hhat, axis=-1, keepdims=True))

    row = pl.BlockSpec((tm, d), lambda i, c: (i, 0))
    chunk = pl.BlockSpec((tm, fc), lambda i, c: (i, c))
    vec = pl.BlockSpec((1, d), lambda i, c: (0, 0))
    return pl.pallas_call(
        body, name=name, grid=(n_t, n_c),
        out_shape=[jax.ShapeDtypeStruct((n_rows, d), F32)] + [jax.ShapeDtypeStruct((n_rows, d_ff), _MXU)] * 3
        + [jax.ShapeDtypeStruct((n_rows, d), _MXU)] * 2 + [jax.ShapeDtypeStruct((1, d), F32)] * 2,
        in_specs=[row, row, row, chunk, chunk, vec, vec, _resident(), _resident(), _resident()],
        out_specs=[row, chunk, chunk, chunk, row, row, vec, vec],
        scratch_shapes=[pltpu.VMEM((tm, d), F32)],
        compiler_params=_params(2),
    )(dho, h, f, a, b, g_pre, g_post, wgt, wut, wd)


def _wgrad(xm, ym, name):
    n_rows, a_dim = xm.shape
    b_dim = ym.shape[1]
    tk = n_rows
    for cand in (2736, 1296, 432, 48, 16):
        if n_rows % cand == 0:
            tk = cand
            break
    ta = a_dim
    for cand in (1408, 1024, 512):
        if a_dim % cand == 0:
            ta = cand
            break

    def body(x_ref, y_ref, o_ref):
        @pl.when(pl.program_id(1) == 0)
        def _():
            o_ref[...] = jnp.zeros_like(o_ref)

        o_ref[...] += _dot_tn(x_ref[...], y_ref[...])

    return pl.pallas_call(
        body, name=name, grid=(a_dim // ta, n_rows // tk), out_shape=jax.ShapeDtypeStruct((a_dim, b_dim), F32),
        in_specs=[pl.BlockSpec((tk, ta), lambda j, k: (k, j)), pl.BlockSpec((tk, b_dim), lambda j, k: (k, 0))],
        out_specs=pl.BlockSpec((ta, b_dim), lambda j, k: (j, 0)), compiler_params=_params(2),
    )(xm, ym)


def _mix_in_fwd(h, g, w_in, name):
    n_rows, d = h.shape
    tm = _row_tile(n_rows)

    def body(h_ref, g_ref, w_ref, n_ref, p_ref):
        hv = h_ref[...]
        nv = (hv * _rs(hv) * g_ref[...]).astype(_MXU)
        n_ref[...] = nv
        p_ref[...] = _dot(nv, w_ref[...])

    row = pl.BlockSpec((tm, d), lambda i: (i, 0))
    return pl.pallas_call(
        body, name=name, grid=(n_rows // tm,),
        out_shape=[jax.ShapeDtypeStruct((n_rows, d), _MXU), jax.ShapeDtypeStruct((n_rows, w_in.shape[1]), F32)],
        in_specs=[row, _full((1, d)), _resident()],
        out_specs=[row, pl.BlockSpec((tm, w_in.shape[1]), lambda i: (i, 0))], compiler_params=_params(),
    )(h, g, w_in)


def _gelu_parts(y):
    th = jnp.tanh(GELU_C0 * (y + GELU_C1 * (y * y * y)))
    return 0.5 * (1.0 + th), th


def _scan_blocks(n_blocks, width, coef_ref, load, store, reverse):
    n_cols = coef_ref.shape[2] // width
    for cb in range(n_cols):
        cols = pl.ds(cb * width, width)

        def step(t, carry, cols=cols):
            tb = (n_blocks - 1 - t) if reverse else t
            r0 = pl.multiple_of(tb * SUBLANES, SUBLANES)
            vr, vi = load(r0, cols)
            xr, xi = vr, vi
            for lvl, k in enumerate((1, 2, 4)):
                kr, ki = coef_ref[2 * lvl, :, cols], coef_ref[2 * lvl + 1, :, cols]
                shift = SUBLANES - k if reverse else k
                sr, si = pltpu.roll(xr, shift, 0), pltpu.roll(xi, shift, 0)
                xr, xi = xr + (kr * sr - ki * si), xi + (kr * si + ki * sr)
            pr, pi = coef_ref[6, :, cols], coef_ref[7, :, cols]
            cr, ci = carry[0], carry[1]
            xr, xi = xr + (pr * cr - pi * ci), xi + (pr * ci + pi * cr)
            extra = store(r0, cols, xr, xi, vr, vi, carry[2:])
            edge = 0 if reverse else SUBLANES - 1
            return (xr[edge:edge + 1, :], xi[edge:edge + 1, :]) + tuple(extra)

        yield cb, cols, step


def _ssm_fwd(proj, coef, b_re, b_im, c_re, c_im, wz1, wz2, d_skip, g_out, name):
    n_rows = proj.shape[0]
    n_blk, _, n_state = b_re.shape
    width = n_blk * LANES
    tm = _row_tile(n_rows)
    n_tb = tm // SUBLANES

    def body(u_ref, coef_ref, bre_ref, bim_ref, cre_ref, cim_ref, wz1_ref, wz2_ref, d_ref, g_ref,
             xre_ref, xim_ref, o_ref, car_re, car_im, out_scr):
        @pl.when(pl.program_id(0) == 0)
        def _():
            car_re[...] = jnp.zeros_like(car_re)
            car_im[...] = jnp.zeros_like(car_im)

        ub = u_ref[...].astype(_MXU)
        for j in range(n_blk):
            uj = ub[:, j * LANES:(j + 1) * LANES]
            xre_ref[:, j * n_state:(j + 1) * n_state] = _dot(uj, bre_ref[j])
            xim_ref[:, j * n_state:(j + 1) * n_state] = _dot(uj, bim_ref[j])

        def load(r0, cols):
            return xre_ref[pl.ds(r0, SUBLANES), cols], xim_ref[pl.ds(r0, SUBLANES), cols]

        def store(r0, cols, xr, xi, vr, vi, extra):
            xre_ref[pl.ds(r0, SUBLANES), cols] = xr
            xim_ref[pl.ds(r0, SUBLANES), cols] = xi
            return ()

        for cb, cols, step in _scan_blocks(n_tb, 512, coef_ref, load, store, reverse=False):
            fin = lax.fori_loop(0, n_tb, step, (car_re[:, cols], car_im[:, cols]))
            car_re[:, cols] = fin[0]
            car_im[:, cols] = fin[1]

        ssq = None
        for j in range(n_blk):
            sl = slice(j * LANES, (j + 1) * LANES)
            st = slice(j * n_state, (j + 1) * n_state)
            yc = _dot(xre_ref[:, st].astype(_MXU), cre_ref[j]) - _dot(xim_ref[:, st].astype(_MXU), cim_ref[j])
            y = yc + d_ref[:, sl] * u_ref[:, sl]
            cdf, _ = _gelu_parts(y)
            gy = (y * cdf).astype(_MXU)
            out = _dot(gy, wz1_ref[j]) * _sigmoid(_dot(gy, wz2_ref[j]))
            out_scr[:, sl] = out
            part = jnp.sum(out * out, axis=-1, keepdims=True)
            ssq = part if ssq is None else ssq + part
        r = lax.rsqrt(ssq / width + RMS_EPS)
        o_ref[...] = (out_scr[...] * r * g_ref[...]).astype(_MXU)

    half = pl.BlockSpec((tm, width), lambda i: (i, 0))
    state = pl.BlockSpec((tm, n_blk * n_state), lambda i: (i, 0))
    return pl.pallas_call(
        body, name=name, grid=(n_rows // tm,),
        out_shape=[jax.ShapeDtypeStruct((n_rows, n_blk * n_state), F32)] * 2 + [jax.ShapeDtypeStruct((n_rows, width), _MXU)],
        in_specs=[half, _full(coef.shape), _full(b_re.shape), _full(b_im.shape), _full(c_re.shape), _full(c_im.shape),
                  _full(wz1.shape), _full(wz2.shape), _full((1, width)), _full((1, width))],
        out_specs=[state, state, half],
        scratch_shapes=[pltpu.VMEM((1, n_blk * n_state), F32)] * 2 + [pltpu.VMEM((tm, width), F32)],
        compiler_params=_params(),
    )(proj, coef, b_re, b_im, c_re, c_im, wz1, wz2, d_skip, g_out)


def _ssm_bwd(proj, x_re, x_im, dcat, coef_rev, b_re, b_im, c_re, c_im, wz1, wz2, d_skip, g_out, name):
    n_rows = proj.shape[0]
    n_blk, _, n_state = b_re.shape
    width = n_blk * LANES
    n_all = n_blk * n_state
    tm = _row_tile(n_rows)
    n_t, n_tb = n_rows // tm, tm // SUBLANES

    def body(u_ref, xre_ref, xim_ref, dc_ref, coef_ref, bre_ref, bim_ref, cre_ref, cim_ref, wz1_ref, wz2_ref, d_ref, g_ref,
             du_ref, dg_ref, dd_ref, dwz1_ref, dwz2_ref, dcre_ref, dcim_ref, dbre_ref, dbim_ref, are_ref, aim_ref,
             car_re, car_im, gre, gim, y_s, z1_s, sg_s, out_s, gy_s):
        @pl.when(pl.program_id(0) == 0)
        def _():
            for ref in (dg_ref, dd_ref, dwz1_ref, dwz2_ref, dcre_ref, dcim_ref, dbre_ref, dbim_ref, are_ref, aim_ref,
                        car_re, car_im):
                ref[...] = jnp.zeros_like(ref)

        ssq = None
        for j in range(n_blk):
            sl = slice(j * LANES, (j + 1) * LANES)
            st = slice(j * n_state, (j + 1) * n_state)
            yc = _dot(xre_ref[:, st].astype(_MXU), cre_ref[j]) - _dot(xim_ref[:, st].astype(_MXU), cim_ref[j])
            y = yc + d_ref[:, sl] * u_ref[:, sl]
            cdf, _ = _gelu_parts(y)
            gy = (y * cdf).astype(_MXU)
            z1 = _dot(gy, wz1_ref[j])
            sg = _sigmoid(_dot(gy, wz2_ref[j]))
            out = z1 * sg
            y_s[:, sl], z1_s[:, sl], sg_s[:, sl], out_s[:, sl], gy_s[:, sl] = y, z1, sg, out, gy
            part = jnp.sum(out * out, axis=-1, keepdims=True)
            ssq = part if ssq is None else ssq + part
        r = lax.rsqrt(ssq / width + RMS_EPS)
        ohat = out_s[...] * r
        dcv = dc_ref[...]
        dg_ref[...] += jnp.sum(dcv * ohat, axis=0, keepdims=True)
        doh = dcv * g_ref[...]
        out_s[...] = r * (doh - ohat * (jnp.sum(doh * ohat, axis=-1, keepdims=True) / width))

        for j in range(n_blk):
            sl = slice(j * LANES, (j + 1) * LANES)
            st = slice(j * n_state, (j + 1) * n_state)
            dout, sg, z1, y = out_s[:, sl], sg_s[:, sl], z1_s[:, sl], y_s[:, sl]
            dz1 = (dout * sg).astype(_MXU)
            dz2 = (dout * z1 * sg * (1.0 - sg)).astype(_MXU)
            gy = gy_s[:, sl]
            dwz1_ref[j] += _dot_tn(gy, dz1)
            dwz2_ref[j] += _dot_tn(gy, dz2)
            dgy = _dot_nt(dz1, wz1_ref[j]) + _dot_nt(dz2, wz2_ref[j])
            cdf, th = _gelu_parts(y)
            dy = dgy * (cdf + y * (0.5 * (1.0 - th * th) * GELU_C0 * (1.0 + 3.0 * GELU_C1 * (y * y))))
            uj = u_ref[:, sl]
            dd_ref[:, sl] += jnp.sum(dy * uj, axis=0, keepdims=True)
            du_ref[:, sl] = d_ref[:, sl] * dy
            dyb = dy.astype(_MXU)
            dcre_ref[j] += _dot_tn(dyb, xre_ref[:, st].astype(_MXU))
            dcim_ref[j] -= _dot_tn(dyb, xim_ref[:, st].astype(_MXU))
            gre[:, st] = _dot_nt(dyb, cre_ref[j])
            gim[:, st] = -_dot_nt(dyb, cim_ref[j])

        def load(r0, cols):
            return gre[pl.ds(r0, SUBLANES), cols], gim[pl.ds(r0, SUBLANES), cols]

        def store(r0, cols, gr, gi, vr, vi, extra):
            gre[pl.ds(r0, SUBLANES), cols] = gr
            gim[pl.ds(r0, SUBLANES), cols] = gi
            hr, hi = gr - vr, gi - vi
            xr, xi = xre_ref[pl.ds(r0, SUBLANES), cols], xim_ref[pl.ds(r0, SUBLANES), cols]
            return extra[0] + (xr * hr + xi * hi), extra[1] + (xr * hi - xi * hr)

        for cb, cols, step in _scan_blocks(n_tb, 512, coef_ref, load, store, reverse=True):
            zero = jnp.zeros((SUBLANES, 512), F32)
            fin = lax.fori_loop(0, n_tb, step, (car_re[:, cols], car_im[:, cols], zero, zero))
            car_re[:, cols] = fin[0]
            car_im[:, cols] = fin[1]
            are_ref[:, cols] += jnp.sum(fin[2], axis=0, keepdims=True)
            aim_ref[:, cols] += jnp.sum(fin[3], axis=0, keepdims=True)

        for j in range(n_blk):
            sl = slice(j * LANES, (j + 1) * LANES)
            st = slice(j * n_state, (j + 1) * n_state)
            ujb = u_ref[:, sl].astype(_MXU)
            grb, gib = gre[:, st].astype(_MXU), gim[:, st].astype(_MXU)
            dbre_ref[j] += _dot_tn(ujb, grb)
            dbim_ref[j] += _dot_tn(ujb, gib)
            du_ref[:, sl] += _dot_nt(grb, bre_ref[j]) + _dot_nt(gib, bim_ref[j])

    half = pl.BlockSpec((tm, width), lambda i: (n_t - 1 - i, 0))
    state = pl.BlockSpec((tm, n_all), lambda i: (n_t - 1 - i, 0))
    small = [(1, width), (1, width), wz1.shape, wz2.shape, (n_blk, LANES, n_state), (n_blk, LANES, n_state),
             (n_blk, LANES, n_state), (n_blk, LANES, n_state), (1, n_all), (1, n_all)]
    return pl.pallas_call(
        body, name=name, grid=(n_t,),
        out_shape=[jax.ShapeDtypeStruct((n_rows, width), F32)] + [jax.ShapeDtypeStruct(s, F32) for s in small],
        in_specs=[half, state, state, half, _full(coef_rev.shape), _full(b_re.shape), _full(b_im.shape), _full(c_re.shape),
                  _full(c_im.shape), _full(wz1.shape), _full(wz2.shape), _full((1, width)), _full((1, width))],
        out_specs=[half] + [_full(s) for s in small],
        scratch_shapes=[pltpu.VMEM((1, n_all), F32)] * 2 + [pltpu.VMEM((tm, n_all), F32)] * 2
        + [pltpu.VMEM((tm, width), F32)] * 4 + [pltpu.VMEM((tm, width), _MXU)],
        compiler_params=_params(),
    )(proj, x_re, x_im, dcat, coef_rev, b_re, b_im, c_re, c_im, wz1, wz2, d_skip, g_out)


def _pool_counts(tile, tm, window):
    t = tile * tm + lax.broadcasted_iota(jnp.int32, (tm, 1), 0)
    return jnp.minimum(t + 1, window).astype(F32)


def _pool_fwd(proj, pool_w, scale, g_out, name):
    n_rows = proj.shape[0]
    n_grp, grp, _ = pool_w.shape
    width = n_grp * grp
    tm = _row_tile(n_rows)

    def body(u_ref, pw_ref, sc_ref, g_ref, o_ref, ext, y_s):
        i = pl.program_id(0)

        @pl.when(i == 0)
        def _():
            ext[0:POOL_HALO, :] = jnp.zeros((POOL_HALO, width), F32)

        ext[POOL_HALO:, :] = u_ref[...]
        ssq = None
        for gi, w in enumerate(POOL_WINDOWS):
            sl = slice(gi * grp, (gi + 1) * grp)
            tot = ext[POOL_HALO:, sl]
            for k in range(1, w):
                tot = tot + ext[POOL_HALO - k:POOL_HALO - k + tm, sl]
            pooled = tot / _pool_counts(i, tm, w) - u_ref[:, sl]
            y = _dot(pooled.astype(_MXU), pw_ref[gi]) * sc_ref[:, sl]
            y_s[:, sl] = y
            part = jnp.sum(y * y, axis=-1, keepdims=True)
            ssq = part if ssq is None else ssq + part
        r = lax.rsqrt(ssq / width + RMS_EPS)
        o_ref[...] = (y_s[...] * r * g_ref[...]).astype(_MXU)
        ext[0:POOL_HALO, :] = u_ref[tm - POOL_HALO:, :]

    half_in = pl.BlockSpec((tm, width), lambda i: (i, 1))
    half = pl.BlockSpec((tm, width), lambda i: (i, 0))
    return pl.pallas_call(
        body, name=name, grid=(n_rows // tm,), out_shape=jax.ShapeDtypeStruct((n_rows, width), _MXU),
        in_specs=[half_in, _full(pool_w.shape), _full((1, width)), _full((1, width))], out_specs=half,
        scratch_shapes=[pltpu.VMEM((tm + POOL_HALO, width), F32), pltpu.VMEM((tm, width), F32)],
        compiler_params=_params(),
    )(proj, pool_w, scale, g_out)


def _pool_bwd(proj, dcat, pool_w, scale, g_out, name):
    n_rows = proj.shape[0]
    n_grp, grp, _ = pool_w.shape
    width = n_grp * grp
    tm = _row_tile(n_rows)
    n_t = n_rows // tm
    halo_blocks = tm // POOL_HALO

    def body(u_ref, up_ref, dc_ref, pw_ref, sc_ref, g_ref, du_ref, dg_ref, dsc_ref, dpw_ref, ext, qext, y_s, pl_s):
        i = pl.program_id(0)
        tile = n_t - 1 - i

        @pl.when(i == 0)
        def _():
            for ref in (dg_ref, dsc_ref, dpw_ref):
                ref[...] = jnp.zeros_like(ref)
            qext[tm:, :] = jnp.zeros((POOL_HALO, width), F32)

        ext[0:POOL_HALO, :] = jnp.where(tile > 0, up_ref[...], 0.0)
        ext[POOL_HALO:, :] = u_ref[...]
        ssq = None
        for gi, w in enumerate(POOL_WINDOWS):
            sl = slice(gi * grp, (gi + 1) * grp)
            tot = ext[POOL_HALO:, sl]
            for k in range(1, w):
                tot = tot + ext[POOL_HALO - k:POOL_HALO - k + tm, sl]
            pooled = (tot / _pool_counts(tile, tm, w) - u_ref[:, sl]).astype(_MXU)
            pl_s[:, sl] = pooled
            y0 = _dot(pooled, pw_ref[gi])
            y_s[:, sl] = y0
            y = y0 * sc_ref[:, sl]
            part = jnp.sum(y * y, axis=-1, keepdims=True)
            ssq = part if ssq is None else ssq + part
        r = lax.rsqrt(ssq / width + RMS_EPS)
        y0 = y_s[...]
        yhat = y0 * sc_ref[...] * r
        dcv = dc_ref[...]
        dg_ref[...] += jnp.sum(dcv * yhat, axis=0, keepdims=True)
        dyh = dcv * g_ref[...]
        dy = r * (dyh - yhat * (jnp.sum(dyh * yhat, axis=-1, keepdims=True) / width))
        dsc_ref[...] += jnp.sum(dy * y0, axis=0, keepdims=True)
        y_s[...] = dy * sc_ref[...]
        for gi, w in enumerate(POOL_WINDOWS):
            sl = slice(gi * grp, (gi + 1) * grp)
            dm = y_s[:, sl].astype(_MXU)
            dpw_ref[gi] += _dot_tn(pl_s[:, sl], dm)
            dpooled = _dot_nt(dm, pw_ref[gi])
            y_s[:, sl] = dpooled
            qext[0:tm, sl] = dpooled / _pool_counts(tile, tm, w)
        for gi, w in enumerate(POOL_WINDOWS):
            sl = slice(gi * grp, (gi + 1) * grp)
            tot = qext[0:tm, sl]
            for k in range(1, w):
                tot = tot + qext[k:k + tm, sl]
            du_ref[:, sl] = tot - y_s[:, sl]
        qext[tm:, :] = qext[0:POOL_HALO, :]

    half_in = pl.BlockSpec((tm, width), lambda i: (n_t - 1 - i, 1))
    prev = pl.BlockSpec((POOL_HALO, width), lambda i: (jnp.maximum((n_t - 1 - i) * halo_blocks - 1, 0), 1))
    half = pl.BlockSpec((tm, width), lambda i: (n_t - 1 - i, 0))
    return pl.pallas_call(
        body, name=name, grid=(n_t,),
        out_shape=[jax.ShapeDtypeStruct((n_rows, width), F32), jax.ShapeDtypeStruct((1, width), F32),
                   jax.ShapeDtypeStruct((1, width), F32), jax.ShapeDtypeStruct(pool_w.shape, F32)],
        in_specs=[half_in, prev, half, _full(pool_w.shape), _full((1, width)), _full((1, width))],
        out_specs=[half, _full((1, width)), _full((1, width)), _full(pool_w.shape)],
        scratch_shapes=[pltpu.VMEM((tm + POOL_HALO, width), F32), pltpu.VMEM((tm + POOL_HALO, width), F32),
                        pltpu.VMEM((tm, width), F32), pltpu.VMEM((tm, width), _MXU)],
        compiler_params=_params(),
    )(proj, proj, dcat, pool_w, scale, g_out)


def _mix_out_fwd(cat_s, cat_p, h, g, wo_s, wo_p, name):
    n_rows, d = h.shape
    width = cat_s.shape[1]
    tm = _row_tile(n_rows)

    def body(cs_ref, cp_ref, h_ref, g_ref, ws_ref, wp_ref, m_ref, ho_ref):
        m = _dot(cs_ref[...], ws_ref[...]) + _dot(cp_ref[...], wp_ref[...])
        m_ref[...] = m
        ho_ref[...] = h_ref[...] + m * _rs(m) * g_ref[...]

    row = pl.BlockSpec((tm, d), lambda i: (i, 0))
    half = pl.BlockSpec((tm, width), lambda i: (i, 0))
    return pl.pallas_call(
        body, name=name, grid=(n_rows // tm,), out_shape=[jax.ShapeDtypeStruct((n_rows, d), F32)] * 2,
        in_specs=[half, half, row, _full((1, d)), _resident(), _resident()], out_specs=[row, row],
        compiler_params=_params(),
    )(cat_s, cat_p, h, g, wo_s, wo_p)


def _mix_out_bwd(dho, mixed, g, wo_s, wo_p, name):
    n_rows, d = mixed.shape
    width = wo_s.shape[0]
    tm = _row_tile(n_rows)

    def body(dho_ref, m_ref, g_ref, ws_ref, wp_ref, dm_ref, dcs_ref, dcp_ref, dg_ref):
        @pl.when(pl.program_id(0) == 0)
        def _():
            dg_ref[...] = jnp.zeros_like(dg_ref)

        m = m_ref[...]
        r = _rs(m)
        mh = m * r
        dy = dho_ref[...]
        dg_ref[...] += jnp.sum(dy * mh, axis=0, keepdims=True)
        dmh = dy * g_ref[...]
        dm = (r * (dmh - mh * jnp.mean(dmh * mh, axis=-1, keepdims=True))).astype(_MXU)
        dm_ref[...] = dm
        dcs_ref[...] = _dot_nt(dm, ws_ref[...])
        dcp_ref[...] = _dot_nt(dm, wp_ref[...])

    row = pl.BlockSpec((tm, d), lambda i: (i, 0))
    half = pl.BlockSpec((tm, width), lambda i: (i, 0))
    return pl.pallas_call(
        body, name=name, grid=(n_rows // tm,),
        out_shape=[jax.ShapeDtypeStruct((n_rows, d), _MXU), jax.ShapeDtypeStruct((n_rows, width), F32),
                   jax.ShapeDtypeStruct((n_rows, width), F32), jax.ShapeDtypeStruct((1, d), F32)],
        in_specs=[row, row, _full((1, d)), _resident(), _resident()], out_specs=[row, half, half, _full((1, d))],
        compiler_params=_params(),
    )(dho, mixed, g, wo_s, wo_p)


def _mix_in_bwd(du_s, du_p, h, dho, g, wi_s, wi_p, name):
    n_rows, d = h.shape
    width = du_s.shape[1]
    tm = _row_tile(n_rows)

    def body(dus_ref, dup_ref, h_ref, dho_ref, g_ref, ws_ref, wp_ref, dh_ref, dp_ref, dg_ref):
        @pl.when(pl.program_id(0) == 0)
        def _():
            dg_ref[...] = jnp.zeros_like(dg_ref)

        dus, dup = dus_ref[...].astype(_MXU), dup_ref[...].astype(_MXU)
        dp_ref[:, 0:width] = dus
        dp_ref[:, width:2 * width] = dup
        dn = _dot_nt(dus, ws_ref[...]) + _dot_nt(dup, wp_ref[...])
        hv = h_ref[...]
        r = _rs(hv)
        hh = hv * r
        dg_ref[...] += jnp.sum(dn * hh, axis=0, keepdims=True)
        dhh = dn * g_ref[...]
        dh_ref[...] = dho_ref[...] + r * (dhh - hh * jnp.mean(dhh * hh, axis=-1, keepdims=True))

    row = pl.BlockSpec((tm, d), lambda i: (i, 0))
    half = pl.BlockSpec((tm, width), lambda i: (i, 0))
    return pl.pallas_call(
        body, name=name, grid=(n_rows // tm,),
        out_shape=[jax.ShapeDtypeStruct((n_rows, d), F32), jax.ShapeDtypeStruct((n_rows, 2 * width), _MXU),
                   jax.ShapeDtypeStruct((1, d), F32)],
        in_specs=[half, half, row, row, _full((1, d)), _resident(), _resident()],
        out_specs=[row, pl.BlockSpec((tm, 2 * width), lambda i: (i, 0)), _full((1, d))], compiler_params=_params(),
    )(du_s, du_p, h, dho, g, wi_s, wi_p)


def _loss_grad(h, target, name):
    n_rows, d = h.shape
    tm = _row_tile(n_rows)

    def body(h_ref, t_ref, dh_ref, l_ref):
        i = pl.program_id(0)

        @pl.when(i == 0)
        def _():
            l_ref[...] = jnp.zeros_like(l_ref)

        rows = i * tm + lax.broadcasted_iota(jnp.int32, (tm, 1), 0)
        err = jnp.where(rows >= N_META, h_ref[...] - t_ref[...], 0.0)
        dh_ref[...] = err / d
        l_ref[...] += jnp.sum(jnp.sum(err * err, axis=0, keepdims=True), axis=1, keepdims=True)

    row = pl.BlockSpec((tm, d), lambda i: (i, 0))
    return pl.pallas_call(
        body, name=name, grid=(n_rows // tm,),
        out_shape=[jax.ShapeDtypeStruct((n_rows, d), F32), jax.ShapeDtypeStruct((1, LANES), F32)],
        in_specs=[row, row], out_specs=[row, _full((1, LANES))], compiler_params=_params(),
    )(h, target)


def _adamw(w, g, m, v, name):
    def body(w_ref, g_ref, m_ref, v_ref, d_ref, mo_ref, vo_ref):
        gv = g_ref[...]
        mn = ADAM_B1 * m_ref[...] + (1.0 - ADAM_B1) * gv
        vn = ADAM_B2 * v_ref[...] + (1.0 - ADAM_B2) * (gv * gv)
        m_hat = mn / (1.0 - ADAM_B1 ** ADAM_STEP)
        v_hat = vn / (1.0 - ADAM_B2 ** ADAM_STEP)
        d_ref[...] = -ADAM_LR * (m_hat / (jnp.sqrt(v_hat) + ADAM_EPS) + ADAM_WD * w_ref[...])
        mo_ref[...] = mn
        vo_ref[...] = vn

    spec = _full(w.shape)
    return pl.pallas_call(
        body, name=name, grid=(1,), out_shape=[jax.ShapeDtypeStruct(w.shape, F32)] * 3,
        in_specs=[spec] * 4, out_specs=[spec] * 3, compiler_params=_params(),
    )(w, g, m, v)


def _discretize(lam_re, lam_im, log_dt, b_re, b_im):
    dt = jnp.exp(log_dt)[:, None]
    decay = jnp.exp(lam_re * dt)
    ang = lam_im * dt
    a_re = decay * jnp.cos(ang)
    a_im = decay * jnp.sin(ang)
    nr = a_re - 1.0
    den = lam_re * lam_re + lam_im * lam_im
    q_re = (nr * lam_re + a_im * lam_im) / den
    q_im = (a_im * lam_re - nr * lam_im) / den
    bb_re = q_re[..., None] * b_re - q_im[..., None] * b_im
    bb_im = q_re[..., None] * b_im + q_im[..., None] * b_re
    return a_re, a_im, bb_re, bb_im


def _cmul(a, b):
    return a[0] * b[0] - a[1] * b[1], a[0] * b[1] + a[1] * b[0]


def _scan_coefficients(a_re, a_im, reverse):
    a = (a_re.reshape(1, -1), -a_im.reshape(1, -1) if reverse else a_im.reshape(1, -1))
    powers = [a]
    for _ in range(SUBLANES - 1):
        powers.append(_cmul(powers[-1], a))
    row = jnp.arange(SUBLANES)[:, None]
    out = []
    for k in (1, 2, 4):
        keep = (row < SUBLANES - k) if reverse else (row >= k)
        out += [jnp.where(keep, powers[k - 1][0], 0.0), jnp.where(keep, powers[k - 1][1], 0.0)]
    order = range(SUBLANES - 1, -1, -1) if reverse else range(SUBLANES)
    out += [jnp.concatenate([powers[t][0] for t in order], axis=0), jnp.concatenate([powers[t][1] for t in order], axis=0)]
    return jnp.stack(out).astype(F32)


def _block_diag(p, n_blk):
    g, r, c = p.shape
    per = g // n_blk
    eye = jnp.eye(per, dtype=p.dtype)
    return jnp.einsum("jgrc,gk->jgrkc", p.reshape(n_blk, per, r, c), eye).reshape(n_blk, per * r, per * c)


def _block_diag_t(m, g):
    n_blk = m.shape[0]
    per = g // n_blk
    r, c = m.shape[1] // per, m.shape[2] // per
    eye = jnp.eye(per, dtype=m.dtype)
    return jnp.einsum("jgrkc,gk->jgrc", m.reshape(n_blk, per, r, per, c), eye).reshape(g, r, c)


def _pack_rows(parts, cols, multiple):
    flat = jnp.concatenate([p.reshape(-1) for p in parts])
    size = -(-flat.shape[0] // (cols * multiple)) * cols * multiple
    return jnp.pad(flat, (0, size - flat.shape[0])).reshape(-1, cols)


def _unpack(flat, shapes):
    out, pos = [], 0
    flat = flat.reshape(-1)
    for s in shapes:
        n = int(np.prod(s))
        out.append(flat[pos:pos + n].reshape(s))
        pos += n
    return out


def kernel(x, meta_tokens, ffn1_pre_norm, ffn1_post_norm, ffn1_w_gate, ffn1_w_up, ffn1_w_down, mix_pre_norm, mix_post_norm, w_in, ssm_lambda_re, ssm_lambda_im, ssm_log_dt, ssm_b_re, ssm_b_im, ssm_c_re, ssm_c_im, ssm_d, ssm_w_glu, pool_w, pool_scale, ssm_out_norm, pool_out_norm, w_out, ffn2_pre_norm, ffn2_post_norm, ffn2_w_gate, ffn2_w_up, ffn2_w_down, loss_target, m_meta_tokens, m_ffn1_pre_norm, m_ffn1_post_norm, m_ffn1_w_gate, m_ffn1_w_up, m_ffn1_w_down, m_mix_pre_norm, m_mix_post_norm, m_w_in, m_ssm_lambda_re, m_ssm_lambda_im, m_ssm_log_dt, m_ssm_b_re, m_ssm_b_im, m_ssm_c_re, m_ssm_c_im, m_ssm_d, m_ssm_w_glu, m_pool_w, m_pool_scale, m_ssm_out_norm, m_pool_out_norm, m_w_out, m_ffn2_pre_norm, m_ffn2_post_norm, m_ffn2_w_gate, m_ffn2_w_up, m_ffn2_w_down, v_meta_tokens, v_ffn1_pre_norm, v_ffn1_post_norm, v_ffn1_w_gate, v_ffn1_w_up, v_ffn1_w_down, v_mix_pre_norm, v_mix_post_norm, v_w_in, v_ssm_lambda_re, v_ssm_lambda_im, v_ssm_log_dt, v_ssm_b_re, v_ssm_b_im, v_ssm_c_re, v_ssm_c_im, v_ssm_d, v_ssm_w_glu, v_pool_w, v_pool_scale, v_ssm_out_norm, v_pool_out_norm, v_w_out, v_ffn2_pre_norm, v_ffn2_post_norm, v_ffn2_w_gate, v_ffn2_w_up, v_ffn2_w_down):
    args = dict(locals())
    names = ["meta_tokens", "ffn1_pre_norm", "ffn1_post_norm", "ffn1_w_gate", "ffn1_w_up", "ffn1_w_down", "mix_pre_norm",
             "mix_post_norm", "w_in", "ssm_lambda_re", "ssm_lambda_im", "ssm_log_dt", "ssm_b_re", "ssm_b_im", "ssm_c_re",
             "ssm_c_im", "ssm_d", "ssm_w_glu", "pool_w", "pool_scale", "ssm_out_norm", "pool_out_norm", "w_out",
             "ffn2_pre_norm", "ffn2_post_norm", "ffn2_w_gate", "ffn2_w_up", "ffn2_w_down"]
    sharded = ("meta_tokens", "ffn1_w_gate", "ffn1_w_up", "ffn1_w_down", "w_in", "w_out", "ffn2_w_gate", "ffn2_w_up",
               "ffn2_w_down")
    small = [n for n in names if n not in sharded]

    d = x.shape[-1]
    ff_shard = ffn1_w_down.shape[1]
    mix_shard = w_in.shape[1]
    d_ff = N_DEV * ff_shard
    width = d // 2
    n_grp = ssm_lambda_re.shape[1]
    n_blk = width // LANES

    shard_rows = [ffn1_w_gate[0].T, ffn1_w_up[0].T, ffn1_w_down[0], ffn2_w_gate[0].T, ffn2_w_up[0].T, ffn2_w_down[0],
                  w_in[0], w_out[0]]
    row_counts = [s.shape[0] for s in shard_rows]
    offsets = np.concatenate([[0], np.cumsum(row_counts)])
    packed = jnp.concatenate([s.astype(_WIRE) for s in shard_rows], axis=0)
    gathered, meta_all = _exchange([packed, meta_tokens], ["gather", "gather"], "gather_weights")

    def full_rows(k):
        return gathered[:, offsets[k]:offsets[k + 1], :].reshape(N_DEV * row_counts[k], d).astype(_MXU)

    wgt1, wut1, wd1, wgt2, wut2, wd2, w_in_f, w_out_f = [full_rows(k) for k in range(8)]
    meta_full = jnp.transpose(meta_all, (1, 0, 2)).reshape(N_META, d)

    h0 = jnp.concatenate([meta_full, x[0]], axis=0)
    target = jnp.concatenate([jnp.zeros((N_META, d), F32), loss_target[0]], axis=0)

    a1, b1, f1, h1 = _ffn_fwd(h0, ffn1_pre_norm, ffn1_post_norm, wgt1, wut1, wd1, "ffn1_fwd")

    a_re, a_im, bb_re, bb_im = _discretize(ssm_lambda_re[0], ssm_lambda_im[0], ssm_log_dt[0], ssm_b_re[0], ssm_b_im[0])
    coef = _scan_coefficients(a_re, a_im, reverse=False)
    coef_rev = _scan_coefficients(a_re, a_im, reverse=True)
    bmat_re = _block_diag(jnp.swapaxes(bb_re, 1, 2), n_blk).astype(_MXU)
    bmat_im = _block_diag(jnp.swapaxes(bb_im, 1, 2), n_blk).astype(_MXU)
    cmat_re = _block_diag(jnp.swapaxes(ssm_c_re[0], 1, 2), n_blk).astype(_MXU)
    cmat_im = _block_diag(jnp.swapaxes(ssm_c_im[0], 1, 2), n_blk).astype(_MXU)
    wz1 = _block_diag(ssm_w_glu[0][:, :, :SSM_GROUP_CH], n_blk).astype(_MXU)
    wz2 = _block_diag(ssm_w_glu[0][:, :, SSM_GROUP_CH:], n_blk).astype(_MXU)
    pool_wm = pool_w[0].astype(_MXU)

    n2, proj = _mix_in_fwd(h1, mix_pre_norm, w_in_f, "mix_in_fwd")
    x_re, x_im, cat_s = _ssm_fwd(proj, coef, bmat_re, bmat_im, cmat_re, cmat_im, wz1, wz2, ssm_d, ssm_out_norm, "ssm_fwd")
    cat_p = _pool_fwd(proj, pool_wm, pool_scale, pool_out_norm, "pool_fwd")
    wo_s, wo_p = w_out_f[:width], w_out_f[width:]
    mixed, h2 = _mix_out_fwd(cat_s, cat_p, h1, mix_post_norm, wo_s, wo_p, "mix_out_fwd")

    a2, b2, f2, h3 = _ffn_fwd(h2, ffn2_pre_norm, ffn2_post_norm, wgt2, wut2, wd2, "ffn2_fwd")
    dh3, sq_err = _loss_grad(h3, target, "loss_grad")

    g = {}
    dh2, da2, db2, s2, df2, nf2, g["ffn2_pre_norm"], g["ffn2_post_norm"] = _ffn_bwd(
        dh3, h2, f2, a2, b2, ffn2_pre_norm, ffn2_post_norm, wgt2, wut2, wd2, "ffn2_bwd")
    big = [None] * 8
    big[3] = _wgrad(da2, nf2, "ffn2_dgate")
    big[4] = _wgrad(db2, nf2, "ffn2_dup")
    big[5] = _wgrad(s2, df2, "ffn2_ddown")

    dmixed, dcat_s, dcat_p, g["mix_post_norm"] = _mix_out_bwd(dh2, mixed, mix_post_norm, wo_s, wo_p, "mix_out_bwd")
    big[7] = jnp.concatenate([_wgrad(cat_s, dmixed, "dwout_s"), _wgrad(cat_p, dmixed, "dwout_p")], axis=0)
    (du_s, g["ssm_out_norm"], g["ssm_d"], dwz1, dwz2, dcm_re, dcm_im, dbm_re, dbm_im, acc_re, acc_im) = _ssm_bwd(
        proj, x_re, x_im, dcat_s, coef_rev, bmat_re, bmat_im, cmat_re, cmat_im, wz1, wz2, ssm_d, ssm_out_norm, "ssm_bwd")
    du_p, g["pool_out_norm"], g["pool_scale"], dpw = _pool_bwd(proj, dcat_p, pool_wm, pool_scale, pool_out_norm, "pool_bwd")
    wi_s, wi_p = w_in_f[:, :width], w_in_f[:, width:]
    dh1, dproj, g["mix_pre_norm"] = _mix_in_bwd(du_s, du_p, h1, dh2, mix_pre_norm, wi_s, wi_p, "mix_in_bwd")
    big[6] = _wgrad(n2, dproj, "dwin")

    dh0, da1, db1, s1, df1, nf1, g["ffn1_pre_norm"], g["ffn1_post_norm"] = _ffn_bwd(
        dh1, h0, f1, a1, b1, ffn1_pre_norm, ffn1_post_norm, wgt1, wut1, wd1, "ffn1_bwd")
    big[0] = _wgrad(da1, nf1, "ffn1_dgate")
    big[1] = _wgrad(db1, nf1, "ffn1_dup")
    big[2] = _wgrad(s1, df1, "ffn1_ddown")

    g["ssm_c_re"] = jnp.swapaxes(_block_diag_t(jnp.swapaxes(dcm_re, 1, 2), n_grp), 1, 2)[None]
    g["ssm_c_im"] = jnp.swapaxes(_block_diag_t(jnp.swapaxes(dcm_im, 1, 2), n_grp), 1, 2)[None]
    g["ssm_w_glu"] = jnp.concatenate([_block_diag_t(dwz1, n_grp), _block_diag_t(dwz2, n_grp)], axis=-1)[None]
    dbb_re = jnp.swapaxes(_block_diag_t(dbm_re, n_grp), 1, 2)
    dbb_im = jnp.swapaxes(_block_diag_t(dbm_im, n_grp), 1, 2)
    acc_re, acc_im = acc_re.reshape(a_re.shape), acc_im.reshape(a_re.shape)
    norm = a_re * a_re + a_im * a_im
    da_re = (acc_re * a_re - acc_im * a_im) / norm
    da_im = (acc_re * a_im + acc_im * a_re) / norm
    _, disc_vjp = jax.vjp(_discretize, ssm_lambda_re[0], ssm_lambda_im[0], ssm_log_dt[0], ssm_b_re[0], ssm_b_im[0])
    d_lre, d_lim, d_ldt, d_bre, d_bim = disc_vjp((da_re, da_im, dbb_re, dbb_im))
    g["ssm_lambda_re"], g["ssm_lambda_im"], g["ssm_log_dt"] = d_lre[None], d_lim[None], d_ldt[None]
    g["ssm_b_re"], g["ssm_b_im"] = d_bre[None], d_bim[None]
    g["pool_w"] = dpw[None]

    chunks = jnp.concatenate([b.reshape(N_DEV, rc, d) for b, rc in zip(big, row_counts)], axis=1).astype(_WIRE)
    small_parts = [g[n] for n in small] + [dh0[:N_META], sq_err[:, :1]]
    small_vec = _pack_rows(small_parts, 1024, SUBLANES)
    recv_big, recv_small = _exchange([chunks, small_vec], ["scatter", "gather"], "reduce_grads")
    g_rows = _sum_slots(recv_big, "sum_matrix_grads")
    g_small = _sum_slots(recv_small, "sum_small_grads")
    summed = _unpack(g_small, [g[n].shape for n in small] + [(N_META, d), (1,)])
    for n, val in zip(small, summed):
        g[n] = val
    me = 4 * lax.axis_index("x") + 2 * lax.axis_index("y") + lax.axis_index("c")
    g["meta_tokens"] = lax.dynamic_slice_in_dim(summed[-2], me * (d // N_DEV), d // N_DEV, axis=1)
    loss = (0.5 / d) * summed[-1][0]

    def rows_of(k):
        return g_rows[offsets[k]:offsets[k + 1]]

    g["ffn1_w_gate"], g["ffn1_w_up"], g["ffn1_w_down"] = rows_of(0).T[None], rows_of(1).T[None], rows_of(2)[None]
    g["ffn2_w_gate"], g["ffn2_w_up"], g["ffn2_w_down"] = rows_of(3).T[None], rows_of(4).T[None], rows_of(5)[None]
    g["w_in"], g["w_out"] = rows_of(6)[None], rows_of(7)[None]

    delta, new_m, new_v = {}, {}, {}
    for n in sharded:
        shape = args[n].shape
        two_d = (-1, shape[-1])
        dl, mn, vn = _adamw(args[n].reshape(two_d), g[n].reshape(two_d), args["m_" + n].reshape(two_d),
                            args["v_" + n].reshape(two_d), "adamw_" + n)
        delta[n], new_m[n], new_v[n] = dl.reshape(shape), mn.reshape(shape), vn.reshape(shape)
    packs = [_pack_rows([src[n] if pre is None else args[pre + n] for n in small], 1024, SUBLANES)
             for src, pre in ((args, None), (g, None), (None, "m_"), (None, "v_"))]
    outs = _adamw(*packs, "adamw_small")
    shapes = [args[n].shape for n in small]
    for store, flat in zip((delta, new_m, new_v), outs):
        for n, val in zip(small, _unpack(flat, shapes)):
            store[n] = val

    grad_x = dh0[N_META:][None]
    return (loss, grad_x, *[g[n] for n in names], *[delta[n] for n in names], *[new_m[n] for n in names],
            *[new_v[n] for n in names])
```

```python
import functools
import math

import jax
import jax.numpy as jnp
import numpy as np
from jax import lax
from jax.experimental import pallas as pl
from jax.experimental.pallas import tpu as pltpu

F32 = jnp.float32
_MXU = jnp.bfloat16
_ACT = jnp.bfloat16
_WIRE = jnp.bfloat16

N_DEV = 8
N_META = 16
RMS_EPS = 1e-6
SSM_GROUP_CH = 16
SSM_STATE = 64
LANES = 128
SUBLANES = 8
POOL_WINDOWS = (2, 4, 8, 16)
POOL_HALO = 16
ADAM_LR = 0.001
ADAM_B1 = 0.9
ADAM_B2 = 0.999
ADAM_EPS = 1e-08
ADAM_WD = 0.01
ADAM_STEP = 10
GELU_C0 = math.sqrt(2.0 / math.pi)
GELU_C1 = 0.044715
VMEM_LIMIT = 56 * 1024 * 1024

_NT = (((1,), (1,)), ((), ()))
_TN = (((0,), (0,)), ((), ()))


def _dot(a, b):
    return jnp.dot(a, b, preferred_element_type=F32)


def _dot_nt(a, b):
    return lax.dot_general(a, b, _NT, preferred_element_type=F32)


def _dot_tn(a, b):
    return lax.dot_general(a, b, _TN, preferred_element_type=F32)


def _rs(x):
    return lax.rsqrt(jnp.mean(x * x, axis=-1, keepdims=True) + RMS_EPS)


def _sigmoid(x):
    return 1.0 / (1.0 + jnp.exp(-x))


def _row_tile(n_rows, largest=432):
    for t in (432, 304, 48, 16):
        if t <= largest and n_rows % t == 0:
            return t
    raise ValueError(n_rows)


def _ff_chunk(d_ff):
    return d_ff // 2 if (d_ff // 2) % LANES == 0 else d_ff


def _params(n_axes=1):
    return pltpu.CompilerParams(dimension_semantics=("arbitrary",) * n_axes, vmem_limit_bytes=VMEM_LIMIT)


def _resident():
    return pl.BlockSpec(memory_space=pltpu.VMEM)


def _full(shape):
    nd = len(shape)
    return pl.BlockSpec(shape, lambda *_: (0,) * nd)


PEER_ORDER = (1, 2, 4, 3, 5, 6, 7)


def _split(refs, counts):
    out, pos = [], 0
    for n in counts:
        out.append(refs[pos:pos + n])
        pos += n
    return out


def _peer(r):
    x, y, c = lax.axis_index("x"), lax.axis_index("y"), lax.axis_index("c")
    return (1 - x if r & 4 else x, 1 - y if r & 2 else y, 1 - c if r & 1 else c)


def _my_slot():
    return 4 * lax.axis_index("x") + 2 * lax.axis_index("y") + lax.axis_index("c")


class _Xchg:
    def __init__(self, srcs, kinds):
        self.srcs, self.kinds, self.n = list(srcs), list(kinds), len(srcs)
        self.out_shape = [jax.ShapeDtypeStruct((N_DEV,) + s.shape if k == "gather" else s.shape, s.dtype)
                          for s, k in zip(self.srcs, self.kinds)]
        self.specs = [pl.BlockSpec(memory_space=pl.ANY)] * self.n
        self.scratch = [pltpu.SemaphoreType.DMA((self.n * (N_DEV - 1),)), pltpu.SemaphoreType.DMA((self.n * (N_DEV - 1),)),
                        pltpu.SemaphoreType.DMA((self.n,))]

    def copies(self, src, dst, sems):
        send_sems, recv_sems, local_sems = sems
        me = _my_slot()
        out = []
        for a in range(self.n):
            mine = src[a] if self.kinds[a] == "gather" else src[a].at[me]
            out.append(pltpu.make_async_copy(mine, dst[a].at[me], local_sems.at[a]))
            for r in PEER_ORDER:
                px, py, pc = _peer(r)
                part = src[a] if self.kinds[a] == "gather" else src[a].at[4 * px + 2 * py + pc]
                k = a * (N_DEV - 1) + r - 1
                out.append(pltpu.make_async_remote_copy(
                    src_ref=part, dst_ref=dst[a].at[me], send_sem=send_sems.at[k], recv_sem=recv_sems.at[k],
                    device_id=(px, py, pc), device_id_type=pl.DeviceIdType.MESH))
        return out

    def start(self, src, dst, sems):
        for cp in self.copies(src, dst, sems):
            cp.start()

    def wait(self, src, dst, sems):
        for cp in self.copies(src, dst, sems):
            cp.wait()


class _NoXchg:
    n, srcs, out_shape, specs, scratch = 0, [], [], [], []

    def start(self, *_):
        pass

    wait = start


def _call(body, args, *, name, grid, out_shape, in_specs, out_specs, scratch_shapes=(), xchg=None):
    xc = xchg or _NoXchg()
    counts = (len(in_specs), xc.n, len(out_shape), xc.n, len(scratch_shapes), len(xc.scratch))

    def wrapped(*refs):
        ins, xsrc, outs, xdst, scr, sems = _split(refs, counts)
        ids = [pl.program_id(k) for k in range(len(grid))]
        if xc.n:
            @pl.when(functools.reduce(jnp.logical_and, [i == 0 for i in ids]))
            def _():
                xc.start(xsrc, xdst, sems)

        body(*ins, *outs, *scr)
        if xc.n:
            @pl.when(functools.reduce(jnp.logical_and, [i == g - 1 for i, g in zip(ids, grid)]))
            def _():
                xc.wait(xsrc, xdst, sems)

    res = pl.pallas_call(
        wrapped, name=name, grid=grid, out_shape=list(out_shape) + xc.out_shape,
        in_specs=list(in_specs) + xc.specs, out_specs=list(out_specs) + xc.specs,
        scratch_shapes=list(scratch_shapes) + xc.scratch, compiler_params=_params(len(grid)),
    )(*args, *xc.srcs)
    return res[:len(out_shape)], res[len(out_shape):]


def _exchange(srcs, kinds, name):
    xc = _Xchg(srcs, kinds)

    def body(*refs):
        src, dst, sems = _split(refs, (xc.n, xc.n, 3))
        xc.start(src, dst, sems)
        xc.wait(src, dst, sems)

    return pl.pallas_call(body, name=name, out_shape=xc.out_shape, in_specs=xc.specs, out_specs=xc.specs,
                          scratch_shapes=xc.scratch)(*srcs)


def _gather_two_level(srcs, name):
    n = len(srcs)
    out_shape = [jax.ShapeDtypeStruct((N_DEV,) + s.shape, s.dtype) for s in srcs]
    chips = (2, 4, 6)

    def body(*refs):
        src, dst, (send_sems, recv_sems, local_sems) = _split(refs, (n, n, 3))
        x, y, c = lax.axis_index("x"), lax.axis_index("y"), lax.axis_index("c")
        me = 4 * x + 2 * y + c
        sibling = (x, y, 1 - c)

        def copy(a, k, slot, to, from_src=False):
            return pltpu.make_async_remote_copy(
                src_ref=src[a] if from_src else dst[a].at[slot], dst_ref=dst[a].at[slot],
                send_sem=send_sems.at[a * 7 + k], recv_sem=recv_sems.at[a * 7 + k],
                device_id=to, device_id_type=pl.DeviceIdType.MESH)

        def slot_of(r, core):
            px, py, _ = _peer(r)
            return 4 * px + 2 * py + core

        local = [pltpu.make_async_copy(src[a], dst[a].at[me], local_sems.at[a]) for a in range(n)]
        sent = []
        for a in range(n):
            local[a].start()
            sent.append(copy(a, 0, me, sibling, from_src=True))
            sent += [copy(a, 1 + j, me, _peer(r), from_src=True) for j, r in enumerate(chips)]
        for cp in sent:
            cp.start()
        for j, r in enumerate(chips):
            for a in range(n):
                copy(a, 1 + j, slot_of(r, c), _peer(r)).wait_recv()
                cp = copy(a, 4 + j, slot_of(r, c), sibling)
                cp.start()
                sent.append(cp)
        for a in range(n):
            copy(a, 0, slot_of(0, 1 - c), sibling).wait_recv()
            for j, r in enumerate(chips):
                copy(a, 4 + j, slot_of(r, 1 - c), sibling).wait_recv()
        for cp in local:
            cp.wait()
        for cp in sent:
            cp.wait_send()

    any_spec = pl.BlockSpec(memory_space=pl.ANY)
    return pl.pallas_call(
        body, name=name, out_shape=out_shape, in_specs=[any_spec] * n, out_specs=[any_spec] * n,
        scratch_shapes=[pltpu.SemaphoreType.DMA((n * 7,)), pltpu.SemaphoreType.DMA((n * 7,)), pltpu.SemaphoreType.DMA((n,))],
    )(*srcs)


def _sum_slots(r, name):
    _, rows, cols = r.shape
    blk = rows
    for cand in (rows, 592, 512, 256, 128, 64, 32, 16):
        if rows % cand == 0 and N_DEV * cand * cols * r.dtype.itemsize <= 8 * 1024 * 1024:
            blk = cand
            break

    def body(r_ref, o_ref):
        acc = r_ref[0].astype(F32)
        for d in range(1, N_DEV):
            acc = acc + r_ref[d].astype(F32)
        o_ref[...] = acc

    return pl.pallas_call(
        body, name=name, grid=(rows // blk,), out_shape=jax.ShapeDtypeStruct((rows, cols), F32),
        in_specs=[pl.BlockSpec((N_DEV, blk, cols), lambda i: (0, i, 0))],
        out_specs=pl.BlockSpec((blk, cols), lambda i: (i, 0)), compiler_params=_params(),
    )(r)


def _ffn_fwd(h, g_pre, g_post, wgt, wut, wd, name, xchg=None):
    n_rows, d = h.shape
    d_ff = wd.shape[0]
    tm, fc = _row_tile(n_rows), _ff_chunk(d_ff)
    n_t, n_c = n_rows // tm, d_ff // fc

    def body(h_ref, gpre_ref, gpost_ref, wgt_ref, wut_ref, wd_ref, a_ref, b_ref, f_ref, ho_ref, n_scr, acc):
        c = pl.program_id(1)

        @pl.when(c == 0)
        def _():
            hv = h_ref[...]
            n_scr[...] = (hv * _rs(hv) * gpre_ref[...]).astype(_MXU)
            acc[...] = jnp.zeros_like(acc)

        rows = pl.ds(pl.multiple_of(c * fc, fc), fc)
        nv = n_scr[...]
        a = _dot_nt(nv, wgt_ref[rows, :])
        b = _dot_nt(nv, wut_ref[rows, :])
        a_ref[...] = a.astype(_ACT)
        b_ref[...] = b.astype(_ACT)
        s = a * _sigmoid(a) * b
        acc[...] += _dot(s.astype(_MXU), wd_ref[rows, :])

        @pl.when(c == n_c - 1)
        def _():
            f = acc[...]
            f_ref[...] = f
            ho_ref[...] = h_ref[...] + 0.5 * (f * _rs(f) * gpost_ref[...])

    row = pl.BlockSpec((tm, d), lambda i, c: (i, 0))
    chunk = pl.BlockSpec((tm, fc), lambda i, c: (i, c))
    return _call(
        body, (h, g_pre, g_post, wgt, wut, wd), name=name, grid=(n_t, n_c),
        out_shape=[jax.ShapeDtypeStruct((n_rows, d_ff), _ACT), jax.ShapeDtypeStruct((n_rows, d_ff), _ACT),
                   jax.ShapeDtypeStruct((n_rows, d), F32), jax.ShapeDtypeStruct((n_rows, d), F32)],
        in_specs=[row, _full((1, d)), _full((1, d)), _resident(), _resident(), _resident()],
        out_specs=[chunk, chunk, row, row],
        scratch_shapes=[pltpu.VMEM((tm, d), _MXU), pltpu.VMEM((tm, d), F32)], xchg=xchg)


def _ffn_bwd(dho, h, f, a, b, g_pre, g_post, wgt, wut, wd, name, xchg=None):
    n_rows, d = h.shape
    d_ff = wd.shape[0]
    tm, fc = _row_tile(n_rows, 304), _ff_chunk(d_ff)
    n_t, n_c = n_rows // tm, d_ff // fc

    def body(dho_ref, h_ref, f_ref, a_ref, b_ref, gpre_ref, gpost_ref, wgt_ref, wut_ref, wd_ref,
             dh_ref, da_ref, db_ref, s_ref, df_ref, n_ref, dgpre_ref, dgpost_ref, dn_acc):
        i, c = pl.program_id(0), pl.program_id(1)

        @pl.when((i == 0) & (c == 0))
        def _():
            dgpre_ref[...] = jnp.zeros_like(dgpre_ref)
            dgpost_ref[...] = jnp.zeros_like(dgpost_ref)

        @pl.when(c == 0)
        def _():
            fv = f_ref[...]
            rf = _rs(fv)
            fhat = fv * rf
            dy = 0.5 * dho_ref[...]
            dgpost_ref[...] += jnp.sum(dy * fhat, axis=0, keepdims=True)
            dfhat = dy * gpost_ref[...]
            df = rf * (dfhat - fhat * jnp.mean(dfhat * fhat, axis=-1, keepdims=True))
            df_ref[...] = df.astype(_MXU)
            hv = h_ref[...]
            n_ref[...] = (hv * _rs(hv) * gpre_ref[...]).astype(_MXU)
            dn_acc[...] = jnp.zeros_like(dn_acc)

        rows = pl.ds(pl.multiple_of(c * fc, fc), fc)
        ds = _dot_nt(df_ref[...], wd_ref[rows, :])
        av = a_ref[...].astype(F32)
        bv = b_ref[...].astype(F32)
        sg = _sigmoid(av)
        si = av * sg
        da = (ds * bv * (sg * (1.0 + av * (1.0 - sg)))).astype(_MXU)
        db = (ds * si).astype(_MXU)
        da_ref[...] = da
        db_ref[...] = db
        s_ref[...] = (si * bv).astype(_MXU)
        dn_acc[...] += _dot(da, wgt_ref[rows, :]) + _dot(db, wut_ref[rows, :])

        @pl.when(c == n_c - 1)
        def _():
            dn = dn_acc[...]
            hv = h_ref[...]
            r = _rs(hv)
            hhat = hv * r
            dgpre_ref[...] += jnp.sum(dn * hhat, axis=0, keepdims=True)
            dhh = dn * gpre_ref[...]
            dh_ref[...] = dho_ref[...] + r * (dhh - hhat * jnp.mean(dhh * hhat, axis=-1, keepdims=True))

    row = pl.BlockSpec((tm, d), lambda i, c: (i, 0))
    chunk = pl.BlockSpec((tm, fc), lambda i, c: (i, c))
    vec = pl.BlockSpec((1, d), lambda i, c: (0, 0))
    return _call(
        body, (dho, h, f, a, b, g_pre, g_post, wgt, wut, wd), name=name, grid=(n_t, n_c),
        out_shape=[jax.ShapeDtypeStruct((n_rows, d), F32)] + [jax.ShapeDtypeStruct((n_rows, d_ff), _MXU)] * 3
        + [jax.ShapeDtypeStruct((n_rows, d), _MXU)] * 2 + [jax.ShapeDtypeStruct((1, d), F32)] * 2,
        in_specs=[row, row, row, chunk, chunk, vec, vec, _resident(), _resident(), _resident()],
        out_specs=[row, chunk, chunk, chunk, row, row, vec, vec],
        scratch_shapes=[pltpu.VMEM((tm, d), F32)], xchg=xchg)


def _wgrad(xm, ym, name, xchg=None):
    n_rows, a_dim = xm.shape
    b_dim = ym.shape[1]
    tk = n_rows
    for cand in (2736, 1296, 432, 48, 16):
        if n_rows % cand == 0:
            tk = cand
            break
    ta = a_dim
    for cand in (1408, 1024, 512):
        if a_dim % cand == 0:
            ta = cand
            break

    n_k = n_rows // tk

    def body(x_ref, y_ref, o_ref, acc):
        k = pl.program_id(1)

        @pl.when(k == 0)
        def _():
            acc[...] = jnp.zeros_like(acc)

        acc[...] += _dot_tn(x_ref[...], y_ref[...])

        @pl.when(k == n_k - 1)
        def _():
            o_ref[...] = acc[...].astype(o_ref.dtype)

    (out,), extra = _call(
        body, (xm, ym), name=name, grid=(a_dim // ta, n_k), out_shape=[jax.ShapeDtypeStruct((a_dim, b_dim), _WIRE)],
        in_specs=[pl.BlockSpec((tk, ta), lambda j, k: (k, j)), pl.BlockSpec((tk, b_dim), lambda j, k: (k, 0))],
        out_specs=[pl.BlockSpec((ta, b_dim), lambda j, k: (j, 0))], scratch_shapes=[pltpu.VMEM((ta, b_dim), F32)],
        xchg=xchg)
    return out, extra


def _mix_in_fwd(h, g, w_in, name):
    n_rows, d = h.shape
    tm = _row_tile(n_rows)

    def body(h_ref, g_ref, w_ref, n_ref, p_ref):
        hv = h_ref[...]
        nv = (hv * _rs(hv) * g_ref[...]).astype(_MXU)
        n_ref[...] = nv
        p_ref[...] = _dot(nv, w_ref[...])

    row = pl.BlockSpec((tm, d), lambda i: (i, 0))
    return pl.pallas_call(
        body, name=name, grid=(n_rows // tm,),
        out_shape=[jax.ShapeDtypeStruct((n_rows, d), _MXU), jax.ShapeDtypeStruct((n_rows, w_in.shape[1]), F32)],
        in_specs=[row, _full((1, d)), _resident()],
        out_specs=[row, pl.BlockSpec((tm, w_in.shape[1]), lambda i: (i, 0))], compiler_params=_params(),
    )(h, g, w_in)


def _gelu_parts(y):
    th = jnp.tanh(GELU_C0 * (y + GELU_C1 * (y * y * y)))
    return 0.5 * (1.0 + th), th


def _scan_blocks(n_blocks, width, coef_ref, load, store, reverse):
    n_cols = coef_ref.shape[2] // width
    for cb in range(n_cols):
        cols = pl.ds(cb * width, width)

        def step(t, carry, cols=cols):
            tb = (n_blocks - 1 - t) if reverse else t
            r0 = pl.multiple_of(tb * SUBLANES, SUBLANES)
            vr, vi = load(r0, cols)
            xr, xi = vr, vi
            for lvl, k in enumerate((1, 2, 4)):
                kr, ki = coef_ref[2 * lvl, :, cols], coef_ref[2 * lvl + 1, :, cols]
                shift = SUBLANES - k if reverse else k
                sr, si = pltpu.roll(xr, shift, 0), pltpu.roll(xi, shift, 0)
                xr, xi = xr + (kr * sr - ki * si), xi + (kr * si + ki * sr)
            pr, pi = coef_ref[6, :, cols], coef_ref[7, :, cols]
            cr, ci = carry[0], carry[1]
            xr, xi = xr + (pr * cr - pi * ci), xi + (pr * ci + pi * cr)
            extra = store(r0, cols, xr, xi, vr, vi, carry[2:])
            edge = 0 if reverse else SUBLANES - 1
            return (xr[edge:edge + 1, :], xi[edge:edge + 1, :]) + tuple(extra)

        yield cb, cols, step


def _ssm_fwd(proj, coef, b_re, b_im, c_re, c_im, wz1, wz2, d_skip, g_out, name, xchg=None):
    n_rows = proj.shape[0]
    n_blk, _, n_state = b_re.shape
    width = n_blk * LANES
    tm = _row_tile(n_rows)
    n_tb = tm // SUBLANES

    def body(u_ref, coef_ref, bre_ref, bim_ref, cre_ref, cim_ref, wz1_ref, wz2_ref, d_ref, g_ref,
             xre_ref, xim_ref, o_ref, car_re, car_im, out_scr):
        @pl.when(pl.program_id(0) == 0)
        def _():
            car_re[...] = jnp.zeros_like(car_re)
            car_im[...] = jnp.zeros_like(car_im)

        ub = u_ref[...].astype(_MXU)
        for j in range(n_blk):
            uj = ub[:, j * LANES:(j + 1) * LANES]
            xre_ref[:, j * n_state:(j + 1) * n_state] = _dot(uj, bre_ref[j])
            xim_ref[:, j * n_state:(j + 1) * n_state] = _dot(uj, bim_ref[j])

        def load(r0, cols):
            return xre_ref[pl.ds(r0, SUBLANES), cols], xim_ref[pl.ds(r0, SUBLANES), cols]

        def store(r0, cols, xr, xi, vr, vi, extra):
            xre_ref[pl.ds(r0, SUBLANES), cols] = xr
            xim_ref[pl.ds(r0, SUBLANES), cols] = xi
            return ()

        for cb, cols, step in _scan_blocks(n_tb, 512, coef_ref, load, store, reverse=False):
            fin = lax.fori_loop(0, n_tb, step, (car_re[:, cols], car_im[:, cols]))
            car_re[:, cols] = fin[0]
            car_im[:, cols] = fin[1]

        ssq = None
        for j in range(n_blk):
            sl = slice(j * LANES, (j + 1) * LANES)
            st = slice(j * n_state, (j + 1) * n_state)
            yc = _dot(xre_ref[:, st].astype(_MXU), cre_ref[j]) - _dot(xim_ref[:, st].astype(_MXU), cim_ref[j])
            y = yc + d_ref[:, sl] * u_ref[:, sl]
            cdf, _ = _gelu_parts(y)
            gy = (y * cdf).astype(_MXU)
            out = _dot(gy, wz1_ref[j]) * _sigmoid(_dot(gy, wz2_ref[j]))
            out_scr[:, sl] = out
            part = jnp.sum(out * out, axis=-1, keepdims=True)
            ssq = part if ssq is None else ssq + part
        r = lax.rsqrt(ssq / width + RMS_EPS)
        o_ref[...] = (out_scr[...] * r * g_ref[...]).astype(_MXU)

    half = pl.BlockSpec((tm, width), lambda i: (i, 0))
    state = pl.BlockSpec((tm, n_blk * n_state), lambda i: (i, 0))
    return _call(
        body, (proj, coef, b_re, b_im, c_re, c_im, wz1, wz2, d_skip, g_out), name=name, grid=(n_rows // tm,),
        out_shape=[jax.ShapeDtypeStruct((n_rows, n_blk * n_state), F32)] * 2 + [jax.ShapeDtypeStruct((n_rows, width), _MXU)],
        in_specs=[half, _full(coef.shape), _full(b_re.shape), _full(b_im.shape), _full(c_re.shape), _full(c_im.shape),
                  _full(wz1.shape), _full(wz2.shape), _full((1, width)), _full((1, width))],
        out_specs=[state, state, half],
        scratch_shapes=[pltpu.VMEM((1, n_blk * n_state), F32)] * 2 + [pltpu.VMEM((tm, width), F32)], xchg=xchg)


def _ssm_bwd(proj, x_re, x_im, dcat, coef_rev, b_re, b_im, c_re, c_im, wz1, wz2, d_skip, g_out, name, xchg=None):
    n_rows = proj.shape[0]
    n_blk, _, n_state = b_re.shape
    width = n_blk * LANES
    n_all = n_blk * n_state
    tm = _row_tile(n_rows)
    n_t, n_tb = n_rows // tm, tm // SUBLANES

    def body(u_ref, xre_ref, xim_ref, dc_ref, coef_ref, bre_ref, bim_ref, cre_ref, cim_ref, wz1_ref, wz2_ref, d_ref, g_ref,
             du_ref, dg_ref, dd_ref, dwz1_ref, dwz2_ref, dcre_ref, dcim_ref, dbre_ref, dbim_ref, are_ref, aim_ref,
             car_re, car_im, gre, gim, y_s, z1_s, sg_s, out_s, gy_s):
        @pl.when(pl.program_id(0) == 0)
        def _():
            for ref in (dg_ref, dd_ref, dwz1_ref, dwz2_ref, dcre_ref, dcim_ref, dbre_ref, dbim_ref, are_ref, aim_ref,
                        car_re, car_im):
                ref[...] = jnp.zeros_like(ref)

        ssq = None
        for j in range(n_blk):
            sl = slice(j * LANES, (j + 1) * LANES)
            st = slice(j * n_state, (j + 1) * n_state)
            yc = _dot(xre_ref[:, st].astype(_MXU), cre_ref[j]) - _dot(xim_ref[:, st].astype(_MXU), cim_ref[j])
            y = yc + d_ref[:, sl] * u_ref[:, sl]
            cdf, _ = _gelu_parts(y)
            gy = (y * cdf).astype(_MXU)
            z1 = _dot(gy, wz1_ref[j])
            sg = _sigmoid(_dot(gy, wz2_ref[j]))
            out = z1 * sg
            y_s[:, sl], z1_s[:, sl], sg_s[:, sl], out_s[:, sl], gy_s[:, sl] = y, z1, sg, out, gy
            part = jnp.sum(out * out, axis=-1, keepdims=True)
            ssq = part if ssq is None else ssq + part
        r = lax.rsqrt(ssq / width + RMS_EPS)
        ohat = out_s[...] * r
        dcv = dc_ref[...]
        dg_ref[...] += jnp.sum(dcv * ohat, axis=0, keepdims=True)
        doh = dcv * g_ref[...]
        out_s[...] = r * (doh - ohat * (jnp.sum(doh * ohat, axis=-1, keepdims=True) / width))

        for j in range(n_blk):
            sl = slice(j * LANES, (j + 1) * LANES)
            st = slice(j * n_state, (j + 1) * n_state)
            dout, sg, z1, y = out_s[:, sl], sg_s[:, sl], z1_s[:, sl], y_s[:, sl]
            dz1 = (dout * sg).astype(_MXU)
            dz2 = (dout * z1 * sg * (1.0 - sg)).astype(_MXU)
            gy = gy_s[:, sl]
            dwz1_ref[j] += _dot_tn(gy, dz1)
            dwz2_ref[j] += _dot_tn(gy, dz2)
            dgy = _dot_nt(dz1, wz1_ref[j]) + _dot_nt(dz2, wz2_ref[j])
            cdf, th = _gelu_parts(y)
            dy = dgy * (cdf + y * (0.5 * (1.0 - th * th) * GELU_C0 * (1.0 + 3.0 * GELU_C1 * (y * y))))
            uj = u_ref[:, sl]
            dd_ref[:, sl] += jnp.sum(dy * uj, axis=0, keepdims=True)
            du_ref[:, sl] = d_ref[:, sl] * dy
            dyb = dy.astype(_MXU)
            dcre_ref[j] += _dot_tn(dyb, xre_ref[:, st].astype(_MXU))
            dcim_ref[j] -= _dot_tn(dyb, xim_ref[:, st].astype(_MXU))
            gre[:, st] = _dot_nt(dyb, cre_ref[j])
            gim[:, st] = -_dot_nt(dyb, cim_ref[j])

        def load(r0, cols):
            return gre[pl.ds(r0, SUBLANES), cols], gim[pl.ds(r0, SUBLANES), cols]

        def store(r0, cols, gr, gi, vr, vi, extra):
            gre[pl.ds(r0, SUBLANES), cols] = gr
            gim[pl.ds(r0, SUBLANES), cols] = gi
            hr, hi = gr - vr, gi - vi
            xr, xi = xre_ref[pl.ds(r0, SUBLANES), cols], xim_ref[pl.ds(r0, SUBLANES), cols]
            return extra[0] + (xr * hr + xi * hi), extra[1] + (xr * hi - xi * hr)

        for cb, cols, step in _scan_blocks(n_tb, 512, coef_ref, load, store, reverse=True):
            zero = jnp.zeros((SUBLANES, 512), F32)
            fin = lax.fori_loop(0, n_tb, step, (car_re[:, cols], car_im[:, cols], zero, zero))
            car_re[:, cols] = fin[0]
            car_im[:, cols] = fin[1]
            are_ref[:, cols] += jnp.sum(fin[2], axis=0, keepdims=True)
            aim_ref[:, cols] += jnp.sum(fin[3], axis=0, keepdims=True)

        for j in range(n_blk):
            sl = slice(j * LANES, (j + 1) * LANES)
            st = slice(j * n_state, (j + 1) * n_state)
            ujb = u_ref[:, sl].astype(_MXU)
            grb, gib = gre[:, st].astype(_MXU), gim[:, st].astype(_MXU)
            dbre_ref[j] += _dot_tn(ujb, grb)
            dbim_ref[j] += _dot_tn(ujb, gib)
            du_ref[:, sl] += _dot_nt(grb, bre_ref[j]) + _dot_nt(gib, bim_ref[j])

    half = pl.BlockSpec((tm, width), lambda i: (n_t - 1 - i, 0))
    state = pl.BlockSpec((tm, n_all), lambda i: (n_t - 1 - i, 0))
    small = [(1, width), (1, width), wz1.shape, wz2.shape, (n_blk, LANES, n_state), (n_blk, LANES, n_state),
             (n_blk, LANES, n_state), (n_blk, LANES, n_state), (1, n_all), (1, n_all)]
    return _call(
        body, (proj, x_re, x_im, dcat, coef_rev, b_re, b_im, c_re, c_im, wz1, wz2, d_skip, g_out), name=name, grid=(n_t,),
        out_shape=[jax.ShapeDtypeStruct((n_rows, width), F32)] + [jax.ShapeDtypeStruct(s, F32) for s in small],
        in_specs=[half, state, state, half, _full(coef_rev.shape), _full(b_re.shape), _full(b_im.shape), _full(c_re.shape),
                  _full(c_im.shape), _full(wz1.shape), _full(wz2.shape), _full((1, width)), _full((1, width))],
        out_specs=[half] + [_full(s) for s in small],
        scratch_shapes=[pltpu.VMEM((1, n_all), F32)] * 2 + [pltpu.VMEM((tm, n_all), F32)] * 2
        + [pltpu.VMEM((tm, width), F32)] * 4 + [pltpu.VMEM((tm, width), _MXU)], xchg=xchg)


def _pool_counts(tile, tm, window):
    t = tile * tm + lax.broadcasted_iota(jnp.int32, (tm, 1), 0)
    return jnp.minimum(t + 1, window).astype(F32)


def _pool_fwd(proj, pool_w, scale, g_out, name):
    n_rows = proj.shape[0]
    n_grp, grp, _ = pool_w.shape
    width = n_grp * grp
    tm = _row_tile(n_rows)

    def body(u_ref, pw_ref, sc_ref, g_ref, o_ref, ext, y_s):
        i = pl.program_id(0)

        @pl.when(i == 0)
        def _():
            ext[0:POOL_HALO, :] = jnp.zeros((POOL_HALO, width), F32)

        ext[POOL_HALO:, :] = u_ref[...]
        ssq = None
        for gi, w in enumerate(POOL_WINDOWS):
            sl = slice(gi * grp, (gi + 1) * grp)
            tot = ext[POOL_HALO:, sl]
            for k in range(1, w):
                tot = tot + ext[POOL_HALO - k:POOL_HALO - k + tm, sl]
            pooled = tot / _pool_counts(i, tm, w) - u_ref[:, sl]
            y = _dot(pooled.astype(_MXU), pw_ref[gi]) * sc_ref[:, sl]
            y_s[:, sl] = y
            part = jnp.sum(y * y, axis=-1, keepdims=True)
            ssq = part if ssq is None else ssq + part
        r = lax.rsqrt(ssq / width + RMS_EPS)
        o_ref[...] = (y_s[...] * r * g_ref[...]).astype(_MXU)
        ext[0:POOL_HALO, :] = u_ref[tm - POOL_HALO:, :]

    half_in = pl.BlockSpec((tm, width), lambda i: (i, 1))
    half = pl.BlockSpec((tm, width), lambda i: (i, 0))
    return pl.pallas_call(
        body, name=name, grid=(n_rows // tm,), out_shape=jax.ShapeDtypeStruct((n_rows, width), _MXU),
        in_specs=[half_in, _full(pool_w.shape), _full((1, width)), _full((1, width))], out_specs=half,
        scratch_shapes=[pltpu.VMEM((tm + POOL_HALO, width), F32), pltpu.VMEM((tm, width), F32)],
        compiler_params=_params(),
    )(proj, pool_w, scale, g_out)


def _pool_bwd(proj, dcat, pool_w, scale, g_out, name):
    n_rows = proj.shape[0]
    n_grp, grp, _ = pool_w.shape
    width = n_grp * grp
    tm = _row_tile(n_rows)
    n_t = n_rows // tm
    halo_blocks = tm // POOL_HALO

    def body(u_ref, up_ref, dc_ref, pw_ref, sc_ref, g_ref, du_ref, dg_ref, dsc_ref, dpw_ref, ext, qext, y_s, pl_s):
        i = pl.program_id(0)
        tile = n_t - 1 - i

        @pl.when(i == 0)
        def _():
            for ref in (dg_ref, dsc_ref, dpw_ref):
                ref[...] = jnp.zeros_like(ref)
            qext[tm:, :] = jnp.zeros((POOL_HALO, width), F32)

        ext[0:POOL_HALO, :] = jnp.where(tile > 0, up_ref[...], 0.0)
        ext[POOL_HALO:, :] = u_ref[...]
        ssq = None
        for gi, w in enumerate(POOL_WINDOWS):
            sl = slice(gi * grp, (gi + 1) * grp)
            tot = ext[POOL_HALO:, sl]
            for k in range(1, w):
                tot = tot + ext[POOL_HALO - k:POOL_HALO - k + tm, sl]
            pooled = (tot / _pool_counts(tile, tm, w) - u_ref[:, sl]).astype(_MXU)
            pl_s[:, sl] = pooled
            y0 = _dot(pooled, pw_ref[gi])
            y_s[:, sl] = y0
            y = y0 * sc_ref[:, sl]
            part = jnp.sum(y * y, axis=-1, keepdims=True)
            ssq = part if ssq is None else ssq + part
        r = lax.rsqrt(ssq / width + RMS_EPS)
        y0 = y_s[...]
        yhat = y0 * sc_ref[...] * r
        dcv = dc_ref[...]
        dg_ref[...] += jnp.sum(dcv * yhat, axis=0, keepdims=True)
        dyh = dcv * g_ref[...]
        dy = r * (dyh - yhat * (jnp.sum(dyh * yhat, axis=-1, keepdims=True) / width))
        dsc_ref[...] += jnp.sum(dy * y0, axis=0, keepdims=True)
        y_s[...] = dy * sc_ref[...]
        for gi, w in enumerate(POOL_WINDOWS):
            sl = slice(gi * grp, (gi + 1) * grp)
            dm = y_s[:, sl].astype(_MXU)
            dpw_ref[gi] += _dot_tn(pl_s[:, sl], dm)
            dpooled = _dot_nt(dm, pw_ref[gi])
            y_s[:, sl] = dpooled
            qext[0:tm, sl] = dpooled / _pool_counts(tile, tm, w)
        for gi, w in enumerate(POOL_WINDOWS):
            sl = slice(gi * grp, (gi + 1) * grp)
            tot = qext[0:tm, sl]
            for k in range(1, w):
                tot = tot + qext[k:k + tm, sl]
            du_ref[:, sl] = tot - y_s[:, sl]
        qext[tm:, :] = qext[0:POOL_HALO, :]

    half_in = pl.BlockSpec((tm, width), lambda i: (n_t - 1 - i, 1))
    prev = pl.BlockSpec((POOL_HALO, width), lambda i: (jnp.maximum((n_t - 1 - i) * halo_blocks - 1, 0), 1))
    half = pl.BlockSpec((tm, width), lambda i: (n_t - 1 - i, 0))
    return pl.pallas_call(
        body, name=name, grid=(n_t,),
        out_shape=[jax.ShapeDtypeStruct((n_rows, width), F32), jax.ShapeDtypeStruct((1, width), F32),
                   jax.ShapeDtypeStruct((1, width), F32), jax.ShapeDtypeStruct(pool_w.shape, F32)],
        in_specs=[half_in, prev, half, _full(pool_w.shape), _full((1, width)), _full((1, width))],
        out_specs=[half, _full((1, width)), _full((1, width)), _full(pool_w.shape)],
        scratch_shapes=[pltpu.VMEM((tm + POOL_HALO, width), F32), pltpu.VMEM((tm + POOL_HALO, width), F32),
                        pltpu.VMEM((tm, width), F32), pltpu.VMEM((tm, width), _MXU)],
        compiler_params=_params(),
    )(proj, proj, dcat, pool_w, scale, g_out)


def _mix_out_fwd(cat_s, cat_p, h, g, wo_s, wo_p, name):
    n_rows, d = h.shape
    width = cat_s.shape[1]
    tm = _row_tile(n_rows)

    def body(cs_ref, cp_ref, h_ref, g_ref, ws_ref, wp_ref, m_ref, ho_ref):
        m = _dot(cs_ref[...], ws_ref[...]) + _dot(cp_ref[...], wp_ref[...])
        m_ref[...] = m
        ho_ref[...] = h_ref[...] + m * _rs(m) * g_ref[...]

    row = pl.BlockSpec((tm, d), lambda i: (i, 0))
    half = pl.BlockSpec((tm, width), lambda i: (i, 0))
    return pl.pallas_call(
        body, name=name, grid=(n_rows // tm,), out_shape=[jax.ShapeDtypeStruct((n_rows, d), F32)] * 2,
        in_specs=[half, half, row, _full((1, d)), _resident(), _resident()], out_specs=[row, row],
        compiler_params=_params(),
    )(cat_s, cat_p, h, g, wo_s, wo_p)


def _mix_out_bwd(dho, mixed, g, wo_s, wo_p, name):
    n_rows, d = mixed.shape
    width = wo_s.shape[0]
    tm = _row_tile(n_rows)

    def body(dho_ref, m_ref, g_ref, ws_ref, wp_ref, dm_ref, dcs_ref, dcp_ref, dg_ref):
        @pl.when(pl.program_id(0) == 0)
        def _():
            dg_ref[...] = jnp.zeros_like(dg_ref)

        m = m_ref[...]
        r = _rs(m)
        mh = m * r
        dy = dho_ref[...]
        dg_ref[...] += jnp.sum(dy * mh, axis=0, keepdims=True)
        dmh = dy * g_ref[...]
        dm = (r * (dmh - mh * jnp.mean(dmh * mh, axis=-1, keepdims=True))).astype(_MXU)
        dm_ref[...] = dm
        dcs_ref[...] = _dot_nt(dm, ws_ref[...])
        dcp_ref[...] = _dot_nt(dm, wp_ref[...])

    row = pl.BlockSpec((tm, d), lambda i: (i, 0))
    half = pl.BlockSpec((tm, width), lambda i: (i, 0))
    return pl.pallas_call(
        body, name=name, grid=(n_rows // tm,),
        out_shape=[jax.ShapeDtypeStruct((n_rows, d), _MXU), jax.ShapeDtypeStruct((n_rows, width), F32),
                   jax.ShapeDtypeStruct((n_rows, width), F32), jax.ShapeDtypeStruct((1, d), F32)],
        in_specs=[row, row, _full((1, d)), _resident(), _resident()], out_specs=[row, half, half, _full((1, d))],
        compiler_params=_params(),
    )(dho, mixed, g, wo_s, wo_p)


def _mix_in_bwd(du_s, du_p, h, dho, g, wi_s, wi_p, name):
    n_rows, d = h.shape
    width = du_s.shape[1]
    tm = _row_tile(n_rows)

    def body(dus_ref, dup_ref, h_ref, dho_ref, g_ref, ws_ref, wp_ref, dh_ref, dp_ref, dg_ref):
        @pl.when(pl.program_id(0) == 0)
        def _():
            dg_ref[...] = jnp.zeros_like(dg_ref)

        dus, dup = dus_ref[...].astype(_MXU), dup_ref[...].astype(_MXU)
        dp_ref[:, 0:width] = dus
        dp_ref[:, width:2 * width] = dup
        dn = _dot_nt(dus, ws_ref[...]) + _dot_nt(dup, wp_ref[...])
        hv = h_ref[...]
        r = _rs(hv)
        hh = hv * r
        dg_ref[...] += jnp.sum(dn * hh, axis=0, keepdims=True)
        dhh = dn * g_ref[...]
        dh_ref[...] = dho_ref[...] + r * (dhh - hh * jnp.mean(dhh * hh, axis=-1, keepdims=True))

    row = pl.BlockSpec((tm, d), lambda i: (i, 0))
    half = pl.BlockSpec((tm, width), lambda i: (i, 0))
    return pl.pallas_call(
        body, name=name, grid=(n_rows // tm,),
        out_shape=[jax.ShapeDtypeStruct((n_rows, d), F32), jax.ShapeDtypeStruct((n_rows, 2 * width), _MXU),
                   jax.ShapeDtypeStruct((1, d), F32)],
        in_specs=[half, half, row, row, _full((1, d)), _resident(), _resident()],
        out_specs=[row, pl.BlockSpec((tm, 2 * width), lambda i: (i, 0)), _full((1, d))], compiler_params=_params(),
    )(du_s, du_p, h, dho, g, wi_s, wi_p)


def _loss_grad(h, target, name):
    n_rows, d = h.shape
    tm = _row_tile(n_rows)

    def body(h_ref, t_ref, dh_ref, l_ref):
        i = pl.program_id(0)

        @pl.when(i == 0)
        def _():
            l_ref[...] = jnp.zeros_like(l_ref)

        rows = i * tm + lax.broadcasted_iota(jnp.int32, (tm, 1), 0)
        err = jnp.where(rows >= N_META, h_ref[...] - t_ref[...], 0.0)
        dh_ref[...] = err / d
        l_ref[...] += jnp.sum(jnp.sum(err * err, axis=0, keepdims=True), axis=1, keepdims=True)

    row = pl.BlockSpec((tm, d), lambda i: (i, 0))
    return pl.pallas_call(
        body, name=name, grid=(n_rows // tm,),
        out_shape=[jax.ShapeDtypeStruct((n_rows, d), F32), jax.ShapeDtypeStruct((1, LANES), F32)],
        in_specs=[row, row], out_specs=[row, _full((1, LANES))], compiler_params=_params(),
    )(h, target)


def _adamw_update(w_ref, gv, m_ref, v_ref, d_ref, mo_ref, vo_ref):
    mn = ADAM_B1 * m_ref[...] + (1.0 - ADAM_B1) * gv
    vn = ADAM_B2 * v_ref[...] + (1.0 - ADAM_B2) * (gv * gv)
    m_hat = mn / (1.0 - ADAM_B1 ** ADAM_STEP)
    v_hat = vn / (1.0 - ADAM_B2 ** ADAM_STEP)
    d_ref[...] = -ADAM_LR * (m_hat / (jnp.sqrt(v_hat) + ADAM_EPS) + ADAM_WD * w_ref[...])
    mo_ref[...] = mn
    vo_ref[...] = vn


def _adamw(w, g, m, v, name):
    def body(w_ref, g_ref, m_ref, v_ref, d_ref, mo_ref, vo_ref):
        _adamw_update(w_ref, g_ref[...], m_ref, v_ref, d_ref, mo_ref, vo_ref)

    spec = _full(w.shape)
    return pl.pallas_call(
        body, name=name, grid=(1,), out_shape=[jax.ShapeDtypeStruct(w.shape, F32)] * 3,
        in_specs=[spec] * 4, out_specs=[spec] * 3, compiler_params=_params(),
    )(w, g, m, v)


def _adamw_slots(w, slots, m, v, name):
    def body(w_ref, s_ref, m_ref, v_ref, g_ref, d_ref, mo_ref, vo_ref):
        gv = s_ref[0].astype(F32)
        for k in range(1, N_DEV):
            gv = gv + s_ref[k].astype(F32)
        g_ref[...] = gv
        _adamw_update(w_ref, gv, m_ref, v_ref, d_ref, mo_ref, vo_ref)

    spec = _full(w.shape)
    return pl.pallas_call(
        body, name=name, grid=(1,), out_shape=[jax.ShapeDtypeStruct(w.shape, F32)] * 4,
        in_specs=[spec, _full(slots.shape), spec, spec], out_specs=[spec] * 4, compiler_params=_params(),
    )(w, slots, m, v)


def _discretize(lam_re, lam_im, log_dt, b_re, b_im):
    dt = jnp.exp(log_dt)[:, None]
    decay = jnp.exp(lam_re * dt)
    ang = lam_im * dt
    a_re = decay * jnp.cos(ang)
    a_im = decay * jnp.sin(ang)
    nr = a_re - 1.0
    den = lam_re * lam_re + lam_im * lam_im
    q_re = (nr * lam_re + a_im * lam_im) / den
    q_im = (a_im * lam_re - nr * lam_im) / den
    bb_re = q_re[..., None] * b_re - q_im[..., None] * b_im
    bb_im = q_re[..., None] * b_im + q_im[..., None] * b_re
    return a_re, a_im, bb_re, bb_im


def _cmul(a, b):
    return a[0] * b[0] - a[1] * b[1], a[0] * b[1] + a[1] * b[0]


def _scan_coefficients(a_re, a_im, reverse):
    a = (a_re.reshape(1, -1), -a_im.reshape(1, -1) if reverse else a_im.reshape(1, -1))
    powers = [a]
    for _ in range(SUBLANES - 1):
        powers.append(_cmul(powers[-1], a))
    row = jnp.arange(SUBLANES)[:, None]
    out = []
    for k in (1, 2, 4):
        keep = (row < SUBLANES - k) if reverse else (row >= k)
        out += [jnp.where(keep, powers[k - 1][0], 0.0), jnp.where(keep, powers[k - 1][1], 0.0)]
    order = range(SUBLANES - 1, -1, -1) if reverse else range(SUBLANES)
    out += [jnp.concatenate([powers[t][0] for t in order], axis=0), jnp.concatenate([powers[t][1] for t in order], axis=0)]
    return jnp.stack(out).astype(F32)


def _block_diag(p, n_blk):
    g, r, c = p.shape
    per = g // n_blk
    eye = jnp.eye(per, dtype=p.dtype)
    return jnp.einsum("jgrc,gk->jgrkc", p.reshape(n_blk, per, r, c), eye).reshape(n_blk, per * r, per * c)


def _block_diag_t(m, g):
    n_blk = m.shape[0]
    per = g // n_blk
    r, c = m.shape[1] // per, m.shape[2] // per
    eye = jnp.eye(per, dtype=m.dtype)
    return jnp.einsum("jgrkc,gk->jgrc", m.reshape(n_blk, per, r, per, c), eye).reshape(g, r, c)


def _pack_rows(parts, cols, multiple):
    flat = jnp.concatenate([p.reshape(-1) for p in parts])
    size = -(-flat.shape[0] // (cols * multiple)) * cols * multiple
    return jnp.pad(flat, (0, size - flat.shape[0])).reshape(-1, cols)


def _unpack(flat, shapes):
    out, pos = [], 0
    flat = flat.reshape(-1)
    for s in shapes:
        n = int(np.prod(s))
        out.append(flat[pos:pos + n].reshape(s))
        pos += n
    return out


def kernel(x, meta_tokens, ffn1_pre_norm, ffn1_post_norm, ffn1_w_gate, ffn1_w_up, ffn1_w_down, mix_pre_norm, mix_post_norm, w_in, ssm_lambda_re, ssm_lambda_im, ssm_log_dt, ssm_b_re, ssm_b_im, ssm_c_re, ssm_c_im, ssm_d, ssm_w_glu, pool_w, pool_scale, ssm_out_norm, pool_out_norm, w_out, ffn2_pre_norm, ffn2_post_norm, ffn2_w_gate, ffn2_w_up, ffn2_w_down, loss_target, m_meta_tokens, m_ffn1_pre_norm, m_ffn1_post_norm, m_ffn1_w_gate, m_ffn1_w_up, m_ffn1_w_down, m_mix_pre_norm, m_mix_post_norm, m_w_in, m_ssm_lambda_re, m_ssm_lambda_im, m_ssm_log_dt, m_ssm_b_re, m_ssm_b_im, m_ssm_c_re, m_ssm_c_im, m_ssm_d, m_ssm_w_glu, m_pool_w, m_pool_scale, m_ssm_out_norm, m_pool_out_norm, m_w_out, m_ffn2_pre_norm, m_ffn2_post_norm, m_ffn2_w_gate, m_ffn2_w_up, m_ffn2_w_down, v_meta_tokens, v_ffn1_pre_norm, v_ffn1_post_norm, v_ffn1_w_gate, v_ffn1_w_up, v_ffn1_w_down, v_mix_pre_norm, v_mix_post_norm, v_w_in, v_ssm_lambda_re, v_ssm_lambda_im, v_ssm_log_dt, v_ssm_b_re, v_ssm_b_im, v_ssm_c_re, v_ssm_c_im, v_ssm_d, v_ssm_w_glu, v_pool_w, v_pool_scale, v_ssm_out_norm, v_pool_out_norm, v_w_out, v_ffn2_pre_norm, v_ffn2_post_norm, v_ffn2_w_gate, v_ffn2_w_up, v_ffn2_w_down):
    args = dict(locals())
    names = ["meta_tokens", "ffn1_pre_norm", "ffn1_post_norm", "ffn1_w_gate", "ffn1_w_up", "ffn1_w_down", "mix_pre_norm",
             "mix_post_norm", "w_in", "ssm_lambda_re", "ssm_lambda_im", "ssm_log_dt", "ssm_b_re", "ssm_b_im", "ssm_c_re",
             "ssm_c_im", "ssm_d", "ssm_w_glu", "pool_w", "pool_scale", "ssm_out_norm", "pool_out_norm", "w_out",
             "ffn2_pre_norm", "ffn2_post_norm", "ffn2_w_gate", "ffn2_w_up", "ffn2_w_down"]
    sharded = ("meta_tokens", "ffn1_w_gate", "ffn1_w_up", "ffn1_w_down", "w_in", "w_out", "ffn2_w_gate", "ffn2_w_up",
               "ffn2_w_down")
    small = [n for n in names if n not in sharded]

    d = x.shape[-1]
    width = d // 2
    n_grp = ssm_lambda_re.shape[1]
    n_blk = width // LANES

    def wire(m):
        return m.astype(_WIRE)

    def stacked(gathered):
        return gathered.reshape(-1, d).astype(_MXU)

    def chunked(m):
        return m.reshape(N_DEV, -1, d)

    g_gate1, g_up1, g_down1, meta_all = _gather_two_level(
        [wire(ffn1_w_gate[0].T), wire(ffn1_w_up[0].T), wire(ffn1_w_down[0]), meta_tokens], "gather_ffn1")
    wgt1, wut1, wd1 = stacked(g_gate1), stacked(g_up1), stacked(g_down1)
    meta_full = jnp.transpose(meta_all, (1, 0, 2)).reshape(N_META, d)

    h0 = jnp.concatenate([meta_full, x[0]], axis=0)
    target = jnp.concatenate([jnp.zeros((N_META, d), F32), loss_target[0]], axis=0)

    (a1, b1, f1, h1), (g_win, g_wout) = _ffn_fwd(
        h0, ffn1_pre_norm, ffn1_post_norm, wgt1, wut1, wd1, "ffn1_fwd",
        xchg=_Xchg([wire(w_in[0]), wire(w_out[0])], ["gather"] * 2))
    w_in_f, w_out_f = stacked(g_win), stacked(g_wout)

    a_re, a_im, bb_re, bb_im = _discretize(ssm_lambda_re[0], ssm_lambda_im[0], ssm_log_dt[0], ssm_b_re[0], ssm_b_im[0])
    coef = _scan_coefficients(a_re, a_im, reverse=False)
    coef_rev = _scan_coefficients(a_re, a_im, reverse=True)
    bmat_re = _block_diag(jnp.swapaxes(bb_re, 1, 2), n_blk).astype(_MXU)
    bmat_im = _block_diag(jnp.swapaxes(bb_im, 1, 2), n_blk).astype(_MXU)
    cmat_re = _block_diag(jnp.swapaxes(ssm_c_re[0], 1, 2), n_blk).astype(_MXU)
    cmat_im = _block_diag(jnp.swapaxes(ssm_c_im[0], 1, 2), n_blk).astype(_MXU)
    wz1 = _block_diag(ssm_w_glu[0][:, :, :SSM_GROUP_CH], n_blk).astype(_MXU)
    wz2 = _block_diag(ssm_w_glu[0][:, :, SSM_GROUP_CH:], n_blk).astype(_MXU)
    pool_wm = pool_w[0].astype(_MXU)

    n2, proj = _mix_in_fwd(h1, mix_pre_norm, w_in_f, "mix_in_fwd")
    (x_re, x_im, cat_s), (g_gate2, g_up2, g_down2) = _ssm_fwd(
        proj, coef, bmat_re, bmat_im, cmat_re, cmat_im, wz1, wz2, ssm_d, ssm_out_norm, "ssm_fwd",
        xchg=_Xchg([wire(ffn2_w_gate[0].T), wire(ffn2_w_up[0].T), wire(ffn2_w_down[0])], ["gather"] * 3))
    wgt2, wut2, wd2 = stacked(g_gate2), stacked(g_up2), stacked(g_down2)
    cat_p = _pool_fwd(proj, pool_wm, pool_scale, pool_out_norm, "pool_fwd")
    wo_s, wo_p = w_out_f[:width], w_out_f[width:]
    mixed, h2 = _mix_out_fwd(cat_s, cat_p, h1, mix_post_norm, wo_s, wo_p, "mix_out_fwd")

    (a2, b2, f2, h3), _ = _ffn_fwd(h2, ffn2_pre_norm, ffn2_post_norm, wgt2, wut2, wd2, "ffn2_fwd")
    dh3, sq_err = _loss_grad(h3, target, "loss_grad")

    g, slots = {}, {}
    (dh2, da2, db2, s2, df2, nf2, g["ffn2_pre_norm"], g["ffn2_post_norm"]), _ = _ffn_bwd(
        dh3, h2, f2, a2, b2, ffn2_pre_norm, ffn2_post_norm, wgt2, wut2, wd2, "ffn2_bwd")
    dgate2, _ = _wgrad(da2, nf2, "ffn2_dgate")
    dup2, _ = _wgrad(db2, nf2, "ffn2_dup")
    ddown2, _ = _wgrad(s2, df2, "ffn2_ddown")

    dmixed, dcat_s, dcat_p, g["mix_post_norm"] = _mix_out_bwd(dh2, mixed, mix_post_norm, wo_s, wo_p, "mix_out_bwd")
    dwout = jnp.concatenate([_wgrad(cat_s, dmixed, "dwout_s")[0], _wgrad(cat_p, dmixed, "dwout_p")[0]], axis=0)
    ((du_s, g["ssm_out_norm"], g["ssm_d"], dwz1, dwz2, dcm_re, dcm_im, dbm_re, dbm_im, acc_re, acc_im),
     (slots["ffn2_w_gate"], slots["ffn2_w_up"], slots["ffn2_w_down"])) = _ssm_bwd(
        proj, x_re, x_im, dcat_s, coef_rev, bmat_re, bmat_im, cmat_re, cmat_im, wz1, wz2, ssm_d, ssm_out_norm, "ssm_bwd",
        xchg=_Xchg([chunked(dgate2), chunked(dup2), chunked(ddown2)], ["scatter"] * 3))
    du_p, g["pool_out_norm"], g["pool_scale"], dpw = _pool_bwd(proj, dcat_p, pool_wm, pool_scale, pool_out_norm, "pool_bwd")
    wi_s, wi_p = w_in_f[:, :width], w_in_f[:, width:]
    dh1, dproj, g["mix_pre_norm"] = _mix_in_bwd(du_s, du_p, h1, dh2, mix_pre_norm, wi_s, wi_p, "mix_in_bwd")
    dwin, _ = _wgrad(n2, dproj, "dwin")

    ((dh0, da1, db1, s1, df1, nf1, g["ffn1_pre_norm"], g["ffn1_post_norm"]), (slots["w_out"], slots["w_in"])) = _ffn_bwd(
        dh1, h0, f1, a1, b1, ffn1_pre_norm, ffn1_post_norm, wgt1, wut1, wd1, "ffn1_bwd",
        xchg=_Xchg([chunked(dwout), chunked(dwin)], ["scatter"] * 2))

    g["ssm_c_re"] = jnp.swapaxes(_block_diag_t(jnp.swapaxes(dcm_re, 1, 2), n_grp), 1, 2)[None]
    g["ssm_c_im"] = jnp.swapaxes(_block_diag_t(jnp.swapaxes(dcm_im, 1, 2), n_grp), 1, 2)[None]
    g["ssm_w_glu"] = jnp.concatenate([_block_diag_t(dwz1, n_grp), _block_diag_t(dwz2, n_grp)], axis=-1)[None]
    dbb_re = jnp.swapaxes(_block_diag_t(dbm_re, n_grp), 1, 2)
    dbb_im = jnp.swapaxes(_block_diag_t(dbm_im, n_grp), 1, 2)
    acc_re, acc_im = acc_re.reshape(a_re.shape), acc_im.reshape(a_re.shape)
    norm = a_re * a_re + a_im * a_im
    da_re = (acc_re * a_re - acc_im * a_im) / norm
    da_im = (acc_re * a_im + acc_im * a_re) / norm
    _, disc_vjp = jax.vjp(_discretize, ssm_lambda_re[0], ssm_lambda_im[0], ssm_log_dt[0], ssm_b_re[0], ssm_b_im[0])
    d_lre, d_lim, d_ldt, d_bre, d_bim = disc_vjp((da_re, da_im, dbb_re, dbb_im))
    g["ssm_lambda_re"], g["ssm_lambda_im"], g["ssm_log_dt"] = d_lre[None], d_lim[None], d_ldt[None]
    g["ssm_b_re"], g["ssm_b_im"] = d_bre[None], d_bim[None]
    g["pool_w"] = dpw[None]

    small_parts = [g[n] for n in small] + [dh0[:N_META], sq_err[:, :1]]
    small_vec = _pack_rows(small_parts, 1024, SUBLANES)
    dgate1, (recv_small,) = _wgrad(da1, nf1, "ffn1_dgate", xchg=_Xchg([small_vec], ["gather"]))
    dup1, (slots["ffn1_w_gate"],) = _wgrad(db1, nf1, "ffn1_dup", xchg=_Xchg([chunked(dgate1)], ["scatter"]))
    ddown1, (slots["ffn1_w_up"],) = _wgrad(s1, df1, "ffn1_ddown", xchg=_Xchg([chunked(dup1)], ["scatter"]))
    (slots["ffn1_w_down"],) = _exchange([chunked(ddown1)], ["scatter"], "reduce_last")

    g_small = _sum_slots(recv_small, "sum_small_grads")
    summed = _unpack(g_small, [g[n].shape for n in small] + [(N_META, d), (1,)])
    for n, val in zip(small, summed):
        g[n] = val
    g["meta_tokens"] = lax.dynamic_slice_in_dim(summed[-2], _my_slot() * (d // N_DEV), d // N_DEV, axis=1)
    loss = (0.5 / d) * summed[-1][0]

    delta, new_m, new_v = {}, {}, {}
    for n in sharded:
        shape = args[n].shape
        two_d = (-1, shape[-1])
        w2, m2, v2 = args[n].reshape(two_d), args["m_" + n].reshape(two_d), args["v_" + n].reshape(two_d)
        if n == "meta_tokens":
            dl, mn, vn = _adamw(w2, g[n], m2, v2, "adamw_" + n)
        elif n.endswith("gate") or n.endswith("up"):
            g[n] = _sum_slots(slots[n], "sum_" + n).T[None]
            dl, mn, vn = _adamw(w2, g[n][0], m2, v2, "adamw_" + n)
        else:
            gs, dl, mn, vn = _adamw_slots(w2, slots[n], m2, v2, "adamw_" + n)
            g[n] = gs[None]
        delta[n], new_m[n], new_v[n] = dl.reshape(shape), mn.reshape(shape), vn.reshape(shape)
    packs = [_pack_rows([src[n] if pre is None else args[pre + n] for n in small], 1024, SUBLANES)
             for src, pre in ((args, None), (g, None), (None, "m_"), (None, "v_"))]
    outs = _adamw(*packs, "adamw_small")
    shapes = [args[n].shape for n in small]
    for store, flat in zip((delta, new_m, new_v), outs):
        for n, val in zip(small, _unpack(flat, shapes)):
            store[n] = val

    grad_x = dh0[N_META:][None]
    return (loss, grad_x, *[g[n] for n in names], *[delta[n] for n in names], *[new_m[n] for n in names],
            *[new_v[n] for n in names])
```

```python
import functools
import math

import jax
import jax.numpy as jnp
import numpy as np
from jax import lax
from jax.experimental import pallas as pl
from jax.experimental.pallas import tpu as pltpu

F32 = jnp.float32
_MXU = jnp.bfloat16
_ACT = jnp.bfloat16
_WIRE = jnp.bfloat16

N_DEV = 8
N_META = 16
RMS_EPS = 1e-6
SSM_GROUP_CH = 16
SSM_STATE = 64
LANES = 128
SUBLANES = 8
POOL_WINDOWS = (2, 4, 8, 16)
POOL_HALO = 16
ADAM_LR = 0.001
ADAM_B1 = 0.9
ADAM_B2 = 0.999
ADAM_EPS = 1e-08
ADAM_WD = 0.01
ADAM_STEP = 10
GELU_C0 = math.sqrt(2.0 / math.pi)
GELU_C1 = 0.044715
VMEM_LIMIT = 56 * 1024 * 1024

_NT = (((1,), (1,)), ((), ()))
_TN = (((0,), (0,)), ((), ()))


def _dot(a, b):
    return jnp.dot(a, b, preferred_element_type=F32)


def _dot_nt(a, b):
    return lax.dot_general(a, b, _NT, preferred_element_type=F32)


def _dot_tn(a, b):
    return lax.dot_general(a, b, _TN, preferred_element_type=F32)


def _rs(x):
    return lax.rsqrt(jnp.mean(x * x, axis=-1, keepdims=True) + RMS_EPS)


def _sigmoid(x):
    return 1.0 / (1.0 + jnp.exp(-x))


def _row_tile(n_rows, largest=432):
    for t in (432, 304, 48, 16):
        if t <= largest and n_rows % t == 0:
            return t
    raise ValueError(n_rows)


def _ff_chunk(d_ff):
    return d_ff // 2 if (d_ff // 2) % LANES == 0 else d_ff


def _params(n_axes=1):
    return pltpu.CompilerParams(dimension_semantics=("arbitrary",) * n_axes, vmem_limit_bytes=VMEM_LIMIT)


def _resident():
    return pl.BlockSpec(memory_space=pltpu.VMEM)


def _full(shape):
    nd = len(shape)
    return pl.BlockSpec(shape, lambda *_: (0,) * nd)


PEER_ORDER = (1, 2, 4, 3, 5, 6, 7)


def _split(refs, counts):
    out, pos = [], 0
    for n in counts:
        out.append(refs[pos:pos + n])
        pos += n
    return out


def _peer(r):
    x, y, c = lax.axis_index("x"), lax.axis_index("y"), lax.axis_index("c")
    return (1 - x if r & 4 else x, 1 - y if r & 2 else y, 1 - c if r & 1 else c)


def _my_slot():
    return 4 * lax.axis_index("x") + 2 * lax.axis_index("y") + lax.axis_index("c")


class _Xchg:
    def __init__(self, srcs, kinds):
        self.srcs, self.kinds, self.n = list(srcs), list(kinds), len(srcs)
        self.out_shape = [jax.ShapeDtypeStruct((N_DEV,) + s.shape if k == "gather" else s.shape, s.dtype)
                          for s, k in zip(self.srcs, self.kinds)]
        self.specs = [pl.BlockSpec(memory_space=pl.ANY)] * self.n
        self.scratch = [pltpu.SemaphoreType.DMA((self.n * (N_DEV - 1),)), pltpu.SemaphoreType.DMA((self.n * (N_DEV - 1),)),
                        pltpu.SemaphoreType.DMA((self.n,))]

    def copies(self, src, dst, sems):
        send_sems, recv_sems, local_sems = sems
        me = _my_slot()
        out = []
        for a in range(self.n):
            mine = src[a] if self.kinds[a] == "gather" else src[a].at[me]
            out.append(pltpu.make_async_copy(mine, dst[a].at[me], local_sems.at[a]))
            for r in PEER_ORDER:
                px, py, pc = _peer(r)
                part = src[a] if self.kinds[a] == "gather" else src[a].at[4 * px + 2 * py + pc]
                k = a * (N_DEV - 1) + r - 1
                out.append(pltpu.make_async_remote_copy(
                    src_ref=part, dst_ref=dst[a].at[me], send_sem=send_sems.at[k], recv_sem=recv_sems.at[k],
                    device_id=(px, py, pc), device_id_type=pl.DeviceIdType.MESH))
        return out

    def start(self, src, dst, sems):
        for cp in self.copies(src, dst, sems):
            cp.start()

    def wait(self, src, dst, sems):
        for cp in self.copies(src, dst, sems):
            cp.wait()


class _NoXchg:
    n, srcs, out_shape, specs, scratch = 0, [], [], [], []

    def start(self, *_):
        pass

    wait = start


def _call(body, args, *, name, grid, out_shape, in_specs, out_specs, scratch_shapes=(), xchg=None):
    xc = xchg or _NoXchg()
    counts = (len(in_specs), xc.n, len(out_shape), xc.n, len(scratch_shapes), len(xc.scratch))

    def wrapped(*refs):
        ins, xsrc, outs, xdst, scr, sems = _split(refs, counts)
        ids = [pl.program_id(k) for k in range(len(grid))]
        if xc.n:
            @pl.when(functools.reduce(jnp.logical_and, [i == 0 for i in ids]))
            def _():
                xc.start(xsrc, xdst, sems)

        body(*ins, *outs, *scr)
        if xc.n:
            @pl.when(functools.reduce(jnp.logical_and, [i == g - 1 for i, g in zip(ids, grid)]))
            def _():
                xc.wait(xsrc, xdst, sems)

    res = pl.pallas_call(
        wrapped, name=name, grid=grid, out_shape=list(out_shape) + xc.out_shape,
        in_specs=list(in_specs) + xc.specs, out_specs=list(out_specs) + xc.specs,
        scratch_shapes=list(scratch_shapes) + xc.scratch, compiler_params=_params(len(grid)),
    )(*args, *xc.srcs)
    return res[:len(out_shape)], res[len(out_shape):]


def _exchange(srcs, kinds, name):
    xc = _Xchg(srcs, kinds)

    def body(*refs):
        src, dst, sems = _split(refs, (xc.n, xc.n, 3))
        xc.start(src, dst, sems)
        xc.wait(src, dst, sems)

    return pl.pallas_call(body, name=name, out_shape=xc.out_shape, in_specs=xc.specs, out_specs=xc.specs,
                          scratch_shapes=xc.scratch)(*srcs)


def _gather_two_level(srcs, name):
    n = len(srcs)
    out_shape = [jax.ShapeDtypeStruct((N_DEV,) + s.shape, s.dtype) for s in srcs]
    chips = (2, 4, 6)

    def body(*refs):
        src, dst, (send_sems, recv_sems, local_sems) = _split(refs, (n, n, 3))
        x, y, c = lax.axis_index("x"), lax.axis_index("y"), lax.axis_index("c")
        me = 4 * x + 2 * y + c
        sibling = (x, y, 1 - c)

        def copy(a, k, slot, to, from_src=False):
            return pltpu.make_async_remote_copy(
                src_ref=src[a] if from_src else dst[a].at[slot], dst_ref=dst[a].at[slot],
                send_sem=send_sems.at[a * 7 + k], recv_sem=recv_sems.at[a * 7 + k],
                device_id=to, device_id_type=pl.DeviceIdType.MESH)

        def slot_of(r, core):
            px, py, _ = _peer(r)
            return 4 * px + 2 * py + core

        local = [pltpu.make_async_copy(src[a], dst[a].at[me], local_sems.at[a]) for a in range(n)]
        sent = []
        for a in range(n):
            local[a].start()
            sent.append(copy(a, 0, me, sibling, from_src=True))
            sent += [copy(a, 1 + j, me, _peer(r), from_src=True) for j, r in enumerate(chips)]
        for cp in sent:
            cp.start()
        for j, r in enumerate(chips):
            for a in range(n):
                copy(a, 1 + j, slot_of(r, c), _peer(r)).wait_recv()
                cp = copy(a, 4 + j, slot_of(r, c), sibling)
                cp.start()
                sent.append(cp)
        for a in range(n):
            copy(a, 0, slot_of(0, 1 - c), sibling).wait_recv()
            for j, r in enumerate(chips):
                copy(a, 4 + j, slot_of(r, 1 - c), sibling).wait_recv()
        for cp in local:
            cp.wait()
        for cp in sent:
            cp.wait_send()

    any_spec = pl.BlockSpec(memory_space=pl.ANY)
    return pl.pallas_call(
        body, name=name, out_shape=out_shape, in_specs=[any_spec] * n, out_specs=[any_spec] * n,
        scratch_shapes=[pltpu.SemaphoreType.DMA((n * 7,)), pltpu.SemaphoreType.DMA((n * 7,)), pltpu.SemaphoreType.DMA((n,))],
    )(*srcs)


def _sum_slots(r, name):
    _, rows, cols = r.shape
    blk = rows
    for cand in (rows, 592, 512, 256, 128, 64, 32, 16):
        if rows % cand == 0 and N_DEV * cand * cols * r.dtype.itemsize <= 8 * 1024 * 1024:
            blk = cand
            break

    def body(r_ref, o_ref):
        acc = r_ref[0].astype(F32)
        for d in range(1, N_DEV):
            acc = acc + r_ref[d].astype(F32)
        o_ref[...] = acc

    return pl.pallas_call(
        body, name=name, grid=(rows // blk,), out_shape=jax.ShapeDtypeStruct((rows, cols), F32),
        in_specs=[pl.BlockSpec((N_DEV, blk, cols), lambda i: (0, i, 0))],
        out_specs=pl.BlockSpec((blk, cols), lambda i: (i, 0)), compiler_params=_params(),
    )(r)


def _ffn_fwd(h, g_pre, g_post, wgt, wut, wd, name, xchg=None):
    n_rows, d = h.shape
    d_ff = wd.shape[0]
    tm, fc = _row_tile(n_rows), _ff_chunk(d_ff)
    n_t, n_c = n_rows // tm, d_ff // fc

    def body(h_ref, gpre_ref, gpost_ref, wgt_ref, wut_ref, wd_ref, a_ref, b_ref, f_ref, ho_ref, n_scr, acc):
        c = pl.program_id(1)

        @pl.when(c == 0)
        def _():
            hv = h_ref[...]
            n_scr[...] = (hv * _rs(hv) * gpre_ref[...]).astype(_MXU)
            acc[...] = jnp.zeros_like(acc)

        rows = pl.ds(pl.multiple_of(c * fc, fc), fc)
        nv = n_scr[...]
        a = _dot_nt(nv, wgt_ref[rows, :])
        b = _dot_nt(nv, wut_ref[rows, :])
        a_ref[...] = a.astype(_ACT)
        b_ref[...] = b.astype(_ACT)
        s = a * _sigmoid(a) * b
        acc[...] += _dot(s.astype(_MXU), wd_ref[rows, :])

        @pl.when(c == n_c - 1)
        def _():
            f = acc[...]
            f_ref[...] = f
            ho_ref[...] = h_ref[...] + 0.5 * (f * _rs(f) * gpost_ref[...])

    row = pl.BlockSpec((tm, d), lambda i, c: (i, 0))
    chunk = pl.BlockSpec((tm, fc), lambda i, c: (i, c))
    return _call(
        body, (h, g_pre, g_post, wgt, wut, wd), name=name, grid=(n_t, n_c),
        out_shape=[jax.ShapeDtypeStruct((n_rows, d_ff), _ACT), jax.ShapeDtypeStruct((n_rows, d_ff), _ACT),
                   jax.ShapeDtypeStruct((n_rows, d), F32), jax.ShapeDtypeStruct((n_rows, d), F32)],
        in_specs=[row, _full((1, d)), _full((1, d)), _resident(), _resident(), _resident()],
        out_specs=[chunk, chunk, row, row],
        scratch_shapes=[pltpu.VMEM((tm, d), _MXU), pltpu.VMEM((tm, d), F32)], xchg=xchg)


def _ffn_bwd_hidden(dho, h, f, a, b, g_pre, g_post, wd, name, xchg=None):
    n_rows, d = h.shape
    d_ff = wd.shape[0]
    tm, fc = _row_tile(n_rows), _ff_chunk(d_ff)
    n_t, n_c = n_rows // tm, d_ff // fc

    def body(dho_ref, h_ref, f_ref, a_ref, b_ref, gpre_ref, gpost_ref, wd_ref,
             da_ref, db_ref, s_ref, df_ref, n_ref, dgpost_ref):
        i, c = pl.program_id(0), pl.program_id(1)

        @pl.when((i == 0) & (c == 0))
        def _():
            dgpost_ref[...] = jnp.zeros_like(dgpost_ref)

        @pl.when(c == 0)
        def _():
            fv = f_ref[...]
            rf = _rs(fv)
            fhat = fv * rf
            dy = 0.5 * dho_ref[...]
            dgpost_ref[...] += jnp.sum(dy * fhat, axis=0, keepdims=True)
            dfhat = dy * gpost_ref[...]
            df = rf * (dfhat - fhat * jnp.mean(dfhat * fhat, axis=-1, keepdims=True))
            df_ref[...] = df.astype(_MXU)
            hv = h_ref[...]
            n_ref[...] = (hv * _rs(hv) * gpre_ref[...]).astype(_MXU)

        rows = pl.ds(pl.multiple_of(c * fc, fc), fc)
        ds = _dot_nt(df_ref[...], wd_ref[rows, :])
        av = a_ref[...].astype(F32)
        bv = b_ref[...].astype(F32)
        sg = _sigmoid(av)
        si = av * sg
        da_ref[...] = (ds * bv * (sg * (1.0 + av * (1.0 - sg)))).astype(_MXU)
        db_ref[...] = (ds * si).astype(_MXU)
        s_ref[...] = (si * bv).astype(_MXU)

    row = pl.BlockSpec((tm, d), lambda i, c: (i, 0))
    chunk = pl.BlockSpec((tm, fc), lambda i, c: (i, c))
    vec = pl.BlockSpec((1, d), lambda i, c: (0, 0))
    return _call(
        body, (dho, h, f, a, b, g_pre, g_post, wd), name=name, grid=(n_t, n_c),
        out_shape=[jax.ShapeDtypeStruct((n_rows, d_ff), _MXU)] * 3 + [jax.ShapeDtypeStruct((n_rows, d), _MXU)] * 2
        + [jax.ShapeDtypeStruct((1, d), F32)],
        in_specs=[row, row, row, chunk, chunk, vec, vec, _resident()],
        out_specs=[chunk, chunk, chunk, row, row, vec], xchg=xchg)


def _ffn_bwd_input(dho, h, da, db, g_pre, wgt, wut, name, xchg=None):
    n_rows, d = h.shape
    d_ff = wgt.shape[0]
    tm, fc = _row_tile(n_rows), _ff_chunk(d_ff)
    n_t, n_c = n_rows // tm, d_ff // fc

    def body(dho_ref, h_ref, da_ref, db_ref, gpre_ref, wgt_ref, wut_ref, dh_ref, dgpre_ref, dn_acc):
        i, c = pl.program_id(0), pl.program_id(1)

        @pl.when((i == 0) & (c == 0))
        def _():
            dgpre_ref[...] = jnp.zeros_like(dgpre_ref)

        rows = pl.ds(pl.multiple_of(c * fc, fc), fc)
        part = _dot(da_ref[...], wgt_ref[rows, :]) + _dot(db_ref[...], wut_ref[rows, :])

        @pl.when(c == 0)
        def _():
            dn_acc[...] = part

        @pl.when(c > 0)
        def _():
            dn_acc[...] += part

        @pl.when(c == n_c - 1)
        def _():
            dn = dn_acc[...]
            hv = h_ref[...]
            r = _rs(hv)
            hhat = hv * r
            dgpre_ref[...] += jnp.sum(dn * hhat, axis=0, keepdims=True)
            dhh = dn * gpre_ref[...]
            dh_ref[...] = dho_ref[...] + r * (dhh - hhat * jnp.mean(dhh * hhat, axis=-1, keepdims=True))

    row = pl.BlockSpec((tm, d), lambda i, c: (i, 0))
    chunk = pl.BlockSpec((tm, fc), lambda i, c: (i, c))
    vec = pl.BlockSpec((1, d), lambda i, c: (0, 0))
    return _call(
        body, (dho, h, da, db, g_pre, wgt, wut), name=name, grid=(n_t, n_c),
        out_shape=[jax.ShapeDtypeStruct((n_rows, d), F32), jax.ShapeDtypeStruct((1, d), F32)],
        in_specs=[row, row, chunk, chunk, vec, _resident(), _resident()], out_specs=[row, vec],
        scratch_shapes=[pltpu.VMEM((tm, d), F32)], xchg=xchg)


def _wgrad(xm, ym, name, xchg=None):
    n_rows, a_dim = xm.shape
    b_dim = ym.shape[1]
    tk = n_rows
    for cand in (2736, 1296, 432, 48, 16):
        if n_rows % cand == 0:
            tk = cand
            break
    ta = a_dim
    for cand in (1408, 1024, 512):
        if a_dim % cand == 0:
            ta = cand
            break

    n_k = n_rows // tk

    def body(x_ref, y_ref, o_ref, acc):
        k = pl.program_id(1)

        @pl.when(k == 0)
        def _():
            acc[...] = jnp.zeros_like(acc)

        acc[...] += _dot_tn(x_ref[...], y_ref[...])

        @pl.when(k == n_k - 1)
        def _():
            o_ref[...] = acc[...].astype(o_ref.dtype)

    (out,), extra = _call(
        body, (xm, ym), name=name, grid=(a_dim // ta, n_k), out_shape=[jax.ShapeDtypeStruct((a_dim, b_dim), _WIRE)],
        in_specs=[pl.BlockSpec((tk, ta), lambda j, k: (k, j)), pl.BlockSpec((tk, b_dim), lambda j, k: (k, 0))],
        out_specs=[pl.BlockSpec((ta, b_dim), lambda j, k: (j, 0))], scratch_shapes=[pltpu.VMEM((ta, b_dim), F32)],
        xchg=xchg)
    return out, extra


def _mix_in_fwd(h, g, w_in, name):
    n_rows, d = h.shape
    tm = _row_tile(n_rows)

    width = w_in.shape[1] // 2

    def body(h_ref, g_ref, w_ref, n_ref, us_ref, up_ref):
        hv = h_ref[...]
        nv = (hv * _rs(hv) * g_ref[...]).astype(_MXU)
        n_ref[...] = nv
        p = _dot(nv, w_ref[...])
        us_ref[...] = p[:, :width]
        up_ref[...] = p[:, width:]

    row = pl.BlockSpec((tm, d), lambda i: (i, 0))
    half = pl.BlockSpec((tm, width), lambda i: (i, 0))
    return pl.pallas_call(
        body, name=name, grid=(n_rows // tm,),
        out_shape=[jax.ShapeDtypeStruct((n_rows, d), _MXU)] + [jax.ShapeDtypeStruct((n_rows, width), F32)] * 2,
        in_specs=[row, _full((1, d)), _resident()], out_specs=[row, half, half], compiler_params=_params(),
    )(h, g, w_in)


def _gelu_parts(y):
    th = jnp.tanh(GELU_C0 * (y + GELU_C1 * (y * y * y)))
    return 0.5 * (1.0 + th), th


def _scan_blocks(n_blocks, width, coef_ref, load, store, reverse):
    n_cols = coef_ref.shape[2] // width
    for cb in range(n_cols):
        cols = pl.ds(cb * width, width)

        def step(t, carry, cols=cols):
            tb = (n_blocks - 1 - t) if reverse else t
            r0 = pl.multiple_of(tb * SUBLANES, SUBLANES)
            vr, vi = load(r0, cols)
            xr, xi = vr, vi
            for lvl, k in enumerate((1, 2, 4)):
                kr, ki = coef_ref[2 * lvl, :, cols], coef_ref[2 * lvl + 1, :, cols]
                shift = SUBLANES - k if reverse else k
                sr, si = pltpu.roll(xr, shift, 0), pltpu.roll(xi, shift, 0)
                xr, xi = xr + (kr * sr - ki * si), xi + (kr * si + ki * sr)
            pr, pi = coef_ref[6, :, cols], coef_ref[7, :, cols]
            cr, ci = carry[0], carry[1]
            xr, xi = xr + (pr * cr - pi * ci), xi + (pr * ci + pi * cr)
            extra = store(r0, cols, xr, xi, vr, vi, carry[2:])
            edge = 0 if reverse else SUBLANES - 1
            return (xr[edge:edge + 1, :], xi[edge:edge + 1, :]) + tuple(extra)

        yield cb, cols, step


def _ssm_fwd(proj, coef, b_re, b_im, c_re, c_im, wz1, wz2, d_skip, g_out, name, xchg=None):
    n_rows = proj.shape[0]
    n_blk, _, n_state = b_re.shape
    width = n_blk * LANES
    tm = _row_tile(n_rows)
    n_tb = tm // SUBLANES

    def body(u_ref, coef_ref, bre_ref, bim_ref, cre_ref, cim_ref, wz1_ref, wz2_ref, d_ref, g_ref,
             xre_ref, xim_ref, o_ref, car_re, car_im, out_scr):
        @pl.when(pl.program_id(0) == 0)
        def _():
            car_re[...] = jnp.zeros_like(car_re)
            car_im[...] = jnp.zeros_like(car_im)

        ub = u_ref[...].astype(_MXU)
        for j in range(n_blk):
            uj = ub[:, j * LANES:(j + 1) * LANES]
            xre_ref[:, j * n_state:(j + 1) * n_state] = _dot(uj, bre_ref[j])
            xim_ref[:, j * n_state:(j + 1) * n_state] = _dot(uj, bim_ref[j])

        def load(r0, cols):
            return xre_ref[pl.ds(r0, SUBLANES), cols], xim_ref[pl.ds(r0, SUBLANES), cols]

        def store(r0, cols, xr, xi, vr, vi, extra):
            xre_ref[pl.ds(r0, SUBLANES), cols] = xr
            xim_ref[pl.ds(r0, SUBLANES), cols] = xi
            return ()

        for cb, cols, step in _scan_blocks(n_tb, 512, coef_ref, load, store, reverse=False):
            fin = lax.fori_loop(0, n_tb, step, (car_re[:, cols], car_im[:, cols]))
            car_re[:, cols] = fin[0]
            car_im[:, cols] = fin[1]

        ssq = None
        for j in range(n_blk):
            sl = slice(j * LANES, (j + 1) * LANES)
            st = slice(j * n_state, (j + 1) * n_state)
            yc = _dot(xre_ref[:, st].astype(_MXU), cre_ref[j]) - _dot(xim_ref[:, st].astype(_MXU), cim_ref[j])
            y = yc + d_ref[:, sl] * u_ref[:, sl]
            cdf, _ = _gelu_parts(y)
            gy = (y * cdf).astype(_MXU)
            out = _dot(gy, wz1_ref[j]) * _sigmoid(_dot(gy, wz2_ref[j]))
            out_scr[:, sl] = out
            part = jnp.sum(out * out, axis=-1, keepdims=True)
            ssq = part if ssq is None else ssq + part
        r = lax.rsqrt(ssq / width + RMS_EPS)
        o_ref[...] = (out_scr[...] * r * g_ref[...]).astype(_MXU)

    half = pl.BlockSpec((tm, width), lambda i: (i, 0))
    state = pl.BlockSpec((tm, n_blk * n_state), lambda i: (i, 0))
    return _call(
        body, (proj, coef, b_re, b_im, c_re, c_im, wz1, wz2, d_skip, g_out), name=name, grid=(n_rows // tm,),
        out_shape=[jax.ShapeDtypeStruct((n_rows, n_blk * n_state), F32)] * 2 + [jax.ShapeDtypeStruct((n_rows, width), _MXU)],
        in_specs=[half, _full(coef.shape), _full(b_re.shape), _full(b_im.shape), _full(c_re.shape), _full(c_im.shape),
                  _full(wz1.shape), _full(wz2.shape), _full((1, width)), _full((1, width))],
        out_specs=[state, state, half],
        scratch_shapes=[pltpu.VMEM((1, n_blk * n_state), F32)] * 2 + [pltpu.VMEM((tm, width), F32)], xchg=xchg)


def _ssm_bwd(proj, x_re, x_im, dcat, coef_rev, b_re, b_im, c_re, c_im, wz1, wz2, d_skip, g_out, name, xchg=None):
    n_rows = proj.shape[0]
    n_blk, _, n_state = b_re.shape
    width = n_blk * LANES
    n_all = n_blk * n_state
    tm = _row_tile(n_rows)
    n_t, n_tb = n_rows // tm, tm // SUBLANES

    def body(u_ref, xre_ref, xim_ref, dc_ref, coef_ref, bre_ref, bim_ref, cre_ref, cim_ref, wz1_ref, wz2_ref, d_ref, g_ref,
             du_ref, dg_ref, dd_ref, dwz1_ref, dwz2_ref, dcre_ref, dcim_ref, dbre_ref, dbim_ref, are_ref, aim_ref,
             car_re, car_im, gre, gim, y_s, z1_s, sg_s, out_s, gy_s):
        @pl.when(pl.program_id(0) == 0)
        def _():
            for ref in (dg_ref, dd_ref, dwz1_ref, dwz2_ref, dcre_ref, dcim_ref, dbre_ref, dbim_ref, are_ref, aim_ref,
                        car_re, car_im):
                ref[...] = jnp.zeros_like(ref)

        ssq = None
        for j in range(n_blk):
            sl = slice(j * LANES, (j + 1) * LANES)
            st = slice(j * n_state, (j + 1) * n_state)
            yc = _dot(xre_ref[:, st].astype(_MXU), cre_ref[j]) - _dot(xim_ref[:, st].astype(_MXU), cim_ref[j])
            y = yc + d_ref[:, sl] * u_ref[:, sl]
            cdf, _ = _gelu_parts(y)
            gy = (y * cdf).astype(_MXU)
            z1 = _dot(gy, wz1_ref[j])
            sg = _sigmoid(_dot(gy, wz2_ref[j]))
            out = z1 * sg
            y_s[:, sl], z1_s[:, sl], sg_s[:, sl], out_s[:, sl], gy_s[:, sl] = y, z1, sg, out, gy
            part = jnp.sum(out * out, axis=-1, keepdims=True)
            ssq = part if ssq is None else ssq + part
        r = lax.rsqrt(ssq / width + RMS_EPS)
        ohat = out_s[...] * r
        dcv = dc_ref[...]
        dg_ref[...] += jnp.sum(dcv * ohat, axis=0, keepdims=True)
        doh = dcv * g_ref[...]
        out_s[...] = r * (doh - ohat * (jnp.sum(doh * ohat, axis=-1, keepdims=True) / width))

        for j in range(n_blk):
            sl = slice(j * LANES, (j + 1) * LANES)
            st = slice(j * n_state, (j + 1) * n_state)
            dout, sg, z1, y = out_s[:, sl], sg_s[:, sl], z1_s[:, sl], y_s[:, sl]
            dz1 = (dout * sg).astype(_MXU)
            dz2 = (dout * z1 * sg * (1.0 - sg)).astype(_MXU)
            gy = gy_s[:, sl]
            dwz1_ref[j] += _dot_tn(gy, dz1)
            dwz2_ref[j] += _dot_tn(gy, dz2)
            dgy = _dot_nt(dz1, wz1_ref[j]) + _dot_nt(dz2, wz2_ref[j])
            cdf, th = _gelu_parts(y)
            dy = dgy * (cdf + y * (0.5 * (1.0 - th * th) * GELU_C0 * (1.0 + 3.0 * GELU_C1 * (y * y))))
            uj = u_ref[:, sl]
            dd_ref[:, sl] += jnp.sum(dy * uj, axis=0, keepdims=True)
            du_ref[:, sl] = d_ref[:, sl] * dy
            dyb = dy.astype(_MXU)
            dcre_ref[j] += _dot_tn(dyb, xre_ref[:, st].astype(_MXU))
            dcim_ref[j] -= _dot_tn(dyb, xim_ref[:, st].astype(_MXU))
            gre[:, st] = _dot_nt(dyb, cre_ref[j])
            gim[:, st] = -_dot_nt(dyb, cim_ref[j])

        def load(r0, cols):
            return gre[pl.ds(r0, SUBLANES), cols], gim[pl.ds(r0, SUBLANES), cols]

        def store(r0, cols, gr, gi, vr, vi, extra):
            gre[pl.ds(r0, SUBLANES), cols] = gr
            gim[pl.ds(r0, SUBLANES), cols] = gi
            hr, hi = gr - vr, gi - vi
            xr, xi = xre_ref[pl.ds(r0, SUBLANES), cols], xim_ref[pl.ds(r0, SUBLANES), cols]
            return extra[0] + (xr * hr + xi * hi), extra[1] + (xr * hi - xi * hr)

        for cb, cols, step in _scan_blocks(n_tb, 512, coef_ref, load, store, reverse=True):
            zero = jnp.zeros((SUBLANES, 512), F32)
            fin = lax.fori_loop(0, n_tb, step, (car_re[:, cols], car_im[:, cols], zero, zero))
            car_re[:, cols] = fin[0]
            car_im[:, cols] = fin[1]
            are_ref[:, cols] += jnp.sum(fin[2], axis=0, keepdims=True)
            aim_ref[:, cols] += jnp.sum(fin[3], axis=0, keepdims=True)

        for j in range(n_blk):
            sl = slice(j * LANES, (j + 1) * LANES)
            st = slice(j * n_state, (j + 1) * n_state)
            ujb = u_ref[:, sl].astype(_MXU)
            grb, gib = gre[:, st].astype(_MXU), gim[:, st].astype(_MXU)
            dbre_ref[j] += _dot_tn(ujb, grb)
            dbim_ref[j] += _dot_tn(ujb, gib)
            du_ref[:, sl] += _dot_nt(grb, bre_ref[j]) + _dot_nt(gib, bim_ref[j])

    half = pl.BlockSpec((tm, width), lambda i: (n_t - 1 - i, 0))
    state = pl.BlockSpec((tm, n_all), lambda i: (n_t - 1 - i, 0))
    small = [(1, width), (1, width), wz1.shape, wz2.shape, (n_blk, LANES, n_state), (n_blk, LANES, n_state),
             (n_blk, LANES, n_state), (n_blk, LANES, n_state), (1, n_all), (1, n_all)]
    return _call(
        body, (proj, x_re, x_im, dcat, coef_rev, b_re, b_im, c_re, c_im, wz1, wz2, d_skip, g_out), name=name, grid=(n_t,),
        out_shape=[jax.ShapeDtypeStruct((n_rows, width), F32)] + [jax.ShapeDtypeStruct(s, F32) for s in small],
        in_specs=[half, state, state, half, _full(coef_rev.shape), _full(b_re.shape), _full(b_im.shape), _full(c_re.shape),
                  _full(c_im.shape), _full(wz1.shape), _full(wz2.shape), _full((1, width)), _full((1, width))],
        out_specs=[half] + [_full(s) for s in small],
        scratch_shapes=[pltpu.VMEM((1, n_all), F32)] * 2 + [pltpu.VMEM((tm, n_all), F32)] * 2
        + [pltpu.VMEM((tm, width), F32)] * 4 + [pltpu.VMEM((tm, width), _MXU)], xchg=xchg)


def _pool_counts(tile, tm, window):
    t = tile * tm + lax.broadcasted_iota(jnp.int32, (tm, 1), 0)
    return jnp.minimum(t + 1, window).astype(F32)


def _pool_fwd(proj, pool_w, scale, g_out, name):
    n_rows = proj.shape[0]
    n_grp, grp, _ = pool_w.shape
    width = n_grp * grp
    tm = _row_tile(n_rows)

    def body(u_ref, pw_ref, sc_ref, g_ref, o_ref, ext, y_s):
        i = pl.program_id(0)

        @pl.when(i == 0)
        def _():
            ext[0:POOL_HALO, :] = jnp.zeros((POOL_HALO, width), F32)

        ext[POOL_HALO:, :] = u_ref[...]
        ssq = None
        for gi, w in enumerate(POOL_WINDOWS):
            sl = slice(gi * grp, (gi + 1) * grp)
            tot = ext[POOL_HALO:, sl]
            for k in range(1, w):
                tot = tot + ext[POOL_HALO - k:POOL_HALO - k + tm, sl]
            pooled = tot / _pool_counts(i, tm, w) - u_ref[:, sl]
            y = _dot(pooled.astype(_MXU), pw_ref[gi]) * sc_ref[:, sl]
            y_s[:, sl] = y
            part = jnp.sum(y * y, axis=-1, keepdims=True)
            ssq = part if ssq is None else ssq + part
        r = lax.rsqrt(ssq / width + RMS_EPS)
        o_ref[...] = (y_s[...] * r * g_ref[...]).astype(_MXU)
        ext[0:POOL_HALO, :] = u_ref[tm - POOL_HALO:, :]

    half_in = pl.BlockSpec((tm, width), lambda i: (i, 0))
    half = pl.BlockSpec((tm, width), lambda i: (i, 0))
    return pl.pallas_call(
        body, name=name, grid=(n_rows // tm,), out_shape=jax.ShapeDtypeStruct((n_rows, width), _MXU),
        in_specs=[half_in, _full(pool_w.shape), _full((1, width)), _full((1, width))], out_specs=half,
        scratch_shapes=[pltpu.VMEM((tm + POOL_HALO, width), F32), pltpu.VMEM((tm, width), F32)],
        compiler_params=_params(),
    )(proj, pool_w, scale, g_out)


def _pool_bwd(proj, dcat, pool_w, scale, g_out, name):
    n_rows = proj.shape[0]
    n_grp, grp, _ = pool_w.shape
    width = n_grp * grp
    tm = _row_tile(n_rows)
    n_t = n_rows // tm
    halo_blocks = tm // POOL_HALO

    def body(u_ref, up_ref, dc_ref, pw_ref, sc_ref, g_ref, du_ref, dg_ref, dsc_ref, dpw_ref, ext, qext, y_s, pl_s):
        i = pl.program_id(0)
        tile = n_t - 1 - i

        @pl.when(i == 0)
        def _():
            for ref in (dg_ref, dsc_ref, dpw_ref):
                ref[...] = jnp.zeros_like(ref)
            qext[tm:, :] = jnp.zeros((POOL_HALO, width), F32)

        ext[0:POOL_HALO, :] = jnp.where(tile > 0, up_ref[...], 0.0)
        ext[POOL_HALO:, :] = u_ref[...]
        ssq = None
        for gi, w in enumerate(POOL_WINDOWS):
            sl = slice(gi * grp, (gi + 1) * grp)
            tot = ext[POOL_HALO:, sl]
            for k in range(1, w):
                tot = tot + ext[POOL_HALO - k:POOL_HALO - k + tm, sl]
            pooled = (tot / _pool_counts(tile, tm, w) - u_ref[:, sl]).astype(_MXU)
            pl_s[:, sl] = pooled
            y0 = _dot(pooled, pw_ref[gi])
            y_s[:, sl] = y0
            y = y0 * sc_ref[:, sl]
            part = jnp.sum(y * y, axis=-1, keepdims=True)
            ssq = part if ssq is None else ssq + part
        r = lax.rsqrt(ssq / width + RMS_EPS)
        y0 = y_s[...]
        yhat = y0 * sc_ref[...] * r
        dcv = dc_ref[...]
        dg_ref[...] += jnp.sum(dcv * yhat, axis=0, keepdims=True)
        dyh = dcv * g_ref[...]
        dy = r * (dyh - yhat * (jnp.sum(dyh * yhat, axis=-1, keepdims=True) / width))
        dsc_ref[...] += jnp.sum(dy * y0, axis=0, keepdims=True)
        y_s[...] = dy * sc_ref[...]
        for gi, w in enumerate(POOL_WINDOWS):
            sl = slice(gi * grp, (gi + 1) * grp)
            dm = y_s[:, sl].astype(_MXU)
            dpw_ref[gi] += _dot_tn(pl_s[:, sl], dm)
            dpooled = _dot_nt(dm, pw_ref[gi])
            y_s[:, sl] = dpooled
            qext[0:tm, sl] = dpooled / _pool_counts(tile, tm, w)
        for gi, w in enumerate(POOL_WINDOWS):
            sl = slice(gi * grp, (gi + 1) * grp)
            tot = qext[0:tm, sl]
            for k in range(1, w):
                tot = tot + qext[k:k + tm, sl]
            du_ref[:, sl] = tot - y_s[:, sl]
        qext[tm:, :] = qext[0:POOL_HALO, :]

    half_in = pl.BlockSpec((tm, width), lambda i: (n_t - 1 - i, 0))
    prev = pl.BlockSpec((POOL_HALO, width), lambda i: (jnp.maximum((n_t - 1 - i) * halo_blocks - 1, 0), 0))
    half = pl.BlockSpec((tm, width), lambda i: (n_t - 1 - i, 0))
    return pl.pallas_call(
        body, name=name, grid=(n_t,),
        out_shape=[jax.ShapeDtypeStruct((n_rows, width), F32), jax.ShapeDtypeStruct((1, width), F32),
                   jax.ShapeDtypeStruct((1, width), F32), jax.ShapeDtypeStruct(pool_w.shape, F32)],
        in_specs=[half_in, prev, half, _full(pool_w.shape), _full((1, width)), _full((1, width))],
        out_specs=[half, _full((1, width)), _full((1, width)), _full(pool_w.shape)],
        scratch_shapes=[pltpu.VMEM((tm + POOL_HALO, width), F32), pltpu.VMEM((tm + POOL_HALO, width), F32),
                        pltpu.VMEM((tm, width), F32), pltpu.VMEM((tm, width), _MXU)],
        compiler_params=_params(),
    )(proj, proj, dcat, pool_w, scale, g_out)


def _mix_out_fwd(cat_s, cat_p, h, g, wo_s, wo_p, name):
    n_rows, d = h.shape
    width = cat_s.shape[1]
    tm = _row_tile(n_rows)

    def body(cs_ref, cp_ref, h_ref, g_ref, ws_ref, wp_ref, m_ref, ho_ref):
        m = _dot(cs_ref[...], ws_ref[...]) + _dot(cp_ref[...], wp_ref[...])
        m_ref[...] = m
        ho_ref[...] = h_ref[...] + m * _rs(m) * g_ref[...]

    row = pl.BlockSpec((tm, d), lambda i: (i, 0))
    half = pl.BlockSpec((tm, width), lambda i: (i, 0))
    return pl.pallas_call(
        body, name=name, grid=(n_rows // tm,), out_shape=[jax.ShapeDtypeStruct((n_rows, d), F32)] * 2,
        in_specs=[half, half, row, _full((1, d)), _resident(), _resident()], out_specs=[row, row],
        compiler_params=_params(),
    )(cat_s, cat_p, h, g, wo_s, wo_p)


def _mix_out_bwd(dho, mixed, g, wo_s, wo_p, name):
    n_rows, d = mixed.shape
    width = wo_s.shape[0]
    tm = _row_tile(n_rows)

    def body(dho_ref, m_ref, g_ref, ws_ref, wp_ref, dm_ref, dcs_ref, dcp_ref, dg_ref):
        @pl.when(pl.program_id(0) == 0)
        def _():
            dg_ref[...] = jnp.zeros_like(dg_ref)

        m = m_ref[...]
        r = _rs(m)
        mh = m * r
        dy = dho_ref[...]
        dg_ref[...] += jnp.sum(dy * mh, axis=0, keepdims=True)
        dmh = dy * g_ref[...]
        dm = (r * (dmh - mh * jnp.mean(dmh * mh, axis=-1, keepdims=True))).astype(_MXU)
        dm_ref[...] = dm
        dcs_ref[...] = _dot_nt(dm, ws_ref[...])
        dcp_ref[...] = _dot_nt(dm, wp_ref[...])

    row = pl.BlockSpec((tm, d), lambda i: (i, 0))
    half = pl.BlockSpec((tm, width), lambda i: (i, 0))
    return pl.pallas_call(
        body, name=name, grid=(n_rows // tm,),
        out_shape=[jax.ShapeDtypeStruct((n_rows, d), _MXU), jax.ShapeDtypeStruct((n_rows, width), F32),
                   jax.ShapeDtypeStruct((n_rows, width), F32), jax.ShapeDtypeStruct((1, d), F32)],
        in_specs=[row, row, _full((1, d)), _resident(), _resident()], out_specs=[row, half, half, _full((1, d))],
        compiler_params=_params(),
    )(dho, mixed, g, wo_s, wo_p)


def _mix_in_bwd(du_s, du_p, h, dho, g, wi_s, wi_p, name):
    n_rows, d = h.shape
    width = du_s.shape[1]
    tm = _row_tile(n_rows)

    def body(dus_ref, dup_ref, h_ref, dho_ref, g_ref, ws_ref, wp_ref, dh_ref, dp_ref, dg_ref):
        @pl.when(pl.program_id(0) == 0)
        def _():
            dg_ref[...] = jnp.zeros_like(dg_ref)

        dus, dup = dus_ref[...].astype(_MXU), dup_ref[...].astype(_MXU)
        dp_ref[:, 0:width] = dus
        dp_ref[:, width:2 * width] = dup
        dn = _dot_nt(dus, ws_ref[...]) + _dot_nt(dup, wp_ref[...])
        hv = h_ref[...]
        r = _rs(hv)
        hh = hv * r
        dg_ref[...] += jnp.sum(dn * hh, axis=0, keepdims=True)
        dhh = dn * g_ref[...]
        dh_ref[...] = dho_ref[...] + r * (dhh - hh * jnp.mean(dhh * hh, axis=-1, keepdims=True))

    row = pl.BlockSpec((tm, d), lambda i: (i, 0))
    half = pl.BlockSpec((tm, width), lambda i: (i, 0))
    return pl.pallas_call(
        body, name=name, grid=(n_rows // tm,),
        out_shape=[jax.ShapeDtypeStruct((n_rows, d), F32), jax.ShapeDtypeStruct((n_rows, 2 * width), _MXU),
                   jax.ShapeDtypeStruct((1, d), F32)],
        in_specs=[half, half, row, row, _full((1, d)), _resident(), _resident()],
        out_specs=[row, pl.BlockSpec((tm, 2 * width), lambda i: (i, 0)), _full((1, d))], compiler_params=_params(),
    )(du_s, du_p, h, dho, g, wi_s, wi_p)


def _loss_grad(h, target, name):
    n_rows, d = h.shape
    tm = _row_tile(n_rows)

    def body(h_ref, t_ref, dh_ref, l_ref):
        i = pl.program_id(0)

        @pl.when(i == 0)
        def _():
            l_ref[...] = jnp.zeros_like(l_ref)

        rows = i * tm + lax.broadcasted_iota(jnp.int32, (tm, 1), 0)
        err = jnp.where(rows >= N_META, h_ref[...] - t_ref[...], 0.0)
        dh_ref[...] = err / d
        l_ref[...] += jnp.sum(jnp.sum(err * err, axis=0, keepdims=True), axis=1, keepdims=True)

    row = pl.BlockSpec((tm, d), lambda i: (i, 0))
    return pl.pallas_call(
        body, name=name, grid=(n_rows // tm,),
        out_shape=[jax.ShapeDtypeStruct((n_rows, d), F32), jax.ShapeDtypeStruct((1, LANES), F32)],
        in_specs=[row, row], out_specs=[row, _full((1, LANES))], compiler_params=_params(),
    )(h, target)


def _adamw_update(w_ref, gv, m_ref, v_ref, d_ref, mo_ref, vo_ref):
    mn = ADAM_B1 * m_ref[...] + (1.0 - ADAM_B1) * gv
    vn = ADAM_B2 * v_ref[...] + (1.0 - ADAM_B2) * (gv * gv)
    m_hat = mn / (1.0 - ADAM_B1 ** ADAM_STEP)
    v_hat = vn / (1.0 - ADAM_B2 ** ADAM_STEP)
    d_ref[...] = -ADAM_LR * (m_hat / (jnp.sqrt(v_hat) + ADAM_EPS) + ADAM_WD * w_ref[...])
    mo_ref[...] = mn
    vo_ref[...] = vn


def _adamw(w, g, m, v, name):
    def body(w_ref, g_ref, m_ref, v_ref, d_ref, mo_ref, vo_ref):
        _adamw_update(w_ref, g_ref[...], m_ref, v_ref, d_ref, mo_ref, vo_ref)

    spec = _full(w.shape)
    return pl.pallas_call(
        body, name=name, grid=(1,), out_shape=[jax.ShapeDtypeStruct(w.shape, F32)] * 3,
        in_specs=[spec] * 4, out_specs=[spec] * 3, compiler_params=_params(),
    )(w, g, m, v)


def _adamw_slots(w, slots, m, v, name):
    def body(w_ref, s_ref, m_ref, v_ref, g_ref, d_ref, mo_ref, vo_ref):
        gv = s_ref[0].astype(F32)
        for k in range(1, N_DEV):
            gv = gv + s_ref[k].astype(F32)
        g_ref[...] = gv
        _adamw_update(w_ref, gv, m_ref, v_ref, d_ref, mo_ref, vo_ref)

    spec = _full(w.shape)
    return pl.pallas_call(
        body, name=name, grid=(1,), out_shape=[jax.ShapeDtypeStruct(w.shape, F32)] * 4,
        in_specs=[spec, _full(slots.shape), spec, spec], out_specs=[spec] * 4, compiler_params=_params(),
    )(w, slots, m, v)


def _discretize(lam_re, lam_im, log_dt, b_re, b_im):
    dt = jnp.exp(log_dt)[:, None]
    decay = jnp.exp(lam_re * dt)
    ang = lam_im * dt
    a_re = decay * jnp.cos(ang)
    a_im = decay * jnp.sin(ang)
    nr = a_re - 1.0
    den = lam_re * lam_re + lam_im * lam_im
    q_re = (nr * lam_re + a_im * lam_im) / den
    q_im = (a_im * lam_re - nr * lam_im) / den
    bb_re = q_re[..., None] * b_re - q_im[..., None] * b_im
    bb_im = q_re[..., None] * b_im + q_im[..., None] * b_re
    return a_re, a_im, bb_re, bb_im


def _cmul(a, b):
    return a[0] * b[0] - a[1] * b[1], a[0] * b[1] + a[1] * b[0]


def _scan_coefficients(a_re, a_im, reverse):
    a = (a_re.reshape(1, -1), -a_im.reshape(1, -1) if reverse else a_im.reshape(1, -1))
    powers = [a]
    for _ in range(SUBLANES - 1):
        powers.append(_cmul(powers[-1], a))
    row = jnp.arange(SUBLANES)[:, None]
    out = []
    for k in (1, 2, 4):
        keep = (row < SUBLANES - k) if reverse else (row >= k)
        out += [jnp.where(keep, powers[k - 1][0], 0.0), jnp.where(keep, powers[k - 1][1], 0.0)]
    order = range(SUBLANES - 1, -1, -1) if reverse else range(SUBLANES)
    out += [jnp.concatenate([powers[t][0] for t in order], axis=0), jnp.concatenate([powers[t][1] for t in order], axis=0)]
    return jnp.stack(out).astype(F32)


def _block_diag(p, n_blk):
    g, r, c = p.shape
    per = g // n_blk
    eye = jnp.eye(per, dtype=p.dtype)
    return jnp.einsum("jgrc,gk->jgrkc", p.reshape(n_blk, per, r, c), eye).reshape(n_blk, per * r, per * c)


def _block_diag_t(m, g):
    n_blk = m.shape[0]
    per = g // n_blk
    r, c = m.shape[1] // per, m.shape[2] // per
    eye = jnp.eye(per, dtype=m.dtype)
    return jnp.einsum("jgrkc,gk->jgrc", m.reshape(n_blk, per, r, per, c), eye).reshape(g, r, c)


def _pack_rows(parts, cols, multiple):
    flat = jnp.concatenate([p.reshape(-1) for p in parts])
    size = -(-flat.shape[0] // (cols * multiple)) * cols * multiple
    return jnp.pad(flat, (0, size - flat.shape[0])).reshape(-1, cols)


def _unpack(flat, shapes):
    out, pos = [], 0
    flat = flat.reshape(-1)
    for s in shapes:
        n = int(np.prod(s))
        out.append(flat[pos:pos + n].reshape(s))
        pos += n
    return out


def kernel(x, meta_tokens, ffn1_pre_norm, ffn1_post_norm, ffn1_w_gate, ffn1_w_up, ffn1_w_down, mix_pre_norm, mix_post_norm, w_in, ssm_lambda_re, ssm_lambda_im, ssm_log_dt, ssm_b_re, ssm_b_im, ssm_c_re, ssm_c_im, ssm_d, ssm_w_glu, pool_w, pool_scale, ssm_out_norm, pool_out_norm, w_out, ffn2_pre_norm, ffn2_post_norm, ffn2_w_gate, ffn2_w_up, ffn2_w_down, loss_target, m_meta_tokens, m_ffn1_pre_norm, m_ffn1_post_norm, m_ffn1_w_gate, m_ffn1_w_up, m_ffn1_w_down, m_mix_pre_norm, m_mix_post_norm, m_w_in, m_ssm_lambda_re, m_ssm_lambda_im, m_ssm_log_dt, m_ssm_b_re, m_ssm_b_im, m_ssm_c_re, m_ssm_c_im, m_ssm_d, m_ssm_w_glu, m_pool_w, m_pool_scale, m_ssm_out_norm, m_pool_out_norm, m_w_out, m_ffn2_pre_norm, m_ffn2_post_norm, m_ffn2_w_gate, m_ffn2_w_up, m_ffn2_w_down, v_meta_tokens, v_ffn1_pre_norm, v_ffn1_post_norm, v_ffn1_w_gate, v_ffn1_w_up, v_ffn1_w_down, v_mix_pre_norm, v_mix_post_norm, v_w_in, v_ssm_lambda_re, v_ssm_lambda_im, v_ssm_log_dt, v_ssm_b_re, v_ssm_b_im, v_ssm_c_re, v_ssm_c_im, v_ssm_d, v_ssm_w_glu, v_pool_w, v_pool_scale, v_ssm_out_norm, v_pool_out_norm, v_w_out, v_ffn2_pre_norm, v_ffn2_post_norm, v_ffn2_w_gate, v_ffn2_w_up, v_ffn2_w_down):
    args = dict(locals())
    names = ["meta_tokens", "ffn1_pre_norm", "ffn1_post_norm", "ffn1_w_gate", "ffn1_w_up", "ffn1_w_down", "mix_pre_norm",
             "mix_post_norm", "w_in", "ssm_lambda_re", "ssm_lambda_im", "ssm_log_dt", "ssm_b_re", "ssm_b_im", "ssm_c_re",
             "ssm_c_im", "ssm_d", "ssm_w_glu", "pool_w", "pool_scale", "ssm_out_norm", "pool_out_norm", "w_out",
             "ffn2_pre_norm", "ffn2_post_norm", "ffn2_w_gate", "ffn2_w_up", "ffn2_w_down"]
    sharded = ("meta_tokens", "ffn1_w_gate", "ffn1_w_up", "ffn1_w_down", "w_in", "w_out", "ffn2_w_gate", "ffn2_w_up",
               "ffn2_w_down")
    small = [n for n in names if n not in sharded]

    d = x.shape[-1]
    width = d // 2
    n_grp = ssm_lambda_re.shape[1]
    n_blk = width // LANES

    def wire(m):
        return m.astype(_WIRE)

    def stacked(gathered):
        return gathered.reshape(-1, d).astype(_MXU)

    def chunked(m):
        return m.reshape(N_DEV, -1, d)

    g_gate1, g_up1, g_down1, meta_all = _gather_two_level(
        [wire(ffn1_w_gate[0].T), wire(ffn1_w_up[0].T), wire(ffn1_w_down[0]), meta_tokens], "gather_ffn1")
    wgt1, wut1, wd1 = stacked(g_gate1), stacked(g_up1), stacked(g_down1)
    meta_full = jnp.transpose(meta_all, (1, 0, 2)).reshape(N_META, d)

    h0 = jnp.concatenate([meta_full, x[0]], axis=0)
    target = jnp.concatenate([jnp.zeros((N_META, d), F32), loss_target[0]], axis=0)

    (a1, b1, f1, h1), (g_win, g_wout) = _ffn_fwd(
        h0, ffn1_pre_norm, ffn1_post_norm, wgt1, wut1, wd1, "ffn1_fwd",
        xchg=_Xchg([wire(w_in[0]), wire(w_out[0])], ["gather"] * 2))
    w_in_f, w_out_f = stacked(g_win), stacked(g_wout)

    a_re, a_im, bb_re, bb_im = _discretize(ssm_lambda_re[0], ssm_lambda_im[0], ssm_log_dt[0], ssm_b_re[0], ssm_b_im[0])
    coef = _scan_coefficients(a_re, a_im, reverse=False)
    coef_rev = _scan_coefficients(a_re, a_im, reverse=True)
    bmat_re = _block_diag(jnp.swapaxes(bb_re, 1, 2), n_blk).astype(_MXU)
    bmat_im = _block_diag(jnp.swapaxes(bb_im, 1, 2), n_blk).astype(_MXU)
    cmat_re = _block_diag(jnp.swapaxes(ssm_c_re[0], 1, 2), n_blk).astype(_MXU)
    cmat_im = _block_diag(jnp.swapaxes(ssm_c_im[0], 1, 2), n_blk).astype(_MXU)
    wz1 = _block_diag(ssm_w_glu[0][:, :, :SSM_GROUP_CH], n_blk).astype(_MXU)
    wz2 = _block_diag(ssm_w_glu[0][:, :, SSM_GROUP_CH:], n_blk).astype(_MXU)
    pool_wm = pool_w[0].astype(_MXU)

    n2, u_s, u_p = _mix_in_fwd(h1, mix_pre_norm, w_in_f, "mix_in_fwd")
    (x_re, x_im, cat_s), (g_gate2, g_up2, g_down2) = _ssm_fwd(
        u_s, coef, bmat_re, bmat_im, cmat_re, cmat_im, wz1, wz2, ssm_d, ssm_out_norm, "ssm_fwd",
        xchg=_Xchg([wire(ffn2_w_gate[0].T), wire(ffn2_w_up[0].T), wire(ffn2_w_down[0])], ["gather"] * 3))
    wgt2, wut2, wd2 = stacked(g_gate2), stacked(g_up2), stacked(g_down2)
    cat_p = _pool_fwd(u_p, pool_wm, pool_scale, pool_out_norm, "pool_fwd")
    wo_s, wo_p = w_out_f[:width], w_out_f[width:]
    mixed, h2 = _mix_out_fwd(cat_s, cat_p, h1, mix_post_norm, wo_s, wo_p, "mix_out_fwd")

    (a2, b2, f2, h3), _ = _ffn_fwd(h2, ffn2_pre_norm, ffn2_post_norm, wgt2, wut2, wd2, "ffn2_fwd")
    dh3, sq_err = _loss_grad(h3, target, "loss_grad")

    g, slots = {}, {}
    (da2, db2, s2, df2, nf2, g["ffn2_post_norm"]), _ = _ffn_bwd_hidden(
        dh3, h2, f2, a2, b2, ffn2_pre_norm, ffn2_post_norm, wd2, "ffn2_bwd_hidden")
    dgate2, _ = _wgrad(da2, nf2, "ffn2_dgate")
    dup2, _ = _wgrad(db2, nf2, "ffn2_dup")
    ddown2, _ = _wgrad(s2, df2, "ffn2_ddown")
    (dh2, g["ffn2_pre_norm"]), (slots["ffn2_w_gate"], slots["ffn2_w_up"]) = _ffn_bwd_input(
        dh3, h2, da2, db2, ffn2_pre_norm, wgt2, wut2, "ffn2_bwd_input",
        xchg=_Xchg([chunked(dgate2), chunked(dup2)], ["scatter"] * 2))

    dmixed, dcat_s, dcat_p, g["mix_post_norm"] = _mix_out_bwd(dh2, mixed, mix_post_norm, wo_s, wo_p, "mix_out_bwd")
    dwout = jnp.concatenate([_wgrad(cat_s, dmixed, "dwout_s")[0], _wgrad(cat_p, dmixed, "dwout_p")[0]], axis=0)
    ((du_s, g["ssm_out_norm"], g["ssm_d"], dwz1, dwz2, dcm_re, dcm_im, dbm_re, dbm_im, acc_re, acc_im),
     (slots["ffn2_w_down"], slots["w_out"])) = _ssm_bwd(
        u_s, x_re, x_im, dcat_s, coef_rev, bmat_re, bmat_im, cmat_re, cmat_im, wz1, wz2, ssm_d, ssm_out_norm, "ssm_bwd",
        xchg=_Xchg([chunked(ddown2), chunked(dwout)], ["scatter"] * 2))
    du_p, g["pool_out_norm"], g["pool_scale"], dpw = _pool_bwd(u_p, dcat_p, pool_wm, pool_scale, pool_out_norm, "pool_bwd")
    wi_s, wi_p = w_in_f[:, :width], w_in_f[:, width:]
    dh1, dproj, g["mix_pre_norm"] = _mix_in_bwd(du_s, du_p, h1, dh2, mix_pre_norm, wi_s, wi_p, "mix_in_bwd")
    dwin, _ = _wgrad(n2, dproj, "dwin")

    (da1, db1, s1, df1, nf1, g["ffn1_post_norm"]), (slots["w_in"],) = _ffn_bwd_hidden(
        dh1, h0, f1, a1, b1, ffn1_pre_norm, ffn1_post_norm, wd1, "ffn1_bwd_hidden",
        xchg=_Xchg([chunked(dwin)], ["scatter"]))

    g["ssm_c_re"] = jnp.swapaxes(_block_diag_t(jnp.swapaxes(dcm_re, 1, 2), n_grp), 1, 2)[None]
    g["ssm_c_im"] = jnp.swapaxes(_block_diag_t(jnp.swapaxes(dcm_im, 1, 2), n_grp), 1, 2)[None]
    g["ssm_w_glu"] = jnp.concatenate([_block_diag_t(dwz1, n_grp), _block_diag_t(dwz2, n_grp)], axis=-1)[None]
    dbb_re = jnp.swapaxes(_block_diag_t(dbm_re, n_grp), 1, 2)
    dbb_im = jnp.swapaxes(_block_diag_t(dbm_im, n_grp), 1, 2)
    acc_re, acc_im = acc_re.reshape(a_re.shape), acc_im.reshape(a_re.shape)
    norm = a_re * a_re + a_im * a_im
    da_re = (acc_re * a_re - acc_im * a_im) / norm
    da_im = (acc_re * a_im + acc_im * a_re) / norm
    _, disc_vjp = jax.vjp(_discretize, ssm_lambda_re[0], ssm_lambda_im[0], ssm_log_dt[0], ssm_b_re[0], ssm_b_im[0])
    d_lre, d_lim, d_ldt, d_bre, d_bim = disc_vjp((da_re, da_im, dbb_re, dbb_im))
    g["ssm_lambda_re"], g["ssm_lambda_im"], g["ssm_log_dt"] = d_lre[None], d_lim[None], d_ldt[None]
    g["ssm_b_re"], g["ssm_b_im"] = d_bre[None], d_bim[None]
    g["pool_w"] = dpw[None]

    early = [n for n in small if n != "ffn1_pre_norm"]
    early_vec = _pack_rows([g[n] for n in early] + [sq_err[:, :1]], 1024, SUBLANES)
    dgate1, (recv_early,) = _wgrad(da1, nf1, "ffn1_dgate", xchg=_Xchg([early_vec], ["gather"]))
    dup1, (slots["ffn1_w_gate"],) = _wgrad(db1, nf1, "ffn1_dup", xchg=_Xchg([chunked(dgate1)], ["scatter"]))
    ddown1, (slots["ffn1_w_up"],) = _wgrad(s1, df1, "ffn1_ddown", xchg=_Xchg([chunked(dup1)], ["scatter"]))
    (dh0, g["ffn1_pre_norm"]), (slots["ffn1_w_down"],) = _ffn_bwd_input(
        dh1, h0, da1, db1, ffn1_pre_norm, wgt1, wut1, "ffn1_bwd_input", xchg=_Xchg([chunked(ddown1)], ["scatter"]))
    late_vec = _pack_rows([g["ffn1_pre_norm"], dh0[:N_META]], 1024, SUBLANES)
    (recv_late,) = _exchange([late_vec], ["gather"], "gather_last")

    summed = _unpack(_sum_slots(recv_early, "sum_small_grads"), [g[n].shape for n in early] + [(1,)])
    for n, val in zip(early, summed):
        g[n] = val
    loss = (0.5 / d) * summed[-1][0]
    g["ffn1_pre_norm"], dmeta = _unpack(_sum_slots(recv_late, "sum_last_grads"), [(1, d), (N_META, d)])
    g["meta_tokens"] = lax.dynamic_slice_in_dim(dmeta, _my_slot() * (d // N_DEV), d // N_DEV, axis=1)

    delta, new_m, new_v = {}, {}, {}
    for n in sharded:
        shape = args[n].shape
        two_d = (-1, shape[-1])
        w2, m2, v2 = args[n].reshape(two_d), args["m_" + n].reshape(two_d), args["v_" + n].reshape(two_d)
        if n == "meta_tokens":
            dl, mn, vn = _adamw(w2, g[n], m2, v2, "adamw_" + n)
        elif n.endswith("gate") or n.endswith("up"):
            g[n] = _sum_slots(slots[n], "sum_" + n).T[None]
            dl, mn, vn = _adamw(w2, g[n][0], m2, v2, "adamw_" + n)
        else:
            gs, dl, mn, vn = _adamw_slots(w2, slots[n], m2, v2, "adamw_" + n)
            g[n] = gs[None]
        delta[n], new_m[n], new_v[n] = dl.reshape(shape), mn.reshape(shape), vn.reshape(shape)
    packs = [_pack_rows([src[n] if pre is None else args[pre + n] for n in small], 1024, SUBLANES)
             for src, pre in ((args, None), (g, None), (None, "m_"), (None, "v_"))]
    outs = _adamw(*packs, "adamw_small")
    shapes = [args[n].shape for n in small]
    for store, flat in zip((delta, new_m, new_v), outs):
        for n, val in zip(small, _unpack(flat, shapes)):
            store[n] = val

    grad_x = dh0[N_META:][None]
    return (loss, grad_x, *[g[n] for n in names], *[delta[n] for n in names], *[new_m[n] for n in names],
            *[new_v[n] for n in names])
```

```python
import functools
import math

import jax
import jax.numpy as jnp
import numpy as np
from jax import lax
from jax.experimental import pallas as pl
from jax.experimental.pallas import tpu as pltpu

F32 = jnp.float32
_MXU = jnp.bfloat16
_ACT = jnp.bfloat16
_WIRE = jnp.bfloat16

N_DEV = 8
N_META = 16
RMS_EPS = 1e-6
SSM_GROUP_CH = 16
SSM_STATE = 64
LANES = 128
SUBLANES = 8
POOL_WINDOWS = (2, 4, 8, 16)
POOL_HALO = 16
ADAM_LR = 0.001
ADAM_B1 = 0.9
ADAM_B2 = 0.999
ADAM_EPS = 1e-08
ADAM_WD = 0.01
ADAM_STEP = 10
GELU_C0 = math.sqrt(2.0 / math.pi)
GELU_C1 = 0.044715
VMEM_LIMIT = 56 * 1024 * 1024

_NT = (((1,), (1,)), ((), ()))
_TN = (((0,), (0,)), ((), ()))


def _dot(a, b):
    return jnp.dot(a, b, preferred_element_type=F32)


def _dot_nt(a, b):
    return lax.dot_general(a, b, _NT, preferred_element_type=F32)


def _dot_tn(a, b):
    return lax.dot_general(a, b, _TN, preferred_element_type=F32)


def _rs(x):
    return lax.rsqrt(jnp.mean(x * x, axis=-1, keepdims=True) + RMS_EPS)


def _sigmoid(x):
    return 1.0 / (1.0 + jnp.exp(-x))


def _row_tile(n_rows, largest=432):
    for t in (432, 304, 48, 16):
        if t <= largest and n_rows % t == 0:
            return t
    raise ValueError(n_rows)


def _ff_chunk(d_ff):
    return d_ff // 2 if (d_ff // 2) % LANES == 0 else d_ff


def _params(n_axes=1):
    return pltpu.CompilerParams(dimension_semantics=("arbitrary",) * n_axes, vmem_limit_bytes=VMEM_LIMIT)


def _resident():
    return pl.BlockSpec(memory_space=pltpu.VMEM)


def _full(shape):
    nd = len(shape)
    return pl.BlockSpec(shape, lambda *_: (0,) * nd)


PEER_ORDER = (1, 2, 4, 3, 5, 6, 7)


def _split(refs, counts):
    out, pos = [], 0
    for n in counts:
        out.append(refs[pos:pos + n])
        pos += n
    return out


def _peer(r):
    x, y, c = lax.axis_index("x"), lax.axis_index("y"), lax.axis_index("c")
    return (1 - x if r & 4 else x, 1 - y if r & 2 else y, 1 - c if r & 1 else c)


def _my_slot():
    return 4 * lax.axis_index("x") + 2 * lax.axis_index("y") + lax.axis_index("c")


class _Xchg:
    def __init__(self, srcs, kinds):
        self.srcs, self.kinds, self.n = list(srcs), list(kinds), len(srcs)
        self.out_shape = [jax.ShapeDtypeStruct((N_DEV,) + s.shape if k == "gather" else s.shape, s.dtype)
                          for s, k in zip(self.srcs, self.kinds)]
        self.specs = [pl.BlockSpec(memory_space=pl.ANY)] * self.n
        self.scratch = [pltpu.SemaphoreType.DMA((self.n * (N_DEV - 1),)), pltpu.SemaphoreType.DMA((self.n * (N_DEV - 1),)),
                        pltpu.SemaphoreType.DMA((self.n,))]

    def copies(self, src, dst, sems):
        send_sems, recv_sems, local_sems = sems
        me = _my_slot()
        out = []
        for a in range(self.n):
            mine = src[a] if self.kinds[a] == "gather" else src[a].at[me]
            out.append(pltpu.make_async_copy(mine, dst[a].at[me], local_sems.at[a]))
            for r in PEER_ORDER:
                px, py, pc = _peer(r)
                part = src[a] if self.kinds[a] == "gather" else src[a].at[4 * px + 2 * py + pc]
                k = a * (N_DEV - 1) + r - 1
                out.append(pltpu.make_async_remote_copy(
                    src_ref=part, dst_ref=dst[a].at[me], send_sem=send_sems.at[k], recv_sem=recv_sems.at[k],
                    device_id=(px, py, pc), device_id_type=pl.DeviceIdType.MESH))
        return out

    def start(self, src, dst, sems):
        for cp in self.copies(src, dst, sems):
            cp.start()

    def wait(self, src, dst, sems):
        for cp in self.copies(src, dst, sems):
            cp.wait()


class _NoXchg:
    n, srcs, out_shape, specs, scratch = 0, [], [], [], []

    def start(self, *_):
        pass

    wait = start


def _call(body, args, *, name, grid, out_shape, in_specs, out_specs, scratch_shapes=(), xchg=None):
    xc = xchg or _NoXchg()
    counts = (len(in_specs), xc.n, len(out_shape), xc.n, len(scratch_shapes), len(xc.scratch))

    def wrapped(*refs):
        ins, xsrc, outs, xdst, scr, sems = _split(refs, counts)
        ids = [pl.program_id(k) for k in range(len(grid))]
        if xc.n:
            @pl.when(functools.reduce(jnp.logical_and, [i == 0 for i in ids]))
            def _():
                xc.start(xsrc, xdst, sems)

        body(*ins, *outs, *scr)
        if xc.n:
            @pl.when(functools.reduce(jnp.logical_and, [i == g - 1 for i, g in zip(ids, grid)]))
            def _():
                xc.wait(xsrc, xdst, sems)

    res = pl.pallas_call(
        wrapped, name=name, grid=grid, out_shape=list(out_shape) + xc.out_shape,
        in_specs=list(in_specs) + xc.specs, out_specs=list(out_specs) + xc.specs,
        scratch_shapes=list(scratch_shapes) + xc.scratch, compiler_params=_params(len(grid)),
    )(*args, *xc.srcs)
    return res[:len(out_shape)], res[len(out_shape):]


def _exchange(srcs, kinds, name):
    xc = _Xchg(srcs, kinds)

    def body(*refs):
        src, dst, sems = _split(refs, (xc.n, xc.n, 3))
        xc.start(src, dst, sems)
        xc.wait(src, dst, sems)

    return pl.pallas_call(body, name=name, out_shape=xc.out_shape, in_specs=xc.specs, out_specs=xc.specs,
                          scratch_shapes=xc.scratch)(*srcs)


def _gather_two_level(srcs, name):
    n = len(srcs)
    out_shape = [jax.ShapeDtypeStruct((N_DEV,) + s.shape, s.dtype) for s in srcs]
    chips = (2, 4, 6)

    def body(*refs):
        src, dst, (send_sems, recv_sems, local_sems) = _split(refs, (n, n, 3))
        x, y, c = lax.axis_index("x"), lax.axis_index("y"), lax.axis_index("c")
        me = 4 * x + 2 * y + c
        sibling = (x, y, 1 - c)

        def copy(a, k, slot, to, from_src=False):
            return pltpu.make_async_remote_copy(
                src_ref=src[a] if from_src else dst[a].at[slot], dst_ref=dst[a].at[slot],
                send_sem=send_sems.at[a * 7 + k], recv_sem=recv_sems.at[a * 7 + k],
                device_id=to, device_id_type=pl.DeviceIdType.MESH)

        def slot_of(r, core):
            px, py, _ = _peer(r)
            return 4 * px + 2 * py + core

        local = [pltpu.make_async_copy(src[a], dst[a].at[me], local_sems.at[a]) for a in range(n)]
        sent = []
        for a in range(n):
            local[a].start()
            sent.append(copy(a, 0, me, sibling, from_src=True))
            sent += [copy(a, 1 + j, me, _peer(r), from_src=True) for j, r in enumerate(chips)]
        for cp in sent:
            cp.start()
        for j, r in enumerate(chips):
            for a in range(n):
                copy(a, 1 + j, slot_of(r, c), _peer(r)).wait_recv()
                cp = copy(a, 4 + j, slot_of(r, c), sibling)
                cp.start()
                sent.append(cp)
        for a in range(n):
            copy(a, 0, slot_of(0, 1 - c), sibling).wait_recv()
            for j, r in enumerate(chips):
                copy(a, 4 + j, slot_of(r, 1 - c), sibling).wait_recv()
        for cp in local:
            cp.wait()
        for cp in sent:
            cp.wait_send()

    any_spec = pl.BlockSpec(memory_space=pl.ANY)
    return pl.pallas_call(
        body, name=name, out_shape=out_shape, in_specs=[any_spec] * n, out_specs=[any_spec] * n,
        scratch_shapes=[pltpu.SemaphoreType.DMA((n * 7,)), pltpu.SemaphoreType.DMA((n * 7,)), pltpu.SemaphoreType.DMA((n,))],
    )(*srcs)


def _sum_slots(r, name):
    _, rows, cols = r.shape
    blk = rows
    for cand in (rows, 592, 512, 256, 128, 64, 32, 16):
        if rows % cand == 0 and N_DEV * cand * cols * r.dtype.itemsize <= 8 * 1024 * 1024:
            blk = cand
            break

    def body(r_ref, o_ref):
        acc = r_ref[0].astype(F32)
        for d in range(1, N_DEV):
            acc = acc + r_ref[d].astype(F32)
        o_ref[...] = acc

    return pl.pallas_call(
        body, name=name, grid=(rows // blk,), out_shape=jax.ShapeDtypeStruct((rows, cols), F32),
        in_specs=[pl.BlockSpec((N_DEV, blk, cols), lambda i: (0, i, 0))],
        out_specs=pl.BlockSpec((blk, cols), lambda i: (i, 0)), compiler_params=_params(),
    )(r)


def _ffn_fwd(h, g_pre, g_post, wgt, wut, wd, name, xchg=None):
    n_rows, d = h.shape
    d_ff = wd.shape[0]
    tm, fc = _row_tile(n_rows), _ff_chunk(d_ff)
    n_t, n_c = n_rows // tm, d_ff // fc

    def body(h_ref, gpre_ref, gpost_ref, wgt_ref, wut_ref, wd_ref, a_ref, b_ref, f_ref, ho_ref, n_scr, acc):
        c = pl.program_id(1)

        @pl.when(c == 0)
        def _():
            hv = h_ref[...]
            n_scr[...] = (hv * _rs(hv) * gpre_ref[...]).astype(_MXU)
            acc[...] = jnp.zeros_like(acc)

        rows = pl.ds(pl.multiple_of(c * fc, fc), fc)
        nv = n_scr[...]
        a = _dot_nt(nv, wgt_ref[rows, :])
        b = _dot_nt(nv, wut_ref[rows, :])
        a_ref[...] = a.astype(_ACT)
        b_ref[...] = b.astype(_ACT)
        s = a * _sigmoid(a) * b
        acc[...] += _dot(s.astype(_MXU), wd_ref[rows, :])

        @pl.when(c == n_c - 1)
        def _():
            f = acc[...]
            f_ref[...] = f
            ho_ref[...] = h_ref[...] + 0.5 * (f * _rs(f) * gpost_ref[...])

    row = pl.BlockSpec((tm, d), lambda i, c: (i, 0))
    chunk = pl.BlockSpec((tm, fc), lambda i, c: (i, c))
    return _call(
        body, (h, g_pre, g_post, wgt, wut, wd), name=name, grid=(n_t, n_c),
        out_shape=[jax.ShapeDtypeStruct((n_rows, d_ff), _ACT), jax.ShapeDtypeStruct((n_rows, d_ff), _ACT),
                   jax.ShapeDtypeStruct((n_rows, d), F32), jax.ShapeDtypeStruct((n_rows, d), F32)],
        in_specs=[row, _full((1, d)), _full((1, d)), _resident(), _resident(), _resident()],
        out_specs=[chunk, chunk, row, row],
        scratch_shapes=[pltpu.VMEM((tm, d), _MXU), pltpu.VMEM((tm, d), F32)], xchg=xchg)


def _ffn_bwd(dho, h, f, a, b, g_pre, g_post, wgt, wut, wd, name, xchg=None):
    n_rows, d = h.shape
    d_ff = wd.shape[0]
    tm, fc = _row_tile(n_rows, 304), _ff_chunk(d_ff)
    n_t, n_c = n_rows // tm, d_ff // fc

    def body(dho_ref, h_ref, f_ref, a_ref, b_ref, gpre_ref, gpost_ref, wgt_ref, wut_ref, wd_ref,
             dh_ref, da_ref, db_ref, s_ref, df_ref, n_ref, dgpre_ref, dgpost_ref, dn_acc):
        i, c = pl.program_id(0), pl.program_id(1)

        @pl.when((i == 0) & (c == 0))
        def _():
            dgpre_ref[...] = jnp.zeros_like(dgpre_ref)
            dgpost_ref[...] = jnp.zeros_like(dgpost_ref)

        @pl.when(c == 0)
        def _():
            fv = f_ref[...]
            rf = _rs(fv)
            fhat = fv * rf
            dy = 0.5 * dho_ref[...]
            dgpost_ref[...] += jnp.sum(dy * fhat, axis=0, keepdims=True)
            dfhat = dy * gpost_ref[...]
            df = rf * (dfhat - fhat * jnp.mean(dfhat * fhat, axis=-1, keepdims=True))
            df_ref[...] = df.astype(_MXU)
            hv = h_ref[...]
            n_ref[...] = (hv * _rs(hv) * gpre_ref[...]).astype(_MXU)
            dn_acc[...] = jnp.zeros_like(dn_acc)

        rows = pl.ds(pl.multiple_of(c * fc, fc), fc)
        ds = _dot_nt(df_ref[...], wd_ref[rows, :])
        av = a_ref[...].astype(F32)
        bv = b_ref[...].astype(F32)
        sg = _sigmoid(av)
        si = av * sg
        da = (ds * bv * (sg * (1.0 + av * (1.0 - sg)))).astype(_MXU)
        db = (ds * si).astype(_MXU)
        da_ref[...] = da
        db_ref[...] = db
        s_ref[...] = (si * bv).astype(_MXU)
        dn_acc[...] += _dot(da, wgt_ref[rows, :]) + _dot(db, wut_ref[rows, :])

        @pl.when(c == n_c - 1)
        def _():
            dn = dn_acc[...]
            hv = h_ref[...]
            r = _rs(hv)
            hhat = hv * r
            dgpre_ref[...] += jnp.sum(dn * hhat, axis=0, keepdims=True)
            dhh = dn * gpre_ref[...]
            dh_ref[...] = dho_ref[...] + r * (dhh - hhat * jnp.mean(dhh * hhat, axis=-1, keepdims=True))

    row = pl.BlockSpec((tm, d), lambda i, c: (i, 0))
    chunk = pl.BlockSpec((tm, fc), lambda i, c: (i, c))
    vec = pl.BlockSpec((1, d), lambda i, c: (0, 0))
    return _call(
        body, (dho, h, f, a, b, g_pre, g_post, wgt, wut, wd), name=name, grid=(n_t, n_c),
        out_shape=[jax.ShapeDtypeStruct((n_rows, d), F32)] + [jax.ShapeDtypeStruct((n_rows, d_ff), _MXU)] * 3
        + [jax.ShapeDtypeStruct((n_rows, d), _MXU)] * 2 + [jax.ShapeDtypeStruct((1, d), F32)] * 2,
        in_specs=[row, row, row, chunk, chunk, vec, vec, _resident(), _resident(), _resident()],
        out_specs=[row, chunk, chunk, chunk, row, row, vec, vec],
        scratch_shapes=[pltpu.VMEM((tm, d), F32)], xchg=xchg)


def _wgrad(xm, ym, name, xchg=None):
    n_rows, a_dim = xm.shape
    b_dim = ym.shape[1]
    tk = n_rows
    for cand in (2736, 1296, 432, 48, 16):
        if n_rows % cand == 0:
            tk = cand
            break
    ta = a_dim
    for cand in (1408, 1024, 512):
        if a_dim % cand == 0:
            ta = cand
            break

    n_k = n_rows // tk

    def body(x_ref, y_ref, o_ref, acc):
        k = pl.program_id(1)

        @pl.when(k == 0)
        def _():
            acc[...] = jnp.zeros_like(acc)

        acc[...] += _dot_tn(x_ref[...], y_ref[...])

        @pl.when(k == n_k - 1)
        def _():
            o_ref[...] = acc[...].astype(o_ref.dtype)

    (out,), extra = _call(
        body, (xm, ym), name=name, grid=(a_dim // ta, n_k), out_shape=[jax.ShapeDtypeStruct((a_dim, b_dim), _WIRE)],
        in_specs=[pl.BlockSpec((tk, ta), lambda j, k: (k, j)), pl.BlockSpec((tk, b_dim), lambda j, k: (k, 0))],
        out_specs=[pl.BlockSpec((ta, b_dim), lambda j, k: (j, 0))], scratch_shapes=[pltpu.VMEM((ta, b_dim), F32)],
        xchg=xchg)
    return out, extra


def _mix_in_fwd(h, g, w_in, name):
    n_rows, d = h.shape
    tm = _row_tile(n_rows)

    width = w_in.shape[1] // 2

    def body(h_ref, g_ref, w_ref, n_ref, us_ref, up_ref):
        hv = h_ref[...]
        nv = (hv * _rs(hv) * g_ref[...]).astype(_MXU)
        n_ref[...] = nv
        p = _dot(nv, w_ref[...])
        us_ref[...] = p[:, :width]
        up_ref[...] = p[:, width:]

    row = pl.BlockSpec((tm, d), lambda i: (i, 0))
    half = pl.BlockSpec((tm, width), lambda i: (i, 0))
    return pl.pallas_call(
        body, name=name, grid=(n_rows // tm,),
        out_shape=[jax.ShapeDtypeStruct((n_rows, d), _MXU)] + [jax.ShapeDtypeStruct((n_rows, width), F32)] * 2,
        in_specs=[row, _full((1, d)), _resident()], out_specs=[row, half, half], compiler_params=_params(),
    )(h, g, w_in)


def _gelu_parts(y):
    th = jnp.tanh(GELU_C0 * (y + GELU_C1 * (y * y * y)))
    return 0.5 * (1.0 + th), th


def _scan_blocks(n_blocks, width, coef_ref, load, store, reverse):
    n_cols = coef_ref.shape[2] // width
    for cb in range(n_cols):
        cols = pl.ds(cb * width, width)

        def step(t, carry, cols=cols):
            tb = (n_blocks - 1 - t) if reverse else t
            r0 = pl.multiple_of(tb * SUBLANES, SUBLANES)
            vr, vi = load(r0, cols)
            xr, xi = vr, vi
            for lvl, k in enumerate((1, 2, 4)):
                kr, ki = coef_ref[2 * lvl, :, cols], coef_ref[2 * lvl + 1, :, cols]
                shift = SUBLANES - k if reverse else k
                sr, si = pltpu.roll(xr, shift, 0), pltpu.roll(xi, shift, 0)
                xr, xi = xr + (kr * sr - ki * si), xi + (kr * si + ki * sr)
            pr, pi = coef_ref[6, :, cols], coef_ref[7, :, cols]
            cr, ci = carry[0], carry[1]
            xr, xi = xr + (pr * cr - pi * ci), xi + (pr * ci + pi * cr)
            extra = store(r0, cols, xr, xi, vr, vi, carry[2:])
            edge = 0 if reverse else SUBLANES - 1
            return (xr[edge:edge + 1, :], xi[edge:edge + 1, :]) + tuple(extra)

        yield cb, cols, step


def _ssm_fwd(proj, coef, b_re, b_im, c_re, c_im, wz1, wz2, d_skip, g_out, name, xchg=None):
    n_rows = proj.shape[0]
    n_blk, _, n_state = b_re.shape
    width = n_blk * LANES
    tm = _row_tile(n_rows)
    n_tb = tm // SUBLANES

    def body(u_ref, coef_ref, bre_ref, bim_ref, cre_ref, cim_ref, wz1_ref, wz2_ref, d_ref, g_ref,
             xre_ref, xim_ref, o_ref, car_re, car_im, out_scr):
        @pl.when(pl.program_id(0) == 0)
        def _():
            car_re[...] = jnp.zeros_like(car_re)
            car_im[...] = jnp.zeros_like(car_im)

        ub = u_ref[...].astype(_MXU)
        for j in range(n_blk):
            uj = ub[:, j * LANES:(j + 1) * LANES]
            xre_ref[:, j * n_state:(j + 1) * n_state] = _dot(uj, bre_ref[j])
            xim_ref[:, j * n_state:(j + 1) * n_state] = _dot(uj, bim_ref[j])

        def load(r0, cols):
            return xre_ref[pl.ds(r0, SUBLANES), cols], xim_ref[pl.ds(r0, SUBLANES), cols]

        def store(r0, cols, xr, xi, vr, vi, extra):
            xre_ref[pl.ds(r0, SUBLANES), cols] = xr
            xim_ref[pl.ds(r0, SUBLANES), cols] = xi
            return ()

        for cb, cols, step in _scan_blocks(n_tb, 512, coef_ref, load, store, reverse=False):
            fin = lax.fori_loop(0, n_tb, step, (car_re[:, cols], car_im[:, cols]))
            car_re[:, cols] = fin[0]
            car_im[:, cols] = fin[1]

        ssq = None
        for j in range(n_blk):
            sl = slice(j * LANES, (j + 1) * LANES)
            st = slice(j * n_state, (j + 1) * n_state)
            yc = _dot(xre_ref[:, st].astype(_MXU), cre_ref[j]) - _dot(xim_ref[:, st].astype(_MXU), cim_ref[j])
            y = yc + d_ref[:, sl] * u_ref[:, sl]
            cdf, _ = _gelu_parts(y)
            gy = (y * cdf).astype(_MXU)
            out = _dot(gy, wz1_ref[j]) * _sigmoid(_dot(gy, wz2_ref[j]))
            out_scr[:, sl] = out
            part = jnp.sum(out * out, axis=-1, keepdims=True)
            ssq = part if ssq is None else ssq + part
        r = lax.rsqrt(ssq / width + RMS_EPS)
        o_ref[...] = (out_scr[...] * r * g_ref[...]).astype(_MXU)

    half = pl.BlockSpec((tm, width), lambda i: (i, 0))
    state = pl.BlockSpec((tm, n_blk * n_state), lambda i: (i, 0))
    return _call(
        body, (proj, coef, b_re, b_im, c_re, c_im, wz1, wz2, d_skip, g_out), name=name, grid=(n_rows // tm,),
        out_shape=[jax.ShapeDtypeStruct((n_rows, n_blk * n_state), F32)] * 2 + [jax.ShapeDtypeStruct((n_rows, width), _MXU)],
        in_specs=[half, _full(coef.shape), _full(b_re.shape), _full(b_im.shape), _full(c_re.shape), _full(c_im.shape),
                  _full(wz1.shape), _full(wz2.shape), _full((1, width)), _full((1, width))],
        out_specs=[state, state, half],
        scratch_shapes=[pltpu.VMEM((1, n_blk * n_state), F32)] * 2 + [pltpu.VMEM((tm, width), F32)], xchg=xchg)


def _ssm_bwd(proj, x_re, x_im, dcat, coef_rev, b_re, b_im, c_re, c_im, wz1, wz2, d_skip, g_out, name, xchg=None):
    n_rows = proj.shape[0]
    n_blk, _, n_state = b_re.shape
    width = n_blk * LANES
    n_all = n_blk * n_state
    tm = _row_tile(n_rows)
    n_t, n_tb = n_rows // tm, tm // SUBLANES

    def body(u_ref, xre_ref, xim_ref, dc_ref, coef_ref, bre_ref, bim_ref, cre_ref, cim_ref, wz1_ref, wz2_ref, d_ref, g_ref,
             du_ref, dg_ref, dd_ref, dwz1_ref, dwz2_ref, dcre_ref, dcim_ref, dbre_ref, dbim_ref, are_ref, aim_ref,
             car_re, car_im, gre, gim, y_s, z1_s, sg_s, out_s, gy_s):
        @pl.when(pl.program_id(0) == 0)
        def _():
            for ref in (dg_ref, dd_ref, dwz1_ref, dwz2_ref, dcre_ref, dcim_ref, dbre_ref, dbim_ref, are_ref, aim_ref,
                        car_re, car_im):
                ref[...] = jnp.zeros_like(ref)

        ssq = None
        for j in range(n_blk):
            sl = slice(j * LANES, (j + 1) * LANES)
            st = slice(j * n_state, (j + 1) * n_state)
            yc = _dot(xre_ref[:, st].astype(_MXU), cre_ref[j]) - _dot(xim_ref[:, st].astype(_MXU), cim_ref[j])
            y = yc + d_ref[:, sl] * u_ref[:, sl]
            cdf, _ = _gelu_parts(y)
            gy = (y * cdf).astype(_MXU)
            z1 = _dot(gy, wz1_ref[j])
            sg = _sigmoid(_dot(gy, wz2_ref[j]))
            out = z1 * sg
            y_s[:, sl], z1_s[:, sl], sg_s[:, sl], out_s[:, sl], gy_s[:, sl] = y, z1, sg, out, gy
            part = jnp.sum(out * out, axis=-1, keepdims=True)
            ssq = part if ssq is None else ssq + part
        r = lax.rsqrt(ssq / width + RMS_EPS)
        ohat = out_s[...] * r
        dcv = dc_ref[...]
        dg_ref[...] += jnp.sum(dcv * ohat, axis=0, keepdims=True)
        doh = dcv * g_ref[...]
        out_s[...] = r * (doh - ohat * (jnp.sum(doh * ohat, axis=-1, keepdims=True) / width))

        for j in range(n_blk):
            sl = slice(j * LANES, (j + 1) * LANES)
            st = slice(j * n_state, (j + 1) * n_state)
            dout, sg, z1, y = out_s[:, sl], sg_s[:, sl], z1_s[:, sl], y_s[:, sl]
            dz1 = (dout * sg).astype(_MXU)
            dz2 = (dout * z1 * sg * (1.0 - sg)).astype(_MXU)
            gy = gy_s[:, sl]
            dwz1_ref[j] += _dot_tn(gy, dz1)
            dwz2_ref[j] += _dot_tn(gy, dz2)
            dgy = _dot_nt(dz1, wz1_ref[j]) + _dot_nt(dz2, wz2_ref[j])
            cdf, th = _gelu_parts(y)
            dy = dgy * (cdf + y * (0.5 * (1.0 - th * th) * GELU_C0 * (1.0 + 3.0 * GELU_C1 * (y * y))))
            uj = u_ref[:, sl]
            dd_ref[:, sl] += jnp.sum(dy * uj, axis=0, keepdims=True)
            du_ref[:, sl] = d_ref[:, sl] * dy
            dyb = dy.astype(_MXU)
            dcre_ref[j] += _dot_tn(dyb, xre_ref[:, st].astype(_MXU))
            dcim_ref[j] -= _dot_tn(dyb, xim_ref[:, st].astype(_MXU))
            gre[:, st] = _dot_nt(dyb, cre_ref[j])
            gim[:, st] = -_dot_nt(dyb, cim_ref[j])

        def load(r0, cols):
            return gre[pl.ds(r0, SUBLANES), cols], gim[pl.ds(r0, SUBLANES), cols]

        def store(r0, cols, gr, gi, vr, vi, extra):
            gre[pl.ds(r0, SUBLANES), cols] = gr
            gim[pl.ds(r0, SUBLANES), cols] = gi
            hr, hi = gr - vr, gi - vi
            xr, xi = xre_ref[pl.ds(r0, SUBLANES), cols], xim_ref[pl.ds(r0, SUBLANES), cols]
            return extra[0] + (xr * hr + xi * hi), extra[1] + (xr * hi - xi * hr)

        for cb, cols, step in _scan_blocks(n_tb, 512, coef_ref, load, store, reverse=True):
            zero = jnp.zeros((SUBLANES, 512), F32)
            fin = lax.fori_loop(0, n_tb, step, (car_re[:, cols], car_im[:, cols], zero, zero))
            car_re[:, cols] = fin[0]
            car_im[:, cols] = fin[1]
            are_ref[:, cols] += jnp.sum(fin[2], axis=0, keepdims=True)
            aim_ref[:, cols] += jnp.sum(fin[3], axis=0, keepdims=True)

        for j in range(n_blk):
            sl = slice(j * LANES, (j + 1) * LANES)
            st = slice(j * n_state, (j + 1) * n_state)
            ujb = u_ref[:, sl].astype(_MXU)
            grb, gib = gre[:, st].astype(_MXU), gim[:, st].astype(_MXU)
            dbre_ref[j] += _dot_tn(ujb, grb)
            dbim_ref[j] += _dot_tn(ujb, gib)
            du_ref[:, sl] += _dot_nt(grb, bre_ref[j]) + _dot_nt(gib, bim_ref[j])

    half = pl.BlockSpec((tm, width), lambda i: (n_t - 1 - i, 0))
    state = pl.BlockSpec((tm, n_all), lambda i: (n_t - 1 - i, 0))
    small = [(1, width), (1, width), wz1.shape, wz2.shape, (n_blk, LANES, n_state), (n_blk, LANES, n_state),
             (n_blk, LANES, n_state), (n_blk, LANES, n_state), (1, n_all), (1, n_all)]
    return _call(
        body, (proj, x_re, x_im, dcat, coef_rev, b_re, b_im, c_re, c_im, wz1, wz2, d_skip, g_out), name=name, grid=(n_t,),
        out_shape=[jax.ShapeDtypeStruct((n_rows, width), F32)] + [jax.ShapeDtypeStruct(s, F32) for s in small],
        in_specs=[half, state, state, half, _full(coef_rev.shape), _full(b_re.shape), _full(b_im.shape), _full(c_re.shape),
                  _full(c_im.shape), _full(wz1.shape), _full(wz2.shape), _full((1, width)), _full((1, width))],
        out_specs=[half] + [_full(s) for s in small],
        scratch_shapes=[pltpu.VMEM((1, n_all), F32)] * 2 + [pltpu.VMEM((tm, n_all), F32)] * 2
        + [pltpu.VMEM((tm, width), F32)] * 4 + [pltpu.VMEM((tm, width), _MXU)], xchg=xchg)


def _pool_counts(tile, tm, window):
    t = tile * tm + lax.broadcasted_iota(jnp.int32, (tm, 1), 0)
    return jnp.minimum(t + 1, window).astype(F32)


def _pool_fwd(proj, pool_w, scale, g_out, name):
    n_rows = proj.shape[0]
    n_grp, grp, _ = pool_w.shape
    width = n_grp * grp
    tm = _row_tile(n_rows)

    def body(u_ref, pw_ref, sc_ref, g_ref, o_ref, ext, y_s):
        i = pl.program_id(0)

        @pl.when(i == 0)
        def _():
            ext[0:POOL_HALO, :] = jnp.zeros((POOL_HALO, width), F32)

        ext[POOL_HALO:, :] = u_ref[...]
        ssq = None
        for gi, w in enumerate(POOL_WINDOWS):
            sl = slice(gi * grp, (gi + 1) * grp)
            tot = ext[POOL_HALO:, sl]
            for k in range(1, w):
                tot = tot + ext[POOL_HALO - k:POOL_HALO - k + tm, sl]
            pooled = tot / _pool_counts(i, tm, w) - u_ref[:, sl]
            y = _dot(pooled.astype(_MXU), pw_ref[gi]) * sc_ref[:, sl]
            y_s[:, sl] = y
            part = jnp.sum(y * y, axis=-1, keepdims=True)
            ssq = part if ssq is None else ssq + part
        r = lax.rsqrt(ssq / width + RMS_EPS)
        o_ref[...] = (y_s[...] * r * g_ref[...]).astype(_MXU)
        ext[0:POOL_HALO, :] = u_ref[tm - POOL_HALO:, :]

    half_in = pl.BlockSpec((tm, width), lambda i: (i, 0))
    half = pl.BlockSpec((tm, width), lambda i: (i, 0))
    return pl.pallas_call(
        body, name=name, grid=(n_rows // tm,), out_shape=jax.ShapeDtypeStruct((n_rows, width), _MXU),
        in_specs=[half_in, _full(pool_w.shape), _full((1, width)), _full((1, width))], out_specs=half,
        scratch_shapes=[pltpu.VMEM((tm + POOL_HALO, width), F32), pltpu.VMEM((tm, width), F32)],
        compiler_params=_params(),
    )(proj, pool_w, scale, g_out)


def _pool_bwd(proj, dcat, pool_w, scale, g_out, name):
    n_rows = proj.shape[0]
    n_grp, grp, _ = pool_w.shape
    width = n_grp * grp
    tm = _row_tile(n_rows)
    n_t = n_rows // tm
    halo_blocks = tm // POOL_HALO

    def body(u_ref, up_ref, dc_ref, pw_ref, sc_ref, g_ref, du_ref, dg_ref, dsc_ref, dpw_ref, ext, qext, y_s, pl_s):
        i = pl.program_id(0)
        tile = n_t - 1 - i

        @pl.when(i == 0)
        def _():
            for ref in (dg_ref, dsc_ref, dpw_ref):
                ref[...] = jnp.zeros_like(ref)
            qext[tm:, :] = jnp.zeros((POOL_HALO, width), F32)

        ext[0:POOL_HALO, :] = jnp.where(tile > 0, up_ref[...], 0.0)
        ext[POOL_HALO:, :] = u_ref[...]
        ssq = None
        for gi, w in enumerate(POOL_WINDOWS):
            sl = slice(gi * grp, (gi + 1) * grp)
            tot = ext[POOL_HALO:, sl]
            for k in range(1, w):
                tot = tot + ext[POOL_HALO - k:POOL_HALO - k + tm, sl]
            pooled = (tot / _pool_counts(tile, tm, w) - u_ref[:, sl]).astype(_MXU)
            pl_s[:, sl] = pooled
            y0 = _dot(pooled, pw_ref[gi])
            y_s[:, sl] = y0
            y = y0 * sc_ref[:, sl]
            part = jnp.sum(y * y, axis=-1, keepdims=True)
            ssq = part if ssq is None else ssq + part
        r = lax.rsqrt(ssq / width + RMS_EPS)
        y0 = y_s[...]
        yhat = y0 * sc_ref[...] * r
        dcv = dc_ref[...]
        dg_ref[...] += jnp.sum(dcv * yhat, axis=0, keepdims=True)
        dyh = dcv * g_ref[...]
        dy = r * (dyh - yhat * (jnp.sum(dyh * yhat, axis=-1, keepdims=True) / width))
        dsc_ref[...] += jnp.sum(dy * y0, axis=0, keepdims=True)
        y_s[...] = dy * sc_ref[...]
        for gi, w in enumerate(POOL_WINDOWS):
            sl = slice(gi * grp, (gi + 1) * grp)
            dm = y_s[:, sl].astype(_MXU)
            dpw_ref[gi] += _dot_tn(pl_s[:, sl], dm)
            dpooled = _dot_nt(dm, pw_ref[gi])
            y_s[:, sl] = dpooled
            qext[0:tm, sl] = dpooled / _pool_counts(tile, tm, w)
        for gi, w in enumerate(POOL_WINDOWS):
            sl = slice(gi * grp, (gi + 1) * grp)
            tot = qext[0:tm, sl]
            for k in range(1, w):
                tot = tot + qext[k:k + tm, sl]
            du_ref[:, sl] = tot - y_s[:, sl]
        qext[tm:, :] = qext[0:POOL_HALO, :]

    half_in = pl.BlockSpec((tm, width), lambda i: (n_t - 1 - i, 0))
    prev = pl.BlockSpec((POOL_HALO, width), lambda i: (jnp.maximum((n_t - 1 - i) * halo_blocks - 1, 0), 0))
    half = pl.BlockSpec((tm, width), lambda i: (n_t - 1 - i, 0))
    return pl.pallas_call(
        body, name=name, grid=(n_t,),
        out_shape=[jax.ShapeDtypeStruct((n_rows, width), F32), jax.ShapeDtypeStruct((1, width), F32),
                   jax.ShapeDtypeStruct((1, width), F32), jax.ShapeDtypeStruct(pool_w.shape, F32)],
        in_specs=[half_in, prev, half, _full(pool_w.shape), _full((1, width)), _full((1, width))],
        out_specs=[half, _full((1, width)), _full((1, width)), _full(pool_w.shape)],
        scratch_shapes=[pltpu.VMEM((tm + POOL_HALO, width), F32), pltpu.VMEM((tm + POOL_HALO, width), F32),
                        pltpu.VMEM((tm, width), F32), pltpu.VMEM((tm, width), _MXU)],
        compiler_params=_params(),
    )(proj, proj, dcat, pool_w, scale, g_out)


def _mix_out_fwd(cat_s, cat_p, h, g, wo_s, wo_p, name):
    n_rows, d = h.shape
    width = cat_s.shape[1]
    tm = _row_tile(n_rows)

    def body(cs_ref, cp_ref, h_ref, g_ref, ws_ref, wp_ref, m_ref, ho_ref):
        m = _dot(cs_ref[...], ws_ref[...]) + _dot(cp_ref[...], wp_ref[...])
        m_ref[...] = m
        ho_ref[...] = h_ref[...] + m * _rs(m) * g_ref[...]

    row = pl.BlockSpec((tm, d), lambda i: (i, 0))
    half = pl.BlockSpec((tm, width), lambda i: (i, 0))
    return pl.pallas_call(
        body, name=name, grid=(n_rows // tm,), out_shape=[jax.ShapeDtypeStruct((n_rows, d), F32)] * 2,
        in_specs=[half, half, row, _full((1, d)), _resident(), _resident()], out_specs=[row, row],
        compiler_params=_params(),
    )(cat_s, cat_p, h, g, wo_s, wo_p)


def _mix_out_bwd(dho, mixed, g, wo_s, wo_p, name):
    n_rows, d = mixed.shape
    width = wo_s.shape[0]
    tm = _row_tile(n_rows)

    def body(dho_ref, m_ref, g_ref, ws_ref, wp_ref, dm_ref, dcs_ref, dcp_ref, dg_ref):
        @pl.when(pl.program_id(0) == 0)
        def _():
            dg_ref[...] = jnp.zeros_like(dg_ref)

        m = m_ref[...]
        r = _rs(m)
        mh = m * r
        dy = dho_ref[...]
        dg_ref[...] += jnp.sum(dy * mh, axis=0, keepdims=True)
        dmh = dy * g_ref[...]
        dm = (r * (dmh - mh * jnp.mean(dmh * mh, axis=-1, keepdims=True))).astype(_MXU)
        dm_ref[...] = dm
        dcs_ref[...] = _dot_nt(dm, ws_ref[...])
        dcp_ref[...] = _dot_nt(dm, wp_ref[...])

    row = pl.BlockSpec((tm, d), lambda i: (i, 0))
    half = pl.BlockSpec((tm, width), lambda i: (i, 0))
    return pl.pallas_call(
        body, name=name, grid=(n_rows // tm,),
        out_shape=[jax.ShapeDtypeStruct((n_rows, d), _MXU), jax.ShapeDtypeStruct((n_rows, width), F32),
                   jax.ShapeDtypeStruct((n_rows, width), F32), jax.ShapeDtypeStruct((1, d), F32)],
        in_specs=[row, row, _full((1, d)), _resident(), _resident()], out_specs=[row, half, half, _full((1, d))],
        compiler_params=_params(),
    )(dho, mixed, g, wo_s, wo_p)


def _mix_in_bwd(du_s, du_p, h, dho, g, wi_s, wi_p, name):
    n_rows, d = h.shape
    width = du_s.shape[1]
    tm = _row_tile(n_rows)

    def body(dus_ref, dup_ref, h_ref, dho_ref, g_ref, ws_ref, wp_ref, dh_ref, dp_ref, dg_ref):
        @pl.when(pl.program_id(0) == 0)
        def _():
            dg_ref[...] = jnp.zeros_like(dg_ref)

        dus, dup = dus_ref[...].astype(_MXU), dup_ref[...].astype(_MXU)
        dp_ref[:, 0:width] = dus
        dp_ref[:, width:2 * width] = dup
        dn = _dot_nt(dus, ws_ref[...]) + _dot_nt(dup, wp_ref[...])
        hv = h_ref[...]
        r = _rs(hv)
        hh = hv * r
        dg_ref[...] += jnp.sum(dn * hh, axis=0, keepdims=True)
        dhh = dn * g_ref[...]
        dh_ref[...] = dho_ref[...] + r * (dhh - hh * jnp.mean(dhh * hh, axis=-1, keepdims=True))

    row = pl.BlockSpec((tm, d), lambda i: (i, 0))
    half = pl.BlockSpec((tm, width), lambda i: (i, 0))
    return pl.pallas_call(
        body, name=name, grid=(n_rows // tm,),
        out_shape=[jax.ShapeDtypeStruct((n_rows, d), F32), jax.ShapeDtypeStruct((n_rows, 2 * width), _MXU),
                   jax.ShapeDtypeStruct((1, d), F32)],
        in_specs=[half, half, row, row, _full((1, d)), _resident(), _resident()],
        out_specs=[row, pl.BlockSpec((tm, 2 * width), lambda i: (i, 0)), _full((1, d))], compiler_params=_params(),
    )(du_s, du_p, h, dho, g, wi_s, wi_p)


def _loss_grad(h, target, name):
    n_rows, d = h.shape
    tm = _row_tile(n_rows)

    def body(h_ref, t_ref, dh_ref, l_ref):
        i = pl.program_id(0)

        @pl.when(i == 0)
        def _():
            l_ref[...] = jnp.zeros_like(l_ref)

        rows = i * tm + lax.broadcasted_iota(jnp.int32, (tm, 1), 0)
        err = jnp.where(rows >= N_META, h_ref[...] - t_ref[...], 0.0)
        dh_ref[...] = err / d
        l_ref[...] += jnp.sum(jnp.sum(err * err, axis=0, keepdims=True), axis=1, keepdims=True)

    row = pl.BlockSpec((tm, d), lambda i: (i, 0))
    return pl.pallas_call(
        body, name=name, grid=(n_rows // tm,),
        out_shape=[jax.ShapeDtypeStruct((n_rows, d), F32), jax.ShapeDtypeStruct((1, LANES), F32)],
        in_specs=[row, row], out_specs=[row, _full((1, LANES))], compiler_params=_params(),
    )(h, target)


def _adamw_update(w_ref, gv, m_ref, v_ref, d_ref, mo_ref, vo_ref):
    mn = ADAM_B1 * m_ref[...] + (1.0 - ADAM_B1) * gv
    vn = ADAM_B2 * v_ref[...] + (1.0 - ADAM_B2) * (gv * gv)
    m_hat = mn / (1.0 - ADAM_B1 ** ADAM_STEP)
    v_hat = vn / (1.0 - ADAM_B2 ** ADAM_STEP)
    d_ref[...] = -ADAM_LR * (m_hat / (jnp.sqrt(v_hat) + ADAM_EPS) + ADAM_WD * w_ref[...])
    mo_ref[...] = mn
    vo_ref[...] = vn


def _adamw(w, g, m, v, name):
    def body(w_ref, g_ref, m_ref, v_ref, d_ref, mo_ref, vo_ref):
        _adamw_update(w_ref, g_ref[...], m_ref, v_ref, d_ref, mo_ref, vo_ref)

    spec = _full(w.shape)
    return pl.pallas_call(
        body, name=name, grid=(1,), out_shape=[jax.ShapeDtypeStruct(w.shape, F32)] * 3,
        in_specs=[spec] * 4, out_specs=[spec] * 3, compiler_params=_params(),
    )(w, g, m, v)


def _adamw_slots(w, slots, m, v, name):
    def body(w_ref, s_ref, m_ref, v_ref, g_ref, d_ref, mo_ref, vo_ref):
        gv = s_ref[0].astype(F32)
        for k in range(1, N_DEV):
            gv = gv + s_ref[k].astype(F32)
        g_ref[...] = gv
        _adamw_update(w_ref, gv, m_ref, v_ref, d_ref, mo_ref, vo_ref)

    spec = _full(w.shape)
    return pl.pallas_call(
        body, name=name, grid=(1,), out_shape=[jax.ShapeDtypeStruct(w.shape, F32)] * 4,
        in_specs=[spec, _full(slots.shape), spec, spec], out_specs=[spec] * 4, compiler_params=_params(),
    )(w, slots, m, v)


def _discretize(lam_re, lam_im, log_dt, b_re, b_im):
    dt = jnp.exp(log_dt)[:, None]
    decay = jnp.exp(lam_re * dt)
    ang = lam_im * dt
    a_re = decay * jnp.cos(ang)
    a_im = decay * jnp.sin(ang)
    nr = a_re - 1.0
    den = lam_re * lam_re + lam_im * lam_im
    q_re = (nr * lam_re + a_im * lam_im) / den
    q_im = (a_im * lam_re - nr * lam_im) / den
    bb_re = q_re[..., None] * b_re - q_im[..., None] * b_im
    bb_im = q_re[..., None] * b_im + q_im[..., None] * b_re
    return a_re, a_im, bb_re, bb_im


def _cmul(a, b):
    return a[0] * b[0] - a[1] * b[1], a[0] * b[1] + a[1] * b[0]


def _scan_coefficients(a_re, a_im, reverse):
    a = (a_re.reshape(1, -1), -a_im.reshape(1, -1) if reverse else a_im.reshape(1, -1))
    powers = [a]
    for _ in range(SUBLANES - 1):
        powers.append(_cmul(powers[-1], a))
    row = jnp.arange(SUBLANES)[:, None]
    out = []
    for k in (1, 2, 4):
        keep = (row < SUBLANES - k) if reverse else (row >= k)
        out += [jnp.where(keep, powers[k - 1][0], 0.0), jnp.where(keep, powers[k - 1][1], 0.0)]
    order = range(SUBLANES - 1, -1, -1) if reverse else range(SUBLANES)
    out += [jnp.concatenate([powers[t][0] for t in order], axis=0), jnp.concatenate([powers[t][1] for t in order], axis=0)]
    return jnp.stack(out).astype(F32)


def _block_diag(p, n_blk):
    g, r, c = p.shape
    per = g // n_blk
    eye = jnp.eye(per, dtype=p.dtype)
    return jnp.einsum("jgrc,gk->jgrkc", p.reshape(n_blk, per, r, c), eye).reshape(n_blk, per * r, per * c)


def _block_diag_t(m, g):
    n_blk = m.shape[0]
    per = g // n_blk
    r, c = m.shape[1] // per, m.shape[2] // per
    eye = jnp.eye(per, dtype=m.dtype)
    return jnp.einsum("jgrkc,gk->jgrc", m.reshape(n_blk, per, r, per, c), eye).reshape(g, r, c)


def _pack_rows(parts, cols, multiple):
    flat = jnp.concatenate([p.reshape(-1) for p in parts])
    size = -(-flat.shape[0] // (cols * multiple)) * cols * multiple
    return jnp.pad(flat, (0, size - flat.shape[0])).reshape(-1, cols)


def _unpack(flat, shapes):
    out, pos = [], 0
    flat = flat.reshape(-1)
    for s in shapes:
        n = int(np.prod(s))
        out.append(flat[pos:pos + n].reshape(s))
        pos += n
    return out


def kernel(x, meta_tokens, ffn1_pre_norm, ffn1_post_norm, ffn1_w_gate, ffn1_w_up, ffn1_w_down, mix_pre_norm, mix_post_norm, w_in, ssm_lambda_re, ssm_lambda_im, ssm_log_dt, ssm_b_re, ssm_b_im, ssm_c_re, ssm_c_im, ssm_d, ssm_w_glu, pool_w, pool_scale, ssm_out_norm, pool_out_norm, w_out, ffn2_pre_norm, ffn2_post_norm, ffn2_w_gate, ffn2_w_up, ffn2_w_down, loss_target, m_meta_tokens, m_ffn1_pre_norm, m_ffn1_post_norm, m_ffn1_w_gate, m_ffn1_w_up, m_ffn1_w_down, m_mix_pre_norm, m_mix_post_norm, m_w_in, m_ssm_lambda_re, m_ssm_lambda_im, m_ssm_log_dt, m_ssm_b_re, m_ssm_b_im, m_ssm_c_re, m_ssm_c_im, m_ssm_d, m_ssm_w_glu, m_pool_w, m_pool_scale, m_ssm_out_norm, m_pool_out_norm, m_w_out, m_ffn2_pre_norm, m_ffn2_post_norm, m_ffn2_w_gate, m_ffn2_w_up, m_ffn2_w_down, v_meta_tokens, v_ffn1_pre_norm, v_ffn1_post_norm, v_ffn1_w_gate, v_ffn1_w_up, v_ffn1_w_down, v_mix_pre_norm, v_mix_post_norm, v_w_in, v_ssm_lambda_re, v_ssm_lambda_im, v_ssm_log_dt, v_ssm_b_re, v_ssm_b_im, v_ssm_c_re, v_ssm_c_im, v_ssm_d, v_ssm_w_glu, v_pool_w, v_pool_scale, v_ssm_out_norm, v_pool_out_norm, v_w_out, v_ffn2_pre_norm, v_ffn2_post_norm, v_ffn2_w_gate, v_ffn2_w_up, v_ffn2_w_down):
    args = dict(locals())
    names = ["meta_tokens", "ffn1_pre_norm", "ffn1_post_norm", "ffn1_w_gate", "ffn1_w_up", "ffn1_w_down", "mix_pre_norm",
             "mix_post_norm", "w_in", "ssm_lambda_re", "ssm_lambda_im", "ssm_log_dt", "ssm_b_re", "ssm_b_im", "ssm_c_re",
             "ssm_c_im", "ssm_d", "ssm_w_glu", "pool_w", "pool_scale", "ssm_out_norm", "pool_out_norm", "w_out",
             "ffn2_pre_norm", "ffn2_post_norm", "ffn2_w_gate", "ffn2_w_up", "ffn2_w_down"]
    sharded = ("meta_tokens", "ffn1_w_gate", "ffn1_w_up", "ffn1_w_down", "w_in", "w_out", "ffn2_w_gate", "ffn2_w_up",
               "ffn2_w_down")
    small = [n for n in names if n not in sharded]

    d = x.shape[-1]
    width = d // 2
    n_grp = ssm_lambda_re.shape[1]
    n_blk = width // LANES

    def wire(m):
        return m.astype(_WIRE)

    def stacked(gathered):
        return gathered.reshape(-1, d).astype(_MXU)

    def chunked(m):
        return m.reshape(N_DEV, -1, d)

    g_gate1, g_up1, g_down1, meta_all = _gather_two_level(
        [wire(ffn1_w_gate[0].T), wire(ffn1_w_up[0].T), wire(ffn1_w_down[0]), meta_tokens], "gather_ffn1")
    wgt1, wut1, wd1 = stacked(g_gate1), stacked(g_up1), stacked(g_down1)
    meta_full = jnp.transpose(meta_all, (1, 0, 2)).reshape(N_META, d)

    h0 = jnp.concatenate([meta_full, x[0]], axis=0)
    target = jnp.concatenate([jnp.zeros((N_META, d), F32), loss_target[0]], axis=0)

    (a1, b1, f1, h1), (g_win, g_wout) = _ffn_fwd(
        h0, ffn1_pre_norm, ffn1_post_norm, wgt1, wut1, wd1, "ffn1_fwd",
        xchg=_Xchg([wire(w_in[0]), wire(w_out[0])], ["gather"] * 2))
    w_in_f, w_out_f = stacked(g_win), stacked(g_wout)

    a_re, a_im, bb_re, bb_im = _discretize(ssm_lambda_re[0], ssm_lambda_im[0], ssm_log_dt[0], ssm_b_re[0], ssm_b_im[0])
    coef = _scan_coefficients(a_re, a_im, reverse=False)
    coef_rev = _scan_coefficients(a_re, a_im, reverse=True)
    bmat_re = _block_diag(jnp.swapaxes(bb_re, 1, 2), n_blk).astype(_MXU)
    bmat_im = _block_diag(jnp.swapaxes(bb_im, 1, 2), n_blk).astype(_MXU)
    cmat_re = _block_diag(jnp.swapaxes(ssm_c_re[0], 1, 2), n_blk).astype(_MXU)
    cmat_im = _block_diag(jnp.swapaxes(ssm_c_im[0], 1, 2), n_blk).astype(_MXU)
    wz1 = _block_diag(ssm_w_glu[0][:, :, :SSM_GROUP_CH], n_blk).astype(_MXU)
    wz2 = _block_diag(ssm_w_glu[0][:, :, SSM_GROUP_CH:], n_blk).astype(_MXU)
    pool_wm = pool_w[0].astype(_MXU)

    n2, u_s, u_p = _mix_in_fwd(h1, mix_pre_norm, w_in_f, "mix_in_fwd")
    (x_re, x_im, cat_s), (g_gate2, g_up2, g_down2) = _ssm_fwd(
        u_s, coef, bmat_re, bmat_im, cmat_re, cmat_im, wz1, wz2, ssm_d, ssm_out_norm, "ssm_fwd",
        xchg=_Xchg([wire(ffn2_w_gate[0].T), wire(ffn2_w_up[0].T), wire(ffn2_w_down[0])], ["gather"] * 3))
    wgt2, wut2, wd2 = stacked(g_gate2), stacked(g_up2), stacked(g_down2)
    cat_p = _pool_fwd(u_p, pool_wm, pool_scale, pool_out_norm, "pool_fwd")
    wo_s, wo_p = w_out_f[:width], w_out_f[width:]
    mixed, h2 = _mix_out_fwd(cat_s, cat_p, h1, mix_post_norm, wo_s, wo_p, "mix_out_fwd")

    (a2, b2, f2, h3), _ = _ffn_fwd(h2, ffn2_pre_norm, ffn2_post_norm, wgt2, wut2, wd2, "ffn2_fwd")
    dh3, sq_err = _loss_grad(h3, target, "loss_grad")

    g, slots = {}, {}
    (dh2, da2, db2, s2, df2, nf2, g["ffn2_pre_norm"], g["ffn2_post_norm"]), _ = _ffn_bwd(
        dh3, h2, f2, a2, b2, ffn2_pre_norm, ffn2_post_norm, wgt2, wut2, wd2, "ffn2_bwd")
    dgate2, _ = _wgrad(da2, nf2, "ffn2_dgate")
    dup2, _ = _wgrad(db2, nf2, "ffn2_dup")
    ddown2, _ = _wgrad(s2, df2, "ffn2_ddown")

    dmixed, dcat_s, dcat_p, g["mix_post_norm"] = _mix_out_bwd(dh2, mixed, mix_post_norm, wo_s, wo_p, "mix_out_bwd")
    dwout = jnp.concatenate([_wgrad(cat_s, dmixed, "dwout_s")[0], _wgrad(cat_p, dmixed, "dwout_p")[0]], axis=0)
    ((du_s, g["ssm_out_norm"], g["ssm_d"], dwz1, dwz2, dcm_re, dcm_im, dbm_re, dbm_im, acc_re, acc_im),
     (slots["ffn2_w_gate"], slots["ffn2_w_up"], slots["ffn2_w_down"])) = _ssm_bwd(
        u_s, x_re, x_im, dcat_s, coef_rev, bmat_re, bmat_im, cmat_re, cmat_im, wz1, wz2, ssm_d, ssm_out_norm, "ssm_bwd",
        xchg=_Xchg([chunked(dgate2), chunked(dup2), chunked(ddown2)], ["scatter"] * 3))
    du_p, g["pool_out_norm"], g["pool_scale"], dpw = _pool_bwd(u_p, dcat_p, pool_wm, pool_scale, pool_out_norm, "pool_bwd")
    wi_s, wi_p = w_in_f[:, :width], w_in_f[:, width:]
    dh1, dproj, g["mix_pre_norm"] = _mix_in_bwd(du_s, du_p, h1, dh2, mix_pre_norm, wi_s, wi_p, "mix_in_bwd")
    dwin, _ = _wgrad(n2, dproj, "dwin")

    g["ssm_c_re"] = jnp.swapaxes(_block_diag_t(jnp.swapaxes(dcm_re, 1, 2), n_grp), 1, 2)[None]
    g["ssm_c_im"] = jnp.swapaxes(_block_diag_t(jnp.swapaxes(dcm_im, 1, 2), n_grp), 1, 2)[None]
    g["ssm_w_glu"] = jnp.concatenate([_block_diag_t(dwz1, n_grp), _block_diag_t(dwz2, n_grp)], axis=-1)[None]
    dbb_re = jnp.swapaxes(_block_diag_t(dbm_re, n_grp), 1, 2)
    dbb_im = jnp.swapaxes(_block_diag_t(dbm_im, n_grp), 1, 2)
    acc_re, acc_im = acc_re.reshape(a_re.shape), acc_im.reshape(a_re.shape)
    norm = a_re * a_re + a_im * a_im
    da_re = (acc_re * a_re - acc_im * a_im) / norm
    da_im = (acc_re * a_im + acc_im * a_re) / norm
    _, disc_vjp = jax.vjp(_discretize, ssm_lambda_re[0], ssm_lambda_im[0], ssm_log_dt[0], ssm_b_re[0], ssm_b_im[0])
    d_lre, d_lim, d_ldt, d_bre, d_bim = disc_vjp((da_re, da_im, dbb_re, dbb_im))
    g["ssm_lambda_re"], g["ssm_lambda_im"], g["ssm_log_dt"] = d_lre[None], d_lim[None], d_ldt[None]
    g["ssm_b_re"], g["ssm_b_im"] = d_bre[None], d_bim[None]
    g["pool_w"] = dpw[None]

    late = ["ffn1_pre_norm", "ffn1_post_norm"]
    early = [n for n in small if n not in late]
    early_vec = _pack_rows([g[n] for n in early] + [sq_err[:, :1]], 1024, SUBLANES)
    ((dh0, da1, db1, s1, df1, nf1, g["ffn1_pre_norm"], g["ffn1_post_norm"]),
     (slots["w_out"], slots["w_in"], recv_early)) = _ffn_bwd(
        dh1, h0, f1, a1, b1, ffn1_pre_norm, ffn1_post_norm, wgt1, wut1, wd1, "ffn1_bwd",
        xchg=_Xchg([chunked(dwout), chunked(dwin), early_vec], ["scatter", "scatter", "gather"]))
    late_vec = _pack_rows([g[n] for n in late] + [dh0[:N_META]], 1024, SUBLANES)
    dgate1, (recv_late,) = _wgrad(da1, nf1, "ffn1_dgate", xchg=_Xchg([late_vec], ["gather"]))
    dup1, (slots["ffn1_w_gate"],) = _wgrad(db1, nf1, "ffn1_dup", xchg=_Xchg([chunked(dgate1)], ["scatter"]))
    ddown1, (slots["ffn1_w_up"],) = _wgrad(s1, df1, "ffn1_ddown", xchg=_Xchg([chunked(dup1)], ["scatter"]))
    (slots["ffn1_w_down"],) = _exchange([chunked(ddown1)], ["scatter"], "reduce_last")

    summed = _unpack(_sum_slots(recv_early, "sum_small_grads"), [g[n].shape for n in early] + [(1,)])
    for n, val in zip(early, summed):
        g[n] = val
    loss = (0.5 / d) * summed[-1][0]
    g[late[0]], g[late[1]], dmeta = _unpack(_sum_slots(recv_late, "sum_last_grads"), [(1, d), (1, d), (N_META, d)])
    g["meta_tokens"] = lax.dynamic_slice_in_dim(dmeta, _my_slot() * (d // N_DEV), d // N_DEV, axis=1)

    delta, new_m, new_v = {}, {}, {}
    for n in sharded:
        shape = args[n].shape
        two_d = (-1, shape[-1])
        w2, m2, v2 = args[n].reshape(two_d), args["m_" + n].reshape(two_d), args["v_" + n].reshape(two_d)
        if n == "meta_tokens":
            dl, mn, vn = _adamw(w2, g[n], m2, v2, "adamw_" + n)
        elif n.endswith("gate") or n.endswith("up"):
            g[n] = _sum_slots(slots[n], "sum_" + n).T[None]
            dl, mn, vn = _adamw(w2, g[n][0], m2, v2, "adamw_" + n)
        else:
            gs, dl, mn, vn = _adamw_slots(w2, slots[n], m2, v2, "adamw_" + n)
            g[n] = gs[None]
        delta[n], new_m[n], new_v[n] = dl.reshape(shape), mn.reshape(shape), vn.reshape(shape)
    packs = [_pack_rows([src[n] if pre is None else args[pre + n] for n in small], 1024, SUBLANES)
             for src, pre in ((args, None), (g, None), (None, "m_"), (None, "v_"))]
    outs = _adamw(*packs, "adamw_small")
    shapes = [args[n].shape for n in small]
    for store, flat in zip((delta, new_m, new_v), outs):
        for n, val in zip(small, _unpack(flat, shapes)):
            store[n] = val

    grad_x = dh0[N_META:][None]
    return (loss, grad_x, *[g[n] for n in names], *[delta[n] for n in names], *[new_m[n] for n in names],
            *[new_v[n] for n in names])
```

```python
import functools
import math

import jax
import jax.numpy as jnp
import numpy as np
from jax import lax
from jax.experimental import pallas as pl
from jax.experimental.pallas import tpu as pltpu

F32 = jnp.float32
_MXU = jnp.bfloat16
_ACT = jnp.bfloat16
_WIRE = jnp.bfloat16

N_DEV = 8
N_META = 16
RMS_EPS = 1e-6
SSM_GROUP_CH = 16
SSM_STATE = 64
LANES = 128
SUBLANES = 8
POOL_WINDOWS = (2, 4, 8, 16)
POOL_HALO = 16
ADAM_LR = 0.001
ADAM_B1 = 0.9
ADAM_B2 = 0.999
ADAM_EPS = 1e-08
ADAM_WD = 0.01
ADAM_STEP = 10
GELU_C0 = math.sqrt(2.0 / math.pi)
GELU_C1 = 0.044715
VMEM_LIMIT = 56 * 1024 * 1024

_NT = (((1,), (1,)), ((), ()))
_TN = (((0,), (0,)), ((), ()))


def _dot(a, b):
    return jnp.dot(a, b, preferred_element_type=F32)


def _dot_nt(a, b):
    return lax.dot_general(a, b, _NT, preferred_element_type=F32)


def _dot_tn(a, b):
    return lax.dot_general(a, b, _TN, preferred_element_type=F32)


def _rs(x):
    return lax.rsqrt(jnp.mean(x * x, axis=-1, keepdims=True) + RMS_EPS)


def _sigmoid(x):
    return 0.5 * jnp.tanh(0.5 * x) + 0.5


def _row_tile(n_rows, largest=432):
    for t in (432, 304, 48, 16):
        if t <= largest and n_rows % t == 0:
            return t
    raise ValueError(n_rows)


def _ff_chunk(d_ff):
    return d_ff // 2 if (d_ff // 2) % LANES == 0 else d_ff


def _params(n_axes=1):
    return pltpu.CompilerParams(dimension_semantics=("arbitrary",) * n_axes, vmem_limit_bytes=VMEM_LIMIT)


def _resident():
    return pl.BlockSpec(memory_space=pltpu.VMEM)


def _full(shape):
    nd = len(shape)
    return pl.BlockSpec(shape, lambda *_: (0,) * nd)


def _lane_blocks(n_blk, tm, tile_of=lambda i: i):
    return pl.BlockSpec((n_blk, tm, LANES), lambda i: (0, tile_of(i), 0))


PEER_ORDER = (1, 2, 4, 3, 5, 6, 7)


def _split(refs, counts):
    out, pos = [], 0
    for n in counts:
        out.append(refs[pos:pos + n])
        pos += n
    return out


def _peer(r):
    x, y, c = lax.axis_index("x"), lax.axis_index("y"), lax.axis_index("c")
    return (1 - x if r & 4 else x, 1 - y if r & 2 else y, 1 - c if r & 1 else c)


def _my_slot():
    return 4 * lax.axis_index("x") + 2 * lax.axis_index("y") + lax.axis_index("c")


class _Xchg:
    def __init__(self, srcs, kinds):
        self.srcs, self.kinds, self.n = list(srcs), list(kinds), len(srcs)
        self.out_shape = [jax.ShapeDtypeStruct((N_DEV,) + s.shape if k == "gather" else s.shape, s.dtype)
                          for s, k in zip(self.srcs, self.kinds)]
        self.specs = [pl.BlockSpec(memory_space=pl.ANY)] * self.n
        self.scratch = [pltpu.SemaphoreType.DMA((self.n * (N_DEV - 1),)), pltpu.SemaphoreType.DMA((self.n * (N_DEV - 1),)),
                        pltpu.SemaphoreType.DMA((self.n,))]

    def copies(self, src, dst, sems):
        send_sems, recv_sems, local_sems = sems
        me = _my_slot()
        out = []
        for a in range(self.n):
            mine = src[a] if self.kinds[a] == "gather" else src[a].at[me]
            out.append(pltpu.make_async_copy(mine, dst[a].at[me], local_sems.at[a]))
            for r in PEER_ORDER:
                px, py, pc = _peer(r)
                part = src[a] if self.kinds[a] == "gather" else src[a].at[4 * px + 2 * py + pc]
                k = a * (N_DEV - 1) + r - 1
                out.append(pltpu.make_async_remote_copy(
                    src_ref=part, dst_ref=dst[a].at[me], send_sem=send_sems.at[k], recv_sem=recv_sems.at[k],
                    device_id=(px, py, pc), device_id_type=pl.DeviceIdType.MESH))
        return out

    def start(self, src, dst, sems):
        for cp in self.copies(src, dst, sems):
            cp.start()

    def wait(self, src, dst, sems):
        for cp in self.copies(src, dst, sems):
            cp.wait()


class _NoXchg:
    n, srcs, out_shape, specs, scratch = 0, [], [], [], []

    def start(self, *_):
        pass

    wait = start


def _call(body, args, *, name, grid, out_shape, in_specs, out_specs, scratch_shapes=(), xchg=None):
    xc = xchg or _NoXchg()
    counts = (len(in_specs), xc.n, len(out_shape), xc.n, len(scratch_shapes), len(xc.scratch))

    def wrapped(*refs):
        ins, xsrc, outs, xdst, scr, sems = _split(refs, counts)
        ids = [pl.program_id(k) for k in range(len(grid))]
        if xc.n:
            @pl.when(functools.reduce(jnp.logical_and, [i == 0 for i in ids]))
            def _():
                xc.start(xsrc, xdst, sems)

        body(*ins, *outs, *scr)
        if xc.n:
            @pl.when(functools.reduce(jnp.logical_and, [i == g - 1 for i, g in zip(ids, grid)]))
            def _():
                xc.wait(xsrc, xdst, sems)

    res = pl.pallas_call(
        wrapped, name=name, grid=grid, out_shape=list(out_shape) + xc.out_shape,
        in_specs=list(in_specs) + xc.specs, out_specs=list(out_specs) + xc.specs,
        scratch_shapes=list(scratch_shapes) + xc.scratch, compiler_params=_params(len(grid)),
    )(*args, *xc.srcs)
    return res[:len(out_shape)], res[len(out_shape):]


def _exchange(srcs, kinds, name):
    xc = _Xchg(srcs, kinds)

    def body(*refs):
        src, dst, sems = _split(refs, (xc.n, xc.n, 3))
        xc.start(src, dst, sems)
        xc.wait(src, dst, sems)

    return pl.pallas_call(body, name=name, out_shape=xc.out_shape, in_specs=xc.specs, out_specs=xc.specs,
                          scratch_shapes=xc.scratch)(*srcs)


def _gather_two_level(srcs, name):
    n = len(srcs)
    out_shape = [jax.ShapeDtypeStruct((N_DEV,) + s.shape, s.dtype) for s in srcs]
    chips = (2, 4, 6)

    def body(*refs):
        src, dst, (send_sems, recv_sems, local_sems) = _split(refs, (n, n, 3))
        x, y, c = lax.axis_index("x"), lax.axis_index("y"), lax.axis_index("c")
        me = 4 * x + 2 * y + c
        sibling = (x, y, 1 - c)

        def copy(a, k, slot, to, from_src=False):
            return pltpu.make_async_remote_copy(
                src_ref=src[a] if from_src else dst[a].at[slot], dst_ref=dst[a].at[slot],
                send_sem=send_sems.at[a * 7 + k], recv_sem=recv_sems.at[a * 7 + k],
                device_id=to, device_id_type=pl.DeviceIdType.MESH)

        def slot_of(r, core):
            px, py, _ = _peer(r)
            return 4 * px + 2 * py + core

        local = [pltpu.make_async_copy(src[a], dst[a].at[me], local_sems.at[a]) for a in range(n)]
        sent = []
        for a in range(n):
            local[a].start()
            sent.append(copy(a, 0, me, sibling, from_src=True))
            sent += [copy(a, 1 + j, me, _peer(r), from_src=True) for j, r in enumerate(chips)]
        for cp in sent:
            cp.start()
        for j, r in enumerate(chips):
            for a in range(n):
                copy(a, 1 + j, slot_of(r, c), _peer(r)).wait_recv()
                cp = copy(a, 4 + j, slot_of(r, c), sibling)
                cp.start()
                sent.append(cp)
        for a in range(n):
            copy(a, 0, slot_of(0, 1 - c), sibling).wait_recv()
            for j, r in enumerate(chips):
                copy(a, 4 + j, slot_of(r, 1 - c), sibling).wait_recv()
        for cp in local:
            cp.wait()
        for cp in sent:
            cp.wait_send()

    any_spec = pl.BlockSpec(memory_space=pl.ANY)
    return pl.pallas_call(
        body, name=name, out_shape=out_shape, in_specs=[any_spec] * n, out_specs=[any_spec] * n,
        scratch_shapes=[pltpu.SemaphoreType.DMA((n * 7,)), pltpu.SemaphoreType.DMA((n * 7,)), pltpu.SemaphoreType.DMA((n,))],
    )(*srcs)


def _sum_slots(r, name):
    _, rows, cols = r.shape
    blk = rows
    for cand in (rows, 592, 512, 256, 128, 64, 32, 16):
        if rows % cand == 0 and N_DEV * cand * cols * r.dtype.itemsize <= 8 * 1024 * 1024:
            blk = cand
            break

    def body(r_ref, o_ref):
        acc = r_ref[0].astype(F32)
        for d in range(1, N_DEV):
            acc = acc + r_ref[d].astype(F32)
        o_ref[...] = acc

    return pl.pallas_call(
        body, name=name, grid=(rows // blk,), out_shape=jax.ShapeDtypeStruct((rows, cols), F32),
        in_specs=[pl.BlockSpec((N_DEV, blk, cols), lambda i: (0, i, 0))],
        out_specs=pl.BlockSpec((blk, cols), lambda i: (i, 0)), compiler_params=_params(),
    )(r)


def _ffn_fwd(h, g_pre, g_post, wgt, wut, wd, name, xchg=None):
    n_rows, d = h.shape
    d_ff = wd.shape[0]
    tm, fc = _row_tile(n_rows), _ff_chunk(d_ff)
    n_t, n_c = n_rows // tm, d_ff // fc

    def body(h_ref, gpre_ref, gpost_ref, wgt_ref, wut_ref, wd_ref, a_ref, b_ref, f_ref, ho_ref, n_scr, acc):
        c = pl.program_id(1)

        @pl.when(c == 0)
        def _():
            hv = h_ref[...]
            n_scr[...] = (hv * _rs(hv) * gpre_ref[...]).astype(_MXU)
            acc[...] = jnp.zeros_like(acc)

        rows = pl.ds(pl.multiple_of(c * fc, fc), fc)
        nv = n_scr[...]
        a = _dot_nt(nv, wgt_ref[rows, :])
        b = _dot_nt(nv, wut_ref[rows, :])
        a_ref[...] = a.astype(_ACT)
        b_ref[...] = b.astype(_ACT)
        s = a * _sigmoid(a) * b
        acc[...] += _dot(s.astype(_MXU), wd_ref[rows, :])

        @pl.when(c == n_c - 1)
        def _():
            f = acc[...]
            f_ref[...] = f
            ho_ref[...] = h_ref[...] + 0.5 * (f * _rs(f) * gpost_ref[...])

    row = pl.BlockSpec((tm, d), lambda i, c: (i, 0))
    chunk = pl.BlockSpec((tm, fc), lambda i, c: (i, c))
    return _call(
        body, (h, g_pre, g_post, wgt, wut, wd), name=name, grid=(n_t, n_c),
        out_shape=[jax.ShapeDtypeStruct((n_rows, d_ff), _ACT), jax.ShapeDtypeStruct((n_rows, d_ff), _ACT),
                   jax.ShapeDtypeStruct((n_rows, d), F32), jax.ShapeDtypeStruct((n_rows, d), F32)],
        in_specs=[row, _full((1, d)), _full((1, d)), _resident(), _resident(), _resident()],
        out_specs=[chunk, chunk, row, row],
        scratch_shapes=[pltpu.VMEM((tm, d), _MXU), pltpu.VMEM((tm, d), F32)], xchg=xchg)


def _ffn_bwd(dho, h, f, a, b, g_pre, g_post, wgt, wut, wd, name, xchg=None):
    n_rows, d = h.shape
    d_ff = wd.shape[0]
    tm, fc = _row_tile(n_rows, 304), _ff_chunk(d_ff)
    n_t, n_c = n_rows // tm, d_ff // fc

    def body(dho_ref, h_ref, f_ref, a_ref, b_ref, gpre_ref, gpost_ref, wgt_ref, wut_ref, wd_ref,
             dh_ref, da_ref, db_ref, s_ref, df_ref, n_ref, dgpre_ref, dgpost_ref, dn_acc):
        i, c = pl.program_id(0), pl.program_id(1)

        @pl.when((i == 0) & (c == 0))
        def _():
            dgpre_ref[...] = jnp.zeros_like(dgpre_ref)
            dgpost_ref[...] = jnp.zeros_like(dgpost_ref)

        @pl.when(c == 0)
        def _():
            fv = f_ref[...]
            rf = _rs(fv)
            fhat = fv * rf
            dy = 0.5 * dho_ref[...]
            dgpost_ref[...] += jnp.sum(dy * fhat, axis=0, keepdims=True)
            dfhat = dy * gpost_ref[...]
            df = rf * (dfhat - fhat * jnp.mean(dfhat * fhat, axis=-1, keepdims=True))
            df_ref[...] = df.astype(_MXU)
            hv = h_ref[...]
            n_ref[...] = (hv * _rs(hv) * gpre_ref[...]).astype(_MXU)
            dn_acc[...] = jnp.zeros_like(dn_acc)

        rows = pl.ds(pl.multiple_of(c * fc, fc), fc)
        ds = _dot_nt(df_ref[...], wd_ref[rows, :])
        av = a_ref[...].astype(F32)
        bv = b_ref[...].astype(F32)
        sg = _sigmoid(av)
        si = av * sg
        da = (ds * bv * (sg * (1.0 + av * (1.0 - sg)))).astype(_MXU)
        db = (ds * si).astype(_MXU)
        da_ref[...] = da
        db_ref[...] = db
        s_ref[...] = (si * bv).astype(_MXU)
        dn_acc[...] += _dot(da, wgt_ref[rows, :]) + _dot(db, wut_ref[rows, :])

        @pl.when(c == n_c - 1)
        def _():
            dn = dn_acc[...]
            hv = h_ref[...]
            r = _rs(hv)
            hhat = hv * r
            dgpre_ref[...] += jnp.sum(dn * hhat, axis=0, keepdims=True)
            dhh = dn * gpre_ref[...]
            dh_ref[...] = dho_ref[...] + r * (dhh - hhat * jnp.mean(dhh * hhat, axis=-1, keepdims=True))

    row = pl.BlockSpec((tm, d), lambda i, c: (i, 0))
    chunk = pl.BlockSpec((tm, fc), lambda i, c: (i, c))
    vec = pl.BlockSpec((1, d), lambda i, c: (0, 0))
    return _call(
        body, (dho, h, f, a, b, g_pre, g_post, wgt, wut, wd), name=name, grid=(n_t, n_c),
        out_shape=[jax.ShapeDtypeStruct((n_rows, d), F32)] + [jax.ShapeDtypeStruct((n_rows, d_ff), _MXU)] * 3
        + [jax.ShapeDtypeStruct((n_rows, d), _MXU)] * 2 + [jax.ShapeDtypeStruct((1, d), F32)] * 2,
        in_specs=[row, row, row, chunk, chunk, vec, vec, _resident(), _resident(), _resident()],
        out_specs=[row, chunk, chunk, chunk, row, row, vec, vec],
        scratch_shapes=[pltpu.VMEM((tm, d), F32)], xchg=xchg)


def _wgrad(xm, ym, name, xchg=None):
    n_rows, a_dim = xm.shape
    b_dim = ym.shape[1]
    tk = n_rows
    for cand in (2736, 1296, 432, 48, 16):
        if n_rows % cand == 0:
            tk = cand
            break
    ta = a_dim
    for cand in (1408, 1024, 512):
        if a_dim % cand == 0:
            ta = cand
            break

    n_k = n_rows // tk

    def body(x_ref, y_ref, o_ref, acc):
        k = pl.program_id(1)

        @pl.when(k == 0)
        def _():
            acc[...] = jnp.zeros_like(acc)

        acc[...] += _dot_tn(x_ref[...], y_ref[...])

        @pl.when(k == n_k - 1)
        def _():
            o_ref[...] = acc[...].astype(o_ref.dtype)

    (out,), extra = _call(
        body, (xm, ym), name=name, grid=(a_dim // ta, n_k), out_shape=[jax.ShapeDtypeStruct((a_dim, b_dim), _WIRE)],
        in_specs=[pl.BlockSpec((tk, ta), lambda j, k: (k, j)), pl.BlockSpec((tk, b_dim), lambda j, k: (k, 0))],
        out_specs=[pl.BlockSpec((ta, b_dim), lambda j, k: (j, 0))], scratch_shapes=[pltpu.VMEM((ta, b_dim), F32)],
        xchg=xchg)
    return out, extra


def _mix_in_fwd(h, g, w_in, name):
    n_rows, d = h.shape
    tm = _row_tile(n_rows)
    width = w_in.shape[1] // 2
    n_blk = width // LANES

    def body(h_ref, g_ref, w_ref, n_ref, us_ref, up_ref):
        hv = h_ref[...]
        nv = (hv * _rs(hv) * g_ref[...]).astype(_MXU)
        n_ref[...] = nv
        p = _dot(nv, w_ref[...])
        for k in range(n_blk):
            us_ref[k] = p[:, k * LANES:(k + 1) * LANES]
        up_ref[...] = p[:, width:]

    row = pl.BlockSpec((tm, d), lambda i: (i, 0))
    half = pl.BlockSpec((tm, width), lambda i: (i, 0))
    return pl.pallas_call(
        body, name=name, grid=(n_rows // tm,),
        out_shape=[jax.ShapeDtypeStruct((n_rows, d), _MXU), jax.ShapeDtypeStruct((n_blk, n_rows, LANES), F32),
                   jax.ShapeDtypeStruct((n_rows, width), F32)],
        in_specs=[row, _full((1, d)), _resident()], out_specs=[row, _lane_blocks(n_blk, tm), half],
        compiler_params=_params(),
    )(h, g, w_in)


def _gelu_parts(y):
    th = jnp.tanh(GELU_C0 * (y + GELU_C1 * (y * y * y)))
    return 0.5 * (1.0 + th), th


def _scan_blocks(n_blocks, width, coef_ref, load, store, reverse):
    n_cols = coef_ref.shape[2] // width
    for cb in range(n_cols):
        cols = pl.ds(cb * width, width)

        def step(t, carry, cols=cols):
            tb = (n_blocks - 1 - t) if reverse else t
            r0 = pl.multiple_of(tb * SUBLANES, SUBLANES)
            vr, vi = load(r0, cols)
            xr, xi = vr, vi
            for lvl, k in enumerate((1, 2, 4)):
                kr, ki = coef_ref[2 * lvl, :, cols], coef_ref[2 * lvl + 1, :, cols]
                shift = SUBLANES - k if reverse else k
                sr, si = pltpu.roll(xr, shift, 0), pltpu.roll(xi, shift, 0)
                xr, xi = xr + (kr * sr - ki * si), xi + (kr * si + ki * sr)
            pr, pi = coef_ref[6, :, cols], coef_ref[7, :, cols]
            cr, ci = carry[0], carry[1]
            xr, xi = xr + (pr * cr - pi * ci), xi + (pr * ci + pi * cr)
            extra = store(r0, cols, xr, xi, vr, vi, carry[2:])
            edge = 0 if reverse else SUBLANES - 1
            return (xr[edge:edge + 1, :], xi[edge:edge + 1, :]) + tuple(extra)

        yield cb, cols, step


def _ssm_fwd(proj, coef, b_re, b_im, c_re, c_im, wz1, wz2, d_skip, g_out, name, xchg=None):
    n_rows = proj.shape[0]
    n_blk, _, n_state = b_re.shape
    width = n_blk * LANES
    tm = _row_tile(n_rows)
    n_tb = tm // SUBLANES

    def body(u_ref, coef_ref, bre_ref, bim_ref, cre_ref, cim_ref, wz1_ref, wz2_ref, d_ref, g_ref,
             xre_ref, xim_ref, o_ref, car_re, car_im, out_scr):
        @pl.when(pl.program_id(0) == 0)
        def _():
            car_re[...] = jnp.zeros_like(car_re)
            car_im[...] = jnp.zeros_like(car_im)

        ub = u_ref[...].astype(_MXU)
        for j in range(n_blk):
            uj = ub[:, j * LANES:(j + 1) * LANES]
            xre_ref[:, j * n_state:(j + 1) * n_state] = _dot(uj, bre_ref[j])
            xim_ref[:, j * n_state:(j + 1) * n_state] = _dot(uj, bim_ref[j])

        def load(r0, cols):
            return xre_ref[pl.ds(r0, SUBLANES), cols], xim_ref[pl.ds(r0, SUBLANES), cols]

        def store(r0, cols, xr, xi, vr, vi, extra):
            xre_ref[pl.ds(r0, SUBLANES), cols] = xr
            xim_ref[pl.ds(r0, SUBLANES), cols] = xi
            return ()

        for cb, cols, step in _scan_blocks(n_tb, 512, coef_ref, load, store, reverse=False):
            fin = lax.fori_loop(0, n_tb, step, (car_re[:, cols], car_im[:, cols]))
            car_re[:, cols] = fin[0]
            car_im[:, cols] = fin[1]

        ssq = None
        for j in range(n_blk):
            sl = slice(j * LANES, (j + 1) * LANES)
            st = slice(j * n_state, (j + 1) * n_state)
            yc = _dot(xre_ref[:, st].astype(_MXU), cre_ref[j]) - _dot(xim_ref[:, st].astype(_MXU), cim_ref[j])
            y = yc + d_ref[:, sl] * u_ref[:, sl]
            cdf, _ = _gelu_parts(y)
            gy = (y * cdf).astype(_MXU)
            out = _dot(gy, wz1_ref[j]) * _sigmoid(_dot(gy, wz2_ref[j]))
            out_scr[:, sl] = out
            part = jnp.sum(out * out, axis=-1, keepdims=True)
            ssq = part if ssq is None else ssq + part
        r = lax.rsqrt(ssq / width + RMS_EPS)
        o_ref[...] = (out_scr[...] * r * g_ref[...]).astype(_MXU)

    half = pl.BlockSpec((tm, width), lambda i: (i, 0))
    state = pl.BlockSpec((tm, n_blk * n_state), lambda i: (i, 0))
    return _call(
        body, (proj, coef, b_re, b_im, c_re, c_im, wz1, wz2, d_skip, g_out), name=name, grid=(n_rows // tm,),
        out_shape=[jax.ShapeDtypeStruct((n_rows, n_blk * n_state), F32)] * 2 + [jax.ShapeDtypeStruct((n_rows, width), _MXU)],
        in_specs=[half, _full(coef.shape), _full(b_re.shape), _full(b_im.shape), _full(c_re.shape), _full(c_im.shape),
                  _full(wz1.shape), _full(wz2.shape), _full((1, width)), _full((1, width))],
        out_specs=[state, state, half],
        scratch_shapes=[pltpu.VMEM((1, n_blk * n_state), F32)] * 2 + [pltpu.VMEM((tm, width), F32)], xchg=xchg)


def _ssm_bwd(proj, x_re, x_im, dcat, coef_rev, b_re, b_im, c_re, c_im, wz1, wz2, d_skip, g_out, name, xchg=None):
    n_rows = proj.shape[0]
    n_blk, _, n_state = b_re.shape
    width = n_blk * LANES
    n_all = n_blk * n_state
    tm = _row_tile(n_rows)
    n_t, n_tb = n_rows // tm, tm // SUBLANES

    def body(u_ref, xre_ref, xim_ref, dc_ref, coef_ref, bre_ref, bim_ref, cre_ref, cim_ref, wz1_ref, wz2_ref, d_ref, g_ref,
             du_ref, dg_ref, dd_ref, dwz1_ref, dwz2_ref, dcre_ref, dcim_ref, dbre_ref, dbim_ref, are_ref, aim_ref,
             car_re, car_im, gre, gim, y_s, z1_s, sg_s, out_s, gy_s):
        @pl.when(pl.program_id(0) == 0)
        def _():
            for ref in (dg_ref, dd_ref, dwz1_ref, dwz2_ref, dcre_ref, dcim_ref, dbre_ref, dbim_ref, are_ref, aim_ref,
                        car_re, car_im):
                ref[...] = jnp.zeros_like(ref)

        ssq = None
        for j in range(n_blk):
            sl = slice(j * LANES, (j + 1) * LANES)
            st = slice(j * n_state, (j + 1) * n_state)
            yc = _dot(xre_ref[:, st].astype(_MXU), cre_ref[j]) - _dot(xim_ref[:, st].astype(_MXU), cim_ref[j])
            y = yc + d_ref[:, sl] * u_ref[:, sl]
            cdf, _ = _gelu_parts(y)
            gy = (y * cdf).astype(_MXU)
            z1 = _dot(gy, wz1_ref[j])
            sg = _sigmoid(_dot(gy, wz2_ref[j]))
            out = z1 * sg
            y_s[:, sl], z1_s[:, sl], sg_s[:, sl], out_s[:, sl], gy_s[:, sl] = y, z1, sg, out, gy
            part = jnp.sum(out * out, axis=-1, keepdims=True)
            ssq = part if ssq is None else ssq + part
        r = lax.rsqrt(ssq / width + RMS_EPS)
        ohat = out_s[...] * r
        dcv = dc_ref[...]
        dg_ref[...] += jnp.sum(dcv * ohat, axis=0, keepdims=True)
        doh = dcv * g_ref[...]
        out_s[...] = r * (doh - ohat * (jnp.sum(doh * ohat, axis=-1, keepdims=True) / width))

        for j in range(n_blk):
            sl = slice(j * LANES, (j + 1) * LANES)
            st = slice(j * n_state, (j + 1) * n_state)
            dout, sg, z1, y = out_s[:, sl], sg_s[:, sl], z1_s[:, sl], y_s[:, sl]
            dz1 = (dout * sg).astype(_MXU)
            dz2 = (dout * z1 * sg * (1.0 - sg)).astype(_MXU)
            gy = gy_s[:, sl]
            dwz1_ref[j] += _dot_tn(gy, dz1)
            dwz2_ref[j] += _dot_tn(gy, dz2)
            dgy = _dot_nt(dz1, wz1_ref[j]) + _dot_nt(dz2, wz2_ref[j])
            cdf, th = _gelu_parts(y)
            dy = dgy * (cdf + y * (0.5 * (1.0 - th * th) * GELU_C0 * (1.0 + 3.0 * GELU_C1 * (y * y))))
            uj = u_ref[:, sl]
            dd_ref[:, sl] += jnp.sum(dy * uj, axis=0, keepdims=True)
            du_ref[:, sl] = d_ref[:, sl] * dy
            dyb = dy.astype(_MXU)
            dcre_ref[j] += _dot_tn(dyb, xre_ref[:, st].astype(_MXU))
            dcim_ref[j] -= _dot_tn(dyb, xim_ref[:, st].astype(_MXU))
            gre[:, st] = _dot_nt(dyb, cre_ref[j])
            gim[:, st] = -_dot_nt(dyb, cim_ref[j])

        def load(r0, cols):
            return gre[pl.ds(r0, SUBLANES), cols], gim[pl.ds(r0, SUBLANES), cols]

        def store(r0, cols, gr, gi, vr, vi, extra):
            gre[pl.ds(r0, SUBLANES), cols] = gr
            gim[pl.ds(r0, SUBLANES), cols] = gi
            hr, hi = gr - vr, gi - vi
            xr, xi = xre_ref[pl.ds(r0, SUBLANES), cols], xim_ref[pl.ds(r0, SUBLANES), cols]
            return extra[0] + (xr * hr + xi * hi), extra[1] + (xr * hi - xi * hr)

        for cb, cols, step in _scan_blocks(n_tb, 512, coef_ref, load, store, reverse=True):
            zero = jnp.zeros((SUBLANES, 512), F32)
            fin = lax.fori_loop(0, n_tb, step, (car_re[:, cols], car_im[:, cols], zero, zero))
            car_re[:, cols] = fin[0]
            car_im[:, cols] = fin[1]
            are_ref[:, cols] += jnp.sum(fin[2], axis=0, keepdims=True)
            aim_ref[:, cols] += jnp.sum(fin[3], axis=0, keepdims=True)

        for j in range(n_blk):
            sl = slice(j * LANES, (j + 1) * LANES)
            st = slice(j * n_state, (j + 1) * n_state)
            ujb = u_ref[:, sl].astype(_MXU)
            grb, gib = gre[:, st].astype(_MXU), gim[:, st].astype(_MXU)
            dbre_ref[j] += _dot_tn(ujb, grb)
            dbim_ref[j] += _dot_tn(ujb, gib)
            du_ref[:, sl] += _dot_nt(grb, bre_ref[j]) + _dot_nt(gib, bim_ref[j])

    half = pl.BlockSpec((tm, width), lambda i: (n_t - 1 - i, 0))
    state = pl.BlockSpec((tm, n_all), lambda i: (n_t - 1 - i, 0))
    small = [(1, width), (1, width), wz1.shape, wz2.shape, (n_blk, LANES, n_state), (n_blk, LANES, n_state),
             (n_blk, LANES, n_state), (n_blk, LANES, n_state), (1, n_all), (1, n_all)]
    return _call(
        body, (proj, x_re, x_im, dcat, coef_rev, b_re, b_im, c_re, c_im, wz1, wz2, d_skip, g_out), name=name, grid=(n_t,),
        out_shape=[jax.ShapeDtypeStruct((n_rows, width), F32)] + [jax.ShapeDtypeStruct(s, F32) for s in small],
        in_specs=[half, state, state, half, _full(coef_rev.shape), _full(b_re.shape), _full(b_im.shape), _full(c_re.shape),
                  _full(c_im.shape), _full(wz1.shape), _full(wz2.shape), _full((1, width)), _full((1, width))],
        out_specs=[half] + [_full(s) for s in small],
        scratch_shapes=[pltpu.VMEM((1, n_all), F32)] * 2 + [pltpu.VMEM((tm, n_all), F32)] * 2
        + [pltpu.VMEM((tm, width), F32)] * 4 + [pltpu.VMEM((tm, width), _MXU)], xchg=xchg)


SCAN_LANES = 512


def _regroup(src_ref, dst_ref, seg):
    for k in range(src_ref.shape[0]):
        for j in range(seg):
            dst_ref[j * SUBLANES:(j + 1) * SUBLANES, k * LANES:(k + 1) * LANES] = src_ref[k, pl.ds(j, SUBLANES, stride=seg), :]


def _ungroup(src_ref, dst_ref, seg):
    for k in range(dst_ref.shape[0]):
        for j in range(seg):
            dst_ref[k, pl.ds(j, SUBLANES, stride=seg), :] = src_ref[j * SUBLANES:(j + 1) * SUBLANES, k * LANES:(k + 1) * LANES]


def _rows_to_sublanes(rows):
    rid = lax.broadcasted_iota(jnp.int32, (SUBLANES, rows[0].shape[1]), 0)
    out = jnp.broadcast_to(rows[0], rid.shape)
    for s in range(1, SUBLANES):
        out = jnp.where(rid == s, rows[s], out)
    return out


def _segment_scan(re_ref, im_ref, tab_ref, car_re, car_im, seg, reverse, x_refs=None):
    n_all = re_ref.shape[1]
    first = seg - 1 if reverse else 0
    sums = []
    for cb in range(n_all // SCAN_LANES):
        cols = pl.ds(cb * SCAN_LANES, SCAN_LANES)
        ar, ai = tab_ref[0, first, :, cols], tab_ref[1, first, :, cols]

        def local(t, carry, cols=cols, ar=ar, ai=ai):
            r0 = pl.multiple_of((seg - 1 - t if reverse else t) * SUBLANES, SUBLANES)
            xr, xi = carry
            nr = ar * xr - ai * xi + re_ref[pl.ds(r0, SUBLANES), cols]
            ni = ar * xi + ai * xr + im_ref[pl.ds(r0, SUBLANES), cols]
            re_ref[pl.ds(r0, SUBLANES), cols] = nr
            im_ref[pl.ds(r0, SUBLANES), cols] = ni
            return nr, ni

        zero = jnp.zeros((SUBLANES, SCAN_LANES), F32)
        fr, fi = lax.fori_loop(0, seg, local, (zero, zero))

        last = 0 if reverse else seg - 1
        sr, si = tab_ref[0, last, 0:1, cols], tab_ref[1, last, 0:1, cols]
        cr, ci = car_re[:, cols], car_im[:, cols]
        rows_r, rows_i = [None] * SUBLANES, [None] * SUBLANES
        for s in (range(SUBLANES - 1, -1, -1) if reverse else range(SUBLANES)):
            rows_r[s], rows_i[s] = cr, ci
            cr, ci = sr * cr - si * ci + fr[s:s + 1], sr * ci + si * cr + fi[s:s + 1]
        car_re[:, cols] = cr
        car_im[:, cols] = ci
        cmr, cmi = _rows_to_sublanes(rows_r), _rows_to_sublanes(rows_i)

        def fix(t, carry, cols=cols, cmr=cmr, cmi=cmi):
            j = seg - 1 - t if reverse else t
            r0 = pl.multiple_of(j * SUBLANES, SUBLANES)
            pr, pi = tab_ref[0, j, :, cols], tab_ref[1, j, :, cols]
            gr = re_ref[pl.ds(r0, SUBLANES), cols] + (pr * cmr - pi * cmi)
            gi = im_ref[pl.ds(r0, SUBLANES), cols] + (pr * cmi + pi * cmr)
            re_ref[pl.ds(r0, SUBLANES), cols] = gr
            im_ref[pl.ds(r0, SUBLANES), cols] = gi
            if x_refs is None:
                return carry
            nxr, nxi, accr, acci = carry
            xr, xi = x_refs[0][pl.ds(r0, SUBLANES), cols], x_refs[1][pl.ds(r0, SUBLANES), cols]
            return gr, gi, accr + (xr * nxr + xi * nxi), acci + (xr * nxi - xi * nxr)

        if x_refs is None:
            lax.fori_loop(0, seg, fix, 0)
        else:
            fin = lax.fori_loop(0, seg, fix, (cmr, cmi, zero, zero))
            sums.append((jnp.sum(fin[2], axis=0, keepdims=True), jnp.sum(fin[3], axis=0, keepdims=True)))
    return sums


def _ssm_forward(u, table, b_re, b_im, c_re, c_im, wz1, wz2, d_skip, g_out, name, xchg=None):
    n_rows = u.shape[1]
    n_blk, _, n_state = b_re.shape
    width, n_all = n_blk * LANES, n_blk * n_state
    tm = _row_tile(n_rows)
    seg = tm // SUBLANES

    def body(u_ref, tab_ref, bre_ref, bim_ref, cre_ref, cim_ref, wz1_ref, wz2_ref, d_ref, g_ref,
             xre_ref, xim_ref, o_ref, car_re, car_im, ug, out_scr, blocks):
        @pl.when(pl.program_id(0) == 0)
        def _():
            car_re[...] = jnp.zeros_like(car_re)
            car_im[...] = jnp.zeros_like(car_im)

        _regroup(u_ref, ug, seg)
        ub = ug[...].astype(_MXU)
        for j in range(n_blk):
            uj = ub[:, j * LANES:(j + 1) * LANES]
            xre_ref[:, j * n_state:(j + 1) * n_state] = _dot(uj, bre_ref[j])
            xim_ref[:, j * n_state:(j + 1) * n_state] = _dot(uj, bim_ref[j])
        _segment_scan(xre_ref, xim_ref, tab_ref, car_re, car_im, seg, reverse=False)

        ssq = None
        for j in range(n_blk):
            sl = slice(j * LANES, (j + 1) * LANES)
            st = slice(j * n_state, (j + 1) * n_state)
            yc = _dot(xre_ref[:, st].astype(_MXU), cre_ref[j]) - _dot(xim_ref[:, st].astype(_MXU), cim_ref[j])
            y = yc + d_ref[:, sl] * ug[:, sl]
            cdf, _ = _gelu_parts(y)
            gy = (y * cdf).astype(_MXU)
            out = _dot(gy, wz1_ref[j]) * _sigmoid(_dot(gy, wz2_ref[j]))
            out_scr[:, sl] = out
            part = jnp.sum(out * out, axis=-1, keepdims=True)
            ssq = part if ssq is None else ssq + part
        r = lax.rsqrt(ssq / width + RMS_EPS)
        out_scr[...] = out_scr[...] * r * g_ref[...]
        _ungroup(out_scr, blocks, seg)
        for k in range(n_blk):
            o_ref[:, k * LANES:(k + 1) * LANES] = blocks[k].astype(_MXU)

    half = pl.BlockSpec((tm, width), lambda i: (i, 0))
    state = pl.BlockSpec((tm, n_all), lambda i: (i, 0))
    return _call(
        body, (u, table, b_re, b_im, c_re, c_im, wz1, wz2, d_skip, g_out), name=name, grid=(n_rows // tm,),
        out_shape=[jax.ShapeDtypeStruct((n_rows, n_all), F32)] * 2 + [jax.ShapeDtypeStruct((n_rows, width), _MXU)],
        in_specs=[_lane_blocks(n_blk, tm), _resident(), _full(b_re.shape), _full(b_im.shape), _full(c_re.shape),
                  _full(c_im.shape), _full(wz1.shape), _full(wz2.shape), _full((1, width)), _full((1, width))],
        out_specs=[state, state, half],
        scratch_shapes=[pltpu.VMEM((1, n_all), F32)] * 2 + [pltpu.VMEM((tm, width), F32)] * 2
        + [pltpu.VMEM((n_blk, tm, LANES), F32)], xchg=xchg)


def _ssm_backward(u, x_re, x_im, dcat, table, b_re, b_im, c_re, c_im, wz1, wz2, d_skip, g_out, name, xchg=None):
    n_rows = u.shape[1]
    n_blk, _, n_state = b_re.shape
    width, n_all = n_blk * LANES, n_blk * n_state
    tm = _row_tile(n_rows)
    n_t, seg = n_rows // tm, tm // SUBLANES

    def body(u_ref, xre_ref, xim_ref, dc_ref, tab_ref, bre_ref, bim_ref, cre_ref, cim_ref, wz1_ref, wz2_ref, d_ref, g_ref,
             du_ref, dg_ref, dd_ref, dwz1_ref, dwz2_ref, dcre_ref, dcim_ref, dbre_ref, dbim_ref, are_ref, aim_ref,
             car_re, car_im, gre, gim, ug, y_s, z1_s, sg_s, out_s, gy_s):
        @pl.when(pl.program_id(0) == 0)
        def _():
            for ref in (dg_ref, dd_ref, dwz1_ref, dwz2_ref, dcre_ref, dcim_ref, dbre_ref, dbim_ref, are_ref, aim_ref,
                        car_re, car_im):
                ref[...] = jnp.zeros_like(ref)

        _regroup(u_ref, ug, seg)
        ssq = None
        for j in range(n_blk):
            sl = slice(j * LANES, (j + 1) * LANES)
            st = slice(j * n_state, (j + 1) * n_state)
            yc = _dot(xre_ref[:, st].astype(_MXU), cre_ref[j]) - _dot(xim_ref[:, st].astype(_MXU), cim_ref[j])
            y = yc + d_ref[:, sl] * ug[:, sl]
            cdf, _ = _gelu_parts(y)
            gy = (y * cdf).astype(_MXU)
            z1 = _dot(gy, wz1_ref[j])
            sg = _sigmoid(_dot(gy, wz2_ref[j]))
            out = z1 * sg
            y_s[:, sl], z1_s[:, sl], sg_s[:, sl], out_s[:, sl], gy_s[:, sl] = y, z1, sg, out, gy
            part = jnp.sum(out * out, axis=-1, keepdims=True)
            ssq = part if ssq is None else ssq + part
        r = lax.rsqrt(ssq / width + RMS_EPS)
        ohat = out_s[...] * r
        _regroup(dc_ref, out_s, seg)
        dcv = out_s[...]
        dg_ref[...] += jnp.sum(dcv * ohat, axis=0, keepdims=True)
        doh = dcv * g_ref[...]
        out_s[...] = r * (doh - ohat * (jnp.sum(doh * ohat, axis=-1, keepdims=True) / width))

        for j in range(n_blk):
            sl = slice(j * LANES, (j + 1) * LANES)
            st = slice(j * n_state, (j + 1) * n_state)
            dout, sg, z1, y = out_s[:, sl], sg_s[:, sl], z1_s[:, sl], y_s[:, sl]
            dz1 = (dout * sg).astype(_MXU)
            dz2 = (dout * z1 * sg * (1.0 - sg)).astype(_MXU)
            gy = gy_s[:, sl]
            dwz1_ref[j] += _dot_tn(gy, dz1)
            dwz2_ref[j] += _dot_tn(gy, dz2)
            dgy = _dot_nt(dz1, wz1_ref[j]) + _dot_nt(dz2, wz2_ref[j])
            cdf, th = _gelu_parts(y)
            dy = dgy * (cdf + y * (0.5 * (1.0 - th * th) * GELU_C0 * (1.0 + 3.0 * GELU_C1 * (y * y))))
            uj = ug[:, sl]
            dd_ref[:, sl] += jnp.sum(dy * uj, axis=0, keepdims=True)
            z1_s[:, sl] = d_ref[:, sl] * dy
            dyb = dy.astype(_MXU)
            dcre_ref[j] += _dot_tn(dyb, xre_ref[:, st].astype(_MXU))
            dcim_ref[j] -= _dot_tn(dyb, xim_ref[:, st].astype(_MXU))
            gre[:, st] = _dot_nt(dyb, cre_ref[j])
            gim[:, st] = -_dot_nt(dyb, cim_ref[j])

        sums = _segment_scan(gre, gim, tab_ref, car_re, car_im, seg, reverse=True, x_refs=(xre_ref, xim_ref))
        for cb, (sum_re, sum_im) in enumerate(sums):
            cols = pl.ds(cb * SCAN_LANES, SCAN_LANES)
            are_ref[:, cols] += sum_re
            aim_ref[:, cols] += sum_im

        for j in range(n_blk):
            sl = slice(j * LANES, (j + 1) * LANES)
            st = slice(j * n_state, (j + 1) * n_state)
            ujb = ug[:, sl].astype(_MXU)
            grb, gib = gre[:, st].astype(_MXU), gim[:, st].astype(_MXU)
            dbre_ref[j] += _dot_tn(ujb, grb)
            dbim_ref[j] += _dot_tn(ujb, gib)
            z1_s[:, sl] += _dot_nt(grb, bre_ref[j]) + _dot_nt(gib, bim_ref[j])
        _ungroup(z1_s, du_ref, seg)

    half = _lane_blocks(n_blk, tm, lambda i: n_t - 1 - i)
    state = pl.BlockSpec((tm, n_all), lambda i: (n_t - 1 - i, 0))
    small = [(1, width), (1, width), wz1.shape, wz2.shape, (n_blk, LANES, n_state), (n_blk, LANES, n_state),
             (n_blk, LANES, n_state), (n_blk, LANES, n_state), (1, n_all), (1, n_all)]
    return _call(
        body, (u, x_re, x_im, dcat, table, b_re, b_im, c_re, c_im, wz1, wz2, d_skip, g_out), name=name, grid=(n_t,),
        out_shape=[jax.ShapeDtypeStruct((n_blk, n_rows, LANES), F32)] + [jax.ShapeDtypeStruct(s, F32) for s in small],
        in_specs=[half, state, state, half, _resident(), _full(b_re.shape), _full(b_im.shape), _full(c_re.shape),
                  _full(c_im.shape), _full(wz1.shape), _full(wz2.shape), _full((1, width)), _full((1, width))],
        out_specs=[half] + [_full(s) for s in small],
        scratch_shapes=[pltpu.VMEM((1, n_all), F32)] * 2 + [pltpu.VMEM((tm, n_all), F32)] * 2
        + [pltpu.VMEM((tm, width), F32)] * 5 + [pltpu.VMEM((tm, width), _MXU)], xchg=xchg)


def _pool_counts(tile, tm, window):
    t = tile * tm + lax.broadcasted_iota(jnp.int32, (tm, 1), 0)
    return jnp.minimum(t + 1, window).astype(F32)


def _pool_fwd(proj, pool_w, scale, g_out, name):
    n_rows = proj.shape[0]
    n_grp, grp, _ = pool_w.shape
    width = n_grp * grp
    tm = _row_tile(n_rows)

    def body(u_ref, pw_ref, sc_ref, g_ref, o_ref, ext, y_s):
        i = pl.program_id(0)

        @pl.when(i == 0)
        def _():
            ext[0:POOL_HALO, :] = jnp.zeros((POOL_HALO, width), F32)

        ext[POOL_HALO:, :] = u_ref[...]
        ssq = None
        for gi, w in enumerate(POOL_WINDOWS):
            sl = slice(gi * grp, (gi + 1) * grp)
            tot = ext[POOL_HALO:, sl]
            for k in range(1, w):
                tot = tot + ext[POOL_HALO - k:POOL_HALO - k + tm, sl]
            pooled = tot / _pool_counts(i, tm, w) - u_ref[:, sl]
            y = _dot(pooled.astype(_MXU), pw_ref[gi]) * sc_ref[:, sl]
            y_s[:, sl] = y
            part = jnp.sum(y * y, axis=-1, keepdims=True)
            ssq = part if ssq is None else ssq + part
        r = lax.rsqrt(ssq / width + RMS_EPS)
        o_ref[...] = (y_s[...] * r * g_ref[...]).astype(_MXU)
        ext[0:POOL_HALO, :] = u_ref[tm - POOL_HALO:, :]

    half_in = pl.BlockSpec((tm, width), lambda i: (i, 0))
    half = pl.BlockSpec((tm, width), lambda i: (i, 0))
    return pl.pallas_call(
        body, name=name, grid=(n_rows // tm,), out_shape=jax.ShapeDtypeStruct((n_rows, width), _MXU),
        in_specs=[half_in, _full(pool_w.shape), _full((1, width)), _full((1, width))], out_specs=half,
        scratch_shapes=[pltpu.VMEM((tm + POOL_HALO, width), F32), pltpu.VMEM((tm, width), F32)],
        compiler_params=_params(),
    )(proj, pool_w, scale, g_out)


def _pool_bwd(proj, dcat, pool_w, scale, g_out, name):
    n_rows = proj.shape[0]
    n_grp, grp, _ = pool_w.shape
    width = n_grp * grp
    tm = _row_tile(n_rows)
    n_t = n_rows // tm
    halo_blocks = tm // POOL_HALO

    def body(u_ref, up_ref, dc_ref, pw_ref, sc_ref, g_ref, du_ref, dg_ref, dsc_ref, dpw_ref, ext, qext, y_s, pl_s):
        i = pl.program_id(0)
        tile = n_t - 1 - i

        @pl.when(i == 0)
        def _():
            for ref in (dg_ref, dsc_ref, dpw_ref):
                ref[...] = jnp.zeros_like(ref)
            qext[tm:, :] = jnp.zeros((POOL_HALO, width), F32)

        ext[0:POOL_HALO, :] = jnp.where(tile > 0, up_ref[...], 0.0)
        ext[POOL_HALO:, :] = u_ref[...]
        ssq = None
        for gi, w in enumerate(POOL_WINDOWS):
            sl = slice(gi * grp, (gi + 1) * grp)
            tot = ext[POOL_HALO:, sl]
            for k in range(1, w):
                tot = tot + ext[POOL_HALO - k:POOL_HALO - k + tm, sl]
            pooled = (tot / _pool_counts(tile, tm, w) - u_ref[:, sl]).astype(_MXU)
            pl_s[:, sl] = pooled
            y0 = _dot(pooled, pw_ref[gi])
            y_s[:, sl] = y0
            y = y0 * sc_ref[:, sl]
            part = jnp.sum(y * y, axis=-1, keepdims=True)
            ssq = part if ssq is None else ssq + part
        r = lax.rsqrt(ssq / width + RMS_EPS)
        y0 = y_s[...]
        yhat = y0 * sc_ref[...] * r
        dcv = dc_ref[...]
        dg_ref[...] += jnp.sum(dcv * yhat, axis=0, keepdims=True)
        dyh = dcv * g_ref[...]
        dy = r * (dyh - yhat * (jnp.sum(dyh * yhat, axis=-1, keepdims=True) / width))
        dsc_ref[...] += jnp.sum(dy * y0, axis=0, keepdims=True)
        y_s[...] = dy * sc_ref[...]
        for gi, w in enumerate(POOL_WINDOWS):
            sl = slice(gi * grp, (gi + 1) * grp)
            dm = y_s[:, sl].astype(_MXU)
            dpw_ref[gi] += _dot_tn(pl_s[:, sl], dm)
            dpooled = _dot_nt(dm, pw_ref[gi])
            y_s[:, sl] = dpooled
            qext[0:tm, sl] = dpooled / _pool_counts(tile, tm, w)
        for gi, w in enumerate(POOL_WINDOWS):
            sl = slice(gi * grp, (gi + 1) * grp)
            tot = qext[0:tm, sl]
            for k in range(1, w):
                tot = tot + qext[k:k + tm, sl]
            du_ref[:, sl] = tot - y_s[:, sl]
        qext[tm:, :] = qext[0:POOL_HALO, :]

    half_in = pl.BlockSpec((tm, width), lambda i: (n_t - 1 - i, 0))
    prev = pl.BlockSpec((POOL_HALO, width), lambda i: (jnp.maximum((n_t - 1 - i) * halo_blocks - 1, 0), 0))
    half = pl.BlockSpec((tm, width), lambda i: (n_t - 1 - i, 0))
    return pl.pallas_call(
        body, name=name, grid=(n_t,),
        out_shape=[jax.ShapeDtypeStruct((n_rows, width), F32), jax.ShapeDtypeStruct((1, width), F32),
                   jax.ShapeDtypeStruct((1, width), F32), jax.ShapeDtypeStruct(pool_w.shape, F32)],
        in_specs=[half_in, prev, half, _full(pool_w.shape), _full((1, width)), _full((1, width))],
        out_specs=[half, _full((1, width)), _full((1, width)), _full(pool_w.shape)],
        scratch_shapes=[pltpu.VMEM((tm + POOL_HALO, width), F32), pltpu.VMEM((tm + POOL_HALO, width), F32),
                        pltpu.VMEM((tm, width), F32), pltpu.VMEM((tm, width), _MXU)],
        compiler_params=_params(),
    )(proj, proj, dcat, pool_w, scale, g_out)


def _mix_out_fwd(cat_s, cat_p, h, g, wo_s, wo_p, name):
    n_rows, d = h.shape
    width = cat_s.shape[1]
    tm = _row_tile(n_rows)

    def body(cs_ref, cp_ref, h_ref, g_ref, ws_ref, wp_ref, m_ref, ho_ref):
        m = _dot(cs_ref[...], ws_ref[...]) + _dot(cp_ref[...], wp_ref[...])
        m_ref[...] = m
        ho_ref[...] = h_ref[...] + m * _rs(m) * g_ref[...]

    row = pl.BlockSpec((tm, d), lambda i: (i, 0))
    half = pl.BlockSpec((tm, width), lambda i: (i, 0))
    return pl.pallas_call(
        body, name=name, grid=(n_rows // tm,), out_shape=[jax.ShapeDtypeStruct((n_rows, d), F32)] * 2,
        in_specs=[half, half, row, _full((1, d)), _resident(), _resident()], out_specs=[row, row],
        compiler_params=_params(),
    )(cat_s, cat_p, h, g, wo_s, wo_p)


def _mix_out_bwd(dho, mixed, g, wo_s, wo_p, name):
    n_rows, d = mixed.shape
    width = wo_s.shape[0]
    n_blk = width // LANES
    tm = _row_tile(n_rows)

    def body(dho_ref, m_ref, g_ref, ws_ref, wp_ref, dm_ref, dcs_ref, dcp_ref, dg_ref):
        @pl.when(pl.program_id(0) == 0)
        def _():
            dg_ref[...] = jnp.zeros_like(dg_ref)

        m = m_ref[...]
        r = _rs(m)
        mh = m * r
        dy = dho_ref[...]
        dg_ref[...] += jnp.sum(dy * mh, axis=0, keepdims=True)
        dmh = dy * g_ref[...]
        dm = (r * (dmh - mh * jnp.mean(dmh * mh, axis=-1, keepdims=True))).astype(_MXU)
        dm_ref[...] = dm
        dcs = _dot_nt(dm, ws_ref[...])
        for k in range(n_blk):
            dcs_ref[k] = dcs[:, k * LANES:(k + 1) * LANES]
        dcp_ref[...] = _dot_nt(dm, wp_ref[...])

    row = pl.BlockSpec((tm, d), lambda i: (i, 0))
    half = pl.BlockSpec((tm, width), lambda i: (i, 0))
    return pl.pallas_call(
        body, name=name, grid=(n_rows // tm,),
        out_shape=[jax.ShapeDtypeStruct((n_rows, d), _MXU), jax.ShapeDtypeStruct((n_blk, n_rows, LANES), F32),
                   jax.ShapeDtypeStruct((n_rows, width), F32), jax.ShapeDtypeStruct((1, d), F32)],
        in_specs=[row, row, _full((1, d)), _resident(), _resident()],
        out_specs=[row, _lane_blocks(n_blk, tm), half, _full((1, d))], compiler_params=_params(),
    )(dho, mixed, g, wo_s, wo_p)


def _mix_in_bwd(du_s, du_p, h, dho, g, wi_s, wi_p, name):
    n_rows, d = h.shape
    width = du_p.shape[1]
    n_blk = width // LANES
    tm = _row_tile(n_rows)

    def body(dus_ref, dup_ref, h_ref, dho_ref, g_ref, ws_ref, wp_ref, dh_ref, dp_ref, dg_ref):
        @pl.when(pl.program_id(0) == 0)
        def _():
            dg_ref[...] = jnp.zeros_like(dg_ref)

        for k in range(n_blk):
            dp_ref[:, k * LANES:(k + 1) * LANES] = dus_ref[k].astype(_MXU)
        dup = dup_ref[...].astype(_MXU)
        dp_ref[:, width:2 * width] = dup
        dn = _dot_nt(dp_ref[:, 0:width], ws_ref[...]) + _dot_nt(dup, wp_ref[...])
        hv = h_ref[...]
        r = _rs(hv)
        hh = hv * r
        dg_ref[...] += jnp.sum(dn * hh, axis=0, keepdims=True)
        dhh = dn * g_ref[...]
        dh_ref[...] = dho_ref[...] + r * (dhh - hh * jnp.mean(dhh * hh, axis=-1, keepdims=True))

    row = pl.BlockSpec((tm, d), lambda i: (i, 0))
    half = pl.BlockSpec((tm, width), lambda i: (i, 0))
    return pl.pallas_call(
        body, name=name, grid=(n_rows // tm,),
        out_shape=[jax.ShapeDtypeStruct((n_rows, d), F32), jax.ShapeDtypeStruct((n_rows, 2 * width), _MXU),
                   jax.ShapeDtypeStruct((1, d), F32)],
        in_specs=[_lane_blocks(n_blk, tm), half, row, row, _full((1, d)), _resident(), _resident()],
        out_specs=[row, pl.BlockSpec((tm, 2 * width), lambda i: (i, 0)), _full((1, d))], compiler_params=_params(),
    )(du_s, du_p, h, dho, g, wi_s, wi_p)


def _loss_grad(h, target, name):
    n_rows, d = h.shape
    tm = _row_tile(n_rows)

    def body(h_ref, t_ref, dh_ref, l_ref):
        i = pl.program_id(0)

        @pl.when(i == 0)
        def _():
            l_ref[...] = jnp.zeros_like(l_ref)

        rows = i * tm + lax.broadcasted_iota(jnp.int32, (tm, 1), 0)
        err = jnp.where(rows >= N_META, h_ref[...] - t_ref[...], 0.0)
        dh_ref[...] = err / d
        l_ref[...] += jnp.sum(jnp.sum(err * err, axis=0, keepdims=True), axis=1, keepdims=True)

    row = pl.BlockSpec((tm, d), lambda i: (i, 0))
    return pl.pallas_call(
        body, name=name, grid=(n_rows // tm,),
        out_shape=[jax.ShapeDtypeStruct((n_rows, d), F32), jax.ShapeDtypeStruct((1, LANES), F32)],
        in_specs=[row, row], out_specs=[row, _full((1, LANES))], compiler_params=_params(),
    )(h, target)


def _adamw_update(w_ref, gv, m_ref, v_ref, d_ref, mo_ref, vo_ref):
    mn = ADAM_B1 * m_ref[...] + (1.0 - ADAM_B1) * gv
    vn = ADAM_B2 * v_ref[...] + (1.0 - ADAM_B2) * (gv * gv)
    m_hat = mn / (1.0 - ADAM_B1 ** ADAM_STEP)
    v_hat = vn / (1.0 - ADAM_B2 ** ADAM_STEP)
    d_ref[...] = -ADAM_LR * (m_hat / (jnp.sqrt(v_hat) + ADAM_EPS) + ADAM_WD * w_ref[...])
    mo_ref[...] = mn
    vo_ref[...] = vn


def _adamw(w, g, m, v, name):
    def body(w_ref, g_ref, m_ref, v_ref, d_ref, mo_ref, vo_ref):
        _adamw_update(w_ref, g_ref[...], m_ref, v_ref, d_ref, mo_ref, vo_ref)

    spec = _full(w.shape)
    return pl.pallas_call(
        body, name=name, grid=(1,), out_shape=[jax.ShapeDtypeStruct(w.shape, F32)] * 3,
        in_specs=[spec] * 4, out_specs=[spec] * 3, compiler_params=_params(),
    )(w, g, m, v)


def _adamw_slots(w, slots, m, v, name):
    def body(w_ref, s_ref, m_ref, v_ref, g_ref, d_ref, mo_ref, vo_ref):
        gv = s_ref[0].astype(F32)
        for k in range(1, N_DEV):
            gv = gv + s_ref[k].astype(F32)
        g_ref[...] = gv
        _adamw_update(w_ref, gv, m_ref, v_ref, d_ref, mo_ref, vo_ref)

    spec = _full(w.shape)
    return pl.pallas_call(
        body, name=name, grid=(1,), out_shape=[jax.ShapeDtypeStruct(w.shape, F32)] * 4,
        in_specs=[spec, _full(slots.shape), spec, spec], out_specs=[spec] * 4, compiler_params=_params(),
    )(w, slots, m, v)


def _discretize(lam_re, lam_im, log_dt, b_re, b_im):
    dt = jnp.exp(log_dt)[:, None]
    decay = jnp.exp(lam_re * dt)
    ang = lam_im * dt
    a_re = decay * jnp.cos(ang)
    a_im = decay * jnp.sin(ang)
    nr = a_re - 1.0
    den = lam_re * lam_re + lam_im * lam_im
    q_re = (nr * lam_re + a_im * lam_im) / den
    q_im = (a_im * lam_re - nr * lam_im) / den
    bb_re = q_re[..., None] * b_re - q_im[..., None] * b_im
    bb_im = q_re[..., None] * b_im + q_im[..., None] * b_re
    return a_re, a_im, bb_re, bb_im


def _cmul(a, b):
    return a[0] * b[0] - a[1] * b[1], a[0] * b[1] + a[1] * b[0]


def _scan_coefficients(a_re, a_im, reverse):
    a = (a_re.reshape(1, -1), -a_im.reshape(1, -1) if reverse else a_im.reshape(1, -1))
    powers = [a]
    for _ in range(SUBLANES - 1):
        powers.append(_cmul(powers[-1], a))
    row = jnp.arange(SUBLANES)[:, None]
    out = []
    for k in (1, 2, 4):
        keep = (row < SUBLANES - k) if reverse else (row >= k)
        out += [jnp.where(keep, powers[k - 1][0], 0.0), jnp.where(keep, powers[k - 1][1], 0.0)]
    order = range(SUBLANES - 1, -1, -1) if reverse else range(SUBLANES)
    out += [jnp.concatenate([powers[t][0] for t in order], axis=0), jnp.concatenate([powers[t][1] for t in order], axis=0)]
    return jnp.stack(out).astype(F32)


def _power_table(a_re, a_im, count, reverse):
    base = (a_re.reshape(1, -1), (-a_im if reverse else a_im).reshape(1, -1))
    exponent = (jnp.arange(count, 0, -1) if reverse else jnp.arange(1, count + 1))[:, None]
    shape = (count, base[0].shape[1])
    res = (jnp.ones(shape, F32), jnp.zeros(shape, F32))
    for bit in range(int(count).bit_length()):
        prod = _cmul(res, base)
        take = ((exponent >> bit) & 1) == 1
        res = (jnp.where(take, prod[0], res[0]), jnp.where(take, prod[1], res[1]))
        base = _cmul(base, base)
    return jnp.broadcast_to(jnp.stack(res)[:, :, None, :], (2, count, SUBLANES, shape[1]))


def _block_diag(p, n_blk):
    g, r, c = p.shape
    per = g // n_blk
    eye = jnp.eye(per, dtype=p.dtype)
    return jnp.einsum("jgrc,gk->jgrkc", p.reshape(n_blk, per, r, c), eye).reshape(n_blk, per * r, per * c)


def _block_diag_t(m, g):
    n_blk = m.shape[0]
    per = g // n_blk
    r, c = m.shape[1] // per, m.shape[2] // per
    eye = jnp.eye(per, dtype=m.dtype)
    return jnp.einsum("jgrkc,gk->jgrc", m.reshape(n_blk, per, r, per, c), eye).reshape(g, r, c)


def _pack_rows(parts, cols, multiple):
    flat = jnp.concatenate([p.reshape(-1) for p in parts])
    size = -(-flat.shape[0] // (cols * multiple)) * cols * multiple
    return jnp.pad(flat, (0, size - flat.shape[0])).reshape(-1, cols)


def _unpack(flat, shapes):
    out, pos = [], 0
    flat = flat.reshape(-1)
    for s in shapes:
        n = int(np.prod(s))
        out.append(flat[pos:pos + n].reshape(s))
        pos += n
    return out


def kernel(x, meta_tokens, ffn1_pre_norm, ffn1_post_norm, ffn1_w_gate, ffn1_w_up, ffn1_w_down, mix_pre_norm, mix_post_norm, w_in, ssm_lambda_re, ssm_lambda_im, ssm_log_dt, ssm_b_re, ssm_b_im, ssm_c_re, ssm_c_im, ssm_d, ssm_w_glu, pool_w, pool_scale, ssm_out_norm, pool_out_norm, w_out, ffn2_pre_norm, ffn2_post_norm, ffn2_w_gate, ffn2_w_up, ffn2_w_down, loss_target, m_meta_tokens, m_ffn1_pre_norm, m_ffn1_post_norm, m_ffn1_w_gate, m_ffn1_w_up, m_ffn1_w_down, m_mix_pre_norm, m_mix_post_norm, m_w_in, m_ssm_lambda_re, m_ssm_lambda_im, m_ssm_log_dt, m_ssm_b_re, m_ssm_b_im, m_ssm_c_re, m_ssm_c_im, m_ssm_d, m_ssm_w_glu, m_pool_w, m_pool_scale, m_ssm_out_norm, m_pool_out_norm, m_w_out, m_ffn2_pre_norm, m_ffn2_post_norm, m_ffn2_w_gate, m_ffn2_w_up, m_ffn2_w_down, v_meta_tokens, v_ffn1_pre_norm, v_ffn1_post_norm, v_ffn1_w_gate, v_ffn1_w_up, v_ffn1_w_down, v_mix_pre_norm, v_mix_post_norm, v_w_in, v_ssm_lambda_re, v_ssm_lambda_im, v_ssm_log_dt, v_ssm_b_re, v_ssm_b_im, v_ssm_c_re, v_ssm_c_im, v_ssm_d, v_ssm_w_glu, v_pool_w, v_pool_scale, v_ssm_out_norm, v_pool_out_norm, v_w_out, v_ffn2_pre_norm, v_ffn2_post_norm, v_ffn2_w_gate, v_ffn2_w_up, v_ffn2_w_down):
    args = dict(locals())
    names = ["meta_tokens", "ffn1_pre_norm", "ffn1_post_norm", "ffn1_w_gate", "ffn1_w_up", "ffn1_w_down", "mix_pre_norm",
             "mix_post_norm", "w_in", "ssm_lambda_re", "ssm_lambda_im", "ssm_log_dt", "ssm_b_re", "ssm_b_im", "ssm_c_re",
             "ssm_c_im", "ssm_d", "ssm_w_glu", "pool_w", "pool_scale", "ssm_out_norm", "pool_out_norm", "w_out",
             "ffn2_pre_norm", "ffn2_post_norm", "ffn2_w_gate", "ffn2_w_up", "ffn2_w_down"]
    sharded = ("meta_tokens", "ffn1_w_gate", "ffn1_w_up", "ffn1_w_down", "w_in", "w_out", "ffn2_w_gate", "ffn2_w_up",
               "ffn2_w_down")
    small = [n for n in names if n not in sharded]

    d = x.shape[-1]
    width = d // 2
    n_grp = ssm_lambda_re.shape[1]
    n_blk = width // LANES

    def wire(m):
        return m.astype(_WIRE)

    def stacked(gathered):
        return gathered.reshape(-1, d).astype(_MXU)

    def chunked(m):
        return m.reshape(N_DEV, -1, d)

    g_gate1, g_up1, g_down1, meta_all = _gather_two_level(
        [wire(ffn1_w_gate[0].T), wire(ffn1_w_up[0].T), wire(ffn1_w_down[0]), meta_tokens], "gather_ffn1")
    wgt1, wut1, wd1 = stacked(g_gate1), stacked(g_up1), stacked(g_down1)
    meta_full = jnp.transpose(meta_all, (1, 0, 2)).reshape(N_META, d)

    h0 = jnp.concatenate([meta_full, x[0]], axis=0)
    target = jnp.concatenate([jnp.zeros((N_META, d), F32), loss_target[0]], axis=0)

    (a1, b1, f1, h1), (g_win, g_wout) = _ffn_fwd(
        h0, ffn1_pre_norm, ffn1_post_norm, wgt1, wut1, wd1, "ffn1_fwd",
        xchg=_Xchg([wire(w_in[0]), wire(w_out[0])], ["gather"] * 2))
    w_in_f, w_out_f = stacked(g_win), stacked(g_wout)

    a_re, a_im, bb_re, bb_im = _discretize(ssm_lambda_re[0], ssm_lambda_im[0], ssm_log_dt[0], ssm_b_re[0], ssm_b_im[0])
    steps = _row_tile(N_META + x.shape[1]) // SUBLANES
    coef = _power_table(a_re, a_im, steps, reverse=False)
    coef_rev = _power_table(a_re, a_im, steps, reverse=True)
    bmat_re = _block_diag(jnp.swapaxes(bb_re, 1, 2), n_blk).astype(_MXU)
    bmat_im = _block_diag(jnp.swapaxes(bb_im, 1, 2), n_blk).astype(_MXU)
    cmat_re = _block_diag(jnp.swapaxes(ssm_c_re[0], 1, 2), n_blk).astype(_MXU)
    cmat_im = _block_diag(jnp.swapaxes(ssm_c_im[0], 1, 2), n_blk).astype(_MXU)
    wz1 = _block_diag(ssm_w_glu[0][:, :, :SSM_GROUP_CH], n_blk).astype(_MXU)
    wz2 = _block_diag(ssm_w_glu[0][:, :, SSM_GROUP_CH:], n_blk).astype(_MXU)
    pool_wm = pool_w[0].astype(_MXU)

    n2, u_s, u_p = _mix_in_fwd(h1, mix_pre_norm, w_in_f, "mix_in_fwd")
    (x_re, x_im, cat_s), (g_gate2, g_up2, g_down2) = _ssm_forward(
        u_s, coef, bmat_re, bmat_im, cmat_re, cmat_im, wz1, wz2, ssm_d, ssm_out_norm, "ssm_fwd",
        xchg=_Xchg([wire(ffn2_w_gate[0].T), wire(ffn2_w_up[0].T), wire(ffn2_w_down[0])], ["gather"] * 3))
    wgt2, wut2, wd2 = stacked(g_gate2), stacked(g_up2), stacked(g_down2)
    cat_p = _pool_fwd(u_p, pool_wm, pool_scale, pool_out_norm, "pool_fwd")
    wo_s, wo_p = w_out_f[:width], w_out_f[width:]
    mixed, h2 = _mix_out_fwd(cat_s, cat_p, h1, mix_post_norm, wo_s, wo_p, "mix_out_fwd")

    (a2, b2, f2, h3), _ = _ffn_fwd(h2, ffn2_pre_norm, ffn2_post_norm, wgt2, wut2, wd2, "ffn2_fwd")
    dh3, sq_err = _loss_grad(h3, target, "loss_grad")

    g, slots = {}, {}
    (dh2, da2, db2, s2, df2, nf2, g["ffn2_pre_norm"], g["ffn2_post_norm"]), _ = _ffn_bwd(
        dh3, h2, f2, a2, b2, ffn2_pre_norm, ffn2_post_norm, wgt2, wut2, wd2, "ffn2_bwd")
    dgate2, _ = _wgrad(da2, nf2, "ffn2_dgate")
    dup2, _ = _wgrad(db2, nf2, "ffn2_dup")
    ddown2, _ = _wgrad(s2, df2, "ffn2_ddown")

    dmixed, dcat_s, dcat_p, g["mix_post_norm"] = _mix_out_bwd(dh2, mixed, mix_post_norm, wo_s, wo_p, "mix_out_bwd")
    dwout = jnp.concatenate([_wgrad(cat_s, dmixed, "dwout_s")[0], _wgrad(cat_p, dmixed, "dwout_p")[0]], axis=0)
    ((du_s, g["ssm_out_norm"], g["ssm_d"], dwz1, dwz2, dcm_re, dcm_im, dbm_re, dbm_im, acc_re, acc_im),
     (slots["ffn2_w_gate"], slots["ffn2_w_up"], slots["ffn2_w_down"])) = _ssm_backward(
        u_s, x_re, x_im, dcat_s, coef_rev, bmat_re, bmat_im, cmat_re, cmat_im, wz1, wz2, ssm_d, ssm_out_norm, "ssm_bwd",
        xchg=_Xchg([chunked(dgate2), chunked(dup2), chunked(ddown2)], ["scatter"] * 3))
    du_p, g["pool_out_norm"], g["pool_scale"], dpw = _pool_bwd(u_p, dcat_p, pool_wm, pool_scale, pool_out_norm, "pool_bwd")
    wi_s, wi_p = w_in_f[:, :width], w_in_f[:, width:]
    dh1, dproj, g["mix_pre_norm"] = _mix_in_bwd(du_s, du_p, h1, dh2, mix_pre_norm, wi_s, wi_p, "mix_in_bwd")
    dwin, _ = _wgrad(n2, dproj, "dwin")

    g["ssm_c_re"] = jnp.swapaxes(_block_diag_t(jnp.swapaxes(dcm_re, 1, 2), n_grp), 1, 2)[None]
    g["ssm_c_im"] = jnp.swapaxes(_block_diag_t(jnp.swapaxes(dcm_im, 1, 2), n_grp), 1, 2)[None]
    g["ssm_w_glu"] = jnp.concatenate([_block_diag_t(dwz1, n_grp), _block_diag_t(dwz2, n_grp)], axis=-1)[None]
    dbb_re = jnp.swapaxes(_block_diag_t(dbm_re, n_grp), 1, 2)
    dbb_im = jnp.swapaxes(_block_diag_t(dbm_im, n_grp), 1, 2)
    da_re, da_im = acc_re.reshape(a_re.shape), acc_im.reshape(a_re.shape)
    _, disc_vjp = jax.vjp(_discretize, ssm_lambda_re[0], ssm_lambda_im[0], ssm_log_dt[0], ssm_b_re[0], ssm_b_im[0])
    d_lre, d_lim, d_ldt, d_bre, d_bim = disc_vjp((da_re, da_im, dbb_re, dbb_im))
    g["ssm_lambda_re"], g["ssm_lambda_im"], g["ssm_log_dt"] = d_lre[None], d_lim[None], d_ldt[None]
    g["ssm_b_re"], g["ssm_b_im"] = d_bre[None], d_bim[None]
    g["pool_w"] = dpw[None]

    late = ["ffn1_pre_norm", "ffn1_post_norm"]
    early = [n for n in small if n not in late]
    early_vec = _pack_rows([g[n] for n in early] + [sq_err[:, :1]], 1024, SUBLANES)
    ((dh0, da1, db1, s1, df1, nf1, g["ffn1_pre_norm"], g["ffn1_post_norm"]),
     (slots["w_out"], slots["w_in"], recv_early)) = _ffn_bwd(
        dh1, h0, f1, a1, b1, ffn1_pre_norm, ffn1_post_norm, wgt1, wut1, wd1, "ffn1_bwd",
        xchg=_Xchg([chunked(dwout), chunked(dwin), early_vec], ["scatter", "scatter", "gather"]))
    late_vec = _pack_rows([g[n] for n in late] + [dh0[:N_META]], 1024, SUBLANES)
    dgate1, (recv_late,) = _wgrad(da1, nf1, "ffn1_dgate", xchg=_Xchg([late_vec], ["gather"]))
    dup1, (slots["ffn1_w_gate"],) = _wgrad(db1, nf1, "ffn1_dup", xchg=_Xchg([chunked(dgate1)], ["scatter"]))
    ddown1, (slots["ffn1_w_up"],) = _wgrad(s1, df1, "ffn1_ddown", xchg=_Xchg([chunked(dup1)], ["scatter"]))
    (slots["ffn1_w_down"],) = _exchange([chunked(ddown1)], ["scatter"], "reduce_last")

    summed = _unpack(_sum_slots(recv_early, "sum_small_grads"), [g[n].shape for n in early] + [(1,)])
    for n, val in zip(early, summed):
        g[n] = val
    loss = (0.5 / d) * summed[-1][0]
    g[late[0]], g[late[1]], dmeta = _unpack(_sum_slots(recv_late, "sum_last_grads"), [(1, d), (1, d), (N_META, d)])
    g["meta_tokens"] = lax.dynamic_slice_in_dim(dmeta, _my_slot() * (d // N_DEV), d // N_DEV, axis=1)

    delta, new_m, new_v = {}, {}, {}
    for n in sharded:
        shape = args[n].shape
        two_d = (-1, shape[-1])
        w2, m2, v2 = args[n].reshape(two_d), args["m_" + n].reshape(two_d), args["v_" + n].reshape(two_d)
        if n == "meta_tokens":
            dl, mn, vn = _adamw(w2, g[n], m2, v2, "adamw_" + n)
        elif n.endswith("gate") or n.endswith("up"):
            g[n] = _sum_slots(slots[n], "sum_" + n).T[None]
            dl, mn, vn = _adamw(w2, g[n][0], m2, v2, "adamw_" + n)
        else:
            gs, dl, mn, vn = _adamw_slots(w2, slots[n], m2, v2, "adamw_" + n)
            g[n] = gs[None]
        delta[n], new_m[n], new_v[n] = dl.reshape(shape), mn.reshape(shape), vn.reshape(shape)
    packs = [_pack_rows([src[n] if pre is None else args[pre + n] for n in small], 1024, SUBLANES)
             for src, pre in ((args, None), (g, None), (None, "m_"), (None, "v_"))]
    outs = _adamw(*packs, "adamw_small")
    shapes = [args[n].shape for n in small]
    for store, flat in zip((delta, new_m, new_v), outs):
        for n, val in zip(small, _unpack(flat, shapes)):
            store[n] = val

    grad_x = dh0[N_META:][None]
    return (loss, grad_x, *[g[n] for n in names], *[delta[n] for n in names], *[new_m[n] for n in names],
            *[new_v[n] for n in names])
```

```python
import functools
import math

import jax
import jax.numpy as jnp
import numpy as np
from jax import lax
from jax.experimental import pallas as pl
from jax.experimental.pallas import tpu as pltpu

F32 = jnp.float32
_MXU = jnp.bfloat16
_ACT = jnp.bfloat16
_WIRE = jnp.bfloat16

N_DEV = 8
N_META = 16
RMS_EPS = 1e-6
SSM_GROUP_CH = 16
SSM_STATE = 64
LANES = 128
SUBLANES = 8
POOL_WINDOWS = (2, 4, 8, 16)
POOL_HALO = 16
ADAM_LR = 0.001
ADAM_B1 = 0.9
ADAM_B2 = 0.999
ADAM_EPS = 1e-08
ADAM_WD = 0.01
ADAM_STEP = 10
GELU_C0 = math.sqrt(2.0 / math.pi)
GELU_C1 = 0.044715
VMEM_LIMIT = 56 * 1024 * 1024

_NT = (((1,), (1,)), ((), ()))
_TN = (((0,), (0,)), ((), ()))


def _dot(a, b):
    return jnp.dot(a, b, preferred_element_type=F32)


def _dot_nt(a, b):
    return lax.dot_general(a, b, _NT, preferred_element_type=F32)


def _dot_tn(a, b):
    return lax.dot_general(a, b, _TN, preferred_element_type=F32)


def _rs(x):
    return lax.rsqrt(jnp.mean(x * x, axis=-1, keepdims=True) + RMS_EPS)


def _sigmoid(x):
    return 0.5 * jnp.tanh(0.5 * x) + 0.5


def _row_tile(n_rows, largest=432):
    for t in (432, 304, 48, 16):
        if t <= largest and n_rows % t == 0:
            return t
    raise ValueError(n_rows)


def _ff_chunk(d_ff):
    return d_ff // 2 if (d_ff // 2) % LANES == 0 else d_ff


def _params(n_axes=1):
    return pltpu.CompilerParams(dimension_semantics=("arbitrary",) * n_axes, vmem_limit_bytes=VMEM_LIMIT)


def _resident():
    return pl.BlockSpec(memory_space=pltpu.VMEM)


def _full(shape):
    nd = len(shape)
    return pl.BlockSpec(shape, lambda *_: (0,) * nd)


def _lane_blocks(n_blk, tm, tile_of=lambda i: i):
    return pl.BlockSpec((n_blk, tm, LANES), lambda i: (0, tile_of(i), 0))


PEER_ORDER = (1, 2, 4, 3, 5, 6, 7)


def _split(refs, counts):
    out, pos = [], 0
    for n in counts:
        out.append(refs[pos:pos + n])
        pos += n
    return out


def _peer(r):
    x, y, c = lax.axis_index("x"), lax.axis_index("y"), lax.axis_index("c")
    return (1 - x if r & 4 else x, 1 - y if r & 2 else y, 1 - c if r & 1 else c)


def _my_slot():
    return 4 * lax.axis_index("x") + 2 * lax.axis_index("y") + lax.axis_index("c")


class _Xchg:
    def __init__(self, srcs, kinds):
        self.srcs, self.kinds, self.n = list(srcs), list(kinds), len(srcs)
        self.out_shape = [jax.ShapeDtypeStruct((N_DEV,) + s.shape if k == "gather" else s.shape, s.dtype)
                          for s, k in zip(self.srcs, self.kinds)]
        self.specs = [pl.BlockSpec(memory_space=pl.ANY)] * self.n
        self.scratch = [pltpu.SemaphoreType.DMA((self.n * (N_DEV - 1),)), pltpu.SemaphoreType.DMA((self.n * (N_DEV - 1),)),
                        pltpu.SemaphoreType.DMA((self.n,))]

    def copies(self, src, dst, sems):
        send_sems, recv_sems, local_sems = sems
        me = _my_slot()
        out = []
        for a in range(self.n):
            mine = src[a] if self.kinds[a] == "gather" else src[a].at[me]
            out.append(pltpu.make_async_copy(mine, dst[a].at[me], local_sems.at[a]))
            for r in PEER_ORDER:
                px, py, pc = _peer(r)
                part = src[a] if self.kinds[a] == "gather" else src[a].at[4 * px + 2 * py + pc]
                k = a * (N_DEV - 1) + r - 1
                out.append(pltpu.make_async_remote_copy(
                    src_ref=part, dst_ref=dst[a].at[me], send_sem=send_sems.at[k], recv_sem=recv_sems.at[k],
                    device_id=(px, py, pc), device_id_type=pl.DeviceIdType.MESH))
        return out

    def start(self, src, dst, sems):
        for cp in self.copies(src, dst, sems):
            cp.start()

    def wait(self, src, dst, sems):
        for cp in self.copies(src, dst, sems):
            cp.wait()


class _NoXchg:
    n, srcs, out_shape, specs, scratch = 0, [], [], [], []

    def start(self, *_):
        pass

    wait = start


def _call(body, args, *, name, grid, out_shape, in_specs, out_specs, scratch_shapes=(), xchg=None):
    xc = xchg or _NoXchg()
    counts = (len(in_specs), xc.n, len(out_shape), xc.n, len(scratch_shapes), len(xc.scratch))

    def wrapped(*refs):
        ins, xsrc, outs, xdst, scr, sems = _split(refs, counts)
        ids = [pl.program_id(k) for k in range(len(grid))]
        if xc.n:
            @pl.when(functools.reduce(jnp.logical_and, [i == 0 for i in ids]))
            def _():
                xc.start(xsrc, xdst, sems)

        body(*ins, *outs, *scr)
        if xc.n:
            @pl.when(functools.reduce(jnp.logical_and, [i == g - 1 for i, g in zip(ids, grid)]))
            def _():
                xc.wait(xsrc, xdst, sems)

    res = pl.pallas_call(
        wrapped, name=name, grid=grid, out_shape=list(out_shape) + xc.out_shape,
        in_specs=list(in_specs) + xc.specs, out_specs=list(out_specs) + xc.specs,
        scratch_shapes=list(scratch_shapes) + xc.scratch, compiler_params=_params(len(grid)),
    )(*args, *xc.srcs)
    return res[:len(out_shape)], res[len(out_shape):]


def _exchange(srcs, kinds, name):
    xc = _Xchg(srcs, kinds)

    def body(*refs):
        src, dst, sems = _split(refs, (xc.n, xc.n, 3))
        xc.start(src, dst, sems)
        xc.wait(src, dst, sems)

    return pl.pallas_call(body, name=name, out_shape=xc.out_shape, in_specs=xc.specs, out_specs=xc.specs,
                          scratch_shapes=xc.scratch)(*srcs)


def _gather_two_level(srcs, name):
    n = len(srcs)
    out_shape = [jax.ShapeDtypeStruct((N_DEV,) + s.shape, s.dtype) for s in srcs]
    chips = (2, 4, 6)

    def body(*refs):
        src, dst, (send_sems, recv_sems, local_sems) = _split(refs, (n, n, 3))
        x, y, c = lax.axis_index("x"), lax.axis_index("y"), lax.axis_index("c")
        me = 4 * x + 2 * y + c
        sibling = (x, y, 1 - c)

        def copy(a, k, slot, to, from_src=False):
            return pltpu.make_async_remote_copy(
                src_ref=src[a] if from_src else dst[a].at[slot], dst_ref=dst[a].at[slot],
                send_sem=send_sems.at[a * 7 + k], recv_sem=recv_sems.at[a * 7 + k],
                device_id=to, device_id_type=pl.DeviceIdType.MESH)

        def slot_of(r, core):
            px, py, _ = _peer(r)
            return 4 * px + 2 * py + core

        local = [pltpu.make_async_copy(src[a], dst[a].at[me], local_sems.at[a]) for a in range(n)]
        sent = []
        for a in range(n):
            local[a].start()
            sent.append(copy(a, 0, me, sibling, from_src=True))
            sent += [copy(a, 1 + j, me, _peer(r), from_src=True) for j, r in enumerate(chips)]
        for cp in sent:
            cp.start()
        for j, r in enumerate(chips):
            for a in range(n):
                copy(a, 1 + j, slot_of(r, c), _peer(r)).wait_recv()
                cp = copy(a, 4 + j, slot_of(r, c), sibling)
                cp.start()
                sent.append(cp)
        for a in range(n):
            copy(a, 0, slot_of(0, 1 - c), sibling).wait_recv()
            for j, r in enumerate(chips):
                copy(a, 4 + j, slot_of(r, 1 - c), sibling).wait_recv()
        for cp in local:
            cp.wait()
        for cp in sent:
            cp.wait_send()

    any_spec = pl.BlockSpec(memory_space=pl.ANY)
    return pl.pallas_call(
        body, name=name, out_shape=out_shape, in_specs=[any_spec] * n, out_specs=[any_spec] * n,
        scratch_shapes=[pltpu.SemaphoreType.DMA((n * 7,)), pltpu.SemaphoreType.DMA((n * 7,)), pltpu.SemaphoreType.DMA((n,))],
    )(*srcs)


def _to_wire(mats, name):
    def body(*refs):
        for src, dst in zip(refs[:len(mats)], refs[len(mats):]):
            dst[...] = src[...].astype(_WIRE)

    return pl.pallas_call(
        body, name=name, grid=(1,), out_shape=[jax.ShapeDtypeStruct(m.shape, _WIRE) for m in mats],
        in_specs=[_full(m.shape) for m in mats], out_specs=[_full(m.shape) for m in mats], compiler_params=_params(),
    )(*mats)


def _sum_slots(r, name):
    _, rows, cols = r.shape
    blk = rows
    for cand in (rows, 592, 512, 256, 128, 64, 32, 16):
        if rows % cand == 0 and N_DEV * cand * cols * r.dtype.itemsize <= 8 * 1024 * 1024:
            blk = cand
            break

    def body(r_ref, o_ref):
        acc = r_ref[0].astype(F32)
        for d in range(1, N_DEV):
            acc = acc + r_ref[d].astype(F32)
        o_ref[...] = acc

    return pl.pallas_call(
        body, name=name, grid=(rows // blk,), out_shape=jax.ShapeDtypeStruct((rows, cols), F32),
        in_specs=[pl.BlockSpec((N_DEV, blk, cols), lambda i: (0, i, 0))],
        out_specs=pl.BlockSpec((blk, cols), lambda i: (i, 0)), compiler_params=_params(),
    )(r)


def _ffn_fwd(h, g_pre, g_post, wgt, wut, wd, name, xchg=None):
    n_rows, d = h.shape
    d_ff = wd.shape[0]
    tm, fc = _row_tile(n_rows), _ff_chunk(d_ff)
    n_t, n_c = n_rows // tm, d_ff // fc

    def body(h_ref, gpre_ref, gpost_ref, wgt_ref, wut_ref, wd_ref, a_ref, b_ref, f_ref, ho_ref, n_scr, acc):
        c = pl.program_id(1)

        @pl.when(c == 0)
        def _():
            hv = h_ref[...]
            n_scr[...] = (hv * _rs(hv) * gpre_ref[...]).astype(_MXU)
            acc[...] = jnp.zeros_like(acc)

        rows = pl.ds(pl.multiple_of(c * fc, fc), fc)
        nv = n_scr[...]
        a = _dot_nt(nv, wgt_ref[rows, :])
        b = _dot_nt(nv, wut_ref[rows, :])
        a_ref[...] = a.astype(_ACT)
        b_ref[...] = b.astype(_ACT)
        s = a * _sigmoid(a) * b
        acc[...] += _dot(s.astype(_MXU), wd_ref[rows, :])

        @pl.when(c == n_c - 1)
        def _():
            f = acc[...]
            f_ref[...] = f
            ho_ref[...] = h_ref[...] + 0.5 * (f * _rs(f) * gpost_ref[...])

    row = pl.BlockSpec((tm, d), lambda i, c: (i, 0))
    chunk = pl.BlockSpec((tm, fc), lambda i, c: (i, c))
    return _call(
        body, (h, g_pre, g_post, wgt, wut, wd), name=name, grid=(n_t, n_c),
        out_shape=[jax.ShapeDtypeStruct((n_rows, d_ff), _ACT), jax.ShapeDtypeStruct((n_rows, d_ff), _ACT),
                   jax.ShapeDtypeStruct((n_rows, d), F32), jax.ShapeDtypeStruct((n_rows, d), F32)],
        in_specs=[row, _full((1, d)), _full((1, d)), _resident(), _resident(), _resident()],
        out_specs=[chunk, chunk, row, row],
        scratch_shapes=[pltpu.VMEM((tm, d), _MXU), pltpu.VMEM((tm, d), F32)], xchg=xchg)


def _ffn_bwd(dho, h, f, a, b, g_pre, g_post, wgt, wut, wd, name, xchg=None):
    n_rows, d = h.shape
    d_ff = wd.shape[0]
    tm, fc = _row_tile(n_rows, 304), _ff_chunk(d_ff)
    n_t, n_c = n_rows // tm, d_ff // fc

    def body(dho_ref, h_ref, f_ref, a_ref, b_ref, gpre_ref, gpost_ref, wgt_ref, wut_ref, wd_ref,
             dh_ref, da_ref, db_ref, s_ref, df_ref, n_ref, dgpre_ref, dgpost_ref, dn_acc):
        i, c = pl.program_id(0), pl.program_id(1)

        @pl.when((i == 0) & (c == 0))
        def _():
            dgpre_ref[...] = jnp.zeros_like(dgpre_ref)
            dgpost_ref[...] = jnp.zeros_like(dgpost_ref)

        @pl.when(c == 0)
        def _():
            fv = f_ref[...]
            rf = _rs(fv)
            fhat = fv * rf
            dy = 0.5 * dho_ref[...]
            dgpost_ref[...] += jnp.sum(dy * fhat, axis=0, keepdims=True)
            dfhat = dy * gpost_ref[...]
            df = rf * (dfhat - fhat * jnp.mean(dfhat * fhat, axis=-1, keepdims=True))
            df_ref[...] = df.astype(_MXU)
            hv = h_ref[...]
            n_ref[...] = (hv * _rs(hv) * gpre_ref[...]).astype(_MXU)
            dn_acc[...] = jnp.zeros_like(dn_acc)

        rows = pl.ds(pl.multiple_of(c * fc, fc), fc)
        ds = _dot_nt(df_ref[...], wd_ref[rows, :])
        av = a_ref[...].astype(F32)
        bv = b_ref[...].astype(F32)
        sg = _sigmoid(av)
        si = av * sg
        da = (ds * bv * (sg * (1.0 + av * (1.0 - sg)))).astype(_MXU)
        db = (ds * si).astype(_MXU)
        da_ref[...] = da
        db_ref[...] = db
        s_ref[...] = (si * bv).astype(_MXU)
        dn_acc[...] += _dot(da, wgt_ref[rows, :]) + _dot(db, wut_ref[rows, :])

        @pl.when(c == n_c - 1)
        def _():
            dn = dn_acc[...]
            hv = h_ref[...]
            r = _rs(hv)
            hhat = hv * r
            dgpre_ref[...] += jnp.sum(dn * hhat, axis=0, keepdims=True)
            dhh = dn * gpre_ref[...]
            dh_ref[...] = dho_ref[...] + r * (dhh - hhat * jnp.mean(dhh * hhat, axis=-1, keepdims=True))

    row = pl.BlockSpec((tm, d), lambda i, c: (i, 0))
    chunk = pl.BlockSpec((tm, fc), lambda i, c: (i, c))
    vec = pl.BlockSpec((1, d), lambda i, c: (0, 0))
    return _call(
        body, (dho, h, f, a, b, g_pre, g_post, wgt, wut, wd), name=name, grid=(n_t, n_c),
        out_shape=[jax.ShapeDtypeStruct((n_rows, d), F32)] + [jax.ShapeDtypeStruct((n_rows, d_ff), _MXU)] * 3
        + [jax.ShapeDtypeStruct((n_rows, d), _MXU)] * 2 + [jax.ShapeDtypeStruct((1, d), F32)] * 2,
        in_specs=[row, row, row, chunk, chunk, vec, vec, _resident(), _resident(), _resident()],
        out_specs=[row, chunk, chunk, chunk, row, row, vec, vec],
        scratch_shapes=[pltpu.VMEM((tm, d), F32)], xchg=xchg)


def _wgrad(xm, ym, name, xchg=None):
    n_rows, a_dim = xm.shape
    b_dim = ym.shape[1]
    tk = n_rows
    for cand in (2736, 1296, 432, 48, 16):
        if n_rows % cand == 0:
            tk = cand
            break
    ta = a_dim
    for cand in (1408, 1024, 512):
        if a_dim % cand == 0:
            ta = cand
            break

    n_k = n_rows // tk

    def body(x_ref, y_ref, o_ref, acc):
        k = pl.program_id(1)

        @pl.when(k == 0)
        def _():
            acc[...] = jnp.zeros_like(acc)

        acc[...] += _dot_tn(x_ref[...], y_ref[...])

        @pl.when(k == n_k - 1)
        def _():
            o_ref[...] = acc[...].astype(o_ref.dtype)

    (out,), extra = _call(
        body, (xm, ym), name=name, grid=(a_dim // ta, n_k), out_shape=[jax.ShapeDtypeStruct((a_dim, b_dim), _WIRE)],
        in_specs=[pl.BlockSpec((tk, ta), lambda j, k: (k, j)), pl.BlockSpec((tk, b_dim), lambda j, k: (k, 0))],
        out_specs=[pl.BlockSpec((ta, b_dim), lambda j, k: (j, 0))], scratch_shapes=[pltpu.VMEM((ta, b_dim), F32)],
        xchg=xchg)
    return out, extra


def _mix_in_fwd(h, g, w_in, name):
    n_rows, d = h.shape
    tm = _row_tile(n_rows)
    width = w_in.shape[1] // 2
    n_blk = width // LANES

    def body(h_ref, g_ref, w_ref, n_ref, us_ref, up_ref):
        hv = h_ref[...]
        nv = (hv * _rs(hv) * g_ref[...]).astype(_MXU)
        n_ref[...] = nv
        p = _dot(nv, w_ref[...])
        for k in range(n_blk):
            us_ref[k] = p[:, k * LANES:(k + 1) * LANES]
        up_ref[...] = p[:, width:]

    row = pl.BlockSpec((tm, d), lambda i: (i, 0))
    half = pl.BlockSpec((tm, width), lambda i: (i, 0))
    return pl.pallas_call(
        body, name=name, grid=(n_rows // tm,),
        out_shape=[jax.ShapeDtypeStruct((n_rows, d), _MXU), jax.ShapeDtypeStruct((n_blk, n_rows, LANES), F32),
                   jax.ShapeDtypeStruct((n_rows, width), F32)],
        in_specs=[row, _full((1, d)), _resident()], out_specs=[row, _lane_blocks(n_blk, tm), half],
        compiler_params=_params(),
    )(h, g, w_in)


def _gelu_parts(y):
    th = jnp.tanh(GELU_C0 * (y + GELU_C1 * (y * y * y)))
    return 0.5 * (1.0 + th), th


def _scan_blocks(n_blocks, width, coef_ref, load, store, reverse):
    n_cols = coef_ref.shape[2] // width
    for cb in range(n_cols):
        cols = pl.ds(cb * width, width)

        def step(t, carry, cols=cols):
            tb = (n_blocks - 1 - t) if reverse else t
            r0 = pl.multiple_of(tb * SUBLANES, SUBLANES)
            vr, vi = load(r0, cols)
            xr, xi = vr, vi
            for lvl, k in enumerate((1, 2, 4)):
                kr, ki = coef_ref[2 * lvl, :, cols], coef_ref[2 * lvl + 1, :, cols]
                shift = SUBLANES - k if reverse else k
                sr, si = pltpu.roll(xr, shift, 0), pltpu.roll(xi, shift, 0)
                xr, xi = xr + (kr * sr - ki * si), xi + (kr * si + ki * sr)
            pr, pi = coef_ref[6, :, cols], coef_ref[7, :, cols]
            cr, ci = carry[0], carry[1]
            xr, xi = xr + (pr * cr - pi * ci), xi + (pr * ci + pi * cr)
            extra = store(r0, cols, xr, xi, vr, vi, carry[2:])
            edge = 0 if reverse else SUBLANES - 1
            return (xr[edge:edge + 1, :], xi[edge:edge + 1, :]) + tuple(extra)

        yield cb, cols, step


def _ssm_fwd(proj, coef, b_re, b_im, c_re, c_im, wz1, wz2, d_skip, g_out, name, xchg=None):
    n_rows = proj.shape[0]
    n_blk, _, n_state = b_re.shape
    width = n_blk * LANES
    tm = _row_tile(n_rows)
    n_tb = tm // SUBLANES

    def body(u_ref, coef_ref, bre_ref, bim_ref, cre_ref, cim_ref, wz1_ref, wz2_ref, d_ref, g_ref,
             xre_ref, xim_ref, o_ref, car_re, car_im, out_scr):
        @pl.when(pl.program_id(0) == 0)
        def _():
            car_re[...] = jnp.zeros_like(car_re)
            car_im[...] = jnp.zeros_like(car_im)

        ub = u_ref[...].astype(_MXU)
        for j in range(n_blk):
            uj = ub[:, j * LANES:(j + 1) * LANES]
            xre_ref[:, j * n_state:(j + 1) * n_state] = _dot(uj, bre_ref[j])
            xim_ref[:, j * n_state:(j + 1) * n_state] = _dot(uj, bim_ref[j])

        def load(r0, cols):
            return xre_ref[pl.ds(r0, SUBLANES), cols], xim_ref[pl.ds(r0, SUBLANES), cols]

        def store(r0, cols, xr, xi, vr, vi, extra):
            xre_ref[pl.ds(r0, SUBLANES), cols] = xr
            xim_ref[pl.ds(r0, SUBLANES), cols] = xi
            return ()

        for cb, cols, step in _scan_blocks(n_tb, 512, coef_ref, load, store, reverse=False):
            fin = lax.fori_loop(0, n_tb, step, (car_re[:, cols], car_im[:, cols]))
            car_re[:, cols] = fin[0]
            car_im[:, cols] = fin[1]

        ssq = None
        for j in range(n_blk):
            sl = slice(j * LANES, (j + 1) * LANES)
            st = slice(j * n_state, (j + 1) * n_state)
            yc = _dot(xre_ref[:, st].astype(_MXU), cre_ref[j]) - _dot(xim_ref[:, st].astype(_MXU), cim_ref[j])
            y = yc + d_ref[:, sl] * u_ref[:, sl]
            cdf, _ = _gelu_parts(y)
            gy = (y * cdf).astype(_MXU)
            out = _dot(gy, wz1_ref[j]) * _sigmoid(_dot(gy, wz2_ref[j]))
            out_scr[:, sl] = out
            part = jnp.sum(out * out, axis=-1, keepdims=True)
            ssq = part if ssq is None else ssq + part
        r = lax.rsqrt(ssq / width + RMS_EPS)
        o_ref[...] = (out_scr[...] * r * g_ref[...]).astype(_MXU)

    half = pl.BlockSpec((tm, width), lambda i: (i, 0))
    state = pl.BlockSpec((tm, n_blk * n_state), lambda i: (i, 0))
    return _call(
        body, (proj, coef, b_re, b_im, c_re, c_im, wz1, wz2, d_skip, g_out), name=name, grid=(n_rows // tm,),
        out_shape=[jax.ShapeDtypeStruct((n_rows, n_blk * n_state), F32)] * 2 + [jax.ShapeDtypeStruct((n_rows, width), _MXU)],
        in_specs=[half, _full(coef.shape), _full(b_re.shape), _full(b_im.shape), _full(c_re.shape), _full(c_im.shape),
                  _full(wz1.shape), _full(wz2.shape), _full((1, width)), _full((1, width))],
        out_specs=[state, state, half],
        scratch_shapes=[pltpu.VMEM((1, n_blk * n_state), F32)] * 2 + [pltpu.VMEM((tm, width), F32)], xchg=xchg)


def _ssm_bwd(proj, x_re, x_im, dcat, coef_rev, b_re, b_im, c_re, c_im, wz1, wz2, d_skip, g_out, name, xchg=None):
    n_rows = proj.shape[0]
    n_blk, _, n_state = b_re.shape
    width = n_blk * LANES
    n_all = n_blk * n_state
    tm = _row_tile(n_rows)
    n_t, n_tb = n_rows // tm, tm // SUBLANES

    def body(u_ref, xre_ref, xim_ref, dc_ref, coef_ref, bre_ref, bim_ref, cre_ref, cim_ref, wz1_ref, wz2_ref, d_ref, g_ref,
             du_ref, dg_ref, dd_ref, dwz1_ref, dwz2_ref, dcre_ref, dcim_ref, dbre_ref, dbim_ref, are_ref, aim_ref,
             car_re, car_im, gre, gim, y_s, z1_s, sg_s, out_s, gy_s):
        @pl.when(pl.program_id(0) == 0)
        def _():
            for ref in (dg_ref, dd_ref, dwz1_ref, dwz2_ref, dcre_ref, dcim_ref, dbre_ref, dbim_ref, are_ref, aim_ref,
                        car_re, car_im):
                ref[...] = jnp.zeros_like(ref)

        ssq = None
        for j in range(n_blk):
            sl = slice(j * LANES, (j + 1) * LANES)
            st = slice(j * n_state, (j + 1) * n_state)
            yc = _dot(xre_ref[:, st].astype(_MXU), cre_ref[j]) - _dot(xim_ref[:, st].astype(_MXU), cim_ref[j])
            y = yc + d_ref[:, sl] * u_ref[:, sl]
            cdf, _ = _gelu_parts(y)
            gy = (y * cdf).astype(_MXU)
            z1 = _dot(gy, wz1_ref[j])
            sg = _sigmoid(_dot(gy, wz2_ref[j]))
            out = z1 * sg
            y_s[:, sl], z1_s[:, sl], sg_s[:, sl], out_s[:, sl], gy_s[:, sl] = y, z1, sg, out, gy
            part = jnp.sum(out * out, axis=-1, keepdims=True)
            ssq = part if ssq is None else ssq + part
        r = lax.rsqrt(ssq / width + RMS_EPS)
        ohat = out_s[...] * r
        dcv = dc_ref[...]
        dg_ref[...] += jnp.sum(dcv * ohat, axis=0, keepdims=True)
        doh = dcv * g_ref[...]
        out_s[...] = r * (doh - ohat * (jnp.sum(doh * ohat, axis=-1, keepdims=True) / width))

        for j in range(n_blk):
            sl = slice(j * LANES, (j + 1) * LANES)
            st = slice(j * n_state, (j + 1) * n_state)
            dout, sg, z1, y = out_s[:, sl], sg_s[:, sl], z1_s[:, sl], y_s[:, sl]
            dz1 = (dout * sg).astype(_MXU)
            dz2 = (dout * z1 * sg * (1.0 - sg)).astype(_MXU)
            gy = gy_s[:, sl]
            dwz1_ref[j] += _dot_tn(gy, dz1)
            dwz2_ref[j] += _dot_tn(gy, dz2)
            dgy = _dot_nt(dz1, wz1_ref[j]) + _dot_nt(dz2, wz2_ref[j])
            cdf, th = _gelu_parts(y)
            dy = dgy * (cdf + y * (0.5 * (1.0 - th * th) * GELU_C0 * (1.0 + 3.0 * GELU_C1 * (y * y))))
            uj = u_ref[:, sl]
            dd_ref[:, sl] += jnp.sum(dy * uj, axis=0, keepdims=True)
            du_ref[:, sl] = d_ref[:, sl] * dy
            dyb = dy.astype(_MXU)
            dcre_ref[j] += _dot_tn(dyb, xre_ref[:, st].astype(_MXU))
            dcim_ref[j] -= _dot_tn(dyb, xim_ref[:, st].astype(_MXU))
            gre[:, st] = _dot_nt(dyb, cre_ref[j])
            gim[:, st] = -_dot_nt(dyb, cim_ref[j])

        def load(r0, cols):
            return gre[pl.ds(r0, SUBLANES), cols], gim[pl.ds(r0, SUBLANES), cols]

        def store(r0, cols, gr, gi, vr, vi, extra):
            gre[pl.ds(r0, SUBLANES), cols] = gr
            gim[pl.ds(r0, SUBLANES), cols] = gi
            hr, hi = gr - vr, gi - vi
            xr, xi = xre_ref[pl.ds(r0, SUBLANES), cols], xim_ref[pl.ds(r0, SUBLANES), cols]
            return extra[0] + (xr * hr + xi * hi), extra[1] + (xr * hi - xi * hr)

        for cb, cols, step in _scan_blocks(n_tb, 512, coef_ref, load, store, reverse=True):
            zero = jnp.zeros((SUBLANES, 512), F32)
            fin = lax.fori_loop(0, n_tb, step, (car_re[:, cols], car_im[:, cols], zero, zero))
            car_re[:, cols] = fin[0]
            car_im[:, cols] = fin[1]
            are_ref[:, cols] += jnp.sum(fin[2], axis=0, keepdims=True)
            aim_ref[:, cols] += jnp.sum(fin[3], axis=0, keepdims=True)

        for j in range(n_blk):
            sl = slice(j * LANES, (j + 1) * LANES)
            st = slice(j * n_state, (j + 1) * n_state)
            ujb = u_ref[:, sl].astype(_MXU)
            grb, gib = gre[:, st].astype(_MXU), gim[:, st].astype(_MXU)
            dbre_ref[j] += _dot_tn(ujb, grb)
            dbim_ref[j] += _dot_tn(ujb, gib)
            du_ref[:, sl] += _dot_nt(grb, bre_ref[j]) + _dot_nt(gib, bim_ref[j])

    half = pl.BlockSpec((tm, width), lambda i: (n_t - 1 - i, 0))
    state = pl.BlockSpec((tm, n_all), lambda i: (n_t - 1 - i, 0))
    small = [(1, width), (1, width), wz1.shape, wz2.shape, (n_blk, LANES, n_state), (n_blk, LANES, n_state),
             (n_blk, LANES, n_state), (n_blk, LANES, n_state), (1, n_all), (1, n_all)]
    return _call(
        body, (proj, x_re, x_im, dcat, coef_rev, b_re, b_im, c_re, c_im, wz1, wz2, d_skip, g_out), name=name, grid=(n_t,),
        out_shape=[jax.ShapeDtypeStruct((n_rows, width), F32)] + [jax.ShapeDtypeStruct(s, F32) for s in small],
        in_specs=[half, state, state, half, _full(coef_rev.shape), _full(b_re.shape), _full(b_im.shape), _full(c_re.shape),
                  _full(c_im.shape), _full(wz1.shape), _full(wz2.shape), _full((1, width)), _full((1, width))],
        out_specs=[half] + [_full(s) for s in small],
        scratch_shapes=[pltpu.VMEM((1, n_all), F32)] * 2 + [pltpu.VMEM((tm, n_all), F32)] * 2
        + [pltpu.VMEM((tm, width), F32)] * 4 + [pltpu.VMEM((tm, width), _MXU)], xchg=xchg)


SCAN_LANES = 512


def _regroup(src_ref, dst_ref, seg):
    for k in range(src_ref.shape[0]):
        for j in range(seg):
            dst_ref[j * SUBLANES:(j + 1) * SUBLANES, k * LANES:(k + 1) * LANES] = src_ref[k, pl.ds(j, SUBLANES, stride=seg), :]


def _ungroup(src_ref, dst_ref, seg):
    for k in range(dst_ref.shape[0]):
        for j in range(seg):
            dst_ref[k, pl.ds(j, SUBLANES, stride=seg), :] = src_ref[j * SUBLANES:(j + 1) * SUBLANES, k * LANES:(k + 1) * LANES]


def _rows_to_sublanes(rows):
    rid = lax.broadcasted_iota(jnp.int32, (SUBLANES, rows[0].shape[1]), 0)
    out = jnp.broadcast_to(rows[0], rid.shape)
    for s in range(1, SUBLANES):
        out = jnp.where(rid == s, rows[s], out)
    return out


def _segment_scan(re_ref, im_ref, tab_ref, car_re, car_im, seg, reverse, x_refs=None):
    n_all = re_ref.shape[1]
    first = seg - 1 if reverse else 0
    sums = []
    for cb in range(n_all // SCAN_LANES):
        cols = pl.ds(cb * SCAN_LANES, SCAN_LANES)
        ar, ai = tab_ref[0, first, :, cols], tab_ref[1, first, :, cols]

        def local(t, carry, cols=cols, ar=ar, ai=ai):
            r0 = pl.multiple_of((seg - 1 - t if reverse else t) * SUBLANES, SUBLANES)
            xr, xi = carry
            nr = ar * xr - ai * xi + re_ref[pl.ds(r0, SUBLANES), cols]
            ni = ar * xi + ai * xr + im_ref[pl.ds(r0, SUBLANES), cols]
            re_ref[pl.ds(r0, SUBLANES), cols] = nr
            im_ref[pl.ds(r0, SUBLANES), cols] = ni
            return nr, ni

        zero = jnp.zeros((SUBLANES, SCAN_LANES), F32)
        fr, fi = lax.fori_loop(0, seg, local, (zero, zero))

        last = 0 if reverse else seg - 1
        sr, si = tab_ref[0, last, 0:1, cols], tab_ref[1, last, 0:1, cols]
        cr, ci = car_re[:, cols], car_im[:, cols]
        rows_r, rows_i = [None] * SUBLANES, [None] * SUBLANES
        for s in (range(SUBLANES - 1, -1, -1) if reverse else range(SUBLANES)):
            rows_r[s], rows_i[s] = cr, ci
            cr, ci = sr * cr - si * ci + fr[s:s + 1], sr * ci + si * cr + fi[s:s + 1]
        car_re[:, cols] = cr
        car_im[:, cols] = ci
        cmr, cmi = _rows_to_sublanes(rows_r), _rows_to_sublanes(rows_i)

        def fix(t, carry, cols=cols, cmr=cmr, cmi=cmi):
            j = seg - 1 - t if reverse else t
            r0 = pl.multiple_of(j * SUBLANES, SUBLANES)
            pr, pi = tab_ref[0, j, :, cols], tab_ref[1, j, :, cols]
            gr = re_ref[pl.ds(r0, SUBLANES), cols] + (pr * cmr - pi * cmi)
            gi = im_ref[pl.ds(r0, SUBLANES), cols] + (pr * cmi + pi * cmr)
            re_ref[pl.ds(r0, SUBLANES), cols] = gr
            im_ref[pl.ds(r0, SUBLANES), cols] = gi
            if x_refs is None:
                return carry
            nxr, nxi, accr, acci = carry
            xr, xi = x_refs[0][pl.ds(r0, SUBLANES), cols], x_refs[1][pl.ds(r0, SUBLANES), cols]
            return gr, gi, accr + (xr * nxr + xi * nxi), acci + (xr * nxi - xi * nxr)

        if x_refs is None:
            lax.fori_loop(0, seg, fix, 0)
        else:
            fin = lax.fori_loop(0, seg, fix, (cmr, cmi, zero, zero))
            sums.append((jnp.sum(fin[2], axis=0, keepdims=True), jnp.sum(fin[3], axis=0, keepdims=True)))
    return sums


def _ssm_forward(u, table, b_re, b_im, c_re, c_im, wz1, wz2, d_skip, g_out, name, xchg=None):
    n_rows = u.shape[1]
    n_blk, _, n_state = b_re.shape
    width, n_all = n_blk * LANES, n_blk * n_state
    tm = _row_tile(n_rows)
    seg = tm // SUBLANES

    def body(u_ref, tab_ref, bre_ref, bim_ref, cre_ref, cim_ref, wz1_ref, wz2_ref, d_ref, g_ref,
             xre_ref, xim_ref, o_ref, car_re, car_im, ug, out_scr, blocks):
        @pl.when(pl.program_id(0) == 0)
        def _():
            car_re[...] = jnp.zeros_like(car_re)
            car_im[...] = jnp.zeros_like(car_im)

        _regroup(u_ref, ug, seg)
        ub = ug[...].astype(_MXU)
        for j in range(n_blk):
            uj = ub[:, j * LANES:(j + 1) * LANES]
            xre_ref[:, j * n_state:(j + 1) * n_state] = _dot(uj, bre_ref[j])
            xim_ref[:, j * n_state:(j + 1) * n_state] = _dot(uj, bim_ref[j])
        _segment_scan(xre_ref, xim_ref, tab_ref, car_re, car_im, seg, reverse=False)

        ssq = None
        for j in range(n_blk):
            sl = slice(j * LANES, (j + 1) * LANES)
            st = slice(j * n_state, (j + 1) * n_state)
            yc = _dot(xre_ref[:, st].astype(_MXU), cre_ref[j]) - _dot(xim_ref[:, st].astype(_MXU), cim_ref[j])
            y = yc + d_ref[:, sl] * ug[:, sl]
            cdf, _ = _gelu_parts(y)
            gy = (y * cdf).astype(_MXU)
            out = _dot(gy, wz1_ref[j]) * _sigmoid(_dot(gy, wz2_ref[j]))
            out_scr[:, sl] = out
            part = jnp.sum(out * out, axis=-1, keepdims=True)
            ssq = part if ssq is None else ssq + part
        r = lax.rsqrt(ssq / width + RMS_EPS)
        out_scr[...] = out_scr[...] * r * g_ref[...]
        _ungroup(out_scr, blocks, seg)
        for k in range(n_blk):
            o_ref[:, k * LANES:(k + 1) * LANES] = blocks[k].astype(_MXU)

    half = pl.BlockSpec((tm, width), lambda i: (i, 0))
    state = pl.BlockSpec((tm, n_all), lambda i: (i, 0))
    return _call(
        body, (u, table, b_re, b_im, c_re, c_im, wz1, wz2, d_skip, g_out), name=name, grid=(n_rows // tm,),
        out_shape=[jax.ShapeDtypeStruct((n_rows, n_all), F32)] * 2 + [jax.ShapeDtypeStruct((n_rows, width), _MXU)],
        in_specs=[_lane_blocks(n_blk, tm), _resident(), _full(b_re.shape), _full(b_im.shape), _full(c_re.shape),
                  _full(c_im.shape), _full(wz1.shape), _full(wz2.shape), _full((1, width)), _full((1, width))],
        out_specs=[state, state, half],
        scratch_shapes=[pltpu.VMEM((1, n_all), F32)] * 2 + [pltpu.VMEM((tm, width), F32)] * 2
        + [pltpu.VMEM((n_blk, tm, LANES), F32)], xchg=xchg)


def _ssm_backward(u, x_re, x_im, dcat, table, b_re, b_im, c_re, c_im, wz1, wz2, d_skip, g_out, name, xchg=None):
    n_rows = u.shape[1]
    n_blk, _, n_state = b_re.shape
    width, n_all = n_blk * LANES, n_blk * n_state
    tm = _row_tile(n_rows)
    n_t, seg = n_rows // tm, tm // SUBLANES

    def body(u_ref, xre_ref, xim_ref, dc_ref, tab_ref, bre_ref, bim_ref, cre_ref, cim_ref, wz1_ref, wz2_ref, d_ref, g_ref,
             du_ref, dg_ref, dd_ref, dwz1_ref, dwz2_ref, dcre_ref, dcim_ref, dbre_ref, dbim_ref, are_ref, aim_ref,
             car_re, car_im, gre, gim, ug, y_s, z1_s, sg_s, out_s, gy_s):
        @pl.when(pl.program_id(0) == 0)
        def _():
            for ref in (dg_ref, dd_ref, dwz1_ref, dwz2_ref, dcre_ref, dcim_ref, dbre_ref, dbim_ref, are_ref, aim_ref,
                        car_re, car_im):
                ref[...] = jnp.zeros_like(ref)

        _regroup(u_ref, ug, seg)
        ssq = None
        for j in range(n_blk):
            sl = slice(j * LANES, (j + 1) * LANES)
            st = slice(j * n_state, (j + 1) * n_state)
            yc = _dot(xre_ref[:, st].astype(_MXU), cre_ref[j]) - _dot(xim_ref[:, st].astype(_MXU), cim_ref[j])
            y = yc + d_ref[:, sl] * ug[:, sl]
            cdf, _ = _gelu_parts(y)
            gy = (y * cdf).astype(_MXU)
            z1 = _dot(gy, wz1_ref[j])
            sg = _sigmoid(_dot(gy, wz2_ref[j]))
            out = z1 * sg
            y_s[:, sl], z1_s[:, sl], sg_s[:, sl], out_s[:, sl], gy_s[:, sl] = y, z1, sg, out, gy
            part = jnp.sum(out * out, axis=-1, keepdims=True)
            ssq = part if ssq is None else ssq + part
        r = lax.rsqrt(ssq / width + RMS_EPS)
        ohat = out_s[...] * r
        _regroup(dc_ref, out_s, seg)
        dcv = out_s[...]
        dg_ref[...] += jnp.sum(dcv * ohat, axis=0, keepdims=True)
        doh = dcv * g_ref[...]
        out_s[...] = r * (doh - ohat * (jnp.sum(doh * ohat, axis=-1, keepdims=True) / width))

        for j in range(n_blk):
            sl = slice(j * LANES, (j + 1) * LANES)
            st = slice(j * n_state, (j + 1) * n_state)
            dout, sg, z1, y = out_s[:, sl], sg_s[:, sl], z1_s[:, sl], y_s[:, sl]
            dz1 = (dout * sg).astype(_MXU)
            dz2 = (dout * z1 * sg * (1.0 - sg)).astype(_MXU)
            gy = gy_s[:, sl]
            dwz1_ref[j] += _dot_tn(gy, dz1)
            dwz2_ref[j] += _dot_tn(gy, dz2)
            dgy = _dot_nt(dz1, wz1_ref[j]) + _dot_nt(dz2, wz2_ref[j])
            cdf, th = _gelu_parts(y)
            dy = dgy * (cdf + y * (0.5 * (1.0 - th * th) * GELU_C0 * (1.0 + 3.0 * GELU_C1 * (y * y))))
            uj = ug[:, sl]
            dd_ref[:, sl] += jnp.sum(dy * uj, axis=0, keepdims=True)
            z1_s[:, sl] = d_ref[:, sl] * dy
            dyb = dy.astype(_MXU)
            dcre_ref[j] += _dot_tn(dyb, xre_ref[:, st].astype(_MXU))
            dcim_ref[j] -= _dot_tn(dyb, xim_ref[:, st].astype(_MXU))
            gre[:, st] = _dot_nt(dyb, cre_ref[j])
            gim[:, st] = -_dot_nt(dyb, cim_ref[j])

        sums = _segment_scan(gre, gim, tab_ref, car_re, car_im, seg, reverse=True, x_refs=(xre_ref, xim_ref))
        for cb, (sum_re, sum_im) in enumerate(sums):
            cols = pl.ds(cb * SCAN_LANES, SCAN_LANES)
            are_ref[:, cols] += sum_re
            aim_ref[:, cols] += sum_im

        for j in range(n_blk):
            sl = slice(j * LANES, (j + 1) * LANES)
            st = slice(j * n_state, (j + 1) * n_state)
            ujb = ug[:, sl].astype(_MXU)
            grb, gib = gre[:, st].astype(_MXU), gim[:, st].astype(_MXU)
            dbre_ref[j] += _dot_tn(ujb, grb)
            dbim_ref[j] += _dot_tn(ujb, gib)
            z1_s[:, sl] += _dot_nt(grb, bre_ref[j]) + _dot_nt(gib, bim_ref[j])
        _ungroup(z1_s, du_ref, seg)

    half = _lane_blocks(n_blk, tm, lambda i: n_t - 1 - i)
    state = pl.BlockSpec((tm, n_all), lambda i: (n_t - 1 - i, 0))
    small = [(1, width), (1, width), wz1.shape, wz2.shape, (n_blk, LANES, n_state), (n_blk, LANES, n_state),
             (n_blk, LANES, n_state), (n_blk, LANES, n_state), (1, n_all), (1, n_all)]
    return _call(
        body, (u, x_re, x_im, dcat, table, b_re, b_im, c_re, c_im, wz1, wz2, d_skip, g_out), name=name, grid=(n_t,),
        out_shape=[jax.ShapeDtypeStruct((n_blk, n_rows, LANES), F32)] + [jax.ShapeDtypeStruct(s, F32) for s in small],
        in_specs=[half, state, state, half, _resident(), _full(b_re.shape), _full(b_im.shape), _full(c_re.shape),
                  _full(c_im.shape), _full(wz1.shape), _full(wz2.shape), _full((1, width)), _full((1, width))],
        out_specs=[half] + [_full(s) for s in small],
        scratch_shapes=[pltpu.VMEM((1, n_all), F32)] * 2 + [pltpu.VMEM((tm, n_all), F32)] * 2
        + [pltpu.VMEM((tm, width), F32)] * 5 + [pltpu.VMEM((tm, width), _MXU)], xchg=xchg)


def _pool_counts(tile, tm, window):
    t = tile * tm + lax.broadcasted_iota(jnp.int32, (tm, 1), 0)
    return jnp.minimum(t + 1, window).astype(F32)


def _pool_fwd(proj, pool_w, scale, g_out, name):
    n_rows = proj.shape[0]
    n_grp, grp, _ = pool_w.shape
    width = n_grp * grp
    tm = _row_tile(n_rows)

    def body(u_ref, pw_ref, sc_ref, g_ref, o_ref, ext, y_s):
        i = pl.program_id(0)

        @pl.when(i == 0)
        def _():
            ext[0:POOL_HALO, :] = jnp.zeros((POOL_HALO, width), F32)

        ext[POOL_HALO:, :] = u_ref[...]
        ssq = None
        for gi, w in enumerate(POOL_WINDOWS):
            sl = slice(gi * grp, (gi + 1) * grp)
            tot = ext[POOL_HALO:, sl]
            for k in range(1, w):
                tot = tot + ext[POOL_HALO - k:POOL_HALO - k + tm, sl]
            pooled = tot / _pool_counts(i, tm, w) - u_ref[:, sl]
            y = _dot(pooled.astype(_MXU), pw_ref[gi]) * sc_ref[:, sl]
            y_s[:, sl] = y
            part = jnp.sum(y * y, axis=-1, keepdims=True)
            ssq = part if ssq is None else ssq + part
        r = lax.rsqrt(ssq / width + RMS_EPS)
        o_ref[...] = (y_s[...] * r * g_ref[...]).astype(_MXU)
        ext[0:POOL_HALO, :] = u_ref[tm - POOL_HALO:, :]

    half_in = pl.BlockSpec((tm, width), lambda i: (i, 0))
    half = pl.BlockSpec((tm, width), lambda i: (i, 0))
    return pl.pallas_call(
        body, name=name, grid=(n_rows // tm,), out_shape=jax.ShapeDtypeStruct((n_rows, width), _MXU),
        in_specs=[half_in, _full(pool_w.shape), _full((1, width)), _full((1, width))], out_specs=half,
        scratch_shapes=[pltpu.VMEM((tm + POOL_HALO, width), F32), pltpu.VMEM((tm, width), F32)],
        compiler_params=_params(),
    )(proj, pool_w, scale, g_out)


def _pool_bwd(proj, dcat, pool_w, scale, g_out, name):
    n_rows = proj.shape[0]
    n_grp, grp, _ = pool_w.shape
    width = n_grp * grp
    tm = _row_tile(n_rows)
    n_t = n_rows // tm
    halo_blocks = tm // POOL_HALO

    def body(u_ref, up_ref, dc_ref, pw_ref, sc_ref, g_ref, du_ref, dg_ref, dsc_ref, dpw_ref, ext, qext, y_s, pl_s):
        i = pl.program_id(0)
        tile = n_t - 1 - i

        @pl.when(i == 0)
        def _():
            for ref in (dg_ref, dsc_ref, dpw_ref):
                ref[...] = jnp.zeros_like(ref)
            qext[tm:, :] = jnp.zeros((POOL_HALO, width), F32)

        ext[0:POOL_HALO, :] = jnp.where(tile > 0, up_ref[...], 0.0)
        ext[POOL_HALO:, :] = u_ref[...]
        ssq = None
        for gi, w in enumerate(POOL_WINDOWS):
            sl = slice(gi * grp, (gi + 1) * grp)
            tot = ext[POOL_HALO:, sl]
            for k in range(1, w):
                tot = tot + ext[POOL_HALO - k:POOL_HALO - k + tm, sl]
            pooled = (tot / _pool_counts(tile, tm, w) - u_ref[:, sl]).astype(_MXU)
            pl_s[:, sl] = pooled
            y0 = _dot(pooled, pw_ref[gi])
            y_s[:, sl] = y0
            y = y0 * sc_ref[:, sl]
            part = jnp.sum(y * y, axis=-1, keepdims=True)
            ssq = part if ssq is None else ssq + part
        r = lax.rsqrt(ssq / width + RMS_EPS)
        y0 = y_s[...]
        yhat = y0 * sc_ref[...] * r
        dcv = dc_ref[...]
        dg_ref[...] += jnp.sum(dcv * yhat, axis=0, keepdims=True)
        dyh = dcv * g_ref[...]
        dy = r * (dyh - yhat * (jnp.sum(dyh * yhat, axis=-1, keepdims=True) / width))
        dsc_ref[...] += jnp.sum(dy * y0, axis=0, keepdims=True)
        y_s[...] = dy * sc_ref[...]
        for gi, w in enumerate(POOL_WINDOWS):
            sl = slice(gi * grp, (gi + 1) * grp)
            dm = y_s[:, sl].astype(_MXU)
            dpw_ref[gi] += _dot_tn(pl_s[:, sl], dm)
            dpooled = _dot_nt(dm, pw_ref[gi])
            y_s[:, sl] = dpooled
            qext[0:tm, sl] = dpooled / _pool_counts(tile, tm, w)
        for gi, w in enumerate(POOL_WINDOWS):
            sl = slice(gi * grp, (gi + 1) * grp)
            tot = qext[0:tm, sl]
            for k in range(1, w):
                tot = tot + qext[k:k + tm, sl]
            du_ref[:, sl] = tot - y_s[:, sl]
        qext[tm:, :] = qext[0:POOL_HALO, :]

    half_in = pl.BlockSpec((tm, width), lambda i: (n_t - 1 - i, 0))
    prev = pl.BlockSpec((POOL_HALO, width), lambda i: (jnp.maximum((n_t - 1 - i) * halo_blocks - 1, 0), 0))
    half = pl.BlockSpec((tm, width), lambda i: (n_t - 1 - i, 0))
    return pl.pallas_call(
        body, name=name, grid=(n_t,),
        out_shape=[jax.ShapeDtypeStruct((n_rows, width), F32), jax.ShapeDtypeStruct((1, width), F32),
                   jax.ShapeDtypeStruct((1, width), F32), jax.ShapeDtypeStruct(pool_w.shape, F32)],
        in_specs=[half_in, prev, half, _full(pool_w.shape), _full((1, width)), _full((1, width))],
        out_specs=[half, _full((1, width)), _full((1, width)), _full(pool_w.shape)],
        scratch_shapes=[pltpu.VMEM((tm + POOL_HALO, width), F32), pltpu.VMEM((tm + POOL_HALO, width), F32),
                        pltpu.VMEM((tm, width), F32), pltpu.VMEM((tm, width), _MXU)],
        compiler_params=_params(),
    )(proj, proj, dcat, pool_w, scale, g_out)


def _mix_out_fwd(cat_s, cat_p, h, g, wo_s, wo_p, name):
    n_rows, d = h.shape
    width = cat_s.shape[1]
    tm = _row_tile(n_rows)

    def body(cs_ref, cp_ref, h_ref, g_ref, ws_ref, wp_ref, m_ref, ho_ref):
        m = _dot(cs_ref[...], ws_ref[...]) + _dot(cp_ref[...], wp_ref[...])
        m_ref[...] = m
        ho_ref[...] = h_ref[...] + m * _rs(m) * g_ref[...]

    row = pl.BlockSpec((tm, d), lambda i: (i, 0))
    half = pl.BlockSpec((tm, width), lambda i: (i, 0))
    return pl.pallas_call(
        body, name=name, grid=(n_rows // tm,), out_shape=[jax.ShapeDtypeStruct((n_rows, d), F32)] * 2,
        in_specs=[half, half, row, _full((1, d)), _resident(), _resident()], out_specs=[row, row],
        compiler_params=_params(),
    )(cat_s, cat_p, h, g, wo_s, wo_p)


def _mix_out_bwd(dho, mixed, g, wo_s, wo_p, name):
    n_rows, d = mixed.shape
    width = wo_s.shape[0]
    n_blk = width // LANES
    tm = _row_tile(n_rows)

    def body(dho_ref, m_ref, g_ref, ws_ref, wp_ref, dm_ref, dcs_ref, dcp_ref, dg_ref):
        @pl.when(pl.program_id(0) == 0)
        def _():
            dg_ref[...] = jnp.zeros_like(dg_ref)

        m = m_ref[...]
        r = _rs(m)
        mh = m * r
        dy = dho_ref[...]
        dg_ref[...] += jnp.sum(dy * mh, axis=0, keepdims=True)
        dmh = dy * g_ref[...]
        dm = (r * (dmh - mh * jnp.mean(dmh * mh, axis=-1, keepdims=True))).astype(_MXU)
        dm_ref[...] = dm
        dcs = _dot_nt(dm, ws_ref[...])
        for k in range(n_blk):
            dcs_ref[k] = dcs[:, k * LANES:(k + 1) * LANES]
        dcp_ref[...] = _dot_nt(dm, wp_ref[...])

    row = pl.BlockSpec((tm, d), lambda i: (i, 0))
    half = pl.BlockSpec((tm, width), lambda i: (i, 0))
    return pl.pallas_call(
        body, name=name, grid=(n_rows // tm,),
        out_shape=[jax.ShapeDtypeStruct((n_rows, d), _MXU), jax.ShapeDtypeStruct((n_blk, n_rows, LANES), F32),
                   jax.ShapeDtypeStruct((n_rows, width), F32), jax.ShapeDtypeStruct((1, d), F32)],
        in_specs=[row, row, _full((1, d)), _resident(), _resident()],
        out_specs=[row, _lane_blocks(n_blk, tm), half, _full((1, d))], compiler_params=_params(),
    )(dho, mixed, g, wo_s, wo_p)


def _mix_in_bwd(du_s, du_p, h, dho, g, wi_s, wi_p, name):
    n_rows, d = h.shape
    width = du_p.shape[1]
    n_blk = width // LANES
    tm = _row_tile(n_rows)

    def body(dus_ref, dup_ref, h_ref, dho_ref, g_ref, ws_ref, wp_ref, dh_ref, dp_ref, dg_ref):
        @pl.when(pl.program_id(0) == 0)
        def _():
            dg_ref[...] = jnp.zeros_like(dg_ref)

        for k in range(n_blk):
            dp_ref[:, k * LANES:(k + 1) * LANES] = dus_ref[k].astype(_MXU)
        dup = dup_ref[...].astype(_MXU)
        dp_ref[:, width:2 * width] = dup
        dn = _dot_nt(dp_ref[:, 0:width], ws_ref[...]) + _dot_nt(dup, wp_ref[...])
        hv = h_ref[...]
        r = _rs(hv)
        hh = hv * r
        dg_ref[...] += jnp.sum(dn * hh, axis=0, keepdims=True)
        dhh = dn * g_ref[...]
        dh_ref[...] = dho_ref[...] + r * (dhh - hh * jnp.mean(dhh * hh, axis=-1, keepdims=True))

    row = pl.BlockSpec((tm, d), lambda i: (i, 0))
    half = pl.BlockSpec((tm, width), lambda i: (i, 0))
    return pl.pallas_call(
        body, name=name, grid=(n_rows // tm,),
        out_shape=[jax.ShapeDtypeStruct((n_rows, d), F32), jax.ShapeDtypeStruct((n_rows, 2 * width), _MXU),
                   jax.ShapeDtypeStruct((1, d), F32)],
        in_specs=[_lane_blocks(n_blk, tm), half, row, row, _full((1, d)), _resident(), _resident()],
        out_specs=[row, pl.BlockSpec((tm, 2 * width), lambda i: (i, 0)), _full((1, d))], compiler_params=_params(),
    )(du_s, du_p, h, dho, g, wi_s, wi_p)


def _loss_grad(h, target, name):
    n_rows, d = h.shape
    tm = _row_tile(n_rows)

    def body(h_ref, t_ref, dh_ref, l_ref):
        i = pl.program_id(0)

        @pl.when(i == 0)
        def _():
            l_ref[...] = jnp.zeros_like(l_ref)

        rows = i * tm + lax.broadcasted_iota(jnp.int32, (tm, 1), 0)
        err = jnp.where(rows >= N_META, h_ref[...] - t_ref[...], 0.0)
        dh_ref[...] = err / d
        l_ref[...] += jnp.sum(jnp.sum(err * err, axis=0, keepdims=True), axis=1, keepdims=True)

    row = pl.BlockSpec((tm, d), lambda i: (i, 0))
    return pl.pallas_call(
        body, name=name, grid=(n_rows // tm,),
        out_shape=[jax.ShapeDtypeStruct((n_rows, d), F32), jax.ShapeDtypeStruct((1, LANES), F32)],
        in_specs=[row, row], out_specs=[row, _full((1, LANES))], compiler_params=_params(),
    )(h, target)


def _adamw_update(w_ref, gv, m_ref, v_ref, d_ref, mo_ref, vo_ref):
    mn = ADAM_B1 * m_ref[...] + (1.0 - ADAM_B1) * gv
    vn = ADAM_B2 * v_ref[...] + (1.0 - ADAM_B2) * (gv * gv)
    m_hat = mn / (1.0 - ADAM_B1 ** ADAM_STEP)
    v_hat = vn / (1.0 - ADAM_B2 ** ADAM_STEP)
    d_ref[...] = -ADAM_LR * (m_hat / (jnp.sqrt(v_hat) + ADAM_EPS) + ADAM_WD * w_ref[...])
    mo_ref[...] = mn
    vo_ref[...] = vn


def _adamw(w, g, m, v, name):
    def body(w_ref, g_ref, m_ref, v_ref, d_ref, mo_ref, vo_ref):
        _adamw_update(w_ref, g_ref[...], m_ref, v_ref, d_ref, mo_ref, vo_ref)

    spec = _full(w.shape)
    return pl.pallas_call(
        body, name=name, grid=(1,), out_shape=[jax.ShapeDtypeStruct(w.shape, F32)] * 3,
        in_specs=[spec] * 4, out_specs=[spec] * 3, compiler_params=_params(),
    )(w, g, m, v)


def _adamw_slots(w, slots, m, v, name):
    def body(w_ref, s_ref, m_ref, v_ref, g_ref, d_ref, mo_ref, vo_ref):
        gv = s_ref[0].astype(F32)
        for k in range(1, N_DEV):
            gv = gv + s_ref[k].astype(F32)
        g_ref[...] = gv
        _adamw_update(w_ref, gv, m_ref, v_ref, d_ref, mo_ref, vo_ref)

    spec = _full(w.shape)
    return pl.pallas_call(
        body, name=name, grid=(1,), out_shape=[jax.ShapeDtypeStruct(w.shape, F32)] * 4,
        in_specs=[spec, _full(slots.shape), spec, spec], out_specs=[spec] * 4, compiler_params=_params(),
    )(w, slots, m, v)


def _discretize(lam_re, lam_im, log_dt, b_re, b_im):
    dt = jnp.exp(log_dt)[:, None]
    decay = jnp.exp(lam_re * dt)
    ang = lam_im * dt
    a_re = decay * jnp.cos(ang)
    a_im = decay * jnp.sin(ang)
    nr = a_re - 1.0
    den = lam_re * lam_re + lam_im * lam_im
    q_re = (nr * lam_re + a_im * lam_im) / den
    q_im = (a_im * lam_re - nr * lam_im) / den
    bb_re = q_re[..., None] * b_re - q_im[..., None] * b_im
    bb_im = q_re[..., None] * b_im + q_im[..., None] * b_re
    return a_re, a_im, bb_re, bb_im


def _cmul(a, b):
    return a[0] * b[0] - a[1] * b[1], a[0] * b[1] + a[1] * b[0]


def _scan_coefficients(a_re, a_im, reverse):
    a = (a_re.reshape(1, -1), -a_im.reshape(1, -1) if reverse else a_im.reshape(1, -1))
    powers = [a]
    for _ in range(SUBLANES - 1):
        powers.append(_cmul(powers[-1], a))
    row = jnp.arange(SUBLANES)[:, None]
    out = []
    for k in (1, 2, 4):
        keep = (row < SUBLANES - k) if reverse else (row >= k)
        out += [jnp.where(keep, powers[k - 1][0], 0.0), jnp.where(keep, powers[k - 1][1], 0.0)]
    order = range(SUBLANES - 1, -1, -1) if reverse else range(SUBLANES)
    out += [jnp.concatenate([powers[t][0] for t in order], axis=0), jnp.concatenate([powers[t][1] for t in order], axis=0)]
    return jnp.stack(out).astype(F32)


def _power_table(a_re, a_im, count, reverse):
    base = (a_re.reshape(1, -1), (-a_im if reverse else a_im).reshape(1, -1))
    exponent = (jnp.arange(count, 0, -1) if reverse else jnp.arange(1, count + 1))[:, None]
    shape = (count, base[0].shape[1])
    res = (jnp.ones(shape, F32), jnp.zeros(shape, F32))
    for bit in range(int(count).bit_length()):
        prod = _cmul(res, base)
        take = ((exponent >> bit) & 1) == 1
        res = (jnp.where(take, prod[0], res[0]), jnp.where(take, prod[1], res[1]))
        base = _cmul(base, base)
    return jnp.broadcast_to(jnp.stack(res)[:, :, None, :], (2, count, SUBLANES, shape[1]))


def _block_diag(p, n_blk):
    g, r, c = p.shape
    per = g // n_blk
    eye = jnp.eye(per, dtype=p.dtype)
    return jnp.einsum("jgrc,gk->jgrkc", p.reshape(n_blk, per, r, c), eye).reshape(n_blk, per * r, per * c)


def _block_diag_t(m, g):
    n_blk = m.shape[0]
    per = g // n_blk
    r, c = m.shape[1] // per, m.shape[2] // per
    eye = jnp.eye(per, dtype=m.dtype)
    return jnp.einsum("jgrkc,gk->jgrc", m.reshape(n_blk, per, r, per, c), eye).reshape(g, r, c)


def _pack_rows(parts, cols, multiple):
    flat = jnp.concatenate([p.reshape(-1) for p in parts])
    size = -(-flat.shape[0] // (cols * multiple)) * cols * multiple
    return jnp.pad(flat, (0, size - flat.shape[0])).reshape(-1, cols)


def _unpack(flat, shapes):
    out, pos = [], 0
    flat = flat.reshape(-1)
    for s in shapes:
        n = int(np.prod(s))
        out.append(flat[pos:pos + n].reshape(s))
        pos += n
    return out


def kernel(x, meta_tokens, ffn1_pre_norm, ffn1_post_norm, ffn1_w_gate, ffn1_w_up, ffn1_w_down, mix_pre_norm, mix_post_norm, w_in, ssm_lambda_re, ssm_lambda_im, ssm_log_dt, ssm_b_re, ssm_b_im, ssm_c_re, ssm_c_im, ssm_d, ssm_w_glu, pool_w, pool_scale, ssm_out_norm, pool_out_norm, w_out, ffn2_pre_norm, ffn2_post_norm, ffn2_w_gate, ffn2_w_up, ffn2_w_down, loss_target, m_meta_tokens, m_ffn1_pre_norm, m_ffn1_post_norm, m_ffn1_w_gate, m_ffn1_w_up, m_ffn1_w_down, m_mix_pre_norm, m_mix_post_norm, m_w_in, m_ssm_lambda_re, m_ssm_lambda_im, m_ssm_log_dt, m_ssm_b_re, m_ssm_b_im, m_ssm_c_re, m_ssm_c_im, m_ssm_d, m_ssm_w_glu, m_pool_w, m_pool_scale, m_ssm_out_norm, m_pool_out_norm, m_w_out, m_ffn2_pre_norm, m_ffn2_post_norm, m_ffn2_w_gate, m_ffn2_w_up, m_ffn2_w_down, v_meta_tokens, v_ffn1_pre_norm, v_ffn1_post_norm, v_ffn1_w_gate, v_ffn1_w_up, v_ffn1_w_down, v_mix_pre_norm, v_mix_post_norm, v_w_in, v_ssm_lambda_re, v_ssm_lambda_im, v_ssm_log_dt, v_ssm_b_re, v_ssm_b_im, v_ssm_c_re, v_ssm_c_im, v_ssm_d, v_ssm_w_glu, v_pool_w, v_pool_scale, v_ssm_out_norm, v_pool_out_norm, v_w_out, v_ffn2_pre_norm, v_ffn2_post_norm, v_ffn2_w_gate, v_ffn2_w_up, v_ffn2_w_down):
    args = dict(locals())
    names = ["meta_tokens", "ffn1_pre_norm", "ffn1_post_norm", "ffn1_w_gate", "ffn1_w_up", "ffn1_w_down", "mix_pre_norm",
             "mix_post_norm", "w_in", "ssm_lambda_re", "ssm_lambda_im", "ssm_log_dt", "ssm_b_re", "ssm_b_im", "ssm_c_re",
             "ssm_c_im", "ssm_d", "ssm_w_glu", "pool_w", "pool_scale", "ssm_out_norm", "pool_out_norm", "w_out",
             "ffn2_pre_norm", "ffn2_post_norm", "ffn2_w_gate", "ffn2_w_up", "ffn2_w_down"]
    sharded = ("meta_tokens", "ffn1_w_gate", "ffn1_w_up", "ffn1_w_down", "w_in", "w_out", "ffn2_w_gate", "ffn2_w_up",
               "ffn2_w_down")
    small = [n for n in names if n not in sharded]

    d = x.shape[-1]
    width = d // 2
    n_grp = ssm_lambda_re.shape[1]
    n_blk = width // LANES

    def stacked(gathered):
        return gathered.reshape(-1, d).astype(_MXU)

    def chunked(m):
        return m.reshape(N_DEV, -1, d)

    s_gate1, s_up1, s_down1 = _to_wire([ffn1_w_gate[0].T, ffn1_w_up[0].T, ffn1_w_down[0]], "wire_ffn1")
    s_win, s_wout, s_gate2, s_up2, s_down2 = _to_wire(
        [w_in[0], w_out[0], ffn2_w_gate[0].T, ffn2_w_up[0].T, ffn2_w_down[0]], "wire_rest")
    g_gate1, g_up1, g_down1, meta_all = _gather_two_level([s_gate1, s_up1, s_down1, meta_tokens], "gather_ffn1")
    wgt1, wut1, wd1 = stacked(g_gate1), stacked(g_up1), stacked(g_down1)
    meta_full = jnp.transpose(meta_all, (1, 0, 2)).reshape(N_META, d)

    h0 = jnp.concatenate([meta_full, x[0]], axis=0)
    target = jnp.concatenate([jnp.zeros((N_META, d), F32), loss_target[0]], axis=0)

    (a1, b1, f1, h1), (g_win, g_wout, g_gate2) = _ffn_fwd(
        h0, ffn1_pre_norm, ffn1_post_norm, wgt1, wut1, wd1, "ffn1_fwd",
        xchg=_Xchg([s_win, s_wout, s_gate2], ["gather"] * 3))
    w_in_f, w_out_f = stacked(g_win), stacked(g_wout)

    a_re, a_im, bb_re, bb_im = _discretize(ssm_lambda_re[0], ssm_lambda_im[0], ssm_log_dt[0], ssm_b_re[0], ssm_b_im[0])
    steps = _row_tile(N_META + x.shape[1]) // SUBLANES
    coef = _power_table(a_re, a_im, steps, reverse=False)
    coef_rev = _power_table(a_re, a_im, steps, reverse=True)
    bmat_re = _block_diag(jnp.swapaxes(bb_re, 1, 2), n_blk).astype(_MXU)
    bmat_im = _block_diag(jnp.swapaxes(bb_im, 1, 2), n_blk).astype(_MXU)
    cmat_re = _block_diag(jnp.swapaxes(ssm_c_re[0], 1, 2), n_blk).astype(_MXU)
    cmat_im = _block_diag(jnp.swapaxes(ssm_c_im[0], 1, 2), n_blk).astype(_MXU)
    wz1 = _block_diag(ssm_w_glu[0][:, :, :SSM_GROUP_CH], n_blk).astype(_MXU)
    wz2 = _block_diag(ssm_w_glu[0][:, :, SSM_GROUP_CH:], n_blk).astype(_MXU)
    pool_wm = pool_w[0].astype(_MXU)

    n2, u_s, u_p = _mix_in_fwd(h1, mix_pre_norm, w_in_f, "mix_in_fwd")
    (x_re, x_im, cat_s), (g_up2, g_down2) = _ssm_forward(
        u_s, coef, bmat_re, bmat_im, cmat_re, cmat_im, wz1, wz2, ssm_d, ssm_out_norm, "ssm_fwd",
        xchg=_Xchg([s_up2, s_down2], ["gather"] * 2))
    wgt2, wut2, wd2 = stacked(g_gate2), stacked(g_up2), stacked(g_down2)
    cat_p = _pool_fwd(u_p, pool_wm, pool_scale, pool_out_norm, "pool_fwd")
    wo_s, wo_p = w_out_f[:width], w_out_f[width:]
    mixed, h2 = _mix_out_fwd(cat_s, cat_p, h1, mix_post_norm, wo_s, wo_p, "mix_out_fwd")

    (a2, b2, f2, h3), _ = _ffn_fwd(h2, ffn2_pre_norm, ffn2_post_norm, wgt2, wut2, wd2, "ffn2_fwd")
    dh3, sq_err = _loss_grad(h3, target, "loss_grad")

    g, slots = {}, {}
    (dh2, da2, db2, s2, df2, nf2, g["ffn2_pre_norm"], g["ffn2_post_norm"]), _ = _ffn_bwd(
        dh3, h2, f2, a2, b2, ffn2_pre_norm, ffn2_post_norm, wgt2, wut2, wd2, "ffn2_bwd")
    dgate2, _ = _wgrad(da2, nf2, "ffn2_dgate")
    dup2, _ = _wgrad(db2, nf2, "ffn2_dup")
    ddown2, _ = _wgrad(s2, df2, "ffn2_ddown")

    dmixed, dcat_s, dcat_p, g["mix_post_norm"] = _mix_out_bwd(dh2, mixed, mix_post_norm, wo_s, wo_p, "mix_out_bwd")
    dwout = jnp.concatenate([_wgrad(cat_s, dmixed, "dwout_s")[0], _wgrad(cat_p, dmixed, "dwout_p")[0]], axis=0)
    ((du_s, g["ssm_out_norm"], g["ssm_d"], dwz1, dwz2, dcm_re, dcm_im, dbm_re, dbm_im, acc_re, acc_im),
     (slots["ffn2_w_gate"], slots["ffn2_w_up"], slots["ffn2_w_down"])) = _ssm_backward(
        u_s, x_re, x_im, dcat_s, coef_rev, bmat_re, bmat_im, cmat_re, cmat_im, wz1, wz2, ssm_d, ssm_out_norm, "ssm_bwd",
        xchg=_Xchg([chunked(dgate2), chunked(dup2), chunked(ddown2)], ["scatter"] * 3))
    du_p, g["pool_out_norm"], g["pool_scale"], dpw = _pool_bwd(u_p, dcat_p, pool_wm, pool_scale, pool_out_norm, "pool_bwd")
    wi_s, wi_p = w_in_f[:, :width], w_in_f[:, width:]
    dh1, dproj, g["mix_pre_norm"] = _mix_in_bwd(du_s, du_p, h1, dh2, mix_pre_norm, wi_s, wi_p, "mix_in_bwd")
    dwin, _ = _wgrad(n2, dproj, "dwin")

    g["ssm_c_re"] = jnp.swapaxes(_block_diag_t(jnp.swapaxes(dcm_re, 1, 2), n_grp), 1, 2)[None]
    g["ssm_c_im"] = jnp.swapaxes(_block_diag_t(jnp.swapaxes(dcm_im, 1, 2), n_grp), 1, 2)[None]
    g["ssm_w_glu"] = jnp.concatenate([_block_diag_t(dwz1, n_grp), _block_diag_t(dwz2, n_grp)], axis=-1)[None]
    dbb_re = jnp.swapaxes(_block_diag_t(dbm_re, n_grp), 1, 2)
    dbb_im = jnp.swapaxes(_block_diag_t(dbm_im, n_grp), 1, 2)
    da_re, da_im = acc_re.reshape(a_re.shape), acc_im.reshape(a_re.shape)
    _, disc_vjp = jax.vjp(_discretize, ssm_lambda_re[0], ssm_lambda_im[0], ssm_log_dt[0], ssm_b_re[0], ssm_b_im[0])
    d_lre, d_lim, d_ldt, d_bre, d_bim = disc_vjp((da_re, da_im, dbb_re, dbb_im))
    g["ssm_lambda_re"], g["ssm_lambda_im"], g["ssm_log_dt"] = d_lre[None], d_lim[None], d_ldt[None]
    g["ssm_b_re"], g["ssm_b_im"] = d_bre[None], d_bim[None]
    g["pool_w"] = dpw[None]

    late = ["ffn1_pre_norm", "ffn1_post_norm"]
    early = [n for n in small if n not in late]
    early_vec = _pack_rows([g[n] for n in early] + [sq_err[:, :1]], 1024, SUBLANES)
    ((dh0, da1, db1, s1, df1, nf1, g["ffn1_pre_norm"], g["ffn1_post_norm"]),
     (slots["w_out"], slots["w_in"], recv_early)) = _ffn_bwd(
        dh1, h0, f1, a1, b1, ffn1_pre_norm, ffn1_post_norm, wgt1, wut1, wd1, "ffn1_bwd",
        xchg=_Xchg([chunked(dwout), chunked(dwin), early_vec], ["scatter", "scatter", "gather"]))
    late_vec = _pack_rows([g[n] for n in late] + [dh0[:N_META]], 1024, SUBLANES)
    dgate1, (recv_late,) = _wgrad(da1, nf1, "ffn1_dgate", xchg=_Xchg([late_vec], ["gather"]))
    dup1, (slots["ffn1_w_gate"],) = _wgrad(db1, nf1, "ffn1_dup", xchg=_Xchg([chunked(dgate1)], ["scatter"]))
    ddown1, (slots["ffn1_w_up"],) = _wgrad(s1, df1, "ffn1_ddown", xchg=_Xchg([chunked(dup1)], ["scatter"]))
    (slots["ffn1_w_down"],) = _exchange([chunked(ddown1)], ["scatter"], "reduce_last")

    summed = _unpack(_sum_slots(recv_early, "sum_small_grads"), [g[n].shape for n in early] + [(1,)])
    for n, val in zip(early, summed):
        g[n] = val
    loss = (0.5 / d) * summed[-1][0]
    g[late[0]], g[late[1]], dmeta = _unpack(_sum_slots(recv_late, "sum_last_grads"), [(1, d), (1, d), (N_META, d)])
    g["meta_tokens"] = lax.dynamic_slice_in_dim(dmeta, _my_slot() * (d // N_DEV), d // N_DEV, axis=1)

    delta, new_m, new_v = {}, {}, {}
    for n in sharded:
        shape = args[n].shape
        two_d = (-1, shape[-1])
        w2, m2, v2 = args[n].reshape(two_d), args["m_" + n].reshape(two_d), args["v_" + n].reshape(two_d)
        if n == "meta_tokens":
            dl, mn, vn = _adamw(w2, g[n], m2, v2, "adamw_" + n)
        elif n.endswith("gate") or n.endswith("up"):
            gs, dl, mn, vn = [t.T for t in _adamw_slots(w2.T, slots[n], m2.T, v2.T, "adamw_" + n)]
        else:
            gs, dl, mn, vn = _adamw_slots(w2, slots[n], m2, v2, "adamw_" + n)
        if n != "meta_tokens":
            g[n] = gs[None]
        delta[n], new_m[n], new_v[n] = dl.reshape(shape), mn.reshape(shape), vn.reshape(shape)
    packs = [_pack_rows([src[n] if pre is None else args[pre + n] for n in small], 1024, SUBLANES)
             for src, pre in ((args, None), (g, None), (None, "m_"), (None, "v_"))]
    outs = _adamw(*packs, "adamw_small")
    shapes = [args[n].shape for n in small]
    for store, flat in zip((delta, new_m, new_v), outs):
        for n, val in zip(small, _unpack(flat, shapes)):
            store[n] = val

    grad_x = dh0[N_META:][None]
    return (loss, grad_x, *[g[n] for n in names], *[delta[n] for n in names], *[new_m[n] for n in names],
            *[new_v[n] for n in names])
```

```python
import functools
import math

import jax
import jax.numpy as jnp
import numpy as np
from jax import lax
from jax.experimental import pallas as pl
from jax.experimental.pallas import tpu as pltpu

F32 = jnp.float32
_MXU = jnp.bfloat16
_ACT = jnp.bfloat16
_WIRE = jnp.bfloat16

N_DEV = 8
N_META = 16
RMS_EPS = 1e-6
SSM_GROUP_CH = 16
SSM_STATE = 64
LANES = 128
SUBLANES = 8
POOL_WINDOWS = (2, 4, 8, 16)
POOL_HALO = 16
ADAM_LR = 0.001
ADAM_B1 = 0.9
ADAM_B2 = 0.999
ADAM_EPS = 1e-08
ADAM_WD = 0.01
ADAM_STEP = 10
GELU_C0 = math.sqrt(2.0 / math.pi)
GELU_C1 = 0.044715
VMEM_LIMIT = 56 * 1024 * 1024

_NT = (((1,), (1,)), ((), ()))
_TN = (((0,), (0,)), ((), ()))


def _dot(a, b):
    return jnp.dot(a, b, preferred_element_type=F32)


def _dot_nt(a, b):
    return lax.dot_general(a, b, _NT, preferred_element_type=F32)


def _dot_tn(a, b):
    return lax.dot_general(a, b, _TN, preferred_element_type=F32)


def _rs(x):
    return lax.rsqrt(jnp.mean(x * x, axis=-1, keepdims=True) + RMS_EPS)


def _sigmoid(x):
    return 0.5 * jnp.tanh(0.5 * x) + 0.5


def _row_tile(n_rows, largest=432):
    for t in (432, 304, 48, 16):
        if t <= largest and n_rows % t == 0:
            return t
    raise ValueError(n_rows)


def _ff_chunk(d_ff):
    return d_ff // 2 if (d_ff // 2) % LANES == 0 else d_ff


def _params(n_axes=1):
    return pltpu.CompilerParams(dimension_semantics=("arbitrary",) * n_axes, vmem_limit_bytes=VMEM_LIMIT)


def _resident():
    return pl.BlockSpec(memory_space=pltpu.VMEM)


def _full(shape):
    nd = len(shape)
    return pl.BlockSpec(shape, lambda *_: (0,) * nd)


def _lane_blocks(n_blk, tm, tile_of=lambda i: i):
    return pl.BlockSpec((n_blk, tm, LANES), lambda i: (0, tile_of(i), 0))


PEER_ORDER = (1, 2, 4, 3, 5, 6, 7)


def _split(refs, counts):
    out, pos = [], 0
    for n in counts:
        out.append(refs[pos:pos + n])
        pos += n
    return out


def _peer(r):
    x, y, c = lax.axis_index("x"), lax.axis_index("y"), lax.axis_index("c")
    return (1 - x if r & 4 else x, 1 - y if r & 2 else y, 1 - c if r & 1 else c)


def _my_slot():
    return 4 * lax.axis_index("x") + 2 * lax.axis_index("y") + lax.axis_index("c")


class _Xchg:
    def __init__(self, srcs, kinds):
        self.srcs, self.kinds, self.n = list(srcs), list(kinds), len(srcs)
        self.out_shape = [jax.ShapeDtypeStruct((N_DEV,) + s.shape if k == "gather" else s.shape, s.dtype)
                          for s, k in zip(self.srcs, self.kinds)]
        self.specs = [pl.BlockSpec(memory_space=pl.ANY)] * self.n
        self.scratch = [pltpu.SemaphoreType.DMA((self.n * (N_DEV - 1),)), pltpu.SemaphoreType.DMA((self.n * (N_DEV - 1),)),
                        pltpu.SemaphoreType.DMA((self.n,))]

    def copies(self, src, dst, sems):
        send_sems, recv_sems, local_sems = sems
        me = _my_slot()
        out = []
        for a in range(self.n):
            mine = src[a] if self.kinds[a] == "gather" else src[a].at[me]
            out.append(pltpu.make_async_copy(mine, dst[a].at[me], local_sems.at[a]))
            for r in PEER_ORDER:
                px, py, pc = _peer(r)
                part = src[a] if self.kinds[a] == "gather" else src[a].at[4 * px + 2 * py + pc]
                k = a * (N_DEV - 1) + r - 1
                out.append(pltpu.make_async_remote_copy(
                    src_ref=part, dst_ref=dst[a].at[me], send_sem=send_sems.at[k], recv_sem=recv_sems.at[k],
                    device_id=(px, py, pc), device_id_type=pl.DeviceIdType.MESH))
        return out

    def start(self, src, dst, sems):
        for cp in self.copies(src, dst, sems):
            cp.start()

    def wait(self, src, dst, sems):
        for cp in self.copies(src, dst, sems):
            cp.wait()


class _NoXchg:
    n, srcs, out_shape, specs, scratch = 0, [], [], [], []

    def start(self, *_):
        pass

    wait = start


def _call(body, args, *, name, grid, out_shape, in_specs, out_specs, scratch_shapes=(), xchg=None):
    xc = xchg or _NoXchg()
    counts = (len(in_specs), xc.n, len(out_shape), xc.n, len(scratch_shapes), len(xc.scratch))

    def wrapped(*refs):
        ins, xsrc, outs, xdst, scr, sems = _split(refs, counts)
        ids = [pl.program_id(k) for k in range(len(grid))]
        if xc.n:
            @pl.when(functools.reduce(jnp.logical_and, [i == 0 for i in ids]))
            def _():
                xc.start(xsrc, xdst, sems)

        body(*ins, *outs, *scr)
        if xc.n:
            @pl.when(functools.reduce(jnp.logical_and, [i == g - 1 for i, g in zip(ids, grid)]))
            def _():
                xc.wait(xsrc, xdst, sems)

    res = pl.pallas_call(
        wrapped, name=name, grid=grid, out_shape=list(out_shape) + xc.out_shape,
        in_specs=list(in_specs) + xc.specs, out_specs=list(out_specs) + xc.specs,
        scratch_shapes=list(scratch_shapes) + xc.scratch, compiler_params=_params(len(grid)),
    )(*args, *xc.srcs)
    return res[:len(out_shape)], res[len(out_shape):]


def _exchange(srcs, kinds, name):
    xc = _Xchg(srcs, kinds)

    def body(*refs):
        src, dst, sems = _split(refs, (xc.n, xc.n, 3))
        xc.start(src, dst, sems)
        xc.wait(src, dst, sems)

    return pl.pallas_call(body, name=name, out_shape=xc.out_shape, in_specs=xc.specs, out_specs=xc.specs,
                          scratch_shapes=xc.scratch)(*srcs)


def _gather_two_level(srcs, name):
    n = len(srcs)
    out_shape = [jax.ShapeDtypeStruct((N_DEV,) + s.shape, s.dtype) for s in srcs]
    chips = (2, 4, 6)

    def body(*refs):
        src, dst, (send_sems, recv_sems, local_sems) = _split(refs, (n, n, 3))
        x, y, c = lax.axis_index("x"), lax.axis_index("y"), lax.axis_index("c")
        me = 4 * x + 2 * y + c
        sibling = (x, y, 1 - c)

        def copy(a, k, slot, to, from_src=False):
            return pltpu.make_async_remote_copy(
                src_ref=src[a] if from_src else dst[a].at[slot], dst_ref=dst[a].at[slot],
                send_sem=send_sems.at[a * 7 + k], recv_sem=recv_sems.at[a * 7 + k],
                device_id=to, device_id_type=pl.DeviceIdType.MESH)

        def slot_of(r, core):
            px, py, _ = _peer(r)
            return 4 * px + 2 * py + core

        local = [pltpu.make_async_copy(src[a], dst[a].at[me], local_sems.at[a]) for a in range(n)]
        sent = []
        for a in range(n):
            local[a].start()
            sent.append(copy(a, 0, me, sibling, from_src=True))
            sent += [copy(a, 1 + j, me, _peer(r), from_src=True) for j, r in enumerate(chips)]
        for cp in sent:
            cp.start()
        for j, r in enumerate(chips):
            for a in range(n):
                copy(a, 1 + j, slot_of(r, c), _peer(r)).wait_recv()
                cp = copy(a, 4 + j, slot_of(r, c), sibling)
                cp.start()
                sent.append(cp)
        for a in range(n):
            copy(a, 0, slot_of(0, 1 - c), sibling).wait_recv()
            for j, r in enumerate(chips):
                copy(a, 4 + j, slot_of(r, 1 - c), sibling).wait_recv()
        for cp in local:
            cp.wait()
        for cp in sent:
            cp.wait_send()

    any_spec = pl.BlockSpec(memory_space=pl.ANY)
    return pl.pallas_call(
        body, name=name, out_shape=out_shape, in_specs=[any_spec] * n, out_specs=[any_spec] * n,
        scratch_shapes=[pltpu.SemaphoreType.DMA((n * 7,)), pltpu.SemaphoreType.DMA((n * 7,)), pltpu.SemaphoreType.DMA((n,))],
    )(*srcs)


def _to_wire(mats, name):
    def body(*refs):
        for src, dst in zip(refs[:len(mats)], refs[len(mats):]):
            dst[...] = src[...].astype(_WIRE)

    return pl.pallas_call(
        body, name=name, grid=(1,), out_shape=[jax.ShapeDtypeStruct(m.shape, _WIRE) for m in mats],
        in_specs=[_full(m.shape) for m in mats], out_specs=[_full(m.shape) for m in mats], compiler_params=_params(),
    )(*mats)


def _sum_slots(r, name):
    _, rows, cols = r.shape
    blk = rows
    for cand in (rows, 592, 512, 256, 128, 64, 32, 16):
        if rows % cand == 0 and N_DEV * cand * cols * r.dtype.itemsize <= 8 * 1024 * 1024:
            blk = cand
            break

    def body(r_ref, o_ref):
        acc = r_ref[0].astype(F32)
        for d in range(1, N_DEV):
            acc = acc + r_ref[d].astype(F32)
        o_ref[...] = acc

    return pl.pallas_call(
        body, name=name, grid=(rows // blk,), out_shape=jax.ShapeDtypeStruct((rows, cols), F32),
        in_specs=[pl.BlockSpec((N_DEV, blk, cols), lambda i: (0, i, 0))],
        out_specs=pl.BlockSpec((blk, cols), lambda i: (i, 0)), compiler_params=_params(),
    )(r)


def _token_tile_copy(tokens_ref, buf, sems, i, tm, write=False):
    if isinstance(i, int) and i == 0:
        far, near = tokens_ref.at[pl.ds(0, tm - N_META)], buf.at[0, pl.ds(N_META, tm - N_META)]
    else:
        start = i * tm - N_META if isinstance(i, int) else pl.multiple_of(i * tm - N_META, SUBLANES)
        far, near = tokens_ref.at[pl.ds(start, tm)], buf.at[i % 2]
    return pltpu.make_async_copy(near, far, sems.at[i % 2]) if write else pltpu.make_async_copy(far, near, sems.at[i % 2])


def _fetch_token_tile(tokens_ref, buf, sems, i, n_t, tm):
    @pl.when(i == 0)
    def _():
        _token_tile_copy(tokens_ref, buf, sems, 0, tm).start()

    @pl.when(i + 1 < n_t)
    def _():
        _token_tile_copy(tokens_ref, buf, sems, i + 1, tm).start()

    @pl.when(i == 0)
    def _():
        _token_tile_copy(tokens_ref, buf, sems, 0, tm).wait()

    @pl.when(i > 0)
    def _():
        _token_tile_copy(tokens_ref, buf, sems, i, tm).wait()


def _ffn_fwd(h, g_pre, g_post, wgt, wut, wd, name, xchg=None, *, meta=None, tokens=None, target=None):
    first = h is None
    d = wd.shape[1]
    n_rows = N_META + tokens.shape[0] if first else h.shape[0]
    d_ff = wd.shape[0]
    tm, fc = _row_tile(n_rows), _ff_chunk(d_ff)
    n_t, n_c = n_rows // tm, d_ff // fc

    def body(src_ref, side_ref, gpre_ref, gpost_ref, wgt_ref, wut_ref, wd_ref, a_ref, b_ref, f_ref, o1_ref, o2_ref,
             n_scr, acc, buf, sems):
        i, c = pl.program_id(0), pl.program_id(1)
        slot = i % 2

        @pl.when(c == 0)
        def _():
            if first:
                _fetch_token_tile(side_ref, buf, sems, i, n_t, tm)

                @pl.when(i == 0)
                def _():
                    buf[0, 0:N_META, :] = src_ref[...]

                hv = buf[slot]
                o2_ref[...] = hv
            else:
                _fetch_token_tile(side_ref, buf, sems, i, n_t, tm)

                @pl.when(i == 0)
                def _():
                    buf[0, 0:N_META, :] = jnp.zeros((N_META, d), F32)
                    o2_ref[...] = jnp.zeros_like(o2_ref)

                hv = src_ref[...]
            n_scr[...] = (hv * _rs(hv) * gpre_ref[...]).astype(_MXU)
            acc[...] = jnp.zeros_like(acc)

        rows = pl.ds(pl.multiple_of(c * fc, fc), fc)
        nv = n_scr[...]
        a = _dot_nt(nv, wgt_ref[rows, :])
        b = _dot_nt(nv, wut_ref[rows, :])
        a_ref[...] = a.astype(_ACT)
        b_ref[...] = b.astype(_ACT)
        s = a * _sigmoid(a) * b
        acc[...] += _dot(s.astype(_MXU), wd_ref[rows, :])

        @pl.when(c == n_c - 1)
        def _():
            f = acc[...]
            f_ref[...] = f
            step = 0.5 * (f * _rs(f) * gpost_ref[...])
            if first:
                o1_ref[...] = buf[slot] + step
            else:
                row = i * tm + lax.broadcasted_iota(jnp.int32, (tm, 1), 0)
                err = jnp.where(row >= N_META, (src_ref[...] + step) - buf[slot], 0.0)
                o1_ref[...] = err / d
                o2_ref[...] += jnp.sum(jnp.sum(err * err, axis=0, keepdims=True), axis=1, keepdims=True)

    row = pl.BlockSpec((tm, d), lambda i, c: (i, 0))
    chunk = pl.BlockSpec((tm, fc), lambda i, c: (i, c))
    hbm = pl.BlockSpec(memory_space=pl.ANY)
    if first:
        operands, specs = (meta, tokens), [_full(meta.shape), hbm]
        last_shape, last_spec = jax.ShapeDtypeStruct((n_rows, d), F32), row
    else:
        operands, specs = (h, target), [row, hbm]
        last_shape, last_spec = jax.ShapeDtypeStruct((1, LANES), F32), pl.BlockSpec((1, LANES), lambda i, c: (0, 0))
    return _call(
        body, (*operands, g_pre, g_post, wgt, wut, wd), name=name, grid=(n_t, n_c),
        out_shape=[jax.ShapeDtypeStruct((n_rows, d_ff), _ACT), jax.ShapeDtypeStruct((n_rows, d_ff), _ACT),
                   jax.ShapeDtypeStruct((n_rows, d), F32), jax.ShapeDtypeStruct((n_rows, d), F32), last_shape],
        in_specs=specs + [_full((1, d)), _full((1, d)), _resident(), _resident(), _resident()],
        out_specs=[chunk, chunk, row, row, last_spec],
        scratch_shapes=[pltpu.VMEM((tm, d), _MXU), pltpu.VMEM((tm, d), F32), pltpu.VMEM((2, tm, d), F32),
                        pltpu.SemaphoreType.DMA((2,))], xchg=xchg)


def _ffn_bwd(dho, h, f, a, b, g_pre, g_post, wgt, wut, wd, name, xchg=None, *, split_meta=False):
    n_rows, d = h.shape
    d_ff = wd.shape[0]
    tm, fc = _row_tile(n_rows, 304), _ff_chunk(d_ff)
    n_t, n_c = n_rows // tm, d_ff // fc
    assert n_t >= 2

    def body(dho_ref, h_ref, f_ref, a_ref, b_ref, gpre_ref, gpost_ref, wgt_ref, wut_ref, wd_ref,
             dh_ref, da_ref, db_ref, s_ref, df_ref, n_ref, dgpre_ref, dgpost_ref, *rest):
        dn_acc = rest[-1]
        i, c = pl.program_id(0), pl.program_id(1)

        @pl.when((i == 0) & (c == 0))
        def _():
            dgpre_ref[...] = jnp.zeros_like(dgpre_ref)
            dgpost_ref[...] = jnp.zeros_like(dgpost_ref)

        @pl.when(c == 0)
        def _():
            fv = f_ref[...]
            rf = _rs(fv)
            fhat = fv * rf
            dy = 0.5 * dho_ref[...]
            dgpost_ref[...] += jnp.sum(dy * fhat, axis=0, keepdims=True)
            dfhat = dy * gpost_ref[...]
            df = rf * (dfhat - fhat * jnp.mean(dfhat * fhat, axis=-1, keepdims=True))
            df_ref[...] = df.astype(_MXU)
            hv = h_ref[...]
            n_ref[...] = (hv * _rs(hv) * gpre_ref[...]).astype(_MXU)
            dn_acc[...] = jnp.zeros_like(dn_acc)

        rows = pl.ds(pl.multiple_of(c * fc, fc), fc)
        ds = _dot_nt(df_ref[...], wd_ref[rows, :])
        av = a_ref[...].astype(F32)
        bv = b_ref[...].astype(F32)
        sg = _sigmoid(av)
        si = av * sg
        da = (ds * bv * (sg * (1.0 + av * (1.0 - sg)))).astype(_MXU)
        db = (ds * si).astype(_MXU)
        da_ref[...] = da
        db_ref[...] = db
        s_ref[...] = (si * bv).astype(_MXU)
        dn_acc[...] += _dot(da, wgt_ref[rows, :]) + _dot(db, wut_ref[rows, :])

        @pl.when(c == n_c - 1)
        def _():
            dn = dn_acc[...]
            hv = h_ref[...]
            r = _rs(hv)
            hhat = hv * r
            dgpre_ref[...] += jnp.sum(dn * hhat, axis=0, keepdims=True)
            dhh = dn * gpre_ref[...]
            dh = dho_ref[...] + r * (dhh - hhat * jnp.mean(dhh * hhat, axis=-1, keepdims=True))
            if not split_meta:
                dh_ref[...] = dh
                return
            dmeta_ref, buf, sems = rest[0], rest[1], rest[2]

            def out_copy(k):
                return _token_tile_copy(dh_ref, buf, sems, k, tm, write=True)

            @pl.when(i == 2)
            def _():
                out_copy(0).wait()

            @pl.when(i > 2)
            def _():
                out_copy(i - 2).wait()

            buf[i % 2] = dh

            @pl.when(i == 0)
            def _():
                dmeta_ref[...] = buf[0, 0:N_META, :]
                out_copy(0).start()

            @pl.when(i > 0)
            def _():
                out_copy(i).start()

            @pl.when(i == n_t - 1)
            def _():
                out_copy(n_t - 2).wait()
                out_copy(n_t - 1).wait()

    row = pl.BlockSpec((tm, d), lambda i, c: (i, 0))
    chunk = pl.BlockSpec((tm, fc), lambda i, c: (i, c))
    vec = pl.BlockSpec((1, d), lambda i, c: (0, 0))
    shapes = [jax.ShapeDtypeStruct((n_rows, d_ff), _MXU)] * 3 + [jax.ShapeDtypeStruct((n_rows, d), _MXU)] * 2 \
        + [jax.ShapeDtypeStruct((1, d), F32)] * 2
    specs = [chunk, chunk, chunk, row, row, vec, vec]
    scratch = [pltpu.VMEM((tm, d), F32)]
    if split_meta:
        shapes = [jax.ShapeDtypeStruct((n_rows - N_META, d), F32)] + shapes + [jax.ShapeDtypeStruct((N_META, d), F32)]
        specs = [pl.BlockSpec(memory_space=pl.ANY)] + specs + [pl.BlockSpec((N_META, d), lambda i, c: (0, 0))]
        scratch = [pltpu.VMEM((2, tm, d), F32), pltpu.SemaphoreType.DMA((2,))] + scratch
    else:
        shapes, specs = [jax.ShapeDtypeStruct((n_rows, d), F32)] + shapes, [row] + specs
    return _call(
        body, (dho, h, f, a, b, g_pre, g_post, wgt, wut, wd), name=name, grid=(n_t, n_c), out_shape=shapes,
        in_specs=[row, row, row, chunk, chunk, vec, vec, _resident(), _resident(), _resident()], out_specs=specs,
        scratch_shapes=scratch, xchg=xchg)


def _wgrad(xm, ym, name, xchg=None):
    n_rows, a_dim = xm.shape
    b_dim = ym.shape[1]
    tk = n_rows
    for cand in (2736, 1296, 432, 48, 16):
        if n_rows % cand == 0:
            tk = cand
            break
    ta = a_dim
    for cand in (1408, 1024, 512):
        if a_dim % cand == 0:
            ta = cand
            break

    n_k = n_rows // tk

    def body(x_ref, y_ref, o_ref, acc):
        k = pl.program_id(1)

        @pl.when(k == 0)
        def _():
            acc[...] = jnp.zeros_like(acc)

        acc[...] += _dot_tn(x_ref[...], y_ref[...])

        @pl.when(k == n_k - 1)
        def _():
            o_ref[...] = acc[...].astype(o_ref.dtype)

    (out,), extra = _call(
        body, (xm, ym), name=name, grid=(a_dim // ta, n_k), out_shape=[jax.ShapeDtypeStruct((a_dim, b_dim), _WIRE)],
        in_specs=[pl.BlockSpec((tk, ta), lambda j, k: (k, j)), pl.BlockSpec((tk, b_dim), lambda j, k: (k, 0))],
        out_specs=[pl.BlockSpec((ta, b_dim), lambda j, k: (j, 0))], scratch_shapes=[pltpu.VMEM((ta, b_dim), F32)],
        xchg=xchg)
    return out, extra


def _mix_in_fwd(h, g, w_in, name):
    n_rows, d = h.shape
    tm = _row_tile(n_rows)
    width = w_in.shape[1] // 2
    n_blk = width // LANES

    def body(h_ref, g_ref, w_ref, n_ref, us_ref, up_ref):
        hv = h_ref[...]
        nv = (hv * _rs(hv) * g_ref[...]).astype(_MXU)
        n_ref[...] = nv
        p = _dot(nv, w_ref[...])
        for k in range(n_blk):
            us_ref[k] = p[:, k * LANES:(k + 1) * LANES]
        up_ref[...] = p[:, width:]

    row = pl.BlockSpec((tm, d), lambda i: (i, 0))
    half = pl.BlockSpec((tm, width), lambda i: (i, 0))
    return pl.pallas_call(
        body, name=name, grid=(n_rows // tm,),
        out_shape=[jax.ShapeDtypeStruct((n_rows, d), _MXU), jax.ShapeDtypeStruct((n_blk, n_rows, LANES), F32),
                   jax.ShapeDtypeStruct((n_rows, width), F32)],
        in_specs=[row, _full((1, d)), _resident()], out_specs=[row, _lane_blocks(n_blk, tm), half],
        compiler_params=_params(),
    )(h, g, w_in)


def _gelu_parts(y):
    th = jnp.tanh(GELU_C0 * (y + GELU_C1 * (y * y * y)))
    return 0.5 * (1.0 + th), th


def _scan_blocks(n_blocks, width, coef_ref, load, store, reverse):
    n_cols = coef_ref.shape[2] // width
    for cb in range(n_cols):
        cols = pl.ds(cb * width, width)

        def step(t, carry, cols=cols):
            tb = (n_blocks - 1 - t) if reverse else t
            r0 = pl.multiple_of(tb * SUBLANES, SUBLANES)
            vr, vi = load(r0, cols)
            xr, xi = vr, vi
            for lvl, k in enumerate((1, 2, 4)):
                kr, ki = coef_ref[2 * lvl, :, cols], coef_ref[2 * lvl + 1, :, cols]
                shift = SUBLANES - k if reverse else k
                sr, si = pltpu.roll(xr, shift, 0), pltpu.roll(xi, shift, 0)
                xr, xi = xr + (kr * sr - ki * si), xi + (kr * si + ki * sr)
            pr, pi = coef_ref[6, :, cols], coef_ref[7, :, cols]
            cr, ci = carry[0], carry[1]
            xr, xi = xr + (pr * cr - pi * ci), xi + (pr * ci + pi * cr)
            extra = store(r0, cols, xr, xi, vr, vi, carry[2:])
            edge = 0 if reverse else SUBLANES - 1
            return (xr[edge:edge + 1, :], xi[edge:edge + 1, :]) + tuple(extra)

        yield cb, cols, step


def _ssm_fwd(proj, coef, b_re, b_im, c_re, c_im, wz1, wz2, d_skip, g_out, name, xchg=None):
    n_rows = proj.shape[0]
    n_blk, _, n_state = b_re.shape
    width = n_blk * LANES
    tm = _row_tile(n_rows)
    n_tb = tm // SUBLANES

    def body(u_ref, coef_ref, bre_ref, bim_ref, cre_ref, cim_ref, wz1_ref, wz2_ref, d_ref, g_ref,
             xre_ref, xim_ref, o_ref, car_re, car_im, out_scr):
        @pl.when(pl.program_id(0) == 0)
        def _():
            car_re[...] = jnp.zeros_like(car_re)
            car_im[...] = jnp.zeros_like(car_im)

        ub = u_ref[...].astype(_MXU)
        for j in range(n_blk):
            uj = ub[:, j * LANES:(j + 1) * LANES]
            xre_ref[:, j * n_state:(j + 1) * n_state] = _dot(uj, bre_ref[j])
            xim_ref[:, j * n_state:(j + 1) * n_state] = _dot(uj, bim_ref[j])

        def load(r0, cols):
            return xre_ref[pl.ds(r0, SUBLANES), cols], xim_ref[pl.ds(r0, SUBLANES), cols]

        def store(r0, cols, xr, xi, vr, vi, extra):
            xre_ref[pl.ds(r0, SUBLANES), cols] = xr
            xim_ref[pl.ds(r0, SUBLANES), cols] = xi
            return ()

        for cb, cols, step in _scan_blocks(n_tb, 512, coef_ref, load, store, reverse=False):
            fin = lax.fori_loop(0, n_tb, step, (car_re[:, cols], car_im[:, cols]))
            car_re[:, cols] = fin[0]
            car_im[:, cols] = fin[1]

        ssq = None
        for j in range(n_blk):
            sl = slice(j * LANES, (j + 1) * LANES)
            st = slice(j * n_state, (j + 1) * n_state)
            yc = _dot(xre_ref[:, st].astype(_MXU), cre_ref[j]) - _dot(xim_ref[:, st].astype(_MXU), cim_ref[j])
            y = yc + d_ref[:, sl] * u_ref[:, sl]
            cdf, _ = _gelu_parts(y)
            gy = (y * cdf).astype(_MXU)
            out = _dot(gy, wz1_ref[j]) * _sigmoid(_dot(gy, wz2_ref[j]))
            out_scr[:, sl] = out
            part = jnp.sum(out * out, axis=-1, keepdims=True)
            ssq = part if ssq is None else ssq + part
        r = lax.rsqrt(ssq / width + RMS_EPS)
        o_ref[...] = (out_scr[...] * r * g_ref[...]).astype(_MXU)

    half = pl.BlockSpec((tm, width), lambda i: (i, 0))
    state = pl.BlockSpec((tm, n_blk * n_state), lambda i: (i, 0))
    return _call(
        body, (proj, coef, b_re, b_im, c_re, c_im, wz1, wz2, d_skip, g_out), name=name, grid=(n_rows // tm,),
        out_shape=[jax.ShapeDtypeStruct((n_rows, n_blk * n_state), F32)] * 2 + [jax.ShapeDtypeStruct((n_rows, width), _MXU)],
        in_specs=[half, _full(coef.shape), _full(b_re.shape), _full(b_im.shape), _full(c_re.shape), _full(c_im.shape),
                  _full(wz1.shape), _full(wz2.shape), _full((1, width)), _full((1, width))],
        out_specs=[state, state, half],
        scratch_shapes=[pltpu.VMEM((1, n_blk * n_state), F32)] * 2 + [pltpu.VMEM((tm, width), F32)], xchg=xchg)


def _ssm_bwd(proj, x_re, x_im, dcat, coef_rev, b_re, b_im, c_re, c_im, wz1, wz2, d_skip, g_out, name, xchg=None):
    n_rows = proj.shape[0]
    n_blk, _, n_state = b_re.shape
    width = n_blk * LANES
    n_all = n_blk * n_state
    tm = _row_tile(n_rows)
    n_t, n_tb = n_rows // tm, tm // SUBLANES

    def body(u_ref, xre_ref, xim_ref, dc_ref, coef_ref, bre_ref, bim_ref, cre_ref, cim_ref, wz1_ref, wz2_ref, d_ref, g_ref,
             du_ref, dg_ref, dd_ref, dwz1_ref, dwz2_ref, dcre_ref, dcim_ref, dbre_ref, dbim_ref, are_ref, aim_ref,
             car_re, car_im, gre, gim, y_s, z1_s, sg_s, out_s, gy_s):
        @pl.when(pl.program_id(0) == 0)
        def _():
            for ref in (dg_ref, dd_ref, dwz1_ref, dwz2_ref, dcre_ref, dcim_ref, dbre_ref, dbim_ref, are_ref, aim_ref,
                        car_re, car_im):
                ref[...] = jnp.zeros_like(ref)

        ssq = None
        for j in range(n_blk):
            sl = slice(j * LANES, (j + 1) * LANES)
            st = slice(j * n_state, (j + 1) * n_state)
            yc = _dot(xre_ref[:, st].astype(_MXU), cre_ref[j]) - _dot(xim_ref[:, st].astype(_MXU), cim_ref[j])
            y = yc + d_ref[:, sl] * u_ref[:, sl]
            cdf, _ = _gelu_parts(y)
            gy = (y * cdf).astype(_MXU)
            z1 = _dot(gy, wz1_ref[j])
            sg = _sigmoid(_dot(gy, wz2_ref[j]))
            out = z1 * sg
            y_s[:, sl], z1_s[:, sl], sg_s[:, sl], out_s[:, sl], gy_s[:, sl] = y, z1, sg, out, gy
            part = jnp.sum(out * out, axis=-1, keepdims=True)
            ssq = part if ssq is None else ssq + part
        r = lax.rsqrt(ssq / width + RMS_EPS)
        ohat = out_s[...] * r
        dcv = dc_ref[...]
        dg_ref[...] += jnp.sum(dcv * ohat, axis=0, keepdims=True)
        doh = dcv * g_ref[...]
        out_s[...] = r * (doh - ohat * (jnp.sum(doh * ohat, axis=-1, keepdims=True) / width))

        for j in range(n_blk):
            sl = slice(j * LANES, (j + 1) * LANES)
            st = slice(j * n_state, (j + 1) * n_state)
            dout, sg, z1, y = out_s[:, sl], sg_s[:, sl], z1_s[:, sl], y_s[:, sl]
            dz1 = (dout * sg).astype(_MXU)
            dz2 = (dout * z1 * sg * (1.0 - sg)).astype(_MXU)
            gy = gy_s[:, sl]
            dwz1_ref[j] += _dot_tn(gy, dz1)
            dwz2_ref[j] += _dot_tn(gy, dz2)
            dgy = _dot_nt(dz1, wz1_ref[j]) + _dot_nt(dz2, wz2_ref[j])
            cdf, th = _gelu_parts(y)
            dy = dgy * (cdf + y * (0.5 * (1.0 - th * th) * GELU_C0 * (1.0 + 3.0 * GELU_C1 * (y * y))))
            uj = u_ref[:, sl]
            dd_ref[:, sl] += jnp.sum(dy * uj, axis=0, keepdims=True)
            du_ref[:, sl] = d_ref[:, sl] * dy
            dyb = dy.astype(_MXU)
            dcre_ref[j] += _dot_tn(dyb, xre_ref[:, st].astype(_MXU))
            dcim_ref[j] -= _dot_tn(dyb, xim_ref[:, st].astype(_MXU))
            gre[:, st] = _dot_nt(dyb, cre_ref[j])
            gim[:, st] = -_dot_nt(dyb, cim_ref[j])

        def load(r0, cols):
            return gre[pl.ds(r0, SUBLANES), cols], gim[pl.ds(r0, SUBLANES), cols]

        def store(r0, cols, gr, gi, vr, vi, extra):
            gre[pl.ds(r0, SUBLANES), cols] = gr
            gim[pl.ds(r0, SUBLANES), cols] = gi
            hr, hi = gr - vr, gi - vi
            xr, xi = xre_ref[pl.ds(r0, SUBLANES), cols], xim_ref[pl.ds(r0, SUBLANES), cols]
            return extra[0] + (xr * hr + xi * hi), extra[1] + (xr * hi - xi * hr)

        for cb, cols, step in _scan_blocks(n_tb, 512, coef_ref, load, store, reverse=True):
            zero = jnp.zeros((SUBLANES, 512), F32)
            fin = lax.fori_loop(0, n_tb, step, (car_re[:, cols], car_im[:, cols], zero, zero))
            car_re[:, cols] = fin[0]
            car_im[:, cols] = fin[1]
            are_ref[:, cols] += jnp.sum(fin[2], axis=0, keepdims=True)
            aim_ref[:, cols] += jnp.sum(fin[3], axis=0, keepdims=True)

        for j in range(n_blk):
            sl = slice(j * LANES, (j + 1) * LANES)
            st = slice(j * n_state, (j + 1) * n_state)
            ujb = u_ref[:, sl].astype(_MXU)
            grb, gib = gre[:, st].astype(_MXU), gim[:, st].astype(_MXU)
            dbre_ref[j] += _dot_tn(ujb, grb)
            dbim_ref[j] += _dot_tn(ujb, gib)
            du_ref[:, sl] += _dot_nt(grb, bre_ref[j]) + _dot_nt(gib, bim_ref[j])

    half = pl.BlockSpec((tm, width), lambda i: (n_t - 1 - i, 0))
    state = pl.BlockSpec((tm, n_all), lambda i: (n_t - 1 - i, 0))
    small = [(1, width), (1, width), wz1.shape, wz2.shape, (n_blk, LANES, n_state), (n_blk, LANES, n_state),
             (n_blk, LANES, n_state), (n_blk, LANES, n_state), (1, n_all), (1, n_all)]
    return _call(
        body, (proj, x_re, x_im, dcat, coef_rev, b_re, b_im, c_re, c_im, wz1, wz2, d_skip, g_out), name=name, grid=(n_t,),
        out_shape=[jax.ShapeDtypeStruct((n_rows, width), F32)] + [jax.ShapeDtypeStruct(s, F32) for s in small],
        in_specs=[half, state, state, half, _full(coef_rev.shape), _full(b_re.shape), _full(b_im.shape), _full(c_re.shape),
                  _full(c_im.shape), _full(wz1.shape), _full(wz2.shape), _full((1, width)), _full((1, width))],
        out_specs=[half] + [_full(s) for s in small],
        scratch_shapes=[pltpu.VMEM((1, n_all), F32)] * 2 + [pltpu.VMEM((tm, n_all), F32)] * 2
        + [pltpu.VMEM((tm, width), F32)] * 4 + [pltpu.VMEM((tm, width), _MXU)], xchg=xchg)


SCAN_LANES = 512


def _regroup(src_ref, dst_ref, seg):
    for k in range(src_ref.shape[0]):
        for j in range(seg):
            dst_ref[j * SUBLANES:(j + 1) * SUBLANES, k * LANES:(k + 1) * LANES] = src_ref[k, pl.ds(j, SUBLANES, stride=seg), :]


def _ungroup(src_ref, dst_ref, seg):
    for k in range(dst_ref.shape[0]):
        for j in range(seg):
            dst_ref[k, pl.ds(j, SUBLANES, stride=seg), :] = src_ref[j * SUBLANES:(j + 1) * SUBLANES, k * LANES:(k + 1) * LANES]


def _rows_to_sublanes(rows):
    rid = lax.broadcasted_iota(jnp.int32, (SUBLANES, rows[0].shape[1]), 0)
    out = jnp.broadcast_to(rows[0], rid.shape)
    for s in range(1, SUBLANES):
        out = jnp.where(rid == s, rows[s], out)
    return out


def _segment_scan(re_ref, im_ref, tab_ref, car_re, car_im, seg, reverse, x_refs=None):
    n_all = re_ref.shape[1]
    first = seg - 1 if reverse else 0
    sums = []
    for cb in range(n_all // SCAN_LANES):
        cols = pl.ds(cb * SCAN_LANES, SCAN_LANES)
        ar, ai = tab_ref[0, first, :, cols], tab_ref[1, first, :, cols]

        def local(t, carry, cols=cols, ar=ar, ai=ai):
            r0 = pl.multiple_of((seg - 1 - t if reverse else t) * SUBLANES, SUBLANES)
            xr, xi = carry
            nr = ar * xr - ai * xi + re_ref[pl.ds(r0, SUBLANES), cols]
            ni = ar * xi + ai * xr + im_ref[pl.ds(r0, SUBLANES), cols]
            re_ref[pl.ds(r0, SUBLANES), cols] = nr
            im_ref[pl.ds(r0, SUBLANES), cols] = ni
            return nr, ni

        zero = jnp.zeros((SUBLANES, SCAN_LANES), F32)
        fr, fi = lax.fori_loop(0, seg, local, (zero, zero))

        last = 0 if reverse else seg - 1
        sr, si = tab_ref[0, last, 0:1, cols], tab_ref[1, last, 0:1, cols]
        cr, ci = car_re[:, cols], car_im[:, cols]
        rows_r, rows_i = [None] * SUBLANES, [None] * SUBLANES
        for s in (range(SUBLANES - 1, -1, -1) if reverse else range(SUBLANES)):
            rows_r[s], rows_i[s] = cr, ci
            cr, ci = sr * cr - si * ci + fr[s:s + 1], sr * ci + si * cr + fi[s:s + 1]
        car_re[:, cols] = cr
        car_im[:, cols] = ci
        cmr, cmi = _rows_to_sublanes(rows_r), _rows_to_sublanes(rows_i)

        def fix(t, carry, cols=cols, cmr=cmr, cmi=cmi):
            j = seg - 1 - t if reverse else t
            r0 = pl.multiple_of(j * SUBLANES, SUBLANES)
            pr, pi = tab_ref[0, j, :, cols], tab_ref[1, j, :, cols]
            gr = re_ref[pl.ds(r0, SUBLANES), cols] + (pr * cmr - pi * cmi)
            gi = im_ref[pl.ds(r0, SUBLANES), cols] + (pr * cmi + pi * cmr)
            re_ref[pl.ds(r0, SUBLANES), cols] = gr
            im_ref[pl.ds(r0, SUBLANES), cols] = gi
            if x_refs is None:
                return carry
            nxr, nxi, accr, acci = carry
            xr, xi = x_refs[0][pl.ds(r0, SUBLANES), cols], x_refs[1][pl.ds(r0, SUBLANES), cols]
            return gr, gi, accr + (xr * nxr + xi * nxi), acci + (xr * nxi - xi * nxr)

        if x_refs is None:
            lax.fori_loop(0, seg, fix, 0)
        else:
            fin = lax.fori_loop(0, seg, fix, (cmr, cmi, zero, zero))
            sums.append((jnp.sum(fin[2], axis=0, keepdims=True), jnp.sum(fin[3], axis=0, keepdims=True)))
    return sums


def _ssm_forward(u, table, b_re, b_im, c_re, c_im, wz1, wz2, d_skip, g_out, name, xchg=None):
    n_rows = u.shape[1]
    n_blk, _, n_state = b_re.shape
    width, n_all = n_blk * LANES, n_blk * n_state
    tm = _row_tile(n_rows)
    seg = tm // SUBLANES

    def body(u_ref, tab_ref, bre_ref, bim_ref, cre_ref, cim_ref, wz1_ref, wz2_ref, d_ref, g_ref,
             xre_ref, xim_ref, o_ref, car_re, car_im, ug, out_scr, blocks):
        @pl.when(pl.program_id(0) == 0)
        def _():
            car_re[...] = jnp.zeros_like(car_re)
            car_im[...] = jnp.zeros_like(car_im)

        _regroup(u_ref, ug, seg)
        ub = ug[...].astype(_MXU)
        for j in range(n_blk):
            uj = ub[:, j * LANES:(j + 1) * LANES]
            xre_ref[:, j * n_state:(j + 1) * n_state] = _dot(uj, bre_ref[j])
            xim_ref[:, j * n_state:(j + 1) * n_state] = _dot(uj, bim_ref[j])
        _segment_scan(xre_ref, xim_ref, tab_ref, car_re, car_im, seg, reverse=False)

        ssq = None
        for j in range(n_blk):
            sl = slice(j * LANES, (j + 1) * LANES)
            st = slice(j * n_state, (j + 1) * n_state)
            yc = _dot(xre_ref[:, st].astype(_MXU), cre_ref[j]) - _dot(xim_ref[:, st].astype(_MXU), cim_ref[j])
            y = yc + d_ref[:, sl] * ug[:, sl]
            cdf, _ = _gelu_parts(y)
            gy = (y * cdf).astype(_MXU)
            out = _dot(gy, wz1_ref[j]) * _sigmoid(_dot(gy, wz2_ref[j]))
            out_scr[:, sl] = out
            part = jnp.sum(out * out, axis=-1, keepdims=True)
            ssq = part if ssq is None else ssq + part
        r = lax.rsqrt(ssq / width + RMS_EPS)
        out_scr[...] = out_scr[...] * r * g_ref[...]
        _ungroup(out_scr, blocks, seg)
        for k in range(n_blk):
            o_ref[:, k * LANES:(k + 1) * LANES] = blocks[k].astype(_MXU)

    half = pl.BlockSpec((tm, width), lambda i: (i, 0))
    state = pl.BlockSpec((tm, n_all), lambda i: (i, 0))
    return _call(
        body, (u, table, b_re, b_im, c_re, c_im, wz1, wz2, d_skip, g_out), name=name, grid=(n_rows // tm,),
        out_shape=[jax.ShapeDtypeStruct((n_rows, n_all), F32)] * 2 + [jax.ShapeDtypeStruct((n_rows, width), _MXU)],
        in_specs=[_lane_blocks(n_blk, tm), _resident(), _full(b_re.shape), _full(b_im.shape), _full(c_re.shape),
                  _full(c_im.shape), _full(wz1.shape), _full(wz2.shape), _full((1, width)), _full((1, width))],
        out_specs=[state, state, half],
        scratch_shapes=[pltpu.VMEM((1, n_all), F32)] * 2 + [pltpu.VMEM((tm, width), F32)] * 2
        + [pltpu.VMEM((n_blk, tm, LANES), F32)], xchg=xchg)


def _ssm_backward(u, x_re, x_im, dcat, table, b_re, b_im, c_re, c_im, wz1, wz2, d_skip, g_out, name, xchg=None):
    n_rows = u.shape[1]
    n_blk, _, n_state = b_re.shape
    width, n_all = n_blk * LANES, n_blk * n_state
    tm = _row_tile(n_rows)
    n_t, seg = n_rows // tm, tm // SUBLANES

    def body(u_ref, xre_ref, xim_ref, dc_ref, tab_ref, bre_ref, bim_ref, cre_ref, cim_ref, wz1_ref, wz2_ref, d_ref, g_ref,
             du_ref, dg_ref, dd_ref, dwz1_ref, dwz2_ref, dcre_ref, dcim_ref, dbre_ref, dbim_ref, are_ref, aim_ref,
             car_re, car_im, gre, gim, ug, y_s, z1_s, sg_s, out_s, gy_s):
        @pl.when(pl.program_id(0) == 0)
        def _():
            for ref in (dg_ref, dd_ref, dwz1_ref, dwz2_ref, dcre_ref, dcim_ref, dbre_ref, dbim_ref, are_ref, aim_ref,
                        car_re, car_im):
                ref[...] = jnp.zeros_like(ref)

        _regroup(u_ref, ug, seg)
        ssq = None
        for j in range(n_blk):
            sl = slice(j * LANES, (j + 1) * LANES)
            st = slice(j * n_state, (j + 1) * n_state)
            yc = _dot(xre_ref[:, st].astype(_MXU), cre_ref[j]) - _dot(xim_ref[:, st].astype(_MXU), cim_ref[j])
            y = yc + d_ref[:, sl] * ug[:, sl]
            cdf, _ = _gelu_parts(y)
            gy = (y * cdf).astype(_MXU)
            z1 = _dot(gy, wz1_ref[j])
            sg = _sigmoid(_dot(gy, wz2_ref[j]))
            out = z1 * sg
            y_s[:, sl], z1_s[:, sl], sg_s[:, sl], out_s[:, sl], gy_s[:, sl] = y, z1, sg, out, gy
            part = jnp.sum(out * out, axis=-1, keepdims=True)
            ssq = part if ssq is None else ssq + part
        r = lax.rsqrt(ssq / width + RMS_EPS)
        ohat = out_s[...] * r
        _regroup(dc_ref, out_s, seg)
        dcv = out_s[...]
        dg_ref[...] += jnp.sum(dcv * ohat, axis=0, keepdims=True)
        doh = dcv * g_ref[...]
        out_s[...] = r * (doh - ohat * (jnp.sum(doh * ohat, axis=-1, keepdims=True) / width))

        for j in range(n_blk):
            sl = slice(j * LANES, (j + 1) * LANES)
            st = slice(j * n_state, (j + 1) * n_state)
            dout, sg, z1, y = out_s[:, sl], sg_s[:, sl], z1_s[:, sl], y_s[:, sl]
            dz1 = (dout * sg).astype(_MXU)
            dz2 = (dout * z1 * sg * (1.0 - sg)).astype(_MXU)
            gy = gy_s[:, sl]
            dwz1_ref[j] += _dot_tn(gy, dz1)
            dwz2_ref[j] += _dot_tn(gy, dz2)
            dgy = _dot_nt(dz1, wz1_ref[j]) + _dot_nt(dz2, wz2_ref[j])
            cdf, th = _gelu_parts(y)
            dy = dgy * (cdf + y * (0.5 * (1.0 - th * th) * GELU_C0 * (1.0 + 3.0 * GELU_C1 * (y * y))))
            uj = ug[:, sl]
            dd_ref[:, sl] += jnp.sum(dy * uj, axis=0, keepdims=True)
            z1_s[:, sl] = d_ref[:, sl] * dy
            dyb = dy.astype(_MXU)
            dcre_ref[j] += _dot_tn(dyb, xre_ref[:, st].astype(_MXU))
            dcim_ref[j] -= _dot_tn(dyb, xim_ref[:, st].astype(_MXU))
            gre[:, st] = _dot_nt(dyb, cre_ref[j])
            gim[:, st] = -_dot_nt(dyb, cim_ref[j])

        sums = _segment_scan(gre, gim, tab_ref, car_re, car_im, seg, reverse=True, x_refs=(xre_ref, xim_ref))
        for cb, (sum_re, sum_im) in enumerate(sums):
            cols = pl.ds(cb * SCAN_LANES, SCAN_LANES)
            are_ref[:, cols] += sum_re
            aim_ref[:, cols] += sum_im

        for j in range(n_blk):
            sl = slice(j * LANES, (j + 1) * LANES)
            st = slice(j * n_state, (j + 1) * n_state)
            ujb = ug[:, sl].astype(_MXU)
            grb, gib = gre[:, st].astype(_MXU), gim[:, st].astype(_MXU)
            dbre_ref[j] += _dot_tn(ujb, grb)
            dbim_ref[j] += _dot_tn(ujb, gib)
            z1_s[:, sl] += _dot_nt(grb, bre_ref[j]) + _dot_nt(gib, bim_ref[j])
        _ungroup(z1_s, du_ref, seg)

    half = _lane_blocks(n_blk, tm, lambda i: n_t - 1 - i)
    state = pl.BlockSpec((tm, n_all), lambda i: (n_t - 1 - i, 0))
    small = [(1, width), (1, width), wz1.shape, wz2.shape, (n_blk, LANES, n_state), (n_blk, LANES, n_state),
             (n_blk, LANES, n_state), (n_blk, LANES, n_state), (1, n_all), (1, n_all)]
    return _call(
        body, (u, x_re, x_im, dcat, table, b_re, b_im, c_re, c_im, wz1, wz2, d_skip, g_out), name=name, grid=(n_t,),
        out_shape=[jax.ShapeDtypeStruct((n_blk, n_rows, LANES), F32)] + [jax.ShapeDtypeStruct(s, F32) for s in small],
        in_specs=[half, state, state, half, _resident(), _full(b_re.shape), _full(b_im.shape), _full(c_re.shape),
                  _full(c_im.shape), _full(wz1.shape), _full(wz2.shape), _full((1, width)), _full((1, width))],
        out_specs=[half] + [_full(s) for s in small],
        scratch_shapes=[pltpu.VMEM((1, n_all), F32)] * 2 + [pltpu.VMEM((tm, n_all), F32)] * 2
        + [pltpu.VMEM((tm, width), F32)] * 5 + [pltpu.VMEM((tm, width), _MXU)], xchg=xchg)


def _pool_counts(tile, tm, window):
    t = tile * tm + lax.broadcasted_iota(jnp.int32, (tm, 1), 0)
    return jnp.minimum(t + 1, window).astype(F32)


def _pool_fwd(proj, pool_w, scale, g_out, name):
    n_rows = proj.shape[0]
    n_grp, grp, _ = pool_w.shape
    width = n_grp * grp
    tm = _row_tile(n_rows)

    def body(u_ref, pw_ref, sc_ref, g_ref, o_ref, ext, y_s):
        i = pl.program_id(0)

        @pl.when(i == 0)
        def _():
            ext[0:POOL_HALO, :] = jnp.zeros((POOL_HALO, width), F32)

        ext[POOL_HALO:, :] = u_ref[...]
        ssq = None
        for gi, w in enumerate(POOL_WINDOWS):
            sl = slice(gi * grp, (gi + 1) * grp)
            tot = ext[POOL_HALO:, sl]
            for k in range(1, w):
                tot = tot + ext[POOL_HALO - k:POOL_HALO - k + tm, sl]
            pooled = tot / _pool_counts(i, tm, w) - u_ref[:, sl]
            y = _dot(pooled.astype(_MXU), pw_ref[gi]) * sc_ref[:, sl]
            y_s[:, sl] = y
            part = jnp.sum(y * y, axis=-1, keepdims=True)
            ssq = part if ssq is None else ssq + part
        r = lax.rsqrt(ssq / width + RMS_EPS)
        o_ref[...] = (y_s[...] * r * g_ref[...]).astype(_MXU)
        ext[0:POOL_HALO, :] = u_ref[tm - POOL_HALO:, :]

    half_in = pl.BlockSpec((tm, width), lambda i: (i, 0))
    half = pl.BlockSpec((tm, width), lambda i: (i, 0))
    return pl.pallas_call(
        body, name=name, grid=(n_rows // tm,), out_shape=jax.ShapeDtypeStruct((n_rows, width), _MXU),
        in_specs=[half_in, _full(pool_w.shape), _full((1, width)), _full((1, width))], out_specs=half,
        scratch_shapes=[pltpu.VMEM((tm + POOL_HALO, width), F32), pltpu.VMEM((tm, width), F32)],
        compiler_params=_params(),
    )(proj, pool_w, scale, g_out)


def _pool_bwd(proj, dcat, pool_w, scale, g_out, name):
    n_rows = proj.shape[0]
    n_grp, grp, _ = pool_w.shape
    width = n_grp * grp
    tm = _row_tile(n_rows)
    n_t = n_rows // tm
    halo_blocks = tm // POOL_HALO

    def body(u_ref, up_ref, dc_ref, pw_ref, sc_ref, g_ref, du_ref, dg_ref, dsc_ref, dpw_ref, ext, qext, y_s, pl_s):
        i = pl.program_id(0)
        tile = n_t - 1 - i

        @pl.when(i == 0)
        def _():
            for ref in (dg_ref, dsc_ref, dpw_ref):
                ref[...] = jnp.zeros_like(ref)
            qext[tm:, :] = jnp.zeros((POOL_HALO, width), F32)

        ext[0:POOL_HALO, :] = jnp.where(tile > 0, up_ref[...], 0.0)
        ext[POOL_HALO:, :] = u_ref[...]
        ssq = None
        for gi, w in enumerate(POOL_WINDOWS):
            sl = slice(gi * grp, (gi + 1) * grp)
            tot = ext[POOL_HALO:, sl]
            for k in range(1, w):
                tot = tot + ext[POOL_HALO - k:POOL_HALO - k + tm, sl]
            pooled = (tot / _pool_counts(tile, tm, w) - u_ref[:, sl]).astype(_MXU)
            pl_s[:, sl] = pooled
            y0 = _dot(pooled, pw_ref[gi])
            y_s[:, sl] = y0
            y = y0 * sc_ref[:, sl]
            part = jnp.sum(y * y, axis=-1, keepdims=True)
            ssq = part if ssq is None else ssq + part
        r = lax.rsqrt(ssq / width + RMS_EPS)
        y0 = y_s[...]
        yhat = y0 * sc_ref[...] * r
        dcv = dc_ref[...]
        dg_ref[...] += jnp.sum(dcv * yhat, axis=0, keepdims=True)
        dyh = dcv * g_ref[...]
        dy = r * (dyh - yhat * (jnp.sum(dyh * yhat, axis=-1, keepdims=True) / width))
        dsc_ref[...] += jnp.sum(dy * y0, axis=0, keepdims=True)
        y_s[...] = dy * sc_ref[...]
        for gi, w in enumerate(POOL_WINDOWS):
            sl = slice(gi * grp, (gi + 1) * grp)
            dm = y_s[:, sl].astype(_MXU)
            dpw_ref[gi] += _dot_tn(pl_s[:, sl], dm)
            dpooled = _dot_nt(dm, pw_ref[gi])
            y_s[:, sl] = dpooled
            qext[0:tm, sl] = dpooled / _pool_counts(tile, tm, w)
        for gi, w in enumerate(POOL_WINDOWS):
            sl = slice(gi * grp, (gi + 1) * grp)
            tot = qext[0:tm, sl]
            for k in range(1, w):
                tot = tot + qext[k:k + tm, sl]
            du_ref[:, sl] = tot - y_s[:, sl]
        qext[tm:, :] = qext[0:POOL_HALO, :]

    half_in = pl.BlockSpec((tm, width), lambda i: (n_t - 1 - i, 0))
    prev = pl.BlockSpec((POOL_HALO, width), lambda i: (jnp.maximum((n_t - 1 - i) * halo_blocks - 1, 0), 0))
    half = pl.BlockSpec((tm, width), lambda i: (n_t - 1 - i, 0))
    return pl.pallas_call(
        body, name=name, grid=(n_t,),
        out_shape=[jax.ShapeDtypeStruct((n_rows, width), F32), jax.ShapeDtypeStruct((1, width), F32),
                   jax.ShapeDtypeStruct((1, width), F32), jax.ShapeDtypeStruct(pool_w.shape, F32)],
        in_specs=[half_in, prev, half, _full(pool_w.shape), _full((1, width)), _full((1, width))],
        out_specs=[half, _full((1, width)), _full((1, width)), _full(pool_w.shape)],
        scratch_shapes=[pltpu.VMEM((tm + POOL_HALO, width), F32), pltpu.VMEM((tm + POOL_HALO, width), F32),
                        pltpu.VMEM((tm, width), F32), pltpu.VMEM((tm, width), _MXU)],
        compiler_params=_params(),
    )(proj, proj, dcat, pool_w, scale, g_out)


def _mix_out_fwd(cat_s, cat_p, h, g, wo_s, wo_p, name):
    n_rows, d = h.shape
    width = cat_s.shape[1]
    tm = _row_tile(n_rows)

    def body(cs_ref, cp_ref, h_ref, g_ref, ws_ref, wp_ref, m_ref, ho_ref):
        m = _dot(cs_ref[...], ws_ref[...]) + _dot(cp_ref[...], wp_ref[...])
        m_ref[...] = m
        ho_ref[...] = h_ref[...] + m * _rs(m) * g_ref[...]

    row = pl.BlockSpec((tm, d), lambda i: (i, 0))
    half = pl.BlockSpec((tm, width), lambda i: (i, 0))
    return pl.pallas_call(
        body, name=name, grid=(n_rows // tm,), out_shape=[jax.ShapeDtypeStruct((n_rows, d), F32)] * 2,
        in_specs=[half, half, row, _full((1, d)), _resident(), _resident()], out_specs=[row, row],
        compiler_params=_params(),
    )(cat_s, cat_p, h, g, wo_s, wo_p)


def _mix_out_bwd(dho, mixed, g, wo_s, wo_p, name):
    n_rows, d = mixed.shape
    width = wo_s.shape[0]
    n_blk = width // LANES
    tm = _row_tile(n_rows)

    def body(dho_ref, m_ref, g_ref, ws_ref, wp_ref, dm_ref, dcs_ref, dcp_ref, dg_ref):
        @pl.when(pl.program_id(0) == 0)
        def _():
            dg_ref[...] = jnp.zeros_like(dg_ref)

        m = m_ref[...]
        r = _rs(m)
        mh = m * r
        dy = dho_ref[...]
        dg_ref[...] += jnp.sum(dy * mh, axis=0, keepdims=True)
        dmh = dy * g_ref[...]
        dm = (r * (dmh - mh * jnp.mean(dmh * mh, axis=-1, keepdims=True))).astype(_MXU)
        dm_ref[...] = dm
        dcs = _dot_nt(dm, ws_ref[...])
        for k in range(n_blk):
            dcs_ref[k] = dcs[:, k * LANES:(k + 1) * LANES]
        dcp_ref[...] = _dot_nt(dm, wp_ref[...])

    row = pl.BlockSpec((tm, d), lambda i: (i, 0))
    half = pl.BlockSpec((tm, width), lambda i: (i, 0))
    return pl.pallas_call(
        body, name=name, grid=(n_rows // tm,),
        out_shape=[jax.ShapeDtypeStruct((n_rows, d), _MXU), jax.ShapeDtypeStruct((n_blk, n_rows, LANES), F32),
                   jax.ShapeDtypeStruct((n_rows, width), F32), jax.ShapeDtypeStruct((1, d), F32)],
        in_specs=[row, row, _full((1, d)), _resident(), _resident()],
        out_specs=[row, _lane_blocks(n_blk, tm), half, _full((1, d))], compiler_params=_params(),
    )(dho, mixed, g, wo_s, wo_p)


def _mix_in_bwd(du_s, du_p, h, dho, g, wi_s, wi_p, name):
    n_rows, d = h.shape
    width = du_p.shape[1]
    n_blk = width // LANES
    tm = _row_tile(n_rows)

    def body(dus_ref, dup_ref, h_ref, dho_ref, g_ref, ws_ref, wp_ref, dh_ref, dp_ref, dg_ref):
        @pl.when(pl.program_id(0) == 0)
        def _():
            dg_ref[...] = jnp.zeros_like(dg_ref)

        for k in range(n_blk):
            dp_ref[:, k * LANES:(k + 1) * LANES] = dus_ref[k].astype(_MXU)
        dup = dup_ref[...].astype(_MXU)
        dp_ref[:, width:2 * width] = dup
        dn = _dot_nt(dp_ref[:, 0:width], ws_ref[...]) + _dot_nt(dup, wp_ref[...])
        hv = h_ref[...]
        r = _rs(hv)
        hh = hv * r
        dg_ref[...] += jnp.sum(dn * hh, axis=0, keepdims=True)
        dhh = dn * g_ref[...]
        dh_ref[...] = dho_ref[...] + r * (dhh - hh * jnp.mean(dhh * hh, axis=-1, keepdims=True))

    row = pl.BlockSpec((tm, d), lambda i: (i, 0))
    half = pl.BlockSpec((tm, width), lambda i: (i, 0))
    return pl.pallas_call(
        body, name=name, grid=(n_rows // tm,),
        out_shape=[jax.ShapeDtypeStruct((n_rows, d), F32), jax.ShapeDtypeStruct((n_rows, 2 * width), _MXU),
                   jax.ShapeDtypeStruct((1, d), F32)],
        in_specs=[_lane_blocks(n_blk, tm), half, row, row, _full((1, d)), _resident(), _resident()],
        out_specs=[row, pl.BlockSpec((tm, 2 * width), lambda i: (i, 0)), _full((1, d))], compiler_params=_params(),
    )(du_s, du_p, h, dho, g, wi_s, wi_p)


def _loss_grad(h, target, name):
    n_rows, d = h.shape
    tm = _row_tile(n_rows)

    def body(h_ref, t_ref, dh_ref, l_ref):
        i = pl.program_id(0)

        @pl.when(i == 0)
        def _():
            l_ref[...] = jnp.zeros_like(l_ref)

        rows = i * tm + lax.broadcasted_iota(jnp.int32, (tm, 1), 0)
        err = jnp.where(rows >= N_META, h_ref[...] - t_ref[...], 0.0)
        dh_ref[...] = err / d
        l_ref[...] += jnp.sum(jnp.sum(err * err, axis=0, keepdims=True), axis=1, keepdims=True)

    row = pl.BlockSpec((tm, d), lambda i: (i, 0))
    return pl.pallas_call(
        body, name=name, grid=(n_rows // tm,),
        out_shape=[jax.ShapeDtypeStruct((n_rows, d), F32), jax.ShapeDtypeStruct((1, LANES), F32)],
        in_specs=[row, row], out_specs=[row, _full((1, LANES))], compiler_params=_params(),
    )(h, target)


def _adamw_update(w_ref, gv, m_ref, v_ref, d_ref, mo_ref, vo_ref):
    mn = ADAM_B1 * m_ref[...] + (1.0 - ADAM_B1) * gv
    vn = ADAM_B2 * v_ref[...] + (1.0 - ADAM_B2) * (gv * gv)
    m_hat = mn / (1.0 - ADAM_B1 ** ADAM_STEP)
    v_hat = vn / (1.0 - ADAM_B2 ** ADAM_STEP)
    d_ref[...] = -ADAM_LR * (m_hat / (jnp.sqrt(v_hat) + ADAM_EPS) + ADAM_WD * w_ref[...])
    mo_ref[...] = mn
    vo_ref[...] = vn


def _adamw(w, g, m, v, name):
    def body(w_ref, g_ref, m_ref, v_ref, d_ref, mo_ref, vo_ref):
        _adamw_update(w_ref, g_ref[...], m_ref, v_ref, d_ref, mo_ref, vo_ref)

    spec = _full(w.shape)
    return pl.pallas_call(
        body, name=name, grid=(1,), out_shape=[jax.ShapeDtypeStruct(w.shape, F32)] * 3,
        in_specs=[spec] * 4, out_specs=[spec] * 3, compiler_params=_params(),
    )(w, g, m, v)


def _adamw_slots(w, slots, m, v, name):
    def body(w_ref, s_ref, m_ref, v_ref, g_ref, d_ref, mo_ref, vo_ref):
        gv = s_ref[0].astype(F32)
        for k in range(1, N_DEV):
            gv = gv + s_ref[k].astype(F32)
        g_ref[...] = gv
        _adamw_update(w_ref, gv, m_ref, v_ref, d_ref, mo_ref, vo_ref)

    spec = _full(w.shape)
    return pl.pallas_call(
        body, name=name, grid=(1,), out_shape=[jax.ShapeDtypeStruct(w.shape, F32)] * 4,
        in_specs=[spec, _full(slots.shape), spec, spec], out_specs=[spec] * 4, compiler_params=_params(),
    )(w, slots, m, v)


def _discretize(lam_re, lam_im, log_dt, b_re, b_im):
    dt = jnp.exp(log_dt)[:, None]
    decay = jnp.exp(lam_re * dt)
    ang = lam_im * dt
    a_re = decay * jnp.cos(ang)
    a_im = decay * jnp.sin(ang)
    nr = a_re - 1.0
    den = lam_re * lam_re + lam_im * lam_im
    q_re = (nr * lam_re + a_im * lam_im) / den
    q_im = (a_im * lam_re - nr * lam_im) / den
    bb_re = q_re[..., None] * b_re - q_im[..., None] * b_im
    bb_im = q_re[..., None] * b_im + q_im[..., None] * b_re
    return a_re, a_im, bb_re, bb_im


def _cmul(a, b):
    return a[0] * b[0] - a[1] * b[1], a[0] * b[1] + a[1] * b[0]


def _scan_coefficients(a_re, a_im, reverse):
    a = (a_re.reshape(1, -1), -a_im.reshape(1, -1) if reverse else a_im.reshape(1, -1))
    powers = [a]
    for _ in range(SUBLANES - 1):
        powers.append(_cmul(powers[-1], a))
    row = jnp.arange(SUBLANES)[:, None]
    out = []
    for k in (1, 2, 4):
        keep = (row < SUBLANES - k) if reverse else (row >= k)
        out += [jnp.where(keep, powers[k - 1][0], 0.0), jnp.where(keep, powers[k - 1][1], 0.0)]
    order = range(SUBLANES - 1, -1, -1) if reverse else range(SUBLANES)
    out += [jnp.concatenate([powers[t][0] for t in order], axis=0), jnp.concatenate([powers[t][1] for t in order], axis=0)]
    return jnp.stack(out).astype(F32)


def _power_table(a_re, a_im, count, reverse):
    base = (a_re.reshape(1, -1), (-a_im if reverse else a_im).reshape(1, -1))
    exponent = (jnp.arange(count, 0, -1) if reverse else jnp.arange(1, count + 1))[:, None]
    shape = (count, base[0].shape[1])
    res = (jnp.ones(shape, F32), jnp.zeros(shape, F32))
    for bit in range(int(count).bit_length()):
        prod = _cmul(res, base)
        take = ((exponent >> bit) & 1) == 1
        res = (jnp.where(take, prod[0], res[0]), jnp.where(take, prod[1], res[1]))
        base = _cmul(base, base)
    return jnp.broadcast_to(jnp.stack(res)[:, :, None, :], (2, count, SUBLANES, shape[1]))


def _block_diag(p, n_blk):
    g, r, c = p.shape
    per = g // n_blk
    eye = jnp.eye(per, dtype=p.dtype)
    return jnp.einsum("jgrc,gk->jgrkc", p.reshape(n_blk, per, r, c), eye).reshape(n_blk, per * r, per * c)


def _block_diag_t(m, g):
    n_blk = m.shape[0]
    per = g // n_blk
    r, c = m.shape[1] // per, m.shape[2] // per
    eye = jnp.eye(per, dtype=m.dtype)
    return jnp.einsum("jgrkc,gk->jgrc", m.reshape(n_blk, per, r, per, c), eye).reshape(g, r, c)


def _pack_rows(parts, cols, multiple):
    flat = jnp.concatenate([p.reshape(-1) for p in parts])
    size = -(-flat.shape[0] // (cols * multiple)) * cols * multiple
    return jnp.pad(flat, (0, size - flat.shape[0])).reshape(-1, cols)


def _unpack(flat, shapes):
    out, pos = [], 0
    flat = flat.reshape(-1)
    for s in shapes:
        n = int(np.prod(s))
        out.append(flat[pos:pos + n].reshape(s))
        pos += n
    return out


def kernel(x, meta_tokens, ffn1_pre_norm, ffn1_post_norm, ffn1_w_gate, ffn1_w_up, ffn1_w_down, mix_pre_norm, mix_post_norm, w_in, ssm_lambda_re, ssm_lambda_im, ssm_log_dt, ssm_b_re, ssm_b_im, ssm_c_re, ssm_c_im, ssm_d, ssm_w_glu, pool_w, pool_scale, ssm_out_norm, pool_out_norm, w_out, ffn2_pre_norm, ffn2_post_norm, ffn2_w_gate, ffn2_w_up, ffn2_w_down, loss_target, m_meta_tokens, m_ffn1_pre_norm, m_ffn1_post_norm, m_ffn1_w_gate, m_ffn1_w_up, m_ffn1_w_down, m_mix_pre_norm, m_mix_post_norm, m_w_in, m_ssm_lambda_re, m_ssm_lambda_im, m_ssm_log_dt, m_ssm_b_re, m_ssm_b_im, m_ssm_c_re, m_ssm_c_im, m_ssm_d, m_ssm_w_glu, m_pool_w, m_pool_scale, m_ssm_out_norm, m_pool_out_norm, m_w_out, m_ffn2_pre_norm, m_ffn2_post_norm, m_ffn2_w_gate, m_ffn2_w_up, m_ffn2_w_down, v_meta_tokens, v_ffn1_pre_norm, v_ffn1_post_norm, v_ffn1_w_gate, v_ffn1_w_up, v_ffn1_w_down, v_mix_pre_norm, v_mix_post_norm, v_w_in, v_ssm_lambda_re, v_ssm_lambda_im, v_ssm_log_dt, v_ssm_b_re, v_ssm_b_im, v_ssm_c_re, v_ssm_c_im, v_ssm_d, v_ssm_w_glu, v_pool_w, v_pool_scale, v_ssm_out_norm, v_pool_out_norm, v_w_out, v_ffn2_pre_norm, v_ffn2_post_norm, v_ffn2_w_gate, v_ffn2_w_up, v_ffn2_w_down):
    args = dict(locals())
    names = ["meta_tokens", "ffn1_pre_norm", "ffn1_post_norm", "ffn1_w_gate", "ffn1_w_up", "ffn1_w_down", "mix_pre_norm",
             "mix_post_norm", "w_in", "ssm_lambda_re", "ssm_lambda_im", "ssm_log_dt", "ssm_b_re", "ssm_b_im", "ssm_c_re",
             "ssm_c_im", "ssm_d", "ssm_w_glu", "pool_w", "pool_scale", "ssm_out_norm", "pool_out_norm", "w_out",
             "ffn2_pre_norm", "ffn2_post_norm", "ffn2_w_gate", "ffn2_w_up", "ffn2_w_down"]
    sharded = ("meta_tokens", "ffn1_w_gate", "ffn1_w_up", "ffn1_w_down", "w_in", "w_out", "ffn2_w_gate", "ffn2_w_up",
               "ffn2_w_down")
    small = [n for n in names if n not in sharded]

    d = x.shape[-1]
    width = d // 2
    n_grp = ssm_lambda_re.shape[1]
    n_blk = width // LANES

    def stacked(gathered):
        return gathered.reshape(-1, d).astype(_MXU)

    def chunked(m):
        return m.reshape(N_DEV, -1, d)

    s_gate1, s_up1, s_down1 = _to_wire([ffn1_w_gate[0].T, ffn1_w_up[0].T, ffn1_w_down[0]], "wire_ffn1")
    s_win, s_wout, s_gate2, s_up2, s_down2 = _to_wire(
        [w_in[0], w_out[0], ffn2_w_gate[0].T, ffn2_w_up[0].T, ffn2_w_down[0]], "wire_rest")
    g_gate1, g_up1, g_down1, meta_all = _gather_two_level([s_gate1, s_up1, s_down1, meta_tokens], "gather_ffn1")
    wgt1, wut1, wd1 = stacked(g_gate1), stacked(g_up1), stacked(g_down1)
    meta_full = jnp.transpose(meta_all, (1, 0, 2)).reshape(N_META, d)

    (a1, b1, f1, h1, h0), (g_win, g_wout, g_gate2) = _ffn_fwd(
        None, ffn1_pre_norm, ffn1_post_norm, wgt1, wut1, wd1, "ffn1_fwd",
        xchg=_Xchg([s_win, s_wout, s_gate2], ["gather"] * 3), meta=meta_full, tokens=x[0])
    w_in_f, w_out_f = stacked(g_win), stacked(g_wout)

    a_re, a_im, bb_re, bb_im = _discretize(ssm_lambda_re[0], ssm_lambda_im[0], ssm_log_dt[0], ssm_b_re[0], ssm_b_im[0])
    steps = _row_tile(N_META + x.shape[1]) // SUBLANES
    coef = _power_table(a_re, a_im, steps, reverse=False)
    coef_rev = _power_table(a_re, a_im, steps, reverse=True)
    bmat_re = _block_diag(jnp.swapaxes(bb_re, 1, 2), n_blk).astype(_MXU)
    bmat_im = _block_diag(jnp.swapaxes(bb_im, 1, 2), n_blk).astype(_MXU)
    cmat_re = _block_diag(jnp.swapaxes(ssm_c_re[0], 1, 2), n_blk).astype(_MXU)
    cmat_im = _block_diag(jnp.swapaxes(ssm_c_im[0], 1, 2), n_blk).astype(_MXU)
    wz1 = _block_diag(ssm_w_glu[0][:, :, :SSM_GROUP_CH], n_blk).astype(_MXU)
    wz2 = _block_diag(ssm_w_glu[0][:, :, SSM_GROUP_CH:], n_blk).astype(_MXU)
    pool_wm = pool_w[0].astype(_MXU)

    n2, u_s, u_p = _mix_in_fwd(h1, mix_pre_norm, w_in_f, "mix_in_fwd")
    (x_re, x_im, cat_s), (g_up2, g_down2) = _ssm_forward(
        u_s, coef, bmat_re, bmat_im, cmat_re, cmat_im, wz1, wz2, ssm_d, ssm_out_norm, "ssm_fwd",
        xchg=_Xchg([s_up2, s_down2], ["gather"] * 2))
    wgt2, wut2, wd2 = stacked(g_gate2), stacked(g_up2), stacked(g_down2)
    cat_p = _pool_fwd(u_p, pool_wm, pool_scale, pool_out_norm, "pool_fwd")
    wo_s, wo_p = w_out_f[:width], w_out_f[width:]
    mixed, h2 = _mix_out_fwd(cat_s, cat_p, h1, mix_post_norm, wo_s, wo_p, "mix_out_fwd")

    (a2, b2, f2, dh3, sq_err), _ = _ffn_fwd(
        h2, ffn2_pre_norm, ffn2_post_norm, wgt2, wut2, wd2, "ffn2_fwd", target=loss_target[0])

    g, slots = {}, {}
    (dh2, da2, db2, s2, df2, nf2, g["ffn2_pre_norm"], g["ffn2_post_norm"]), _ = _ffn_bwd(
        dh3, h2, f2, a2, b2, ffn2_pre_norm, ffn2_post_norm, wgt2, wut2, wd2, "ffn2_bwd")
    dgate2, _ = _wgrad(da2, nf2, "ffn2_dgate")
    dup2, _ = _wgrad(db2, nf2, "ffn2_dup")
    ddown2, _ = _wgrad(s2, df2, "ffn2_ddown")

    dmixed, dcat_s, dcat_p, g["mix_post_norm"] = _mix_out_bwd(dh2, mixed, mix_post_norm, wo_s, wo_p, "mix_out_bwd")
    dwout = jnp.concatenate([_wgrad(cat_s, dmixed, "dwout_s")[0], _wgrad(cat_p, dmixed, "dwout_p")[0]], axis=0)
    ((du_s, g["ssm_out_norm"], g["ssm_d"], dwz1, dwz2, dcm_re, dcm_im, dbm_re, dbm_im, acc_re, acc_im),
     (slots["ffn2_w_gate"], slots["ffn2_w_up"], slots["ffn2_w_down"])) = _ssm_backward(
        u_s, x_re, x_im, dcat_s, coef_rev, bmat_re, bmat_im, cmat_re, cmat_im, wz1, wz2, ssm_d, ssm_out_norm, "ssm_bwd",
        xchg=_Xchg([chunked(dgate2), chunked(dup2), chunked(ddown2)], ["scatter"] * 3))
    du_p, g["pool_out_norm"], g["pool_scale"], dpw = _pool_bwd(u_p, dcat_p, pool_wm, pool_scale, pool_out_norm, "pool_bwd")
    wi_s, wi_p = w_in_f[:, :width], w_in_f[:, width:]
    dh1, dproj, g["mix_pre_norm"] = _mix_in_bwd(du_s, du_p, h1, dh2, mix_pre_norm, wi_s, wi_p, "mix_in_bwd")
    dwin, _ = _wgrad(n2, dproj, "dwin")

    g["ssm_c_re"] = jnp.swapaxes(_block_diag_t(jnp.swapaxes(dcm_re, 1, 2), n_grp), 1, 2)[None]
    g["ssm_c_im"] = jnp.swapaxes(_block_diag_t(jnp.swapaxes(dcm_im, 1, 2), n_grp), 1, 2)[None]
    g["ssm_w_glu"] = jnp.concatenate([_block_diag_t(dwz1, n_grp), _block_diag_t(dwz2, n_grp)], axis=-1)[None]
    dbb_re = jnp.swapaxes(_block_diag_t(dbm_re, n_grp), 1, 2)
    dbb_im = jnp.swapaxes(_block_diag_t(dbm_im, n_grp), 1, 2)
    da_re, da_im = acc_re.reshape(a_re.shape), acc_im.reshape(a_re.shape)
    _, disc_vjp = jax.vjp(_discretize, ssm_lambda_re[0], ssm_lambda_im[0], ssm_log_dt[0], ssm_b_re[0], ssm_b_im[0])
    d_lre, d_lim, d_ldt, d_bre, d_bim = disc_vjp((da_re, da_im, dbb_re, dbb_im))
    g["ssm_lambda_re"], g["ssm_lambda_im"], g["ssm_log_dt"] = d_lre[None], d_lim[None], d_ldt[None]
    g["ssm_b_re"], g["ssm_b_im"] = d_bre[None], d_bim[None]
    g["pool_w"] = dpw[None]

    late = ["ffn1_pre_norm", "ffn1_post_norm"]
    early = [n for n in small if n not in late]
    early_vec = _pack_rows([g[n] for n in early] + [sq_err[:, :1]], 1024, SUBLANES)
    ((dx, da1, db1, s1, df1, nf1, g["ffn1_pre_norm"], g["ffn1_post_norm"], dmeta_part),
     (slots["w_out"], slots["w_in"], recv_early)) = _ffn_bwd(
        dh1, h0, f1, a1, b1, ffn1_pre_norm, ffn1_post_norm, wgt1, wut1, wd1, "ffn1_bwd",
        xchg=_Xchg([chunked(dwout), chunked(dwin), early_vec], ["scatter", "scatter", "gather"]), split_meta=True)
    late_vec = _pack_rows([g[n] for n in late] + [dmeta_part], 1024, SUBLANES)
    dgate1, (recv_late,) = _wgrad(da1, nf1, "ffn1_dgate", xchg=_Xchg([late_vec], ["gather"]))
    dup1, (slots["ffn1_w_gate"],) = _wgrad(db1, nf1, "ffn1_dup", xchg=_Xchg([chunked(dgate1)], ["scatter"]))
    ddown1, (slots["ffn1_w_up"],) = _wgrad(s1, df1, "ffn1_ddown", xchg=_Xchg([chunked(dup1)], ["scatter"]))
    (slots["ffn1_w_down"],) = _exchange([chunked(ddown1)], ["scatter"], "reduce_last")

    summed = _unpack(_sum_slots(recv_early, "sum_small_grads"), [g[n].shape for n in early] + [(1,)])
    for n, val in zip(early, summed):
        g[n] = val
    loss = (0.5 / d) * summed[-1][0]
    g[late[0]], g[late[1]], dmeta = _unpack(_sum_slots(recv_late, "sum_last_grads"), [(1, d), (1, d), (N_META, d)])
    g["meta_tokens"] = lax.dynamic_slice_in_dim(dmeta, _my_slot() * (d // N_DEV), d // N_DEV, axis=1)

    delta, new_m, new_v = {}, {}, {}
    for n in sharded:
        shape = args[n].shape
        two_d = (-1, shape[-1])
        w2, m2, v2 = args[n].reshape(two_d), args["m_" + n].reshape(two_d), args["v_" + n].reshape(two_d)
        if n == "meta_tokens":
            dl, mn, vn = _adamw(w2, g[n], m2, v2, "adamw_" + n)
        elif n.endswith("gate") or n.endswith("up"):
            gs, dl, mn, vn = [t.T for t in _adamw_slots(w2.T, slots[n], m2.T, v2.T, "adamw_" + n)]
        else:
            gs, dl, mn, vn = _adamw_slots(w2, slots[n], m2, v2, "adamw_" + n)
        if n != "meta_tokens":
            g[n] = gs[None]
        delta[n], new_m[n], new_v[n] = dl.reshape(shape), mn.reshape(shape), vn.reshape(shape)
    packs = [_pack_rows([src[n] if pre is None else args[pre + n] for n in small], 1024, SUBLANES)
             for src, pre in ((args, None), (g, None), (None, "m_"), (None, "v_"))]
    outs = _adamw(*packs, "adamw_small")
    shapes = [args[n].shape for n in small]
    for store, flat in zip((delta, new_m, new_v), outs):
        for n, val in zip(small, _unpack(flat, shapes)):
            store[n] = val

    grad_x = dx[None]
    return (loss, grad_x, *[g[n] for n in names], *[delta[n] for n in names], *[new_m[n] for n in names],
            *[new_v[n] for n in names])
```

```python
import functools
import math

import jax
import jax.numpy as jnp
import numpy as np
from jax import lax
from jax.experimental import pallas as pl
from jax.experimental.pallas import tpu as pltpu

F32 = jnp.float32
_MXU = jnp.bfloat16
_ACT = jnp.bfloat16
_WIRE = jnp.bfloat16

N_DEV = 8
N_META = 16
RMS_EPS = 1e-6
SSM_GROUP_CH = 16
LANES = 128
SUBLANES = 8
POOL_WINDOWS = (2, 4, 8, 16)
POOL_HALO = 16
ADAM_LR = 0.001
ADAM_B1 = 0.9
ADAM_B2 = 0.999
ADAM_EPS = 1e-08
ADAM_WD = 0.01
ADAM_STEP = 10
GELU_C0 = math.sqrt(2.0 / math.pi)
GELU_C1 = 0.044715
VMEM_LIMIT = 62 * 1024 * 1024

_NT = (((1,), (1,)), ((), ()))
_TN = (((0,), (0,)), ((), ()))


def _dot(a, b):
    return jnp.dot(a, b, preferred_element_type=F32)


def _dot_nt(a, b):
    return lax.dot_general(a, b, _NT, preferred_element_type=F32)


def _dot_tn(a, b):
    return lax.dot_general(a, b, _TN, preferred_element_type=F32)


def _rs(x):
    return lax.rsqrt(jnp.mean(x * x, axis=-1, keepdims=True) + RMS_EPS)


def _sigmoid(x):
    return 0.5 * jnp.tanh(0.5 * x) + 0.5


def _row_tile(n_rows, largest=432):
    for t in (432, 304, 48, 16):
        if t <= largest and n_rows % t == 0:
            return t
    raise ValueError(n_rows)


def _ff_chunk(d_ff):
    return d_ff // 2 if (d_ff // 2) % LANES == 0 else d_ff


def _params(n_axes=1):
    return pltpu.CompilerParams(dimension_semantics=("arbitrary",) * n_axes, vmem_limit_bytes=VMEM_LIMIT)


def _resident():
    return pl.BlockSpec(memory_space=pltpu.VMEM)


def _full(shape):
    nd = len(shape)
    return pl.BlockSpec(shape, lambda *_: (0,) * nd)


def _lane_blocks(n_blk, tm, tile_of=lambda i: i):
    return pl.BlockSpec((n_blk, tm, LANES), lambda i: (0, tile_of(i), 0))


PEER_ORDER = (1, 2, 4, 3, 5, 6, 7)


def _split(refs, counts):
    out, pos = [], 0
    for n in counts:
        out.append(refs[pos:pos + n])
        pos += n
    return out


def _peer(r):
    x, y, c = lax.axis_index("x"), lax.axis_index("y"), lax.axis_index("c")
    return (1 - x if r & 4 else x, 1 - y if r & 2 else y, 1 - c if r & 1 else c)


def _my_slot():
    return 4 * lax.axis_index("x") + 2 * lax.axis_index("y") + lax.axis_index("c")


class _Xchg:
    def __init__(self, srcs, kinds):
        self.srcs, self.kinds, self.n = list(srcs), list(kinds), len(srcs)
        self.out_shape = [jax.ShapeDtypeStruct((N_DEV,) + s.shape if k == "gather" else s.shape, s.dtype)
                          for s, k in zip(self.srcs, self.kinds)]
        self.specs = [pl.BlockSpec(memory_space=pl.ANY)] * self.n
        self.scratch = [pltpu.SemaphoreType.DMA((self.n * (N_DEV - 1),)), pltpu.SemaphoreType.DMA((self.n * (N_DEV - 1),)),
                        pltpu.SemaphoreType.DMA((self.n,))]

    def copies(self, src, dst, sems):
        send_sems, recv_sems, local_sems = sems
        me = _my_slot()
        out = []
        for a in range(self.n):
            mine = src[a] if self.kinds[a] == "gather" else src[a].at[me]
            out.append(pltpu.make_async_copy(mine, dst[a].at[me], local_sems.at[a]))
            for r in PEER_ORDER:
                px, py, pc = _peer(r)
                part = src[a] if self.kinds[a] == "gather" else src[a].at[4 * px + 2 * py + pc]
                k = a * (N_DEV - 1) + r - 1
                out.append(pltpu.make_async_remote_copy(
                    src_ref=part, dst_ref=dst[a].at[me], send_sem=send_sems.at[k], recv_sem=recv_sems.at[k],
                    device_id=(px, py, pc), device_id_type=pl.DeviceIdType.MESH))
        return out

    def start(self, src, dst, sems):
        for cp in self.copies(src, dst, sems):
            cp.start()

    def wait(self, src, dst, sems):
        for cp in self.copies(src, dst, sems):
            cp.wait()


class _NoXchg:
    n, srcs, out_shape, specs, scratch = 0, [], [], [], []

    def start(self, *_):
        pass

    wait = start


def _call(body, args, *, name, grid, out_shape, in_specs, out_specs, scratch_shapes=(), xchg=None):
    xc = xchg or _NoXchg()
    counts = (len(in_specs), xc.n, len(out_shape), xc.n, len(scratch_shapes), len(xc.scratch))

    def wrapped(*refs):
        ins, xsrc, outs, xdst, scr, sems = _split(refs, counts)
        ids = [pl.program_id(k) for k in range(len(grid))]
        if xc.n:
            @pl.when(functools.reduce(jnp.logical_and, [i == 0 for i in ids]))
            def _():
                xc.start(xsrc, xdst, sems)

        body(*ins, *outs, *scr)
        if xc.n:
            @pl.when(functools.reduce(jnp.logical_and, [i == g - 1 for i, g in zip(ids, grid)]))
            def _():
                xc.wait(xsrc, xdst, sems)

    res = pl.pallas_call(
        wrapped, name=name, grid=grid, out_shape=list(out_shape) + xc.out_shape,
        in_specs=list(in_specs) + xc.specs, out_specs=list(out_specs) + xc.specs,
        scratch_shapes=list(scratch_shapes) + xc.scratch, compiler_params=_params(len(grid)),
    )(*args, *xc.srcs)
    return res[:len(out_shape)], res[len(out_shape):]


def _exchange(srcs, kinds, name):
    xc = _Xchg(srcs, kinds)

    def body(*refs):
        src, dst, sems = _split(refs, (xc.n, xc.n, 3))
        xc.start(src, dst, sems)
        xc.wait(src, dst, sems)

    return pl.pallas_call(body, name=name, out_shape=xc.out_shape, in_specs=xc.specs, out_specs=xc.specs,
                          scratch_shapes=xc.scratch)(*srcs)


def _gather_two_level(srcs, name):
    n = len(srcs)
    out_shape = [jax.ShapeDtypeStruct((N_DEV,) + s.shape, s.dtype) for s in srcs]
    chips = (2, 4, 6)

    def body(*refs):
        src, dst, (send_sems, recv_sems, local_sems) = _split(refs, (n, n, 3))
        x, y, c = lax.axis_index("x"), lax.axis_index("y"), lax.axis_index("c")
        me = 4 * x + 2 * y + c
        sibling = (x, y, 1 - c)

        def copy(a, k, slot, to, from_src=False):
            return pltpu.make_async_remote_copy(
                src_ref=src[a] if from_src else dst[a].at[slot], dst_ref=dst[a].at[slot],
                send_sem=send_sems.at[a * 7 + k], recv_sem=recv_sems.at[a * 7 + k],
                device_id=to, device_id_type=pl.DeviceIdType.MESH)

        def slot_of(r, core):
            px, py, _ = _peer(r)
            return 4 * px + 2 * py + core

        local = [pltpu.make_async_copy(src[a], dst[a].at[me], local_sems.at[a]) for a in range(n)]
        sent = []
        for a in range(n):
            local[a].start()
            sent.append(copy(a, 0, me, sibling, from_src=True))
            sent += [copy(a, 1 + j, me, _peer(r), from_src=True) for j, r in enumerate(chips)]
        for cp in sent:
            cp.start()
        for j, r in enumerate(chips):
            for a in range(n):
                copy(a, 1 + j, slot_of(r, c), _peer(r)).wait_recv()
                cp = copy(a, 4 + j, slot_of(r, c), sibling)
                cp.start()
                sent.append(cp)
        for a in range(n):
            copy(a, 0, slot_of(0, 1 - c), sibling).wait_recv()
            for j, r in enumerate(chips):
                copy(a, 4 + j, slot_of(r, 1 - c), sibling).wait_recv()
        for cp in local:
            cp.wait()
        for cp in sent:
            cp.wait_send()

    any_spec = pl.BlockSpec(memory_space=pl.ANY)
    return pl.pallas_call(
        body, name=name, out_shape=out_shape, in_specs=[any_spec] * n, out_specs=[any_spec] * n,
        scratch_shapes=[pltpu.SemaphoreType.DMA((n * 7,)), pltpu.SemaphoreType.DMA((n * 7,)), pltpu.SemaphoreType.DMA((n,))],
    )(*srcs)


def _to_wire(mats, name):
    def body(*refs):
        for src, dst in zip(refs[:len(mats)], refs[len(mats):]):
            dst[...] = src[...].astype(_WIRE)

    return pl.pallas_call(
        body, name=name, grid=(1,), out_shape=[jax.ShapeDtypeStruct(m.shape, _WIRE) for m in mats],
        in_specs=[_full(m.shape) for m in mats], out_specs=[_full(m.shape) for m in mats], compiler_params=_params(),
    )(*mats)


def _sum_slots(r, name):
    _, rows, cols = r.shape
    blk = rows
    for cand in (rows, 592, 512, 256, 128, 64, 32, 16):
        if rows % cand == 0 and N_DEV * cand * cols * r.dtype.itemsize <= 8 * 1024 * 1024:
            blk = cand
            break

    def body(r_ref, o_ref):
        acc = r_ref[0].astype(F32)
        for d in range(1, N_DEV):
            acc = acc + r_ref[d].astype(F32)
        o_ref[...] = acc

    return pl.pallas_call(
        body, name=name, grid=(rows // blk,), out_shape=jax.ShapeDtypeStruct((rows, cols), F32),
        in_specs=[pl.BlockSpec((N_DEV, blk, cols), lambda i: (0, i, 0))],
        out_specs=pl.BlockSpec((blk, cols), lambda i: (i, 0)), compiler_params=_params(),
    )(r)


def _token_tile_copy(tokens_ref, buf, sems, i, tm, write=False):
    if isinstance(i, int) and i == 0:
        far, near = tokens_ref.at[pl.ds(0, tm - N_META)], buf.at[0, pl.ds(N_META, tm - N_META)]
    else:
        start = i * tm - N_META if isinstance(i, int) else pl.multiple_of(i * tm - N_META, SUBLANES)
        far, near = tokens_ref.at[pl.ds(start, tm)], buf.at[i % 2]
    return pltpu.make_async_copy(near, far, sems.at[i % 2]) if write else pltpu.make_async_copy(far, near, sems.at[i % 2])


def _fetch_token_tile(tokens_ref, buf, sems, i, n_t, tm):
    @pl.when(i == 0)
    def _():
        _token_tile_copy(tokens_ref, buf, sems, 0, tm).start()

    @pl.when(i + 1 < n_t)
    def _():
        _token_tile_copy(tokens_ref, buf, sems, i + 1, tm).start()

    @pl.when(i == 0)
    def _():
        _token_tile_copy(tokens_ref, buf, sems, 0, tm).wait()

    @pl.when(i > 0)
    def _():
        _token_tile_copy(tokens_ref, buf, sems, i, tm).wait()


def _ffn_fwd(h, g_pre, g_post, wgt, wut, wd, name, xchg=None, *, meta=None, tokens=None, target=None):
    first = h is None
    d = wd.shape[1]
    n_rows = N_META + tokens.shape[0] if first else h.shape[0]
    d_ff = wd.shape[0]
    tm, fc = _row_tile(n_rows), _ff_chunk(d_ff)
    n_t, n_c = n_rows // tm, d_ff // fc

    def body(src_ref, side_ref, gpre_ref, gpost_ref, wgt_ref, wut_ref, wd_ref, a_ref, b_ref, f_ref, o1_ref, o2_ref,
             n_scr, acc, buf, sems):
        i, c = pl.program_id(0), pl.program_id(1)
        slot = i % 2

        @pl.when(c == 0)
        def _():
            if first:
                _fetch_token_tile(side_ref, buf, sems, i, n_t, tm)

                @pl.when(i == 0)
                def _():
                    buf[0, 0:N_META, :] = src_ref[...]

                hv = buf[slot]
                o2_ref[...] = hv
            else:
                _fetch_token_tile(side_ref, buf, sems, i, n_t, tm)

                @pl.when(i == 0)
                def _():
                    buf[0, 0:N_META, :] = jnp.zeros((N_META, d), F32)
                    o2_ref[...] = jnp.zeros_like(o2_ref)

                hv = src_ref[...]
            n_scr[...] = (hv * _rs(hv) * gpre_ref[...]).astype(_MXU)
            acc[...] = jnp.zeros_like(acc)

        rows = pl.ds(pl.multiple_of(c * fc, fc), fc)
        nv = n_scr[...]
        a = _dot_nt(nv, wgt_ref[rows, :])
        b = _dot_nt(nv, wut_ref[rows, :])
        a_ref[...] = a.astype(_ACT)
        b_ref[...] = b.astype(_ACT)
        s = a * _sigmoid(a) * b
        acc[...] += _dot(s.astype(_MXU), wd_ref[rows, :])

        @pl.when(c == n_c - 1)
        def _():
            f = acc[...]
            f_ref[...] = f
            step = 0.5 * (f * _rs(f) * gpost_ref[...])
            if first:
                o1_ref[...] = buf[slot] + step
            else:
                row = i * tm + lax.broadcasted_iota(jnp.int32, (tm, 1), 0)
                err = jnp.where(row >= N_META, (src_ref[...] + step) - buf[slot], 0.0)
                o1_ref[...] = err / d
                o2_ref[...] += jnp.sum(jnp.sum(err * err, axis=0, keepdims=True), axis=1, keepdims=True)

    row = pl.BlockSpec((tm, d), lambda i, c: (i, 0))
    chunk = pl.BlockSpec((tm, fc), lambda i, c: (i, c))
    hbm = pl.BlockSpec(memory_space=pl.ANY)
    if first:
        operands, specs = (meta, tokens), [_full(meta.shape), hbm]
        last_shape, last_spec = jax.ShapeDtypeStruct((n_rows, d), F32), row
    else:
        operands, specs = (h, target), [row, hbm]
        last_shape, last_spec = jax.ShapeDtypeStruct((1, LANES), F32), pl.BlockSpec((1, LANES), lambda i, c: (0, 0))
    return _call(
        body, (*operands, g_pre, g_post, wgt, wut, wd), name=name, grid=(n_t, n_c),
        out_shape=[jax.ShapeDtypeStruct((n_rows, d_ff), _ACT), jax.ShapeDtypeStruct((n_rows, d_ff), _ACT),
                   jax.ShapeDtypeStruct((n_rows, d), F32), jax.ShapeDtypeStruct((n_rows, d), F32), last_shape],
        in_specs=specs + [_full((1, d)), _full((1, d)), _resident(), _resident(), _resident()],
        out_specs=[chunk, chunk, row, row, last_spec],
        scratch_shapes=[pltpu.VMEM((tm, d), _MXU), pltpu.VMEM((tm, d), F32), pltpu.VMEM((2, tm, d), F32),
                        pltpu.SemaphoreType.DMA((2,))], xchg=xchg)


def _ffn_bwd(dho, h, f, a, b, g_pre, g_post, wgt, wut, wd, name, xchg=None, *, split_meta=False):
    n_rows, d = h.shape
    d_ff = wd.shape[0]
    tm, fc = _row_tile(n_rows), _ff_chunk(d_ff)
    n_t, n_c = n_rows // tm, d_ff // fc
    assert n_t >= 2

    def body(dho_ref, h_ref, f_ref, a_ref, b_ref, gpre_ref, gpost_ref, wgt_ref, wut_ref, wd_ref,
             dh_ref, da_ref, db_ref, s_ref, df_ref, n_ref, dgpre_ref, dgpost_ref, *rest):
        dn_acc = rest[-1]
        i, c = pl.program_id(0), pl.program_id(1)

        @pl.when((i == 0) & (c == 0))
        def _():
            dgpre_ref[...] = jnp.zeros_like(dgpre_ref)
            dgpost_ref[...] = jnp.zeros_like(dgpost_ref)

        @pl.when(c == 0)
        def _():
            fv = f_ref[...]
            rf = _rs(fv)
            fhat = fv * rf
            dy = 0.5 * dho_ref[...]
            dgpost_ref[...] += jnp.sum(dy * fhat, axis=0, keepdims=True)
            dfhat = dy * gpost_ref[...]
            df = rf * (dfhat - fhat * jnp.mean(dfhat * fhat, axis=-1, keepdims=True))
            df_ref[...] = df.astype(_MXU)
            hv = h_ref[...]
            n_ref[...] = (hv * _rs(hv) * gpre_ref[...]).astype(_MXU)
            dn_acc[...] = jnp.zeros_like(dn_acc)

        rows = pl.ds(pl.multiple_of(c * fc, fc), fc)
        ds = _dot_nt(df_ref[...], wd_ref[rows, :])
        av = a_ref[...].astype(F32)
        bv = b_ref[...].astype(F32)
        sg = _sigmoid(av)
        si = av * sg
        da = (ds * bv * (sg * (1.0 + av * (1.0 - sg)))).astype(_MXU)
        db = (ds * si).astype(_MXU)
        da_ref[...] = da
        db_ref[...] = db
        s_ref[...] = (si * bv).astype(_MXU)
        dn_acc[...] += _dot(da, wgt_ref[rows, :]) + _dot(db, wut_ref[rows, :])

        @pl.when(c == n_c - 1)
        def _():
            dn = dn_acc[...]
            hv = h_ref[...]
            r = _rs(hv)
            hhat = hv * r
            dgpre_ref[...] += jnp.sum(dn * hhat, axis=0, keepdims=True)
            dhh = dn * gpre_ref[...]
            dh = dho_ref[...] + r * (dhh - hhat * jnp.mean(dhh * hhat, axis=-1, keepdims=True))
            if not split_meta:
                dh_ref[...] = dh
                return
            dmeta_ref, buf, sems = rest[0], rest[1], rest[2]

            def out_copy(k):
                return _token_tile_copy(dh_ref, buf, sems, k, tm, write=True)

            @pl.when(i == 2)
            def _():
                out_copy(0).wait()

            @pl.when(i > 2)
            def _():
                out_copy(i - 2).wait()

            buf[i % 2] = dh

            @pl.when(i == 0)
            def _():
                dmeta_ref[...] = buf[0, 0:N_META, :]
                out_copy(0).start()

            @pl.when(i > 0)
            def _():
                out_copy(i).start()

            @pl.when(i == n_t - 1)
            def _():
                out_copy(n_t - 2).wait()
                out_copy(n_t - 1).wait()

    row = pl.BlockSpec((tm, d), lambda i, c: (i, 0))
    chunk = pl.BlockSpec((tm, fc), lambda i, c: (i, c))
    vec = pl.BlockSpec((1, d), lambda i, c: (0, 0))
    shapes = [jax.ShapeDtypeStruct((n_rows, d_ff), _MXU)] * 3 + [jax.ShapeDtypeStruct((n_rows, d), _MXU)] * 2 \
        + [jax.ShapeDtypeStruct((1, d), F32)] * 2
    specs = [chunk, chunk, chunk, row, row, vec, vec]
    scratch = [pltpu.VMEM((tm, d), F32)]
    if split_meta:
        shapes = [jax.ShapeDtypeStruct((n_rows - N_META, d), F32)] + shapes + [jax.ShapeDtypeStruct((N_META, d), F32)]
        specs = [pl.BlockSpec(memory_space=pl.ANY)] + specs + [pl.BlockSpec((N_META, d), lambda i, c: (0, 0))]
        scratch = [pltpu.VMEM((2, tm, d), F32), pltpu.SemaphoreType.DMA((2,))] + scratch
    else:
        shapes, specs = [jax.ShapeDtypeStruct((n_rows, d), F32)] + shapes, [row] + specs
    return _call(
        body, (dho, h, f, a, b, g_pre, g_post, wgt, wut, wd), name=name, grid=(n_t, n_c), out_shape=shapes,
        in_specs=[row, row, row, chunk, chunk, vec, vec, _resident(), _resident(), _resident()], out_specs=specs,
        scratch_shapes=scratch, xchg=xchg)


def _wgrad(xm, ym, name, xchg=None):
    n_rows, a_dim = xm.shape
    b_dim = ym.shape[1]
    tk = n_rows
    for cand in (2736, 1296, 432, 48, 16):
        if n_rows % cand == 0:
            tk = cand
            break
    ta = a_dim
    for cand in (1408, 1024, 512):
        if a_dim % cand == 0:
            ta = cand
            break

    n_k = n_rows // tk

    def body(x_ref, y_ref, o_ref, acc):
        k = pl.program_id(1)

        @pl.when(k == 0)
        def _():
            acc[...] = jnp.zeros_like(acc)

        acc[...] += _dot_tn(x_ref[...], y_ref[...])

        @pl.when(k == n_k - 1)
        def _():
            o_ref[...] = acc[...].astype(o_ref.dtype)

    (out,), extra = _call(
        body, (xm, ym), name=name, grid=(a_dim // ta, n_k), out_shape=[jax.ShapeDtypeStruct((a_dim, b_dim), _WIRE)],
        in_specs=[pl.BlockSpec((tk, ta), lambda j, k: (k, j)), pl.BlockSpec((tk, b_dim), lambda j, k: (k, 0))],
        out_specs=[pl.BlockSpec((ta, b_dim), lambda j, k: (j, 0))], scratch_shapes=[pltpu.VMEM((ta, b_dim), F32)],
        xchg=xchg)
    return out, extra


def _mix_in_fwd(h, g, w_in, name):
    n_rows, d = h.shape
    tm = _row_tile(n_rows)
    width = w_in.shape[1] // 2
    n_blk = width // LANES

    def body(h_ref, g_ref, w_ref, n_ref, us_ref, up_ref):
        hv = h_ref[...]
        nv = (hv * _rs(hv) * g_ref[...]).astype(_MXU)
        n_ref[...] = nv
        p = _dot(nv, w_ref[...])
        for k in range(n_blk):
            us_ref[k] = p[:, k * LANES:(k + 1) * LANES]
        up_ref[...] = p[:, width:]

    row = pl.BlockSpec((tm, d), lambda i: (i, 0))
    half = pl.BlockSpec((tm, width), lambda i: (i, 0))
    return pl.pallas_call(
        body, name=name, grid=(n_rows // tm,),
        out_shape=[jax.ShapeDtypeStruct((n_rows, d), _MXU), jax.ShapeDtypeStruct((n_blk, n_rows, LANES), F32),
                   jax.ShapeDtypeStruct((n_rows, width), F32)],
        in_specs=[row, _full((1, d)), _resident()], out_specs=[row, _lane_blocks(n_blk, tm), half],
        compiler_params=_params(),
    )(h, g, w_in)


def _gelu_parts(y):
    th = jnp.tanh(GELU_C0 * (y + GELU_C1 * (y * y * y)))
    return 0.5 * (1.0 + th), th


SCAN_LANES = 512


def _regroup(src_ref, dst_ref, seg):
    for k in range(src_ref.shape[0]):
        for j in range(seg):
            dst_ref[j * SUBLANES:(j + 1) * SUBLANES, k * LANES:(k + 1) * LANES] = src_ref[k, pl.ds(j, SUBLANES, stride=seg), :]


def _ungroup(src_ref, dst_ref, seg):
    for k in range(dst_ref.shape[0]):
        for j in range(seg):
            dst_ref[k, pl.ds(j, SUBLANES, stride=seg), :] = src_ref[j * SUBLANES:(j + 1) * SUBLANES, k * LANES:(k + 1) * LANES]


def _rows_to_sublanes(rows):
    rid = lax.broadcasted_iota(jnp.int32, (SUBLANES, rows[0].shape[1]), 0)
    out = jnp.broadcast_to(rows[0], rid.shape)
    for s in range(1, SUBLANES):
        out = jnp.where(rid == s, rows[s], out)
    return out


def _segment_scan(re_ref, im_ref, tab_ref, car_re, car_im, seg, reverse, x_refs=None):
    n_all = re_ref.shape[1]
    first = seg - 1 if reverse else 0
    sums = []
    for cb in range(n_all // SCAN_LANES):
        cols = pl.ds(cb * SCAN_LANES, SCAN_LANES)
        ar, ai = tab_ref[0, first, :, cols], tab_ref[1, first, :, cols]

        def local(t, carry, cols=cols, ar=ar, ai=ai):
            r0 = pl.multiple_of((seg - 1 - t if reverse else t) * SUBLANES, SUBLANES)
            xr, xi = carry
            nr = ar * xr - ai * xi + re_ref[pl.ds(r0, SUBLANES), cols]
            ni = ar * xi + ai * xr + im_ref[pl.ds(r0, SUBLANES), cols]
            re_ref[pl.ds(r0, SUBLANES), cols] = nr
            im_ref[pl.ds(r0, SUBLANES), cols] = ni
            return nr, ni

        zero = jnp.zeros((SUBLANES, SCAN_LANES), F32)
        fr, fi = lax.fori_loop(0, seg, local, (zero, zero))

        last = 0 if reverse else seg - 1
        sr, si = tab_ref[0, last, 0:1, cols], tab_ref[1, last, 0:1, cols]
        cr, ci = car_re[:, cols], car_im[:, cols]
        rows_r, rows_i = [None] * SUBLANES, [None] * SUBLANES
        for s in (range(SUBLANES - 1, -1, -1) if reverse else range(SUBLANES)):
            rows_r[s], rows_i[s] = cr, ci
            cr, ci = sr * cr - si * ci + fr[s:s + 1], sr * ci + si * cr + fi[s:s + 1]
        car_re[:, cols] = cr
        car_im[:, cols] = ci
        cmr, cmi = _rows_to_sublanes(rows_r), _rows_to_sublanes(rows_i)

        def fix(t, carry, cols=cols, cmr=cmr, cmi=cmi):
            j = seg - 1 - t if reverse else t
            r0 = pl.multiple_of(j * SUBLANES, SUBLANES)
            pr, pi = tab_ref[0, j, :, cols], tab_ref[1, j, :, cols]
            gr = re_ref[pl.ds(r0, SUBLANES), cols] + (pr * cmr - pi * cmi)
            gi = im_ref[pl.ds(r0, SUBLANES), cols] + (pr * cmi + pi * cmr)
            re_ref[pl.ds(r0, SUBLANES), cols] = gr
            im_ref[pl.ds(r0, SUBLANES), cols] = gi
            if x_refs is None:
                return carry
            nxr, nxi, accr, acci = carry
            xr, xi = x_refs[0][pl.ds(r0, SUBLANES), cols], x_refs[1][pl.ds(r0, SUBLANES), cols]
            return gr, gi, accr + (xr * nxr + xi * nxi), acci + (xr * nxi - xi * nxr)

        if x_refs is None:
            lax.fori_loop(0, seg, fix, 0)
        else:
            fin = lax.fori_loop(0, seg, fix, (cmr, cmi, zero, zero))
            sums.append((jnp.sum(fin[2], axis=0, keepdims=True), jnp.sum(fin[3], axis=0, keepdims=True)))
    return sums


def _ssm_forward(u, table, b_re, b_im, c_re, c_im, wz1, wz2, d_skip, g_out, name, xchg=None):
    n_rows = u.shape[1]
    n_blk, _, n_state = b_re.shape
    width, n_all = n_blk * LANES, n_blk * n_state
    tm = _row_tile(n_rows)
    seg = tm // SUBLANES

    def body(u_ref, tab_ref, bre_ref, bim_ref, cre_ref, cim_ref, wz1_ref, wz2_ref, d_ref, g_ref,
             xre_ref, xim_ref, o_ref, car_re, car_im, ug, out_scr, blocks):
        @pl.when(pl.program_id(0) == 0)
        def _():
            car_re[...] = jnp.zeros_like(car_re)
            car_im[...] = jnp.zeros_like(car_im)

        _regroup(u_ref, ug, seg)
        ub = ug[...].astype(_MXU)
        for j in range(n_blk):
            uj = ub[:, j * LANES:(j + 1) * LANES]
            xre_ref[:, j * n_state:(j + 1) * n_state] = _dot(uj, bre_ref[j])
            xim_ref[:, j * n_state:(j + 1) * n_state] = _dot(uj, bim_ref[j])
        _segment_scan(xre_ref, xim_ref, tab_ref, car_re, car_im, seg, reverse=False)

        ssq = None
        for j in range(n_blk):
            sl = slice(j * LANES, (j + 1) * LANES)
            st = slice(j * n_state, (j + 1) * n_state)
            yc = _dot(xre_ref[:, st].astype(_MXU), cre_ref[j]) - _dot(xim_ref[:, st].astype(_MXU), cim_ref[j])
            y = yc + d_ref[:, sl] * ug[:, sl]
            cdf, _ = _gelu_parts(y)
            gy = (y * cdf).astype(_MXU)
            out = _dot(gy, wz1_ref[j]) * _sigmoid(_dot(gy, wz2_ref[j]))
            out_scr[:, sl] = out
            part = jnp.sum(out * out, axis=-1, keepdims=True)
            ssq = part if ssq is None else ssq + part
        r = lax.rsqrt(ssq / width + RMS_EPS)
        out_scr[...] = out_scr[...] * r * g_ref[...]
        _ungroup(out_scr, blocks, seg)
        for k in range(n_blk):
            o_ref[:, k * LANES:(k + 1) * LANES] = blocks[k].astype(_MXU)

    half = pl.BlockSpec((tm, width), lambda i: (i, 0))
    state = pl.BlockSpec((tm, n_all), lambda i: (i, 0))
    return _call(
        body, (u, table, b_re, b_im, c_re, c_im, wz1, wz2, d_skip, g_out), name=name, grid=(n_rows // tm,),
        out_shape=[jax.ShapeDtypeStruct((n_rows, n_all), F32)] * 2 + [jax.ShapeDtypeStruct((n_rows, width), _MXU)],
        in_specs=[_lane_blocks(n_blk, tm), _resident(), _full(b_re.shape), _full(b_im.shape), _full(c_re.shape),
                  _full(c_im.shape), _full(wz1.shape), _full(wz2.shape), _full((1, width)), _full((1, width))],
        out_specs=[state, state, half],
        scratch_shapes=[pltpu.VMEM((1, n_all), F32)] * 2 + [pltpu.VMEM((tm, width), F32)] * 2
        + [pltpu.VMEM((n_blk, tm, LANES), F32)], xchg=xchg)


def _ssm_backward(u, x_re, x_im, dcat, table, b_re, b_im, c_re, c_im, wz1, wz2, d_skip, g_out, name, xchg=None):
    n_rows = u.shape[1]
    n_blk, _, n_state = b_re.shape
    width, n_all = n_blk * LANES, n_blk * n_state
    tm = _row_tile(n_rows)
    n_t, seg = n_rows // tm, tm // SUBLANES

    def body(u_ref, xre_ref, xim_ref, dc_ref, tab_ref, bre_ref, bim_ref, cre_ref, cim_ref, wz1_ref, wz2_ref, d_ref, g_ref,
             du_ref, dg_ref, dd_ref, dwz1_ref, dwz2_ref, dcre_ref, dcim_ref, dbre_ref, dbim_ref, are_ref, aim_ref,
             car_re, car_im, gre, gim, ug, y_s, z1_s, sg_s, out_s, gy_s):
        @pl.when(pl.program_id(0) == 0)
        def _():
            for ref in (dg_ref, dd_ref, dwz1_ref, dwz2_ref, dcre_ref, dcim_ref, dbre_ref, dbim_ref, are_ref, aim_ref,
                        car_re, car_im):
                ref[...] = jnp.zeros_like(ref)

        _regroup(u_ref, ug, seg)
        ssq = None
        for j in range(n_blk):
            sl = slice(j * LANES, (j + 1) * LANES)
            st = slice(j * n_state, (j + 1) * n_state)
            yc = _dot(xre_ref[:, st].astype(_MXU), cre_ref[j]) - _dot(xim_ref[:, st].astype(_MXU), cim_ref[j])
            y = yc + d_ref[:, sl] * ug[:, sl]
            cdf, _ = _gelu_parts(y)
            gy = (y * cdf).astype(_MXU)
            z1 = _dot(gy, wz1_ref[j])
            sg = _sigmoid(_dot(gy, wz2_ref[j]))
            out = z1 * sg
            y_s[:, sl], z1_s[:, sl], sg_s[:, sl], out_s[:, sl], gy_s[:, sl] = y, z1, sg, out, gy
            part = jnp.sum(out * out, axis=-1, keepdims=True)
            ssq = part if ssq is None else ssq + part
        r = lax.rsqrt(ssq / width + RMS_EPS)
        ohat = out_s[...] * r
        _regroup(dc_ref, out_s, seg)
        dcv = out_s[...]
        dg_ref[...] += jnp.sum(dcv * ohat, axis=0, keepdims=True)
        doh = dcv * g_ref[...]
        out_s[...] = r * (doh - ohat * (jnp.sum(doh * ohat, axis=-1, keepdims=True) / width))

        for j in range(n_blk):
            sl = slice(j * LANES, (j + 1) * LANES)
            st = slice(j * n_state, (j + 1) * n_state)
            dout, sg, z1, y = out_s[:, sl], sg_s[:, sl], z1_s[:, sl], y_s[:, sl]
            dz1 = (dout * sg).astype(_MXU)
            dz2 = (dout * z1 * sg * (1.0 - sg)).astype(_MXU)
            gy = gy_s[:, sl]
            dwz1_ref[j] += _dot_tn(gy, dz1)
            dwz2_ref[j] += _dot_tn(gy, dz2)
            dgy = _dot_nt(dz1, wz1_ref[j]) + _dot_nt(dz2, wz2_ref[j])
            cdf, th = _gelu_parts(y)
            dy = dgy * (cdf + y * (0.5 * (1.0 - th * th) * GELU_C0 * (1.0 + 3.0 * GELU_C1 * (y * y))))
            uj = ug[:, sl]
            dd_ref[:, sl] += jnp.sum(dy * uj, axis=0, keepdims=True)
            z1_s[:, sl] = d_ref[:, sl] * dy
            dyb = dy.astype(_MXU)
            dcre_ref[j] += _dot_tn(dyb, xre_ref[:, st].astype(_MXU))
            dcim_ref[j] -= _dot_tn(dyb, xim_ref[:, st].astype(_MXU))
            gre[:, st] = _dot_nt(dyb, cre_ref[j])
            gim[:, st] = -_dot_nt(dyb, cim_ref[j])

        sums = _segment_scan(gre, gim, tab_ref, car_re, car_im, seg, reverse=True, x_refs=(xre_ref, xim_ref))
        for cb, (sum_re, sum_im) in enumerate(sums):
            cols = pl.ds(cb * SCAN_LANES, SCAN_LANES)
            are_ref[:, cols] += sum_re
            aim_ref[:, cols] += sum_im

        for j in range(n_blk):
            sl = slice(j * LANES, (j + 1) * LANES)
            st = slice(j * n_state, (j + 1) * n_state)
            ujb = ug[:, sl].astype(_MXU)
            grb, gib = gre[:, st].astype(_MXU), gim[:, st].astype(_MXU)
            dbre_ref[j] += _dot_tn(ujb, grb)
            dbim_ref[j] += _dot_tn(ujb, gib)
            z1_s[:, sl] += _dot_nt(grb, bre_ref[j]) + _dot_nt(gib, bim_ref[j])
        _ungroup(z1_s, du_ref, seg)

    half = _lane_blocks(n_blk, tm, lambda i: n_t - 1 - i)
    state = pl.BlockSpec((tm, n_all), lambda i: (n_t - 1 - i, 0))
    small = [(1, width), (1, width), wz1.shape, wz2.shape, (n_blk, LANES, n_state), (n_blk, LANES, n_state),
             (n_blk, LANES, n_state), (n_blk, LANES, n_state), (1, n_all), (1, n_all)]
    return _call(
        body, (u, x_re, x_im, dcat, table, b_re, b_im, c_re, c_im, wz1, wz2, d_skip, g_out), name=name, grid=(n_t,),
        out_shape=[jax.ShapeDtypeStruct((n_blk, n_rows, LANES), F32)] + [jax.ShapeDtypeStruct(s, F32) for s in small],
        in_specs=[half, state, state, half, _resident(), _full(b_re.shape), _full(b_im.shape), _full(c_re.shape),
                  _full(c_im.shape), _full(wz1.shape), _full(wz2.shape), _full((1, width)), _full((1, width))],
        out_specs=[half] + [_full(s) for s in small],
        scratch_shapes=[pltpu.VMEM((1, n_all), F32)] * 2 + [pltpu.VMEM((tm, n_all), F32)] * 2
        + [pltpu.VMEM((tm, width), F32)] * 5 + [pltpu.VMEM((tm, width), _MXU)], xchg=xchg)


def _pool_counts(tile, tm, window):
    t = tile * tm + lax.broadcasted_iota(jnp.int32, (tm, 1), 0)
    return jnp.minimum(t + 1, window).astype(F32)


def _pool_fwd(proj, pool_w, scale, g_out, name):
    n_rows = proj.shape[0]
    n_grp, grp, _ = pool_w.shape
    width = n_grp * grp
    tm = _row_tile(n_rows)

    def body(u_ref, pw_ref, sc_ref, g_ref, o_ref, ext, y_s):
        i = pl.program_id(0)

        @pl.when(i == 0)
        def _():
            ext[0:POOL_HALO, :] = jnp.zeros((POOL_HALO, width), F32)

        ext[POOL_HALO:, :] = u_ref[...]
        ssq = None
        for gi, w in enumerate(POOL_WINDOWS):
            sl = slice(gi * grp, (gi + 1) * grp)
            tot = ext[POOL_HALO:, sl]
            for k in range(1, w):
                tot = tot + ext[POOL_HALO - k:POOL_HALO - k + tm, sl]
            pooled = tot / _pool_counts(i, tm, w) - u_ref[:, sl]
            y = _dot(pooled.astype(_MXU), pw_ref[gi]) * sc_ref[:, sl]
            y_s[:, sl] = y
            part = jnp.sum(y * y, axis=-1, keepdims=True)
            ssq = part if ssq is None else ssq + part
        r = lax.rsqrt(ssq / width + RMS_EPS)
        o_ref[...] = (y_s[...] * r * g_ref[...]).astype(_MXU)
        ext[0:POOL_HALO, :] = u_ref[tm - POOL_HALO:, :]

    half_in = pl.BlockSpec((tm, width), lambda i: (i, 0))
    half = pl.BlockSpec((tm, width), lambda i: (i, 0))
    return pl.pallas_call(
        body, name=name, grid=(n_rows // tm,), out_shape=jax.ShapeDtypeStruct((n_rows, width), _MXU),
        in_specs=[half_in, _full(pool_w.shape), _full((1, width)), _full((1, width))], out_specs=half,
        scratch_shapes=[pltpu.VMEM((tm + POOL_HALO, width), F32), pltpu.VMEM((tm, width), F32)],
        compiler_params=_params(),
    )(proj, pool_w, scale, g_out)


def _pool_bwd(proj, dcat, pool_w, scale, g_out, name):
    n_rows = proj.shape[0]
    n_grp, grp, _ = pool_w.shape
    width = n_grp * grp
    tm = _row_tile(n_rows)
    n_t = n_rows // tm
    halo_blocks = tm // POOL_HALO

    def body(u_ref, up_ref, dc_ref, pw_ref, sc_ref, g_ref, du_ref, dg_ref, dsc_ref, dpw_ref, ext, qext, y_s, pl_s):
        i = pl.program_id(0)
        tile = n_t - 1 - i

        @pl.when(i == 0)
        def _():
            for ref in (dg_ref, dsc_ref, dpw_ref):
                ref[...] = jnp.zeros_like(ref)
            qext[tm:, :] = jnp.zeros((POOL_HALO, width), F32)

        ext[0:POOL_HALO, :] = jnp.where(tile > 0, up_ref[...], 0.0)
        ext[POOL_HALO:, :] = u_ref[...]
        ssq = None
        for gi, w in enumerate(POOL_WINDOWS):
            sl = slice(gi * grp, (gi + 1) * grp)
            tot = ext[POOL_HALO:, sl]
            for k in range(1, w):
                tot = tot + ext[POOL_HALO - k:POOL_HALO - k + tm, sl]
            pooled = (tot / _pool_counts(tile, tm, w) - u_ref[:, sl]).astype(_MXU)
            pl_s[:, sl] = pooled
            y0 = _dot(pooled, pw_ref[gi])
            y_s[:, sl] = y0
            y = y0 * sc_ref[:, sl]
            part = jnp.sum(y * y, axis=-1, keepdims=True)
            ssq = part if ssq is None else ssq + part
        r = lax.rsqrt(ssq / width + RMS_EPS)
        y0 = y_s[...]
        yhat = y0 * sc_ref[...] * r
        dcv = dc_ref[...]
        dg_ref[...] += jnp.sum(dcv * yhat, axis=0, keepdims=True)
        dyh = dcv * g_ref[...]
        dy = r * (dyh - yhat * (jnp.sum(dyh * yhat, axis=-1, keepdims=True) / width))
        dsc_ref[...] += jnp.sum(dy * y0, axis=0, keepdims=True)
        y_s[...] = dy * sc_ref[...]
        for gi, w in enumerate(POOL_WINDOWS):
            sl = slice(gi * grp, (gi + 1) * grp)
            dm = y_s[:, sl].astype(_MXU)
            dpw_ref[gi] += _dot_tn(pl_s[:, sl], dm)
            dpooled = _dot_nt(dm, pw_ref[gi])
            y_s[:, sl] = dpooled
            qext[0:tm, sl] = dpooled / _pool_counts(tile, tm, w)
        for gi, w in enumerate(POOL_WINDOWS):
            sl = slice(gi * grp, (gi + 1) * grp)
            tot = qext[0:tm, sl]
            for k in range(1, w):
                tot = tot + qext[k:k + tm, sl]
            du_ref[:, sl] = tot - y_s[:, sl]
        qext[tm:, :] = qext[0:POOL_HALO, :]

    half_in = pl.BlockSpec((tm, width), lambda i: (n_t - 1 - i, 0))
    prev = pl.BlockSpec((POOL_HALO, width), lambda i: (jnp.maximum((n_t - 1 - i) * halo_blocks - 1, 0), 0))
    half = pl.BlockSpec((tm, width), lambda i: (n_t - 1 - i, 0))
    return pl.pallas_call(
        body, name=name, grid=(n_t,),
        out_shape=[jax.ShapeDtypeStruct((n_rows, width), F32), jax.ShapeDtypeStruct((1, width), F32),
                   jax.ShapeDtypeStruct((1, width), F32), jax.ShapeDtypeStruct(pool_w.shape, F32)],
        in_specs=[half_in, prev, half, _full(pool_w.shape), _full((1, width)), _full((1, width))],
        out_specs=[half, _full((1, width)), _full((1, width)), _full(pool_w.shape)],
        scratch_shapes=[pltpu.VMEM((tm + POOL_HALO, width), F32), pltpu.VMEM((tm + POOL_HALO, width), F32),
                        pltpu.VMEM((tm, width), F32), pltpu.VMEM((tm, width), _MXU)],
        compiler_params=_params(),
    )(proj, proj, dcat, pool_w, scale, g_out)


def _mix_out_fwd(cat_s, cat_p, h, g, wo_s, wo_p, name):
    n_rows, d = h.shape
    width = cat_s.shape[1]
    tm = _row_tile(n_rows)

    def body(cs_ref, cp_ref, h_ref, g_ref, ws_ref, wp_ref, m_ref, ho_ref):
        m = _dot(cs_ref[...], ws_ref[...]) + _dot(cp_ref[...], wp_ref[...])
        m_ref[...] = m
        ho_ref[...] = h_ref[...] + m * _rs(m) * g_ref[...]

    row = pl.BlockSpec((tm, d), lambda i: (i, 0))
    half = pl.BlockSpec((tm, width), lambda i: (i, 0))
    return pl.pallas_call(
        body, name=name, grid=(n_rows // tm,), out_shape=[jax.ShapeDtypeStruct((n_rows, d), F32)] * 2,
        in_specs=[half, half, row, _full((1, d)), _resident(), _resident()], out_specs=[row, row],
        compiler_params=_params(),
    )(cat_s, cat_p, h, g, wo_s, wo_p)


def _mix_out_bwd(dho, mixed, g, wo_s, wo_p, name):
    n_rows, d = mixed.shape
    width = wo_s.shape[0]
    n_blk = width // LANES
    tm = _row_tile(n_rows)

    def body(dho_ref, m_ref, g_ref, ws_ref, wp_ref, dm_ref, dcs_ref, dcp_ref, dg_ref):
        @pl.when(pl.program_id(0) == 0)
        def _():
            dg_ref[...] = jnp.zeros_like(dg_ref)

        m = m_ref[...]
        r = _rs(m)
        mh = m * r
        dy = dho_ref[...]
        dg_ref[...] += jnp.sum(dy * mh, axis=0, keepdims=True)
        dmh = dy * g_ref[...]
        dm = (r * (dmh - mh * jnp.mean(dmh * mh, axis=-1, keepdims=True))).astype(_MXU)
        dm_ref[...] = dm
        dcs = _dot_nt(dm, ws_ref[...])
        for k in range(n_blk):
            dcs_ref[k] = dcs[:, k * LANES:(k + 1) * LANES]
        dcp_ref[...] = _dot_nt(dm, wp_ref[...])

    row = pl.BlockSpec((tm, d), lambda i: (i, 0))
    half = pl.BlockSpec((tm, width), lambda i: (i, 0))
    return pl.pallas_call(
        body, name=name, grid=(n_rows // tm,),
        out_shape=[jax.ShapeDtypeStruct((n_rows, d), _MXU), jax.ShapeDtypeStruct((n_blk, n_rows, LANES), F32),
                   jax.ShapeDtypeStruct((n_rows, width), F32), jax.ShapeDtypeStruct((1, d), F32)],
        in_specs=[row, row, _full((1, d)), _resident(), _resident()],
        out_specs=[row, _lane_blocks(n_blk, tm), half, _full((1, d))], compiler_params=_params(),
    )(dho, mixed, g, wo_s, wo_p)


def _mix_in_bwd(du_s, du_p, h, dho, g, wi_s, wi_p, name):
    n_rows, d = h.shape
    width = du_p.shape[1]
    n_blk = width // LANES
    tm = _row_tile(n_rows)

    def body(dus_ref, dup_ref, h_ref, dho_ref, g_ref, ws_ref, wp_ref, dh_ref, dp_ref, dg_ref):
        @pl.when(pl.program_id(0) == 0)
        def _():
            dg_ref[...] = jnp.zeros_like(dg_ref)

        for k in range(n_blk):
            dp_ref[:, k * LANES:(k + 1) * LANES] = dus_ref[k].astype(_MXU)
        dup = dup_ref[...].astype(_MXU)
        dp_ref[:, width:2 * width] = dup
        dn = _dot_nt(dp_ref[:, 0:width], ws_ref[...]) + _dot_nt(dup, wp_ref[...])
        hv = h_ref[...]
        r = _rs(hv)
        hh = hv * r
        dg_ref[...] += jnp.sum(dn * hh, axis=0, keepdims=True)
        dhh = dn * g_ref[...]
        dh_ref[...] = dho_ref[...] + r * (dhh - hh * jnp.mean(dhh * hh, axis=-1, keepdims=True))

    row = pl.BlockSpec((tm, d), lambda i: (i, 0))
    half = pl.BlockSpec((tm, width), lambda i: (i, 0))
    return pl.pallas_call(
        body, name=name, grid=(n_rows // tm,),
        out_shape=[jax.ShapeDtypeStruct((n_rows, d), F32), jax.ShapeDtypeStruct((n_rows, 2 * width), _MXU),
                   jax.ShapeDtypeStruct((1, d), F32)],
        in_specs=[_lane_blocks(n_blk, tm), half, row, row, _full((1, d)), _resident(), _resident()],
        out_specs=[row, pl.BlockSpec((tm, 2 * width), lambda i: (i, 0)), _full((1, d))], compiler_params=_params(),
    )(du_s, du_p, h, dho, g, wi_s, wi_p)


def _adamw_update(w_ref, gv, m_ref, v_ref, d_ref, mo_ref, vo_ref):
    mn = ADAM_B1 * m_ref[...] + (1.0 - ADAM_B1) * gv
    vn = ADAM_B2 * v_ref[...] + (1.0 - ADAM_B2) * (gv * gv)
    m_hat = mn / (1.0 - ADAM_B1 ** ADAM_STEP)
    v_hat = vn / (1.0 - ADAM_B2 ** ADAM_STEP)
    d_ref[...] = -ADAM_LR * (m_hat / (jnp.sqrt(v_hat) + ADAM_EPS) + ADAM_WD * w_ref[...])
    mo_ref[...] = mn
    vo_ref[...] = vn


def _adamw(w, g, m, v, name):
    def body(w_ref, g_ref, m_ref, v_ref, d_ref, mo_ref, vo_ref):
        _adamw_update(w_ref, g_ref[...], m_ref, v_ref, d_ref, mo_ref, vo_ref)

    spec = _full(w.shape)
    return pl.pallas_call(
        body, name=name, grid=(1,), out_shape=[jax.ShapeDtypeStruct(w.shape, F32)] * 3,
        in_specs=[spec] * 4, out_specs=[spec] * 3, compiler_params=_params(),
    )(w, g, m, v)


def _adamw_slots(w, slots, m, v, name):
    def body(w_ref, s_ref, m_ref, v_ref, g_ref, d_ref, mo_ref, vo_ref):
        gv = s_ref[0].astype(F32)
        for k in range(1, N_DEV):
            gv = gv + s_ref[k].astype(F32)
        g_ref[...] = gv
        _adamw_update(w_ref, gv, m_ref, v_ref, d_ref, mo_ref, vo_ref)

    spec = _full(w.shape)
    return pl.pallas_call(
        body, name=name, grid=(1,), out_shape=[jax.ShapeDtypeStruct(w.shape, F32)] * 4,
        in_specs=[spec, _full(slots.shape), spec, spec], out_specs=[spec] * 4, compiler_params=_params(),
    )(w, slots, m, v)


def _discretize(lam_re, lam_im, log_dt, b_re, b_im):
    dt = jnp.exp(log_dt)[:, None]
    decay = jnp.exp(lam_re * dt)
    ang = lam_im * dt
    a_re = decay * jnp.cos(ang)
    a_im = decay * jnp.sin(ang)
    nr = a_re - 1.0
    den = lam_re * lam_re + lam_im * lam_im
    q_re = (nr * lam_re + a_im * lam_im) / den
    q_im = (a_im * lam_re - nr * lam_im) / den
    bb_re = q_re[..., None] * b_re - q_im[..., None] * b_im
    bb_im = q_re[..., None] * b_im + q_im[..., None] * b_re
    return a_re, a_im, bb_re, bb_im


def _cmul(a, b):
    return a[0] * b[0] - a[1] * b[1], a[0] * b[1] + a[1] * b[0]


def _power_table(a_re, a_im, count, reverse):
    base = (a_re.reshape(1, -1), (-a_im if reverse else a_im).reshape(1, -1))
    exponent = (jnp.arange(count, 0, -1) if reverse else jnp.arange(1, count + 1))[:, None]
    shape = (count, base[0].shape[1])
    res = (jnp.ones(shape, F32), jnp.zeros(shape, F32))
    for bit in range(int(count).bit_length()):
        prod = _cmul(res, base)
        take = ((exponent >> bit) & 1) == 1
        res = (jnp.where(take, prod[0], res[0]), jnp.where(take, prod[1], res[1]))
        base = _cmul(base, base)
    return jnp.broadcast_to(jnp.stack(res)[:, :, None, :], (2, count, SUBLANES, shape[1]))


def _block_diag(p, n_blk):
    g, r, c = p.shape
    per = g // n_blk
    eye = jnp.eye(per, dtype=p.dtype)
    return jnp.einsum("jgrc,gk->jgrkc", p.reshape(n_blk, per, r, c), eye).reshape(n_blk, per * r, per * c)


def _block_diag_t(m, g):
    n_blk = m.shape[0]
    per = g // n_blk
    r, c = m.shape[1] // per, m.shape[2] // per
    eye = jnp.eye(per, dtype=m.dtype)
    return jnp.einsum("jgrkc,gk->jgrc", m.reshape(n_blk, per, r, per, c), eye).reshape(g, r, c)


def _pack_rows(parts, cols, multiple):
    flat = jnp.concatenate([p.reshape(-1) for p in parts])
    size = -(-flat.shape[0] // (cols * multiple)) * cols * multiple
    return jnp.pad(flat, (0, size - flat.shape[0])).reshape(-1, cols)


def _unpack(flat, shapes):
    out, pos = [], 0
    flat = flat.reshape(-1)
    for s in shapes:
        n = int(np.prod(s))
        out.append(flat[pos:pos + n].reshape(s))
        pos += n
    return out


def kernel(x, meta_tokens, ffn1_pre_norm, ffn1_post_norm, ffn1_w_gate, ffn1_w_up, ffn1_w_down, mix_pre_norm, mix_post_norm, w_in, ssm_lambda_re, ssm_lambda_im, ssm_log_dt, ssm_b_re, ssm_b_im, ssm_c_re, ssm_c_im, ssm_d, ssm_w_glu, pool_w, pool_scale, ssm_out_norm, pool_out_norm, w_out, ffn2_pre_norm, ffn2_post_norm, ffn2_w_gate, ffn2_w_up, ffn2_w_down, loss_target, m_meta_tokens, m_ffn1_pre_norm, m_ffn1_post_norm, m_ffn1_w_gate, m_ffn1_w_up, m_ffn1_w_down, m_mix_pre_norm, m_mix_post_norm, m_w_in, m_ssm_lambda_re, m_ssm_lambda_im, m_ssm_log_dt, m_ssm_b_re, m_ssm_b_im, m_ssm_c_re, m_ssm_c_im, m_ssm_d, m_ssm_w_glu, m_pool_w, m_pool_scale, m_ssm_out_norm, m_pool_out_norm, m_w_out, m_ffn2_pre_norm, m_ffn2_post_norm, m_ffn2_w_gate, m_ffn2_w_up, m_ffn2_w_down, v_meta_tokens, v_ffn1_pre_norm, v_ffn1_post_norm, v_ffn1_w_gate, v_ffn1_w_up, v_ffn1_w_down, v_mix_pre_norm, v_mix_post_norm, v_w_in, v_ssm_lambda_re, v_ssm_lambda_im, v_ssm_log_dt, v_ssm_b_re, v_ssm_b_im, v_ssm_c_re, v_ssm_c_im, v_ssm_d, v_ssm_w_glu, v_pool_w, v_pool_scale, v_ssm_out_norm, v_pool_out_norm, v_w_out, v_ffn2_pre_norm, v_ffn2_post_norm, v_ffn2_w_gate, v_ffn2_w_up, v_ffn2_w_down):
    args = dict(locals())
    names = ["meta_tokens", "ffn1_pre_norm", "ffn1_post_norm", "ffn1_w_gate", "ffn1_w_up", "ffn1_w_down", "mix_pre_norm",
             "mix_post_norm", "w_in", "ssm_lambda_re", "ssm_lambda_im", "ssm_log_dt", "ssm_b_re", "ssm_b_im", "ssm_c_re",
             "ssm_c_im", "ssm_d", "ssm_w_glu", "pool_w", "pool_scale", "ssm_out_norm", "pool_out_norm", "w_out",
             "ffn2_pre_norm", "ffn2_post_norm", "ffn2_w_gate", "ffn2_w_up", "ffn2_w_down"]
    sharded = ("meta_tokens", "ffn1_w_gate", "ffn1_w_up", "ffn1_w_down", "w_in", "w_out", "ffn2_w_gate", "ffn2_w_up",
               "ffn2_w_down")
    small = [n for n in names if n not in sharded]

    d = x.shape[-1]
    width = d // 2
    n_grp = ssm_lambda_re.shape[1]
    n_blk = width // LANES

    def stacked(gathered):
        return gathered.reshape(-1, d).astype(_MXU)

    def chunked(m):
        return m.reshape(N_DEV, -1, d)

    s_gate1, s_up1, s_down1 = _to_wire([ffn1_w_gate[0].T, ffn1_w_up[0].T, ffn1_w_down[0]], "wire_ffn1")
    s_win, s_wout, s_gate2, s_up2, s_down2 = _to_wire(
        [w_in[0], w_out[0], ffn2_w_gate[0].T, ffn2_w_up[0].T, ffn2_w_down[0]], "wire_rest")
    g_gate1, g_up1, g_down1, meta_all = _gather_two_level([s_gate1, s_up1, s_down1, meta_tokens], "gather_ffn1")
    wgt1, wut1, wd1 = stacked(g_gate1), stacked(g_up1), stacked(g_down1)
    meta_full = jnp.transpose(meta_all, (1, 0, 2)).reshape(N_META, d)

    (a1, b1, f1, h1, h0), (g_win, g_wout, g_gate2) = _ffn_fwd(
        None, ffn1_pre_norm, ffn1_post_norm, wgt1, wut1, wd1, "ffn1_fwd",
        xchg=_Xchg([s_win, s_wout, s_gate2], ["gather"] * 3), meta=meta_full, tokens=x[0])
    w_in_f, w_out_f = stacked(g_win), stacked(g_wout)

    a_re, a_im, bb_re, bb_im = _discretize(ssm_lambda_re[0], ssm_lambda_im[0], ssm_log_dt[0], ssm_b_re[0], ssm_b_im[0])
    steps = _row_tile(N_META + x.shape[1]) // SUBLANES
    coef = _power_table(a_re, a_im, steps, reverse=False)
    coef_rev = _power_table(a_re, a_im, steps, reverse=True)
    bmat_re = _block_diag(jnp.swapaxes(bb_re, 1, 2), n_blk).astype(_MXU)
    bmat_im = _block_diag(jnp.swapaxes(bb_im, 1, 2), n_blk).astype(_MXU)
    cmat_re = _block_diag(jnp.swapaxes(ssm_c_re[0], 1, 2), n_blk).astype(_MXU)
    cmat_im = _block_diag(jnp.swapaxes(ssm_c_im[0], 1, 2), n_blk).astype(_MXU)
    wz1 = _block_diag(ssm_w_glu[0][:, :, :SSM_GROUP_CH], n_blk).astype(_MXU)
    wz2 = _block_diag(ssm_w_glu[0][:, :, SSM_GROUP_CH:], n_blk).astype(_MXU)
    pool_wm = pool_w[0].astype(_MXU)

    n2, u_s, u_p = _mix_in_fwd(h1, mix_pre_norm, w_in_f, "mix_in_fwd")
    (x_re, x_im, cat_s), (g_up2, g_down2) = _ssm_forward(
        u_s, coef, bmat_re, bmat_im, cmat_re, cmat_im, wz1, wz2, ssm_d, ssm_out_norm, "ssm_fwd",
        xchg=_Xchg([s_up2, s_down2], ["gather"] * 2))
    wgt2, wut2, wd2 = stacked(g_gate2), stacked(g_up2), stacked(g_down2)
    cat_p = _pool_fwd(u_p, pool_wm, pool_scale, pool_out_norm, "pool_fwd")
    wo_s, wo_p = w_out_f[:width], w_out_f[width:]
    mixed, h2 = _mix_out_fwd(cat_s, cat_p, h1, mix_post_norm, wo_s, wo_p, "mix_out_fwd")

    (a2, b2, f2, dh3, sq_err), _ = _ffn_fwd(
        h2, ffn2_pre_norm, ffn2_post_norm, wgt2, wut2, wd2, "ffn2_fwd", target=loss_target[0])

    g, slots = {}, {}
    (dh2, da2, db2, s2, df2, nf2, g["ffn2_pre_norm"], g["ffn2_post_norm"]), _ = _ffn_bwd(
        dh3, h2, f2, a2, b2, ffn2_pre_norm, ffn2_post_norm, wgt2, wut2, wd2, "ffn2_bwd")
    dgate2, _ = _wgrad(da2, nf2, "ffn2_dgate")
    dup2, _ = _wgrad(db2, nf2, "ffn2_dup")
    ddown2, _ = _wgrad(s2, df2, "ffn2_ddown")

    dmixed, dcat_s, dcat_p, g["mix_post_norm"] = _mix_out_bwd(dh2, mixed, mix_post_norm, wo_s, wo_p, "mix_out_bwd")
    dwout = jnp.concatenate([_wgrad(cat_s, dmixed, "dwout_s")[0], _wgrad(cat_p, dmixed, "dwout_p")[0]], axis=0)
    ((du_s, g["ssm_out_norm"], g["ssm_d"], dwz1, dwz2, dcm_re, dcm_im, dbm_re, dbm_im, acc_re, acc_im),
     (slots["ffn2_w_gate"], slots["ffn2_w_up"], slots["ffn2_w_down"])) = _ssm_backward(
        u_s, x_re, x_im, dcat_s, coef_rev, bmat_re, bmat_im, cmat_re, cmat_im, wz1, wz2, ssm_d, ssm_out_norm, "ssm_bwd",
        xchg=_Xchg([chunked(dgate2), chunked(dup2), chunked(ddown2)], ["scatter"] * 3))
    du_p, g["pool_out_norm"], g["pool_scale"], dpw = _pool_bwd(u_p, dcat_p, pool_wm, pool_scale, pool_out_norm, "pool_bwd")
    wi_s, wi_p = w_in_f[:, :width], w_in_f[:, width:]
    dh1, dproj, g["mix_pre_norm"] = _mix_in_bwd(du_s, du_p, h1, dh2, mix_pre_norm, wi_s, wi_p, "mix_in_bwd")
    dwin, _ = _wgrad(n2, dproj, "dwin")

    g["ssm_c_re"] = jnp.swapaxes(_block_diag_t(jnp.swapaxes(dcm_re, 1, 2), n_grp), 1, 2)[None]
    g["ssm_c_im"] = jnp.swapaxes(_block_diag_t(jnp.swapaxes(dcm_im, 1, 2), n_grp), 1, 2)[None]
    g["ssm_w_glu"] = jnp.concatenate([_block_diag_t(dwz1, n_grp), _block_diag_t(dwz2, n_grp)], axis=-1)[None]
    dbb_re = jnp.swapaxes(_block_diag_t(dbm_re, n_grp), 1, 2)
    dbb_im = jnp.swapaxes(_block_diag_t(dbm_im, n_grp), 1, 2)
    da_re, da_im = acc_re.reshape(a_re.shape), acc_im.reshape(a_re.shape)
    _, disc_vjp = jax.vjp(_discretize, ssm_lambda_re[0], ssm_lambda_im[0], ssm_log_dt[0], ssm_b_re[0], ssm_b_im[0])
    d_lre, d_lim, d_ldt, d_bre, d_bim = disc_vjp((da_re, da_im, dbb_re, dbb_im))
    g["ssm_lambda_re"], g["ssm_lambda_im"], g["ssm_log_dt"] = d_lre[None], d_lim[None], d_ldt[None]
    g["ssm_b_re"], g["ssm_b_im"] = d_bre[None], d_bim[None]
    g["pool_w"] = dpw[None]

    late = ["ffn1_pre_norm", "ffn1_post_norm"]
    early = [n for n in small if n not in late]
    early_vec = _pack_rows([g[n] for n in early] + [sq_err[:, :1]], 1024, SUBLANES)
    ((dx, da1, db1, s1, df1, nf1, g["ffn1_pre_norm"], g["ffn1_post_norm"], dmeta_part),
     (slots["w_out"], slots["w_in"], recv_early)) = _ffn_bwd(
        dh1, h0, f1, a1, b1, ffn1_pre_norm, ffn1_post_norm, wgt1, wut1, wd1, "ffn1_bwd",
        xchg=_Xchg([chunked(dwout), chunked(dwin), early_vec], ["scatter", "scatter", "gather"]), split_meta=True)
    late_vec = _pack_rows([g[n] for n in late] + [dmeta_part], 1024, SUBLANES)
    dgate1, (recv_late,) = _wgrad(da1, nf1, "ffn1_dgate", xchg=_Xchg([late_vec], ["gather"]))
    dup1, (slots["ffn1_w_gate"],) = _wgrad(db1, nf1, "ffn1_dup", xchg=_Xchg([chunked(dgate1)], ["scatter"]))
    ddown1, (slots["ffn1_w_up"],) = _wgrad(s1, df1, "ffn1_ddown", xchg=_Xchg([chunked(dup1)], ["scatter"]))
    (slots["ffn1_w_down"],) = _exchange([chunked(ddown1)], ["scatter"], "reduce_last")

    summed = _unpack(_sum_slots(recv_early, "sum_small_grads"), [g[n].shape for n in early] + [(1,)])
    for n, val in zip(early, summed):
        g[n] = val
    loss = (0.5 / d) * summed[-1][0]
    g[late[0]], g[late[1]], dmeta = _unpack(_sum_slots(recv_late, "sum_last_grads"), [(1, d), (1, d), (N_META, d)])
    g["meta_tokens"] = lax.dynamic_slice_in_dim(dmeta, _my_slot() * (d // N_DEV), d // N_DEV, axis=1)

    delta, new_m, new_v = {}, {}, {}
    for n in sharded:
        shape = args[n].shape
        two_d = (-1, shape[-1])
        w2, m2, v2 = args[n].reshape(two_d), args["m_" + n].reshape(two_d), args["v_" + n].reshape(two_d)
        if n == "meta_tokens":
            dl, mn, vn = _adamw(w2, g[n], m2, v2, "adamw_" + n)
        elif n.endswith("gate") or n.endswith("up"):
            gs, dl, mn, vn = [t.T for t in _adamw_slots(w2.T, slots[n], m2.T, v2.T, "adamw_" + n)]
        else:
            gs, dl, mn, vn = _adamw_slots(w2, slots[n], m2, v2, "adamw_" + n)
        if n != "meta_tokens":
            g[n] = gs[None]
        delta[n], new_m[n], new_v[n] = dl.reshape(shape), mn.reshape(shape), vn.reshape(shape)
    packs = [_pack_rows([src[n] if pre is None else args[pre + n] for n in small], 1024, SUBLANES)
             for src, pre in ((args, None), (g, None), (None, "m_"), (None, "v_"))]
    outs = _adamw(*packs, "adamw_small")
    shapes = [args[n].shape for n in small]
    for store, flat in zip((delta, new_m, new_v), outs):
        for n, val in zip(small, _unpack(flat, shapes)):
            store[n] = val

    grad_x = dx[None]
    return (loss, grad_x, *[g[n] for n in names], *[delta[n] for n in names], *[new_m[n] for n in names],
            *[new_v[n] for n in names])
```

```python
import functools
import math

import jax
import jax.numpy as jnp
import numpy as np
from jax import lax
from jax.experimental import pallas as pl
from jax.experimental.pallas import tpu as pltpu

F32 = jnp.float32
_MXU = jnp.bfloat16
_ACT = jnp.bfloat16
_WIRE = jnp.bfloat16

N_DEV = 8
N_META = 16
RMS_EPS = 1e-6
SSM_GROUP_CH = 16
LANES = 128
SUBLANES = 8
POOL_WINDOWS = (2, 4, 8, 16)
POOL_HALO = 16
ADAM_LR = 0.001
ADAM_B1 = 0.9
ADAM_B2 = 0.999
ADAM_EPS = 1e-08
ADAM_WD = 0.01
ADAM_STEP = 10
GELU_C0 = math.sqrt(2.0 / math.pi)
GELU_C1 = 0.044715
VMEM_LIMIT = 62 * 1024 * 1024

_NT = (((1,), (1,)), ((), ()))
_TN = (((0,), (0,)), ((), ()))


def _dot(a, b):
    return jnp.dot(a, b, preferred_element_type=F32)


def _dot_nt(a, b):
    return lax.dot_general(a, b, _NT, preferred_element_type=F32)


def _dot_tn(a, b):
    return lax.dot_general(a, b, _TN, preferred_element_type=F32)


def _rs(x):
    return lax.rsqrt(jnp.mean(x * x, axis=-1, keepdims=True) + RMS_EPS)


def _sigmoid(x):
    return 0.5 * jnp.tanh(0.5 * x) + 0.5


def _row_tile(n_rows, largest=432):
    for t in (432, 304, 48, 16):
        if t <= largest and n_rows % t == 0:
            return t
    raise ValueError(n_rows)


def _ff_chunk(d_ff):
    return d_ff // 2 if (d_ff // 2) % LANES == 0 else d_ff


def _params(n_axes=1):
    return pltpu.CompilerParams(dimension_semantics=("arbitrary",) * n_axes, vmem_limit_bytes=VMEM_LIMIT)


def _resident():
    return pl.BlockSpec(memory_space=pltpu.VMEM)


def _full(shape):
    nd = len(shape)
    return pl.BlockSpec(shape, lambda *_: (0,) * nd)


def _lane_blocks(n_blk, tm, tile_of=lambda i: i):
    return pl.BlockSpec((n_blk, tm, LANES), lambda i: (0, tile_of(i), 0))


PEER_ORDER = (1, 2, 4, 3, 5, 6, 7)


def _split(refs, counts):
    out, pos = [], 0
    for n in counts:
        out.append(refs[pos:pos + n])
        pos += n
    return out


def _peer(r):
    x, y, c = lax.axis_index("x"), lax.axis_index("y"), lax.axis_index("c")
    return (1 - x if r & 4 else x, 1 - y if r & 2 else y, 1 - c if r & 1 else c)


def _my_slot():
    return 4 * lax.axis_index("x") + 2 * lax.axis_index("y") + lax.axis_index("c")


class _Xchg:
    def __init__(self, srcs, kinds):
        self.srcs, self.kinds, self.n = list(srcs), list(kinds), len(srcs)
        self.out_shape = [jax.ShapeDtypeStruct((N_DEV,) + s.shape if k == "gather" else s.shape, s.dtype)
                          for s, k in zip(self.srcs, self.kinds)]
        self.specs = [pl.BlockSpec(memory_space=pl.ANY)] * self.n
        self.scratch = [pltpu.SemaphoreType.DMA((self.n * (N_DEV - 1),)), pltpu.SemaphoreType.DMA((self.n * (N_DEV - 1),)),
                        pltpu.SemaphoreType.DMA((self.n,))]

    def copies(self, src, dst, sems):
        send_sems, recv_sems, local_sems = sems
        me = _my_slot()
        out = []
        for a in range(self.n):
            mine = src[a] if self.kinds[a] == "gather" else src[a].at[me]
            out.append(pltpu.make_async_copy(mine, dst[a].at[me], local_sems.at[a]))
            for r in PEER_ORDER:
                px, py, pc = _peer(r)
                part = src[a] if self.kinds[a] == "gather" else src[a].at[4 * px + 2 * py + pc]
                k = a * (N_DEV - 1) + r - 1
                out.append(pltpu.make_async_remote_copy(
                    src_ref=part, dst_ref=dst[a].at[me], send_sem=send_sems.at[k], recv_sem=recv_sems.at[k],
                    device_id=(px, py, pc), device_id_type=pl.DeviceIdType.MESH))
        return out

    def start(self, src, dst, sems):
        for cp in self.copies(src, dst, sems):
            cp.start()

    def wait(self, src, dst, sems):
        for cp in self.copies(src, dst, sems):
            cp.wait()


class _NoXchg:
    n, srcs, out_shape, specs, scratch = 0, [], [], [], []

    def start(self, *_):
        pass

    wait = start


def _call(body, args, *, name, grid, out_shape, in_specs, out_specs, scratch_shapes=(), xchg=None):
    xc = xchg or _NoXchg()
    counts = (len(in_specs), xc.n, len(out_shape), xc.n, len(scratch_shapes), len(xc.scratch))

    def wrapped(*refs):
        ins, xsrc, outs, xdst, scr, sems = _split(refs, counts)
        ids = [pl.program_id(k) for k in range(len(grid))]
        if xc.n:
            @pl.when(functools.reduce(jnp.logical_and, [i == 0 for i in ids]))
            def _():
                xc.start(xsrc, xdst, sems)

        body(*ins, *outs, *scr)
        if xc.n:
            @pl.when(functools.reduce(jnp.logical_and, [i == g - 1 for i, g in zip(ids, grid)]))
            def _():
                xc.wait(xsrc, xdst, sems)

    res = pl.pallas_call(
        wrapped, name=name, grid=grid, out_shape=list(out_shape) + xc.out_shape,
        in_specs=list(in_specs) + xc.specs, out_specs=list(out_specs) + xc.specs,
        scratch_shapes=list(scratch_shapes) + xc.scratch, compiler_params=_params(len(grid)),
    )(*args, *xc.srcs)
    return res[:len(out_shape)], res[len(out_shape):]


def _exchange(srcs, kinds, name):
    xc = _Xchg(srcs, kinds)

    def body(*refs):
        src, dst, sems = _split(refs, (xc.n, xc.n, 3))
        xc.start(src, dst, sems)
        xc.wait(src, dst, sems)

    return pl.pallas_call(body, name=name, out_shape=xc.out_shape, in_specs=xc.specs, out_specs=xc.specs,
                          scratch_shapes=xc.scratch)(*srcs)


def _scatter_peers(src_ref, land_ref, send_sems, recv_sems):
    me = _my_slot()
    out = []
    for r in PEER_ORDER:
        px, py, pc = _peer(r)
        out.append(pltpu.make_async_remote_copy(
            src_ref=src_ref.at[4 * px + 2 * py + pc], dst_ref=land_ref.at[me], send_sem=send_sems.at[r - 1],
            recv_sem=recv_sems.at[r - 1], device_id=(px, py, pc), device_id_type=pl.DeviceIdType.MESH))
    return out


def _scatter_start(src, name):
    hbm, sem = pl.BlockSpec(memory_space=pltpu.HBM), pl.BlockSpec(memory_space=pltpu.SEMAPHORE)

    def body(src_ref, land_ref, send_sems, recv_sems, src_thru, land_thru, token):
        for cp in _scatter_peers(src_ref, land_ref, send_sems, recv_sems):
            cp.start()
        token[...] = jnp.zeros_like(token)

    return pl.pallas_call(
        body, name=name,
        out_shape=(pltpu.SemaphoreType.DMA((N_DEV - 1,)), pltpu.SemaphoreType.DMA((N_DEV - 1,)),
                   pltpu.HBM(src.shape, src.dtype), pltpu.HBM(src.shape, src.dtype),
                   jax.ShapeDtypeStruct((SUBLANES, LANES), F32)),
        in_specs=(hbm, hbm), out_specs=(sem, sem, hbm, hbm, pl.BlockSpec(memory_space=pltpu.VMEM)),
        input_output_aliases={0: 2, 1: 3},
        compiler_params=pltpu.CompilerParams(has_side_effects=pltpu.SideEffectType.DATAFLOW_SIDE_EFFECTING),
    )(pltpu.with_memory_space_constraint(src, pltpu.HBM),
      pltpu.with_memory_space_constraint(lax.empty(src.shape, src.dtype), pltpu.HBM))


def _scatter_wait(send_sems, recv_sems, src_thru, land_thru, after, name):
    hbm, sem = pl.BlockSpec(memory_space=pltpu.HBM), pl.BlockSpec(memory_space=pltpu.SEMAPHORE)

    def body(src_ref, land_ref, send_sems, recv_sems, *rest):
        for cp in _scatter_peers(src_ref, land_ref, send_sems, recv_sems):
            cp.wait_send()
            cp.wait_recv()

    return pl.pallas_call(
        body, name=name, out_shape=(pltpu.HBM(src_thru.shape, src_thru.dtype), pltpu.HBM(land_thru.shape, land_thru.dtype)),
        in_specs=(hbm, hbm, sem, sem) + (pl.BlockSpec(memory_space=pl.ANY),) * len(after), out_specs=(hbm, hbm),
        input_output_aliases={0: 0, 1: 1},
        compiler_params=pltpu.CompilerParams(has_side_effects=pltpu.SideEffectType.DATAFLOW_SIDE_EFFECTING),
    )(src_thru, land_thru, send_sems, recv_sems, *after)


def _gather_two_level(srcs, name):
    n = len(srcs)
    out_shape = [jax.ShapeDtypeStruct((N_DEV,) + s.shape, s.dtype) for s in srcs]
    chips = (2, 4, 6)

    def body(*refs):
        src, dst, (send_sems, recv_sems, local_sems) = _split(refs, (n, n, 3))
        x, y, c = lax.axis_index("x"), lax.axis_index("y"), lax.axis_index("c")
        me = 4 * x + 2 * y + c
        sibling = (x, y, 1 - c)

        def copy(a, k, slot, to, from_src=False):
            return pltpu.make_async_remote_copy(
                src_ref=src[a] if from_src else dst[a].at[slot], dst_ref=dst[a].at[slot],
                send_sem=send_sems.at[a * 7 + k], recv_sem=recv_sems.at[a * 7 + k],
                device_id=to, device_id_type=pl.DeviceIdType.MESH)

        def slot_of(r, core):
            px, py, _ = _peer(r)
            return 4 * px + 2 * py + core

        local = [pltpu.make_async_copy(src[a], dst[a].at[me], local_sems.at[a]) for a in range(n)]
        sent = []
        for a in range(n):
            local[a].start()
            sent.append(copy(a, 0, me, sibling, from_src=True))
            sent += [copy(a, 1 + j, me, _peer(r), from_src=True) for j, r in enumerate(chips)]
        for cp in sent:
            cp.start()
        for j, r in enumerate(chips):
            for a in range(n):
                copy(a, 1 + j, slot_of(r, c), _peer(r)).wait_recv()
                cp = copy(a, 4 + j, slot_of(r, c), sibling)
                cp.start()
                sent.append(cp)
        for a in range(n):
            copy(a, 0, slot_of(0, 1 - c), sibling).wait_recv()
            for j, r in enumerate(chips):
                copy(a, 4 + j, slot_of(r, 1 - c), sibling).wait_recv()
        for cp in local:
            cp.wait()
        for cp in sent:
            cp.wait_send()

    any_spec = pl.BlockSpec(memory_space=pl.ANY)
    return pl.pallas_call(
        body, name=name, out_shape=out_shape, in_specs=[any_spec] * n, out_specs=[any_spec] * n,
        scratch_shapes=[pltpu.SemaphoreType.DMA((n * 7,)), pltpu.SemaphoreType.DMA((n * 7,)), pltpu.SemaphoreType.DMA((n,))],
    )(*srcs)


def _to_wire(mats, name):
    def body(*refs):
        for src, dst in zip(refs[:len(mats)], refs[len(mats):]):
            dst[...] = src[...].astype(_WIRE)

    return pl.pallas_call(
        body, name=name, grid=(1,), out_shape=[jax.ShapeDtypeStruct(m.shape, _WIRE) for m in mats],
        in_specs=[_full(m.shape) for m in mats], out_specs=[_full(m.shape) for m in mats], compiler_params=_params(),
    )(*mats)


def _sum_slots(r, name):
    _, rows, cols = r.shape
    blk = rows
    for cand in (rows, 592, 512, 256, 128, 64, 32, 16):
        if rows % cand == 0 and N_DEV * cand * cols * r.dtype.itemsize <= 8 * 1024 * 1024:
            blk = cand
            break

    def body(r_ref, o_ref):
        acc = r_ref[0].astype(F32)
        for d in range(1, N_DEV):
            acc = acc + r_ref[d].astype(F32)
        o_ref[...] = acc

    return pl.pallas_call(
        body, name=name, grid=(rows // blk,), out_shape=jax.ShapeDtypeStruct((rows, cols), F32),
        in_specs=[pl.BlockSpec((N_DEV, blk, cols), lambda i: (0, i, 0))],
        out_specs=pl.BlockSpec((blk, cols), lambda i: (i, 0)), compiler_params=_params(),
    )(r)


def _token_tile_copy(tokens_ref, buf, sems, i, tm, write=False):
    if isinstance(i, int) and i == 0:
        far, near = tokens_ref.at[pl.ds(0, tm - N_META)], buf.at[0, pl.ds(N_META, tm - N_META)]
    else:
        start = i * tm - N_META if isinstance(i, int) else pl.multiple_of(i * tm - N_META, SUBLANES)
        far, near = tokens_ref.at[pl.ds(start, tm)], buf.at[i % 2]
    return pltpu.make_async_copy(near, far, sems.at[i % 2]) if write else pltpu.make_async_copy(far, near, sems.at[i % 2])


def _fetch_token_tile(tokens_ref, buf, sems, i, n_t, tm):
    @pl.when(i == 0)
    def _():
        _token_tile_copy(tokens_ref, buf, sems, 0, tm).start()

    @pl.when(i + 1 < n_t)
    def _():
        _token_tile_copy(tokens_ref, buf, sems, i + 1, tm).start()

    @pl.when(i == 0)
    def _():
        _token_tile_copy(tokens_ref, buf, sems, 0, tm).wait()

    @pl.when(i > 0)
    def _():
        _token_tile_copy(tokens_ref, buf, sems, i, tm).wait()


def _ffn_fwd(h, g_pre, g_post, wgt, wut, wd, name, xchg=None, *, meta=None, tokens=None, target=None):
    first = h is None
    d = wd.shape[1]
    n_rows = N_META + tokens.shape[0] if first else h.shape[0]
    d_ff = wd.shape[0]
    tm, fc = _row_tile(n_rows), _ff_chunk(d_ff)
    n_t, n_c = n_rows // tm, d_ff // fc

    def body(src_ref, side_ref, gpre_ref, gpost_ref, wgt_ref, wut_ref, wd_ref, a_ref, b_ref, f_ref, o1_ref, o2_ref,
             n_scr, acc, buf, sems):
        i, c = pl.program_id(0), pl.program_id(1)
        slot = i % 2

        @pl.when(c == 0)
        def _():
            if first:
                _fetch_token_tile(side_ref, buf, sems, i, n_t, tm)

                @pl.when(i == 0)
                def _():
                    buf[0, 0:N_META, :] = src_ref[...]

                hv = buf[slot]
                o2_ref[...] = hv
            else:
                _fetch_token_tile(side_ref, buf, sems, i, n_t, tm)

                @pl.when(i == 0)
                def _():
                    buf[0, 0:N_META, :] = jnp.zeros((N_META, d), F32)
                    o2_ref[...] = jnp.zeros_like(o2_ref)

                hv = src_ref[...]
            n_scr[...] = (hv * _rs(hv) * gpre_ref[...]).astype(_MXU)
            acc[...] = jnp.zeros_like(acc)

        rows = pl.ds(pl.multiple_of(c * fc, fc), fc)
        nv = n_scr[...]
        a = _dot_nt(nv, wgt_ref[rows, :])
        b = _dot_nt(nv, wut_ref[rows, :])
        a_ref[...] = a.astype(_ACT)
        b_ref[...] = b.astype(_ACT)
        s = a * _sigmoid(a) * b
        acc[...] += _dot(s.astype(_MXU), wd_ref[rows, :])

        @pl.when(c == n_c - 1)
        def _():
            f = acc[...]
            f_ref[...] = f
            step = 0.5 * (f * _rs(f) * gpost_ref[...])
            if first:
                o1_ref[...] = buf[slot] + step
            else:
                row = i * tm + lax.broadcasted_iota(jnp.int32, (tm, 1), 0)
                err = jnp.where(row >= N_META, (src_ref[...] + step) - buf[slot], 0.0)
                o1_ref[...] = err / d
                o2_ref[...] += jnp.sum(jnp.sum(err * err, axis=0, keepdims=True), axis=1, keepdims=True)

    row = pl.BlockSpec((tm, d), lambda i, c: (i, 0))
    chunk = pl.BlockSpec((tm, fc), lambda i, c: (i, c))
    hbm = pl.BlockSpec(memory_space=pl.ANY)
    if first:
        operands, specs = (meta, tokens), [_full(meta.shape), hbm]
        last_shape, last_spec = jax.ShapeDtypeStruct((n_rows, d), F32), row
    else:
        operands, specs = (h, target), [row, hbm]
        last_shape, last_spec = jax.ShapeDtypeStruct((1, LANES), F32), pl.BlockSpec((1, LANES), lambda i, c: (0, 0))
    return _call(
        body, (*operands, g_pre, g_post, wgt, wut, wd), name=name, grid=(n_t, n_c),
        out_shape=[jax.ShapeDtypeStruct((n_rows, d_ff), _ACT), jax.ShapeDtypeStruct((n_rows, d_ff), _ACT),
                   jax.ShapeDtypeStruct((n_rows, d), F32), jax.ShapeDtypeStruct((n_rows, d), F32), last_shape],
        in_specs=specs + [_full((1, d)), _full((1, d)), _resident(), _resident(), _resident()],
        out_specs=[chunk, chunk, row, row, last_spec],
        scratch_shapes=[pltpu.VMEM((tm, d), _MXU), pltpu.VMEM((tm, d), F32), pltpu.VMEM((2, tm, d), F32),
                        pltpu.SemaphoreType.DMA((2,))], xchg=xchg)


def _ffn_bwd(dho, h, f, a, b, g_pre, g_post, wgt, wut, wd, name, xchg=None, *, split_meta=False):
    n_rows, d = h.shape
    d_ff = wd.shape[0]
    tm, fc = _row_tile(n_rows), _ff_chunk(d_ff)
    n_t, n_c = n_rows // tm, d_ff // fc
    assert n_t >= 2

    def body(dho_ref, h_ref, f_ref, a_ref, b_ref, gpre_ref, gpost_ref, wgt_ref, wut_ref, wd_ref,
             dh_ref, da_ref, db_ref, s_ref, df_ref, n_ref, dgpre_ref, dgpost_ref, *rest):
        dn_acc = rest[-1]
        i, c = pl.program_id(0), pl.program_id(1)

        @pl.when((i == 0) & (c == 0))
        def _():
            dgpre_ref[...] = jnp.zeros_like(dgpre_ref)
            dgpost_ref[...] = jnp.zeros_like(dgpost_ref)

        @pl.when(c == 0)
        def _():
            fv = f_ref[...]
            rf = _rs(fv)
            fhat = fv * rf
            dy = 0.5 * dho_ref[...]
            dgpost_ref[...] += jnp.sum(dy * fhat, axis=0, keepdims=True)
            dfhat = dy * gpost_ref[...]
            df = rf * (dfhat - fhat * jnp.mean(dfhat * fhat, axis=-1, keepdims=True))
            df_ref[...] = df.astype(_MXU)
            hv = h_ref[...]
            n_ref[...] = (hv * _rs(hv) * gpre_ref[...]).astype(_MXU)
            dn_acc[...] = jnp.zeros_like(dn_acc)

        rows = pl.ds(pl.multiple_of(c * fc, fc), fc)
        ds = _dot_nt(df_ref[...], wd_ref[rows, :])
        av = a_ref[...].astype(F32)
        bv = b_ref[...].astype(F32)
        sg = _sigmoid(av)
        si = av * sg
        da = (ds * bv * (sg * (1.0 + av * (1.0 - sg)))).astype(_MXU)
        db = (ds * si).astype(_MXU)
        da_ref[...] = da
        db_ref[...] = db
        s_ref[...] = (si * bv).astype(_MXU)
        dn_acc[...] += _dot(da, wgt_ref[rows, :]) + _dot(db, wut_ref[rows, :])

        @pl.when(c == n_c - 1)
        def _():
            dn = dn_acc[...]
            hv = h_ref[...]
            r = _rs(hv)
            hhat = hv * r
            dgpre_ref[...] += jnp.sum(dn * hhat, axis=0, keepdims=True)
            dhh = dn * gpre_ref[...]
            dh = dho_ref[...] + r * (dhh - hhat * jnp.mean(dhh * hhat, axis=-1, keepdims=True))
            if not split_meta:
                dh_ref[...] = dh
                return
            dmeta_ref, buf, sems = rest[0], rest[1], rest[2]

            def out_copy(k):
                return _token_tile_copy(dh_ref, buf, sems, k, tm, write=True)

            @pl.when(i == 2)
            def _():
                out_copy(0).wait()

            @pl.when(i > 2)
            def _():
                out_copy(i - 2).wait()

            buf[i % 2] = dh

            @pl.when(i == 0)
            def _():
                dmeta_ref[...] = buf[0, 0:N_META, :]
                out_copy(0).start()

            @pl.when(i > 0)
            def _():
                out_copy(i).start()

            @pl.when(i == n_t - 1)
            def _():
                out_copy(n_t - 2).wait()
                out_copy(n_t - 1).wait()

    row = pl.BlockSpec((tm, d), lambda i, c: (i, 0))
    chunk = pl.BlockSpec((tm, fc), lambda i, c: (i, c))
    vec = pl.BlockSpec((1, d), lambda i, c: (0, 0))
    shapes = [jax.ShapeDtypeStruct((n_rows, d_ff), _MXU)] * 3 + [jax.ShapeDtypeStruct((n_rows, d), _MXU)] * 2 \
        + [jax.ShapeDtypeStruct((1, d), F32)] * 2
    specs = [chunk, chunk, chunk, row, row, vec, vec]
    scratch = [pltpu.VMEM((tm, d), F32)]
    if split_meta:
        shapes = [jax.ShapeDtypeStruct((n_rows - N_META, d), F32)] + shapes + [jax.ShapeDtypeStruct((N_META, d), F32)]
        specs = [pl.BlockSpec(memory_space=pl.ANY)] + specs + [pl.BlockSpec((N_META, d), lambda i, c: (0, 0))]
        scratch = [pltpu.VMEM((2, tm, d), F32), pltpu.SemaphoreType.DMA((2,))] + scratch
    else:
        shapes, specs = [jax.ShapeDtypeStruct((n_rows, d), F32)] + shapes, [row] + specs
    return _call(
        body, (dho, h, f, a, b, g_pre, g_post, wgt, wut, wd), name=name, grid=(n_t, n_c), out_shape=shapes,
        in_specs=[row, row, row, chunk, chunk, vec, vec, _resident(), _resident(), _resident()], out_specs=specs,
        scratch_shapes=scratch, xchg=xchg)


def _wgrad(xm, ym, name, xchg=None):
    n_rows, a_dim = xm.shape
    b_dim = ym.shape[1]
    tk = n_rows
    for cand in (2736, 1296, 432, 48, 16):
        if n_rows % cand == 0:
            tk = cand
            break
    ta = a_dim
    for cand in (1408, 1024, 512):
        if a_dim % cand == 0:
            ta = cand
            break

    n_k = n_rows // tk

    def body(x_ref, y_ref, o_ref, acc):
        k = pl.program_id(1)

        @pl.when(k == 0)
        def _():
            acc[...] = jnp.zeros_like(acc)

        acc[...] += _dot_tn(x_ref[...], y_ref[...])

        @pl.when(k == n_k - 1)
        def _():
            o_ref[...] = acc[...].astype(o_ref.dtype)

    (out,), extra = _call(
        body, (xm, ym), name=name, grid=(a_dim // ta, n_k), out_shape=[jax.ShapeDtypeStruct((a_dim, b_dim), _WIRE)],
        in_specs=[pl.BlockSpec((tk, ta), lambda j, k: (k, j)), pl.BlockSpec((tk, b_dim), lambda j, k: (k, 0))],
        out_specs=[pl.BlockSpec((ta, b_dim), lambda j, k: (j, 0))], scratch_shapes=[pltpu.VMEM((ta, b_dim), F32)],
        xchg=xchg)
    return out, extra


def _mix_in_fwd(h, g, w_in, name):
    n_rows, d = h.shape
    tm = _row_tile(n_rows)
    width = w_in.shape[1] // 2
    n_blk = width // LANES

    def body(h_ref, g_ref, w_ref, n_ref, us_ref, up_ref):
        hv = h_ref[...]
        nv = (hv * _rs(hv) * g_ref[...]).astype(_MXU)
        n_ref[...] = nv
        p = _dot(nv, w_ref[...])
        for k in range(n_blk):
            us_ref[k] = p[:, k * LANES:(k + 1) * LANES]
        up_ref[...] = p[:, width:]

    row = pl.BlockSpec((tm, d), lambda i: (i, 0))
    half = pl.BlockSpec((tm, width), lambda i: (i, 0))
    return pl.pallas_call(
        body, name=name, grid=(n_rows // tm,),
        out_shape=[jax.ShapeDtypeStruct((n_rows, d), _MXU), jax.ShapeDtypeStruct((n_blk, n_rows, LANES), F32),
                   jax.ShapeDtypeStruct((n_rows, width), F32)],
        in_specs=[row, _full((1, d)), _resident()], out_specs=[row, _lane_blocks(n_blk, tm), half],
        compiler_params=_params(),
    )(h, g, w_in)


def _gelu_parts(y):
    th = jnp.tanh(GELU_C0 * (y + GELU_C1 * (y * y * y)))
    return 0.5 * (1.0 + th), th


SCAN_LANES = 512


def _regroup(src_ref, dst_ref, seg):
    for k in range(src_ref.shape[0]):
        for j in range(seg):
            dst_ref[j * SUBLANES:(j + 1) * SUBLANES, k * LANES:(k + 1) * LANES] = src_ref[k, pl.ds(j, SUBLANES, stride=seg), :]


def _ungroup(src_ref, dst_ref, seg):
    for k in range(dst_ref.shape[0]):
        for j in range(seg):
            dst_ref[k, pl.ds(j, SUBLANES, stride=seg), :] = src_ref[j * SUBLANES:(j + 1) * SUBLANES, k * LANES:(k + 1) * LANES]


def _rows_to_sublanes(rows):
    rid = lax.broadcasted_iota(jnp.int32, (SUBLANES, rows[0].shape[1]), 0)
    out = jnp.broadcast_to(rows[0], rid.shape)
    for s in range(1, SUBLANES):
        out = jnp.where(rid == s, rows[s], out)
    return out


def _segment_scan(re_ref, im_ref, tab_ref, car_re, car_im, seg, reverse, x_refs=None):
    n_all = re_ref.shape[1]
    first = seg - 1 if reverse else 0
    sums = []
    for cb in range(n_all // SCAN_LANES):
        cols = pl.ds(cb * SCAN_LANES, SCAN_LANES)
        ar, ai = tab_ref[0, first, :, cols], tab_ref[1, first, :, cols]

        def local(t, carry, cols=cols, ar=ar, ai=ai):
            r0 = pl.multiple_of((seg - 1 - t if reverse else t) * SUBLANES, SUBLANES)
            xr, xi = carry
            nr = ar * xr - ai * xi + re_ref[pl.ds(r0, SUBLANES), cols]
            ni = ar * xi + ai * xr + im_ref[pl.ds(r0, SUBLANES), cols]
            re_ref[pl.ds(r0, SUBLANES), cols] = nr
            im_ref[pl.ds(r0, SUBLANES), cols] = ni
            return nr, ni

        zero = jnp.zeros((SUBLANES, SCAN_LANES), F32)
        fr, fi = lax.fori_loop(0, seg, local, (zero, zero))

        last = 0 if reverse else seg - 1
        sr, si = tab_ref[0, last, 0:1, cols], tab_ref[1, last, 0:1, cols]
        cr, ci = car_re[:, cols], car_im[:, cols]
        rows_r, rows_i = [None] * SUBLANES, [None] * SUBLANES
        for s in (range(SUBLANES - 1, -1, -1) if reverse else range(SUBLANES)):
            rows_r[s], rows_i[s] = cr, ci
            cr, ci = sr * cr - si * ci + fr[s:s + 1], sr * ci + si * cr + fi[s:s + 1]
        car_re[:, cols] = cr
        car_im[:, cols] = ci
        cmr, cmi = _rows_to_sublanes(rows_r), _rows_to_sublanes(rows_i)

        def fix(t, carry, cols=cols, cmr=cmr, cmi=cmi):
            j = seg - 1 - t if reverse else t
            r0 = pl.multiple_of(j * SUBLANES, SUBLANES)
            pr, pi = tab_ref[0, j, :, cols], tab_ref[1, j, :, cols]
            gr = re_ref[pl.ds(r0, SUBLANES), cols] + (pr * cmr - pi * cmi)
            gi = im_ref[pl.ds(r0, SUBLANES), cols] + (pr * cmi + pi * cmr)
            re_ref[pl.ds(r0, SUBLANES), cols] = gr
            im_ref[pl.ds(r0, SUBLANES), cols] = gi
            if x_refs is None:
                return carry
            nxr, nxi, accr, acci = carry
            xr, xi = x_refs[0][pl.ds(r0, SUBLANES), cols], x_refs[1][pl.ds(r0, SUBLANES), cols]
            return gr, gi, accr + (xr * nxr + xi * nxi), acci + (xr * nxi - xi * nxr)

        if x_refs is None:
            lax.fori_loop(0, seg, fix, 0)
        else:
            fin = lax.fori_loop(0, seg, fix, (cmr, cmi, zero, zero))
            sums.append((jnp.sum(fin[2], axis=0, keepdims=True), jnp.sum(fin[3], axis=0, keepdims=True)))
    return sums


def _ssm_forward(u, table, b_re, b_im, c_re, c_im, wz1, wz2, d_skip, g_out, name, xchg=None):
    n_rows = u.shape[1]
    n_blk, _, n_state = b_re.shape
    width, n_all = n_blk * LANES, n_blk * n_state
    tm = _row_tile(n_rows)
    seg = tm // SUBLANES

    def body(u_ref, tab_ref, bre_ref, bim_ref, cre_ref, cim_ref, wz1_ref, wz2_ref, d_ref, g_ref,
             xre_ref, xim_ref, o_ref, car_re, car_im, ug, out_scr, blocks):
        @pl.when(pl.program_id(0) == 0)
        def _():
            car_re[...] = jnp.zeros_like(car_re)
            car_im[...] = jnp.zeros_like(car_im)

        _regroup(u_ref, ug, seg)
        ub = ug[...].astype(_MXU)
        for j in range(n_blk):
            uj = ub[:, j * LANES:(j + 1) * LANES]
            xre_ref[:, j * n_state:(j + 1) * n_state] = _dot(uj, bre_ref[j])
            xim_ref[:, j * n_state:(j + 1) * n_state] = _dot(uj, bim_ref[j])
        _segment_scan(xre_ref, xim_ref, tab_ref, car_re, car_im, seg, reverse=False)

        ssq = None
        for j in range(n_blk):
            sl = slice(j * LANES, (j + 1) * LANES)
            st = slice(j * n_state, (j + 1) * n_state)
            yc = _dot(xre_ref[:, st].astype(_MXU), cre_ref[j]) - _dot(xim_ref[:, st].astype(_MXU), cim_ref[j])
            y = yc + d_ref[:, sl] * ug[:, sl]
            cdf, _ = _gelu_parts(y)
            gy = (y * cdf).astype(_MXU)
            out = _dot(gy, wz1_ref[j]) * _sigmoid(_dot(gy, wz2_ref[j]))
            out_scr[:, sl] = out
            part = jnp.sum(out * out, axis=-1, keepdims=True)
            ssq = part if ssq is None else ssq + part
        r = lax.rsqrt(ssq / width + RMS_EPS)
        out_scr[...] = out_scr[...] * r * g_ref[...]
        _ungroup(out_scr, blocks, seg)
        for k in range(n_blk):
            o_ref[:, k * LANES:(k + 1) * LANES] = blocks[k].astype(_MXU)

    half = pl.BlockSpec((tm, width), lambda i: (i, 0))
    state = pl.BlockSpec((tm, n_all), lambda i: (i, 0))
    return _call(
        body, (u, table, b_re, b_im, c_re, c_im, wz1, wz2, d_skip, g_out), name=name, grid=(n_rows // tm,),
        out_shape=[jax.ShapeDtypeStruct((n_rows, n_all), F32)] * 2 + [jax.ShapeDtypeStruct((n_rows, width), _MXU)],
        in_specs=[_lane_blocks(n_blk, tm), _resident(), _full(b_re.shape), _full(b_im.shape), _full(c_re.shape),
                  _full(c_im.shape), _full(wz1.shape), _full(wz2.shape), _full((1, width)), _full((1, width))],
        out_specs=[state, state, half],
        scratch_shapes=[pltpu.VMEM((1, n_all), F32)] * 2 + [pltpu.VMEM((tm, width), F32)] * 2
        + [pltpu.VMEM((n_blk, tm, LANES), F32)], xchg=xchg)


def _ssm_backward(u, x_re, x_im, dcat, table, b_re, b_im, c_re, c_im, wz1, wz2, d_skip, g_out, name, xchg=None):
    n_rows = u.shape[1]
    n_blk, _, n_state = b_re.shape
    width, n_all = n_blk * LANES, n_blk * n_state
    tm = _row_tile(n_rows)
    n_t, seg = n_rows // tm, tm // SUBLANES

    def body(u_ref, xre_ref, xim_ref, dc_ref, tab_ref, bre_ref, bim_ref, cre_ref, cim_ref, wz1_ref, wz2_ref, d_ref, g_ref,
             du_ref, dg_ref, dd_ref, dwz1_ref, dwz2_ref, dcre_ref, dcim_ref, dbre_ref, dbim_ref, are_ref, aim_ref,
             car_re, car_im, gre, gim, ug, y_s, z1_s, sg_s, out_s, gy_s):
        @pl.when(pl.program_id(0) == 0)
        def _():
            for ref in (dg_ref, dd_ref, dwz1_ref, dwz2_ref, dcre_ref, dcim_ref, dbre_ref, dbim_ref, are_ref, aim_ref,
                        car_re, car_im):
                ref[...] = jnp.zeros_like(ref)

        _regroup(u_ref, ug, seg)
        ssq = None
        for j in range(n_blk):
            sl = slice(j * LANES, (j + 1) * LANES)
            st = slice(j * n_state, (j + 1) * n_state)
            yc = _dot(xre_ref[:, st].astype(_MXU), cre_ref[j]) - _dot(xim_ref[:, st].astype(_MXU), cim_ref[j])
            y = yc + d_ref[:, sl] * ug[:, sl]
            cdf, _ = _gelu_parts(y)
            gy = (y * cdf).astype(_MXU)
            z1 = _dot(gy, wz1_ref[j])
            sg = _sigmoid(_dot(gy, wz2_ref[j]))
            out = z1 * sg
            y_s[:, sl], z1_s[:, sl], sg_s[:, sl], out_s[:, sl], gy_s[:, sl] = y, z1, sg, out, gy
            part = jnp.sum(out * out, axis=-1, keepdims=True)
            ssq = part if ssq is None else ssq + part
        r = lax.rsqrt(ssq / width + RMS_EPS)
        ohat = out_s[...] * r
        _regroup(dc_ref, out_s, seg)
        dcv = out_s[...]
        dg_ref[...] += jnp.sum(dcv * ohat, axis=0, keepdims=True)
        doh = dcv * g_ref[...]
        out_s[...] = r * (doh - ohat * (jnp.sum(doh * ohat, axis=-1, keepdims=True) / width))

        for j in range(n_blk):
            sl = slice(j * LANES, (j + 1) * LANES)
            st = slice(j * n_state, (j + 1) * n_state)
            dout, sg, z1, y = out_s[:, sl], sg_s[:, sl], z1_s[:, sl], y_s[:, sl]
            dz1 = (dout * sg).astype(_MXU)
            dz2 = (dout * z1 * sg * (1.0 - sg)).astype(_MXU)
            gy = gy_s[:, sl]
            dwz1_ref[j] += _dot_tn(gy, dz1)
            dwz2_ref[j] += _dot_tn(gy, dz2)
            dgy = _dot_nt(dz1, wz1_ref[j]) + _dot_nt(dz2, wz2_ref[j])
            cdf, th = _gelu_parts(y)
            dy = dgy * (cdf + y * (0.5 * (1.0 - th * th) * GELU_C0 * (1.0 + 3.0 * GELU_C1 * (y * y))))
            uj = ug[:, sl]
            dd_ref[:, sl] += jnp.sum(dy * uj, axis=0, keepdims=True)
            z1_s[:, sl] = d_ref[:, sl] * dy
            dyb = dy.astype(_MXU)
            dcre_ref[j] += _dot_tn(dyb, xre_ref[:, st].astype(_MXU))
            dcim_ref[j] -= _dot_tn(dyb, xim_ref[:, st].astype(_MXU))
            gre[:, st] = _dot_nt(dyb, cre_ref[j])
            gim[:, st] = -_dot_nt(dyb, cim_ref[j])

        sums = _segment_scan(gre, gim, tab_ref, car_re, car_im, seg, reverse=True, x_refs=(xre_ref, xim_ref))
        for cb, (sum_re, sum_im) in enumerate(sums):
            cols = pl.ds(cb * SCAN_LANES, SCAN_LANES)
            are_ref[:, cols] += sum_re
            aim_ref[:, cols] += sum_im

        for j in range(n_blk):
            sl = slice(j * LANES, (j + 1) * LANES)
            st = slice(j * n_state, (j + 1) * n_state)
            ujb = ug[:, sl].astype(_MXU)
            grb, gib = gre[:, st].astype(_MXU), gim[:, st].astype(_MXU)
            dbre_ref[j] += _dot_tn(ujb, grb)
            dbim_ref[j] += _dot_tn(ujb, gib)
            z1_s[:, sl] += _dot_nt(grb, bre_ref[j]) + _dot_nt(gib, bim_ref[j])
        _ungroup(z1_s, du_ref, seg)

    half = _lane_blocks(n_blk, tm, lambda i: n_t - 1 - i)
    state = pl.BlockSpec((tm, n_all), lambda i: (n_t - 1 - i, 0))
    small = [(1, width), (1, width), wz1.shape, wz2.shape, (n_blk, LANES, n_state), (n_blk, LANES, n_state),
             (n_blk, LANES, n_state), (n_blk, LANES, n_state), (1, n_all), (1, n_all)]
    return _call(
        body, (u, x_re, x_im, dcat, table, b_re, b_im, c_re, c_im, wz1, wz2, d_skip, g_out), name=name, grid=(n_t,),
        out_shape=[jax.ShapeDtypeStruct((n_blk, n_rows, LANES), F32)] + [jax.ShapeDtypeStruct(s, F32) for s in small],
        in_specs=[half, state, state, half, _resident(), _full(b_re.shape), _full(b_im.shape), _full(c_re.shape),
                  _full(c_im.shape), _full(wz1.shape), _full(wz2.shape), _full((1, width)), _full((1, width))],
        out_specs=[half] + [_full(s) for s in small],
        scratch_shapes=[pltpu.VMEM((1, n_all), F32)] * 2 + [pltpu.VMEM((tm, n_all), F32)] * 2
        + [pltpu.VMEM((tm, width), F32)] * 5 + [pltpu.VMEM((tm, width), _MXU)], xchg=xchg)


def _pool_counts(tile, tm, window):
    t = tile * tm + lax.broadcasted_iota(jnp.int32, (tm, 1), 0)
    return jnp.minimum(t + 1, window).astype(F32)


def _pool_fwd(proj, pool_w, scale, g_out, name):
    n_rows = proj.shape[0]
    n_grp, grp, _ = pool_w.shape
    width = n_grp * grp
    tm = _row_tile(n_rows)

    def body(u_ref, pw_ref, sc_ref, g_ref, o_ref, ext, y_s):
        i = pl.program_id(0)

        @pl.when(i == 0)
        def _():
            ext[0:POOL_HALO, :] = jnp.zeros((POOL_HALO, width), F32)

        ext[POOL_HALO:, :] = u_ref[...]
        ssq = None
        for gi, w in enumerate(POOL_WINDOWS):
            sl = slice(gi * grp, (gi + 1) * grp)
            tot = ext[POOL_HALO:, sl]
            for k in range(1, w):
                tot = tot + ext[POOL_HALO - k:POOL_HALO - k + tm, sl]
            pooled = tot / _pool_counts(i, tm, w) - u_ref[:, sl]
            y = _dot(pooled.astype(_MXU), pw_ref[gi]) * sc_ref[:, sl]
            y_s[:, sl] = y
            part = jnp.sum(y * y, axis=-1, keepdims=True)
            ssq = part if ssq is None else ssq + part
        r = lax.rsqrt(ssq / width + RMS_EPS)
        o_ref[...] = (y_s[...] * r * g_ref[...]).astype(_MXU)
        ext[0:POOL_HALO, :] = u_ref[tm - POOL_HALO:, :]

    half_in = pl.BlockSpec((tm, width), lambda i: (i, 0))
    half = pl.BlockSpec((tm, width), lambda i: (i, 0))
    return pl.pallas_call(
        body, name=name, grid=(n_rows // tm,), out_shape=jax.ShapeDtypeStruct((n_rows, width), _MXU),
        in_specs=[half_in, _full(pool_w.shape), _full((1, width)), _full((1, width))], out_specs=half,
        scratch_shapes=[pltpu.VMEM((tm + POOL_HALO, width), F32), pltpu.VMEM((tm, width), F32)],
        compiler_params=_params(),
    )(proj, pool_w, scale, g_out)


def _pool_bwd(proj, dcat, pool_w, scale, g_out, name):
    n_rows = proj.shape[0]
    n_grp, grp, _ = pool_w.shape
    width = n_grp * grp
    tm = _row_tile(n_rows)
    n_t = n_rows // tm
    halo_blocks = tm // POOL_HALO

    def body(u_ref, up_ref, dc_ref, pw_ref, sc_ref, g_ref, du_ref, dg_ref, dsc_ref, dpw_ref, ext, qext, y_s, pl_s):
        i = pl.program_id(0)
        tile = n_t - 1 - i

        @pl.when(i == 0)
        def _():
            for ref in (dg_ref, dsc_ref, dpw_ref):
                ref[...] = jnp.zeros_like(ref)
            qext[tm:, :] = jnp.zeros((POOL_HALO, width), F32)

        ext[0:POOL_HALO, :] = jnp.where(tile > 0, up_ref[...], 0.0)
        ext[POOL_HALO:, :] = u_ref[...]
        ssq = None
        for gi, w in enumerate(POOL_WINDOWS):
            sl = slice(gi * grp, (gi + 1) * grp)
            tot = ext[POOL_HALO:, sl]
            for k in range(1, w):
                tot = tot + ext[POOL_HALO - k:POOL_HALO - k + tm, sl]
            pooled = (tot / _pool_counts(tile, tm, w) - u_ref[:, sl]).astype(_MXU)
            pl_s[:, sl] = pooled
            y0 = _dot(pooled, pw_ref[gi])
            y_s[:, sl] = y0
            y = y0 * sc_ref[:, sl]
            part = jnp.sum(y * y, axis=-1, keepdims=True)
            ssq = part if ssq is None else ssq + part
        r = lax.rsqrt(ssq / width + RMS_EPS)
        y0 = y_s[...]
        yhat = y0 * sc_ref[...] * r
        dcv = dc_ref[...]
        dg_ref[...] += jnp.sum(dcv * yhat, axis=0, keepdims=True)
        dyh = dcv * g_ref[...]
        dy = r * (dyh - yhat * (jnp.sum(dyh * yhat, axis=-1, keepdims=True) / width))
        dsc_ref[...] += jnp.sum(dy * y0, axis=0, keepdims=True)
        y_s[...] = dy * sc_ref[...]
        for gi, w in enumerate(POOL_WINDOWS):
            sl = slice(gi * grp, (gi + 1) * grp)
            dm = y_s[:, sl].astype(_MXU)
            dpw_ref[gi] += _dot_tn(pl_s[:, sl], dm)
            dpooled = _dot_nt(dm, pw_ref[gi])
            y_s[:, sl] = dpooled
            qext[0:tm, sl] = dpooled / _pool_counts(tile, tm, w)
        for gi, w in enumerate(POOL_WINDOWS):
            sl = slice(gi * grp, (gi + 1) * grp)
            tot = qext[0:tm, sl]
            for k in range(1, w):
                tot = tot + qext[k:k + tm, sl]
            du_ref[:, sl] = tot - y_s[:, sl]
        qext[tm:, :] = qext[0:POOL_HALO, :]

    half_in = pl.BlockSpec((tm, width), lambda i: (n_t - 1 - i, 0))
    prev = pl.BlockSpec((POOL_HALO, width), lambda i: (jnp.maximum((n_t - 1 - i) * halo_blocks - 1, 0), 0))
    half = pl.BlockSpec((tm, width), lambda i: (n_t - 1 - i, 0))
    return pl.pallas_call(
        body, name=name, grid=(n_t,),
        out_shape=[jax.ShapeDtypeStruct((n_rows, width), F32), jax.ShapeDtypeStruct((1, width), F32),
                   jax.ShapeDtypeStruct((1, width), F32), jax.ShapeDtypeStruct(pool_w.shape, F32)],
        in_specs=[half_in, prev, half, _full(pool_w.shape), _full((1, width)), _full((1, width))],
        out_specs=[half, _full((1, width)), _full((1, width)), _full(pool_w.shape)],
        scratch_shapes=[pltpu.VMEM((tm + POOL_HALO, width), F32), pltpu.VMEM((tm + POOL_HALO, width), F32),
                        pltpu.VMEM((tm, width), F32), pltpu.VMEM((tm, width), _MXU)],
        compiler_params=_params(),
    )(proj, proj, dcat, pool_w, scale, g_out)


def _mix_out_fwd(cat_s, cat_p, h, g, wo_s, wo_p, name):
    n_rows, d = h.shape
    width = cat_s.shape[1]
    tm = _row_tile(n_rows)

    def body(cs_ref, cp_ref, h_ref, g_ref, ws_ref, wp_ref, m_ref, ho_ref):
        m = _dot(cs_ref[...], ws_ref[...]) + _dot(cp_ref[...], wp_ref[...])
        m_ref[...] = m
        ho_ref[...] = h_ref[...] + m * _rs(m) * g_ref[...]

    row = pl.BlockSpec((tm, d), lambda i: (i, 0))
    half = pl.BlockSpec((tm, width), lambda i: (i, 0))
    return pl.pallas_call(
        body, name=name, grid=(n_rows // tm,), out_shape=[jax.ShapeDtypeStruct((n_rows, d), F32)] * 2,
        in_specs=[half, half, row, _full((1, d)), _resident(), _resident()], out_specs=[row, row],
        compiler_params=_params(),
    )(cat_s, cat_p, h, g, wo_s, wo_p)


def _mix_out_bwd(dho, mixed, g, wo_s, wo_p, name):
    n_rows, d = mixed.shape
    width = wo_s.shape[0]
    n_blk = width // LANES
    tm = _row_tile(n_rows)

    def body(dho_ref, m_ref, g_ref, ws_ref, wp_ref, dm_ref, dcs_ref, dcp_ref, dg_ref):
        @pl.when(pl.program_id(0) == 0)
        def _():
            dg_ref[...] = jnp.zeros_like(dg_ref)

        m = m_ref[...]
        r = _rs(m)
        mh = m * r
        dy = dho_ref[...]
        dg_ref[...] += jnp.sum(dy * mh, axis=0, keepdims=True)
        dmh = dy * g_ref[...]
        dm = (r * (dmh - mh * jnp.mean(dmh * mh, axis=-1, keepdims=True))).astype(_MXU)
        dm_ref[...] = dm
        dcs = _dot_nt(dm, ws_ref[...])
        for k in range(n_blk):
            dcs_ref[k] = dcs[:, k * LANES:(k + 1) * LANES]
        dcp_ref[...] = _dot_nt(dm, wp_ref[...])

    row = pl.BlockSpec((tm, d), lambda i: (i, 0))
    half = pl.BlockSpec((tm, width), lambda i: (i, 0))
    return pl.pallas_call(
        body, name=name, grid=(n_rows // tm,),
        out_shape=[jax.ShapeDtypeStruct((n_rows, d), _MXU), jax.ShapeDtypeStruct((n_blk, n_rows, LANES), F32),
                   jax.ShapeDtypeStruct((n_rows, width), F32), jax.ShapeDtypeStruct((1, d), F32)],
        in_specs=[row, row, _full((1, d)), _resident(), _resident()],
        out_specs=[row, _lane_blocks(n_blk, tm), half, _full((1, d))], compiler_params=_params(),
    )(dho, mixed, g, wo_s, wo_p)


def _mix_in_bwd(du_s, du_p, h, dho, g, wi_s, wi_p, name):
    n_rows, d = h.shape
    width = du_p.shape[1]
    n_blk = width // LANES
    tm = _row_tile(n_rows)

    def body(dus_ref, dup_ref, h_ref, dho_ref, g_ref, ws_ref, wp_ref, dh_ref, dp_ref, dg_ref):
        @pl.when(pl.program_id(0) == 0)
        def _():
            dg_ref[...] = jnp.zeros_like(dg_ref)

        for k in range(n_blk):
            dp_ref[:, k * LANES:(k + 1) * LANES] = dus_ref[k].astype(_MXU)
        dup = dup_ref[...].astype(_MXU)
        dp_ref[:, width:2 * width] = dup
        dn = _dot_nt(dp_ref[:, 0:width], ws_ref[...]) + _dot_nt(dup, wp_ref[...])
        hv = h_ref[...]
        r = _rs(hv)
        hh = hv * r
        dg_ref[...] += jnp.sum(dn * hh, axis=0, keepdims=True)
        dhh = dn * g_ref[...]
        dh_ref[...] = dho_ref[...] + r * (dhh - hh * jnp.mean(dhh * hh, axis=-1, keepdims=True))

    row = pl.BlockSpec((tm, d), lambda i: (i, 0))
    half = pl.BlockSpec((tm, width), lambda i: (i, 0))
    return pl.pallas_call(
        body, name=name, grid=(n_rows // tm,),
        out_shape=[jax.ShapeDtypeStruct((n_rows, d), F32), jax.ShapeDtypeStruct((n_rows, 2 * width), _MXU),
                   jax.ShapeDtypeStruct((1, d), F32)],
        in_specs=[_lane_blocks(n_blk, tm), half, row, row, _full((1, d)), _resident(), _resident()],
        out_specs=[row, pl.BlockSpec((tm, 2 * width), lambda i: (i, 0)), _full((1, d))], compiler_params=_params(),
    )(du_s, du_p, h, dho, g, wi_s, wi_p)


def _adamw_update(w_ref, gv, m_ref, v_ref, d_ref, mo_ref, vo_ref):
    mn = ADAM_B1 * m_ref[...] + (1.0 - ADAM_B1) * gv
    vn = ADAM_B2 * v_ref[...] + (1.0 - ADAM_B2) * (gv * gv)
    m_hat = mn / (1.0 - ADAM_B1 ** ADAM_STEP)
    v_hat = vn / (1.0 - ADAM_B2 ** ADAM_STEP)
    d_ref[...] = -ADAM_LR * (m_hat / (jnp.sqrt(v_hat) + ADAM_EPS) + ADAM_WD * w_ref[...])
    mo_ref[...] = mn
    vo_ref[...] = vn


def _adamw(w, g, m, v, name):
    def body(w_ref, g_ref, m_ref, v_ref, d_ref, mo_ref, vo_ref):
        _adamw_update(w_ref, g_ref[...], m_ref, v_ref, d_ref, mo_ref, vo_ref)

    spec = _full(w.shape)
    return pl.pallas_call(
        body, name=name, grid=(1,), out_shape=[jax.ShapeDtypeStruct(w.shape, F32)] * 3,
        in_specs=[spec] * 4, out_specs=[spec] * 3, compiler_params=_params(),
    )(w, g, m, v)


def _adamw_slots(w, slots, m, v, name, after=()):
    def body(w_ref, s_ref, m_ref, v_ref, *rest):
        g_ref, d_ref, mo_ref, vo_ref = rest[len(after):]
        gv = s_ref[0].astype(F32)
        for k in range(1, N_DEV):
            gv = gv + s_ref[k].astype(F32)
        g_ref[...] = gv
        _adamw_update(w_ref, gv, m_ref, v_ref, d_ref, mo_ref, vo_ref)

    spec = _full(w.shape)
    return pl.pallas_call(
        body, name=name, grid=(1,), out_shape=[jax.ShapeDtypeStruct(w.shape, F32)] * 4,
        in_specs=[spec, _full(slots.shape), spec, spec] + [pl.BlockSpec(memory_space=pl.ANY)] * len(after),
        out_specs=[spec] * 4, compiler_params=_params(),
    )(w, slots, m, v, *after)


def _discretize(lam_re, lam_im, log_dt, b_re, b_im):
    dt = jnp.exp(log_dt)[:, None]
    decay = jnp.exp(lam_re * dt)
    ang = lam_im * dt
    a_re = decay * jnp.cos(ang)
    a_im = decay * jnp.sin(ang)
    nr = a_re - 1.0
    den = lam_re * lam_re + lam_im * lam_im
    q_re = (nr * lam_re + a_im * lam_im) / den
    q_im = (a_im * lam_re - nr * lam_im) / den
    bb_re = q_re[..., None] * b_re - q_im[..., None] * b_im
    bb_im = q_re[..., None] * b_im + q_im[..., None] * b_re
    return a_re, a_im, bb_re, bb_im


def _cmul(a, b):
    return a[0] * b[0] - a[1] * b[1], a[0] * b[1] + a[1] * b[0]


def _power_table(a_re, a_im, count, reverse):
    base = (a_re.reshape(1, -1), (-a_im if reverse else a_im).reshape(1, -1))
    exponent = (jnp.arange(count, 0, -1) if reverse else jnp.arange(1, count + 1))[:, None]
    shape = (count, base[0].shape[1])
    res = (jnp.ones(shape, F32), jnp.zeros(shape, F32))
    for bit in range(int(count).bit_length()):
        prod = _cmul(res, base)
        take = ((exponent >> bit) & 1) == 1
        res = (jnp.where(take, prod[0], res[0]), jnp.where(take, prod[1], res[1]))
        base = _cmul(base, base)
    return jnp.broadcast_to(jnp.stack(res)[:, :, None, :], (2, count, SUBLANES, shape[1]))


def _block_diag(p, n_blk):
    g, r, c = p.shape
    per = g // n_blk
    eye = jnp.eye(per, dtype=p.dtype)
    return jnp.einsum("jgrc,gk->jgrkc", p.reshape(n_blk, per, r, c), eye).reshape(n_blk, per * r, per * c)


def _block_diag_t(m, g):
    n_blk = m.shape[0]
    per = g // n_blk
    r, c = m.shape[1] // per, m.shape[2] // per
    eye = jnp.eye(per, dtype=m.dtype)
    return jnp.einsum("jgrkc,gk->jgrc", m.reshape(n_blk, per, r, per, c), eye).reshape(g, r, c)


def _pack_rows(parts, cols, multiple):
    flat = jnp.concatenate([p.reshape(-1) for p in parts])
    size = -(-flat.shape[0] // (cols * multiple)) * cols * multiple
    return jnp.pad(flat, (0, size - flat.shape[0])).reshape(-1, cols)


def _unpack(flat, shapes):
    out, pos = [], 0
    flat = flat.reshape(-1)
    for s in shapes:
        n = int(np.prod(s))
        out.append(flat[pos:pos + n].reshape(s))
        pos += n
    return out


def kernel(x, meta_tokens, ffn1_pre_norm, ffn1_post_norm, ffn1_w_gate, ffn1_w_up, ffn1_w_down, mix_pre_norm, mix_post_norm, w_in, ssm_lambda_re, ssm_lambda_im, ssm_log_dt, ssm_b_re, ssm_b_im, ssm_c_re, ssm_c_im, ssm_d, ssm_w_glu, pool_w, pool_scale, ssm_out_norm, pool_out_norm, w_out, ffn2_pre_norm, ffn2_post_norm, ffn2_w_gate, ffn2_w_up, ffn2_w_down, loss_target, m_meta_tokens, m_ffn1_pre_norm, m_ffn1_post_norm, m_ffn1_w_gate, m_ffn1_w_up, m_ffn1_w_down, m_mix_pre_norm, m_mix_post_norm, m_w_in, m_ssm_lambda_re, m_ssm_lambda_im, m_ssm_log_dt, m_ssm_b_re, m_ssm_b_im, m_ssm_c_re, m_ssm_c_im, m_ssm_d, m_ssm_w_glu, m_pool_w, m_pool_scale, m_ssm_out_norm, m_pool_out_norm, m_w_out, m_ffn2_pre_norm, m_ffn2_post_norm, m_ffn2_w_gate, m_ffn2_w_up, m_ffn2_w_down, v_meta_tokens, v_ffn1_pre_norm, v_ffn1_post_norm, v_ffn1_w_gate, v_ffn1_w_up, v_ffn1_w_down, v_mix_pre_norm, v_mix_post_norm, v_w_in, v_ssm_lambda_re, v_ssm_lambda_im, v_ssm_log_dt, v_ssm_b_re, v_ssm_b_im, v_ssm_c_re, v_ssm_c_im, v_ssm_d, v_ssm_w_glu, v_pool_w, v_pool_scale, v_ssm_out_norm, v_pool_out_norm, v_w_out, v_ffn2_pre_norm, v_ffn2_post_norm, v_ffn2_w_gate, v_ffn2_w_up, v_ffn2_w_down):
    args = dict(locals())
    names = ["meta_tokens", "ffn1_pre_norm", "ffn1_post_norm", "ffn1_w_gate", "ffn1_w_up", "ffn1_w_down", "mix_pre_norm",
             "mix_post_norm", "w_in", "ssm_lambda_re", "ssm_lambda_im", "ssm_log_dt", "ssm_b_re", "ssm_b_im", "ssm_c_re",
             "ssm_c_im", "ssm_d", "ssm_w_glu", "pool_w", "pool_scale", "ssm_out_norm", "pool_out_norm", "w_out",
             "ffn2_pre_norm", "ffn2_post_norm", "ffn2_w_gate", "ffn2_w_up", "ffn2_w_down"]
    sharded = ("meta_tokens", "ffn1_w_gate", "ffn1_w_up", "ffn1_w_down", "w_in", "w_out", "ffn2_w_gate", "ffn2_w_up",
               "ffn2_w_down")
    small = [n for n in names if n not in sharded]

    d = x.shape[-1]
    width = d // 2
    n_grp = ssm_lambda_re.shape[1]
    n_blk = width // LANES

    def stacked(gathered):
        return gathered.reshape(-1, d).astype(_MXU)

    def chunked(m):
        return m.reshape(N_DEV, -1, d)

    s_gate1, s_up1, s_down1 = _to_wire([ffn1_w_gate[0].T, ffn1_w_up[0].T, ffn1_w_down[0]], "wire_ffn1")
    s_win, s_wout, s_gate2, s_up2, s_down2 = _to_wire(
        [w_in[0], w_out[0], ffn2_w_gate[0].T, ffn2_w_up[0].T, ffn2_w_down[0]], "wire_rest")
    g_gate1, g_up1, g_down1, meta_all = _gather_two_level([s_gate1, s_up1, s_down1, meta_tokens], "gather_ffn1")
    wgt1, wut1, wd1 = stacked(g_gate1), stacked(g_up1), stacked(g_down1)
    meta_full = jnp.transpose(meta_all, (1, 0, 2)).reshape(N_META, d)

    (a1, b1, f1, h1, h0), (g_win, g_wout, g_gate2) = _ffn_fwd(
        None, ffn1_pre_norm, ffn1_post_norm, wgt1, wut1, wd1, "ffn1_fwd",
        xchg=_Xchg([s_win, s_wout, s_gate2], ["gather"] * 3), meta=meta_full, tokens=x[0])
    w_in_f, w_out_f = stacked(g_win), stacked(g_wout)

    a_re, a_im, bb_re, bb_im = _discretize(ssm_lambda_re[0], ssm_lambda_im[0], ssm_log_dt[0], ssm_b_re[0], ssm_b_im[0])
    steps = _row_tile(N_META + x.shape[1]) // SUBLANES
    coef = _power_table(a_re, a_im, steps, reverse=False)
    coef_rev = _power_table(a_re, a_im, steps, reverse=True)
    bmat_re = _block_diag(jnp.swapaxes(bb_re, 1, 2), n_blk).astype(_MXU)
    bmat_im = _block_diag(jnp.swapaxes(bb_im, 1, 2), n_blk).astype(_MXU)
    cmat_re = _block_diag(jnp.swapaxes(ssm_c_re[0], 1, 2), n_blk).astype(_MXU)
    cmat_im = _block_diag(jnp.swapaxes(ssm_c_im[0], 1, 2), n_blk).astype(_MXU)
    wz1 = _block_diag(ssm_w_glu[0][:, :, :SSM_GROUP_CH], n_blk).astype(_MXU)
    wz2 = _block_diag(ssm_w_glu[0][:, :, SSM_GROUP_CH:], n_blk).astype(_MXU)
    pool_wm = pool_w[0].astype(_MXU)

    n2, u_s, u_p = _mix_in_fwd(h1, mix_pre_norm, w_in_f, "mix_in_fwd")
    (x_re, x_im, cat_s), (g_up2, g_down2) = _ssm_forward(
        u_s, coef, bmat_re, bmat_im, cmat_re, cmat_im, wz1, wz2, ssm_d, ssm_out_norm, "ssm_fwd",
        xchg=_Xchg([s_up2, s_down2], ["gather"] * 2))
    wgt2, wut2, wd2 = stacked(g_gate2), stacked(g_up2), stacked(g_down2)
    cat_p = _pool_fwd(u_p, pool_wm, pool_scale, pool_out_norm, "pool_fwd")
    wo_s, wo_p = w_out_f[:width], w_out_f[width:]
    mixed, h2 = _mix_out_fwd(cat_s, cat_p, h1, mix_post_norm, wo_s, wo_p, "mix_out_fwd")

    (a2, b2, f2, dh3, sq_err), _ = _ffn_fwd(
        h2, ffn2_pre_norm, ffn2_post_norm, wgt2, wut2, wd2, "ffn2_fwd", target=loss_target[0])

    g, slots = {}, {}
    (dh2, da2, db2, s2, df2, nf2, g["ffn2_pre_norm"], g["ffn2_post_norm"]), _ = _ffn_bwd(
        dh3, h2, f2, a2, b2, ffn2_pre_norm, ffn2_post_norm, wgt2, wut2, wd2, "ffn2_bwd")
    dgate2, _ = _wgrad(da2, nf2, "ffn2_dgate")
    dup2, _ = _wgrad(db2, nf2, "ffn2_dup")
    ddown2, _ = _wgrad(s2, df2, "ffn2_ddown")

    dmixed, dcat_s, dcat_p, g["mix_post_norm"] = _mix_out_bwd(dh2, mixed, mix_post_norm, wo_s, wo_p, "mix_out_bwd")
    dwout = jnp.concatenate([_wgrad(cat_s, dmixed, "dwout_s")[0], _wgrad(cat_p, dmixed, "dwout_p")[0]], axis=0)
    ((du_s, g["ssm_out_norm"], g["ssm_d"], dwz1, dwz2, dcm_re, dcm_im, dbm_re, dbm_im, acc_re, acc_im),
     (slots["ffn2_w_gate"], slots["ffn2_w_up"], slots["ffn2_w_down"])) = _ssm_backward(
        u_s, x_re, x_im, dcat_s, coef_rev, bmat_re, bmat_im, cmat_re, cmat_im, wz1, wz2, ssm_d, ssm_out_norm, "ssm_bwd",
        xchg=_Xchg([chunked(dgate2), chunked(dup2), chunked(ddown2)], ["scatter"] * 3))
    du_p, g["pool_out_norm"], g["pool_scale"], dpw = _pool_bwd(u_p, dcat_p, pool_wm, pool_scale, pool_out_norm, "pool_bwd")
    wi_s, wi_p = w_in_f[:, :width], w_in_f[:, width:]
    dh1, dproj, g["mix_pre_norm"] = _mix_in_bwd(du_s, du_p, h1, dh2, mix_pre_norm, wi_s, wi_p, "mix_in_bwd")
    dwin, _ = _wgrad(n2, dproj, "dwin")

    g["ssm_c_re"] = jnp.swapaxes(_block_diag_t(jnp.swapaxes(dcm_re, 1, 2), n_grp), 1, 2)[None]
    g["ssm_c_im"] = jnp.swapaxes(_block_diag_t(jnp.swapaxes(dcm_im, 1, 2), n_grp), 1, 2)[None]
    g["ssm_w_glu"] = jnp.concatenate([_block_diag_t(dwz1, n_grp), _block_diag_t(dwz2, n_grp)], axis=-1)[None]
    dbb_re = jnp.swapaxes(_block_diag_t(dbm_re, n_grp), 1, 2)
    dbb_im = jnp.swapaxes(_block_diag_t(dbm_im, n_grp), 1, 2)
    da_re, da_im = acc_re.reshape(a_re.shape), acc_im.reshape(a_re.shape)
    _, disc_vjp = jax.vjp(_discretize, ssm_lambda_re[0], ssm_lambda_im[0], ssm_log_dt[0], ssm_b_re[0], ssm_b_im[0])
    d_lre, d_lim, d_ldt, d_bre, d_bim = disc_vjp((da_re, da_im, dbb_re, dbb_im))
    g["ssm_lambda_re"], g["ssm_lambda_im"], g["ssm_log_dt"] = d_lre[None], d_lim[None], d_ldt[None]
    g["ssm_b_re"], g["ssm_b_im"] = d_bre[None], d_bim[None]
    g["pool_w"] = dpw[None]

    late = ["ffn1_pre_norm", "ffn1_post_norm"]
    early = [n for n in small if n not in late]
    early_vec = _pack_rows([g[n] for n in early] + [sq_err[:, :1]], 1024, SUBLANES)
    ((dx, da1, db1, s1, df1, nf1, g["ffn1_pre_norm"], g["ffn1_post_norm"], dmeta_part),
     (slots["w_out"], slots["w_in"], recv_early)) = _ffn_bwd(
        dh1, h0, f1, a1, b1, ffn1_pre_norm, ffn1_post_norm, wgt1, wut1, wd1, "ffn1_bwd",
        xchg=_Xchg([chunked(dwout), chunked(dwin), early_vec], ["scatter", "scatter", "gather"]), split_meta=True)
    late_vec = _pack_rows([g[n] for n in late] + [dmeta_part], 1024, SUBLANES)
    dgate1, (recv_late,) = _wgrad(da1, nf1, "ffn1_dgate", xchg=_Xchg([late_vec], ["gather"]))
    dup1, (slots["ffn1_w_gate"],) = _wgrad(db1, nf1, "ffn1_dup", xchg=_Xchg([chunked(dgate1)], ["scatter"]))
    ddown1, (slots["ffn1_w_up"],) = _wgrad(s1, df1, "ffn1_ddown", xchg=_Xchg([chunked(dup1)], ["scatter"]))
    last_send, last_recv, last_src, last_land, started = _scatter_start(chunked(ddown1), "reduce_last_start")

    summed = _unpack(_sum_slots(recv_early, "sum_small_grads"), [g[n].shape for n in early] + [(1,)])
    for n, val in zip(early, summed):
        g[n] = val
    loss = (0.5 / d) * summed[-1][0]
    g[late[0]], g[late[1]], dmeta = _unpack(_sum_slots(recv_late, "sum_last_grads"), [(1, d), (1, d), (N_META, d)])
    g["meta_tokens"] = lax.dynamic_slice_in_dim(dmeta, _my_slot() * (d // N_DEV), d // N_DEV, axis=1)

    delta, new_m, new_v = {}, {}, {}
    done = []
    for n in sorted(sharded, key=lambda n: n == "ffn1_w_down"):
        shape = args[n].shape
        two_d = (-1, shape[-1])
        w2, m2, v2 = args[n].reshape(two_d), args["m_" + n].reshape(two_d), args["v_" + n].reshape(two_d)
        if n == "meta_tokens":
            dl, mn, vn = _adamw(w2, g[n], m2, v2, "adamw_" + n)
        elif n.endswith("gate") or n.endswith("up"):
            gs, dl, mn, vn = [t.T for t in _adamw_slots(w2.T, slots[n], m2.T, v2.T, "adamw_" + n, after=(started,))]
            done.append(dl)
        elif n == "ffn1_w_down":
            sent, land = _scatter_wait(last_send, last_recv, last_src, last_land, done, "reduce_last_wait")
            mine = lax.dynamic_slice_in_dim(sent, _my_slot(), 1, axis=0)
            slots[n] = lax.dynamic_update_slice_in_dim(land, mine, _my_slot(), axis=0)
            gs, dl, mn, vn = _adamw_slots(w2, slots[n], m2, v2, "adamw_" + n)
        else:
            gs, dl, mn, vn = _adamw_slots(w2, slots[n], m2, v2, "adamw_" + n, after=(started,))
            done.append(dl)
        if n != "meta_tokens":
            g[n] = gs[None]
        delta[n], new_m[n], new_v[n] = dl.reshape(shape), mn.reshape(shape), vn.reshape(shape)
    packs = [_pack_rows([src[n] if pre is None else args[pre + n] for n in small], 1024, SUBLANES)
             for src, pre in ((args, None), (g, None), (None, "m_"), (None, "v_"))]
    outs = _adamw(*packs, "adamw_small")
    shapes = [args[n].shape for n in small]
    for store, flat in zip((delta, new_m, new_v), outs):
        for n, val in zip(small, _unpack(flat, shapes)):
            store[n] = val

    grad_x = dx[None]
    return (loss, grad_x, *[g[n] for n in names], *[delta[n] for n in names], *[new_m[n] for n in names],
            *[new_v[n] for n in names])
```

```python
import functools
import math

import jax
import jax.numpy as jnp
import numpy as np
from jax import lax
from jax.experimental import pallas as pl
from jax.experimental.pallas import tpu as pltpu

F32 = jnp.float32
_MXU = jnp.bfloat16
_ACT = jnp.bfloat16
_WIRE = jnp.bfloat16

N_DEV = 8
N_META = 16
RMS_EPS = 1e-6
SSM_GROUP_CH = 16
LANES = 128
SUBLANES = 8
POOL_WINDOWS = (2, 4, 8, 16)
POOL_HALO = 16
ADAM_LR = 0.001
ADAM_B1 = 0.9
ADAM_B2 = 0.999
ADAM_EPS = 1e-08
ADAM_WD = 0.01
ADAM_STEP = 10
GELU_C0 = math.sqrt(2.0 / math.pi)
GELU_C1 = 0.044715
VMEM_LIMIT = 62 * 1024 * 1024

_NT = (((1,), (1,)), ((), ()))
_TN = (((0,), (0,)), ((), ()))


def _dot(a, b):
    return jnp.dot(a, b, preferred_element_type=F32)


def _dot_nt(a, b):
    return lax.dot_general(a, b, _NT, preferred_element_type=F32)


def _dot_tn(a, b):
    return lax.dot_general(a, b, _TN, preferred_element_type=F32)


def _rs(x):
    return lax.rsqrt(jnp.mean(x * x, axis=-1, keepdims=True) + RMS_EPS)


def _sigmoid(x):
    return 0.5 * jnp.tanh(0.5 * x) + 0.5


def _row_tile(n_rows, largest=432):
    for t in (432, 304, 48, 16):
        if t <= largest and n_rows % t == 0:
            return t
    raise ValueError(n_rows)


def _ff_chunk(d_ff):
    return d_ff // 2 if (d_ff // 2) % LANES == 0 else d_ff


def _params(n_axes=1):
    return pltpu.CompilerParams(dimension_semantics=("arbitrary",) * n_axes, vmem_limit_bytes=VMEM_LIMIT)


def _resident():
    return pl.BlockSpec(memory_space=pltpu.VMEM)


def _full(shape):
    nd = len(shape)
    return pl.BlockSpec(shape, lambda *_: (0,) * nd)


def _lane_blocks(n_blk, tm, tile_of=lambda i: i):
    return pl.BlockSpec((n_blk, tm, LANES), lambda i: (0, tile_of(i), 0))


PEER_ORDER = (1, 2, 4, 3, 5, 6, 7)


def _split(refs, counts):
    out, pos = [], 0
    for n in counts:
        out.append(refs[pos:pos + n])
        pos += n
    return out


def _peer(r):
    x, y, c = lax.axis_index("x"), lax.axis_index("y"), lax.axis_index("c")
    return (1 - x if r & 4 else x, 1 - y if r & 2 else y, 1 - c if r & 1 else c)


def _my_slot():
    return 4 * lax.axis_index("x") + 2 * lax.axis_index("y") + lax.axis_index("c")


class _Xchg:
    def __init__(self, srcs, kinds):
        self.srcs, self.kinds, self.n = list(srcs), list(kinds), len(srcs)
        self.out_shape = [jax.ShapeDtypeStruct((N_DEV,) + s.shape if k == "gather" else s.shape, s.dtype)
                          for s, k in zip(self.srcs, self.kinds)]
        self.specs = [pl.BlockSpec(memory_space=pl.ANY)] * self.n
        self.scratch = [pltpu.SemaphoreType.DMA((self.n * (N_DEV - 1),)), pltpu.SemaphoreType.DMA((self.n * (N_DEV - 1),)),
                        pltpu.SemaphoreType.DMA((self.n,))]

    def copies(self, src, dst, sems):
        send_sems, recv_sems, local_sems = sems
        me = _my_slot()
        out = []
        for a in range(self.n):
            mine = src[a] if self.kinds[a] == "gather" else src[a].at[me]
            out.append(pltpu.make_async_copy(mine, dst[a].at[me], local_sems.at[a]))
            for r in PEER_ORDER:
                px, py, pc = _peer(r)
                part = src[a] if self.kinds[a] == "gather" else src[a].at[4 * px + 2 * py + pc]
                k = a * (N_DEV - 1) + r - 1
                out.append(pltpu.make_async_remote_copy(
                    src_ref=part, dst_ref=dst[a].at[me], send_sem=send_sems.at[k], recv_sem=recv_sems.at[k],
                    device_id=(px, py, pc), device_id_type=pl.DeviceIdType.MESH))
        return out

    def start(self, src, dst, sems):
        for cp in self.copies(src, dst, sems):
            cp.start()

    def wait(self, src, dst, sems):
        for cp in self.copies(src, dst, sems):
            cp.wait()


class _NoXchg:
    n, srcs, out_shape, specs, scratch = 0, [], [], [], []

    def start(self, *_):
        pass

    wait = start


def _call(body, args, *, name, grid, out_shape, in_specs, out_specs, scratch_shapes=(), xchg=None, after=()):
    xc = xchg or _NoXchg()
    counts = (len(in_specs), xc.n, len(after), len(out_shape), xc.n, len(scratch_shapes), len(xc.scratch))

    def wrapped(*refs):
        ins, xsrc, _, outs, xdst, scr, sems = _split(refs, counts)
        ids = [pl.program_id(k) for k in range(len(grid))]
        if xc.n:
            @pl.when(functools.reduce(jnp.logical_and, [i == 0 for i in ids]))
            def _():
                xc.start(xsrc, xdst, sems)

        body(*ins, *outs, *scr)
        if xc.n:
            @pl.when(functools.reduce(jnp.logical_and, [i == g - 1 for i, g in zip(ids, grid)]))
            def _():
                xc.wait(xsrc, xdst, sems)

    res = pl.pallas_call(
        wrapped, name=name, grid=grid, out_shape=list(out_shape) + xc.out_shape,
        in_specs=list(in_specs) + xc.specs + [pl.BlockSpec(memory_space=pl.ANY)] * len(after),
        out_specs=list(out_specs) + xc.specs,
        scratch_shapes=list(scratch_shapes) + xc.scratch, compiler_params=_params(len(grid)),
    )(*args, *xc.srcs, *after)
    return res[:len(out_shape)], res[len(out_shape):]


def _exchange(srcs, kinds, name):
    xc = _Xchg(srcs, kinds)

    def body(*refs):
        src, dst, sems = _split(refs, (xc.n, xc.n, 3))
        xc.start(src, dst, sems)
        xc.wait(src, dst, sems)

    return pl.pallas_call(body, name=name, out_shape=xc.out_shape, in_specs=xc.specs, out_specs=xc.specs,
                          scratch_shapes=xc.scratch)(*srcs)


def _split_copies(kinds, src, land, send_sems, recv_sems):
    me = _my_slot()
    out = []
    for a, kind in enumerate(kinds):
        for r in PEER_ORDER:
            px, py, pc = _peer(r)
            k = a * (N_DEV - 1) + r - 1
            out.append(pltpu.make_async_remote_copy(
                src_ref=src[a] if kind == "gather" else src[a].at[4 * px + 2 * py + pc], dst_ref=land[a].at[me],
                send_sem=send_sems.at[k], recv_sem=recv_sems.at[k], device_id=(px, py, pc),
                device_id_type=pl.DeviceIdType.MESH))
    return out


_SPLIT_EFFECT = pltpu.SideEffectType.DATAFLOW_SIDE_EFFECTING


def _split_start(srcs, kinds, name):
    n = len(srcs)
    hbm, sem = pl.BlockSpec(memory_space=pltpu.HBM), pl.BlockSpec(memory_space=pltpu.SEMAPHORE)
    lands = [lax.empty((N_DEV,) + s.shape if k == "gather" else s.shape, s.dtype) for s, k in zip(srcs, kinds)]

    def body(*refs):
        src, land, (send_sems, recv_sems), _, (token,) = _split(refs, (n, n, 2, 2 * n, 1))
        for cp in _split_copies(kinds, src, land, send_sems, recv_sems):
            cp.start()
        token[...] = jnp.zeros_like(token)

    n_sem = n * (N_DEV - 1)
    out = pl.pallas_call(
        body, name=name,
        out_shape=[pltpu.SemaphoreType.DMA((n_sem,)), pltpu.SemaphoreType.DMA((n_sem,))]
        + [pltpu.HBM(a.shape, a.dtype) for a in list(srcs) + lands] + [jax.ShapeDtypeStruct((SUBLANES, LANES), F32)],
        in_specs=[hbm] * (2 * n), out_specs=[sem, sem] + [hbm] * (2 * n) + [pl.BlockSpec(memory_space=pltpu.VMEM)],
        input_output_aliases={k: 2 + k for k in range(2 * n)},
        compiler_params=pltpu.CompilerParams(has_side_effects=_SPLIT_EFFECT),
    )(*[pltpu.with_memory_space_constraint(a, pltpu.HBM) for a in list(srcs) + lands])
    return (kinds, out[0], out[1], out[2:2 + n], out[2 + n:2 + 2 * n]), out[-1]


def _split_wait(started, after, name):
    kinds, send_sems, recv_sems, srcs, lands = started
    n = len(srcs)
    hbm, sem = pl.BlockSpec(memory_space=pltpu.HBM), pl.BlockSpec(memory_space=pltpu.SEMAPHORE)

    def body(*refs):
        src, land, (send_sems, recv_sems) = _split(refs, (n, n, 2))[:3]
        for cp in _split_copies(kinds, src, land, send_sems, recv_sems):
            cp.wait_send()
            cp.wait_recv()

    out = pl.pallas_call(
        body, name=name, out_shape=[pltpu.HBM(a.shape, a.dtype) for a in list(srcs) + list(lands)],
        in_specs=[hbm] * (2 * n) + [sem, sem] + [pl.BlockSpec(memory_space=pl.ANY)] * len(after),
        out_specs=[hbm] * (2 * n), input_output_aliases={k: k for k in range(2 * n)},
        compiler_params=pltpu.CompilerParams(has_side_effects=_SPLIT_EFFECT),
    )(*srcs, *lands, send_sems, recv_sems, *after)
    me = _my_slot()
    filled = []
    for kind, sent, land in zip(kinds, out[:n], out[n:]):
        mine = sent[None] if kind == "gather" else lax.dynamic_slice_in_dim(sent, me, 1, axis=0)
        filled.append(lax.dynamic_update_slice_in_dim(land, mine, me, axis=0))
    return filled


def _gather_two_level(srcs, name):
    n = len(srcs)
    out_shape = [jax.ShapeDtypeStruct((N_DEV,) + s.shape, s.dtype) for s in srcs]
    chips = (2, 4, 6)

    def body(*refs):
        src, dst, (send_sems, recv_sems, local_sems) = _split(refs, (n, n, 3))
        x, y, c = lax.axis_index("x"), lax.axis_index("y"), lax.axis_index("c")
        me = 4 * x + 2 * y + c
        sibling = (x, y, 1 - c)

        def copy(a, k, slot, to, from_src=False):
            return pltpu.make_async_remote_copy(
                src_ref=src[a] if from_src else dst[a].at[slot], dst_ref=dst[a].at[slot],
                send_sem=send_sems.at[a * 7 + k], recv_sem=recv_sems.at[a * 7 + k],
                device_id=to, device_id_type=pl.DeviceIdType.MESH)

        def slot_of(r, core):
            px, py, _ = _peer(r)
            return 4 * px + 2 * py + core

        local = [pltpu.make_async_copy(src[a], dst[a].at[me], local_sems.at[a]) for a in range(n)]
        sent = []
        for a in range(n):
            local[a].start()
            sent.append(copy(a, 0, me, sibling, from_src=True))
            sent += [copy(a, 1 + j, me, _peer(r), from_src=True) for j, r in enumerate(chips)]
        for cp in sent:
            cp.start()
        for j, r in enumerate(chips):
            for a in range(n):
                copy(a, 1 + j, slot_of(r, c), _peer(r)).wait_recv()
                cp = copy(a, 4 + j, slot_of(r, c), sibling)
                cp.start()
                sent.append(cp)
        for a in range(n):
            copy(a, 0, slot_of(0, 1 - c), sibling).wait_recv()
            for j, r in enumerate(chips):
                copy(a, 4 + j, slot_of(r, 1 - c), sibling).wait_recv()
        for cp in local:
            cp.wait()
        for cp in sent:
            cp.wait_send()

    any_spec = pl.BlockSpec(memory_space=pl.ANY)
    return pl.pallas_call(
        body, name=name, out_shape=out_shape, in_specs=[any_spec] * n, out_specs=[any_spec] * n,
        scratch_shapes=[pltpu.SemaphoreType.DMA((n * 7,)), pltpu.SemaphoreType.DMA((n * 7,)), pltpu.SemaphoreType.DMA((n,))],
    )(*srcs)


def _to_wire(mats, name):
    def body(*refs):
        for src, dst in zip(refs[:len(mats)], refs[len(mats):]):
            dst[...] = src[...].astype(_WIRE)

    return pl.pallas_call(
        body, name=name, grid=(1,), out_shape=[jax.ShapeDtypeStruct(m.shape, _WIRE) for m in mats],
        in_specs=[_full(m.shape) for m in mats], out_specs=[_full(m.shape) for m in mats], compiler_params=_params(),
    )(*mats)


def _sum_slots(r, name):
    _, rows, cols = r.shape
    blk = rows
    for cand in (rows, 592, 512, 256, 128, 64, 32, 16):
        if rows % cand == 0 and N_DEV * cand * cols * r.dtype.itemsize <= 8 * 1024 * 1024:
            blk = cand
            break

    def body(r_ref, o_ref):
        acc = r_ref[0].astype(F32)
        for d in range(1, N_DEV):
            acc = acc + r_ref[d].astype(F32)
        o_ref[...] = acc

    return pl.pallas_call(
        body, name=name, grid=(rows // blk,), out_shape=jax.ShapeDtypeStruct((rows, cols), F32),
        in_specs=[pl.BlockSpec((N_DEV, blk, cols), lambda i: (0, i, 0))],
        out_specs=pl.BlockSpec((blk, cols), lambda i: (i, 0)), compiler_params=_params(),
    )(r)


def _token_tile_copy(tokens_ref, buf, sems, i, tm, write=False):
    if isinstance(i, int) and i == 0:
        far, near = tokens_ref.at[pl.ds(0, tm - N_META)], buf.at[0, pl.ds(N_META, tm - N_META)]
    else:
        start = i * tm - N_META if isinstance(i, int) else pl.multiple_of(i * tm - N_META, SUBLANES)
        far, near = tokens_ref.at[pl.ds(start, tm)], buf.at[i % 2]
    return pltpu.make_async_copy(near, far, sems.at[i % 2]) if write else pltpu.make_async_copy(far, near, sems.at[i % 2])


def _fetch_token_tile(tokens_ref, buf, sems, i, n_t, tm):
    @pl.when(i == 0)
    def _():
        _token_tile_copy(tokens_ref, buf, sems, 0, tm).start()

    @pl.when(i + 1 < n_t)
    def _():
        _token_tile_copy(tokens_ref, buf, sems, i + 1, tm).start()

    @pl.when(i == 0)
    def _():
        _token_tile_copy(tokens_ref, buf, sems, 0, tm).wait()

    @pl.when(i > 0)
    def _():
        _token_tile_copy(tokens_ref, buf, sems, i, tm).wait()


def _ffn_fwd(h, g_pre, g_post, wgt, wut, wd, name, xchg=None, *, meta=None, tokens=None, target=None):
    first = h is None
    d = wd.shape[1]
    n_rows = N_META + tokens.shape[0] if first else h.shape[0]
    d_ff = wd.shape[0]
    tm, fc = _row_tile(n_rows), _ff_chunk(d_ff)
    n_t, n_c = n_rows // tm, d_ff // fc

    def body(src_ref, side_ref, gpre_ref, gpost_ref, wgt_ref, wut_ref, wd_ref, a_ref, b_ref, f_ref, o1_ref, o2_ref,
             n_scr, acc, buf, sems):
        i, c = pl.program_id(0), pl.program_id(1)
        slot = i % 2

        @pl.when(c == 0)
        def _():
            if first:
                _fetch_token_tile(side_ref, buf, sems, i, n_t, tm)

                @pl.when(i == 0)
                def _():
                    buf[0, 0:N_META, :] = src_ref[...]

                hv = buf[slot]
                o2_ref[...] = hv
            else:
                _fetch_token_tile(side_ref, buf, sems, i, n_t, tm)

                @pl.when(i == 0)
                def _():
                    buf[0, 0:N_META, :] = jnp.zeros((N_META, d), F32)
                    o2_ref[...] = jnp.zeros_like(o2_ref)

                hv = src_ref[...]
            n_scr[...] = (hv * _rs(hv) * gpre_ref[...]).astype(_MXU)
            acc[...] = jnp.zeros_like(acc)

        rows = pl.ds(pl.multiple_of(c * fc, fc), fc)
        nv = n_scr[...]
        a = _dot_nt(nv, wgt_ref[rows, :])
        b = _dot_nt(nv, wut_ref[rows, :])
        a_ref[...] = a.astype(_ACT)
        b_ref[...] = b.astype(_ACT)
        s = a * _sigmoid(a) * b
        acc[...] += _dot(s.astype(_MXU), wd_ref[rows, :])

        @pl.when(c == n_c - 1)
        def _():
            f = acc[...]
            f_ref[...] = f
            step = 0.5 * (f * _rs(f) * gpost_ref[...])
            if first:
                o1_ref[...] = buf[slot] + step
            else:
                row = i * tm + lax.broadcasted_iota(jnp.int32, (tm, 1), 0)
                err = jnp.where(row >= N_META, (src_ref[...] + step) - buf[slot], 0.0)
                o1_ref[...] = err / d
                o2_ref[...] += jnp.sum(jnp.sum(err * err, axis=0, keepdims=True), axis=1, keepdims=True)

    row = pl.BlockSpec((tm, d), lambda i, c: (i, 0))
    chunk = pl.BlockSpec((tm, fc), lambda i, c: (i, c))
    hbm = pl.BlockSpec(memory_space=pl.ANY)
    if first:
        operands, specs = (meta, tokens), [_full(meta.shape), hbm]
        last_shape, last_spec = jax.ShapeDtypeStruct((n_rows, d), F32), row
    else:
        operands, specs = (h, target), [row, hbm]
        last_shape, last_spec = jax.ShapeDtypeStruct((1, LANES), F32), pl.BlockSpec((1, LANES), lambda i, c: (0, 0))
    return _call(
        body, (*operands, g_pre, g_post, wgt, wut, wd), name=name, grid=(n_t, n_c),
        out_shape=[jax.ShapeDtypeStruct((n_rows, d_ff), _ACT), jax.ShapeDtypeStruct((n_rows, d_ff), _ACT),
                   jax.ShapeDtypeStruct((n_rows, d), F32), jax.ShapeDtypeStruct((n_rows, d), F32), last_shape],
        in_specs=specs + [_full((1, d)), _full((1, d)), _resident(), _resident(), _resident()],
        out_specs=[chunk, chunk, row, row, last_spec],
        scratch_shapes=[pltpu.VMEM((tm, d), _MXU), pltpu.VMEM((tm, d), F32), pltpu.VMEM((2, tm, d), F32),
                        pltpu.SemaphoreType.DMA((2,))], xchg=xchg)


def _ffn_bwd(dho, h, f, a, b, g_pre, g_post, wgt, wut, wd, name, xchg=None, *, split_meta=False, after=()):
    n_rows, d = h.shape
    d_ff = wd.shape[0]
    tm, fc = _row_tile(n_rows), _ff_chunk(d_ff)
    n_t, n_c = n_rows // tm, d_ff // fc
    assert n_t >= 2

    def body(dho_ref, h_ref, f_ref, a_ref, b_ref, gpre_ref, gpost_ref, wgt_ref, wut_ref, wd_ref,
             dh_ref, da_ref, db_ref, s_ref, df_ref, n_ref, dgpre_ref, dgpost_ref, *rest):
        dn_acc = rest[-1]
        i, c = pl.program_id(0), pl.program_id(1)

        @pl.when((i == 0) & (c == 0))
        def _():
            dgpre_ref[...] = jnp.zeros_like(dgpre_ref)
            dgpost_ref[...] = jnp.zeros_like(dgpost_ref)

        @pl.when(c == 0)
        def _():
            fv = f_ref[...]
            rf = _rs(fv)
            fhat = fv * rf
            dy = 0.5 * dho_ref[...]
            dgpost_ref[...] += jnp.sum(dy * fhat, axis=0, keepdims=True)
            dfhat = dy * gpost_ref[...]
            df = rf * (dfhat - fhat * jnp.mean(dfhat * fhat, axis=-1, keepdims=True))
            df_ref[...] = df.astype(_MXU)
            hv = h_ref[...]
            n_ref[...] = (hv * _rs(hv) * gpre_ref[...]).astype(_MXU)
            dn_acc[...] = jnp.zeros_like(dn_acc)

        rows = pl.ds(pl.multiple_of(c * fc, fc), fc)
        ds = _dot_nt(df_ref[...], wd_ref[rows, :])
        av = a_ref[...].astype(F32)
        bv = b_ref[...].astype(F32)
        sg = _sigmoid(av)
        si = av * sg
        da = (ds * bv * (sg * (1.0 + av * (1.0 - sg)))).astype(_MXU)
        db = (ds * si).astype(_MXU)
        da_ref[...] = da
        db_ref[...] = db
        s_ref[...] = (si * bv).astype(_MXU)
        dn_acc[...] += _dot(da, wgt_ref[rows, :]) + _dot(db, wut_ref[rows, :])

        @pl.when(c == n_c - 1)
        def _():
            dn = dn_acc[...]
            hv = h_ref[...]
            r = _rs(hv)
            hhat = hv * r
            dgpre_ref[...] += jnp.sum(dn * hhat, axis=0, keepdims=True)
            dhh = dn * gpre_ref[...]
            dh = dho_ref[...] + r * (dhh - hhat * jnp.mean(dhh * hhat, axis=-1, keepdims=True))
            if not split_meta:
                dh_ref[...] = dh
                return
            dmeta_ref, buf, sems = rest[0], rest[1], rest[2]

            def out_copy(k):
                return _token_tile_copy(dh_ref, buf, sems, k, tm, write=True)

            @pl.when(i == 2)
            def _():
                out_copy(0).wait()

            @pl.when(i > 2)
            def _():
                out_copy(i - 2).wait()

            buf[i % 2] = dh

            @pl.when(i == 0)
            def _():
                dmeta_ref[...] = buf[0, 0:N_META, :]
                out_copy(0).start()

            @pl.when(i > 0)
            def _():
                out_copy(i).start()

            @pl.when(i == n_t - 1)
            def _():
                out_copy(n_t - 2).wait()
                out_copy(n_t - 1).wait()

    row = pl.BlockSpec((tm, d), lambda i, c: (i, 0))
    chunk = pl.BlockSpec((tm, fc), lambda i, c: (i, c))
    vec = pl.BlockSpec((1, d), lambda i, c: (0, 0))
    shapes = [jax.ShapeDtypeStruct((n_rows, d_ff), _MXU)] * 3 + [jax.ShapeDtypeStruct((n_rows, d), _MXU)] * 2 \
        + [jax.ShapeDtypeStruct((1, d), F32)] * 2
    specs = [chunk, chunk, chunk, row, row, vec, vec]
    scratch = [pltpu.VMEM((tm, d), F32)]
    if split_meta:
        shapes = [jax.ShapeDtypeStruct((n_rows - N_META, d), F32)] + shapes + [jax.ShapeDtypeStruct((N_META, d), F32)]
        specs = [pl.BlockSpec(memory_space=pl.ANY)] + specs + [pl.BlockSpec((N_META, d), lambda i, c: (0, 0))]
        scratch = [pltpu.VMEM((2, tm, d), F32), pltpu.SemaphoreType.DMA((2,))] + scratch
    else:
        shapes, specs = [jax.ShapeDtypeStruct((n_rows, d), F32)] + shapes, [row] + specs
    return _call(
        body, (dho, h, f, a, b, g_pre, g_post, wgt, wut, wd), name=name, grid=(n_t, n_c), out_shape=shapes,
        in_specs=[row, row, row, chunk, chunk, vec, vec, _resident(), _resident(), _resident()], out_specs=specs,
        scratch_shapes=scratch, xchg=xchg, after=after)


def _wgrad(xm, ym, name, xchg=None):
    n_rows, a_dim = xm.shape
    b_dim = ym.shape[1]
    tk = n_rows
    for cand in (2736, 1296, 432, 48, 16):
        if n_rows % cand == 0:
            tk = cand
            break
    ta = a_dim
    for cand in (1408, 1024, 512):
        if a_dim % cand == 0:
            ta = cand
            break

    n_k = n_rows // tk

    def body(x_ref, y_ref, o_ref, acc):
        k = pl.program_id(1)

        @pl.when(k == 0)
        def _():
            acc[...] = jnp.zeros_like(acc)

        acc[...] += _dot_tn(x_ref[...], y_ref[...])

        @pl.when(k == n_k - 1)
        def _():
            o_ref[...] = acc[...].astype(o_ref.dtype)

    (out,), extra = _call(
        body, (xm, ym), name=name, grid=(a_dim // ta, n_k), out_shape=[jax.ShapeDtypeStruct((a_dim, b_dim), _WIRE)],
        in_specs=[pl.BlockSpec((tk, ta), lambda j, k: (k, j)), pl.BlockSpec((tk, b_dim), lambda j, k: (k, 0))],
        out_specs=[pl.BlockSpec((ta, b_dim), lambda j, k: (j, 0))], scratch_shapes=[pltpu.VMEM((ta, b_dim), F32)],
        xchg=xchg)
    return out, extra


def _mix_in_fwd(h, g, w_in, name):
    n_rows, d = h.shape
    tm = _row_tile(n_rows)
    width = w_in.shape[1] // 2
    n_blk = width // LANES

    def body(h_ref, g_ref, w_ref, n_ref, us_ref, up_ref):
        hv = h_ref[...]
        nv = (hv * _rs(hv) * g_ref[...]).astype(_MXU)
        n_ref[...] = nv
        p = _dot(nv, w_ref[...])
        for k in range(n_blk):
            us_ref[k] = p[:, k * LANES:(k + 1) * LANES]
        up_ref[...] = p[:, width:]

    row = pl.BlockSpec((tm, d), lambda i: (i, 0))
    half = pl.BlockSpec((tm, width), lambda i: (i, 0))
    return pl.pallas_call(
        body, name=name, grid=(n_rows // tm,),
        out_shape=[jax.ShapeDtypeStruct((n_rows, d), _MXU), jax.ShapeDtypeStruct((n_blk, n_rows, LANES), F32),
                   jax.ShapeDtypeStruct((n_rows, width), F32)],
        in_specs=[row, _full((1, d)), _resident()], out_specs=[row, _lane_blocks(n_blk, tm), half],
        compiler_params=_params(),
    )(h, g, w_in)


def _gelu_parts(y):
    th = jnp.tanh(GELU_C0 * (y + GELU_C1 * (y * y * y)))
    return 0.5 * (1.0 + th), th


SCAN_LANES = 512


def _regroup(src_ref, dst_ref, seg):
    for k in range(src_ref.shape[0]):
        for j in range(seg):
            dst_ref[j * SUBLANES:(j + 1) * SUBLANES, k * LANES:(k + 1) * LANES] = src_ref[k, pl.ds(j, SUBLANES, stride=seg), :]


def _ungroup(src_ref, dst_ref, seg):
    for k in range(dst_ref.shape[0]):
        for j in range(seg):
            dst_ref[k, pl.ds(j, SUBLANES, stride=seg), :] = src_ref[j * SUBLANES:(j + 1) * SUBLANES, k * LANES:(k + 1) * LANES]


def _rows_to_sublanes(rows):
    rid = lax.broadcasted_iota(jnp.int32, (SUBLANES, rows[0].shape[1]), 0)
    out = jnp.broadcast_to(rows[0], rid.shape)
    for s in range(1, SUBLANES):
        out = jnp.where(rid == s, rows[s], out)
    return out


def _segment_scan(re_ref, im_ref, tab_ref, car_re, car_im, seg, reverse, x_refs=None):
    n_all = re_ref.shape[1]
    first = seg - 1 if reverse else 0
    sums = []
    for cb in range(n_all // SCAN_LANES):
        cols = pl.ds(cb * SCAN_LANES, SCAN_LANES)
        ar, ai = tab_ref[0, first, :, cols], tab_ref[1, first, :, cols]

        def local(t, carry, cols=cols, ar=ar, ai=ai):
            r0 = pl.multiple_of((seg - 1 - t if reverse else t) * SUBLANES, SUBLANES)
            xr, xi = carry
            nr = ar * xr - ai * xi + re_ref[pl.ds(r0, SUBLANES), cols]
            ni = ar * xi + ai * xr + im_ref[pl.ds(r0, SUBLANES), cols]
            re_ref[pl.ds(r0, SUBLANES), cols] = nr
            im_ref[pl.ds(r0, SUBLANES), cols] = ni
            return nr, ni

        zero = jnp.zeros((SUBLANES, SCAN_LANES), F32)
        fr, fi = lax.fori_loop(0, seg, local, (zero, zero))

        last = 0 if reverse else seg - 1
        sr, si = tab_ref[0, last, 0:1, cols], tab_ref[1, last, 0:1, cols]
        cr, ci = car_re[:, cols], car_im[:, cols]
        rows_r, rows_i = [None] * SUBLANES, [None] * SUBLANES
        for s in (range(SUBLANES - 1, -1, -1) if reverse else range(SUBLANES)):
            rows_r[s], rows_i[s] = cr, ci
            cr, ci = sr * cr - si * ci + fr[s:s + 1], sr * ci + si * cr + fi[s:s + 1]
        car_re[:, cols] = cr
        car_im[:, cols] = ci
        cmr, cmi = _rows_to_sublanes(rows_r), _rows_to_sublanes(rows_i)

        def fix(t, carry, cols=cols, cmr=cmr, cmi=cmi):
            j = seg - 1 - t if reverse else t
            r0 = pl.multiple_of(j * SUBLANES, SUBLANES)
            pr, pi = tab_ref[0, j, :, cols], tab_ref[1, j, :, cols]
            gr = re_ref[pl.ds(r0, SUBLANES), cols] + (pr * cmr - pi * cmi)
            gi = im_ref[pl.ds(r0, SUBLANES), cols] + (pr * cmi + pi * cmr)
            re_ref[pl.ds(r0, SUBLANES), cols] = gr
            im_ref[pl.ds(r0, SUBLANES), cols] = gi
            if x_refs is None:
                return carry
            nxr, nxi, accr, acci = carry
            xr, xi = x_refs[0][pl.ds(r0, SUBLANES), cols], x_refs[1][pl.ds(r0, SUBLANES), cols]
            return gr, gi, accr + (xr * nxr + xi * nxi), acci + (xr * nxi - xi * nxr)

        if x_refs is None:
            lax.fori_loop(0, seg, fix, 0)
        else:
            fin = lax.fori_loop(0, seg, fix, (cmr, cmi, zero, zero))
            sums.append((jnp.sum(fin[2], axis=0, keepdims=True), jnp.sum(fin[3], axis=0, keepdims=True)))
    return sums


def _ssm_forward(u, table, b_re, b_im, c_re, c_im, wz1, wz2, d_skip, g_out, name, xchg=None):
    n_rows = u.shape[1]
    n_blk, _, n_state = b_re.shape
    width, n_all = n_blk * LANES, n_blk * n_state
    tm = _row_tile(n_rows)
    seg = tm // SUBLANES

    def body(u_ref, tab_ref, bre_ref, bim_ref, cre_ref, cim_ref, wz1_ref, wz2_ref, d_ref, g_ref,
             xre_ref, xim_ref, o_ref, car_re, car_im, ug, out_scr, blocks):
        @pl.when(pl.program_id(0) == 0)
        def _():
            car_re[...] = jnp.zeros_like(car_re)
            car_im[...] = jnp.zeros_like(car_im)

        _regroup(u_ref, ug, seg)
        ub = ug[...].astype(_MXU)
        for j in range(n_blk):
            uj = ub[:, j * LANES:(j + 1) * LANES]
            xre_ref[:, j * n_state:(j + 1) * n_state] = _dot(uj, bre_ref[j])
            xim_ref[:, j * n_state:(j + 1) * n_state] = _dot(uj, bim_ref[j])
        _segment_scan(xre_ref, xim_ref, tab_ref, car_re, car_im, seg, reverse=False)

        ssq = None
        for j in range(n_blk):
            sl = slice(j * LANES, (j + 1) * LANES)
            st = slice(j * n_state, (j + 1) * n_state)
            yc = _dot(xre_ref[:, st].astype(_MXU), cre_ref[j]) - _dot(xim_ref[:, st].astype(_MXU), cim_ref[j])
            y = yc + d_ref[:, sl] * ug[:, sl]
            cdf, _ = _gelu_parts(y)
            gy = (y * cdf).astype(_MXU)
            out = _dot(gy, wz1_ref[j]) * _sigmoid(_dot(gy, wz2_ref[j]))
            out_scr[:, sl] = out
            part = jnp.sum(out * out, axis=-1, keepdims=True)
            ssq = part if ssq is None else ssq + part
        r = lax.rsqrt(ssq / width + RMS_EPS)
        out_scr[...] = out_scr[...] * r * g_ref[...]
        _ungroup(out_scr, blocks, seg)
        for k in range(n_blk):
            o_ref[:, k * LANES:(k + 1) * LANES] = blocks[k].astype(_MXU)

    half = pl.BlockSpec((tm, width), lambda i: (i, 0))
    state = pl.BlockSpec((tm, n_all), lambda i: (i, 0))
    return _call(
        body, (u, table, b_re, b_im, c_re, c_im, wz1, wz2, d_skip, g_out), name=name, grid=(n_rows // tm,),
        out_shape=[jax.ShapeDtypeStruct((n_rows, n_all), F32)] * 2 + [jax.ShapeDtypeStruct((n_rows, width), _MXU)],
        in_specs=[_lane_blocks(n_blk, tm), _resident(), _full(b_re.shape), _full(b_im.shape), _full(c_re.shape),
                  _full(c_im.shape), _full(wz1.shape), _full(wz2.shape), _full((1, width)), _full((1, width))],
        out_specs=[state, state, half],
        scratch_shapes=[pltpu.VMEM((1, n_all), F32)] * 2 + [pltpu.VMEM((tm, width), F32)] * 2
        + [pltpu.VMEM((n_blk, tm, LANES), F32)], xchg=xchg)


def _ssm_backward(u, x_re, x_im, dcat, table, b_re, b_im, c_re, c_im, wz1, wz2, d_skip, g_out, name, xchg=None):
    n_rows = u.shape[1]
    n_blk, _, n_state = b_re.shape
    width, n_all = n_blk * LANES, n_blk * n_state
    tm = _row_tile(n_rows)
    n_t, seg = n_rows // tm, tm // SUBLANES

    def body(u_ref, xre_ref, xim_ref, dc_ref, tab_ref, bre_ref, bim_ref, cre_ref, cim_ref, wz1_ref, wz2_ref, d_ref, g_ref,
             du_ref, dg_ref, dd_ref, dwz1_ref, dwz2_ref, dcre_ref, dcim_ref, dbre_ref, dbim_ref, are_ref, aim_ref,
             car_re, car_im, gre, gim, ug, y_s, z1_s, sg_s, out_s, gy_s):
        @pl.when(pl.program_id(0) == 0)
        def _():
            for ref in (dg_ref, dd_ref, dwz1_ref, dwz2_ref, dcre_ref, dcim_ref, dbre_ref, dbim_ref, are_ref, aim_ref,
                        car_re, car_im):
                ref[...] = jnp.zeros_like(ref)

        _regroup(u_ref, ug, seg)
        ssq = None
        for j in range(n_blk):
            sl = slice(j * LANES, (j + 1) * LANES)
            st = slice(j * n_state, (j + 1) * n_state)
            yc = _dot(xre_ref[:, st].astype(_MXU), cre_ref[j]) - _dot(xim_ref[:, st].astype(_MXU), cim_ref[j])
            y = yc + d_ref[:, sl] * ug[:, sl]
            cdf, _ = _gelu_parts(y)
            gy = (y * cdf).astype(_MXU)
            z1 = _dot(gy, wz1_ref[j])
            sg = _sigmoid(_dot(gy, wz2_ref[j]))
            out = z1 * sg
            y_s[:, sl], z1_s[:, sl], sg_s[:, sl], out_s[:, sl], gy_s[:, sl] = y, z1, sg, out, gy
            part = jnp.sum(out * out, axis=-1, keepdims=True)
            ssq = part if ssq is None else ssq + part
        r = lax.rsqrt(ssq / width + RMS_EPS)
        ohat = out_s[...] * r
        _regroup(dc_ref, out_s, seg)
        dcv = out_s[...]
        dg_ref[...] += jnp.sum(dcv * ohat, axis=0, keepdims=True)
        doh = dcv * g_ref[...]
        out_s[...] = r * (doh - ohat * (jnp.sum(doh * ohat, axis=-1, keepdims=True) / width))

        for j in range(n_blk):
            sl = slice(j * LANES, (j + 1) * LANES)
            st = slice(j * n_state, (j + 1) * n_state)
            dout, sg, z1, y = out_s[:, sl], sg_s[:, sl], z1_s[:, sl], y_s[:, sl]
            dz1 = (dout * sg).astype(_MXU)
            dz2 = (dout * z1 * sg * (1.0 - sg)).astype(_MXU)
            gy = gy_s[:, sl]
            dwz1_ref[j] += _dot_tn(gy, dz1)
            dwz2_ref[j] += _dot_tn(gy, dz2)
            dgy = _dot_nt(dz1, wz1_ref[j]) + _dot_nt(dz2, wz2_ref[j])
            cdf, th = _gelu_parts(y)
            dy = dgy * (cdf + y * (0.5 * (1.0 - th * th) * GELU_C0 * (1.0 + 3.0 * GELU_C1 * (y * y))))
            uj = ug[:, sl]
            dd_ref[:, sl] += jnp.sum(dy * uj, axis=0, keepdims=True)
            z1_s[:, sl] = d_ref[:, sl] * dy
            dyb = dy.astype(_MXU)
            dcre_ref[j] += _dot_tn(dyb, xre_ref[:, st].astype(_MXU))
            dcim_ref[j] -= _dot_tn(dyb, xim_ref[:, st].astype(_MXU))
            gre[:, st] = _dot_nt(dyb, cre_ref[j])
            gim[:, st] = -_dot_nt(dyb, cim_ref[j])

        sums = _segment_scan(gre, gim, tab_ref, car_re, car_im, seg, reverse=True, x_refs=(xre_ref, xim_ref))
        for cb, (sum_re, sum_im) in enumerate(sums):
            cols = pl.ds(cb * SCAN_LANES, SCAN_LANES)
            are_ref[:, cols] += sum_re
            aim_ref[:, cols] += sum_im

        for j in range(n_blk):
            sl = slice(j * LANES, (j + 1) * LANES)
            st = slice(j * n_state, (j + 1) * n_state)
            ujb = ug[:, sl].astype(_MXU)
            grb, gib = gre[:, st].astype(_MXU), gim[:, st].astype(_MXU)
            dbre_ref[j] += _dot_tn(ujb, grb)
            dbim_ref[j] += _dot_tn(ujb, gib)
            z1_s[:, sl] += _dot_nt(grb, bre_ref[j]) + _dot_nt(gib, bim_ref[j])
        _ungroup(z1_s, du_ref, seg)

    half = _lane_blocks(n_blk, tm, lambda i: n_t - 1 - i)
    state = pl.BlockSpec((tm, n_all), lambda i: (n_t - 1 - i, 0))
    small = [(1, width), (1, width), wz1.shape, wz2.shape, (n_blk, LANES, n_state), (n_blk, LANES, n_state),
             (n_blk, LANES, n_state), (n_blk, LANES, n_state), (1, n_all), (1, n_all)]
    return _call(
        body, (u, x_re, x_im, dcat, table, b_re, b_im, c_re, c_im, wz1, wz2, d_skip, g_out), name=name, grid=(n_t,),
        out_shape=[jax.ShapeDtypeStruct((n_blk, n_rows, LANES), F32)] + [jax.ShapeDtypeStruct(s, F32) for s in small],
        in_specs=[half, state, state, half, _resident(), _full(b_re.shape), _full(b_im.shape), _full(c_re.shape),
                  _full(c_im.shape), _full(wz1.shape), _full(wz2.shape), _full((1, width)), _full((1, width))],
        out_specs=[half] + [_full(s) for s in small],
        scratch_shapes=[pltpu.VMEM((1, n_all), F32)] * 2 + [pltpu.VMEM((tm, n_all), F32)] * 2
        + [pltpu.VMEM((tm, width), F32)] * 5 + [pltpu.VMEM((tm, width), _MXU)], xchg=xchg)


def _pool_counts(tile, tm, window):
    t = tile * tm + lax.broadcasted_iota(jnp.int32, (tm, 1), 0)
    return jnp.minimum(t + 1, window).astype(F32)


def _pool_fwd(proj, pool_w, scale, g_out, name):
    n_rows = proj.shape[0]
    n_grp, grp, _ = pool_w.shape
    width = n_grp * grp
    tm = _row_tile(n_rows)

    def body(u_ref, pw_ref, sc_ref, g_ref, o_ref, ext, y_s):
        i = pl.program_id(0)

        @pl.when(i == 0)
        def _():
            ext[0:POOL_HALO, :] = jnp.zeros((POOL_HALO, width), F32)

        ext[POOL_HALO:, :] = u_ref[...]
        ssq = None
        for gi, w in enumerate(POOL_WINDOWS):
            sl = slice(gi * grp, (gi + 1) * grp)
            tot = ext[POOL_HALO:, sl]
            for k in range(1, w):
                tot = tot + ext[POOL_HALO - k:POOL_HALO - k + tm, sl]
            pooled = tot / _pool_counts(i, tm, w) - u_ref[:, sl]
            y = _dot(pooled.astype(_MXU), pw_ref[gi]) * sc_ref[:, sl]
            y_s[:, sl] = y
            part = jnp.sum(y * y, axis=-1, keepdims=True)
            ssq = part if ssq is None else ssq + part
        r = lax.rsqrt(ssq / width + RMS_EPS)
        o_ref[...] = (y_s[...] * r * g_ref[...]).astype(_MXU)
        ext[0:POOL_HALO, :] = u_ref[tm - POOL_HALO:, :]

    half_in = pl.BlockSpec((tm, width), lambda i: (i, 0))
    half = pl.BlockSpec((tm, width), lambda i: (i, 0))
    return pl.pallas_call(
        body, name=name, grid=(n_rows // tm,), out_shape=jax.ShapeDtypeStruct((n_rows, width), _MXU),
        in_specs=[half_in, _full(pool_w.shape), _full((1, width)), _full((1, width))], out_specs=half,
        scratch_shapes=[pltpu.VMEM((tm + POOL_HALO, width), F32), pltpu.VMEM((tm, width), F32)],
        compiler_params=_params(),
    )(proj, pool_w, scale, g_out)


def _pool_bwd(proj, dcat, pool_w, scale, g_out, name):
    n_rows = proj.shape[0]
    n_grp, grp, _ = pool_w.shape
    width = n_grp * grp
    tm = _row_tile(n_rows)
    n_t = n_rows // tm
    halo_blocks = tm // POOL_HALO

    def body(u_ref, up_ref, dc_ref, pw_ref, sc_ref, g_ref, du_ref, dg_ref, dsc_ref, dpw_ref, ext, qext, y_s, pl_s):
        i = pl.program_id(0)
        tile = n_t - 1 - i

        @pl.when(i == 0)
        def _():
            for ref in (dg_ref, dsc_ref, dpw_ref):
                ref[...] = jnp.zeros_like(ref)
            qext[tm:, :] = jnp.zeros((POOL_HALO, width), F32)

        ext[0:POOL_HALO, :] = jnp.where(tile > 0, up_ref[...], 0.0)
        ext[POOL_HALO:, :] = u_ref[...]
        ssq = None
        for gi, w in enumerate(POOL_WINDOWS):
            sl = slice(gi * grp, (gi + 1) * grp)
            tot = ext[POOL_HALO:, sl]
            for k in range(1, w):
                tot = tot + ext[POOL_HALO - k:POOL_HALO - k + tm, sl]
            pooled = (tot / _pool_counts(tile, tm, w) - u_ref[:, sl]).astype(_MXU)
            pl_s[:, sl] = pooled
            y0 = _dot(pooled, pw_ref[gi])
            y_s[:, sl] = y0
            y = y0 * sc_ref[:, sl]
            part = jnp.sum(y * y, axis=-1, keepdims=True)
            ssq = part if ssq is None else ssq + part
        r = lax.rsqrt(ssq / width + RMS_EPS)
        y0 = y_s[...]
        yhat = y0 * sc_ref[...] * r
        dcv = dc_ref[...]
        dg_ref[...] += jnp.sum(dcv * yhat, axis=0, keepdims=True)
        dyh = dcv * g_ref[...]
        dy = r * (dyh - yhat * (jnp.sum(dyh * yhat, axis=-1, keepdims=True) / width))
        dsc_ref[...] += jnp.sum(dy * y0, axis=0, keepdims=True)
        y_s[...] = dy * sc_ref[...]
        for gi, w in enumerate(POOL_WINDOWS):
            sl = slice(gi * grp, (gi + 1) * grp)
            dm = y_s[:, sl].astype(_MXU)
            dpw_ref[gi] += _dot_tn(pl_s[:, sl], dm)
            dpooled = _dot_nt(dm, pw_ref[gi])
            y_s[:, sl] = dpooled
            qext[0:tm, sl] = dpooled / _pool_counts(tile, tm, w)
        for gi, w in enumerate(POOL_WINDOWS):
            sl = slice(gi * grp, (gi + 1) * grp)
            tot = qext[0:tm, sl]
            for k in range(1, w):
                tot = tot + qext[k:k + tm, sl]
            du_ref[:, sl] = tot - y_s[:, sl]
        qext[tm:, :] = qext[0:POOL_HALO, :]

    half_in = pl.BlockSpec((tm, width), lambda i: (n_t - 1 - i, 0))
    prev = pl.BlockSpec((POOL_HALO, width), lambda i: (jnp.maximum((n_t - 1 - i) * halo_blocks - 1, 0), 0))
    half = pl.BlockSpec((tm, width), lambda i: (n_t - 1 - i, 0))
    return pl.pallas_call(
        body, name=name, grid=(n_t,),
        out_shape=[jax.ShapeDtypeStruct((n_rows, width), F32), jax.ShapeDtypeStruct((1, width), F32),
                   jax.ShapeDtypeStruct((1, width), F32), jax.ShapeDtypeStruct(pool_w.shape, F32)],
        in_specs=[half_in, prev, half, _full(pool_w.shape), _full((1, width)), _full((1, width))],
        out_specs=[half, _full((1, width)), _full((1, width)), _full(pool_w.shape)],
        scratch_shapes=[pltpu.VMEM((tm + POOL_HALO, width), F32), pltpu.VMEM((tm + POOL_HALO, width), F32),
                        pltpu.VMEM((tm, width), F32), pltpu.VMEM((tm, width), _MXU)],
        compiler_params=_params(),
    )(proj, proj, dcat, pool_w, scale, g_out)


def _mix_out_fwd(cat_s, cat_p, h, g, wo_s, wo_p, name):
    n_rows, d = h.shape
    width = cat_s.shape[1]
    tm = _row_tile(n_rows)

    def body(cs_ref, cp_ref, h_ref, g_ref, ws_ref, wp_ref, m_ref, ho_ref):
        m = _dot(cs_ref[...], ws_ref[...]) + _dot(cp_ref[...], wp_ref[...])
        m_ref[...] = m
        ho_ref[...] = h_ref[...] + m * _rs(m) * g_ref[...]

    row = pl.BlockSpec((tm, d), lambda i: (i, 0))
    half = pl.BlockSpec((tm, width), lambda i: (i, 0))
    return pl.pallas_call(
        body, name=name, grid=(n_rows // tm,), out_shape=[jax.ShapeDtypeStruct((n_rows, d), F32)] * 2,
        in_specs=[half, half, row, _full((1, d)), _resident(), _resident()], out_specs=[row, row],
        compiler_params=_params(),
    )(cat_s, cat_p, h, g, wo_s, wo_p)


def _mix_out_bwd(dho, mixed, g, wo_s, wo_p, name):
    n_rows, d = mixed.shape
    width = wo_s.shape[0]
    n_blk = width // LANES
    tm = _row_tile(n_rows)

    def body(dho_ref, m_ref, g_ref, ws_ref, wp_ref, dm_ref, dcs_ref, dcp_ref, dg_ref):
        @pl.when(pl.program_id(0) == 0)
        def _():
            dg_ref[...] = jnp.zeros_like(dg_ref)

        m = m_ref[...]
        r = _rs(m)
        mh = m * r
        dy = dho_ref[...]
        dg_ref[...] += jnp.sum(dy * mh, axis=0, keepdims=True)
        dmh = dy * g_ref[...]
        dm = (r * (dmh - mh * jnp.mean(dmh * mh, axis=-1, keepdims=True))).astype(_MXU)
        dm_ref[...] = dm
        dcs = _dot_nt(dm, ws_ref[...])
        for k in range(n_blk):
            dcs_ref[k] = dcs[:, k * LANES:(k + 1) * LANES]
        dcp_ref[...] = _dot_nt(dm, wp_ref[...])

    row = pl.BlockSpec((tm, d), lambda i: (i, 0))
    half = pl.BlockSpec((tm, width), lambda i: (i, 0))
    return pl.pallas_call(
        body, name=name, grid=(n_rows // tm,),
        out_shape=[jax.ShapeDtypeStruct((n_rows, d), _MXU), jax.ShapeDtypeStruct((n_blk, n_rows, LANES), F32),
                   jax.ShapeDtypeStruct((n_rows, width), F32), jax.ShapeDtypeStruct((1, d), F32)],
        in_specs=[row, row, _full((1, d)), _resident(), _resident()],
        out_specs=[row, _lane_blocks(n_blk, tm), half, _full((1, d))], compiler_params=_params(),
    )(dho, mixed, g, wo_s, wo_p)


def _mix_in_bwd(du_s, du_p, h, dho, g, wi_s, wi_p, name):
    n_rows, d = h.shape
    width = du_p.shape[1]
    n_blk = width // LANES
    tm = _row_tile(n_rows)

    def body(dus_ref, dup_ref, h_ref, dho_ref, g_ref, ws_ref, wp_ref, dh_ref, dp_ref, dg_ref):
        @pl.when(pl.program_id(0) == 0)
        def _():
            dg_ref[...] = jnp.zeros_like(dg_ref)

        for k in range(n_blk):
            dp_ref[:, k * LANES:(k + 1) * LANES] = dus_ref[k].astype(_MXU)
        dup = dup_ref[...].astype(_MXU)
        dp_ref[:, width:2 * width] = dup
        dn = _dot_nt(dp_ref[:, 0:width], ws_ref[...]) + _dot_nt(dup, wp_ref[...])
        hv = h_ref[...]
        r = _rs(hv)
        hh = hv * r
        dg_ref[...] += jnp.sum(dn * hh, axis=0, keepdims=True)
        dhh = dn * g_ref[...]
        dh_ref[...] = dho_ref[...] + r * (dhh - hh * jnp.mean(dhh * hh, axis=-1, keepdims=True))

    row = pl.BlockSpec((tm, d), lambda i: (i, 0))
    half = pl.BlockSpec((tm, width), lambda i: (i, 0))
    return pl.pallas_call(
        body, name=name, grid=(n_rows // tm,),
        out_shape=[jax.ShapeDtypeStruct((n_rows, d), F32), jax.ShapeDtypeStruct((n_rows, 2 * width), _MXU),
                   jax.ShapeDtypeStruct((1, d), F32)],
        in_specs=[_lane_blocks(n_blk, tm), half, row, row, _full((1, d)), _resident(), _resident()],
        out_specs=[row, pl.BlockSpec((tm, 2 * width), lambda i: (i, 0)), _full((1, d))], compiler_params=_params(),
    )(du_s, du_p, h, dho, g, wi_s, wi_p)


def _adamw_update(w_ref, gv, m_ref, v_ref, d_ref, mo_ref, vo_ref):
    mn = ADAM_B1 * m_ref[...] + (1.0 - ADAM_B1) * gv
    vn = ADAM_B2 * v_ref[...] + (1.0 - ADAM_B2) * (gv * gv)
    m_hat = mn / (1.0 - ADAM_B1 ** ADAM_STEP)
    v_hat = vn / (1.0 - ADAM_B2 ** ADAM_STEP)
    d_ref[...] = -ADAM_LR * (m_hat / (jnp.sqrt(v_hat) + ADAM_EPS) + ADAM_WD * w_ref[...])
    mo_ref[...] = mn
    vo_ref[...] = vn


def _adamw(w, g, m, v, name):
    def body(w_ref, g_ref, m_ref, v_ref, d_ref, mo_ref, vo_ref):
        _adamw_update(w_ref, g_ref[...], m_ref, v_ref, d_ref, mo_ref, vo_ref)

    spec = _full(w.shape)
    return pl.pallas_call(
        body, name=name, grid=(1,), out_shape=[jax.ShapeDtypeStruct(w.shape, F32)] * 3,
        in_specs=[spec] * 4, out_specs=[spec] * 3, compiler_params=_params(),
    )(w, g, m, v)


def _adamw_many(ws, gs, ms, vs, name):
    n = len(ws)

    def body(*refs):
        w_refs, g_refs, m_refs, v_refs, d_refs, mo_refs, vo_refs = _split(refs, (n,) * 7)
        for k in range(n):
            _adamw_update(w_refs[k], g_refs[k][...], m_refs[k], v_refs[k], d_refs[k], mo_refs[k], vo_refs[k])

    specs = [_full(w.shape) for w in ws]
    shapes = [jax.ShapeDtypeStruct(w.shape, F32) for w in ws]
    outs = pl.pallas_call(
        body, name=name, grid=(1,), out_shape=shapes * 3, in_specs=specs * 4, out_specs=specs * 3,
        compiler_params=_params(),
    )(*ws, *gs, *ms, *vs)
    return outs[:n], outs[n:2 * n], outs[2 * n:]


def _adamw_slots(w, slots, m, v, name, after=()):
    def body(w_ref, s_ref, m_ref, v_ref, *rest):
        g_ref, d_ref, mo_ref, vo_ref = rest[len(after):]
        gv = s_ref[0].astype(F32)
        for k in range(1, N_DEV):
            gv = gv + s_ref[k].astype(F32)
        g_ref[...] = gv
        _adamw_update(w_ref, gv, m_ref, v_ref, d_ref, mo_ref, vo_ref)

    spec = _full(w.shape)
    return pl.pallas_call(
        body, name=name, grid=(1,), out_shape=[jax.ShapeDtypeStruct(w.shape, F32)] * 4,
        in_specs=[spec, _full(slots.shape), spec, spec] + [pl.BlockSpec(memory_space=pl.ANY)] * len(after),
        out_specs=[spec] * 4, compiler_params=_params(),
    )(w, slots, m, v, *after)


def _discretize(lam_re, lam_im, log_dt, b_re, b_im):
    dt = jnp.exp(log_dt)[:, None]
    decay = jnp.exp(lam_re * dt)
    ang = lam_im * dt
    a_re = decay * jnp.cos(ang)
    a_im = decay * jnp.sin(ang)
    nr = a_re - 1.0
    den = lam_re * lam_re + lam_im * lam_im
    q_re = (nr * lam_re + a_im * lam_im) / den
    q_im = (a_im * lam_re - nr * lam_im) / den
    bb_re = q_re[..., None] * b_re - q_im[..., None] * b_im
    bb_im = q_re[..., None] * b_im + q_im[..., None] * b_re
    return a_re, a_im, bb_re, bb_im


def _cmul(a, b):
    return a[0] * b[0] - a[1] * b[1], a[0] * b[1] + a[1] * b[0]


def _power_table(a_re, a_im, count, reverse):
    base = (a_re.reshape(1, -1), (-a_im if reverse else a_im).reshape(1, -1))
    exponent = (jnp.arange(count, 0, -1) if reverse else jnp.arange(1, count + 1))[:, None]
    shape = (count, base[0].shape[1])
    res = (jnp.ones(shape, F32), jnp.zeros(shape, F32))
    for bit in range(int(count).bit_length()):
        prod = _cmul(res, base)
        take = ((exponent >> bit) & 1) == 1
        res = (jnp.where(take, prod[0], res[0]), jnp.where(take, prod[1], res[1]))
        base = _cmul(base, base)
    return jnp.broadcast_to(jnp.stack(res)[:, :, None, :], (2, count, SUBLANES, shape[1]))


def _block_diag(p, n_blk):
    g, r, c = p.shape
    per = g // n_blk
    eye = jnp.eye(per, dtype=p.dtype)
    return jnp.einsum("jgrc,gk->jgrkc", p.reshape(n_blk, per, r, c), eye).reshape(n_blk, per * r, per * c)


def _block_diag_t(m, g):
    n_blk = m.shape[0]
    per = g // n_blk
    r, c = m.shape[1] // per, m.shape[2] // per
    eye = jnp.eye(per, dtype=m.dtype)
    return jnp.einsum("jgrkc,gk->jgrc", m.reshape(n_blk, per, r, per, c), eye).reshape(g, r, c)


def _pack_rows(parts, cols, multiple):
    flat = jnp.concatenate([p.reshape(-1) for p in parts])
    size = -(-flat.shape[0] // (cols * multiple)) * cols * multiple
    return jnp.pad(flat, (0, size - flat.shape[0])).reshape(-1, cols)


def _unpack(flat, shapes):
    out, pos = [], 0
    flat = flat.reshape(-1)
    for s in shapes:
        n = int(np.prod(s))
        out.append(flat[pos:pos + n].reshape(s))
        pos += n
    return out


def kernel(x, meta_tokens, ffn1_pre_norm, ffn1_post_norm, ffn1_w_gate, ffn1_w_up, ffn1_w_down, mix_pre_norm, mix_post_norm, w_in, ssm_lambda_re, ssm_lambda_im, ssm_log_dt, ssm_b_re, ssm_b_im, ssm_c_re, ssm_c_im, ssm_d, ssm_w_glu, pool_w, pool_scale, ssm_out_norm, pool_out_norm, w_out, ffn2_pre_norm, ffn2_post_norm, ffn2_w_gate, ffn2_w_up, ffn2_w_down, loss_target, m_meta_tokens, m_ffn1_pre_norm, m_ffn1_post_norm, m_ffn1_w_gate, m_ffn1_w_up, m_ffn1_w_down, m_mix_pre_norm, m_mix_post_norm, m_w_in, m_ssm_lambda_re, m_ssm_lambda_im, m_ssm_log_dt, m_ssm_b_re, m_ssm_b_im, m_ssm_c_re, m_ssm_c_im, m_ssm_d, m_ssm_w_glu, m_pool_w, m_pool_scale, m_ssm_out_norm, m_pool_out_norm, m_w_out, m_ffn2_pre_norm, m_ffn2_post_norm, m_ffn2_w_gate, m_ffn2_w_up, m_ffn2_w_down, v_meta_tokens, v_ffn1_pre_norm, v_ffn1_post_norm, v_ffn1_w_gate, v_ffn1_w_up, v_ffn1_w_down, v_mix_pre_norm, v_mix_post_norm, v_w_in, v_ssm_lambda_re, v_ssm_lambda_im, v_ssm_log_dt, v_ssm_b_re, v_ssm_b_im, v_ssm_c_re, v_ssm_c_im, v_ssm_d, v_ssm_w_glu, v_pool_w, v_pool_scale, v_ssm_out_norm, v_pool_out_norm, v_w_out, v_ffn2_pre_norm, v_ffn2_post_norm, v_ffn2_w_gate, v_ffn2_w_up, v_ffn2_w_down):
    args = dict(locals())
    names = ["meta_tokens", "ffn1_pre_norm", "ffn1_post_norm", "ffn1_w_gate", "ffn1_w_up", "ffn1_w_down", "mix_pre_norm",
             "mix_post_norm", "w_in", "ssm_lambda_re", "ssm_lambda_im", "ssm_log_dt", "ssm_b_re", "ssm_b_im", "ssm_c_re",
             "ssm_c_im", "ssm_d", "ssm_w_glu", "pool_w", "pool_scale", "ssm_out_norm", "pool_out_norm", "w_out",
             "ffn2_pre_norm", "ffn2_post_norm", "ffn2_w_gate", "ffn2_w_up", "ffn2_w_down"]
    sharded = ("meta_tokens", "ffn1_w_gate", "ffn1_w_up", "ffn1_w_down", "w_in", "w_out", "ffn2_w_gate", "ffn2_w_up",
               "ffn2_w_down")
    small = [n for n in names if n not in sharded]

    d = x.shape[-1]
    width = d // 2
    n_grp = ssm_lambda_re.shape[1]
    n_blk = width // LANES

    def stacked(gathered):
        return gathered.reshape(-1, d).astype(_MXU)

    def chunked(m):
        return m.reshape(N_DEV, -1, d)

    s_gate1, s_up1, s_down1 = _to_wire([ffn1_w_gate[0].T, ffn1_w_up[0].T, ffn1_w_down[0]], "wire_ffn1")
    s_win, s_wout, s_gate2, s_up2, s_down2 = _to_wire(
        [w_in[0], w_out[0], ffn2_w_gate[0].T, ffn2_w_up[0].T, ffn2_w_down[0]], "wire_rest")
    g_gate1, g_up1, g_down1, meta_all = _gather_two_level([s_gate1, s_up1, s_down1, meta_tokens], "gather_ffn1")
    wgt1, wut1, wd1 = stacked(g_gate1), stacked(g_up1), stacked(g_down1)
    meta_full = jnp.transpose(meta_all, (1, 0, 2)).reshape(N_META, d)

    (a1, b1, f1, h1, h0), (g_win, g_wout, g_gate2) = _ffn_fwd(
        None, ffn1_pre_norm, ffn1_post_norm, wgt1, wut1, wd1, "ffn1_fwd",
        xchg=_Xchg([s_win, s_wout, s_gate2], ["gather"] * 3), meta=meta_full, tokens=x[0])
    w_in_f, w_out_f = stacked(g_win), stacked(g_wout)

    a_re, a_im, bb_re, bb_im = _discretize(ssm_lambda_re[0], ssm_lambda_im[0], ssm_log_dt[0], ssm_b_re[0], ssm_b_im[0])
    steps = _row_tile(N_META + x.shape[1]) // SUBLANES
    coef = _power_table(a_re, a_im, steps, reverse=False)
    coef_rev = _power_table(a_re, a_im, steps, reverse=True)
    bmat_re = _block_diag(jnp.swapaxes(bb_re, 1, 2), n_blk).astype(_MXU)
    bmat_im = _block_diag(jnp.swapaxes(bb_im, 1, 2), n_blk).astype(_MXU)
    cmat_re = _block_diag(jnp.swapaxes(ssm_c_re[0], 1, 2), n_blk).astype(_MXU)
    cmat_im = _block_diag(jnp.swapaxes(ssm_c_im[0], 1, 2), n_blk).astype(_MXU)
    wz1 = _block_diag(ssm_w_glu[0][:, :, :SSM_GROUP_CH], n_blk).astype(_MXU)
    wz2 = _block_diag(ssm_w_glu[0][:, :, SSM_GROUP_CH:], n_blk).astype(_MXU)
    pool_wm = pool_w[0].astype(_MXU)

    n2, u_s, u_p = _mix_in_fwd(h1, mix_pre_norm, w_in_f, "mix_in_fwd")
    (x_re, x_im, cat_s), (g_up2, g_down2) = _ssm_forward(
        u_s, coef, bmat_re, bmat_im, cmat_re, cmat_im, wz1, wz2, ssm_d, ssm_out_norm, "ssm_fwd",
        xchg=_Xchg([s_up2, s_down2], ["gather"] * 2))
    wgt2, wut2, wd2 = stacked(g_gate2), stacked(g_up2), stacked(g_down2)
    cat_p = _pool_fwd(u_p, pool_wm, pool_scale, pool_out_norm, "pool_fwd")
    wo_s, wo_p = w_out_f[:width], w_out_f[width:]
    mixed, h2 = _mix_out_fwd(cat_s, cat_p, h1, mix_post_norm, wo_s, wo_p, "mix_out_fwd")

    (a2, b2, f2, dh3, sq_err), _ = _ffn_fwd(
        h2, ffn2_pre_norm, ffn2_post_norm, wgt2, wut2, wd2, "ffn2_fwd", target=loss_target[0])

    g, slots = {}, {}
    (dh2, da2, db2, s2, df2, nf2, g["ffn2_pre_norm"], g["ffn2_post_norm"]), _ = _ffn_bwd(
        dh3, h2, f2, a2, b2, ffn2_pre_norm, ffn2_post_norm, wgt2, wut2, wd2, "ffn2_bwd")
    dgate2, _ = _wgrad(da2, nf2, "ffn2_dgate")
    dup2, _ = _wgrad(db2, nf2, "ffn2_dup")
    ddown2, _ = _wgrad(s2, df2, "ffn2_ddown")

    dmixed, dcat_s, dcat_p, g["mix_post_norm"] = _mix_out_bwd(dh2, mixed, mix_post_norm, wo_s, wo_p, "mix_out_bwd")
    dwout = jnp.concatenate([_wgrad(cat_s, dmixed, "dwout_s")[0], _wgrad(cat_p, dmixed, "dwout_p")[0]], axis=0)
    ((du_s, g["ssm_out_norm"], g["ssm_d"], dwz1, dwz2, dcm_re, dcm_im, dbm_re, dbm_im, acc_re, acc_im),
     (slots["ffn2_w_gate"], slots["ffn2_w_up"], slots["ffn2_w_down"], slots["w_out"])) = _ssm_backward(
        u_s, x_re, x_im, dcat_s, coef_rev, bmat_re, bmat_im, cmat_re, cmat_im, wz1, wz2, ssm_d, ssm_out_norm, "ssm_bwd",
        xchg=_Xchg([chunked(dgate2), chunked(dup2), chunked(ddown2), chunked(dwout)], ["scatter"] * 4))
    du_p, g["pool_out_norm"], g["pool_scale"], dpw = _pool_bwd(u_p, dcat_p, pool_wm, pool_scale, pool_out_norm, "pool_bwd")
    wi_s, wi_p = w_in_f[:, :width], w_in_f[:, width:]
    dh1, dproj, g["mix_pre_norm"] = _mix_in_bwd(du_s, du_p, h1, dh2, mix_pre_norm, wi_s, wi_p, "mix_in_bwd")
    dwin, _ = _wgrad(n2, dproj, "dwin")

    g["ssm_c_re"] = jnp.swapaxes(_block_diag_t(jnp.swapaxes(dcm_re, 1, 2), n_grp), 1, 2)[None]
    g["ssm_c_im"] = jnp.swapaxes(_block_diag_t(jnp.swapaxes(dcm_im, 1, 2), n_grp), 1, 2)[None]
    g["ssm_w_glu"] = jnp.concatenate([_block_diag_t(dwz1, n_grp), _block_diag_t(dwz2, n_grp)], axis=-1)[None]
    dbb_re = jnp.swapaxes(_block_diag_t(dbm_re, n_grp), 1, 2)
    dbb_im = jnp.swapaxes(_block_diag_t(dbm_im, n_grp), 1, 2)
    da_re, da_im = acc_re.reshape(a_re.shape), acc_im.reshape(a_re.shape)
    _, disc_vjp = jax.vjp(_discretize, ssm_lambda_re[0], ssm_lambda_im[0], ssm_log_dt[0], ssm_b_re[0], ssm_b_im[0])
    d_lre, d_lim, d_ldt, d_bre, d_bim = disc_vjp((da_re, da_im, dbb_re, dbb_im))
    g["ssm_lambda_re"], g["ssm_lambda_im"], g["ssm_log_dt"] = d_lre[None], d_lim[None], d_ldt[None]
    g["ssm_b_re"], g["ssm_b_im"] = d_bre[None], d_bim[None]
    g["pool_w"] = dpw[None]

    late = ["ffn1_pre_norm", "ffn1_post_norm"]
    early = [n for n in small if n not in late]
    early_vec = _pack_rows([g[n] for n in early] + [sq_err[:, :1]], 1024, SUBLANES)
    mix_started, mix_token = _split_start([chunked(dwin), early_vec], ["scatter", "gather"], "reduce_mix_start")
    (dx, da1, db1, s1, df1, nf1, g["ffn1_pre_norm"], g["ffn1_post_norm"], dmeta_part), _ = _ffn_bwd(
        dh1, h0, f1, a1, b1, ffn1_pre_norm, ffn1_post_norm, wgt1, wut1, wd1, "ffn1_bwd", split_meta=True,
        after=(mix_token,))
    slots["w_in"], recv_early = _split_wait(mix_started, [dmeta_part], "reduce_mix_wait")
    late_vec = _pack_rows([g[n] for n in late] + [dmeta_part], 1024, SUBLANES)
    dgate1, (recv_late,) = _wgrad(da1, nf1, "ffn1_dgate", xchg=_Xchg([late_vec], ["gather"]))
    dup1, (slots["ffn1_w_gate"],) = _wgrad(db1, nf1, "ffn1_dup", xchg=_Xchg([chunked(dgate1)], ["scatter"]))
    ddown1, (slots["ffn1_w_up"],) = _wgrad(s1, df1, "ffn1_ddown", xchg=_Xchg([chunked(dup1)], ["scatter"]))
    last_started, started = _split_start([chunked(ddown1)], ["scatter"], "reduce_last_start")

    summed = _unpack(_sum_slots(recv_early, "sum_small_grads"), [g[n].shape for n in early] + [(1,)])
    for n, val in zip(early, summed):
        g[n] = val
    loss = (0.5 / d) * summed[-1][0]
    g[late[0]], g[late[1]], dmeta = _unpack(_sum_slots(recv_late, "sum_last_grads"), [(1, d), (1, d), (N_META, d)])
    g["meta_tokens"] = lax.dynamic_slice_in_dim(dmeta, _my_slot() * (d // N_DEV), d // N_DEV, axis=1)

    delta, new_m, new_v = {}, {}, {}
    done = []
    for n in sorted(sharded, key=lambda n: n == "ffn1_w_down"):
        shape = args[n].shape
        two_d = (-1, shape[-1])
        w2, m2, v2 = args[n].reshape(two_d), args["m_" + n].reshape(two_d), args["v_" + n].reshape(two_d)
        if n == "meta_tokens":
            dl, mn, vn = _adamw(w2, g[n], m2, v2, "adamw_" + n)
        elif n.endswith("gate") or n.endswith("up"):
            gs, dl, mn, vn = [t.T for t in _adamw_slots(w2.T, slots[n], m2.T, v2.T, "adamw_" + n, after=(started,))]
            done.append(dl)
        elif n == "ffn1_w_down":
            (slots[n],) = _split_wait(last_started, done, "reduce_last_wait")
            gs, dl, mn, vn = _adamw_slots(w2, slots[n], m2, v2, "adamw_" + n)
        else:
            gs, dl, mn, vn = _adamw_slots(w2, slots[n], m2, v2, "adamw_" + n, after=(started,))
            done.append(dl)
        if n != "meta_tokens":
            g[n] = gs[None]
        delta[n], new_m[n], new_v[n] = dl.reshape(shape), mn.reshape(shape), vn.reshape(shape)
    outs = _adamw_many([args[n] for n in small], [g[n] for n in small], [args["m_" + n] for n in small],
                       [args["v_" + n] for n in small], "adamw_small")
    for store, vals in zip((delta, new_m, new_v), outs):
        for n, val in zip(small, vals):
            store[n] = val

    grad_x = dx[None]
    return (loss, grad_x, *[g[n] for n in names], *[delta[n] for n in names], *[new_m[n] for n in names],
            *[new_v[n] for n in names])
```

```python
import functools
import math

import jax
import jax.numpy as jnp
import numpy as np
from jax import lax
from jax.experimental import pallas as pl
from jax.experimental.pallas import tpu as pltpu

F32 = jnp.float32
_MXU = jnp.bfloat16
_ACT = jnp.bfloat16
_WIRE = jnp.bfloat16

N_DEV = 8
N_META = 16
RMS_EPS = 1e-6
SSM_GROUP_CH = 16
LANES = 128
SUBLANES = 8
POOL_WINDOWS = (2, 4, 8, 16)
POOL_HALO = 16
ADAM_LR = 0.001
ADAM_B1 = 0.9
ADAM_B2 = 0.999
ADAM_EPS = 1e-08
ADAM_WD = 0.01
ADAM_STEP = 10
GELU_C0 = math.sqrt(2.0 / math.pi)
GELU_C1 = 0.044715
VMEM_LIMIT = 62 * 1024 * 1024

_NT = (((1,), (1,)), ((), ()))
_TN = (((0,), (0,)), ((), ()))


def _dot(a, b):
    return jnp.dot(a, b, preferred_element_type=F32)


def _dot_nt(a, b):
    return lax.dot_general(a, b, _NT, preferred_element_type=F32)


def _dot_tn(a, b):
    return lax.dot_general(a, b, _TN, preferred_element_type=F32)


def _rs(x):
    return lax.rsqrt(jnp.mean(x * x, axis=-1, keepdims=True) + RMS_EPS)


def _sigmoid(x):
    return 0.5 * jnp.tanh(0.5 * x) + 0.5


def _row_tile(n_rows, largest=432):
    for t in (432, 304, 48, 16):
        if t <= largest and n_rows % t == 0:
            return t
    raise ValueError(n_rows)


def _ff_chunk(d_ff):
    return d_ff // 2 if (d_ff // 2) % LANES == 0 else d_ff


def _params(n_axes=1):
    return pltpu.CompilerParams(dimension_semantics=("arbitrary",) * n_axes, vmem_limit_bytes=VMEM_LIMIT)


def _resident():
    return pl.BlockSpec(memory_space=pltpu.VMEM)


def _full(shape):
    nd = len(shape)
    return pl.BlockSpec(shape, lambda *_: (0,) * nd)


def _lane_blocks(n_blk, tm, tile_of=lambda i: i):
    return pl.BlockSpec((n_blk, tm, LANES), lambda i: (0, tile_of(i), 0))


PEER_ORDER = (1, 2, 4, 3, 5, 6, 7)


def _split(refs, counts):
    out, pos = [], 0
    for n in counts:
        out.append(refs[pos:pos + n])
        pos += n
    return out


def _peer(r):
    x, y, c = lax.axis_index("x"), lax.axis_index("y"), lax.axis_index("c")
    return (1 - x if r & 4 else x, 1 - y if r & 2 else y, 1 - c if r & 1 else c)


def _my_slot():
    return 4 * lax.axis_index("x") + 2 * lax.axis_index("y") + lax.axis_index("c")


class _Xchg:
    def __init__(self, srcs, kinds):
        self.srcs, self.kinds, self.n = list(srcs), list(kinds), len(srcs)
        self.out_shape = [jax.ShapeDtypeStruct((N_DEV,) + s.shape if k == "gather" else s.shape, s.dtype)
                          for s, k in zip(self.srcs, self.kinds)]
        self.specs = [pl.BlockSpec(memory_space=pl.ANY)] * self.n
        self.scratch = [pltpu.SemaphoreType.DMA((self.n * (N_DEV - 1),)), pltpu.SemaphoreType.DMA((self.n * (N_DEV - 1),)),
                        pltpu.SemaphoreType.DMA((self.n,))]

    def copies(self, src, dst, sems):
        send_sems, recv_sems, local_sems = sems
        me = _my_slot()
        out = []
        for a in range(self.n):
            mine = src[a] if self.kinds[a] == "gather" else src[a].at[me]
            out.append(pltpu.make_async_copy(mine, dst[a].at[me], local_sems.at[a]))
            for r in PEER_ORDER:
                px, py, pc = _peer(r)
                part = src[a] if self.kinds[a] == "gather" else src[a].at[4 * px + 2 * py + pc]
                k = a * (N_DEV - 1) + r - 1
                out.append(pltpu.make_async_remote_copy(
                    src_ref=part, dst_ref=dst[a].at[me], send_sem=send_sems.at[k], recv_sem=recv_sems.at[k],
                    device_id=(px, py, pc), device_id_type=pl.DeviceIdType.MESH))
        return out

    def start(self, src, dst, sems):
        for cp in self.copies(src, dst, sems):
            cp.start()

    def wait(self, src, dst, sems):
        for cp in self.copies(src, dst, sems):
            cp.wait()


class _NoXchg:
    n, srcs, out_shape, specs, scratch = 0, [], [], [], []

    def start(self, *_):
        pass

    wait = start


def _call(body, args, *, name, grid, out_shape, in_specs, out_specs, scratch_shapes=(), xchg=None, after=()):
    xc = xchg or _NoXchg()
    counts = (len(in_specs), xc.n, len(after), len(out_shape), xc.n, len(scratch_shapes), len(xc.scratch))

    def wrapped(*refs):
        ins, xsrc, _, outs, xdst, scr, sems = _split(refs, counts)
        ids = [pl.program_id(k) for k in range(len(grid))]
        if xc.n:
            @pl.when(functools.reduce(jnp.logical_and, [i == 0 for i in ids]))
            def _():
                xc.start(xsrc, xdst, sems)

        body(*ins, *outs, *scr)
        if xc.n:
            @pl.when(functools.reduce(jnp.logical_and, [i == g - 1 for i, g in zip(ids, grid)]))
            def _():
                xc.wait(xsrc, xdst, sems)

    res = pl.pallas_call(
        wrapped, name=name, grid=grid, out_shape=list(out_shape) + xc.out_shape,
        in_specs=list(in_specs) + xc.specs + [pl.BlockSpec(memory_space=pl.ANY)] * len(after),
        out_specs=list(out_specs) + xc.specs,
        scratch_shapes=list(scratch_shapes) + xc.scratch, compiler_params=_params(len(grid)),
    )(*args, *xc.srcs, *after)
    return res[:len(out_shape)], res[len(out_shape):]


def _exchange(srcs, kinds, name):
    xc = _Xchg(srcs, kinds)

    def body(*refs):
        src, dst, sems = _split(refs, (xc.n, xc.n, 3))
        xc.start(src, dst, sems)
        xc.wait(src, dst, sems)

    return pl.pallas_call(body, name=name, out_shape=xc.out_shape, in_specs=xc.specs, out_specs=xc.specs,
                          scratch_shapes=xc.scratch)(*srcs)


def _split_copies(kinds, src, land, send_sems, recv_sems):
    me = _my_slot()
    out = []
    for a, kind in enumerate(kinds):
        for r in PEER_ORDER:
            px, py, pc = _peer(r)
            k = a * (N_DEV - 1) + r - 1
            out.append(pltpu.make_async_remote_copy(
                src_ref=src[a] if kind == "gather" else src[a].at[4 * px + 2 * py + pc], dst_ref=land[a].at[me],
                send_sem=send_sems.at[k], recv_sem=recv_sems.at[k], device_id=(px, py, pc),
                device_id_type=pl.DeviceIdType.MESH))
    return out


_SPLIT_EFFECT = pltpu.SideEffectType.DATAFLOW_SIDE_EFFECTING


def _split_start(srcs, kinds, name):
    n = len(srcs)
    hbm, sem = pl.BlockSpec(memory_space=pltpu.HBM), pl.BlockSpec(memory_space=pltpu.SEMAPHORE)
    lands = [lax.empty((N_DEV,) + s.shape if k == "gather" else s.shape, s.dtype) for s, k in zip(srcs, kinds)]

    def body(*refs):
        src, land, (send_sems, recv_sems), _, (token,) = _split(refs, (n, n, 2, 2 * n, 1))
        for cp in _split_copies(kinds, src, land, send_sems, recv_sems):
            cp.start()
        token[...] = jnp.zeros_like(token)

    n_sem = n * (N_DEV - 1)
    out = pl.pallas_call(
        body, name=name,
        out_shape=[pltpu.SemaphoreType.DMA((n_sem,)), pltpu.SemaphoreType.DMA((n_sem,))]
        + [pltpu.HBM(a.shape, a.dtype) for a in list(srcs) + lands] + [jax.ShapeDtypeStruct((SUBLANES, LANES), F32)],
        in_specs=[hbm] * (2 * n), out_specs=[sem, sem] + [hbm] * (2 * n) + [pl.BlockSpec(memory_space=pltpu.VMEM)],
        input_output_aliases={k: 2 + k for k in range(2 * n)},
        compiler_params=pltpu.CompilerParams(has_side_effects=_SPLIT_EFFECT),
    )(*[pltpu.with_memory_space_constraint(a, pltpu.HBM) for a in list(srcs) + lands])
    return (kinds, out[0], out[1], out[2:2 + n], out[2 + n:2 + 2 * n]), out[-1]


def _split_wait(started, after, name):
    kinds, send_sems, recv_sems, srcs, lands = started
    n = len(srcs)
    hbm, sem = pl.BlockSpec(memory_space=pltpu.HBM), pl.BlockSpec(memory_space=pltpu.SEMAPHORE)

    def body(*refs):
        src, land, (send_sems, recv_sems) = _split(refs, (n, n, 2))[:3]
        for cp in _split_copies(kinds, src, land, send_sems, recv_sems):
            cp.wait_send()
            cp.wait_recv()

    out = pl.pallas_call(
        body, name=name, out_shape=[pltpu.HBM(a.shape, a.dtype) for a in list(srcs) + list(lands)],
        in_specs=[hbm] * (2 * n) + [sem, sem] + [pl.BlockSpec(memory_space=pl.ANY)] * len(after),
        out_specs=[hbm] * (2 * n), input_output_aliases={k: k for k in range(2 * n)},
        compiler_params=pltpu.CompilerParams(has_side_effects=_SPLIT_EFFECT),
    )(*srcs, *lands, send_sems, recv_sems, *after)
    me = _my_slot()
    filled = []
    for kind, sent, land in zip(kinds, out[:n], out[n:]):
        mine = sent[None] if kind == "gather" else lax.dynamic_slice_in_dim(sent, me, 1, axis=0)
        filled.append(lax.dynamic_update_slice_in_dim(land, mine, me, axis=0))
    return filled


def _gather_two_level(srcs, name):
    n = len(srcs)
    out_shape = [jax.ShapeDtypeStruct((N_DEV,) + s.shape, s.dtype) for s in srcs]
    chips = (2, 4, 6)

    def body(*refs):
        src, dst, (send_sems, recv_sems, local_sems) = _split(refs, (n, n, 3))
        x, y, c = lax.axis_index("x"), lax.axis_index("y"), lax.axis_index("c")
        me = 4 * x + 2 * y + c
        sibling = (x, y, 1 - c)

        def copy(a, k, slot, to, from_src=False):
            return pltpu.make_async_remote_copy(
                src_ref=src[a] if from_src else dst[a].at[slot], dst_ref=dst[a].at[slot],
                send_sem=send_sems.at[a * 7 + k], recv_sem=recv_sems.at[a * 7 + k],
                device_id=to, device_id_type=pl.DeviceIdType.MESH)

        def slot_of(r, core):
            px, py, _ = _peer(r)
            return 4 * px + 2 * py + core

        local = [pltpu.make_async_copy(src[a], dst[a].at[me], local_sems.at[a]) for a in range(n)]
        sent = []
        for a in range(n):
            local[a].start()
            sent.append(copy(a, 0, me, sibling, from_src=True))
            sent += [copy(a, 1 + j, me, _peer(r), from_src=True) for j, r in enumerate(chips)]
        for cp in sent:
            cp.start()
        for j, r in enumerate(chips):
            for a in range(n):
                copy(a, 1 + j, slot_of(r, c), _peer(r)).wait_recv()
                cp = copy(a, 4 + j, slot_of(r, c), sibling)
                cp.start()
                sent.append(cp)
        for a in range(n):
            copy(a, 0, slot_of(0, 1 - c), sibling).wait_recv()
            for j, r in enumerate(chips):
                copy(a, 4 + j, slot_of(r, 1 - c), sibling).wait_recv()
        for cp in local:
            cp.wait()
        for cp in sent:
            cp.wait_send()

    any_spec = pl.BlockSpec(memory_space=pl.ANY)
    return pl.pallas_call(
        body, name=name, out_shape=out_shape, in_specs=[any_spec] * n, out_specs=[any_spec] * n,
        scratch_shapes=[pltpu.SemaphoreType.DMA((n * 7,)), pltpu.SemaphoreType.DMA((n * 7,)), pltpu.SemaphoreType.DMA((n,))],
    )(*srcs)


def _to_wire(mats, name):
    def body(*refs):
        for src, dst in zip(refs[:len(mats)], refs[len(mats):]):
            dst[...] = src[...].astype(_WIRE)

    return pl.pallas_call(
        body, name=name, grid=(1,), out_shape=[jax.ShapeDtypeStruct(m.shape, _WIRE) for m in mats],
        in_specs=[_full(m.shape) for m in mats], out_specs=[_full(m.shape) for m in mats], compiler_params=_params(),
    )(*mats)


def _sum_slots(r, name):
    _, rows, cols = r.shape
    blk = rows
    for cand in (rows, 592, 512, 256, 128, 64, 32, 16):
        if rows % cand == 0 and N_DEV * cand * cols * r.dtype.itemsize <= 8 * 1024 * 1024:
            blk = cand
            break

    def body(r_ref, o_ref):
        acc = r_ref[0].astype(F32)
        for d in range(1, N_DEV):
            acc = acc + r_ref[d].astype(F32)
        o_ref[...] = acc

    return pl.pallas_call(
        body, name=name, grid=(rows // blk,), out_shape=jax.ShapeDtypeStruct((rows, cols), F32),
        in_specs=[pl.BlockSpec((N_DEV, blk, cols), lambda i: (0, i, 0))],
        out_specs=pl.BlockSpec((blk, cols), lambda i: (i, 0)), compiler_params=_params(),
    )(r)


def _token_tile_copy(tokens_ref, buf, sems, i, tm, write=False):
    if isinstance(i, int) and i == 0:
        far, near = tokens_ref.at[pl.ds(0, tm - N_META)], buf.at[0, pl.ds(N_META, tm - N_META)]
    else:
        start = i * tm - N_META if isinstance(i, int) else pl.multiple_of(i * tm - N_META, SUBLANES)
        far, near = tokens_ref.at[pl.ds(start, tm)], buf.at[i % 2]
    return pltpu.make_async_copy(near, far, sems.at[i % 2]) if write else pltpu.make_async_copy(far, near, sems.at[i % 2])


def _fetch_token_tile(tokens_ref, buf, sems, i, n_t, tm):
    @pl.when(i == 0)
    def _():
        _token_tile_copy(tokens_ref, buf, sems, 0, tm).start()

    @pl.when(i + 1 < n_t)
    def _():
        _token_tile_copy(tokens_ref, buf, sems, i + 1, tm).start()

    @pl.when(i == 0)
    def _():
        _token_tile_copy(tokens_ref, buf, sems, 0, tm).wait()

    @pl.when(i > 0)
    def _():
        _token_tile_copy(tokens_ref, buf, sems, i, tm).wait()


def _ffn_fwd(h, g_pre, g_post, wgt, wut, wd, name, xchg=None, *, meta=None, tokens=None, target=None):
    first = h is None
    d = wd.shape[1]
    n_rows = N_META + tokens.shape[0] if first else h.shape[0]
    d_ff = wd.shape[0]
    tm, fc = _row_tile(n_rows), _ff_chunk(d_ff)
    n_t, n_c = n_rows // tm, d_ff // fc

    def body(src_ref, side_ref, gpre_ref, gpost_ref, wgt_ref, wut_ref, wd_ref, a_ref, b_ref, f_ref, o1_ref, o2_ref,
             n_scr, acc, buf, sems):
        i, c = pl.program_id(0), pl.program_id(1)
        slot = i % 2

        @pl.when(c == 0)
        def _():
            if first:
                _fetch_token_tile(side_ref, buf, sems, i, n_t, tm)

                @pl.when(i == 0)
                def _():
                    buf[0, 0:N_META, :] = src_ref[...]

                hv = buf[slot]
                o2_ref[...] = hv
            else:
                _fetch_token_tile(side_ref, buf, sems, i, n_t, tm)

                @pl.when(i == 0)
                def _():
                    buf[0, 0:N_META, :] = jnp.zeros((N_META, d), F32)
                    o2_ref[...] = jnp.zeros_like(o2_ref)

                hv = src_ref[...]
            n_scr[...] = (hv * _rs(hv) * gpre_ref[...]).astype(_MXU)
            acc[...] = jnp.zeros_like(acc)

        rows = pl.ds(pl.multiple_of(c * fc, fc), fc)
        nv = n_scr[...]
        a = _dot_nt(nv, wgt_ref[rows, :])
        b = _dot_nt(nv, wut_ref[rows, :])
        a_ref[...] = a.astype(_ACT)
        b_ref[...] = b.astype(_ACT)
        s = a * _sigmoid(a) * b
        acc[...] += _dot(s.astype(_MXU), wd_ref[rows, :])

        @pl.when(c == n_c - 1)
        def _():
            f = acc[...]
            f_ref[...] = f
            step = 0.5 * (f * _rs(f) * gpost_ref[...])
            if first:
                o1_ref[...] = buf[slot] + step
            else:
                row = i * tm + lax.broadcasted_iota(jnp.int32, (tm, 1), 0)
                err = jnp.where(row >= N_META, (src_ref[...] + step) - buf[slot], 0.0)
                o1_ref[...] = err / d
                o2_ref[...] += jnp.sum(jnp.sum(err * err, axis=0, keepdims=True), axis=1, keepdims=True)

    row = pl.BlockSpec((tm, d), lambda i, c: (i, 0))
    chunk = pl.BlockSpec((tm, fc), lambda i, c: (i, c))
    hbm = pl.BlockSpec(memory_space=pl.ANY)
    if first:
        operands, specs = (meta, tokens), [_full(meta.shape), hbm]
        last_shape, last_spec = jax.ShapeDtypeStruct((n_rows, d), F32), row
    else:
        operands, specs = (h, target), [row, hbm]
        last_shape, last_spec = jax.ShapeDtypeStruct((1, LANES), F32), pl.BlockSpec((1, LANES), lambda i, c: (0, 0))
    return _call(
        body, (*operands, g_pre, g_post, wgt, wut, wd), name=name, grid=(n_t, n_c),
        out_shape=[jax.ShapeDtypeStruct((n_rows, d_ff), _ACT), jax.ShapeDtypeStruct((n_rows, d_ff), _ACT),
                   jax.ShapeDtypeStruct((n_rows, d), F32), jax.ShapeDtypeStruct((n_rows, d), F32), last_shape],
        in_specs=specs + [_full((1, d)), _full((1, d)), _resident(), _resident(), _resident()],
        out_specs=[chunk, chunk, row, row, last_spec],
        scratch_shapes=[pltpu.VMEM((tm, d), _MXU), pltpu.VMEM((tm, d), F32), pltpu.VMEM((2, tm, d), F32),
                        pltpu.SemaphoreType.DMA((2,))], xchg=xchg)


def _ffn_bwd(dho, h, f, a, b, g_pre, g_post, wgt, wut, wd, name, xchg=None, *, split_meta=False, after=()):
    n_rows, d = h.shape
    d_ff = wd.shape[0]
    tm, fc = _row_tile(n_rows), _ff_chunk(d_ff)
    n_t, n_c = n_rows // tm, d_ff // fc
    assert n_t >= 2

    def body(dho_ref, h_ref, f_ref, a_ref, b_ref, gpre_ref, gpost_ref, wgt_ref, wut_ref, wd_ref,
             dh_ref, da_ref, db_ref, s_ref, df_ref, n_ref, dgpre_ref, dgpost_ref, *rest):
        dn_acc = rest[-1]
        i, c = pl.program_id(0), pl.program_id(1)

        @pl.when((i == 0) & (c == 0))
        def _():
            dgpre_ref[...] = jnp.zeros_like(dgpre_ref)
            dgpost_ref[...] = jnp.zeros_like(dgpost_ref)

        @pl.when(c == 0)
        def _():
            fv = f_ref[...]
            rf = _rs(fv)
            fhat = fv * rf
            dy = 0.5 * dho_ref[...]
            dgpost_ref[...] += jnp.sum(dy * fhat, axis=0, keepdims=True)
            dfhat = dy * gpost_ref[...]
            df = rf * (dfhat - fhat * jnp.mean(dfhat * fhat, axis=-1, keepdims=True))
            df_ref[...] = df.astype(_MXU)
            hv = h_ref[...]
            n_ref[...] = (hv * _rs(hv) * gpre_ref[...]).astype(_MXU)
            dn_acc[...] = jnp.zeros_like(dn_acc)

        rows = pl.ds(pl.multiple_of(c * fc, fc), fc)
        ds = _dot_nt(df_ref[...], wd_ref[rows, :])
        av = a_ref[...].astype(F32)
        bv = b_ref[...].astype(F32)
        sg = _sigmoid(av)
        si = av * sg
        da = (ds * bv * (sg * (1.0 + av * (1.0 - sg)))).astype(_MXU)
        db = (ds * si).astype(_MXU)
        da_ref[...] = da
        db_ref[...] = db
        s_ref[...] = (si * bv).astype(_MXU)
        dn_acc[...] += _dot(da, wgt_ref[rows, :]) + _dot(db, wut_ref[rows, :])

        @pl.when(c == n_c - 1)
        def _():
            dn = dn_acc[...]
            hv = h_ref[...]
            r = _rs(hv)
            hhat = hv * r
            dgpre_ref[...] += jnp.sum(dn * hhat, axis=0, keepdims=True)
            dhh = dn * gpre_ref[...]
            dh = dho_ref[...] + r * (dhh - hhat * jnp.mean(dhh * hhat, axis=-1, keepdims=True))
            if not split_meta:
                dh_ref[...] = dh
                return
            dmeta_ref, buf, sems = rest[0], rest[1], rest[2]

            def out_copy(k):
                return _token_tile_copy(dh_ref, buf, sems, k, tm, write=True)

            @pl.when(i == 2)
            def _():
                out_copy(0).wait()

            @pl.when(i > 2)
            def _():
                out_copy(i - 2).wait()

            buf[i % 2] = dh

            @pl.when(i == 0)
            def _():
                dmeta_ref[...] = buf[0, 0:N_META, :]
                out_copy(0).start()

            @pl.when(i > 0)
            def _():
                out_copy(i).start()

            @pl.when(i == n_t - 1)
            def _():
                out_copy(n_t - 2).wait()
                out_copy(n_t - 1).wait()

    row = pl.BlockSpec((tm, d), lambda i, c: (i, 0))
    chunk = pl.BlockSpec((tm, fc), lambda i, c: (i, c))
    vec = pl.BlockSpec((1, d), lambda i, c: (0, 0))
    shapes = [jax.ShapeDtypeStruct((n_rows, d_ff), _MXU)] * 3 + [jax.ShapeDtypeStruct((n_rows, d), _MXU)] * 2 \
        + [jax.ShapeDtypeStruct((1, d), F32)] * 2
    specs = [chunk, chunk, chunk, row, row, vec, vec]
    scratch = [pltpu.VMEM((tm, d), F32)]
    if split_meta:
        shapes = [jax.ShapeDtypeStruct((n_rows - N_META, d), F32)] + shapes + [jax.ShapeDtypeStruct((N_META, d), F32)]
        specs = [pl.BlockSpec(memory_space=pl.ANY)] + specs + [pl.BlockSpec((N_META, d), lambda i, c: (0, 0))]
        scratch = [pltpu.VMEM((2, tm, d), F32), pltpu.SemaphoreType.DMA((2,))] + scratch
    else:
        shapes, specs = [jax.ShapeDtypeStruct((n_rows, d), F32)] + shapes, [row] + specs
    return _call(
        body, (dho, h, f, a, b, g_pre, g_post, wgt, wut, wd), name=name, grid=(n_t, n_c), out_shape=shapes,
        in_specs=[row, row, row, chunk, chunk, vec, vec, _resident(), _resident(), _resident()], out_specs=specs,
        scratch_shapes=scratch, xchg=xchg, after=after)


def _wgrad(xm, ym, name, xchg=None, after=()):
    n_rows, a_dim = xm.shape
    b_dim = ym.shape[1]
    tk = n_rows
    for cand in (2736, 1296, 432, 48, 16):
        if n_rows % cand == 0:
            tk = cand
            break
    ta = a_dim
    for cand in (1408, 1024, 512):
        if a_dim % cand == 0:
            ta = cand
            break

    n_k = n_rows // tk

    def body(x_ref, y_ref, o_ref, acc):
        k = pl.program_id(1)

        @pl.when(k == 0)
        def _():
            acc[...] = jnp.zeros_like(acc)

        acc[...] += _dot_tn(x_ref[...], y_ref[...])

        @pl.when(k == n_k - 1)
        def _():
            o_ref[...] = acc[...].astype(o_ref.dtype)

    (out,), extra = _call(
        body, (xm, ym), name=name, grid=(a_dim // ta, n_k), out_shape=[jax.ShapeDtypeStruct((a_dim, b_dim), _WIRE)],
        in_specs=[pl.BlockSpec((tk, ta), lambda j, k: (k, j)), pl.BlockSpec((tk, b_dim), lambda j, k: (k, 0))],
        out_specs=[pl.BlockSpec((ta, b_dim), lambda j, k: (j, 0))], scratch_shapes=[pltpu.VMEM((ta, b_dim), F32)],
        xchg=xchg, after=after)
    return out, extra


def _mix_in_fwd(h, g, w_in, name):
    n_rows, d = h.shape
    tm = _row_tile(n_rows)
    width = w_in.shape[1] // 2
    n_blk = width // LANES

    def body(h_ref, g_ref, w_ref, n_ref, us_ref, up_ref):
        hv = h_ref[...]
        nv = (hv * _rs(hv) * g_ref[...]).astype(_MXU)
        n_ref[...] = nv
        p = _dot(nv, w_ref[...])
        for k in range(n_blk):
            us_ref[k] = p[:, k * LANES:(k + 1) * LANES]
        up_ref[...] = p[:, width:]

    row = pl.BlockSpec((tm, d), lambda i: (i, 0))
    half = pl.BlockSpec((tm, width), lambda i: (i, 0))
    return pl.pallas_call(
        body, name=name, grid=(n_rows // tm,),
        out_shape=[jax.ShapeDtypeStruct((n_rows, d), _MXU), jax.ShapeDtypeStruct((n_blk, n_rows, LANES), F32),
                   jax.ShapeDtypeStruct((n_rows, width), F32)],
        in_specs=[row, _full((1, d)), _resident()], out_specs=[row, _lane_blocks(n_blk, tm), half],
        compiler_params=_params(),
    )(h, g, w_in)


def _gelu_parts(y):
    th = jnp.tanh(GELU_C0 * (y + GELU_C1 * (y * y * y)))
    return 0.5 * (1.0 + th), th


SCAN_LANES = 512


def _regroup(src_ref, dst_ref, seg):
    for k in range(src_ref.shape[0]):
        for j in range(seg):
            dst_ref[j * SUBLANES:(j + 1) * SUBLANES, k * LANES:(k + 1) * LANES] = src_ref[k, pl.ds(j, SUBLANES, stride=seg), :]


def _ungroup(src_ref, dst_ref, seg):
    for k in range(dst_ref.shape[0]):
        for j in range(seg):
            dst_ref[k, pl.ds(j, SUBLANES, stride=seg), :] = src_ref[j * SUBLANES:(j + 1) * SUBLANES, k * LANES:(k + 1) * LANES]


def _rows_to_sublanes(rows):
    rid = lax.broadcasted_iota(jnp.int32, (SUBLANES, rows[0].shape[1]), 0)
    out = jnp.broadcast_to(rows[0], rid.shape)
    for s in range(1, SUBLANES):
        out = jnp.where(rid == s, rows[s], out)
    return out


def _segment_scan(re_ref, im_ref, tab_ref, car_re, car_im, seg, reverse, x_refs=None):
    n_all = re_ref.shape[1]
    first = seg - 1 if reverse else 0
    sums = []
    for cb in range(n_all // SCAN_LANES):
        cols = pl.ds(cb * SCAN_LANES, SCAN_LANES)
        ar, ai = tab_ref[0, first, :, cols], tab_ref[1, first, :, cols]

        def local(t, carry, cols=cols, ar=ar, ai=ai):
            r0 = pl.multiple_of((seg - 1 - t if reverse else t) * SUBLANES, SUBLANES)
            xr, xi = carry
            nr = ar * xr - ai * xi + re_ref[pl.ds(r0, SUBLANES), cols]
            ni = ar * xi + ai * xr + im_ref[pl.ds(r0, SUBLANES), cols]
            re_ref[pl.ds(r0, SUBLANES), cols] = nr
            im_ref[pl.ds(r0, SUBLANES), cols] = ni
            return nr, ni

        zero = jnp.zeros((SUBLANES, SCAN_LANES), F32)
        fr, fi = lax.fori_loop(0, seg, local, (zero, zero))

        last = 0 if reverse else seg - 1
        sr, si = tab_ref[0, last, 0:1, cols], tab_ref[1, last, 0:1, cols]
        cr, ci = car_re[:, cols], car_im[:, cols]
        rows_r, rows_i = [None] * SUBLANES, [None] * SUBLANES
        for s in (range(SUBLANES - 1, -1, -1) if reverse else range(SUBLANES)):
            rows_r[s], rows_i[s] = cr, ci
            cr, ci = sr * cr - si * ci + fr[s:s + 1], sr * ci + si * cr + fi[s:s + 1]
        car_re[:, cols] = cr
        car_im[:, cols] = ci
        cmr, cmi = _rows_to_sublanes(rows_r), _rows_to_sublanes(rows_i)

        def fix(t, carry, cols=cols, cmr=cmr, cmi=cmi):
            j = seg - 1 - t if reverse else t
            r0 = pl.multiple_of(j * SUBLANES, SUBLANES)
            pr, pi = tab_ref[0, j, :, cols], tab_ref[1, j, :, cols]
            gr = re_ref[pl.ds(r0, SUBLANES), cols] + (pr * cmr - pi * cmi)
            gi = im_ref[pl.ds(r0, SUBLANES), cols] + (pr * cmi + pi * cmr)
            re_ref[pl.ds(r0, SUBLANES), cols] = gr
            im_ref[pl.ds(r0, SUBLANES), cols] = gi
            if x_refs is None:
                return carry
            nxr, nxi, accr, acci = carry
            xr, xi = x_refs[0][pl.ds(r0, SUBLANES), cols], x_refs[1][pl.ds(r0, SUBLANES), cols]
            return gr, gi, accr + (xr * nxr + xi * nxi), acci + (xr * nxi - xi * nxr)

        if x_refs is None:
            lax.fori_loop(0, seg, fix, 0)
        else:
            fin = lax.fori_loop(0, seg, fix, (cmr, cmi, zero, zero))
            sums.append((jnp.sum(fin[2], axis=0, keepdims=True), jnp.sum(fin[3], axis=0, keepdims=True)))
    return sums


def _ssm_forward(u, table, b_re, b_im, c_re, c_im, wz1, wz2, d_skip, g_out, name, xchg=None):
    n_rows = u.shape[1]
    n_blk, _, n_state = b_re.shape
    width, n_all = n_blk * LANES, n_blk * n_state
    tm = _row_tile(n_rows)
    seg = tm // SUBLANES

    def body(u_ref, tab_ref, bre_ref, bim_ref, cre_ref, cim_ref, wz1_ref, wz2_ref, d_ref, g_ref,
             xre_ref, xim_ref, o_ref, car_re, car_im, ug, out_scr, blocks):
        @pl.when(pl.program_id(0) == 0)
        def _():
            car_re[...] = jnp.zeros_like(car_re)
            car_im[...] = jnp.zeros_like(car_im)

        _regroup(u_ref, ug, seg)
        ub = ug[...].astype(_MXU)
        for j in range(n_blk):
            uj = ub[:, j * LANES:(j + 1) * LANES]
            xre_ref[:, j * n_state:(j + 1) * n_state] = _dot(uj, bre_ref[j])
            xim_ref[:, j * n_state:(j + 1) * n_state] = _dot(uj, bim_ref[j])
        _segment_scan(xre_ref, xim_ref, tab_ref, car_re, car_im, seg, reverse=False)

        ssq = None
        for j in range(n_blk):
            sl = slice(j * LANES, (j + 1) * LANES)
            st = slice(j * n_state, (j + 1) * n_state)
            yc = _dot(xre_ref[:, st].astype(_MXU), cre_ref[j]) - _dot(xim_ref[:, st].astype(_MXU), cim_ref[j])
            y = yc + d_ref[:, sl] * ug[:, sl]
            cdf, _ = _gelu_parts(y)
            gy = (y * cdf).astype(_MXU)
            out = _dot(gy, wz1_ref[j]) * _sigmoid(_dot(gy, wz2_ref[j]))
            out_scr[:, sl] = out
            part = jnp.sum(out * out, axis=-1, keepdims=True)
            ssq = part if ssq is None else ssq + part
        r = lax.rsqrt(ssq / width + RMS_EPS)
        out_scr[...] = out_scr[...] * r * g_ref[...]
        _ungroup(out_scr, blocks, seg)
        for k in range(n_blk):
            o_ref[:, k * LANES:(k + 1) * LANES] = blocks[k].astype(_MXU)

    half = pl.BlockSpec((tm, width), lambda i: (i, 0))
    state = pl.BlockSpec((tm, n_all), lambda i: (i, 0))
    return _call(
        body, (u, table, b_re, b_im, c_re, c_im, wz1, wz2, d_skip, g_out), name=name, grid=(n_rows // tm,),
        out_shape=[jax.ShapeDtypeStruct((n_rows, n_all), F32)] * 2 + [jax.ShapeDtypeStruct((n_rows, width), _MXU)],
        in_specs=[_lane_blocks(n_blk, tm), _resident(), _full(b_re.shape), _full(b_im.shape), _full(c_re.shape),
                  _full(c_im.shape), _full(wz1.shape), _full(wz2.shape), _full((1, width)), _full((1, width))],
        out_specs=[state, state, half],
        scratch_shapes=[pltpu.VMEM((1, n_all), F32)] * 2 + [pltpu.VMEM((tm, width), F32)] * 2
        + [pltpu.VMEM((n_blk, tm, LANES), F32)], xchg=xchg)


def _ssm_backward(u, x_re, x_im, dcat, table, b_re, b_im, c_re, c_im, wz1, wz2, d_skip, g_out, name, xchg=None):
    n_rows = u.shape[1]
    n_blk, _, n_state = b_re.shape
    width, n_all = n_blk * LANES, n_blk * n_state
    tm = _row_tile(n_rows)
    n_t, seg = n_rows // tm, tm // SUBLANES

    def body(u_ref, xre_ref, xim_ref, dc_ref, tab_ref, bre_ref, bim_ref, cre_ref, cim_ref, wz1_ref, wz2_ref, d_ref, g_ref,
             du_ref, dg_ref, dd_ref, dwz1_ref, dwz2_ref, dcre_ref, dcim_ref, dbre_ref, dbim_ref, are_ref, aim_ref,
             car_re, car_im, gre, gim, ug, y_s, z1_s, sg_s, out_s, gy_s):
        @pl.when(pl.program_id(0) == 0)
        def _():
            for ref in (dg_ref, dd_ref, dwz1_ref, dwz2_ref, dcre_ref, dcim_ref, dbre_ref, dbim_ref, are_ref, aim_ref,
                        car_re, car_im):
                ref[...] = jnp.zeros_like(ref)

        _regroup(u_ref, ug, seg)
        ssq = None
        for j in range(n_blk):
            sl = slice(j * LANES, (j + 1) * LANES)
            st = slice(j * n_state, (j + 1) * n_state)
            yc = _dot(xre_ref[:, st].astype(_MXU), cre_ref[j]) - _dot(xim_ref[:, st].astype(_MXU), cim_ref[j])
            y = yc + d_ref[:, sl] * ug[:, sl]
            cdf, _ = _gelu_parts(y)
            gy = (y * cdf).astype(_MXU)
            z1 = _dot(gy, wz1_ref[j])
            sg = _sigmoid(_dot(gy, wz2_ref[j]))
            out = z1 * sg
            y_s[:, sl], z1_s[:, sl], sg_s[:, sl], out_s[:, sl], gy_s[:, sl] = y, z1, sg, out, gy
            part = jnp.sum(out * out, axis=-1, keepdims=True)
            ssq = part if ssq is None else ssq + part
        r = lax.rsqrt(ssq / width + RMS_EPS)
        ohat = out_s[...] * r
        _regroup(dc_ref, out_s, seg)
        dcv = out_s[...]
        dg_ref[...] += jnp.sum(dcv * ohat, axis=0, keepdims=True)
        doh = dcv * g_ref[...]
        out_s[...] = r * (doh - ohat * (jnp.sum(doh * ohat, axis=-1, keepdims=True) / width))

        for j in range(n_blk):
            sl = slice(j * LANES, (j + 1) * LANES)
            st = slice(j * n_state, (j + 1) * n_state)
            dout, sg, z1, y = out_s[:, sl], sg_s[:, sl], z1_s[:, sl], y_s[:, sl]
            dz1 = (dout * sg).astype(_MXU)
            dz2 = (dout * z1 * sg * (1.0 - sg)).astype(_MXU)
            gy = gy_s[:, sl]
            dwz1_ref[j] += _dot_tn(gy, dz1)
            dwz2_ref[j] += _dot_tn(gy, dz2)
            dgy = _dot_nt(dz1, wz1_ref[j]) + _dot_nt(dz2, wz2_ref[j])
            cdf, th = _gelu_parts(y)
            dy = dgy * (cdf + y * (0.5 * (1.0 - th * th) * GELU_C0 * (1.0 + 3.0 * GELU_C1 * (y * y))))
            uj = ug[:, sl]
            dd_ref[:, sl] += jnp.sum(dy * uj, axis=0, keepdims=True)
            z1_s[:, sl] = d_ref[:, sl] * dy
            dyb = dy.astype(_MXU)
            dcre_ref[j] += _dot_tn(dyb, xre_ref[:, st].astype(_MXU))
            dcim_ref[j] -= _dot_tn(dyb, xim_ref[:, st].astype(_MXU))
            gre[:, st] = _dot_nt(dyb, cre_ref[j])
            gim[:, st] = -_dot_nt(dyb, cim_ref[j])

        sums = _segment_scan(gre, gim, tab_ref, car_re, car_im, seg, reverse=True, x_refs=(xre_ref, xim_ref))
        for cb, (sum_re, sum_im) in enumerate(sums):
            cols = pl.ds(cb * SCAN_LANES, SCAN_LANES)
            are_ref[:, cols] += sum_re
            aim_ref[:, cols] += sum_im

        for j in range(n_blk):
            sl = slice(j * LANES, (j + 1) * LANES)
            st = slice(j * n_state, (j + 1) * n_state)
            ujb = ug[:, sl].astype(_MXU)
            grb, gib = gre[:, st].astype(_MXU), gim[:, st].astype(_MXU)
            dbre_ref[j] += _dot_tn(ujb, grb)
            dbim_ref[j] += _dot_tn(ujb, gib)
            z1_s[:, sl] += _dot_nt(grb, bre_ref[j]) + _dot_nt(gib, bim_ref[j])
        _ungroup(z1_s, du_ref, seg)

    half = _lane_blocks(n_blk, tm, lambda i: n_t - 1 - i)
    state = pl.BlockSpec((tm, n_all), lambda i: (n_t - 1 - i, 0))
    small = [(1, width), (1, width), wz1.shape, wz2.shape, (n_blk, LANES, n_state), (n_blk, LANES, n_state),
             (n_blk, LANES, n_state), (n_blk, LANES, n_state), (1, n_all), (1, n_all)]
    return _call(
        body, (u, x_re, x_im, dcat, table, b_re, b_im, c_re, c_im, wz1, wz2, d_skip, g_out), name=name, grid=(n_t,),
        out_shape=[jax.ShapeDtypeStruct((n_blk, n_rows, LANES), F32)] + [jax.ShapeDtypeStruct(s, F32) for s in small],
        in_specs=[half, state, state, half, _resident(), _full(b_re.shape), _full(b_im.shape), _full(c_re.shape),
                  _full(c_im.shape), _full(wz1.shape), _full(wz2.shape), _full((1, width)), _full((1, width))],
        out_specs=[half] + [_full(s) for s in small],
        scratch_shapes=[pltpu.VMEM((1, n_all), F32)] * 2 + [pltpu.VMEM((tm, n_all), F32)] * 2
        + [pltpu.VMEM((tm, width), F32)] * 5 + [pltpu.VMEM((tm, width), _MXU)], xchg=xchg)


def _pool_counts(tile, tm, window):
    t = tile * tm + lax.broadcasted_iota(jnp.int32, (tm, 1), 0)
    return jnp.minimum(t + 1, window).astype(F32)


def _pool_fwd(proj, pool_w, scale, g_out, name):
    n_rows = proj.shape[0]
    n_grp, grp, _ = pool_w.shape
    width = n_grp * grp
    tm = _row_tile(n_rows)

    def body(u_ref, pw_ref, sc_ref, g_ref, o_ref, ext, y_s):
        i = pl.program_id(0)

        @pl.when(i == 0)
        def _():
            ext[0:POOL_HALO, :] = jnp.zeros((POOL_HALO, width), F32)

        ext[POOL_HALO:, :] = u_ref[...]
        ssq = None
        for gi, w in enumerate(POOL_WINDOWS):
            sl = slice(gi * grp, (gi + 1) * grp)
            tot = ext[POOL_HALO:, sl]
            for k in range(1, w):
                tot = tot + ext[POOL_HALO - k:POOL_HALO - k + tm, sl]
            pooled = tot / _pool_counts(i, tm, w) - u_ref[:, sl]
            y = _dot(pooled.astype(_MXU), pw_ref[gi]) * sc_ref[:, sl]
            y_s[:, sl] = y
            part = jnp.sum(y * y, axis=-1, keepdims=True)
            ssq = part if ssq is None else ssq + part
        r = lax.rsqrt(ssq / width + RMS_EPS)
        o_ref[...] = (y_s[...] * r * g_ref[...]).astype(_MXU)
        ext[0:POOL_HALO, :] = u_ref[tm - POOL_HALO:, :]

    half_in = pl.BlockSpec((tm, width), lambda i: (i, 0))
    half = pl.BlockSpec((tm, width), lambda i: (i, 0))
    return pl.pallas_call(
        body, name=name, grid=(n_rows // tm,), out_shape=jax.ShapeDtypeStruct((n_rows, width), _MXU),
        in_specs=[half_in, _full(pool_w.shape), _full((1, width)), _full((1, width))], out_specs=half,
        scratch_shapes=[pltpu.VMEM((tm + POOL_HALO, width), F32), pltpu.VMEM((tm, width), F32)],
        compiler_params=_params(),
    )(proj, pool_w, scale, g_out)


def _pool_bwd(proj, dcat, pool_w, scale, g_out, name):
    n_rows = proj.shape[0]
    n_grp, grp, _ = pool_w.shape
    width = n_grp * grp
    tm = _row_tile(n_rows)
    n_t = n_rows // tm
    halo_blocks = tm // POOL_HALO

    def body(u_ref, up_ref, dc_ref, pw_ref, sc_ref, g_ref, du_ref, dg_ref, dsc_ref, dpw_ref, ext, qext, y_s, pl_s):
        i = pl.program_id(0)
        tile = n_t - 1 - i

        @pl.when(i == 0)
        def _():
            for ref in (dg_ref, dsc_ref, dpw_ref):
                ref[...] = jnp.zeros_like(ref)
            qext[tm:, :] = jnp.zeros((POOL_HALO, width), F32)

        ext[0:POOL_HALO, :] = jnp.where(tile > 0, up_ref[...], 0.0)
        ext[POOL_HALO:, :] = u_ref[...]
        ssq = None
        for gi, w in enumerate(POOL_WINDOWS):
            sl = slice(gi * grp, (gi + 1) * grp)
            tot = ext[POOL_HALO:, sl]
            for k in range(1, w):
                tot = tot + ext[POOL_HALO - k:POOL_HALO - k + tm, sl]
            pooled = (tot / _pool_counts(tile, tm, w) - u_ref[:, sl]).astype(_MXU)
            pl_s[:, sl] = pooled
            y0 = _dot(pooled, pw_ref[gi])
            y_s[:, sl] = y0
            y = y0 * sc_ref[:, sl]
            part = jnp.sum(y * y, axis=-1, keepdims=True)
            ssq = part if ssq is None else ssq + part
        r = lax.rsqrt(ssq / width + RMS_EPS)
        y0 = y_s[...]
        yhat = y0 * sc_ref[...] * r
        dcv = dc_ref[...]
        dg_ref[...] += jnp.sum(dcv * yhat, axis=0, keepdims=True)
        dyh = dcv * g_ref[...]
        dy = r * (dyh - yhat * (jnp.sum(dyh * yhat, axis=-1, keepdims=True) / width))
        dsc_ref[...] += jnp.sum(dy * y0, axis=0, keepdims=True)
        y_s[...] = dy * sc_ref[...]
        for gi, w in enumerate(POOL_WINDOWS):
            sl = slice(gi * grp, (gi + 1) * grp)
            dm = y_s[:, sl].astype(_MXU)
            dpw_ref[gi] += _dot_tn(pl_s[:, sl], dm)
            dpooled = _dot_nt(dm, pw_ref[gi])
            y_s[:, sl] = dpooled
            qext[0:tm, sl] = dpooled / _pool_counts(tile, tm, w)
        for gi, w in enumerate(POOL_WINDOWS):
            sl = slice(gi * grp, (gi + 1) * grp)
            tot = qext[0:tm, sl]
            for k in range(1, w):
                tot = tot + qext[k:k + tm, sl]
            du_ref[:, sl] = tot - y_s[:, sl]
        qext[tm:, :] = qext[0:POOL_HALO, :]

    half_in = pl.BlockSpec((tm, width), lambda i: (n_t - 1 - i, 0))
    prev = pl.BlockSpec((POOL_HALO, width), lambda i: (jnp.maximum((n_t - 1 - i) * halo_blocks - 1, 0), 0))
    half = pl.BlockSpec((tm, width), lambda i: (n_t - 1 - i, 0))
    return pl.pallas_call(
        body, name=name, grid=(n_t,),
        out_shape=[jax.ShapeDtypeStruct((n_rows, width), F32), jax.ShapeDtypeStruct((1, width), F32),
                   jax.ShapeDtypeStruct((1, width), F32), jax.ShapeDtypeStruct(pool_w.shape, F32)],
        in_specs=[half_in, prev, half, _full(pool_w.shape), _full((1, width)), _full((1, width))],
        out_specs=[half, _full((1, width)), _full((1, width)), _full(pool_w.shape)],
        scratch_shapes=[pltpu.VMEM((tm + POOL_HALO, width), F32), pltpu.VMEM((tm + POOL_HALO, width), F32),
                        pltpu.VMEM((tm, width), F32), pltpu.VMEM((tm, width), _MXU)],
        compiler_params=_params(),
    )(proj, proj, dcat, pool_w, scale, g_out)


def _mix_out_fwd(cat_s, cat_p, h, g, wo_s, wo_p, name):
    n_rows, d = h.shape
    width = cat_s.shape[1]
    tm = _row_tile(n_rows)

    def body(cs_ref, cp_ref, h_ref, g_ref, ws_ref, wp_ref, m_ref, ho_ref):
        m = _dot(cs_ref[...], ws_ref[...]) + _dot(cp_ref[...], wp_ref[...])
        m_ref[...] = m
        ho_ref[...] = h_ref[...] + m * _rs(m) * g_ref[...]

    row = pl.BlockSpec((tm, d), lambda i: (i, 0))
    half = pl.BlockSpec((tm, width), lambda i: (i, 0))
    return pl.pallas_call(
        body, name=name, grid=(n_rows // tm,), out_shape=[jax.ShapeDtypeStruct((n_rows, d), F32)] * 2,
        in_specs=[half, half, row, _full((1, d)), _resident(), _resident()], out_specs=[row, row],
        compiler_params=_params(),
    )(cat_s, cat_p, h, g, wo_s, wo_p)


def _mix_out_bwd(dho, mixed, g, wo_s, wo_p, name):
    n_rows, d = mixed.shape
    width = wo_s.shape[0]
    n_blk = width // LANES
    tm = _row_tile(n_rows)

    def body(dho_ref, m_ref, g_ref, ws_ref, wp_ref, dm_ref, dcs_ref, dcp_ref, dg_ref):
        @pl.when(pl.program_id(0) == 0)
        def _():
            dg_ref[...] = jnp.zeros_like(dg_ref)

        m = m_ref[...]
        r = _rs(m)
        mh = m * r
        dy = dho_ref[...]
        dg_ref[...] += jnp.sum(dy * mh, axis=0, keepdims=True)
        dmh = dy * g_ref[...]
        dm = (r * (dmh - mh * jnp.mean(dmh * mh, axis=-1, keepdims=True))).astype(_MXU)
        dm_ref[...] = dm
        dcs = _dot_nt(dm, ws_ref[...])
        for k in range(n_blk):
            dcs_ref[k] = dcs[:, k * LANES:(k + 1) * LANES]
        dcp_ref[...] = _dot_nt(dm, wp_ref[...])

    row = pl.BlockSpec((tm, d), lambda i: (i, 0))
    half = pl.BlockSpec((tm, width), lambda i: (i, 0))
    return pl.pallas_call(
        body, name=name, grid=(n_rows // tm,),
        out_shape=[jax.ShapeDtypeStruct((n_rows, d), _MXU), jax.ShapeDtypeStruct((n_blk, n_rows, LANES), F32),
                   jax.ShapeDtypeStruct((n_rows, width), F32), jax.ShapeDtypeStruct((1, d), F32)],
        in_specs=[row, row, _full((1, d)), _resident(), _resident()],
        out_specs=[row, _lane_blocks(n_blk, tm), half, _full((1, d))], compiler_params=_params(),
    )(dho, mixed, g, wo_s, wo_p)


def _mix_in_bwd(du_s, du_p, h, dho, g, wi_s, wi_p, name):
    n_rows, d = h.shape
    width = du_p.shape[1]
    n_blk = width // LANES
    tm = _row_tile(n_rows)

    def body(dus_ref, dup_ref, h_ref, dho_ref, g_ref, ws_ref, wp_ref, dh_ref, dp_ref, dg_ref):
        @pl.when(pl.program_id(0) == 0)
        def _():
            dg_ref[...] = jnp.zeros_like(dg_ref)

        for k in range(n_blk):
            dp_ref[:, k * LANES:(k + 1) * LANES] = dus_ref[k].astype(_MXU)
        dup = dup_ref[...].astype(_MXU)
        dp_ref[:, width:2 * width] = dup
        dn = _dot_nt(dp_ref[:, 0:width], ws_ref[...]) + _dot_nt(dup, wp_ref[...])
        hv = h_ref[...]
        r = _rs(hv)
        hh = hv * r
        dg_ref[...] += jnp.sum(dn * hh, axis=0, keepdims=True)
        dhh = dn * g_ref[...]
        dh_ref[...] = dho_ref[...] + r * (dhh - hh * jnp.mean(dhh * hh, axis=-1, keepdims=True))

    row = pl.BlockSpec((tm, d), lambda i: (i, 0))
    half = pl.BlockSpec((tm, width), lambda i: (i, 0))
    return pl.pallas_call(
        body, name=name, grid=(n_rows // tm,),
        out_shape=[jax.ShapeDtypeStruct((n_rows, d), F32), jax.ShapeDtypeStruct((n_rows, 2 * width), _MXU),
                   jax.ShapeDtypeStruct((1, d), F32)],
        in_specs=[_lane_blocks(n_blk, tm), half, row, row, _full((1, d)), _resident(), _resident()],
        out_specs=[row, pl.BlockSpec((tm, 2 * width), lambda i: (i, 0)), _full((1, d))], compiler_params=_params(),
    )(du_s, du_p, h, dho, g, wi_s, wi_p)


def _adamw_update(w_ref, gv, m_ref, v_ref, d_ref, mo_ref, vo_ref):
    mn = ADAM_B1 * m_ref[...] + (1.0 - ADAM_B1) * gv
    vn = ADAM_B2 * v_ref[...] + (1.0 - ADAM_B2) * (gv * gv)
    m_hat = mn / (1.0 - ADAM_B1 ** ADAM_STEP)
    v_hat = vn / (1.0 - ADAM_B2 ** ADAM_STEP)
    d_ref[...] = -ADAM_LR * (m_hat / (jnp.sqrt(v_hat) + ADAM_EPS) + ADAM_WD * w_ref[...])
    mo_ref[...] = mn
    vo_ref[...] = vn


def _adamw(w, g, m, v, name):
    def body(w_ref, g_ref, m_ref, v_ref, d_ref, mo_ref, vo_ref):
        _adamw_update(w_ref, g_ref[...], m_ref, v_ref, d_ref, mo_ref, vo_ref)

    spec = _full(w.shape)
    return pl.pallas_call(
        body, name=name, grid=(1,), out_shape=[jax.ShapeDtypeStruct(w.shape, F32)] * 3,
        in_specs=[spec] * 4, out_specs=[spec] * 3, compiler_params=_params(),
    )(w, g, m, v)


def _adamw_many(ws, gs, ms, vs, name):
    n = len(ws)

    def body(*refs):
        w_refs, g_refs, m_refs, v_refs, d_refs, mo_refs, vo_refs = _split(refs, (n,) * 7)
        for k in range(n):
            _adamw_update(w_refs[k], g_refs[k][...], m_refs[k], v_refs[k], d_refs[k], mo_refs[k], vo_refs[k])

    specs = [_full(w.shape) for w in ws]
    shapes = [jax.ShapeDtypeStruct(w.shape, F32) for w in ws]
    outs = pl.pallas_call(
        body, name=name, grid=(1,), out_shape=shapes * 3, in_specs=specs * 4, out_specs=specs * 3,
        compiler_params=_params(),
    )(*ws, *gs, *ms, *vs)
    return outs[:n], outs[n:2 * n], outs[2 * n:]


def _adamw_slots(w, slots, m, v, name, after=()):
    def body(w_ref, s_ref, m_ref, v_ref, *rest):
        g_ref, d_ref, mo_ref, vo_ref = rest[len(after):]
        gv = s_ref[0].astype(F32)
        for k in range(1, N_DEV):
            gv = gv + s_ref[k].astype(F32)
        g_ref[...] = gv
        _adamw_update(w_ref, gv, m_ref, v_ref, d_ref, mo_ref, vo_ref)

    spec = _full(w.shape)
    return pl.pallas_call(
        body, name=name, grid=(1,), out_shape=[jax.ShapeDtypeStruct(w.shape, F32)] * 4,
        in_specs=[spec, _full(slots.shape), spec, spec] + [pl.BlockSpec(memory_space=pl.ANY)] * len(after),
        out_specs=[spec] * 4, compiler_params=_params(),
    )(w, slots, m, v, *after)


def _discretize(lam_re, lam_im, log_dt, b_re, b_im):
    dt = jnp.exp(log_dt)[:, None]
    decay = jnp.exp(lam_re * dt)
    ang = lam_im * dt
    a_re = decay * jnp.cos(ang)
    a_im = decay * jnp.sin(ang)
    nr = a_re - 1.0
    den = lam_re * lam_re + lam_im * lam_im
    q_re = (nr * lam_re + a_im * lam_im) / den
    q_im = (a_im * lam_re - nr * lam_im) / den
    bb_re = q_re[..., None] * b_re - q_im[..., None] * b_im
    bb_im = q_re[..., None] * b_im + q_im[..., None] * b_re
    return a_re, a_im, bb_re, bb_im


def _cmul(a, b):
    return a[0] * b[0] - a[1] * b[1], a[0] * b[1] + a[1] * b[0]


def _power_table(a_re, a_im, count, reverse):
    base = (a_re.reshape(1, -1), (-a_im if reverse else a_im).reshape(1, -1))
    exponent = (jnp.arange(count, 0, -1) if reverse else jnp.arange(1, count + 1))[:, None]
    shape = (count, base[0].shape[1])
    res = (jnp.ones(shape, F32), jnp.zeros(shape, F32))
    for bit in range(int(count).bit_length()):
        prod = _cmul(res, base)
        take = ((exponent >> bit) & 1) == 1
        res = (jnp.where(take, prod[0], res[0]), jnp.where(take, prod[1], res[1]))
        base = _cmul(base, base)
    return jnp.broadcast_to(jnp.stack(res)[:, :, None, :], (2, count, SUBLANES, shape[1]))


def _block_diag(p, n_blk):
    g, r, c = p.shape
    per = g // n_blk
    eye = jnp.eye(per, dtype=p.dtype)
    return jnp.einsum("jgrc,gk->jgrkc", p.reshape(n_blk, per, r, c), eye).reshape(n_blk, per * r, per * c)


def _block_diag_t(m, g):
    n_blk = m.shape[0]
    per = g // n_blk
    r, c = m.shape[1] // per, m.shape[2] // per
    eye = jnp.eye(per, dtype=m.dtype)
    return jnp.einsum("jgrkc,gk->jgrc", m.reshape(n_blk, per, r, per, c), eye).reshape(g, r, c)


def _pack_rows(parts, cols, multiple):
    flat = jnp.concatenate([p.reshape(-1) for p in parts])
    size = -(-flat.shape[0] // (cols * multiple)) * cols * multiple
    return jnp.pad(flat, (0, size - flat.shape[0])).reshape(-1, cols)


def _unpack(flat, shapes):
    out, pos = [], 0
    flat = flat.reshape(-1)
    for s in shapes:
        n = int(np.prod(s))
        out.append(flat[pos:pos + n].reshape(s))
        pos += n
    return out


def kernel(x, meta_tokens, ffn1_pre_norm, ffn1_post_norm, ffn1_w_gate, ffn1_w_up, ffn1_w_down, mix_pre_norm, mix_post_norm, w_in, ssm_lambda_re, ssm_lambda_im, ssm_log_dt, ssm_b_re, ssm_b_im, ssm_c_re, ssm_c_im, ssm_d, ssm_w_glu, pool_w, pool_scale, ssm_out_norm, pool_out_norm, w_out, ffn2_pre_norm, ffn2_post_norm, ffn2_w_gate, ffn2_w_up, ffn2_w_down, loss_target, m_meta_tokens, m_ffn1_pre_norm, m_ffn1_post_norm, m_ffn1_w_gate, m_ffn1_w_up, m_ffn1_w_down, m_mix_pre_norm, m_mix_post_norm, m_w_in, m_ssm_lambda_re, m_ssm_lambda_im, m_ssm_log_dt, m_ssm_b_re, m_ssm_b_im, m_ssm_c_re, m_ssm_c_im, m_ssm_d, m_ssm_w_glu, m_pool_w, m_pool_scale, m_ssm_out_norm, m_pool_out_norm, m_w_out, m_ffn2_pre_norm, m_ffn2_post_norm, m_ffn2_w_gate, m_ffn2_w_up, m_ffn2_w_down, v_meta_tokens, v_ffn1_pre_norm, v_ffn1_post_norm, v_ffn1_w_gate, v_ffn1_w_up, v_ffn1_w_down, v_mix_pre_norm, v_mix_post_norm, v_w_in, v_ssm_lambda_re, v_ssm_lambda_im, v_ssm_log_dt, v_ssm_b_re, v_ssm_b_im, v_ssm_c_re, v_ssm_c_im, v_ssm_d, v_ssm_w_glu, v_pool_w, v_pool_scale, v_ssm_out_norm, v_pool_out_norm, v_w_out, v_ffn2_pre_norm, v_ffn2_post_norm, v_ffn2_w_gate, v_ffn2_w_up, v_ffn2_w_down):
    args = dict(locals())
    names = ["meta_tokens", "ffn1_pre_norm", "ffn1_post_norm", "ffn1_w_gate", "ffn1_w_up", "ffn1_w_down", "mix_pre_norm",
             "mix_post_norm", "w_in", "ssm_lambda_re", "ssm_lambda_im", "ssm_log_dt", "ssm_b_re", "ssm_b_im", "ssm_c_re",
             "ssm_c_im", "ssm_d", "ssm_w_glu", "pool_w", "pool_scale", "ssm_out_norm", "pool_out_norm", "w_out",
             "ffn2_pre_norm", "ffn2_post_norm", "ffn2_w_gate", "ffn2_w_up", "ffn2_w_down"]
    sharded = ("meta_tokens", "ffn1_w_gate", "ffn1_w_up", "ffn1_w_down", "w_in", "w_out", "ffn2_w_gate", "ffn2_w_up",
               "ffn2_w_down")
    small = [n for n in names if n not in sharded]

    d = x.shape[-1]
    width = d // 2
    n_grp = ssm_lambda_re.shape[1]
    n_blk = width // LANES

    def stacked(gathered):
        return gathered.reshape(-1, d).astype(_MXU)

    def chunked(m):
        return m.reshape(N_DEV, -1, d)

    s_gate1, s_up1, s_down1 = _to_wire([ffn1_w_gate[0].T, ffn1_w_up[0].T, ffn1_w_down[0]], "wire_ffn1")
    s_win, s_wout, s_gate2, s_up2, s_down2 = _to_wire(
        [w_in[0], w_out[0], ffn2_w_gate[0].T, ffn2_w_up[0].T, ffn2_w_down[0]], "wire_rest")
    g_gate1, g_up1, g_down1, meta_all = _gather_two_level([s_gate1, s_up1, s_down1, meta_tokens], "gather_ffn1")
    wgt1, wut1, wd1 = stacked(g_gate1), stacked(g_up1), stacked(g_down1)
    meta_full = jnp.transpose(meta_all, (1, 0, 2)).reshape(N_META, d)

    (a1, b1, f1, h1, h0), (g_win, g_wout, g_gate2) = _ffn_fwd(
        None, ffn1_pre_norm, ffn1_post_norm, wgt1, wut1, wd1, "ffn1_fwd",
        xchg=_Xchg([s_win, s_wout, s_gate2], ["gather"] * 3), meta=meta_full, tokens=x[0])
    w_in_f, w_out_f = stacked(g_win), stacked(g_wout)

    a_re, a_im, bb_re, bb_im = _discretize(ssm_lambda_re[0], ssm_lambda_im[0], ssm_log_dt[0], ssm_b_re[0], ssm_b_im[0])
    steps = _row_tile(N_META + x.shape[1]) // SUBLANES
    coef = _power_table(a_re, a_im, steps, reverse=False)
    coef_rev = _power_table(a_re, a_im, steps, reverse=True)
    bmat_re = _block_diag(jnp.swapaxes(bb_re, 1, 2), n_blk).astype(_MXU)
    bmat_im = _block_diag(jnp.swapaxes(bb_im, 1, 2), n_blk).astype(_MXU)
    cmat_re = _block_diag(jnp.swapaxes(ssm_c_re[0], 1, 2), n_blk).astype(_MXU)
    cmat_im = _block_diag(jnp.swapaxes(ssm_c_im[0], 1, 2), n_blk).astype(_MXU)
    wz1 = _block_diag(ssm_w_glu[0][:, :, :SSM_GROUP_CH], n_blk).astype(_MXU)
    wz2 = _block_diag(ssm_w_glu[0][:, :, SSM_GROUP_CH:], n_blk).astype(_MXU)
    pool_wm = pool_w[0].astype(_MXU)

    n2, u_s, u_p = _mix_in_fwd(h1, mix_pre_norm, w_in_f, "mix_in_fwd")
    (x_re, x_im, cat_s), (g_up2, g_down2) = _ssm_forward(
        u_s, coef, bmat_re, bmat_im, cmat_re, cmat_im, wz1, wz2, ssm_d, ssm_out_norm, "ssm_fwd",
        xchg=_Xchg([s_up2, s_down2], ["gather"] * 2))
    wgt2, wut2, wd2 = stacked(g_gate2), stacked(g_up2), stacked(g_down2)
    cat_p = _pool_fwd(u_p, pool_wm, pool_scale, pool_out_norm, "pool_fwd")
    wo_s, wo_p = w_out_f[:width], w_out_f[width:]
    mixed, h2 = _mix_out_fwd(cat_s, cat_p, h1, mix_post_norm, wo_s, wo_p, "mix_out_fwd")

    (a2, b2, f2, dh3, sq_err), _ = _ffn_fwd(
        h2, ffn2_pre_norm, ffn2_post_norm, wgt2, wut2, wd2, "ffn2_fwd", target=loss_target[0])

    g, slots = {}, {}
    (dh2, da2, db2, s2, df2, nf2, g["ffn2_pre_norm"], g["ffn2_post_norm"]), _ = _ffn_bwd(
        dh3, h2, f2, a2, b2, ffn2_pre_norm, ffn2_post_norm, wgt2, wut2, wd2, "ffn2_bwd")
    dgate2, _ = _wgrad(da2, nf2, "ffn2_dgate")
    dup2, _ = _wgrad(db2, nf2, "ffn2_dup")
    ddown2, _ = _wgrad(s2, df2, "ffn2_ddown")

    dmixed, dcat_s, dcat_p, g["mix_post_norm"] = _mix_out_bwd(dh2, mixed, mix_post_norm, wo_s, wo_p, "mix_out_bwd")
    dwout = jnp.concatenate([_wgrad(cat_s, dmixed, "dwout_s")[0], _wgrad(cat_p, dmixed, "dwout_p")[0]], axis=0)
    ((du_s, g["ssm_out_norm"], g["ssm_d"], dwz1, dwz2, dcm_re, dcm_im, dbm_re, dbm_im, acc_re, acc_im),
     (slots["ffn2_w_gate"], slots["ffn2_w_up"], slots["ffn2_w_down"], slots["w_out"])) = _ssm_backward(
        u_s, x_re, x_im, dcat_s, coef_rev, bmat_re, bmat_im, cmat_re, cmat_im, wz1, wz2, ssm_d, ssm_out_norm, "ssm_bwd",
        xchg=_Xchg([chunked(dgate2), chunked(dup2), chunked(ddown2), chunked(dwout)], ["scatter"] * 4))
    du_p, g["pool_out_norm"], g["pool_scale"], dpw = _pool_bwd(u_p, dcat_p, pool_wm, pool_scale, pool_out_norm, "pool_bwd")
    wi_s, wi_p = w_in_f[:, :width], w_in_f[:, width:]
    dh1, dproj, g["mix_pre_norm"] = _mix_in_bwd(du_s, du_p, h1, dh2, mix_pre_norm, wi_s, wi_p, "mix_in_bwd")
    dwin, _ = _wgrad(n2, dproj, "dwin")

    g["ssm_c_re"] = jnp.swapaxes(_block_diag_t(jnp.swapaxes(dcm_re, 1, 2), n_grp), 1, 2)[None]
    g["ssm_c_im"] = jnp.swapaxes(_block_diag_t(jnp.swapaxes(dcm_im, 1, 2), n_grp), 1, 2)[None]
    g["ssm_w_glu"] = jnp.concatenate([_block_diag_t(dwz1, n_grp), _block_diag_t(dwz2, n_grp)], axis=-1)[None]
    dbb_re = jnp.swapaxes(_block_diag_t(dbm_re, n_grp), 1, 2)
    dbb_im = jnp.swapaxes(_block_diag_t(dbm_im, n_grp), 1, 2)
    da_re, da_im = acc_re.reshape(a_re.shape), acc_im.reshape(a_re.shape)
    _, disc_vjp = jax.vjp(_discretize, ssm_lambda_re[0], ssm_lambda_im[0], ssm_log_dt[0], ssm_b_re[0], ssm_b_im[0])
    d_lre, d_lim, d_ldt, d_bre, d_bim = disc_vjp((da_re, da_im, dbb_re, dbb_im))
    g["ssm_lambda_re"], g["ssm_lambda_im"], g["ssm_log_dt"] = d_lre[None], d_lim[None], d_ldt[None]
    g["ssm_b_re"], g["ssm_b_im"] = d_bre[None], d_bim[None]
    g["pool_w"] = dpw[None]

    late = ["ffn1_pre_norm", "ffn1_post_norm"]
    early = [n for n in small if n not in late]
    early_vec = _pack_rows([g[n] for n in early] + [sq_err[:, :1]], 1024, SUBLANES)
    mix_started, mix_token = _split_start([chunked(dwin), early_vec], ["scatter", "gather"], "reduce_mix_start")
    (dx, da1, db1, s1, df1, nf1, g["ffn1_pre_norm"], g["ffn1_post_norm"], dmeta_part), _ = _ffn_bwd(
        dh1, h0, f1, a1, b1, ffn1_pre_norm, ffn1_post_norm, wgt1, wut1, wd1, "ffn1_bwd", split_meta=True,
        after=(mix_token,))
    slots["w_in"], recv_early = _split_wait(mix_started, [dmeta_part], "reduce_mix_wait")
    late_vec = _pack_rows([g[n] for n in late] + [dmeta_part], 1024, SUBLANES)
    dgate1, _ = _wgrad(da1, nf1, "ffn1_dgate")
    gate_started, token = _split_start([chunked(dgate1), late_vec], ["scatter", "gather"], "reduce_gate_start")
    dup1, _ = _wgrad(db1, nf1, "ffn1_dup", after=(token,))
    up_started, token = _split_start([chunked(dup1)], ["scatter"], "reduce_up_start")
    ddown1, _ = _wgrad(s1, df1, "ffn1_ddown", after=(token,))
    down_started, token = _split_start([chunked(ddown1)], ["scatter"], "reduce_down_start")
    waits = {"ffn1_w_gate": (gate_started, "reduce_gate_wait"), "ffn1_w_up": (up_started, "reduce_up_wait"),
             "ffn1_w_down": (down_started, "reduce_down_wait")}

    summed = _unpack(_sum_slots(recv_early, "sum_small_grads"), [g[n].shape for n in early] + [(1,)])
    for n, val in zip(early, summed):
        g[n] = val
    loss = (0.5 / d) * summed[-1][0]

    delta, new_m, new_v = {}, {}, {}
    after = (token,)
    for n in sorted(sharded[1:], key=lambda n: list(waits).index(n) if n in waits else -1) + ["meta_tokens"]:
        shape = args[n].shape
        two_d = (-1, shape[-1])
        w2, m2, v2 = args[n].reshape(two_d), args["m_" + n].reshape(two_d), args["v_" + n].reshape(two_d)
        if n in waits:
            slots[n], *rest = _split_wait(waits[n][0], after, waits[n][1])
            if rest:
                g[late[0]], g[late[1]], dmeta = _unpack(_sum_slots(rest[0], "sum_last_grads"), [(1, d), (1, d), (N_META, d)])
                g["meta_tokens"] = lax.dynamic_slice_in_dim(dmeta, _my_slot() * (d // N_DEV), d // N_DEV, axis=1)
        if n == "meta_tokens":
            dl, mn, vn = _adamw(w2, g[n], m2, v2, "adamw_" + n)
        elif n.endswith("gate") or n.endswith("up"):
            raw = _adamw_slots(w2.T, slots[n], m2.T, v2.T, "adamw_" + n, after=after)
            gs, dl, mn, vn = [t.T for t in raw]
        else:
            raw = gs, dl, mn, vn = _adamw_slots(w2, slots[n], m2, v2, "adamw_" + n, after=after)
        if n != "meta_tokens":
            g[n] = gs[None]
            after = (raw[1],)
        delta[n], new_m[n], new_v[n] = dl.reshape(shape), mn.reshape(shape), vn.reshape(shape)
    outs = _adamw_many([args[n] for n in small], [g[n] for n in small], [args["m_" + n] for n in small],
                       [args["v_" + n] for n in small], "adamw_small")
    for store, vals in zip((delta, new_m, new_v), outs):
        for n, val in zip(small, vals):
            store[n] = val

    grad_x = dx[None]
    return (loss, grad_x, *[g[n] for n in names], *[delta[n] for n in names], *[new_m[n] for n in names],
            *[new_v[n] for n in names])
```

```python
import functools
import math

import jax
import jax.numpy as jnp
import numpy as np
from jax import lax
from jax.experimental import pallas as pl
from jax.experimental.pallas import tpu as pltpu

F32 = jnp.float32
_MXU = jnp.bfloat16
_ACT = jnp.bfloat16
_WIRE = jnp.bfloat16

N_DEV = 8
N_META = 16
RMS_EPS = 1e-6
SSM_GROUP_CH = 16
LANES = 128
SUBLANES = 8
POOL_WINDOWS = (2, 4, 8, 16)
POOL_HALO = 16
ADAM_LR = 0.001
ADAM_B1 = 0.9
ADAM_B2 = 0.999
ADAM_EPS = 1e-08
ADAM_WD = 0.01
ADAM_STEP = 10
GELU_C0 = math.sqrt(2.0 / math.pi)
GELU_C1 = 0.044715
VMEM_LIMIT = 62 * 1024 * 1024

_NT = (((1,), (1,)), ((), ()))
_TN = (((0,), (0,)), ((), ()))


def _dot(a, b):
    return jnp.dot(a, b, preferred_element_type=F32)


def _dot_nt(a, b):
    return lax.dot_general(a, b, _NT, preferred_element_type=F32)


def _dot_tn(a, b):
    return lax.dot_general(a, b, _TN, preferred_element_type=F32)


def _rs(x):
    return lax.rsqrt(jnp.mean(x * x, axis=-1, keepdims=True) + RMS_EPS)


def _sigmoid(x):
    return 0.5 * jnp.tanh(0.5 * x) + 0.5


def _row_tile(n_rows, largest=432):
    for t in (432, 304, 48, 16):
        if t <= largest and n_rows % t == 0:
            return t
    raise ValueError(n_rows)


def _ff_chunk(d_ff):
    return d_ff // 2 if (d_ff // 2) % LANES == 0 else d_ff


def _params(n_axes=1):
    return pltpu.CompilerParams(dimension_semantics=("arbitrary",) * n_axes, vmem_limit_bytes=VMEM_LIMIT)


def _resident():
    return pl.BlockSpec(memory_space=pltpu.VMEM)


def _full(shape):
    nd = len(shape)
    return pl.BlockSpec(shape, lambda *_: (0,) * nd)


def _lane_blocks(n_blk, tm, tile_of=lambda i: i):
    return pl.BlockSpec((n_blk, tm, LANES), lambda i: (0, tile_of(i), 0))


PEER_ORDER = (1, 2, 4, 3, 5, 6, 7)


def _split(refs, counts):
    out, pos = [], 0
    for n in counts:
        out.append(refs[pos:pos + n])
        pos += n
    return out


def _peer(r):
    x, y, c = lax.axis_index("x"), lax.axis_index("y"), lax.axis_index("c")
    return (1 - x if r & 4 else x, 1 - y if r & 2 else y, 1 - c if r & 1 else c)


def _my_slot():
    return 4 * lax.axis_index("x") + 2 * lax.axis_index("y") + lax.axis_index("c")


class _Xchg:
    def __init__(self, srcs, kinds):
        self.srcs, self.kinds, self.n = list(srcs), list(kinds), len(srcs)
        self.out_shape = [jax.ShapeDtypeStruct((N_DEV,) + s.shape if k == "gather" else s.shape, s.dtype)
                          for s, k in zip(self.srcs, self.kinds)]
        self.specs = [pl.BlockSpec(memory_space=pl.ANY)] * self.n
        self.scratch = [pltpu.SemaphoreType.DMA((self.n * (N_DEV - 1),)), pltpu.SemaphoreType.DMA((self.n * (N_DEV - 1),)),
                        pltpu.SemaphoreType.DMA((self.n,))]

    def copies(self, src, dst, sems):
        send_sems, recv_sems, local_sems = sems
        me = _my_slot()
        out = []
        for a in range(self.n):
            mine = src[a] if self.kinds[a] == "gather" else src[a].at[me]
            out.append(pltpu.make_async_copy(mine, dst[a].at[me], local_sems.at[a]))
            for r in PEER_ORDER:
                px, py, pc = _peer(r)
                part = src[a] if self.kinds[a] == "gather" else src[a].at[4 * px + 2 * py + pc]
                k = a * (N_DEV - 1) + r - 1
                out.append(pltpu.make_async_remote_copy(
                    src_ref=part, dst_ref=dst[a].at[me], send_sem=send_sems.at[k], recv_sem=recv_sems.at[k],
                    device_id=(px, py, pc), device_id_type=pl.DeviceIdType.MESH))
        return out

    def start(self, src, dst, sems):
        for cp in self.copies(src, dst, sems):
            cp.start()

    def wait(self, src, dst, sems):
        for cp in self.copies(src, dst, sems):
            cp.wait()


class _NoXchg:
    n, srcs, out_shape, specs, scratch = 0, [], [], [], []

    def start(self, *_):
        pass

    wait = start


def _call(body, args, *, name, grid, out_shape, in_specs, out_specs, scratch_shapes=(), xchg=None, after=()):
    xc = xchg or _NoXchg()
    counts = (len(in_specs), xc.n, len(after), len(out_shape), xc.n, len(scratch_shapes), len(xc.scratch))

    def wrapped(*refs):
        ins, xsrc, _, outs, xdst, scr, sems = _split(refs, counts)
        ids = [pl.program_id(k) for k in range(len(grid))]
        if xc.n:
            @pl.when(functools.reduce(jnp.logical_and, [i == 0 for i in ids]))
            def _():
                xc.start(xsrc, xdst, sems)

        body(*ins, *outs, *scr)
        if xc.n:
            @pl.when(functools.reduce(jnp.logical_and, [i == g - 1 for i, g in zip(ids, grid)]))
            def _():
                xc.wait(xsrc, xdst, sems)

    res = pl.pallas_call(
        wrapped, name=name, grid=grid, out_shape=list(out_shape) + xc.out_shape,
        in_specs=list(in_specs) + xc.specs + [pl.BlockSpec(memory_space=pl.ANY)] * len(after),
        out_specs=list(out_specs) + xc.specs,
        scratch_shapes=list(scratch_shapes) + xc.scratch, compiler_params=_params(len(grid)),
    )(*args, *xc.srcs, *after)
    return res[:len(out_shape)], res[len(out_shape):]


def _exchange(srcs, kinds, name):
    xc = _Xchg(srcs, kinds)

    def body(*refs):
        src, dst, sems = _split(refs, (xc.n, xc.n, 3))
        xc.start(src, dst, sems)
        xc.wait(src, dst, sems)

    return pl.pallas_call(body, name=name, out_shape=xc.out_shape, in_specs=xc.specs, out_specs=xc.specs,
                          scratch_shapes=xc.scratch)(*srcs)


def _split_copies(kinds, src, land, send_sems, recv_sems):
    me = _my_slot()
    out = []
    for a, kind in enumerate(kinds):
        for r in PEER_ORDER:
            px, py, pc = _peer(r)
            k = a * (N_DEV - 1) + r - 1
            out.append(pltpu.make_async_remote_copy(
                src_ref=src[a] if kind == "gather" else src[a].at[4 * px + 2 * py + pc], dst_ref=land[a].at[me],
                send_sem=send_sems.at[k], recv_sem=recv_sems.at[k], device_id=(px, py, pc),
                device_id_type=pl.DeviceIdType.MESH))
    return out


_SPLIT_EFFECT = pltpu.SideEffectType.DATAFLOW_SIDE_EFFECTING


def _split_start(srcs, kinds, name):
    n = len(srcs)
    hbm, sem = pl.BlockSpec(memory_space=pltpu.HBM), pl.BlockSpec(memory_space=pltpu.SEMAPHORE)
    lands = [lax.empty((N_DEV,) + s.shape if k == "gather" else s.shape, s.dtype) for s, k in zip(srcs, kinds)]

    def body(*refs):
        src, land, (send_sems, recv_sems), _, (token,) = _split(refs, (n, n, 2, 2 * n, 1))
        for cp in _split_copies(kinds, src, land, send_sems, recv_sems):
            cp.start()
        token[...] = jnp.zeros_like(token)

    n_sem = n * (N_DEV - 1)
    out = pl.pallas_call(
        body, name=name,
        out_shape=[pltpu.SemaphoreType.DMA((n_sem,)), pltpu.SemaphoreType.DMA((n_sem,))]
        + [pltpu.HBM(a.shape, a.dtype) for a in list(srcs) + lands] + [jax.ShapeDtypeStruct((SUBLANES, LANES), F32)],
        in_specs=[hbm] * (2 * n), out_specs=[sem, sem] + [hbm] * (2 * n) + [pl.BlockSpec(memory_space=pltpu.VMEM)],
        input_output_aliases={k: 2 + k for k in range(2 * n)},
        compiler_params=pltpu.CompilerParams(has_side_effects=_SPLIT_EFFECT),
    )(*[pltpu.with_memory_space_constraint(a, pltpu.HBM) for a in list(srcs) + lands])
    return (kinds, out[0], out[1], out[2:2 + n], out[2 + n:2 + 2 * n]), out[-1]


def _split_wait(started, after, name):
    kinds, send_sems, recv_sems, srcs, lands = started
    n = len(srcs)
    hbm, sem = pl.BlockSpec(memory_space=pltpu.HBM), pl.BlockSpec(memory_space=pltpu.SEMAPHORE)

    def body(*refs):
        src, land, (send_sems, recv_sems) = _split(refs, (n, n, 2))[:3]
        for cp in _split_copies(kinds, src, land, send_sems, recv_sems):
            cp.wait_send()
            cp.wait_recv()

    out = pl.pallas_call(
        body, name=name, out_shape=[pltpu.HBM(a.shape, a.dtype) for a in list(srcs) + list(lands)],
        in_specs=[hbm] * (2 * n) + [sem, sem] + [pl.BlockSpec(memory_space=pl.ANY)] * len(after),
        out_specs=[hbm] * (2 * n), input_output_aliases={k: k for k in range(2 * n)},
        compiler_params=pltpu.CompilerParams(has_side_effects=_SPLIT_EFFECT),
    )(*srcs, *lands, send_sems, recv_sems, *after)
    me = _my_slot()
    filled = []
    for kind, sent, land in zip(kinds, out[:n], out[n:]):
        mine = sent[None] if kind == "gather" else lax.dynamic_slice_in_dim(sent, me, 1, axis=0)
        filled.append(lax.dynamic_update_slice_in_dim(land, mine, me, axis=0))
    return filled


def _gather_two_level(srcs, name):
    n = len(srcs)
    out_shape = [jax.ShapeDtypeStruct((N_DEV,) + s.shape, s.dtype) for s in srcs]
    chips = (2, 4, 6)

    def body(*refs):
        src, dst, (send_sems, recv_sems, local_sems) = _split(refs, (n, n, 3))
        x, y, c = lax.axis_index("x"), lax.axis_index("y"), lax.axis_index("c")
        me = 4 * x + 2 * y + c
        sibling = (x, y, 1 - c)

        def copy(a, k, slot, to, from_src=False):
            return pltpu.make_async_remote_copy(
                src_ref=src[a] if from_src else dst[a].at[slot], dst_ref=dst[a].at[slot],
                send_sem=send_sems.at[a * 7 + k], recv_sem=recv_sems.at[a * 7 + k],
                device_id=to, device_id_type=pl.DeviceIdType.MESH)

        def slot_of(r, core):
            px, py, _ = _peer(r)
            return 4 * px + 2 * py + core

        local = [pltpu.make_async_copy(src[a], dst[a].at[me], local_sems.at[a]) for a in range(n)]
        sent = []
        for a in range(n):
            local[a].start()
            sent.append(copy(a, 0, me, sibling, from_src=True))
            sent += [copy(a, 1 + j, me, _peer(r), from_src=True) for j, r in enumerate(chips)]
        for cp in sent:
            cp.start()
        for j, r in enumerate(chips):
            for a in range(n):
                copy(a, 1 + j, slot_of(r, c), _peer(r)).wait_recv()
                cp = copy(a, 4 + j, slot_of(r, c), sibling)
                cp.start()
                sent.append(cp)
        for a in range(n):
            copy(a, 0, slot_of(0, 1 - c), sibling).wait_recv()
            for j, r in enumerate(chips):
                copy(a, 4 + j, slot_of(r, 1 - c), sibling).wait_recv()
        for cp in local:
            cp.wait()
        for cp in sent:
            cp.wait_send()

    any_spec = pl.BlockSpec(memory_space=pl.ANY)
    return pl.pallas_call(
        body, name=name, out_shape=out_shape, in_specs=[any_spec] * n, out_specs=[any_spec] * n,
        scratch_shapes=[pltpu.SemaphoreType.DMA((n * 7,)), pltpu.SemaphoreType.DMA((n * 7,)), pltpu.SemaphoreType.DMA((n,))],
    )(*srcs)


def _to_wire(mats, name):
    def body(*refs):
        for src, dst in zip(refs[:len(mats)], refs[len(mats):]):
            dst[...] = src[...].astype(_WIRE)

    return pl.pallas_call(
        body, name=name, grid=(1,), out_shape=[jax.ShapeDtypeStruct(m.shape, _WIRE) for m in mats],
        in_specs=[_full(m.shape) for m in mats], out_specs=[_full(m.shape) for m in mats], compiler_params=_params(),
    )(*mats)


def _sum_slots(r, name):
    _, rows, cols = r.shape
    blk = rows
    for cand in (rows, 592, 512, 256, 128, 64, 32, 16):
        if rows % cand == 0 and N_DEV * cand * cols * r.dtype.itemsize <= 8 * 1024 * 1024:
            blk = cand
            break

    def body(r_ref, o_ref):
        acc = r_ref[0].astype(F32)
        for d in range(1, N_DEV):
            acc = acc + r_ref[d].astype(F32)
        o_ref[...] = acc

    return pl.pallas_call(
        body, name=name, grid=(rows // blk,), out_shape=jax.ShapeDtypeStruct((rows, cols), F32),
        in_specs=[pl.BlockSpec((N_DEV, blk, cols), lambda i: (0, i, 0))],
        out_specs=pl.BlockSpec((blk, cols), lambda i: (i, 0)), compiler_params=_params(),
    )(r)


def _token_tile_copy(tokens_ref, buf, sems, i, tm, write=False):
    if isinstance(i, int) and i == 0:
        far, near = tokens_ref.at[pl.ds(0, tm - N_META)], buf.at[0, pl.ds(N_META, tm - N_META)]
    else:
        start = i * tm - N_META if isinstance(i, int) else pl.multiple_of(i * tm - N_META, SUBLANES)
        far, near = tokens_ref.at[pl.ds(start, tm)], buf.at[i % 2]
    return pltpu.make_async_copy(near, far, sems.at[i % 2]) if write else pltpu.make_async_copy(far, near, sems.at[i % 2])


def _fetch_token_tile(tokens_ref, buf, sems, i, n_t, tm):
    @pl.when(i == 0)
    def _():
        _token_tile_copy(tokens_ref, buf, sems, 0, tm).start()

    @pl.when(i + 1 < n_t)
    def _():
        _token_tile_copy(tokens_ref, buf, sems, i + 1, tm).start()

    @pl.when(i == 0)
    def _():
        _token_tile_copy(tokens_ref, buf, sems, 0, tm).wait()

    @pl.when(i > 0)
    def _():
        _token_tile_copy(tokens_ref, buf, sems, i, tm).wait()


def _ffn_fwd(h, g_pre, g_post, wgt, wut, wd, name, xchg=None, *, meta=None, tokens=None, target=None):
    first = h is None
    d = wd.shape[1]
    n_rows = N_META + tokens.shape[0] if first else h.shape[0]
    d_ff = wd.shape[0]
    tm, fc = _row_tile(n_rows), _ff_chunk(d_ff)
    n_t, n_c = n_rows // tm, d_ff // fc

    def body(src_ref, side_ref, gpre_ref, gpost_ref, wgt_ref, wut_ref, wd_ref, a_ref, b_ref, f_ref, o1_ref, o2_ref,
             n_scr, acc, buf, sems):
        i, c = pl.program_id(0), pl.program_id(1)
        slot = i % 2

        @pl.when(c == 0)
        def _():
            if first:
                _fetch_token_tile(side_ref, buf, sems, i, n_t, tm)

                @pl.when(i == 0)
                def _():
                    buf[0, 0:N_META, :] = src_ref[...]

                hv = buf[slot]
                o2_ref[...] = hv
            else:
                _fetch_token_tile(side_ref, buf, sems, i, n_t, tm)

                @pl.when(i == 0)
                def _():
                    buf[0, 0:N_META, :] = jnp.zeros((N_META, d), F32)
                    o2_ref[...] = jnp.zeros_like(o2_ref)

                hv = src_ref[...]
            n_scr[...] = (hv * _rs(hv) * gpre_ref[...]).astype(_MXU)
            acc[...] = jnp.zeros_like(acc)

        rows = pl.ds(pl.multiple_of(c * fc, fc), fc)
        nv = n_scr[...]
        a = _dot_nt(nv, wgt_ref[rows, :])
        b = _dot_nt(nv, wut_ref[rows, :])
        a_ref[...] = a.astype(_ACT)
        b_ref[...] = b.astype(_ACT)
        s = a * _sigmoid(a) * b
        acc[...] += _dot(s.astype(_MXU), wd_ref[rows, :])

        @pl.when(c == n_c - 1)
        def _():
            f = acc[...]
            f_ref[...] = f
            step = 0.5 * (f * _rs(f) * gpost_ref[...])
            if first:
                o1_ref[...] = buf[slot] + step
            else:
                row = i * tm + lax.broadcasted_iota(jnp.int32, (tm, 1), 0)
                err = jnp.where(row >= N_META, (src_ref[...] + step) - buf[slot], 0.0)
                o1_ref[...] = err / d
                o2_ref[...] += jnp.sum(jnp.sum(err * err, axis=0, keepdims=True), axis=1, keepdims=True)

    row = pl.BlockSpec((tm, d), lambda i, c: (i, 0))
    chunk = pl.BlockSpec((tm, fc), lambda i, c: (i, c))
    hbm = pl.BlockSpec(memory_space=pl.ANY)
    if first:
        operands, specs = (meta, tokens), [_full(meta.shape), hbm]
        last_shape, last_spec = jax.ShapeDtypeStruct((n_rows, d), F32), row
    else:
        operands, specs = (h, target), [row, hbm]
        last_shape, last_spec = jax.ShapeDtypeStruct((1, LANES), F32), pl.BlockSpec((1, LANES), lambda i, c: (0, 0))
    return _call(
        body, (*operands, g_pre, g_post, wgt, wut, wd), name=name, grid=(n_t, n_c),
        out_shape=[jax.ShapeDtypeStruct((n_rows, d_ff), _ACT), jax.ShapeDtypeStruct((n_rows, d_ff), _ACT),
                   jax.ShapeDtypeStruct((n_rows, d), F32), jax.ShapeDtypeStruct((n_rows, d), F32), last_shape],
        in_specs=specs + [_full((1, d)), _full((1, d)), _resident(), _resident(), _resident()],
        out_specs=[chunk, chunk, row, row, last_spec],
        scratch_shapes=[pltpu.VMEM((tm, d), _MXU), pltpu.VMEM((tm, d), F32), pltpu.VMEM((2, tm, d), F32),
                        pltpu.SemaphoreType.DMA((2,))], xchg=xchg)


def _ffn_bwd(dho, h, f, a, b, g_pre, g_post, wgt, wut, wd, name, xchg=None, *, split_meta=False, after=()):
    n_rows, d = h.shape
    d_ff = wd.shape[0]
    tm, fc = _row_tile(n_rows), _ff_chunk(d_ff)
    n_t, n_c = n_rows // tm, d_ff // fc
    assert n_t >= 2

    def body(dho_ref, h_ref, f_ref, a_ref, b_ref, gpre_ref, gpost_ref, wgt_ref, wut_ref, wd_ref,
             dh_ref, da_ref, db_ref, s_ref, df_ref, n_ref, dgpre_ref, dgpost_ref, *rest):
        dn_acc = rest[-1]
        i, c = pl.program_id(0), pl.program_id(1)

        @pl.when((i == 0) & (c == 0))
        def _():
            dgpre_ref[...] = jnp.zeros_like(dgpre_ref)
            dgpost_ref[...] = jnp.zeros_like(dgpost_ref)

        @pl.when(c == 0)
        def _():
            fv = f_ref[...]
            rf = _rs(fv)
            fhat = fv * rf
            dy = 0.5 * dho_ref[...]
            dgpost_ref[...] += jnp.sum(dy * fhat, axis=0, keepdims=True)
            dfhat = dy * gpost_ref[...]
            df = rf * (dfhat - fhat * jnp.mean(dfhat * fhat, axis=-1, keepdims=True))
            df_ref[...] = df.astype(_MXU)
            hv = h_ref[...]
            n_ref[...] = (hv * _rs(hv) * gpre_ref[...]).astype(_MXU)
            dn_acc[...] = jnp.zeros_like(dn_acc)

        rows = pl.ds(pl.multiple_of(c * fc, fc), fc)
        ds = _dot_nt(df_ref[...], wd_ref[rows, :])
        av = a_ref[...].astype(F32)
        bv = b_ref[...].astype(F32)
        sg = _sigmoid(av)
        si = av * sg
        da = (ds * bv * (sg * (1.0 + av * (1.0 - sg)))).astype(_MXU)
        db = (ds * si).astype(_MXU)
        da_ref[...] = da
        db_ref[...] = db
        s_ref[...] = (si * bv).astype(_MXU)
        dn_acc[...] += _dot(da, wgt_ref[rows, :]) + _dot(db, wut_ref[rows, :])

        @pl.when(c == n_c - 1)
        def _():
            dn = dn_acc[...]
            hv = h_ref[...]
            r = _rs(hv)
            hhat = hv * r
            dgpre_ref[...] += jnp.sum(dn * hhat, axis=0, keepdims=True)
            dhh = dn * gpre_ref[...]
            dh = dho_ref[...] + r * (dhh - hhat * jnp.mean(dhh * hhat, axis=-1, keepdims=True))
            if not split_meta:
                dh_ref[...] = dh
                return
            dmeta_ref, buf, sems = rest[0], rest[1], rest[2]

            def out_copy(k):
                return _token_tile_copy(dh_ref, buf, sems, k, tm, write=True)

            @pl.when(i == 2)
            def _():
                out_copy(0).wait()

            @pl.when(i > 2)
            def _():
                out_copy(i - 2).wait()

            buf[i % 2] = dh

            @pl.when(i == 0)
            def _():
                dmeta_ref[...] = buf[0, 0:N_META, :]
                out_copy(0).start()

            @pl.when(i > 0)
            def _():
                out_copy(i).start()

            @pl.when(i == n_t - 1)
            def _():
                out_copy(n_t - 2).wait()
                out_copy(n_t - 1).wait()

    row = pl.BlockSpec((tm, d), lambda i, c: (i, 0))
    chunk = pl.BlockSpec((tm, fc), lambda i, c: (i, c))
    vec = pl.BlockSpec((1, d), lambda i, c: (0, 0))
    shapes = [jax.ShapeDtypeStruct((n_rows, d_ff), _MXU)] * 3 + [jax.ShapeDtypeStruct((n_rows, d), _MXU)] * 2 \
        + [jax.ShapeDtypeStruct((1, d), F32)] * 2
    specs = [chunk, chunk, chunk, row, row, vec, vec]
    scratch = [pltpu.VMEM((tm, d), F32)]
    if split_meta:
        shapes = [jax.ShapeDtypeStruct((n_rows - N_META, d), F32)] + shapes + [jax.ShapeDtypeStruct((N_META, d), F32)]
        specs = [pl.BlockSpec(memory_space=pl.ANY)] + specs + [pl.BlockSpec((N_META, d), lambda i, c: (0, 0))]
        scratch = [pltpu.VMEM((2, tm, d), F32), pltpu.SemaphoreType.DMA((2,))] + scratch
    else:
        shapes, specs = [jax.ShapeDtypeStruct((n_rows, d), F32)] + shapes, [row] + specs
    return _call(
        body, (dho, h, f, a, b, g_pre, g_post, wgt, wut, wd), name=name, grid=(n_t, n_c), out_shape=shapes,
        in_specs=[row, row, row, chunk, chunk, vec, vec, _resident(), _resident(), _resident()], out_specs=specs,
        scratch_shapes=scratch, xchg=xchg, after=after)


def _wgrad(xm, ym, name, xchg=None, after=()):
    n_rows, a_dim = xm.shape
    b_dim = ym.shape[1]
    tk = n_rows
    for cand in (2736, 1296, 432, 48, 16):
        if n_rows % cand == 0:
            tk = cand
            break
    ta = a_dim
    for cand in (1408, 1024, 512):
        if a_dim % cand == 0:
            ta = cand
            break

    n_k = n_rows // tk

    def body(x_ref, y_ref, o_ref, acc):
        k = pl.program_id(1)

        @pl.when(k == 0)
        def _():
            acc[...] = jnp.zeros_like(acc)

        acc[...] += _dot_tn(x_ref[...], y_ref[...])

        @pl.when(k == n_k - 1)
        def _():
            o_ref[...] = acc[...].astype(o_ref.dtype)

    (out,), extra = _call(
        body, (xm, ym), name=name, grid=(a_dim // ta, n_k), out_shape=[jax.ShapeDtypeStruct((a_dim, b_dim), _WIRE)],
        in_specs=[pl.BlockSpec((tk, ta), lambda j, k: (k, j)), pl.BlockSpec((tk, b_dim), lambda j, k: (k, 0))],
        out_specs=[pl.BlockSpec((ta, b_dim), lambda j, k: (j, 0))], scratch_shapes=[pltpu.VMEM((ta, b_dim), F32)],
        xchg=xchg, after=after)
    return out, extra


def _mix_in_fwd(h, g, w_in, name):
    n_rows, d = h.shape
    tm = _row_tile(n_rows)
    width = w_in.shape[1] // 2
    n_blk = width // LANES

    def body(h_ref, g_ref, w_ref, n_ref, us_ref, up_ref):
        hv = h_ref[...]
        nv = (hv * _rs(hv) * g_ref[...]).astype(_MXU)
        n_ref[...] = nv
        p = _dot(nv, w_ref[...])
        for k in range(n_blk):
            us_ref[k] = p[:, k * LANES:(k + 1) * LANES]
        up_ref[...] = p[:, width:]

    row = pl.BlockSpec((tm, d), lambda i: (i, 0))
    half = pl.BlockSpec((tm, width), lambda i: (i, 0))
    return pl.pallas_call(
        body, name=name, grid=(n_rows // tm,),
        out_shape=[jax.ShapeDtypeStruct((n_rows, d), _MXU), jax.ShapeDtypeStruct((n_blk, n_rows, LANES), F32),
                   jax.ShapeDtypeStruct((n_rows, width), F32)],
        in_specs=[row, _full((1, d)), _resident()], out_specs=[row, _lane_blocks(n_blk, tm), half],
        compiler_params=_params(),
    )(h, g, w_in)


def _gelu_parts(y):
    th = jnp.tanh(GELU_C0 * (y + GELU_C1 * (y * y * y)))
    return 0.5 * (1.0 + th), th


SCAN_LANES = 512


def _regroup(src_ref, dst_ref, seg):
    for k in range(src_ref.shape[0]):
        for j in range(seg):
            dst_ref[j * SUBLANES:(j + 1) * SUBLANES, k * LANES:(k + 1) * LANES] = src_ref[k, pl.ds(j, SUBLANES, stride=seg), :]


def _ungroup(src_ref, dst_ref, seg):
    for k in range(dst_ref.shape[0]):
        for j in range(seg):
            dst_ref[k, pl.ds(j, SUBLANES, stride=seg), :] = src_ref[j * SUBLANES:(j + 1) * SUBLANES, k * LANES:(k + 1) * LANES]


def _rows_to_sublanes(rows):
    rid = lax.broadcasted_iota(jnp.int32, (SUBLANES, rows[0].shape[1]), 0)
    out = jnp.broadcast_to(rows[0], rid.shape)
    for s in range(1, SUBLANES):
        out = jnp.where(rid == s, rows[s], out)
    return out


def _fill_powers(a_ref, tab_ref, seg, reverse):
    for cb in range(tab_ref.shape[3] // SCAN_LANES):
        cols = pl.ds(cb * SCAN_LANES, SCAN_LANES)
        ar = jnp.broadcast_to(a_ref[0, :, cols], (SUBLANES, SCAN_LANES))
        ai = jnp.broadcast_to(a_ref[1, :, cols], (SUBLANES, SCAN_LANES))
        if reverse:
            ai = -ai

        def step(t, carry, cols=cols, ar=ar, ai=ai):
            pr, pi = carry
            j = seg - 1 - t if reverse else t
            tab_ref[0, j, :, cols] = pr
            tab_ref[1, j, :, cols] = pi
            return pr * ar - pi * ai, pr * ai + pi * ar

        lax.fori_loop(0, seg, step, (ar, ai))


def _segment_scan(re_ref, im_ref, tab_ref, car_re, car_im, seg, reverse, x_refs=None):
    n_all = re_ref.shape[1]
    first = seg - 1 if reverse else 0
    sums = []
    for cb in range(n_all // SCAN_LANES):
        cols = pl.ds(cb * SCAN_LANES, SCAN_LANES)
        ar, ai = tab_ref[0, first, :, cols], tab_ref[1, first, :, cols]

        def local(t, carry, cols=cols, ar=ar, ai=ai):
            r0 = pl.multiple_of((seg - 1 - t if reverse else t) * SUBLANES, SUBLANES)
            xr, xi = carry
            nr = ar * xr - ai * xi + re_ref[pl.ds(r0, SUBLANES), cols]
            ni = ar * xi + ai * xr + im_ref[pl.ds(r0, SUBLANES), cols]
            re_ref[pl.ds(r0, SUBLANES), cols] = nr
            im_ref[pl.ds(r0, SUBLANES), cols] = ni
            return nr, ni

        zero = jnp.zeros((SUBLANES, SCAN_LANES), F32)
        fr, fi = lax.fori_loop(0, seg, local, (zero, zero))

        last = 0 if reverse else seg - 1
        sr, si = tab_ref[0, last, 0:1, cols], tab_ref[1, last, 0:1, cols]
        cr, ci = car_re[:, cols], car_im[:, cols]
        rows_r, rows_i = [None] * SUBLANES, [None] * SUBLANES
        for s in (range(SUBLANES - 1, -1, -1) if reverse else range(SUBLANES)):
            rows_r[s], rows_i[s] = cr, ci
            cr, ci = sr * cr - si * ci + fr[s:s + 1], sr * ci + si * cr + fi[s:s + 1]
        car_re[:, cols] = cr
        car_im[:, cols] = ci
        cmr, cmi = _rows_to_sublanes(rows_r), _rows_to_sublanes(rows_i)

        def fix(t, carry, cols=cols, cmr=cmr, cmi=cmi):
            j = seg - 1 - t if reverse else t
            r0 = pl.multiple_of(j * SUBLANES, SUBLANES)
            pr, pi = tab_ref[0, j, :, cols], tab_ref[1, j, :, cols]
            gr = re_ref[pl.ds(r0, SUBLANES), cols] + (pr * cmr - pi * cmi)
            gi = im_ref[pl.ds(r0, SUBLANES), cols] + (pr * cmi + pi * cmr)
            re_ref[pl.ds(r0, SUBLANES), cols] = gr
            im_ref[pl.ds(r0, SUBLANES), cols] = gi
            if x_refs is None:
                return carry
            nxr, nxi, accr, acci = carry
            xr, xi = x_refs[0][pl.ds(r0, SUBLANES), cols], x_refs[1][pl.ds(r0, SUBLANES), cols]
            return gr, gi, accr + (xr * nxr + xi * nxi), acci + (xr * nxi - xi * nxr)

        if x_refs is None:
            lax.fori_loop(0, seg, fix, 0)
        else:
            fin = lax.fori_loop(0, seg, fix, (cmr, cmi, zero, zero))
            sums.append((jnp.sum(fin[2], axis=0, keepdims=True), jnp.sum(fin[3], axis=0, keepdims=True)))
    return sums


def _ssm_forward(u, a_pair, b_re, b_im, c_re, c_im, wz1, wz2, d_skip, g_out, name, xchg=None):
    n_rows = u.shape[1]
    n_blk, _, n_state = b_re.shape
    width, n_all = n_blk * LANES, n_blk * n_state
    tm = _row_tile(n_rows)
    seg = tm // SUBLANES

    def body(u_ref, a_ref, bre_ref, bim_ref, cre_ref, cim_ref, wz1_ref, wz2_ref, d_ref, g_ref,
             xre_ref, xim_ref, o_ref, car_re, car_im, ug, out_scr, blocks, tab_ref):
        @pl.when(pl.program_id(0) == 0)
        def _():
            car_re[...] = jnp.zeros_like(car_re)
            car_im[...] = jnp.zeros_like(car_im)
            _fill_powers(a_ref, tab_ref, seg, reverse=False)

        _regroup(u_ref, ug, seg)
        ub = ug[...].astype(_MXU)
        for j in range(n_blk):
            uj = ub[:, j * LANES:(j + 1) * LANES]
            xre_ref[:, j * n_state:(j + 1) * n_state] = _dot(uj, bre_ref[j])
            xim_ref[:, j * n_state:(j + 1) * n_state] = _dot(uj, bim_ref[j])
        _segment_scan(xre_ref, xim_ref, tab_ref, car_re, car_im, seg, reverse=False)

        ssq = None
        for j in range(n_blk):
            sl = slice(j * LANES, (j + 1) * LANES)
            st = slice(j * n_state, (j + 1) * n_state)
            yc = _dot(xre_ref[:, st].astype(_MXU), cre_ref[j]) - _dot(xim_ref[:, st].astype(_MXU), cim_ref[j])
            y = yc + d_ref[:, sl] * ug[:, sl]
            cdf, _ = _gelu_parts(y)
            gy = (y * cdf).astype(_MXU)
            out = _dot(gy, wz1_ref[j]) * _sigmoid(_dot(gy, wz2_ref[j]))
            out_scr[:, sl] = out
            part = jnp.sum(out * out, axis=-1, keepdims=True)
            ssq = part if ssq is None else ssq + part
        r = lax.rsqrt(ssq / width + RMS_EPS)
        out_scr[...] = out_scr[...] * r * g_ref[...]
        _ungroup(out_scr, blocks, seg)
        for k in range(n_blk):
            o_ref[:, k * LANES:(k + 1) * LANES] = blocks[k].astype(_MXU)

    half = pl.BlockSpec((tm, width), lambda i: (i, 0))
    state = pl.BlockSpec((tm, n_all), lambda i: (i, 0))
    return _call(
        body, (u, a_pair, b_re, b_im, c_re, c_im, wz1, wz2, d_skip, g_out), name=name, grid=(n_rows // tm,),
        out_shape=[jax.ShapeDtypeStruct((n_rows, n_all), F32)] * 2 + [jax.ShapeDtypeStruct((n_rows, width), _MXU)],
        in_specs=[_lane_blocks(n_blk, tm), _full(a_pair.shape), _full(b_re.shape), _full(b_im.shape), _full(c_re.shape),
                  _full(c_im.shape), _full(wz1.shape), _full(wz2.shape), _full((1, width)), _full((1, width))],
        out_specs=[state, state, half],
        scratch_shapes=[pltpu.VMEM((1, n_all), F32)] * 2 + [pltpu.VMEM((tm, width), F32)] * 2
        + [pltpu.VMEM((n_blk, tm, LANES), F32), pltpu.VMEM((2, seg, SUBLANES, n_all), F32)], xchg=xchg)


def _ssm_backward(u, x_re, x_im, dcat, a_pair, b_re, b_im, c_re, c_im, wz1, wz2, d_skip, g_out, name, xchg=None):
    n_rows = u.shape[1]
    n_blk, _, n_state = b_re.shape
    width, n_all = n_blk * LANES, n_blk * n_state
    tm = _row_tile(n_rows)
    n_t, seg = n_rows // tm, tm // SUBLANES

    def body(u_ref, xre_ref, xim_ref, dc_ref, a_ref, bre_ref, bim_ref, cre_ref, cim_ref, wz1_ref, wz2_ref, d_ref, g_ref,
             du_ref, dg_ref, dd_ref, dwz1_ref, dwz2_ref, dcre_ref, dcim_ref, dbre_ref, dbim_ref, are_ref, aim_ref,
             car_re, car_im, gre, gim, ug, y_s, z1_s, sg_s, out_s, gy_s, tab_ref):
        @pl.when(pl.program_id(0) == 0)
        def _():
            for ref in (dg_ref, dd_ref, dwz1_ref, dwz2_ref, dcre_ref, dcim_ref, dbre_ref, dbim_ref, are_ref, aim_ref,
                        car_re, car_im):
                ref[...] = jnp.zeros_like(ref)
            _fill_powers(a_ref, tab_ref, seg, reverse=True)

        _regroup(u_ref, ug, seg)
        ssq = None
        for j in range(n_blk):
            sl = slice(j * LANES, (j + 1) * LANES)
            st = slice(j * n_state, (j + 1) * n_state)
            yc = _dot(xre_ref[:, st].astype(_MXU), cre_ref[j]) - _dot(xim_ref[:, st].astype(_MXU), cim_ref[j])
            y = yc + d_ref[:, sl] * ug[:, sl]
            cdf, _ = _gelu_parts(y)
            gy = (y * cdf).astype(_MXU)
            z1 = _dot(gy, wz1_ref[j])
            sg = _sigmoid(_dot(gy, wz2_ref[j]))
            out = z1 * sg
            y_s[:, sl], z1_s[:, sl], sg_s[:, sl], out_s[:, sl], gy_s[:, sl] = y, z1, sg, out, gy
            part = jnp.sum(out * out, axis=-1, keepdims=True)
            ssq = part if ssq is None else ssq + part
        r = lax.rsqrt(ssq / width + RMS_EPS)
        ohat = out_s[...] * r
        _regroup(dc_ref, out_s, seg)
        dcv = out_s[...]
        dg_ref[...] += jnp.sum(dcv * ohat, axis=0, keepdims=True)
        doh = dcv * g_ref[...]
        out_s[...] = r * (doh - ohat * (jnp.sum(doh * ohat, axis=-1, keepdims=True) / width))

        for j in range(n_blk):
            sl = slice(j * LANES, (j + 1) * LANES)
            st = slice(j * n_state, (j + 1) * n_state)
            dout, sg, z1, y = out_s[:, sl], sg_s[:, sl], z1_s[:, sl], y_s[:, sl]
            dz1 = (dout * sg).astype(_MXU)
            dz2 = (dout * z1 * sg * (1.0 - sg)).astype(_MXU)
            gy = gy_s[:, sl]
            dwz1_ref[j] += _dot_tn(gy, dz1)
            dwz2_ref[j] += _dot_tn(gy, dz2)
            dgy = _dot_nt(dz1, wz1_ref[j]) + _dot_nt(dz2, wz2_ref[j])
            cdf, th = _gelu_parts(y)
            dy = dgy * (cdf + y * (0.5 * (1.0 - th * th) * GELU_C0 * (1.0 + 3.0 * GELU_C1 * (y * y))))
            uj = ug[:, sl]
            dd_ref[:, sl] += jnp.sum(dy * uj, axis=0, keepdims=True)
            z1_s[:, sl] = d_ref[:, sl] * dy
            dyb = dy.astype(_MXU)
            dcre_ref[j] += _dot_tn(dyb, xre_ref[:, st].astype(_MXU))
            dcim_ref[j] -= _dot_tn(dyb, xim_ref[:, st].astype(_MXU))
            gre[:, st] = _dot_nt(dyb, cre_ref[j])
            gim[:, st] = -_dot_nt(dyb, cim_ref[j])

        sums = _segment_scan(gre, gim, tab_ref, car_re, car_im, seg, reverse=True, x_refs=(xre_ref, xim_ref))
        for cb, (sum_re, sum_im) in enumerate(sums):
            cols = pl.ds(cb * SCAN_LANES, SCAN_LANES)
            are_ref[:, cols] += sum_re
            aim_ref[:, cols] += sum_im

        for j in range(n_blk):
            sl = slice(j * LANES, (j + 1) * LANES)
            st = slice(j * n_state, (j + 1) * n_state)
            ujb = ug[:, sl].astype(_MXU)
            grb, gib = gre[:, st].astype(_MXU), gim[:, st].astype(_MXU)
            dbre_ref[j] += _dot_tn(ujb, grb)
            dbim_ref[j] += _dot_tn(ujb, gib)
            z1_s[:, sl] += _dot_nt(grb, bre_ref[j]) + _dot_nt(gib, bim_ref[j])
        _ungroup(z1_s, du_ref, seg)

    half = _lane_blocks(n_blk, tm, lambda i: n_t - 1 - i)
    state = pl.BlockSpec((tm, n_all), lambda i: (n_t - 1 - i, 0))
    small = [(1, width), (1, width), wz1.shape, wz2.shape, (n_blk, LANES, n_state), (n_blk, LANES, n_state),
             (n_blk, LANES, n_state), (n_blk, LANES, n_state), (1, n_all), (1, n_all)]
    return _call(
        body, (u, x_re, x_im, dcat, a_pair, b_re, b_im, c_re, c_im, wz1, wz2, d_skip, g_out), name=name, grid=(n_t,),
        out_shape=[jax.ShapeDtypeStruct((n_blk, n_rows, LANES), F32)] + [jax.ShapeDtypeStruct(s, F32) for s in small],
        in_specs=[half, state, state, half, _full(a_pair.shape), _full(b_re.shape), _full(b_im.shape), _full(c_re.shape),
                  _full(c_im.shape), _full(wz1.shape), _full(wz2.shape), _full((1, width)), _full((1, width))],
        out_specs=[half] + [_full(s) for s in small],
        scratch_shapes=[pltpu.VMEM((1, n_all), F32)] * 2 + [pltpu.VMEM((tm, n_all), F32)] * 2
        + [pltpu.VMEM((tm, width), F32)] * 5 + [pltpu.VMEM((tm, width), _MXU),
                                                pltpu.VMEM((2, seg, SUBLANES, n_all), F32)], xchg=xchg)


def _pool_counts(tile, tm, window):
    t = tile * tm + lax.broadcasted_iota(jnp.int32, (tm, 1), 0)
    return jnp.minimum(t + 1, window).astype(F32)


def _pool_fwd(proj, pool_w, scale, g_out, name):
    n_rows = proj.shape[0]
    n_grp, grp, _ = pool_w.shape
    width = n_grp * grp
    tm = _row_tile(n_rows)

    def body(u_ref, pw_ref, sc_ref, g_ref, o_ref, ext, y_s):
        i = pl.program_id(0)

        @pl.when(i == 0)
        def _():
            ext[0:POOL_HALO, :] = jnp.zeros((POOL_HALO, width), F32)

        ext[POOL_HALO:, :] = u_ref[...]
        ssq = None
        for gi, w in enumerate(POOL_WINDOWS):
            sl = slice(gi * grp, (gi + 1) * grp)
            tot = ext[POOL_HALO:, sl]
            for k in range(1, w):
                tot = tot + ext[POOL_HALO - k:POOL_HALO - k + tm, sl]
            pooled = tot / _pool_counts(i, tm, w) - u_ref[:, sl]
            y = _dot(pooled.astype(_MXU), pw_ref[gi]) * sc_ref[:, sl]
            y_s[:, sl] = y
            part = jnp.sum(y * y, axis=-1, keepdims=True)
            ssq = part if ssq is None else ssq + part
        r = lax.rsqrt(ssq / width + RMS_EPS)
        o_ref[...] = (y_s[...] * r * g_ref[...]).astype(_MXU)
        ext[0:POOL_HALO, :] = u_ref[tm - POOL_HALO:, :]

    half_in = pl.BlockSpec((tm, width), lambda i: (i, 0))
    half = pl.BlockSpec((tm, width), lambda i: (i, 0))
    return pl.pallas_call(
        body, name=name, grid=(n_rows // tm,), out_shape=jax.ShapeDtypeStruct((n_rows, width), _MXU),
        in_specs=[half_in, _full(pool_w.shape), _full((1, width)), _full((1, width))], out_specs=half,
        scratch_shapes=[pltpu.VMEM((tm + POOL_HALO, width), F32), pltpu.VMEM((tm, width), F32)],
        compiler_params=_params(),
    )(proj, pool_w, scale, g_out)


def _pool_bwd(proj, dcat, pool_w, scale, g_out, name):
    n_rows = proj.shape[0]
    n_grp, grp, _ = pool_w.shape
    width = n_grp * grp
    tm = _row_tile(n_rows)
    n_t = n_rows // tm
    halo_blocks = tm // POOL_HALO

    def body(u_ref, up_ref, dc_ref, pw_ref, sc_ref, g_ref, du_ref, dg_ref, dsc_ref, dpw_ref, ext, qext, y_s, pl_s):
        i = pl.program_id(0)
        tile = n_t - 1 - i

        @pl.when(i == 0)
        def _():
            for ref in (dg_ref, dsc_ref, dpw_ref):
                ref[...] = jnp.zeros_like(ref)
            qext[tm:, :] = jnp.zeros((POOL_HALO, width), F32)

        ext[0:POOL_HALO, :] = jnp.where(tile > 0, up_ref[...], 0.0)
        ext[POOL_HALO:, :] = u_ref[...]
        ssq = None
        for gi, w in enumerate(POOL_WINDOWS):
            sl = slice(gi * grp, (gi + 1) * grp)
            tot = ext[POOL_HALO:, sl]
            for k in range(1, w):
                tot = tot + ext[POOL_HALO - k:POOL_HALO - k + tm, sl]
            pooled = (tot / _pool_counts(tile, tm, w) - u_ref[:, sl]).astype(_MXU)
            pl_s[:, sl] = pooled
            y0 = _dot(pooled, pw_ref[gi])
            y_s[:, sl] = y0
            y = y0 * sc_ref[:, sl]
            part = jnp.sum(y * y, axis=-1, keepdims=True)
            ssq = part if ssq is None else ssq + part
        r = lax.rsqrt(ssq / width + RMS_EPS)
        y0 = y_s[...]
        yhat = y0 * sc_ref[...] * r
        dcv = dc_ref[...]
        dg_ref[...] += jnp.sum(dcv * yhat, axis=0, keepdims=True)
        dyh = dcv * g_ref[...]
        dy = r * (dyh - yhat * (jnp.sum(dyh * yhat, axis=-1, keepdims=True) / width))
        dsc_ref[...] += jnp.sum(dy * y0, axis=0, keepdims=True)
        y_s[...] = dy * sc_ref[...]
        for gi, w in enumerate(POOL_WINDOWS):
            sl = slice(gi * grp, (gi + 1) * grp)
            dm = y_s[:, sl].astype(_MXU)
            dpw_ref[gi] += _dot_tn(pl_s[:, sl], dm)
            dpooled = _dot_nt(dm, pw_ref[gi])
            y_s[:, sl] = dpooled
            qext[0:tm, sl] = dpooled / _pool_counts(tile, tm, w)
        for gi, w in enumerate(POOL_WINDOWS):
            sl = slice(gi * grp, (gi + 1) * grp)
            tot = qext[0:tm, sl]
            for k in range(1, w):
                tot = tot + qext[k:k + tm, sl]
            du_ref[:, sl] = tot - y_s[:, sl]
        qext[tm:, :] = qext[0:POOL_HALO, :]

    half_in = pl.BlockSpec((tm, width), lambda i: (n_t - 1 - i, 0))
    prev = pl.BlockSpec((POOL_HALO, width), lambda i: (jnp.maximum((n_t - 1 - i) * halo_blocks - 1, 0), 0))
    half = pl.BlockSpec((tm, width), lambda i: (n_t - 1 - i, 0))
    return pl.pallas_call(
        body, name=name, grid=(n_t,),
        out_shape=[jax.ShapeDtypeStruct((n_rows, width), F32), jax.ShapeDtypeStruct((1, width), F32),
                   jax.ShapeDtypeStruct((1, width), F32), jax.ShapeDtypeStruct(pool_w.shape, F32)],
        in_specs=[half_in, prev, half, _full(pool_w.shape), _full((1, width)), _full((1, width))],
        out_specs=[half, _full((1, width)), _full((1, width)), _full(pool_w.shape)],
        scratch_shapes=[pltpu.VMEM((tm + POOL_HALO, width), F32), pltpu.VMEM((tm + POOL_HALO, width), F32),
                        pltpu.VMEM((tm, width), F32), pltpu.VMEM((tm, width), _MXU)],
        compiler_params=_params(),
    )(proj, proj, dcat, pool_w, scale, g_out)


def _mix_out_fwd(cat_s, cat_p, h, g, wo_s, wo_p, name):
    n_rows, d = h.shape
    width = cat_s.shape[1]
    tm = _row_tile(n_rows)

    def body(cs_ref, cp_ref, h_ref, g_ref, ws_ref, wp_ref, m_ref, ho_ref):
        m = _dot(cs_ref[...], ws_ref[...]) + _dot(cp_ref[...], wp_ref[...])
        m_ref[...] = m
        ho_ref[...] = h_ref[...] + m * _rs(m) * g_ref[...]

    row = pl.BlockSpec((tm, d), lambda i: (i, 0))
    half = pl.BlockSpec((tm, width), lambda i: (i, 0))
    return pl.pallas_call(
        body, name=name, grid=(n_rows // tm,), out_shape=[jax.ShapeDtypeStruct((n_rows, d), F32)] * 2,
        in_specs=[half, half, row, _full((1, d)), _resident(), _resident()], out_specs=[row, row],
        compiler_params=_params(),
    )(cat_s, cat_p, h, g, wo_s, wo_p)


def _mix_out_bwd(dho, mixed, g, wo_s, wo_p, name):
    n_rows, d = mixed.shape
    width = wo_s.shape[0]
    n_blk = width // LANES
    tm = _row_tile(n_rows)

    def body(dho_ref, m_ref, g_ref, ws_ref, wp_ref, dm_ref, dcs_ref, dcp_ref, dg_ref):
        @pl.when(pl.program_id(0) == 0)
        def _():
            dg_ref[...] = jnp.zeros_like(dg_ref)

        m = m_ref[...]
        r = _rs(m)
        mh = m * r
        dy = dho_ref[...]
        dg_ref[...] += jnp.sum(dy * mh, axis=0, keepdims=True)
        dmh = dy * g_ref[...]
        dm = (r * (dmh - mh * jnp.mean(dmh * mh, axis=-1, keepdims=True))).astype(_MXU)
        dm_ref[...] = dm
        dcs = _dot_nt(dm, ws_ref[...])
        for k in range(n_blk):
            dcs_ref[k] = dcs[:, k * LANES:(k + 1) * LANES]
        dcp_ref[...] = _dot_nt(dm, wp_ref[...])

    row = pl.BlockSpec((tm, d), lambda i: (i, 0))
    half = pl.BlockSpec((tm, width), lambda i: (i, 0))
    return pl.pallas_call(
        body, name=name, grid=(n_rows // tm,),
        out_shape=[jax.ShapeDtypeStruct((n_rows, d), _MXU), jax.ShapeDtypeStruct((n_blk, n_rows, LANES), F32),
                   jax.ShapeDtypeStruct((n_rows, width), F32), jax.ShapeDtypeStruct((1, d), F32)],
        in_specs=[row, row, _full((1, d)), _resident(), _resident()],
        out_specs=[row, _lane_blocks(n_blk, tm), half, _full((1, d))], compiler_params=_params(),
    )(dho, mixed, g, wo_s, wo_p)


def _mix_in_bwd(du_s, du_p, h, dho, g, wi_s, wi_p, name):
    n_rows, d = h.shape
    width = du_p.shape[1]
    n_blk = width // LANES
    tm = _row_tile(n_rows)

    def body(dus_ref, dup_ref, h_ref, dho_ref, g_ref, ws_ref, wp_ref, dh_ref, dp_ref, dg_ref):
        @pl.when(pl.program_id(0) == 0)
        def _():
            dg_ref[...] = jnp.zeros_like(dg_ref)

        for k in range(n_blk):
            dp_ref[:, k * LANES:(k + 1) * LANES] = dus_ref[k].astype(_MXU)
        dup = dup_ref[...].astype(_MXU)
        dp_ref[:, width:2 * width] = dup
        dn = _dot_nt(dp_ref[:, 0:width], ws_ref[...]) + _dot_nt(dup, wp_ref[...])
        hv = h_ref[...]
        r = _rs(hv)
        hh = hv * r
        dg_ref[...] += jnp.sum(dn * hh, axis=0, keepdims=True)
        dhh = dn * g_ref[...]
        dh_ref[...] = dho_ref[...] + r * (dhh - hh * jnp.mean(dhh * hh, axis=-1, keepdims=True))

    row = pl.BlockSpec((tm, d), lambda i: (i, 0))
    half = pl.BlockSpec((tm, width), lambda i: (i, 0))
    return pl.pallas_call(
        body, name=name, grid=(n_rows // tm,),
        out_shape=[jax.ShapeDtypeStruct((n_rows, d), F32), jax.ShapeDtypeStruct((n_rows, 2 * width), _MXU),
                   jax.ShapeDtypeStruct((1, d), F32)],
        in_specs=[_lane_blocks(n_blk, tm), half, row, row, _full((1, d)), _resident(), _resident()],
        out_specs=[row, pl.BlockSpec((tm, 2 * width), lambda i: (i, 0)), _full((1, d))], compiler_params=_params(),
    )(du_s, du_p, h, dho, g, wi_s, wi_p)


def _adamw_update(w_ref, gv, m_ref, v_ref, d_ref, mo_ref, vo_ref):
    mn = ADAM_B1 * m_ref[...] + (1.0 - ADAM_B1) * gv
    vn = ADAM_B2 * v_ref[...] + (1.0 - ADAM_B2) * (gv * gv)
    m_hat = mn / (1.0 - ADAM_B1 ** ADAM_STEP)
    v_hat = vn / (1.0 - ADAM_B2 ** ADAM_STEP)
    d_ref[...] = -ADAM_LR * (m_hat / (jnp.sqrt(v_hat) + ADAM_EPS) + ADAM_WD * w_ref[...])
    mo_ref[...] = mn
    vo_ref[...] = vn


def _adamw(w, g, m, v, name):
    def body(w_ref, g_ref, m_ref, v_ref, d_ref, mo_ref, vo_ref):
        _adamw_update(w_ref, g_ref[...], m_ref, v_ref, d_ref, mo_ref, vo_ref)

    spec = _full(w.shape)
    return pl.pallas_call(
        body, name=name, grid=(1,), out_shape=[jax.ShapeDtypeStruct(w.shape, F32)] * 3,
        in_specs=[spec] * 4, out_specs=[spec] * 3, compiler_params=_params(),
    )(w, g, m, v)


def _adamw_many(ws, gs, ms, vs, name):
    n = len(ws)

    def body(*refs):
        w_refs, g_refs, m_refs, v_refs, d_refs, mo_refs, vo_refs = _split(refs, (n,) * 7)
        for k in range(n):
            _adamw_update(w_refs[k], g_refs[k][...], m_refs[k], v_refs[k], d_refs[k], mo_refs[k], vo_refs[k])

    specs = [_full(w.shape) for w in ws]
    shapes = [jax.ShapeDtypeStruct(w.shape, F32) for w in ws]
    outs = pl.pallas_call(
        body, name=name, grid=(1,), out_shape=shapes * 3, in_specs=specs * 4, out_specs=specs * 3,
        compiler_params=_params(),
    )(*ws, *gs, *ms, *vs)
    return outs[:n], outs[n:2 * n], outs[2 * n:]


def _adamw_slots(w, slots, m, v, name, after=()):
    def body(w_ref, s_ref, m_ref, v_ref, *rest):
        g_ref, d_ref, mo_ref, vo_ref = rest[len(after):]
        gv = s_ref[0].astype(F32)
        for k in range(1, N_DEV):
            gv = gv + s_ref[k].astype(F32)
        g_ref[...] = gv
        _adamw_update(w_ref, gv, m_ref, v_ref, d_ref, mo_ref, vo_ref)

    spec = _full(w.shape)
    return pl.pallas_call(
        body, name=name, grid=(1,), out_shape=[jax.ShapeDtypeStruct(w.shape, F32)] * 4,
        in_specs=[spec, _full(slots.shape), spec, spec] + [pl.BlockSpec(memory_space=pl.ANY)] * len(after),
        out_specs=[spec] * 4, compiler_params=_params(),
    )(w, slots, m, v, *after)


def _discretize(lam_re, lam_im, log_dt, b_re, b_im):
    dt = jnp.exp(log_dt)[:, None]
    decay = jnp.exp(lam_re * dt)
    ang = lam_im * dt
    a_re = decay * jnp.cos(ang)
    a_im = decay * jnp.sin(ang)
    nr = a_re - 1.0
    den = lam_re * lam_re + lam_im * lam_im
    q_re = (nr * lam_re + a_im * lam_im) / den
    q_im = (a_im * lam_re - nr * lam_im) / den
    bb_re = q_re[..., None] * b_re - q_im[..., None] * b_im
    bb_im = q_re[..., None] * b_im + q_im[..., None] * b_re
    return a_re, a_im, bb_re, bb_im


def _block_diag(p, n_blk):
    g, r, c = p.shape
    per = g // n_blk
    eye = jnp.eye(per, dtype=p.dtype)
    return jnp.einsum("jgrc,gk->jgrkc", p.reshape(n_blk, per, r, c), eye).reshape(n_blk, per * r, per * c)


def _block_diag_t(m, g):
    n_blk = m.shape[0]
    per = g // n_blk
    r, c = m.shape[1] // per, m.shape[2] // per
    eye = jnp.eye(per, dtype=m.dtype)
    return jnp.einsum("jgrkc,gk->jgrc", m.reshape(n_blk, per, r, per, c), eye).reshape(g, r, c)


def _pack_rows(parts, cols, multiple):
    flat = jnp.concatenate([p.reshape(-1) for p in parts])
    size = -(-flat.shape[0] // (cols * multiple)) * cols * multiple
    return jnp.pad(flat, (0, size - flat.shape[0])).reshape(-1, cols)


def _unpack(flat, shapes):
    out, pos = [], 0
    flat = flat.reshape(-1)
    for s in shapes:
        n = int(np.prod(s))
        out.append(flat[pos:pos + n].reshape(s))
        pos += n
    return out


def kernel(x, meta_tokens, ffn1_pre_norm, ffn1_post_norm, ffn1_w_gate, ffn1_w_up, ffn1_w_down, mix_pre_norm, mix_post_norm, w_in, ssm_lambda_re, ssm_lambda_im, ssm_log_dt, ssm_b_re, ssm_b_im, ssm_c_re, ssm_c_im, ssm_d, ssm_w_glu, pool_w, pool_scale, ssm_out_norm, pool_out_norm, w_out, ffn2_pre_norm, ffn2_post_norm, ffn2_w_gate, ffn2_w_up, ffn2_w_down, loss_target, m_meta_tokens, m_ffn1_pre_norm, m_ffn1_post_norm, m_ffn1_w_gate, m_ffn1_w_up, m_ffn1_w_down, m_mix_pre_norm, m_mix_post_norm, m_w_in, m_ssm_lambda_re, m_ssm_lambda_im, m_ssm_log_dt, m_ssm_b_re, m_ssm_b_im, m_ssm_c_re, m_ssm_c_im, m_ssm_d, m_ssm_w_glu, m_pool_w, m_pool_scale, m_ssm_out_norm, m_pool_out_norm, m_w_out, m_ffn2_pre_norm, m_ffn2_post_norm, m_ffn2_w_gate, m_ffn2_w_up, m_ffn2_w_down, v_meta_tokens, v_ffn1_pre_norm, v_ffn1_post_norm, v_ffn1_w_gate, v_ffn1_w_up, v_ffn1_w_down, v_mix_pre_norm, v_mix_post_norm, v_w_in, v_ssm_lambda_re, v_ssm_lambda_im, v_ssm_log_dt, v_ssm_b_re, v_ssm_b_im, v_ssm_c_re, v_ssm_c_im, v_ssm_d, v_ssm_w_glu, v_pool_w, v_pool_scale, v_ssm_out_norm, v_pool_out_norm, v_w_out, v_ffn2_pre_norm, v_ffn2_post_norm, v_ffn2_w_gate, v_ffn2_w_up, v_ffn2_w_down):
    args = dict(locals())
    names = ["meta_tokens", "ffn1_pre_norm", "ffn1_post_norm", "ffn1_w_gate", "ffn1_w_up", "ffn1_w_down", "mix_pre_norm",
             "mix_post_norm", "w_in", "ssm_lambda_re", "ssm_lambda_im", "ssm_log_dt", "ssm_b_re", "ssm_b_im", "ssm_c_re",
             "ssm_c_im", "ssm_d", "ssm_w_glu", "pool_w", "pool_scale", "ssm_out_norm", "pool_out_norm", "w_out",
             "ffn2_pre_norm", "ffn2_post_norm", "ffn2_w_gate", "ffn2_w_up", "ffn2_w_down"]
    sharded = ("meta_tokens", "ffn1_w_gate", "ffn1_w_up", "ffn1_w_down", "w_in", "w_out", "ffn2_w_gate", "ffn2_w_up",
               "ffn2_w_down")
    small = [n for n in names if n not in sharded]

    d = x.shape[-1]
    width = d // 2
    n_grp = ssm_lambda_re.shape[1]
    n_blk = width // LANES

    def stacked(gathered):
        return gathered.reshape(-1, d).astype(_MXU)

    def chunked(m):
        return m.reshape(N_DEV, -1, d)

    s_gate1, s_up1, s_down1 = _to_wire([ffn1_w_gate[0].T, ffn1_w_up[0].T, ffn1_w_down[0]], "wire_ffn1")
    s_win, s_wout, s_gate2, s_up2, s_down2 = _to_wire(
        [w_in[0], w_out[0], ffn2_w_gate[0].T, ffn2_w_up[0].T, ffn2_w_down[0]], "wire_rest")
    g_gate1, g_up1, g_down1, meta_all = _gather_two_level([s_gate1, s_up1, s_down1, meta_tokens], "gather_ffn1")
    wgt1, wut1, wd1 = stacked(g_gate1), stacked(g_up1), stacked(g_down1)
    meta_full = jnp.transpose(meta_all, (1, 0, 2)).reshape(N_META, d)

    (a1, b1, f1, h1, h0), (g_win, g_wout, g_gate2) = _ffn_fwd(
        None, ffn1_pre_norm, ffn1_post_norm, wgt1, wut1, wd1, "ffn1_fwd",
        xchg=_Xchg([s_win, s_wout, s_gate2], ["gather"] * 3), meta=meta_full, tokens=x[0])
    w_in_f, w_out_f = stacked(g_win), stacked(g_wout)

    a_re, a_im, bb_re, bb_im = _discretize(ssm_lambda_re[0], ssm_lambda_im[0], ssm_log_dt[0], ssm_b_re[0], ssm_b_im[0])
    a_pair = jnp.stack([a_re.reshape(1, -1), a_im.reshape(1, -1)])
    bmat_re = _block_diag(jnp.swapaxes(bb_re, 1, 2), n_blk).astype(_MXU)
    bmat_im = _block_diag(jnp.swapaxes(bb_im, 1, 2), n_blk).astype(_MXU)
    cmat_re = _block_diag(jnp.swapaxes(ssm_c_re[0], 1, 2), n_blk).astype(_MXU)
    cmat_im = _block_diag(jnp.swapaxes(ssm_c_im[0], 1, 2), n_blk).astype(_MXU)
    wz1 = _block_diag(ssm_w_glu[0][:, :, :SSM_GROUP_CH], n_blk).astype(_MXU)
    wz2 = _block_diag(ssm_w_glu[0][:, :, SSM_GROUP_CH:], n_blk).astype(_MXU)
    pool_wm = pool_w[0].astype(_MXU)

    n2, u_s, u_p = _mix_in_fwd(h1, mix_pre_norm, w_in_f, "mix_in_fwd")
    (x_re, x_im, cat_s), (g_up2, g_down2) = _ssm_forward(
        u_s, a_pair, bmat_re, bmat_im, cmat_re, cmat_im, wz1, wz2, ssm_d, ssm_out_norm, "ssm_fwd",
        xchg=_Xchg([s_up2, s_down2], ["gather"] * 2))
    wgt2, wut2, wd2 = stacked(g_gate2), stacked(g_up2), stacked(g_down2)
    cat_p = _pool_fwd(u_p, pool_wm, pool_scale, pool_out_norm, "pool_fwd")
    wo_s, wo_p = w_out_f[:width], w_out_f[width:]
    mixed, h2 = _mix_out_fwd(cat_s, cat_p, h1, mix_post_norm, wo_s, wo_p, "mix_out_fwd")

    (a2, b2, f2, dh3, sq_err), _ = _ffn_fwd(
        h2, ffn2_pre_norm, ffn2_post_norm, wgt2, wut2, wd2, "ffn2_fwd", target=loss_target[0])

    g, slots = {}, {}
    (dh2, da2, db2, s2, df2, nf2, g["ffn2_pre_norm"], g["ffn2_post_norm"]), _ = _ffn_bwd(
        dh3, h2, f2, a2, b2, ffn2_pre_norm, ffn2_post_norm, wgt2, wut2, wd2, "ffn2_bwd")
    dgate2, _ = _wgrad(da2, nf2, "ffn2_dgate")
    dup2, _ = _wgrad(db2, nf2, "ffn2_dup")
    ddown2, _ = _wgrad(s2, df2, "ffn2_ddown")

    dmixed, dcat_s, dcat_p, g["mix_post_norm"] = _mix_out_bwd(dh2, mixed, mix_post_norm, wo_s, wo_p, "mix_out_bwd")
    dwout = jnp.concatenate([_wgrad(cat_s, dmixed, "dwout_s")[0], _wgrad(cat_p, dmixed, "dwout_p")[0]], axis=0)
    ((du_s, g["ssm_out_norm"], g["ssm_d"], dwz1, dwz2, dcm_re, dcm_im, dbm_re, dbm_im, acc_re, acc_im),
     (slots["ffn2_w_gate"], slots["ffn2_w_up"], slots["ffn2_w_down"], slots["w_out"])) = _ssm_backward(
        u_s, x_re, x_im, dcat_s, a_pair, bmat_re, bmat_im, cmat_re, cmat_im, wz1, wz2, ssm_d, ssm_out_norm, "ssm_bwd",
        xchg=_Xchg([chunked(dgate2), chunked(dup2), chunked(ddown2), chunked(dwout)], ["scatter"] * 4))
    du_p, g["pool_out_norm"], g["pool_scale"], dpw = _pool_bwd(u_p, dcat_p, pool_wm, pool_scale, pool_out_norm, "pool_bwd")
    wi_s, wi_p = w_in_f[:, :width], w_in_f[:, width:]
    dh1, dproj, g["mix_pre_norm"] = _mix_in_bwd(du_s, du_p, h1, dh2, mix_pre_norm, wi_s, wi_p, "mix_in_bwd")
    dwin, _ = _wgrad(n2, dproj, "dwin")

    g["ssm_c_re"] = jnp.swapaxes(_block_diag_t(jnp.swapaxes(dcm_re, 1, 2), n_grp), 1, 2)[None]
    g["ssm_c_im"] = jnp.swapaxes(_block_diag_t(jnp.swapaxes(dcm_im, 1, 2), n_grp), 1, 2)[None]
    g["ssm_w_glu"] = jnp.concatenate([_block_diag_t(dwz1, n_grp), _block_diag_t(dwz2, n_grp)], axis=-1)[None]
    dbb_re = jnp.swapaxes(_block_diag_t(dbm_re, n_grp), 1, 2)
    dbb_im = jnp.swapaxes(_block_diag_t(dbm_im, n_grp), 1, 2)
    da_re, da_im = acc_re.reshape(a_re.shape), acc_im.reshape(a_re.shape)
    _, disc_vjp = jax.vjp(_discretize, ssm_lambda_re[0], ssm_lambda_im[0], ssm_log_dt[0], ssm_b_re[0], ssm_b_im[0])
    d_lre, d_lim, d_ldt, d_bre, d_bim = disc_vjp((da_re, da_im, dbb_re, dbb_im))
    g["ssm_lambda_re"], g["ssm_lambda_im"], g["ssm_log_dt"] = d_lre[None], d_lim[None], d_ldt[None]
    g["ssm_b_re"], g["ssm_b_im"] = d_bre[None], d_bim[None]
    g["pool_w"] = dpw[None]

    late = ["ffn1_pre_norm", "ffn1_post_norm"]
    early = [n for n in small if n not in late]
    early_vec = _pack_rows([g[n] for n in early] + [sq_err[:, :1]], 1024, SUBLANES)
    mix_started, mix_token = _split_start([chunked(dwin), early_vec], ["scatter", "gather"], "reduce_mix_start")
    (dx, da1, db1, s1, df1, nf1, g["ffn1_pre_norm"], g["ffn1_post_norm"], dmeta_part), _ = _ffn_bwd(
        dh1, h0, f1, a1, b1, ffn1_pre_norm, ffn1_post_norm, wgt1, wut1, wd1, "ffn1_bwd", split_meta=True,
        after=(mix_token,))
    slots["w_in"], recv_early = _split_wait(mix_started, [dmeta_part], "reduce_mix_wait")
    late_vec = _pack_rows([g[n] for n in late] + [dmeta_part], 1024, SUBLANES)
    dgate1, _ = _wgrad(da1, nf1, "ffn1_dgate")
    gate_started, token = _split_start([chunked(dgate1), late_vec], ["scatter", "gather"], "reduce_gate_start")
    dup1, _ = _wgrad(db1, nf1, "ffn1_dup", after=(token,))
    up_started, token = _split_start([chunked(dup1)], ["scatter"], "reduce_up_start")
    ddown1, _ = _wgrad(s1, df1, "ffn1_ddown", after=(token,))
    down_started, token = _split_start([chunked(ddown1)], ["scatter"], "reduce_down_start")
    waits = {"ffn1_w_gate": (gate_started, "reduce_gate_wait"), "ffn1_w_up": (up_started, "reduce_up_wait"),
             "ffn1_w_down": (down_started, "reduce_down_wait")}

    summed = _unpack(_sum_slots(recv_early, "sum_small_grads"), [g[n].shape for n in early] + [(1,)])
    for n, val in zip(early, summed):
        g[n] = val
    loss = (0.5 / d) * summed[-1][0]

    delta, new_m, new_v = {}, {}, {}
    after = (token,)
    for n in sorted(sharded[1:], key=lambda n: list(waits).index(n) if n in waits else -1) + ["meta_tokens"]:
        shape = args[n].shape
        two_d = (-1, shape[-1])
        w2, m2, v2 = args[n].reshape(two_d), args["m_" + n].reshape(two_d), args["v_" + n].reshape(two_d)
        if n in waits:
            slots[n], *rest = _split_wait(waits[n][0], after, waits[n][1])
            if rest:
                g[late[0]], g[late[1]], dmeta = _unpack(_sum_slots(rest[0], "sum_last_grads"), [(1, d), (1, d), (N_META, d)])
                g["meta_tokens"] = lax.dynamic_slice_in_dim(dmeta, _my_slot() * (d // N_DEV), d // N_DEV, axis=1)
        if n == "meta_tokens":
            dl, mn, vn = _adamw(w2, g[n], m2, v2, "adamw_" + n)
        elif n.endswith("gate") or n.endswith("up"):
            raw = _adamw_slots(w2.T, slots[n], m2.T, v2.T, "adamw_" + n, after=after)
            gs, dl, mn, vn = [t.T for t in raw]
        else:
            raw = gs, dl, mn, vn = _adamw_slots(w2, slots[n], m2, v2, "adamw_" + n, after=after)
        if n != "meta_tokens":
            g[n] = gs[None]
            after = (raw[1],)
        delta[n], new_m[n], new_v[n] = dl.reshape(shape), mn.reshape(shape), vn.reshape(shape)
    outs = _adamw_many([args[n] for n in small], [g[n] for n in small], [args["m_" + n] for n in small],
                       [args["v_" + n] for n in small], "adamw_small")
    for store, vals in zip((delta, new_m, new_v), outs):
        for n, val in zip(small, vals):
            store[n] = val

    grad_x = dx[None]
    return (loss, grad_x, *[g[n] for n in names], *[delta[n] for n in names], *[new_m[n] for n in names],
            *[new_v[n] for n in names])
```

```python
import functools
import math

import jax
import jax.numpy as jnp
import numpy as np
from jax import lax
from jax.experimental import pallas as pl
from jax.experimental.pallas import tpu as pltpu

F32 = jnp.float32
_MXU = jnp.bfloat16
_ACT = jnp.bfloat16
_WIRE = jnp.bfloat16

N_DEV = 8
N_META = 16
RMS_EPS = 1e-6
SSM_GROUP_CH = 16
LANES = 128
SUBLANES = 8
POOL_WINDOWS = (2, 4, 8, 16)
POOL_HALO = 16
ADAM_LR = 0.001
ADAM_B1 = 0.9
ADAM_B2 = 0.999
ADAM_EPS = 1e-08
ADAM_WD = 0.01
ADAM_STEP = 10
GELU_C0 = math.sqrt(2.0 / math.pi)
GELU_C1 = 0.044715
VMEM_LIMIT = 62 * 1024 * 1024

_NT = (((1,), (1,)), ((), ()))
_TN = (((0,), (0,)), ((), ()))


def _dot(a, b):
    return jnp.dot(a, b, preferred_element_type=F32)


def _dot_nt(a, b):
    return lax.dot_general(a, b, _NT, preferred_element_type=F32)


def _dot_tn(a, b):
    return lax.dot_general(a, b, _TN, preferred_element_type=F32)


def _rs(x):
    return lax.rsqrt(jnp.mean(x * x, axis=-1, keepdims=True) + RMS_EPS)


def _sigmoid(x):
    return 0.5 * jnp.tanh(0.5 * x) + 0.5


def _row_tile(n_rows, largest=432):
    for t in (432, 304, 48, 16):
        if t <= largest and n_rows % t == 0:
            return t
    raise ValueError(n_rows)


def _ff_chunk(d_ff):
    return d_ff // 2 if (d_ff // 2) % LANES == 0 else d_ff


def _params(n_axes=1):
    return pltpu.CompilerParams(dimension_semantics=("arbitrary",) * n_axes, vmem_limit_bytes=VMEM_LIMIT)


def _resident():
    return pl.BlockSpec(memory_space=pltpu.VMEM)


def _full(shape):
    nd = len(shape)
    return pl.BlockSpec(shape, lambda *_: (0,) * nd)


def _lane_blocks(n_blk, tm, tile_of=lambda i: i):
    return pl.BlockSpec((n_blk, tm, LANES), lambda i: (0, tile_of(i), 0))


PEER_ORDER = (1, 2, 4, 3, 5, 6, 7)


def _split(refs, counts):
    out, pos = [], 0
    for n in counts:
        out.append(refs[pos:pos + n])
        pos += n
    return out


def _peer(r):
    x, y, c = lax.axis_index("x"), lax.axis_index("y"), lax.axis_index("c")
    return (1 - x if r & 4 else x, 1 - y if r & 2 else y, 1 - c if r & 1 else c)


def _my_slot():
    return 4 * lax.axis_index("x") + 2 * lax.axis_index("y") + lax.axis_index("c")


class _Xchg:
    def __init__(self, srcs, kinds):
        self.srcs, self.kinds, self.n = list(srcs), list(kinds), len(srcs)
        self.out_shape = [jax.ShapeDtypeStruct((N_DEV,) + s.shape if k == "gather" else s.shape, s.dtype)
                          for s, k in zip(self.srcs, self.kinds)]
        self.specs = [pl.BlockSpec(memory_space=pl.ANY)] * self.n
        self.scratch = [pltpu.SemaphoreType.DMA((self.n * (N_DEV - 1),)), pltpu.SemaphoreType.DMA((self.n * (N_DEV - 1),)),
                        pltpu.SemaphoreType.DMA((self.n,))]

    def copies(self, src, dst, sems):
        send_sems, recv_sems, local_sems = sems
        me = _my_slot()
        out = []
        for a in range(self.n):
            mine = src[a] if self.kinds[a] == "gather" else src[a].at[me]
            out.append(pltpu.make_async_copy(mine, dst[a].at[me], local_sems.at[a]))
            for r in PEER_ORDER:
                px, py, pc = _peer(r)
                part = src[a] if self.kinds[a] == "gather" else src[a].at[4 * px + 2 * py + pc]
                k = a * (N_DEV - 1) + r - 1
                out.append(pltpu.make_async_remote_copy(
                    src_ref=part, dst_ref=dst[a].at[me], send_sem=send_sems.at[k], recv_sem=recv_sems.at[k],
                    device_id=(px, py, pc), device_id_type=pl.DeviceIdType.MESH))
        return out

    def start(self, src, dst, sems):
        for cp in self.copies(src, dst, sems):
            cp.start()

    def wait(self, src, dst, sems):
        for cp in self.copies(src, dst, sems):
            cp.wait()


class _NoXchg:
    n, srcs, out_shape, specs, scratch = 0, [], [], [], []

    def start(self, *_):
        pass

    wait = start


def _call(body, args, *, name, grid, out_shape, in_specs, out_specs, scratch_shapes=(), xchg=None, after=()):
    xc = xchg or _NoXchg()
    counts = (len(in_specs), xc.n, len(after), len(out_shape), xc.n, len(scratch_shapes), len(xc.scratch))

    def wrapped(*refs):
        ins, xsrc, _, outs, xdst, scr, sems = _split(refs, counts)
        ids = [pl.program_id(k) for k in range(len(grid))]
        if xc.n:
            @pl.when(functools.reduce(jnp.logical_and, [i == 0 for i in ids]))
            def _():
                xc.start(xsrc, xdst, sems)

        body(*ins, *outs, *scr)
        if xc.n:
            @pl.when(functools.reduce(jnp.logical_and, [i == g - 1 for i, g in zip(ids, grid)]))
            def _():
                xc.wait(xsrc, xdst, sems)

    res = pl.pallas_call(
        wrapped, name=name, grid=grid, out_shape=list(out_shape) + xc.out_shape,
        in_specs=list(in_specs) + xc.specs + [pl.BlockSpec(memory_space=pl.ANY)] * len(after),
        out_specs=list(out_specs) + xc.specs,
        scratch_shapes=list(scratch_shapes) + xc.scratch, compiler_params=_params(len(grid)),
    )(*args, *xc.srcs, *after)
    return res[:len(out_shape)], res[len(out_shape):]


def _exchange(srcs, kinds, name):
    xc = _Xchg(srcs, kinds)

    def body(*refs):
        src, dst, sems = _split(refs, (xc.n, xc.n, 3))
        xc.start(src, dst, sems)
        xc.wait(src, dst, sems)

    return pl.pallas_call(body, name=name, out_shape=xc.out_shape, in_specs=xc.specs, out_specs=xc.specs,
                          scratch_shapes=xc.scratch)(*srcs)


def _split_copies(kinds, src, land, send_sems, recv_sems):
    me = _my_slot()
    out = []
    for a, kind in enumerate(kinds):
        for r in PEER_ORDER:
            px, py, pc = _peer(r)
            k = a * (N_DEV - 1) + r - 1
            out.append(pltpu.make_async_remote_copy(
                src_ref=src[a] if kind == "gather" else src[a].at[4 * px + 2 * py + pc], dst_ref=land[a].at[me],
                send_sem=send_sems.at[k], recv_sem=recv_sems.at[k], device_id=(px, py, pc),
                device_id_type=pl.DeviceIdType.MESH))
    return out


_SPLIT_EFFECT = pltpu.SideEffectType.DATAFLOW_SIDE_EFFECTING


def _split_start(srcs, kinds, name):
    n = len(srcs)
    hbm, sem = pl.BlockSpec(memory_space=pltpu.HBM), pl.BlockSpec(memory_space=pltpu.SEMAPHORE)
    lands = [lax.empty((N_DEV,) + s.shape if k == "gather" else s.shape, s.dtype) for s, k in zip(srcs, kinds)]

    def body(*refs):
        src, land, (send_sems, recv_sems), _, (token,) = _split(refs, (n, n, 2, 2 * n, 1))
        for cp in _split_copies(kinds, src, land, send_sems, recv_sems):
            cp.start()
        token[...] = jnp.zeros_like(token)

    n_sem = n * (N_DEV - 1)
    out = pl.pallas_call(
        body, name=name,
        out_shape=[pltpu.SemaphoreType.DMA((n_sem,)), pltpu.SemaphoreType.DMA((n_sem,))]
        + [pltpu.HBM(a.shape, a.dtype) for a in list(srcs) + lands] + [jax.ShapeDtypeStruct((SUBLANES, LANES), F32)],
        in_specs=[hbm] * (2 * n), out_specs=[sem, sem] + [hbm] * (2 * n) + [pl.BlockSpec(memory_space=pltpu.VMEM)],
        input_output_aliases={k: 2 + k for k in range(2 * n)},
        compiler_params=pltpu.CompilerParams(has_side_effects=_SPLIT_EFFECT),
    )(*[pltpu.with_memory_space_constraint(a, pltpu.HBM) for a in list(srcs) + lands])
    return (kinds, out[0], out[1], out[2:2 + n], out[2 + n:2 + 2 * n]), out[-1]


def _split_wait(started, after, name):
    kinds, send_sems, recv_sems, srcs, lands = started
    n = len(srcs)
    hbm, sem = pl.BlockSpec(memory_space=pltpu.HBM), pl.BlockSpec(memory_space=pltpu.SEMAPHORE)

    def body(*refs):
        src, land, (send_sems, recv_sems) = _split(refs, (n, n, 2))[:3]
        for cp in _split_copies(kinds, src, land, send_sems, recv_sems):
            cp.wait_send()
            cp.wait_recv()

    out = pl.pallas_call(
        body, name=name, out_shape=[pltpu.HBM(a.shape, a.dtype) for a in list(srcs) + list(lands)],
        in_specs=[hbm] * (2 * n) + [sem, sem] + [pl.BlockSpec(memory_space=pl.ANY)] * len(after),
        out_specs=[hbm] * (2 * n), input_output_aliases={k: k for k in range(2 * n)},
        compiler_params=pltpu.CompilerParams(has_side_effects=_SPLIT_EFFECT),
    )(*srcs, *lands, send_sems, recv_sems, *after)
    me = _my_slot()
    filled = []
    for kind, sent, land in zip(kinds, out[:n], out[n:]):
        mine = sent[None] if kind == "gather" else lax.dynamic_slice_in_dim(sent, me, 1, axis=0)
        filled.append(lax.dynamic_update_slice_in_dim(land, mine, me, axis=0))
    return filled


def _gather_two_level(srcs, name):
    n = len(srcs)
    out_shape = [jax.ShapeDtypeStruct((N_DEV,) + s.shape, s.dtype) for s in srcs]
    chips = (2, 4, 6)

    def body(*refs):
        src, dst, (send_sems, recv_sems, local_sems) = _split(refs, (n, n, 3))
        x, y, c = lax.axis_index("x"), lax.axis_index("y"), lax.axis_index("c")
        me = 4 * x + 2 * y + c
        sibling = (x, y, 1 - c)

        def copy(a, k, slot, to, from_src=False):
            return pltpu.make_async_remote_copy(
                src_ref=src[a] if from_src else dst[a].at[slot], dst_ref=dst[a].at[slot],
                send_sem=send_sems.at[a * 7 + k], recv_sem=recv_sems.at[a * 7 + k],
                device_id=to, device_id_type=pl.DeviceIdType.MESH)

        def slot_of(r, core):
            px, py, _ = _peer(r)
            return 4 * px + 2 * py + core

        local = [pltpu.make_async_copy(src[a], dst[a].at[me], local_sems.at[a]) for a in range(n)]
        sent = []
        for a in range(n):
            local[a].start()
            sent.append(copy(a, 0, me, sibling, from_src=True))
            sent += [copy(a, 1 + j, me, _peer(r), from_src=True) for j, r in enumerate(chips)]
        for cp in sent:
            cp.start()
        for j, r in enumerate(chips):
            for a in range(n):
                copy(a, 1 + j, slot_of(r, c), _peer(r)).wait_recv()
                cp = copy(a, 4 + j, slot_of(r, c), sibling)
                cp.start()
                sent.append(cp)
        for a in range(n):
            copy(a, 0, slot_of(0, 1 - c), sibling).wait_recv()
            for j, r in enumerate(chips):
                copy(a, 4 + j, slot_of(r, 1 - c), sibling).wait_recv()
        for cp in local:
            cp.wait()
        for cp in sent:
            cp.wait_send()

    any_spec = pl.BlockSpec(memory_space=pl.ANY)
    return pl.pallas_call(
        body, name=name, out_shape=out_shape, in_specs=[any_spec] * n, out_specs=[any_spec] * n,
        scratch_shapes=[pltpu.SemaphoreType.DMA((n * 7,)), pltpu.SemaphoreType.DMA((n * 7,)), pltpu.SemaphoreType.DMA((n,))],
    )(*srcs)


def _to_wire(mats, name):
    def body(*refs):
        for src, dst in zip(refs[:len(mats)], refs[len(mats):]):
            dst[...] = src[...].astype(_WIRE)

    return pl.pallas_call(
        body, name=name, grid=(1,), out_shape=[jax.ShapeDtypeStruct(m.shape, _WIRE) for m in mats],
        in_specs=[_full(m.shape) for m in mats], out_specs=[_full(m.shape) for m in mats], compiler_params=_params(),
    )(*mats)


def _sum_slots(r, name):
    _, rows, cols = r.shape
    blk = rows
    for cand in (rows, 592, 512, 256, 128, 64, 32, 16):
        if rows % cand == 0 and N_DEV * cand * cols * r.dtype.itemsize <= 8 * 1024 * 1024:
            blk = cand
            break

    def body(r_ref, o_ref):
        acc = r_ref[0].astype(F32)
        for d in range(1, N_DEV):
            acc = acc + r_ref[d].astype(F32)
        o_ref[...] = acc

    return pl.pallas_call(
        body, name=name, grid=(rows // blk,), out_shape=jax.ShapeDtypeStruct((rows, cols), F32),
        in_specs=[pl.BlockSpec((N_DEV, blk, cols), lambda i: (0, i, 0))],
        out_specs=pl.BlockSpec((blk, cols), lambda i: (i, 0)), compiler_params=_params(),
    )(r)


def _token_tile_copy(tokens_ref, buf, sems, i, tm, write=False):
    if isinstance(i, int) and i == 0:
        far, near = tokens_ref.at[pl.ds(0, tm - N_META)], buf.at[0, pl.ds(N_META, tm - N_META)]
    else:
        start = i * tm - N_META if isinstance(i, int) else pl.multiple_of(i * tm - N_META, SUBLANES)
        far, near = tokens_ref.at[pl.ds(start, tm)], buf.at[i % 2]
    return pltpu.make_async_copy(near, far, sems.at[i % 2]) if write else pltpu.make_async_copy(far, near, sems.at[i % 2])


def _fetch_token_tile(tokens_ref, buf, sems, i, n_t, tm):
    @pl.when(i == 0)
    def _():
        _token_tile_copy(tokens_ref, buf, sems, 0, tm).start()

    @pl.when(i + 1 < n_t)
    def _():
        _token_tile_copy(tokens_ref, buf, sems, i + 1, tm).start()

    @pl.when(i == 0)
    def _():
        _token_tile_copy(tokens_ref, buf, sems, 0, tm).wait()

    @pl.when(i > 0)
    def _():
        _token_tile_copy(tokens_ref, buf, sems, i, tm).wait()


def _ffn_fwd(h, g_pre, g_post, wgt, wut, wd, name, xchg=None, *, meta=None, tokens=None, target=None):
    first = h is None
    d = wd.shape[1]
    n_rows = N_META + tokens.shape[0] if first else h.shape[0]
    d_ff = wd.shape[0]
    tm, fc = _row_tile(n_rows), _ff_chunk(d_ff)
    n_t, n_c = n_rows // tm, d_ff // fc

    def body(src_ref, side_ref, gpre_ref, gpost_ref, wgt_ref, wut_ref, wd_ref, a_ref, b_ref, f_ref, o1_ref, o2_ref,
             n_scr, acc, buf, sems):
        i, c = pl.program_id(0), pl.program_id(1)
        slot = i % 2

        @pl.when(c == 0)
        def _():
            if first:
                _fetch_token_tile(side_ref, buf, sems, i, n_t, tm)

                @pl.when(i == 0)
                def _():
                    buf[0, 0:N_META, :] = src_ref[...]

                hv = buf[slot]
                o2_ref[...] = hv
            else:
                _fetch_token_tile(side_ref, buf, sems, i, n_t, tm)

                @pl.when(i == 0)
                def _():
                    buf[0, 0:N_META, :] = jnp.zeros((N_META, d), F32)
                    o2_ref[...] = jnp.zeros_like(o2_ref)

                hv = src_ref[...]
            n_scr[...] = (hv * _rs(hv) * gpre_ref[...]).astype(_MXU)
            acc[...] = jnp.zeros_like(acc)

        rows = pl.ds(pl.multiple_of(c * fc, fc), fc)
        nv = n_scr[...]
        a = _dot_nt(nv, wgt_ref[rows, :])
        b = _dot_nt(nv, wut_ref[rows, :])
        a_ref[...] = a.astype(_ACT)
        b_ref[...] = b.astype(_ACT)
        s = a * _sigmoid(a) * b
        acc[...] += _dot(s.astype(_MXU), wd_ref[rows, :])

        @pl.when(c == n_c - 1)
        def _():
            f = acc[...]
            f_ref[...] = f
            step = 0.5 * (f * _rs(f) * gpost_ref[...])
            if first:
                o1_ref[...] = buf[slot] + step
            else:
                row = i * tm + lax.broadcasted_iota(jnp.int32, (tm, 1), 0)
                err = jnp.where(row >= N_META, (src_ref[...] + step) - buf[slot], 0.0)
                o1_ref[...] = err / d
                o2_ref[...] += jnp.sum(jnp.sum(err * err, axis=0, keepdims=True), axis=1, keepdims=True)

    row = pl.BlockSpec((tm, d), lambda i, c: (i, 0))
    chunk = pl.BlockSpec((tm, fc), lambda i, c: (i, c))
    hbm = pl.BlockSpec(memory_space=pl.ANY)
    if first:
        operands, specs = (meta, tokens), [_full(meta.shape), hbm]
        last_shape, last_spec = jax.ShapeDtypeStruct((n_rows, d), F32), row
    else:
        operands, specs = (h, target), [row, hbm]
        last_shape, last_spec = jax.ShapeDtypeStruct((1, LANES), F32), pl.BlockSpec((1, LANES), lambda i, c: (0, 0))
    return _call(
        body, (*operands, g_pre, g_post, wgt, wut, wd), name=name, grid=(n_t, n_c),
        out_shape=[jax.ShapeDtypeStruct((n_rows, d_ff), _ACT), jax.ShapeDtypeStruct((n_rows, d_ff), _ACT),
                   jax.ShapeDtypeStruct((n_rows, d), F32), jax.ShapeDtypeStruct((n_rows, d), F32), last_shape],
        in_specs=specs + [_full((1, d)), _full((1, d)), _resident(), _resident(), _resident()],
        out_specs=[chunk, chunk, row, row, last_spec],
        scratch_shapes=[pltpu.VMEM((tm, d), _MXU), pltpu.VMEM((tm, d), F32), pltpu.VMEM((2, tm, d), F32),
                        pltpu.SemaphoreType.DMA((2,))], xchg=xchg)


def _ffn_bwd(dho, h, f, a, b, g_pre, g_post, wgt, wut, wd, name, xchg=None, *, split_meta=False, after=()):
    n_rows, d = h.shape
    d_ff = wd.shape[0]
    tm, fc = _row_tile(n_rows), _ff_chunk(d_ff)
    n_t, n_c = n_rows // tm, d_ff // fc
    assert n_t >= 2

    def body(dho_ref, h_ref, f_ref, a_ref, b_ref, gpre_ref, gpost_ref, wgt_ref, wut_ref, wd_ref,
             dh_ref, da_ref, db_ref, s_ref, df_ref, n_ref, dgpre_ref, dgpost_ref, *rest):
        dn_acc = rest[-1]
        i, c = pl.program_id(0), pl.program_id(1)

        @pl.when((i == 0) & (c == 0))
        def _():
            dgpre_ref[...] = jnp.zeros_like(dgpre_ref)
            dgpost_ref[...] = jnp.zeros_like(dgpost_ref)

        @pl.when(c == 0)
        def _():
            fv = f_ref[...]
            rf = _rs(fv)
            fhat = fv * rf
            dy = 0.5 * dho_ref[...]
            dgpost_ref[...] += jnp.sum(dy * fhat, axis=0, keepdims=True)
            dfhat = dy * gpost_ref[...]
            df = rf * (dfhat - fhat * jnp.mean(dfhat * fhat, axis=-1, keepdims=True))
            df_ref[...] = df.astype(_MXU)
            hv = h_ref[...]
            n_ref[...] = (hv * _rs(hv) * gpre_ref[...]).astype(_MXU)
            dn_acc[...] = jnp.zeros_like(dn_acc)

        rows = pl.ds(pl.multiple_of(c * fc, fc), fc)
        ds = _dot_nt(df_ref[...], wd_ref[rows, :])
        av = a_ref[...].astype(F32)
        bv = b_ref[...].astype(F32)
        sg = _sigmoid(av)
        si = av * sg
        da = (ds * bv * (sg * (1.0 + av * (1.0 - sg)))).astype(_MXU)
        db = (ds * si).astype(_MXU)
        da_ref[...] = da
        db_ref[...] = db
        s_ref[...] = (si * bv).astype(_MXU)
        dn_acc[...] += _dot(da, wgt_ref[rows, :]) + _dot(db, wut_ref[rows, :])

        @pl.when(c == n_c - 1)
        def _():
            dn = dn_acc[...]
            hv = h_ref[...]
            r = _rs(hv)
            hhat = hv * r
            dgpre_ref[...] += jnp.sum(dn * hhat, axis=0, keepdims=True)
            dhh = dn * gpre_ref[...]
            dh = dho_ref[...] + r * (dhh - hhat * jnp.mean(dhh * hhat, axis=-1, keepdims=True))
            if not split_meta:
                dh_ref[...] = dh
                return
            dmeta_ref, buf, sems = rest[0], rest[1], rest[2]

            def out_copy(k):
                return _token_tile_copy(dh_ref, buf, sems, k, tm, write=True)

            @pl.when(i == 2)
            def _():
                out_copy(0).wait()

            @pl.when(i > 2)
            def _():
                out_copy(i - 2).wait()

            buf[i % 2] = dh

            @pl.when(i == 0)
            def _():
                dmeta_ref[...] = buf[0, 0:N_META, :]
                out_copy(0).start()

            @pl.when(i > 0)
            def _():
                out_copy(i).start()

            @pl.when(i == n_t - 1)
            def _():
                out_copy(n_t - 2).wait()
                out_copy(n_t - 1).wait()

    row = pl.BlockSpec((tm, d), lambda i, c: (i, 0))
    chunk = pl.BlockSpec((tm, fc), lambda i, c: (i, c))
    vec = pl.BlockSpec((1, d), lambda i, c: (0, 0))
    shapes = [jax.ShapeDtypeStruct((n_rows, d_ff), _MXU)] * 3 + [jax.ShapeDtypeStruct((n_rows, d), _MXU)] * 2 \
        + [jax.ShapeDtypeStruct((1, d), F32)] * 2
    specs = [chunk, chunk, chunk, row, row, vec, vec]
    scratch = [pltpu.VMEM((tm, d), F32)]
    if split_meta:
        shapes = [jax.ShapeDtypeStruct((n_rows - N_META, d), F32)] + shapes + [jax.ShapeDtypeStruct((N_META, d), F32)]
        specs = [pl.BlockSpec(memory_space=pl.ANY)] + specs + [pl.BlockSpec((N_META, d), lambda i, c: (0, 0))]
        scratch = [pltpu.VMEM((2, tm, d), F32), pltpu.SemaphoreType.DMA((2,))] + scratch
    else:
        shapes, specs = [jax.ShapeDtypeStruct((n_rows, d), F32)] + shapes, [row] + specs
    return _call(
        body, (dho, h, f, a, b, g_pre, g_post, wgt, wut, wd), name=name, grid=(n_t, n_c), out_shape=shapes,
        in_specs=[row, row, row, chunk, chunk, vec, vec, _resident(), _resident(), _resident()], out_specs=specs,
        scratch_shapes=scratch, xchg=xchg, after=after)


def _wgrad(xm, ym, name, xchg=None, after=()):
    n_rows, a_dim = xm.shape
    b_dim = ym.shape[1]
    tk = n_rows
    for cand in (2736, 1296, 432, 48, 16):
        if n_rows % cand == 0:
            tk = cand
            break
    ta = a_dim
    for cand in (1408, 1024, 512):
        if a_dim % cand == 0:
            ta = cand
            break

    n_k = n_rows // tk

    def body(x_ref, y_ref, o_ref, acc):
        k = pl.program_id(1)

        @pl.when(k == 0)
        def _():
            acc[...] = jnp.zeros_like(acc)

        acc[...] += _dot_tn(x_ref[...], y_ref[...])

        @pl.when(k == n_k - 1)
        def _():
            o_ref[...] = acc[...].astype(o_ref.dtype)

    (out,), extra = _call(
        body, (xm, ym), name=name, grid=(a_dim // ta, n_k), out_shape=[jax.ShapeDtypeStruct((a_dim, b_dim), _WIRE)],
        in_specs=[pl.BlockSpec((tk, ta), lambda j, k: (k, j)), pl.BlockSpec((tk, b_dim), lambda j, k: (k, 0))],
        out_specs=[pl.BlockSpec((ta, b_dim), lambda j, k: (j, 0))], scratch_shapes=[pltpu.VMEM((ta, b_dim), F32)],
        xchg=xchg, after=after)
    return out, extra


def _mix_in_fwd(h, g, w_in, name):
    n_rows, d = h.shape
    tm = _row_tile(n_rows)
    width = w_in.shape[1] // 2
    n_blk = width // LANES

    def body(h_ref, g_ref, w_ref, n_ref, us_ref, up_ref):
        hv = h_ref[...]
        nv = (hv * _rs(hv) * g_ref[...]).astype(_MXU)
        n_ref[...] = nv
        p = _dot(nv, w_ref[...])
        for k in range(n_blk):
            us_ref[k] = p[:, k * LANES:(k + 1) * LANES]
        up_ref[...] = p[:, width:]

    row = pl.BlockSpec((tm, d), lambda i: (i, 0))
    half = pl.BlockSpec((tm, width), lambda i: (i, 0))
    return pl.pallas_call(
        body, name=name, grid=(n_rows // tm,),
        out_shape=[jax.ShapeDtypeStruct((n_rows, d), _MXU), jax.ShapeDtypeStruct((n_blk, n_rows, LANES), F32),
                   jax.ShapeDtypeStruct((n_rows, width), F32)],
        in_specs=[row, _full((1, d)), _resident()], out_specs=[row, _lane_blocks(n_blk, tm), half],
        compiler_params=_params(),
    )(h, g, w_in)


def _gelu_parts(y):
    th = jnp.tanh(GELU_C0 * (y + GELU_C1 * (y * y * y)))
    return 0.5 * (1.0 + th), th


SCAN_LANES = 1024


def _regroup(src_ref, dst_ref, seg):
    for k in range(src_ref.shape[0]):
        for j in range(seg):
            dst_ref[j * SUBLANES:(j + 1) * SUBLANES, k * LANES:(k + 1) * LANES] = src_ref[k, pl.ds(j, SUBLANES, stride=seg), :]


def _ungroup(src_ref, dst_ref, seg):
    for k in range(dst_ref.shape[0]):
        for j in range(seg):
            dst_ref[k, pl.ds(j, SUBLANES, stride=seg), :] = src_ref[j * SUBLANES:(j + 1) * SUBLANES, k * LANES:(k + 1) * LANES]


def _rows_to_sublanes(rows):
    rid = lax.broadcasted_iota(jnp.int32, (SUBLANES, rows[0].shape[1]), 0)
    out = jnp.broadcast_to(rows[0], rid.shape)
    for s in range(1, SUBLANES):
        out = jnp.where(rid == s, rows[s], out)
    return out


def _fill_powers(a_ref, tab_ref, seg, reverse):
    for cb in range(tab_ref.shape[3] // SCAN_LANES):
        cols = pl.ds(cb * SCAN_LANES, SCAN_LANES)
        ar = jnp.broadcast_to(a_ref[0, :, cols], (SUBLANES, SCAN_LANES))
        ai = jnp.broadcast_to(a_ref[1, :, cols], (SUBLANES, SCAN_LANES))
        if reverse:
            ai = -ai

        def step(t, carry, cols=cols, ar=ar, ai=ai):
            pr, pi = carry
            j = seg - 1 - t if reverse else t
            tab_ref[0, j, :, cols] = pr
            tab_ref[1, j, :, cols] = pi
            return pr * ar - pi * ai, pr * ai + pi * ar

        lax.fori_loop(0, seg, step, (ar, ai))


def _segment_scan(re_ref, im_ref, tab_ref, car_re, car_im, seg, reverse, x_refs=None):
    n_all = re_ref.shape[1]
    first = seg - 1 if reverse else 0
    sums = []
    for cb in range(n_all // SCAN_LANES):
        cols = pl.ds(cb * SCAN_LANES, SCAN_LANES)
        ar, ai = tab_ref[0, first, :, cols], tab_ref[1, first, :, cols]

        def local(t, carry, cols=cols, ar=ar, ai=ai):
            r0 = pl.multiple_of((seg - 1 - t if reverse else t) * SUBLANES, SUBLANES)
            xr, xi = carry
            nr = ar * xr - ai * xi + re_ref[pl.ds(r0, SUBLANES), cols]
            ni = ar * xi + ai * xr + im_ref[pl.ds(r0, SUBLANES), cols]
            re_ref[pl.ds(r0, SUBLANES), cols] = nr
            im_ref[pl.ds(r0, SUBLANES), cols] = ni
            return nr, ni

        zero = jnp.zeros((SUBLANES, SCAN_LANES), F32)
        fr, fi = lax.fori_loop(0, seg, local, (zero, zero))

        last = 0 if reverse else seg - 1
        sr, si = tab_ref[0, last, 0:1, cols], tab_ref[1, last, 0:1, cols]
        cr, ci = car_re[:, cols], car_im[:, cols]
        rows_r, rows_i = [None] * SUBLANES, [None] * SUBLANES
        for s in (range(SUBLANES - 1, -1, -1) if reverse else range(SUBLANES)):
            rows_r[s], rows_i[s] = cr, ci
            cr, ci = sr * cr - si * ci + fr[s:s + 1], sr * ci + si * cr + fi[s:s + 1]
        car_re[:, cols] = cr
        car_im[:, cols] = ci
        cmr, cmi = _rows_to_sublanes(rows_r), _rows_to_sublanes(rows_i)

        def fix(t, carry, cols=cols, cmr=cmr, cmi=cmi):
            j = seg - 1 - t if reverse else t
            r0 = pl.multiple_of(j * SUBLANES, SUBLANES)
            pr, pi = tab_ref[0, j, :, cols], tab_ref[1, j, :, cols]
            gr = re_ref[pl.ds(r0, SUBLANES), cols] + (pr * cmr - pi * cmi)
            gi = im_ref[pl.ds(r0, SUBLANES), cols] + (pr * cmi + pi * cmr)
            re_ref[pl.ds(r0, SUBLANES), cols] = gr
            im_ref[pl.ds(r0, SUBLANES), cols] = gi
            if x_refs is None:
                return carry
            nxr, nxi, accr, acci = carry
            xr, xi = x_refs[0][pl.ds(r0, SUBLANES), cols], x_refs[1][pl.ds(r0, SUBLANES), cols]
            return gr, gi, accr + (xr * nxr + xi * nxi), acci + (xr * nxi - xi * nxr)

        if x_refs is None:
            lax.fori_loop(0, seg, fix, 0)
        else:
            fin = lax.fori_loop(0, seg, fix, (cmr, cmi, zero, zero))
            sums.append((jnp.sum(fin[2], axis=0, keepdims=True), jnp.sum(fin[3], axis=0, keepdims=True)))
    return sums


def _ssm_forward(u, a_pair, b_re, b_im, c_re, c_im, wz1, wz2, d_skip, g_out, name, xchg=None):
    n_rows = u.shape[1]
    n_blk, _, n_state = b_re.shape
    width, n_all = n_blk * LANES, n_blk * n_state
    tm = _row_tile(n_rows)
    seg = tm // SUBLANES

    def body(u_ref, a_ref, bre_ref, bim_ref, cre_ref, cim_ref, wz1_ref, wz2_ref, d_ref, g_ref,
             xre_ref, xim_ref, o_ref, car_re, car_im, ug, out_scr, blocks, tab_ref):
        @pl.when(pl.program_id(0) == 0)
        def _():
            car_re[...] = jnp.zeros_like(car_re)
            car_im[...] = jnp.zeros_like(car_im)
            _fill_powers(a_ref, tab_ref, seg, reverse=False)

        _regroup(u_ref, ug, seg)
        ub = ug[...].astype(_MXU)
        for j in range(n_blk):
            uj = ub[:, j * LANES:(j + 1) * LANES]
            xre_ref[:, j * n_state:(j + 1) * n_state] = _dot(uj, bre_ref[j])
            xim_ref[:, j * n_state:(j + 1) * n_state] = _dot(uj, bim_ref[j])
        _segment_scan(xre_ref, xim_ref, tab_ref, car_re, car_im, seg, reverse=False)

        ssq = None
        for j in range(n_blk):
            sl = slice(j * LANES, (j + 1) * LANES)
            st = slice(j * n_state, (j + 1) * n_state)
            yc = _dot(xre_ref[:, st].astype(_MXU), cre_ref[j]) - _dot(xim_ref[:, st].astype(_MXU), cim_ref[j])
            y = yc + d_ref[:, sl] * ug[:, sl]
            cdf, _ = _gelu_parts(y)
            gy = (y * cdf).astype(_MXU)
            out = _dot(gy, wz1_ref[j]) * _sigmoid(_dot(gy, wz2_ref[j]))
            out_scr[:, sl] = out
            part = jnp.sum(out * out, axis=-1, keepdims=True)
            ssq = part if ssq is None else ssq + part
        r = lax.rsqrt(ssq / width + RMS_EPS)
        out_scr[...] = out_scr[...] * r * g_ref[...]
        _ungroup(out_scr, blocks, seg)
        for k in range(n_blk):
            o_ref[:, k * LANES:(k + 1) * LANES] = blocks[k].astype(_MXU)

    half = pl.BlockSpec((tm, width), lambda i: (i, 0))
    state = pl.BlockSpec((tm, n_all), lambda i: (i, 0))
    return _call(
        body, (u, a_pair, b_re, b_im, c_re, c_im, wz1, wz2, d_skip, g_out), name=name, grid=(n_rows // tm,),
        out_shape=[jax.ShapeDtypeStruct((n_rows, n_all), F32)] * 2 + [jax.ShapeDtypeStruct((n_rows, width), _MXU)],
        in_specs=[_lane_blocks(n_blk, tm), _full(a_pair.shape), _full(b_re.shape), _full(b_im.shape), _full(c_re.shape),
                  _full(c_im.shape), _full(wz1.shape), _full(wz2.shape), _full((1, width)), _full((1, width))],
        out_specs=[state, state, half],
        scratch_shapes=[pltpu.VMEM((1, n_all), F32)] * 2 + [pltpu.VMEM((tm, width), F32)] * 2
        + [pltpu.VMEM((n_blk, tm, LANES), F32), pltpu.VMEM((2, seg, SUBLANES, n_all), F32)], xchg=xchg)


def _ssm_backward(u, x_re, x_im, dcat, a_pair, b_re, b_im, c_re, c_im, wz1, wz2, d_skip, g_out, name, xchg=None):
    n_rows = u.shape[1]
    n_blk, _, n_state = b_re.shape
    width, n_all = n_blk * LANES, n_blk * n_state
    tm = _row_tile(n_rows)
    n_t, seg = n_rows // tm, tm // SUBLANES

    def body(u_ref, xre_ref, xim_ref, dc_ref, a_ref, bre_ref, bim_ref, cre_ref, cim_ref, wz1_ref, wz2_ref, d_ref, g_ref,
             du_ref, dg_ref, dd_ref, dwz1_ref, dwz2_ref, dcre_ref, dcim_ref, dbre_ref, dbim_ref, are_ref, aim_ref,
             car_re, car_im, gre, gim, ug, y_s, z1_s, sg_s, out_s, gy_s, tab_ref):
        @pl.when(pl.program_id(0) == 0)
        def _():
            for ref in (dg_ref, dd_ref, dwz1_ref, dwz2_ref, dcre_ref, dcim_ref, dbre_ref, dbim_ref, are_ref, aim_ref,
                        car_re, car_im):
                ref[...] = jnp.zeros_like(ref)
            _fill_powers(a_ref, tab_ref, seg, reverse=True)

        _regroup(u_ref, ug, seg)
        ssq = None
        for j in range(n_blk):
            sl = slice(j * LANES, (j + 1) * LANES)
            st = slice(j * n_state, (j + 1) * n_state)
            yc = _dot(xre_ref[:, st].astype(_MXU), cre_ref[j]) - _dot(xim_ref[:, st].astype(_MXU), cim_ref[j])
            y = yc + d_ref[:, sl] * ug[:, sl]
            cdf, _ = _gelu_parts(y)
            gy = (y * cdf).astype(_MXU)
            z1 = _dot(gy, wz1_ref[j])
            sg = _sigmoid(_dot(gy, wz2_ref[j]))
            out = z1 * sg
            y_s[:, sl], z1_s[:, sl], sg_s[:, sl], out_s[:, sl], gy_s[:, sl] = y, z1, sg, out, gy
            part = jnp.sum(out * out, axis=-1, keepdims=True)
            ssq = part if ssq is None else ssq + part
        r = lax.rsqrt(ssq / width + RMS_EPS)
        ohat = out_s[...] * r
        _regroup(dc_ref, out_s, seg)
        dcv = out_s[...]
        dg_ref[...] += jnp.sum(dcv * ohat, axis=0, keepdims=True)
        doh = dcv * g_ref[...]
        out_s[...] = r * (doh - ohat * (jnp.sum(doh * ohat, axis=-1, keepdims=True) / width))

        for j in range(n_blk):
            sl = slice(j * LANES, (j + 1) * LANES)
            st = slice(j * n_state, (j + 1) * n_state)
            dout, sg, z1, y = out_s[:, sl], sg_s[:, sl], z1_s[:, sl], y_s[:, sl]
            dz1 = (dout * sg).astype(_MXU)
            dz2 = (dout * z1 * sg * (1.0 - sg)).astype(_MXU)
            gy = gy_s[:, sl]
            dwz1_ref[j] += _dot_tn(gy, dz1)
            dwz2_ref[j] += _dot_tn(gy, dz2)
            dgy = _dot_nt(dz1, wz1_ref[j]) + _dot_nt(dz2, wz2_ref[j])
            cdf, th = _gelu_parts(y)
            dy = dgy * (cdf + y * (0.5 * (1.0 - th * th) * GELU_C0 * (1.0 + 3.0 * GELU_C1 * (y * y))))
            uj = ug[:, sl]
            dd_ref[:, sl] += jnp.sum(dy * uj, axis=0, keepdims=True)
            z1_s[:, sl] = d_ref[:, sl] * dy
            dyb = dy.astype(_MXU)
            dcre_ref[j] += _dot_tn(dyb, xre_ref[:, st].astype(_MXU))
            dcim_ref[j] -= _dot_tn(dyb, xim_ref[:, st].astype(_MXU))
            gre[:, st] = _dot_nt(dyb, cre_ref[j])
            gim[:, st] = -_dot_nt(dyb, cim_ref[j])

        sums = _segment_scan(gre, gim, tab_ref, car_re, car_im, seg, reverse=True, x_refs=(xre_ref, xim_ref))
        for cb, (sum_re, sum_im) in enumerate(sums):
            cols = pl.ds(cb * SCAN_LANES, SCAN_LANES)
            are_ref[:, cols] += sum_re
            aim_ref[:, cols] += sum_im

        for j in range(n_blk):
            sl = slice(j * LANES, (j + 1) * LANES)
            st = slice(j * n_state, (j + 1) * n_state)
            ujb = ug[:, sl].astype(_MXU)
            grb, gib = gre[:, st].astype(_MXU), gim[:, st].astype(_MXU)
            dbre_ref[j] += _dot_tn(ujb, grb)
            dbim_ref[j] += _dot_tn(ujb, gib)
            z1_s[:, sl] += _dot_nt(grb, bre_ref[j]) + _dot_nt(gib, bim_ref[j])
        _ungroup(z1_s, du_ref, seg)

    half = _lane_blocks(n_blk, tm, lambda i: n_t - 1 - i)
    state = pl.BlockSpec((tm, n_all), lambda i: (n_t - 1 - i, 0))
    small = [(1, width), (1, width), wz1.shape, wz2.shape, (n_blk, LANES, n_state), (n_blk, LANES, n_state),
             (n_blk, LANES, n_state), (n_blk, LANES, n_state), (1, n_all), (1, n_all)]
    return _call(
        body, (u, x_re, x_im, dcat, a_pair, b_re, b_im, c_re, c_im, wz1, wz2, d_skip, g_out), name=name, grid=(n_t,),
        out_shape=[jax.ShapeDtypeStruct((n_blk, n_rows, LANES), F32)] + [jax.ShapeDtypeStruct(s, F32) for s in small],
        in_specs=[half, state, state, half, _full(a_pair.shape), _full(b_re.shape), _full(b_im.shape), _full(c_re.shape),
                  _full(c_im.shape), _full(wz1.shape), _full(wz2.shape), _full((1, width)), _full((1, width))],
        out_specs=[half] + [_full(s) for s in small],
        scratch_shapes=[pltpu.VMEM((1, n_all), F32)] * 2 + [pltpu.VMEM((tm, n_all), F32)] * 2
        + [pltpu.VMEM((tm, width), F32)] * 5 + [pltpu.VMEM((tm, width), _MXU),
                                                pltpu.VMEM((2, seg, SUBLANES, n_all), F32)], xchg=xchg)


def _pool_counts(tile, tm, window):
    t = tile * tm + lax.broadcasted_iota(jnp.int32, (tm, 1), 0)
    return jnp.minimum(t + 1, window).astype(F32)


def _pool_fwd(proj, pool_w, scale, g_out, name):
    n_rows = proj.shape[0]
    n_grp, grp, _ = pool_w.shape
    width = n_grp * grp
    tm = _row_tile(n_rows)

    def body(u_ref, pw_ref, sc_ref, g_ref, o_ref, ext, y_s):
        i = pl.program_id(0)

        @pl.when(i == 0)
        def _():
            ext[0:POOL_HALO, :] = jnp.zeros((POOL_HALO, width), F32)

        ext[POOL_HALO:, :] = u_ref[...]
        ssq = None
        for gi, w in enumerate(POOL_WINDOWS):
            sl = slice(gi * grp, (gi + 1) * grp)
            tot = ext[POOL_HALO:, sl]
            for k in range(1, w):
                tot = tot + ext[POOL_HALO - k:POOL_HALO - k + tm, sl]
            pooled = tot / _pool_counts(i, tm, w) - u_ref[:, sl]
            y = _dot(pooled.astype(_MXU), pw_ref[gi]) * sc_ref[:, sl]
            y_s[:, sl] = y
            part = jnp.sum(y * y, axis=-1, keepdims=True)
            ssq = part if ssq is None else ssq + part
        r = lax.rsqrt(ssq / width + RMS_EPS)
        o_ref[...] = (y_s[...] * r * g_ref[...]).astype(_MXU)
        ext[0:POOL_HALO, :] = u_ref[tm - POOL_HALO:, :]

    half_in = pl.BlockSpec((tm, width), lambda i: (i, 0))
    half = pl.BlockSpec((tm, width), lambda i: (i, 0))
    return pl.pallas_call(
        body, name=name, grid=(n_rows // tm,), out_shape=jax.ShapeDtypeStruct((n_rows, width), _MXU),
        in_specs=[half_in, _full(pool_w.shape), _full((1, width)), _full((1, width))], out_specs=half,
        scratch_shapes=[pltpu.VMEM((tm + POOL_HALO, width), F32), pltpu.VMEM((tm, width), F32)],
        compiler_params=_params(),
    )(proj, pool_w, scale, g_out)


def _pool_bwd(proj, dcat, pool_w, scale, g_out, name):
    n_rows = proj.shape[0]
    n_grp, grp, _ = pool_w.shape
    width = n_grp * grp
    tm = _row_tile(n_rows)
    n_t = n_rows // tm
    halo_blocks = tm // POOL_HALO

    def body(u_ref, up_ref, dc_ref, pw_ref, sc_ref, g_ref, du_ref, dg_ref, dsc_ref, dpw_ref, ext, qext, y_s, pl_s):
        i = pl.program_id(0)
        tile = n_t - 1 - i

        @pl.when(i == 0)
        def _():
            for ref in (dg_ref, dsc_ref, dpw_ref):
                ref[...] = jnp.zeros_like(ref)
            qext[tm:, :] = jnp.zeros((POOL_HALO, width), F32)

        ext[0:POOL_HALO, :] = jnp.where(tile > 0, up_ref[...], 0.0)
        ext[POOL_HALO:, :] = u_ref[...]
        ssq = None
        for gi, w in enumerate(POOL_WINDOWS):
            sl = slice(gi * grp, (gi + 1) * grp)
            tot = ext[POOL_HALO:, sl]
            for k in range(1, w):
                tot = tot + ext[POOL_HALO - k:POOL_HALO - k + tm, sl]
            pooled = (tot / _pool_counts(tile, tm, w) - u_ref[:, sl]).astype(_MXU)
            pl_s[:, sl] = pooled
            y0 = _dot(pooled, pw_ref[gi])
            y_s[:, sl] = y0
            y = y0 * sc_ref[:, sl]
            part = jnp.sum(y * y, axis=-1, keepdims=True)
            ssq = part if ssq is None else ssq + part
        r = lax.rsqrt(ssq / width + RMS_EPS)
        y0 = y_s[...]
        yhat = y0 * sc_ref[...] * r
        dcv = dc_ref[...]
        dg_ref[...] += jnp.sum(dcv * yhat, axis=0, keepdims=True)
        dyh = dcv * g_ref[...]
        dy = r * (dyh - yhat * (jnp.sum(dyh * yhat, axis=-1, keepdims=True) / width))
        dsc_ref[...] += jnp.sum(dy * y0, axis=0, keepdims=True)
        y_s[...] = dy * sc_ref[...]
        for gi, w in enumerate(POOL_WINDOWS):
            sl = slice(gi * grp, (gi + 1) * grp)
            dm = y_s[:, sl].astype(_MXU)
            dpw_ref[gi] += _dot_tn(pl_s[:, sl], dm)
            dpooled = _dot_nt(dm, pw_ref[gi])
            y_s[:, sl] = dpooled
            qext[0:tm, sl] = dpooled / _pool_counts(tile, tm, w)
        for gi, w in enumerate(POOL_WINDOWS):
            sl = slice(gi * grp, (gi + 1) * grp)
            tot = qext[0:tm, sl]
            for k in range(1, w):
                tot = tot + qext[k:k + tm, sl]
            du_ref[:, sl] = tot - y_s[:, sl]
        qext[tm:, :] = qext[0:POOL_HALO, :]

    half_in = pl.BlockSpec((tm, width), lambda i: (n_t - 1 - i, 0))
    prev = pl.BlockSpec((POOL_HALO, width), lambda i: (jnp.maximum((n_t - 1 - i) * halo_blocks - 1, 0), 0))
    half = pl.BlockSpec((tm, width), lambda i: (n_t - 1 - i, 0))
    return pl.pallas_call(
        body, name=name, grid=(n_t,),
        out_shape=[jax.ShapeDtypeStruct((n_rows, width), F32), jax.ShapeDtypeStruct((1, width), F32),
                   jax.ShapeDtypeStruct((1, width), F32), jax.ShapeDtypeStruct(pool_w.shape, F32)],
        in_specs=[half_in, prev, half, _full(pool_w.shape), _full((1, width)), _full((1, width))],
        out_specs=[half, _full((1, width)), _full((1, width)), _full(pool_w.shape)],
        scratch_shapes=[pltpu.VMEM((tm + POOL_HALO, width), F32), pltpu.VMEM((tm + POOL_HALO, width), F32),
                        pltpu.VMEM((tm, width), F32), pltpu.VMEM((tm, width), _MXU)],
        compiler_params=_params(),
    )(proj, proj, dcat, pool_w, scale, g_out)


def _mix_out_fwd(cat_s, cat_p, h, g, wo_s, wo_p, name):
    n_rows, d = h.shape
    width = cat_s.shape[1]
    tm = _row_tile(n_rows)

    def body(cs_ref, cp_ref, h_ref, g_ref, ws_ref, wp_ref, m_ref, ho_ref):
        m = _dot(cs_ref[...], ws_ref[...]) + _dot(cp_ref[...], wp_ref[...])
        m_ref[...] = m
        ho_ref[...] = h_ref[...] + m * _rs(m) * g_ref[...]

    row = pl.BlockSpec((tm, d), lambda i: (i, 0))
    half = pl.BlockSpec((tm, width), lambda i: (i, 0))
    return pl.pallas_call(
        body, name=name, grid=(n_rows // tm,), out_shape=[jax.ShapeDtypeStruct((n_rows, d), F32)] * 2,
        in_specs=[half, half, row, _full((1, d)), _resident(), _resident()], out_specs=[row, row],
        compiler_params=_params(),
    )(cat_s, cat_p, h, g, wo_s, wo_p)


def _mix_out_bwd(dho, mixed, g, wo_s, wo_p, name):
    n_rows, d = mixed.shape
    width = wo_s.shape[0]
    n_blk = width // LANES
    tm = _row_tile(n_rows)

    def body(dho_ref, m_ref, g_ref, ws_ref, wp_ref, dm_ref, dcs_ref, dcp_ref, dg_ref):
        @pl.when(pl.program_id(0) == 0)
        def _():
            dg_ref[...] = jnp.zeros_like(dg_ref)

        m = m_ref[...]
        r = _rs(m)
        mh = m * r
        dy = dho_ref[...]
        dg_ref[...] += jnp.sum(dy * mh, axis=0, keepdims=True)
        dmh = dy * g_ref[...]
        dm = (r * (dmh - mh * jnp.mean(dmh * mh, axis=-1, keepdims=True))).astype(_MXU)
        dm_ref[...] = dm
        dcs = _dot_nt(dm, ws_ref[...])
        for k in range(n_blk):
            dcs_ref[k] = dcs[:, k * LANES:(k + 1) * LANES]
        dcp_ref[...] = _dot_nt(dm, wp_ref[...])

    row = pl.BlockSpec((tm, d), lambda i: (i, 0))
    half = pl.BlockSpec((tm, width), lambda i: (i, 0))
    return pl.pallas_call(
        body, name=name, grid=(n_rows // tm,),
        out_shape=[jax.ShapeDtypeStruct((n_rows, d), _MXU), jax.ShapeDtypeStruct((n_blk, n_rows, LANES), F32),
                   jax.ShapeDtypeStruct((n_rows, width), F32), jax.ShapeDtypeStruct((1, d), F32)],
        in_specs=[row, row, _full((1, d)), _resident(), _resident()],
        out_specs=[row, _lane_blocks(n_blk, tm), half, _full((1, d))], compiler_params=_params(),
    )(dho, mixed, g, wo_s, wo_p)


def _mix_in_bwd(du_s, du_p, h, dho, g, wi_s, wi_p, name):
    n_rows, d = h.shape
    width = du_p.shape[1]
    n_blk = width // LANES
    tm = _row_tile(n_rows)

    def body(dus_ref, dup_ref, h_ref, dho_ref, g_ref, ws_ref, wp_ref, dh_ref, dp_ref, dg_ref):
        @pl.when(pl.program_id(0) == 0)
        def _():
            dg_ref[...] = jnp.zeros_like(dg_ref)

        for k in range(n_blk):
            dp_ref[:, k * LANES:(k + 1) * LANES] = dus_ref[k].astype(_MXU)
        dup = dup_ref[...].astype(_MXU)
        dp_ref[:, width:2 * width] = dup
        dn = _dot_nt(dp_ref[:, 0:width], ws_ref[...]) + _dot_nt(dup, wp_ref[...])
        hv = h_ref[...]
        r = _rs(hv)
        hh = hv * r
        dg_ref[...] += jnp.sum(dn * hh, axis=0, keepdims=True)
        dhh = dn * g_ref[...]
        dh_ref[...] = dho_ref[...] + r * (dhh - hh * jnp.mean(dhh * hh, axis=-1, keepdims=True))

    row = pl.BlockSpec((tm, d), lambda i: (i, 0))
    half = pl.BlockSpec((tm, width), lambda i: (i, 0))
    return pl.pallas_call(
        body, name=name, grid=(n_rows // tm,),
        out_shape=[jax.ShapeDtypeStruct((n_rows, d), F32), jax.ShapeDtypeStruct((n_rows, 2 * width), _MXU),
                   jax.ShapeDtypeStruct((1, d), F32)],
        in_specs=[_lane_blocks(n_blk, tm), half, row, row, _full((1, d)), _resident(), _resident()],
        out_specs=[row, pl.BlockSpec((tm, 2 * width), lambda i: (i, 0)), _full((1, d))], compiler_params=_params(),
    )(du_s, du_p, h, dho, g, wi_s, wi_p)


def _adamw_update(w_ref, gv, m_ref, v_ref, d_ref, mo_ref, vo_ref):
    mn = ADAM_B1 * m_ref[...] + (1.0 - ADAM_B1) * gv
    vn = ADAM_B2 * v_ref[...] + (1.0 - ADAM_B2) * (gv * gv)
    m_hat = mn / (1.0 - ADAM_B1 ** ADAM_STEP)
    v_hat = vn / (1.0 - ADAM_B2 ** ADAM_STEP)
    d_ref[...] = -ADAM_LR * (m_hat / (jnp.sqrt(v_hat) + ADAM_EPS) + ADAM_WD * w_ref[...])
    mo_ref[...] = mn
    vo_ref[...] = vn


def _adamw(w, g, m, v, name):
    def body(w_ref, g_ref, m_ref, v_ref, d_ref, mo_ref, vo_ref):
        _adamw_update(w_ref, g_ref[...], m_ref, v_ref, d_ref, mo_ref, vo_ref)

    spec = _full(w.shape)
    return pl.pallas_call(
        body, name=name, grid=(1,), out_shape=[jax.ShapeDtypeStruct(w.shape, F32)] * 3,
        in_specs=[spec] * 4, out_specs=[spec] * 3, compiler_params=_params(),
    )(w, g, m, v)


def _adamw_many(ws, gs, ms, vs, name):
    n = len(ws)

    def body(*refs):
        w_refs, g_refs, m_refs, v_refs, d_refs, mo_refs, vo_refs = _split(refs, (n,) * 7)
        for k in range(n):
            _adamw_update(w_refs[k], g_refs[k][...], m_refs[k], v_refs[k], d_refs[k], mo_refs[k], vo_refs[k])

    specs = [_full(w.shape) for w in ws]
    shapes = [jax.ShapeDtypeStruct(w.shape, F32) for w in ws]
    outs = pl.pallas_call(
        body, name=name, grid=(1,), out_shape=shapes * 3, in_specs=specs * 4, out_specs=specs * 3,
        compiler_params=_params(),
    )(*ws, *gs, *ms, *vs)
    return outs[:n], outs[n:2 * n], outs[2 * n:]


def _adamw_slots(w, slots, m, v, name, after=()):
    def body(w_ref, s_ref, m_ref, v_ref, *rest):
        g_ref, d_ref, mo_ref, vo_ref = rest[len(after):]
        gv = s_ref[0].astype(F32)
        for k in range(1, N_DEV):
            gv = gv + s_ref[k].astype(F32)
        g_ref[...] = gv
        _adamw_update(w_ref, gv, m_ref, v_ref, d_ref, mo_ref, vo_ref)

    spec = _full(w.shape)
    return pl.pallas_call(
        body, name=name, grid=(1,), out_shape=[jax.ShapeDtypeStruct(w.shape, F32)] * 4,
        in_specs=[spec, _full(slots.shape), spec, spec] + [pl.BlockSpec(memory_space=pl.ANY)] * len(after),
        out_specs=[spec] * 4, compiler_params=_params(),
    )(w, slots, m, v, *after)


def _discretize(lam_re, lam_im, log_dt, b_re, b_im):
    dt = jnp.exp(log_dt)[:, None]
    decay = jnp.exp(lam_re * dt)
    ang = lam_im * dt
    a_re = decay * jnp.cos(ang)
    a_im = decay * jnp.sin(ang)
    nr = a_re - 1.0
    den = lam_re * lam_re + lam_im * lam_im
    q_re = (nr * lam_re + a_im * lam_im) / den
    q_im = (a_im * lam_re - nr * lam_im) / den
    bb_re = q_re[..., None] * b_re - q_im[..., None] * b_im
    bb_im = q_re[..., None] * b_im + q_im[..., None] * b_re
    return a_re, a_im, bb_re, bb_im


def _block_diag(p, n_blk):
    g, r, c = p.shape
    per = g // n_blk
    eye = jnp.eye(per, dtype=p.dtype)
    return jnp.einsum("jgrc,gk->jgrkc", p.reshape(n_blk, per, r, c), eye).reshape(n_blk, per * r, per * c)


def _block_diag_t(m, g):
    n_blk = m.shape[0]
    per = g // n_blk
    r, c = m.shape[1] // per, m.shape[2] // per
    eye = jnp.eye(per, dtype=m.dtype)
    return jnp.einsum("jgrkc,gk->jgrc", m.reshape(n_blk, per, r, per, c), eye).reshape(g, r, c)


def _pack_rows(parts, cols, multiple):
    flat = jnp.concatenate([p.reshape(-1) for p in parts])
    size = -(-flat.shape[0] // (cols * multiple)) * cols * multiple
    return jnp.pad(flat, (0, size - flat.shape[0])).reshape(-1, cols)


def _unpack(flat, shapes):
    out, pos = [], 0
    flat = flat.reshape(-1)
    for s in shapes:
        n = int(np.prod(s))
        out.append(flat[pos:pos + n].reshape(s))
        pos += n
    return out


def kernel(x, meta_tokens, ffn1_pre_norm, ffn1_post_norm, ffn1_w_gate, ffn1_w_up, ffn1_w_down, mix_pre_norm, mix_post_norm, w_in, ssm_lambda_re, ssm_lambda_im, ssm_log_dt, ssm_b_re, ssm_b_im, ssm_c_re, ssm_c_im, ssm_d, ssm_w_glu, pool_w, pool_scale, ssm_out_norm, pool_out_norm, w_out, ffn2_pre_norm, ffn2_post_norm, ffn2_w_gate, ffn2_w_up, ffn2_w_down, loss_target, m_meta_tokens, m_ffn1_pre_norm, m_ffn1_post_norm, m_ffn1_w_gate, m_ffn1_w_up, m_ffn1_w_down, m_mix_pre_norm, m_mix_post_norm, m_w_in, m_ssm_lambda_re, m_ssm_lambda_im, m_ssm_log_dt, m_ssm_b_re, m_ssm_b_im, m_ssm_c_re, m_ssm_c_im, m_ssm_d, m_ssm_w_glu, m_pool_w, m_pool_scale, m_ssm_out_norm, m_pool_out_norm, m_w_out, m_ffn2_pre_norm, m_ffn2_post_norm, m_ffn2_w_gate, m_ffn2_w_up, m_ffn2_w_down, v_meta_tokens, v_ffn1_pre_norm, v_ffn1_post_norm, v_ffn1_w_gate, v_ffn1_w_up, v_ffn1_w_down, v_mix_pre_norm, v_mix_post_norm, v_w_in, v_ssm_lambda_re, v_ssm_lambda_im, v_ssm_log_dt, v_ssm_b_re, v_ssm_b_im, v_ssm_c_re, v_ssm_c_im, v_ssm_d, v_ssm_w_glu, v_pool_w, v_pool_scale, v_ssm_out_norm, v_pool_out_norm, v_w_out, v_ffn2_pre_norm, v_ffn2_post_norm, v_ffn2_w_gate, v_ffn2_w_up, v_ffn2_w_down):
    args = dict(locals())
    names = ["meta_tokens", "ffn1_pre_norm", "ffn1_post_norm", "ffn1_w_gate", "ffn1_w_up", "ffn1_w_down", "mix_pre_norm",
             "mix_post_norm", "w_in", "ssm_lambda_re", "ssm_lambda_im", "ssm_log_dt", "ssm_b_re", "ssm_b_im", "ssm_c_re",
             "ssm_c_im", "ssm_d", "ssm_w_glu", "pool_w", "pool_scale", "ssm_out_norm", "pool_out_norm", "w_out",
             "ffn2_pre_norm", "ffn2_post_norm", "ffn2_w_gate", "ffn2_w_up", "ffn2_w_down"]
    sharded = ("meta_tokens", "ffn1_w_gate", "ffn1_w_up", "ffn1_w_down", "w_in", "w_out", "ffn2_w_gate", "ffn2_w_up",
               "ffn2_w_down")
    small = [n for n in names if n not in sharded]

    d = x.shape[-1]
    width = d // 2
    n_grp = ssm_lambda_re.shape[1]
    n_blk = width // LANES

    def stacked(gathered):
        return gathered.reshape(-1, d).astype(_MXU)

    def chunked(m):
        return m.reshape(N_DEV, -1, d)

    s_gate1, s_up1, s_down1 = _to_wire([ffn1_w_gate[0].T, ffn1_w_up[0].T, ffn1_w_down[0]], "wire_ffn1")
    s_win, s_wout, s_gate2, s_up2, s_down2 = _to_wire(
        [w_in[0], w_out[0], ffn2_w_gate[0].T, ffn2_w_up[0].T, ffn2_w_down[0]], "wire_rest")
    g_gate1, g_up1, g_down1, meta_all = _gather_two_level([s_gate1, s_up1, s_down1, meta_tokens], "gather_ffn1")
    wgt1, wut1, wd1 = stacked(g_gate1), stacked(g_up1), stacked(g_down1)
    meta_full = jnp.transpose(meta_all, (1, 0, 2)).reshape(N_META, d)

    (a1, b1, f1, h1, h0), (g_win, g_wout, g_gate2) = _ffn_fwd(
        None, ffn1_pre_norm, ffn1_post_norm, wgt1, wut1, wd1, "ffn1_fwd",
        xchg=_Xchg([s_win, s_wout, s_gate2], ["gather"] * 3), meta=meta_full, tokens=x[0])
    w_in_f, w_out_f = stacked(g_win), stacked(g_wout)

    a_re, a_im, bb_re, bb_im = _discretize(ssm_lambda_re[0], ssm_lambda_im[0], ssm_log_dt[0], ssm_b_re[0], ssm_b_im[0])
    a_pair = jnp.stack([a_re.reshape(1, -1), a_im.reshape(1, -1)])
    bmat_re = _block_diag(jnp.swapaxes(bb_re, 1, 2), n_blk).astype(_MXU)
    bmat_im = _block_diag(jnp.swapaxes(bb_im, 1, 2), n_blk).astype(_MXU)
    cmat_re = _block_diag(jnp.swapaxes(ssm_c_re[0], 1, 2), n_blk).astype(_MXU)
    cmat_im = _block_diag(jnp.swapaxes(ssm_c_im[0], 1, 2), n_blk).astype(_MXU)
    wz1 = _block_diag(ssm_w_glu[0][:, :, :SSM_GROUP_CH], n_blk).astype(_MXU)
    wz2 = _block_diag(ssm_w_glu[0][:, :, SSM_GROUP_CH:], n_blk).astype(_MXU)
    pool_wm = pool_w[0].astype(_MXU)

    n2, u_s, u_p = _mix_in_fwd(h1, mix_pre_norm, w_in_f, "mix_in_fwd")
    (x_re, x_im, cat_s), (g_up2, g_down2) = _ssm_forward(
        u_s, a_pair, bmat_re, bmat_im, cmat_re, cmat_im, wz1, wz2, ssm_d, ssm_out_norm, "ssm_fwd",
        xchg=_Xchg([s_up2, s_down2], ["gather"] * 2))
    wgt2, wut2, wd2 = stacked(g_gate2), stacked(g_up2), stacked(g_down2)
    cat_p = _pool_fwd(u_p, pool_wm, pool_scale, pool_out_norm, "pool_fwd")
    wo_s, wo_p = w_out_f[:width], w_out_f[width:]
    mixed, h2 = _mix_out_fwd(cat_s, cat_p, h1, mix_post_norm, wo_s, wo_p, "mix_out_fwd")

    (a2, b2, f2, dh3, sq_err), _ = _ffn_fwd(
        h2, ffn2_pre_norm, ffn2_post_norm, wgt2, wut2, wd2, "ffn2_fwd", target=loss_target[0])

    g, slots = {}, {}
    (dh2, da2, db2, s2, df2, nf2, g["ffn2_pre_norm"], g["ffn2_post_norm"]), _ = _ffn_bwd(
        dh3, h2, f2, a2, b2, ffn2_pre_norm, ffn2_post_norm, wgt2, wut2, wd2, "ffn2_bwd")
    dgate2, _ = _wgrad(da2, nf2, "ffn2_dgate")
    dup2, _ = _wgrad(db2, nf2, "ffn2_dup")
    ddown2, _ = _wgrad(s2, df2, "ffn2_ddown")

    dmixed, dcat_s, dcat_p, g["mix_post_norm"] = _mix_out_bwd(dh2, mixed, mix_post_norm, wo_s, wo_p, "mix_out_bwd")
    dwout = jnp.concatenate([_wgrad(cat_s, dmixed, "dwout_s")[0], _wgrad(cat_p, dmixed, "dwout_p")[0]], axis=0)
    ((du_s, g["ssm_out_norm"], g["ssm_d"], dwz1, dwz2, dcm_re, dcm_im, dbm_re, dbm_im, acc_re, acc_im),
     (slots["ffn2_w_gate"], slots["ffn2_w_up"], slots["ffn2_w_down"], slots["w_out"])) = _ssm_backward(
        u_s, x_re, x_im, dcat_s, a_pair, bmat_re, bmat_im, cmat_re, cmat_im, wz1, wz2, ssm_d, ssm_out_norm, "ssm_bwd",
        xchg=_Xchg([chunked(dgate2), chunked(dup2), chunked(ddown2), chunked(dwout)], ["scatter"] * 4))
    du_p, g["pool_out_norm"], g["pool_scale"], dpw = _pool_bwd(u_p, dcat_p, pool_wm, pool_scale, pool_out_norm, "pool_bwd")
    wi_s, wi_p = w_in_f[:, :width], w_in_f[:, width:]
    dh1, dproj, g["mix_pre_norm"] = _mix_in_bwd(du_s, du_p, h1, dh2, mix_pre_norm, wi_s, wi_p, "mix_in_bwd")
    dwin, _ = _wgrad(n2, dproj, "dwin")

    g["ssm_c_re"] = jnp.swapaxes(_block_diag_t(jnp.swapaxes(dcm_re, 1, 2), n_grp), 1, 2)[None]
    g["ssm_c_im"] = jnp.swapaxes(_block_diag_t(jnp.swapaxes(dcm_im, 1, 2), n_grp), 1, 2)[None]
    g["ssm_w_glu"] = jnp.concatenate([_block_diag_t(dwz1, n_grp), _block_diag_t(dwz2, n_grp)], axis=-1)[None]
    dbb_re = jnp.swapaxes(_block_diag_t(dbm_re, n_grp), 1, 2)
    dbb_im = jnp.swapaxes(_block_diag_t(dbm_im, n_grp), 1, 2)
    da_re, da_im = acc_re.reshape(a_re.shape), acc_im.reshape(a_re.shape)
    _, disc_vjp = jax.vjp(_discretize, ssm_lambda_re[0], ssm_lambda_im[0], ssm_log_dt[0], ssm_b_re[0], ssm_b_im[0])
    d_lre, d_lim, d_ldt, d_bre, d_bim = disc_vjp((da_re, da_im, dbb_re, dbb_im))
    g["ssm_lambda_re"], g["ssm_lambda_im"], g["ssm_log_dt"] = d_lre[None], d_lim[None], d_ldt[None]
    g["ssm_b_re"], g["ssm_b_im"] = d_bre[None], d_bim[None]
    g["pool_w"] = dpw[None]

    late = ["ffn1_pre_norm", "ffn1_post_norm"]
    early = [n for n in small if n not in late]
    early_vec = _pack_rows([g[n] for n in early] + [sq_err[:, :1]], 1024, SUBLANES)
    mix_started, mix_token = _split_start([chunked(dwin), early_vec], ["scatter", "gather"], "reduce_mix_start")
    (dx, da1, db1, s1, df1, nf1, g["ffn1_pre_norm"], g["ffn1_post_norm"], dmeta_part), _ = _ffn_bwd(
        dh1, h0, f1, a1, b1, ffn1_pre_norm, ffn1_post_norm, wgt1, wut1, wd1, "ffn1_bwd", split_meta=True,
        after=(mix_token,))
    slots["w_in"], recv_early = _split_wait(mix_started, [dmeta_part], "reduce_mix_wait")
    late_vec = _pack_rows([g[n] for n in late] + [dmeta_part], 1024, SUBLANES)
    dgate1, _ = _wgrad(da1, nf1, "ffn1_dgate")
    gate_started, token = _split_start([chunked(dgate1), late_vec], ["scatter", "gather"], "reduce_gate_start")
    dup1, _ = _wgrad(db1, nf1, "ffn1_dup", after=(token,))
    up_started, token = _split_start([chunked(dup1)], ["scatter"], "reduce_up_start")
    ddown1, _ = _wgrad(s1, df1, "ffn1_ddown", after=(token,))
    down_started, token = _split_start([chunked(ddown1)], ["scatter"], "reduce_down_start")
    waits = {"ffn1_w_gate": (gate_started, "reduce_gate_wait"), "ffn1_w_up": (up_started, "reduce_up_wait"),
             "ffn1_w_down": (down_started, "reduce_down_wait")}

    summed = _unpack(_sum_slots(recv_early, "sum_small_grads"), [g[n].shape for n in early] + [(1,)])
    for n, val in zip(early, summed):
        g[n] = val
    loss = (0.5 / d) * summed[-1][0]

    delta, new_m, new_v = {}, {}, {}
    after = (token,)
    for n in sorted(sharded[1:], key=lambda n: list(waits).index(n) if n in waits else -1) + ["meta_tokens"]:
        shape = args[n].shape
        two_d = (-1, shape[-1])
        w2, m2, v2 = args[n].reshape(two_d), args["m_" + n].reshape(two_d), args["v_" + n].reshape(two_d)
        if n in waits:
            slots[n], *rest = _split_wait(waits[n][0], after, waits[n][1])
            if rest:
                g[late[0]], g[late[1]], dmeta = _unpack(_sum_slots(rest[0], "sum_last_grads"), [(1, d), (1, d), (N_META, d)])
                g["meta_tokens"] = lax.dynamic_slice_in_dim(dmeta, _my_slot() * (d // N_DEV), d // N_DEV, axis=1)
        if n == "meta_tokens":
            dl, mn, vn = _adamw(w2, g[n], m2, v2, "adamw_" + n)
        elif n.endswith("gate") or n.endswith("up"):
            raw = _adamw_slots(w2.T, slots[n], m2.T, v2.T, "adamw_" + n, after=after)
            gs, dl, mn, vn = [t.T for t in raw]
        else:
            raw = gs, dl, mn, vn = _adamw_slots(w2, slots[n], m2, v2, "adamw_" + n, after=after)
        if n != "meta_tokens":
            g[n] = gs[None]
            after = (raw[1],)
        delta[n], new_m[n], new_v[n] = dl.reshape(shape), mn.reshape(shape), vn.reshape(shape)
    outs = _adamw_many([args[n] for n in small], [g[n] for n in small], [args["m_" + n] for n in small],
                       [args["v_" + n] for n in small], "adamw_small")
    for store, vals in zip((delta, new_m, new_v), outs):
        for n, val in zip(small, vals):
            store[n] = val

    grad_x = dx[None]
    return (loss, grad_x, *[g[n] for n in names], *[delta[n] for n in names], *[new_m[n] for n in names],
            *[new_v[n] for n in names])
```

```python
import functools
import math

import jax
import jax.numpy as jnp
import numpy as np
from jax import lax
from jax.experimental import pallas as pl
from jax.experimental.pallas import tpu as pltpu

F32 = jnp.float32
_MXU = jnp.bfloat16
_ACT = jnp.bfloat16
_WIRE = jnp.bfloat16

N_DEV = 8
N_META = 16
RMS_EPS = 1e-6
SSM_GROUP_CH = 16
LANES = 128
SUBLANES = 8
POOL_WINDOWS = (2, 4, 8, 16)
POOL_HALO = 16
ADAM_LR = 0.001
ADAM_B1 = 0.9
ADAM_B2 = 0.999
ADAM_EPS = 1e-08
ADAM_WD = 0.01
ADAM_STEP = 10
GELU_C0 = math.sqrt(2.0 / math.pi)
GELU_C1 = 0.044715
VMEM_LIMIT = 62 * 1024 * 1024

_NT = (((1,), (1,)), ((), ()))
_TN = (((0,), (0,)), ((), ()))


def _dot(a, b):
    return jnp.dot(a, b, preferred_element_type=F32)


def _dot_nt(a, b):
    return lax.dot_general(a, b, _NT, preferred_element_type=F32)


def _dot_tn(a, b):
    return lax.dot_general(a, b, _TN, preferred_element_type=F32)


def _rs(x):
    return lax.rsqrt(jnp.mean(x * x, axis=-1, keepdims=True) + RMS_EPS)


def _sigmoid(x):
    return 0.5 * jnp.tanh(0.5 * x) + 0.5


def _row_tile(n_rows, largest=432):
    for t in (432, 304, 48, 16):
        if t <= largest and n_rows % t == 0:
            return t
    raise ValueError(n_rows)


def _ff_chunk(d_ff):
    return d_ff // 2 if (d_ff // 2) % LANES == 0 else d_ff


def _params(n_axes=1):
    return pltpu.CompilerParams(dimension_semantics=("arbitrary",) * n_axes, vmem_limit_bytes=VMEM_LIMIT)


def _resident():
    return pl.BlockSpec(memory_space=pltpu.VMEM)


def _full(shape):
    nd = len(shape)
    return pl.BlockSpec(shape, lambda *_: (0,) * nd)


def _lane_blocks(n_blk, tm, tile_of=lambda i: i):
    return pl.BlockSpec((n_blk, tm, LANES), lambda i: (0, tile_of(i), 0))


PEER_ORDER = (1, 2, 4, 3, 5, 6, 7)


def _split(refs, counts):
    out, pos = [], 0
    for n in counts:
        out.append(refs[pos:pos + n])
        pos += n
    return out


def _peer(r):
    x, y, c = lax.axis_index("x"), lax.axis_index("y"), lax.axis_index("c")
    return (1 - x if r & 4 else x, 1 - y if r & 2 else y, 1 - c if r & 1 else c)


def _my_slot():
    return 4 * lax.axis_index("x") + 2 * lax.axis_index("y") + lax.axis_index("c")


class _Xchg:
    def __init__(self, srcs, kinds):
        self.srcs, self.kinds, self.n = list(srcs), list(kinds), len(srcs)
        self.out_shape = [jax.ShapeDtypeStruct((N_DEV,) + s.shape if k == "gather" else s.shape, s.dtype)
                          for s, k in zip(self.srcs, self.kinds)]
        self.specs = [pl.BlockSpec(memory_space=pl.ANY)] * self.n
        self.scratch = [pltpu.SemaphoreType.DMA((self.n * (N_DEV - 1),)), pltpu.SemaphoreType.DMA((self.n * (N_DEV - 1),)),
                        pltpu.SemaphoreType.DMA((self.n,))]

    def copies(self, src, dst, sems):
        send_sems, recv_sems, local_sems = sems
        me = _my_slot()
        out = []
        for a in range(self.n):
            mine = src[a] if self.kinds[a] == "gather" else src[a].at[me]
            out.append(pltpu.make_async_copy(mine, dst[a].at[me], local_sems.at[a]))
            for r in PEER_ORDER:
                px, py, pc = _peer(r)
                part = src[a] if self.kinds[a] == "gather" else src[a].at[4 * px + 2 * py + pc]
                k = a * (N_DEV - 1) + r - 1
                out.append(pltpu.make_async_remote_copy(
                    src_ref=part, dst_ref=dst[a].at[me], send_sem=send_sems.at[k], recv_sem=recv_sems.at[k],
                    device_id=(px, py, pc), device_id_type=pl.DeviceIdType.MESH))
        return out

    def start(self, src, dst, sems):
        for cp in self.copies(src, dst, sems):
            cp.start()

    def wait(self, src, dst, sems):
        for cp in self.copies(src, dst, sems):
            cp.wait()


class _NoXchg:
    n, srcs, out_shape, specs, scratch = 0, [], [], [], []

    def start(self, *_):
        pass

    wait = start


def _call(body, args, *, name, grid, out_shape, in_specs, out_specs, scratch_shapes=(), xchg=None, after=()):
    xc = xchg or _NoXchg()
    counts = (len(in_specs), xc.n, len(after), len(out_shape), xc.n, len(scratch_shapes), len(xc.scratch))

    def wrapped(*refs):
        ins, xsrc, _, outs, xdst, scr, sems = _split(refs, counts)
        ids = [pl.program_id(k) for k in range(len(grid))]
        if xc.n:
            @pl.when(functools.reduce(jnp.logical_and, [i == 0 for i in ids]))
            def _():
                xc.start(xsrc, xdst, sems)

        body(*ins, *outs, *scr)
        if xc.n:
            @pl.when(functools.reduce(jnp.logical_and, [i == g - 1 for i, g in zip(ids, grid)]))
            def _():
                xc.wait(xsrc, xdst, sems)

    res = pl.pallas_call(
        wrapped, name=name, grid=grid, out_shape=list(out_shape) + xc.out_shape,
        in_specs=list(in_specs) + xc.specs + [pl.BlockSpec(memory_space=pl.ANY)] * len(after),
        out_specs=list(out_specs) + xc.specs,
        scratch_shapes=list(scratch_shapes) + xc.scratch, compiler_params=_params(len(grid)),
    )(*args, *xc.srcs, *after)
    return res[:len(out_shape)], res[len(out_shape):]


def _exchange(srcs, kinds, name):
    xc = _Xchg(srcs, kinds)

    def body(*refs):
        src, dst, sems = _split(refs, (xc.n, xc.n, 3))
        xc.start(src, dst, sems)
        xc.wait(src, dst, sems)

    return pl.pallas_call(body, name=name, out_shape=xc.out_shape, in_specs=xc.specs, out_specs=xc.specs,
                          scratch_shapes=xc.scratch)(*srcs)


def _split_copies(kinds, src, land, send_sems, recv_sems):
    me = _my_slot()
    out = []
    for a, kind in enumerate(kinds):
        for r in PEER_ORDER:
            px, py, pc = _peer(r)
            k = a * (N_DEV - 1) + r - 1
            out.append(pltpu.make_async_remote_copy(
                src_ref=src[a] if kind == "gather" else src[a].at[4 * px + 2 * py + pc], dst_ref=land[a].at[me],
                send_sem=send_sems.at[k], recv_sem=recv_sems.at[k], device_id=(px, py, pc),
                device_id_type=pl.DeviceIdType.MESH))
    return out


_SPLIT_EFFECT = pltpu.SideEffectType.DATAFLOW_SIDE_EFFECTING


def _split_start(srcs, kinds, name):
    n = len(srcs)
    hbm, sem = pl.BlockSpec(memory_space=pltpu.HBM), pl.BlockSpec(memory_space=pltpu.SEMAPHORE)
    lands = [lax.empty((N_DEV,) + s.shape if k == "gather" else s.shape, s.dtype) for s, k in zip(srcs, kinds)]

    def body(*refs):
        src, land, (send_sems, recv_sems), _, (token,) = _split(refs, (n, n, 2, 2 * n, 1))
        for cp in _split_copies(kinds, src, land, send_sems, recv_sems):
            cp.start()
        token[...] = jnp.zeros_like(token)

    n_sem = n * (N_DEV - 1)
    out = pl.pallas_call(
        body, name=name,
        out_shape=[pltpu.SemaphoreType.DMA((n_sem,)), pltpu.SemaphoreType.DMA((n_sem,))]
        + [pltpu.HBM(a.shape, a.dtype) for a in list(srcs) + lands] + [jax.ShapeDtypeStruct((SUBLANES, LANES), F32)],
        in_specs=[hbm] * (2 * n), out_specs=[sem, sem] + [hbm] * (2 * n) + [pl.BlockSpec(memory_space=pltpu.VMEM)],
        input_output_aliases={k: 2 + k for k in range(2 * n)},
        compiler_params=pltpu.CompilerParams(has_side_effects=_SPLIT_EFFECT),
    )(*[pltpu.with_memory_space_constraint(a, pltpu.HBM) for a in list(srcs) + lands])
    return (kinds, out[0], out[1], out[2:2 + n], out[2 + n:2 + 2 * n]), out[-1]


def _split_wait(started, after, name):
    kinds, send_sems, recv_sems, srcs, lands = started
    n = len(srcs)
    hbm, sem = pl.BlockSpec(memory_space=pltpu.HBM), pl.BlockSpec(memory_space=pltpu.SEMAPHORE)

    def body(*refs):
        src, land, (send_sems, recv_sems) = _split(refs, (n, n, 2))[:3]
        for cp in _split_copies(kinds, src, land, send_sems, recv_sems):
            cp.wait_send()
            cp.wait_recv()

    out = pl.pallas_call(
        body, name=name, out_shape=[pltpu.HBM(a.shape, a.dtype) for a in list(srcs) + list(lands)],
        in_specs=[hbm] * (2 * n) + [sem, sem] + [pl.BlockSpec(memory_space=pl.ANY)] * len(after),
        out_specs=[hbm] * (2 * n), input_output_aliases={k: k for k in range(2 * n)},
        compiler_params=pltpu.CompilerParams(has_side_effects=_SPLIT_EFFECT),
    )(*srcs, *lands, send_sems, recv_sems, *after)
    me = _my_slot()
    filled = []
    for kind, sent, land in zip(kinds, out[:n], out[n:]):
        mine = sent[None] if kind == "gather" else lax.dynamic_slice_in_dim(sent, me, 1, axis=0)
        filled.append(lax.dynamic_update_slice_in_dim(land, mine, me, axis=0))
    return filled


def _gather_two_level(srcs, name):
    n = len(srcs)
    out_shape = [jax.ShapeDtypeStruct((N_DEV,) + s.shape, s.dtype) for s in srcs]
    chips = (2, 4, 6)

    def body(*refs):
        src, dst, (send_sems, recv_sems, local_sems) = _split(refs, (n, n, 3))
        x, y, c = lax.axis_index("x"), lax.axis_index("y"), lax.axis_index("c")
        me = 4 * x + 2 * y + c
        sibling = (x, y, 1 - c)

        def copy(a, k, slot, to, from_src=False):
            return pltpu.make_async_remote_copy(
                src_ref=src[a] if from_src else dst[a].at[slot], dst_ref=dst[a].at[slot],
                send_sem=send_sems.at[a * 7 + k], recv_sem=recv_sems.at[a * 7 + k],
                device_id=to, device_id_type=pl.DeviceIdType.MESH)

        def slot_of(r, core):
            px, py, _ = _peer(r)
            return 4 * px + 2 * py + core

        local = [pltpu.make_async_copy(src[a], dst[a].at[me], local_sems.at[a]) for a in range(n)]
        sent = []
        for a in range(n):
            local[a].start()
            sent.append(copy(a, 0, me, sibling, from_src=True))
            sent += [copy(a, 1 + j, me, _peer(r), from_src=True) for j, r in enumerate(chips)]
        for cp in sent:
            cp.start()
        for j, r in enumerate(chips):
            for a in range(n):
                copy(a, 1 + j, slot_of(r, c), _peer(r)).wait_recv()
                cp = copy(a, 4 + j, slot_of(r, c), sibling)
                cp.start()
                sent.append(cp)
        for a in range(n):
            copy(a, 0, slot_of(0, 1 - c), sibling).wait_recv()
            for j, r in enumerate(chips):
                copy(a, 4 + j, slot_of(r, 1 - c), sibling).wait_recv()
        for cp in local:
            cp.wait()
        for cp in sent:
            cp.wait_send()

    any_spec = pl.BlockSpec(memory_space=pl.ANY)
    return pl.pallas_call(
        body, name=name, out_shape=out_shape, in_specs=[any_spec] * n, out_specs=[any_spec] * n,
        scratch_shapes=[pltpu.SemaphoreType.DMA((n * 7,)), pltpu.SemaphoreType.DMA((n * 7,)), pltpu.SemaphoreType.DMA((n,))],
    )(*srcs)


def _to_wire(mats, name):
    def body(*refs):
        for src, dst in zip(refs[:len(mats)], refs[len(mats):]):
            dst[...] = src[...].astype(_WIRE)

    return pl.pallas_call(
        body, name=name, grid=(1,), out_shape=[jax.ShapeDtypeStruct(m.shape, _WIRE) for m in mats],
        in_specs=[_full(m.shape) for m in mats], out_specs=[_full(m.shape) for m in mats], compiler_params=_params(),
    )(*mats)


def _sum_slots(r, name):
    _, rows, cols = r.shape
    blk = rows
    for cand in (rows, 592, 512, 256, 128, 64, 32, 16):
        if rows % cand == 0 and N_DEV * cand * cols * r.dtype.itemsize <= 8 * 1024 * 1024:
            blk = cand
            break

    def body(r_ref, o_ref):
        acc = r_ref[0].astype(F32)
        for d in range(1, N_DEV):
            acc = acc + r_ref[d].astype(F32)
        o_ref[...] = acc

    return pl.pallas_call(
        body, name=name, grid=(rows // blk,), out_shape=jax.ShapeDtypeStruct((rows, cols), F32),
        in_specs=[pl.BlockSpec((N_DEV, blk, cols), lambda i: (0, i, 0))],
        out_specs=pl.BlockSpec((blk, cols), lambda i: (i, 0)), compiler_params=_params(),
    )(r)


def _token_tile_copy(tokens_ref, buf, sems, i, tm, write=False):
    if isinstance(i, int) and i == 0:
        far, near = tokens_ref.at[pl.ds(0, tm - N_META)], buf.at[0, pl.ds(N_META, tm - N_META)]
    else:
        start = i * tm - N_META if isinstance(i, int) else pl.multiple_of(i * tm - N_META, SUBLANES)
        far, near = tokens_ref.at[pl.ds(start, tm)], buf.at[i % 2]
    return pltpu.make_async_copy(near, far, sems.at[i % 2]) if write else pltpu.make_async_copy(far, near, sems.at[i % 2])


def _fetch_token_tile(tokens_ref, buf, sems, i, n_t, tm):
    @pl.when(i == 0)
    def _():
        _token_tile_copy(tokens_ref, buf, sems, 0, tm).start()

    @pl.when(i + 1 < n_t)
    def _():
        _token_tile_copy(tokens_ref, buf, sems, i + 1, tm).start()

    @pl.when(i == 0)
    def _():
        _token_tile_copy(tokens_ref, buf, sems, 0, tm).wait()

    @pl.when(i > 0)
    def _():
        _token_tile_copy(tokens_ref, buf, sems, i, tm).wait()


def _ffn_fwd(h, g_pre, g_post, wgt, wut, wd, name, xchg=None, *, meta=None, tokens=None, target=None, after=()):
    first = h is None
    d = wd.shape[1]
    n_rows = N_META + tokens.shape[0] if first else h.shape[0]
    d_ff = wd.shape[0]
    tm, fc = _row_tile(n_rows), _ff_chunk(d_ff)
    n_t, n_c = n_rows // tm, d_ff // fc

    def body(src_ref, side_ref, gpre_ref, gpost_ref, wgt_ref, wut_ref, wd_ref, a_ref, b_ref, f_ref, o1_ref, o2_ref,
             n_scr, acc, buf, sems):
        i, c = pl.program_id(0), pl.program_id(1)
        slot = i % 2

        @pl.when(c == 0)
        def _():
            if first:
                _fetch_token_tile(side_ref, buf, sems, i, n_t, tm)

                @pl.when(i == 0)
                def _():
                    buf[0, 0:N_META, :] = src_ref[...]

                hv = buf[slot]
                o2_ref[...] = hv
            else:
                _fetch_token_tile(side_ref, buf, sems, i, n_t, tm)

                @pl.when(i == 0)
                def _():
                    buf[0, 0:N_META, :] = jnp.zeros((N_META, d), F32)
                    o2_ref[...] = jnp.zeros_like(o2_ref)

                hv = src_ref[...]
            n_scr[...] = (hv * _rs(hv) * gpre_ref[...]).astype(_MXU)
            acc[...] = jnp.zeros_like(acc)

        rows = pl.ds(pl.multiple_of(c * fc, fc), fc)
        nv = n_scr[...]
        a = _dot_nt(nv, wgt_ref[rows, :])
        b = _dot_nt(nv, wut_ref[rows, :])
        a_ref[...] = a.astype(_ACT)
        b_ref[...] = b.astype(_ACT)
        s = a * _sigmoid(a) * b
        acc[...] += _dot(s.astype(_MXU), wd_ref[rows, :])

        @pl.when(c == n_c - 1)
        def _():
            f = acc[...]
            f_ref[...] = f
            step = 0.5 * (f * _rs(f) * gpost_ref[...])
            if first:
                o1_ref[...] = buf[slot] + step
            else:
                row = i * tm + lax.broadcasted_iota(jnp.int32, (tm, 1), 0)
                err = jnp.where(row >= N_META, (src_ref[...] + step) - buf[slot], 0.0)
                o1_ref[...] = err / d
                o2_ref[...] += jnp.sum(jnp.sum(err * err, axis=0, keepdims=True), axis=1, keepdims=True)

    row = pl.BlockSpec((tm, d), lambda i, c: (i, 0))
    chunk = pl.BlockSpec((tm, fc), lambda i, c: (i, c))
    hbm = pl.BlockSpec(memory_space=pl.ANY)
    if first:
        operands, specs = (meta, tokens), [_full(meta.shape), hbm]
        last_shape, last_spec = jax.ShapeDtypeStruct((n_rows, d), F32), row
    else:
        operands, specs = (h, target), [row, hbm]
        last_shape, last_spec = jax.ShapeDtypeStruct((1, LANES), F32), pl.BlockSpec((1, LANES), lambda i, c: (0, 0))
    return _call(
        body, (*operands, g_pre, g_post, wgt, wut, wd), name=name, grid=(n_t, n_c),
        out_shape=[jax.ShapeDtypeStruct((n_rows, d_ff), _ACT), jax.ShapeDtypeStruct((n_rows, d_ff), _ACT),
                   jax.ShapeDtypeStruct((n_rows, d), F32), jax.ShapeDtypeStruct((n_rows, d), F32), last_shape],
        in_specs=specs + [_full((1, d)), _full((1, d)), _resident(), _resident(), _resident()],
        out_specs=[chunk, chunk, row, row, last_spec],
        scratch_shapes=[pltpu.VMEM((tm, d), _MXU), pltpu.VMEM((tm, d), F32), pltpu.VMEM((2, tm, d), F32),
                        pltpu.SemaphoreType.DMA((2,))], xchg=xchg, after=after)


def _ffn_bwd(dho, h, f, a, b, g_pre, g_post, wgt, wut, wd, name, xchg=None, *, split_meta=False, after=()):
    n_rows, d = h.shape
    d_ff = wd.shape[0]
    tm, fc = _row_tile(n_rows), _ff_chunk(d_ff)
    n_t, n_c = n_rows // tm, d_ff // fc
    assert n_t >= 2

    def body(dho_ref, h_ref, f_ref, a_ref, b_ref, gpre_ref, gpost_ref, wgt_ref, wut_ref, wd_ref,
             dh_ref, da_ref, db_ref, s_ref, df_ref, n_ref, dgpre_ref, dgpost_ref, *rest):
        dn_acc = rest[-1]
        i, c = pl.program_id(0), pl.program_id(1)

        @pl.when((i == 0) & (c == 0))
        def _():
            dgpre_ref[...] = jnp.zeros_like(dgpre_ref)
            dgpost_ref[...] = jnp.zeros_like(dgpost_ref)

        @pl.when(c == 0)
        def _():
            fv = f_ref[...]
            rf = _rs(fv)
            fhat = fv * rf
            dy = 0.5 * dho_ref[...]
            dgpost_ref[...] += jnp.sum(dy * fhat, axis=0, keepdims=True)
            dfhat = dy * gpost_ref[...]
            df = rf * (dfhat - fhat * jnp.mean(dfhat * fhat, axis=-1, keepdims=True))
            df_ref[...] = df.astype(_MXU)
            hv = h_ref[...]
            n_ref[...] = (hv * _rs(hv) * gpre_ref[...]).astype(_MXU)
            dn_acc[...] = jnp.zeros_like(dn_acc)

        rows = pl.ds(pl.multiple_of(c * fc, fc), fc)
        ds = _dot_nt(df_ref[...], wd_ref[rows, :])
        av = a_ref[...].astype(F32)
        bv = b_ref[...].astype(F32)
        sg = _sigmoid(av)
        si = av * sg
        da = (ds * bv * (sg * (1.0 + av * (1.0 - sg)))).astype(_MXU)
        db = (ds * si).astype(_MXU)
        da_ref[...] = da
        db_ref[...] = db
        s_ref[...] = (si * bv).astype(_MXU)
        dn_acc[...] += _dot(da, wgt_ref[rows, :]) + _dot(db, wut_ref[rows, :])

        @pl.when(c == n_c - 1)
        def _():
            dn = dn_acc[...]
            hv = h_ref[...]
            r = _rs(hv)
            hhat = hv * r
            dgpre_ref[...] += jnp.sum(dn * hhat, axis=0, keepdims=True)
            dhh = dn * gpre_ref[...]
            dh = dho_ref[...] + r * (dhh - hhat * jnp.mean(dhh * hhat, axis=-1, keepdims=True))
            if not split_meta:
                dh_ref[...] = dh
                return
            dmeta_ref, buf, sems = rest[0], rest[1], rest[2]

            def out_copy(k):
                return _token_tile_copy(dh_ref, buf, sems, k, tm, write=True)

            @pl.when(i == 2)
            def _():
                out_copy(0).wait()

            @pl.when(i > 2)
            def _():
                out_copy(i - 2).wait()

            buf[i % 2] = dh

            @pl.when(i == 0)
            def _():
                dmeta_ref[...] = buf[0, 0:N_META, :]
                out_copy(0).start()

            @pl.when(i > 0)
            def _():
                out_copy(i).start()

            @pl.when(i == n_t - 1)
            def _():
                out_copy(n_t - 2).wait()
                out_copy(n_t - 1).wait()

    row = pl.BlockSpec((tm, d), lambda i, c: (i, 0))
    chunk = pl.BlockSpec((tm, fc), lambda i, c: (i, c))
    vec = pl.BlockSpec((1, d), lambda i, c: (0, 0))
    shapes = [jax.ShapeDtypeStruct((n_rows, d_ff), _MXU)] * 3 + [jax.ShapeDtypeStruct((n_rows, d), _MXU)] * 2 \
        + [jax.ShapeDtypeStruct((1, d), F32)] * 2
    specs = [chunk, chunk, chunk, row, row, vec, vec]
    scratch = [pltpu.VMEM((tm, d), F32)]
    if split_meta:
        shapes = [jax.ShapeDtypeStruct((n_rows - N_META, d), F32)] + shapes + [jax.ShapeDtypeStruct((N_META, d), F32)]
        specs = [pl.BlockSpec(memory_space=pl.ANY)] + specs + [pl.BlockSpec((N_META, d), lambda i, c: (0, 0))]
        scratch = [pltpu.VMEM((2, tm, d), F32), pltpu.SemaphoreType.DMA((2,))] + scratch
    else:
        shapes, specs = [jax.ShapeDtypeStruct((n_rows, d), F32)] + shapes, [row] + specs
    return _call(
        body, (dho, h, f, a, b, g_pre, g_post, wgt, wut, wd), name=name, grid=(n_t, n_c), out_shape=shapes,
        in_specs=[row, row, row, chunk, chunk, vec, vec, _resident(), _resident(), _resident()], out_specs=specs,
        scratch_shapes=scratch, xchg=xchg, after=after)


def _wgrad(xm, ym, name, xchg=None, after=()):
    n_rows, a_dim = xm.shape
    b_dim = ym.shape[1]
    tk = n_rows
    for cand in (2736, 1296, 432, 48, 16):
        if n_rows % cand == 0:
            tk = cand
            break
    ta = a_dim
    for cand in (1408, 1024, 512):
        if a_dim % cand == 0:
            ta = cand
            break

    n_k = n_rows // tk

    def body(x_ref, y_ref, o_ref, acc):
        k = pl.program_id(1)

        @pl.when(k == 0)
        def _():
            acc[...] = jnp.zeros_like(acc)

        acc[...] += _dot_tn(x_ref[...], y_ref[...])

        @pl.when(k == n_k - 1)
        def _():
            o_ref[...] = acc[...].astype(o_ref.dtype)

    (out,), extra = _call(
        body, (xm, ym), name=name, grid=(a_dim // ta, n_k), out_shape=[jax.ShapeDtypeStruct((a_dim, b_dim), _WIRE)],
        in_specs=[pl.BlockSpec((tk, ta), lambda j, k: (k, j)), pl.BlockSpec((tk, b_dim), lambda j, k: (k, 0))],
        out_specs=[pl.BlockSpec((ta, b_dim), lambda j, k: (j, 0))], scratch_shapes=[pltpu.VMEM((ta, b_dim), F32)],
        xchg=xchg, after=after)
    return out, extra


def _mix_in_fwd(h, g, w_in, name):
    n_rows, d = h.shape
    tm = _row_tile(n_rows)
    width = w_in.shape[1] // 2
    n_blk = width // LANES

    def body(h_ref, g_ref, w_ref, n_ref, us_ref, up_ref):
        hv = h_ref[...]
        nv = (hv * _rs(hv) * g_ref[...]).astype(_MXU)
        n_ref[...] = nv
        p = _dot(nv, w_ref[...])
        for k in range(n_blk):
            us_ref[k] = p[:, k * LANES:(k + 1) * LANES]
        up_ref[...] = p[:, width:]

    row = pl.BlockSpec((tm, d), lambda i: (i, 0))
    half = pl.BlockSpec((tm, width), lambda i: (i, 0))
    return pl.pallas_call(
        body, name=name, grid=(n_rows // tm,),
        out_shape=[jax.ShapeDtypeStruct((n_rows, d), _MXU), jax.ShapeDtypeStruct((n_blk, n_rows, LANES), F32),
                   jax.ShapeDtypeStruct((n_rows, width), F32)],
        in_specs=[row, _full((1, d)), _resident()], out_specs=[row, _lane_blocks(n_blk, tm), half],
        compiler_params=_params(),
    )(h, g, w_in)


def _gelu_parts(y):
    th = jnp.tanh(GELU_C0 * (y + GELU_C1 * (y * y * y)))
    return 0.5 * (1.0 + th), th


SCAN_LANES = 1024


def _regroup(src_ref, dst_ref, seg):
    for k in range(src_ref.shape[0]):
        for j in range(seg):
            dst_ref[j * SUBLANES:(j + 1) * SUBLANES, k * LANES:(k + 1) * LANES] = src_ref[k, pl.ds(j, SUBLANES, stride=seg), :]


def _ungroup(src_ref, dst_ref, seg):
    for k in range(dst_ref.shape[0]):
        for j in range(seg):
            dst_ref[k, pl.ds(j, SUBLANES, stride=seg), :] = src_ref[j * SUBLANES:(j + 1) * SUBLANES, k * LANES:(k + 1) * LANES]


def _rows_to_sublanes(rows):
    rid = lax.broadcasted_iota(jnp.int32, (SUBLANES, rows[0].shape[1]), 0)
    out = jnp.broadcast_to(rows[0], rid.shape)
    for s in range(1, SUBLANES):
        out = jnp.where(rid == s, rows[s], out)
    return out


def _fill_powers(a_ref, tab_ref, seg, reverse):
    for cb in range(tab_ref.shape[3] // SCAN_LANES):
        cols = pl.ds(cb * SCAN_LANES, SCAN_LANES)
        ar = jnp.broadcast_to(a_ref[0, :, cols], (SUBLANES, SCAN_LANES))
        ai = jnp.broadcast_to(a_ref[1, :, cols], (SUBLANES, SCAN_LANES))
        if reverse:
            ai = -ai

        def step(t, carry, cols=cols, ar=ar, ai=ai):
            pr, pi = carry
            j = seg - 1 - t if reverse else t
            tab_ref[0, j, :, cols] = pr
            tab_ref[1, j, :, cols] = pi
            return pr * ar - pi * ai, pr * ai + pi * ar

        lax.fori_loop(0, seg, step, (ar, ai))


def _segment_scan(re_ref, im_ref, tab_ref, car_re, car_im, seg, reverse, x_refs=None):
    n_all = re_ref.shape[1]
    first = seg - 1 if reverse else 0
    sums = []
    for cb in range(n_all // SCAN_LANES):
        cols = pl.ds(cb * SCAN_LANES, SCAN_LANES)
        ar, ai = tab_ref[0, first, :, cols], tab_ref[1, first, :, cols]

        def local(t, carry, cols=cols, ar=ar, ai=ai):
            r0 = pl.multiple_of((seg - 1 - t if reverse else t) * SUBLANES, SUBLANES)
            xr, xi = carry
            nr = ar * xr - ai * xi + re_ref[pl.ds(r0, SUBLANES), cols]
            ni = ar * xi + ai * xr + im_ref[pl.ds(r0, SUBLANES), cols]
            re_ref[pl.ds(r0, SUBLANES), cols] = nr
            im_ref[pl.ds(r0, SUBLANES), cols] = ni
            return nr, ni

        zero = jnp.zeros((SUBLANES, SCAN_LANES), F32)
        fr, fi = lax.fori_loop(0, seg, local, (zero, zero))

        last = 0 if reverse else seg - 1
        sr, si = tab_ref[0, last, 0:1, cols], tab_ref[1, last, 0:1, cols]
        cr, ci = car_re[:, cols], car_im[:, cols]
        rows_r, rows_i = [None] * SUBLANES, [None] * SUBLANES
        for s in (range(SUBLANES - 1, -1, -1) if reverse else range(SUBLANES)):
            rows_r[s], rows_i[s] = cr, ci
            cr, ci = sr * cr - si * ci + fr[s:s + 1], sr * ci + si * cr + fi[s:s + 1]
        car_re[:, cols] = cr
        car_im[:, cols] = ci
        cmr, cmi = _rows_to_sublanes(rows_r), _rows_to_sublanes(rows_i)

        def fix(t, carry, cols=cols, cmr=cmr, cmi=cmi):
            j = seg - 1 - t if reverse else t
            r0 = pl.multiple_of(j * SUBLANES, SUBLANES)
            pr, pi = tab_ref[0, j, :, cols], tab_ref[1, j, :, cols]
            gr = re_ref[pl.ds(r0, SUBLANES), cols] + (pr * cmr - pi * cmi)
            gi = im_ref[pl.ds(r0, SUBLANES), cols] + (pr * cmi + pi * cmr)
            re_ref[pl.ds(r0, SUBLANES), cols] = gr
            im_ref[pl.ds(r0, SUBLANES), cols] = gi
            if x_refs is None:
                return carry
            nxr, nxi, accr, acci = carry
            xr, xi = x_refs[0][pl.ds(r0, SUBLANES), cols], x_refs[1][pl.ds(r0, SUBLANES), cols]
            return gr, gi, accr + (xr * nxr + xi * nxi), acci + (xr * nxi - xi * nxr)

        if x_refs is None:
            lax.fori_loop(0, seg, fix, 0)
        else:
            fin = lax.fori_loop(0, seg, fix, (cmr, cmi, zero, zero))
            sums.append((jnp.sum(fin[2], axis=0, keepdims=True), jnp.sum(fin[3], axis=0, keepdims=True)))
    return sums


def _ssm_forward(u, a_pair, b_re, b_im, c_re, c_im, wz1, wz2, d_skip, g_out, name, xchg=None):
    n_rows = u.shape[1]
    n_blk, _, n_state = b_re.shape
    width, n_all = n_blk * LANES, n_blk * n_state
    tm = _row_tile(n_rows)
    seg = tm // SUBLANES

    def body(u_ref, a_ref, bre_ref, bim_ref, cre_ref, cim_ref, wz1_ref, wz2_ref, d_ref, g_ref,
             xre_ref, xim_ref, o_ref, car_re, car_im, ug, out_scr, blocks, tab_ref):
        @pl.when(pl.program_id(0) == 0)
        def _():
            car_re[...] = jnp.zeros_like(car_re)
            car_im[...] = jnp.zeros_like(car_im)
            _fill_powers(a_ref, tab_ref, seg, reverse=False)

        _regroup(u_ref, ug, seg)
        ub = ug[...].astype(_MXU)
        for j in range(n_blk):
            uj = ub[:, j * LANES:(j + 1) * LANES]
            xre_ref[:, j * n_state:(j + 1) * n_state] = _dot(uj, bre_ref[j])
            xim_ref[:, j * n_state:(j + 1) * n_state] = _dot(uj, bim_ref[j])
        _segment_scan(xre_ref, xim_ref, tab_ref, car_re, car_im, seg, reverse=False)

        ssq = None
        for j in range(n_blk):
            sl = slice(j * LANES, (j + 1) * LANES)
            st = slice(j * n_state, (j + 1) * n_state)
            yc = _dot(xre_ref[:, st].astype(_MXU), cre_ref[j]) - _dot(xim_ref[:, st].astype(_MXU), cim_ref[j])
            y = yc + d_ref[:, sl] * ug[:, sl]
            cdf, _ = _gelu_parts(y)
            gy = (y * cdf).astype(_MXU)
            out = _dot(gy, wz1_ref[j]) * _sigmoid(_dot(gy, wz2_ref[j]))
            out_scr[:, sl] = out
            part = jnp.sum(out * out, axis=-1, keepdims=True)
            ssq = part if ssq is None else ssq + part
        r = lax.rsqrt(ssq / width + RMS_EPS)
        out_scr[...] = out_scr[...] * r * g_ref[...]
        _ungroup(out_scr, blocks, seg)
        for k in range(n_blk):
            o_ref[:, k * LANES:(k + 1) * LANES] = blocks[k].astype(_MXU)

    half = pl.BlockSpec((tm, width), lambda i: (i, 0))
    state = pl.BlockSpec((tm, n_all), lambda i: (i, 0))
    return _call(
        body, (u, a_pair, b_re, b_im, c_re, c_im, wz1, wz2, d_skip, g_out), name=name, grid=(n_rows // tm,),
        out_shape=[jax.ShapeDtypeStruct((n_rows, n_all), F32)] * 2 + [jax.ShapeDtypeStruct((n_rows, width), _MXU)],
        in_specs=[_lane_blocks(n_blk, tm), _full(a_pair.shape), _full(b_re.shape), _full(b_im.shape), _full(c_re.shape),
                  _full(c_im.shape), _full(wz1.shape), _full(wz2.shape), _full((1, width)), _full((1, width))],
        out_specs=[state, state, half],
        scratch_shapes=[pltpu.VMEM((1, n_all), F32)] * 2 + [pltpu.VMEM((tm, width), F32)] * 2
        + [pltpu.VMEM((n_blk, tm, LANES), F32), pltpu.VMEM((2, seg, SUBLANES, n_all), F32)], xchg=xchg)


def _ssm_backward(u, x_re, x_im, dcat, a_pair, b_re, b_im, c_re, c_im, wz1, wz2, d_skip, g_out, name, xchg=None):
    n_rows = u.shape[1]
    n_blk, _, n_state = b_re.shape
    width, n_all = n_blk * LANES, n_blk * n_state
    tm = _row_tile(n_rows)
    n_t, seg = n_rows // tm, tm // SUBLANES

    def body(u_ref, xre_ref, xim_ref, dc_ref, a_ref, bre_ref, bim_ref, cre_ref, cim_ref, wz1_ref, wz2_ref, d_ref, g_ref,
             du_ref, dg_ref, dd_ref, dwz1_ref, dwz2_ref, dcre_ref, dcim_ref, dbre_ref, dbim_ref, are_ref, aim_ref,
             car_re, car_im, gre, gim, ug, y_s, z1_s, sg_s, out_s, gy_s, tab_ref):
        @pl.when(pl.program_id(0) == 0)
        def _():
            for ref in (dg_ref, dd_ref, dwz1_ref, dwz2_ref, dcre_ref, dcim_ref, dbre_ref, dbim_ref, are_ref, aim_ref,
                        car_re, car_im):
                ref[...] = jnp.zeros_like(ref)
            _fill_powers(a_ref, tab_ref, seg, reverse=True)

        _regroup(u_ref, ug, seg)
        ssq = None
        for j in range(n_blk):
            sl = slice(j * LANES, (j + 1) * LANES)
            st = slice(j * n_state, (j + 1) * n_state)
            yc = _dot(xre_ref[:, st].astype(_MXU), cre_ref[j]) - _dot(xim_ref[:, st].astype(_MXU), cim_ref[j])
            y = yc + d_ref[:, sl] * ug[:, sl]
            cdf, _ = _gelu_parts(y)
            gy = (y * cdf).astype(_MXU)
            z1 = _dot(gy, wz1_ref[j])
            sg = _sigmoid(_dot(gy, wz2_ref[j]))
            out = z1 * sg
            y_s[:, sl], z1_s[:, sl], sg_s[:, sl], out_s[:, sl], gy_s[:, sl] = y, z1, sg, out, gy
            part = jnp.sum(out * out, axis=-1, keepdims=True)
            ssq = part if ssq is None else ssq + part
        r = lax.rsqrt(ssq / width + RMS_EPS)
        ohat = out_s[...] * r
        _regroup(dc_ref, out_s, seg)
        dcv = out_s[...]
        dg_ref[...] += jnp.sum(dcv * ohat, axis=0, keepdims=True)
        doh = dcv * g_ref[...]
        out_s[...] = r * (doh - ohat * (jnp.sum(doh * ohat, axis=-1, keepdims=True) / width))

        for j in range(n_blk):
            sl = slice(j * LANES, (j + 1) * LANES)
            st = slice(j * n_state, (j + 1) * n_state)
            dout, sg, z1, y = out_s[:, sl], sg_s[:, sl], z1_s[:, sl], y_s[:, sl]
            dz1 = (dout * sg).astype(_MXU)
            dz2 = (dout * z1 * sg * (1.0 - sg)).astype(_MXU)
            gy = gy_s[:, sl]
            dwz1_ref[j] += _dot_tn(gy, dz1)
            dwz2_ref[j] += _dot_tn(gy, dz2)
            dgy = _dot_nt(dz1, wz1_ref[j]) + _dot_nt(dz2, wz2_ref[j])
            cdf, th = _gelu_parts(y)
            dy = dgy * (cdf + y * (0.5 * (1.0 - th * th) * GELU_C0 * (1.0 + 3.0 * GELU_C1 * (y * y))))
            uj = ug[:, sl]
            dd_ref[:, sl] += jnp.sum(dy * uj, axis=0, keepdims=True)
            z1_s[:, sl] = d_ref[:, sl] * dy
            dyb = dy.astype(_MXU)
            dcre_ref[j] += _dot_tn(dyb, xre_ref[:, st].astype(_MXU))
            dcim_ref[j] -= _dot_tn(dyb, xim_ref[:, st].astype(_MXU))
            gre[:, st] = _dot_nt(dyb, cre_ref[j])
            gim[:, st] = -_dot_nt(dyb, cim_ref[j])

        sums = _segment_scan(gre, gim, tab_ref, car_re, car_im, seg, reverse=True, x_refs=(xre_ref, xim_ref))
        for cb, (sum_re, sum_im) in enumerate(sums):
            cols = pl.ds(cb * SCAN_LANES, SCAN_LANES)
            are_ref[:, cols] += sum_re
            aim_ref[:, cols] += sum_im

        for j in range(n_blk):
            sl = slice(j * LANES, (j + 1) * LANES)
            st = slice(j * n_state, (j + 1) * n_state)
            ujb = ug[:, sl].astype(_MXU)
            grb, gib = gre[:, st].astype(_MXU), gim[:, st].astype(_MXU)
            dbre_ref[j] += _dot_tn(ujb, grb)
            dbim_ref[j] += _dot_tn(ujb, gib)
            z1_s[:, sl] += _dot_nt(grb, bre_ref[j]) + _dot_nt(gib, bim_ref[j])
        _ungroup(z1_s, du_ref, seg)

    half = _lane_blocks(n_blk, tm, lambda i: n_t - 1 - i)
    state = pl.BlockSpec((tm, n_all), lambda i: (n_t - 1 - i, 0))
    small = [(1, width), (1, width), wz1.shape, wz2.shape, (n_blk, LANES, n_state), (n_blk, LANES, n_state),
             (n_blk, LANES, n_state), (n_blk, LANES, n_state), (1, n_all), (1, n_all)]
    return _call(
        body, (u, x_re, x_im, dcat, a_pair, b_re, b_im, c_re, c_im, wz1, wz2, d_skip, g_out), name=name, grid=(n_t,),
        out_shape=[jax.ShapeDtypeStruct((n_blk, n_rows, LANES), F32)] + [jax.ShapeDtypeStruct(s, F32) for s in small],
        in_specs=[half, state, state, half, _full(a_pair.shape), _full(b_re.shape), _full(b_im.shape), _full(c_re.shape),
                  _full(c_im.shape), _full(wz1.shape), _full(wz2.shape), _full((1, width)), _full((1, width))],
        out_specs=[half] + [_full(s) for s in small],
        scratch_shapes=[pltpu.VMEM((1, n_all), F32)] * 2 + [pltpu.VMEM((tm, n_all), F32)] * 2
        + [pltpu.VMEM((tm, width), F32)] * 5 + [pltpu.VMEM((tm, width), _MXU),
                                                pltpu.VMEM((2, seg, SUBLANES, n_all), F32)], xchg=xchg)


def _pool_counts(tile, tm, window):
    t = tile * tm + lax.broadcasted_iota(jnp.int32, (tm, 1), 0)
    return jnp.minimum(t + 1, window).astype(F32)


def _pool_fwd(proj, pool_w, scale, g_out, name):
    n_rows = proj.shape[0]
    n_grp, grp, _ = pool_w.shape
    width = n_grp * grp
    tm = _row_tile(n_rows)

    def body(u_ref, pw_ref, sc_ref, g_ref, o_ref, ext, y_s):
        i = pl.program_id(0)

        @pl.when(i == 0)
        def _():
            ext[0:POOL_HALO, :] = jnp.zeros((POOL_HALO, width), F32)

        ext[POOL_HALO:, :] = u_ref[...]
        ssq = None
        for gi, w in enumerate(POOL_WINDOWS):
            sl = slice(gi * grp, (gi + 1) * grp)
            tot = ext[POOL_HALO:, sl]
            for k in range(1, w):
                tot = tot + ext[POOL_HALO - k:POOL_HALO - k + tm, sl]
            pooled = tot / _pool_counts(i, tm, w) - u_ref[:, sl]
            y = _dot(pooled.astype(_MXU), pw_ref[gi]) * sc_ref[:, sl]
            y_s[:, sl] = y
            part = jnp.sum(y * y, axis=-1, keepdims=True)
            ssq = part if ssq is None else ssq + part
        r = lax.rsqrt(ssq / width + RMS_EPS)
        o_ref[...] = (y_s[...] * r * g_ref[...]).astype(_MXU)
        ext[0:POOL_HALO, :] = u_ref[tm - POOL_HALO:, :]

    half_in = pl.BlockSpec((tm, width), lambda i: (i, 0))
    half = pl.BlockSpec((tm, width), lambda i: (i, 0))
    return pl.pallas_call(
        body, name=name, grid=(n_rows // tm,), out_shape=jax.ShapeDtypeStruct((n_rows, width), _MXU),
        in_specs=[half_in, _full(pool_w.shape), _full((1, width)), _full((1, width))], out_specs=half,
        scratch_shapes=[pltpu.VMEM((tm + POOL_HALO, width), F32), pltpu.VMEM((tm, width), F32)],
        compiler_params=_params(),
    )(proj, pool_w, scale, g_out)


def _pool_bwd(proj, dcat, pool_w, scale, g_out, name):
    n_rows = proj.shape[0]
    n_grp, grp, _ = pool_w.shape
    width = n_grp * grp
    tm = _row_tile(n_rows)
    n_t = n_rows // tm
    halo_blocks = tm // POOL_HALO

    def body(u_ref, up_ref, dc_ref, pw_ref, sc_ref, g_ref, du_ref, dg_ref, dsc_ref, dpw_ref, ext, qext, y_s, pl_s):
        i = pl.program_id(0)
        tile = n_t - 1 - i

        @pl.when(i == 0)
        def _():
            for ref in (dg_ref, dsc_ref, dpw_ref):
                ref[...] = jnp.zeros_like(ref)
            qext[tm:, :] = jnp.zeros((POOL_HALO, width), F32)

        ext[0:POOL_HALO, :] = jnp.where(tile > 0, up_ref[...], 0.0)
        ext[POOL_HALO:, :] = u_ref[...]
        ssq = None
        for gi, w in enumerate(POOL_WINDOWS):
            sl = slice(gi * grp, (gi + 1) * grp)
            tot = ext[POOL_HALO:, sl]
            for k in range(1, w):
                tot = tot + ext[POOL_HALO - k:POOL_HALO - k + tm, sl]
            pooled = (tot / _pool_counts(tile, tm, w) - u_ref[:, sl]).astype(_MXU)
            pl_s[:, sl] = pooled
            y0 = _dot(pooled, pw_ref[gi])
            y_s[:, sl] = y0
            y = y0 * sc_ref[:, sl]
            part = jnp.sum(y * y, axis=-1, keepdims=True)
            ssq = part if ssq is None else ssq + part
        r = lax.rsqrt(ssq / width + RMS_EPS)
        y0 = y_s[...]
        yhat = y0 * sc_ref[...] * r
        dcv = dc_ref[...]
        dg_ref[...] += jnp.sum(dcv * yhat, axis=0, keepdims=True)
        dyh = dcv * g_ref[...]
        dy = r * (dyh - yhat * (jnp.sum(dyh * yhat, axis=-1, keepdims=True) / width))
        dsc_ref[...] += jnp.sum(dy * y0, axis=0, keepdims=True)
        y_s[...] = dy * sc_ref[...]
        for gi, w in enumerate(POOL_WINDOWS):
            sl = slice(gi * grp, (gi + 1) * grp)
            dm = y_s[:, sl].astype(_MXU)
            dpw_ref[gi] += _dot_tn(pl_s[:, sl], dm)
            dpooled = _dot_nt(dm, pw_ref[gi])
            y_s[:, sl] = dpooled
            qext[0:tm, sl] = dpooled / _pool_counts(tile, tm, w)
        for gi, w in enumerate(POOL_WINDOWS):
            sl = slice(gi * grp, (gi + 1) * grp)
            tot = qext[0:tm, sl]
            for k in range(1, w):
                tot = tot + qext[k:k + tm, sl]
            du_ref[:, sl] = tot - y_s[:, sl]
        qext[tm:, :] = qext[0:POOL_HALO, :]

    half_in = pl.BlockSpec((tm, width), lambda i: (n_t - 1 - i, 0))
    prev = pl.BlockSpec((POOL_HALO, width), lambda i: (jnp.maximum((n_t - 1 - i) * halo_blocks - 1, 0), 0))
    half = pl.BlockSpec((tm, width), lambda i: (n_t - 1 - i, 0))
    return pl.pallas_call(
        body, name=name, grid=(n_t,),
        out_shape=[jax.ShapeDtypeStruct((n_rows, width), F32), jax.ShapeDtypeStruct((1, width), F32),
                   jax.ShapeDtypeStruct((1, width), F32), jax.ShapeDtypeStruct(pool_w.shape, F32)],
        in_specs=[half_in, prev, half, _full(pool_w.shape), _full((1, width)), _full((1, width))],
        out_specs=[half, _full((1, width)), _full((1, width)), _full(pool_w.shape)],
        scratch_shapes=[pltpu.VMEM((tm + POOL_HALO, width), F32), pltpu.VMEM((tm + POOL_HALO, width), F32),
                        pltpu.VMEM((tm, width), F32), pltpu.VMEM((tm, width), _MXU)],
        compiler_params=_params(),
    )(proj, proj, dcat, pool_w, scale, g_out)


def _mix_out_fwd(cat_s, cat_p, h, g, wo_s, wo_p, name):
    n_rows, d = h.shape
    width = cat_s.shape[1]
    tm = _row_tile(n_rows)

    def body(cs_ref, cp_ref, h_ref, g_ref, ws_ref, wp_ref, m_ref, ho_ref):
        m = _dot(cs_ref[...], ws_ref[...]) + _dot(cp_ref[...], wp_ref[...])
        m_ref[...] = m
        ho_ref[...] = h_ref[...] + m * _rs(m) * g_ref[...]

    row = pl.BlockSpec((tm, d), lambda i: (i, 0))
    half = pl.BlockSpec((tm, width), lambda i: (i, 0))
    return pl.pallas_call(
        body, name=name, grid=(n_rows // tm,), out_shape=[jax.ShapeDtypeStruct((n_rows, d), F32)] * 2,
        in_specs=[half, half, row, _full((1, d)), _resident(), _resident()], out_specs=[row, row],
        compiler_params=_params(),
    )(cat_s, cat_p, h, g, wo_s, wo_p)


def _mix_out_bwd(dho, mixed, g, wo_s, wo_p, name):
    n_rows, d = mixed.shape
    width = wo_s.shape[0]
    n_blk = width // LANES
    tm = _row_tile(n_rows)

    def body(dho_ref, m_ref, g_ref, ws_ref, wp_ref, dm_ref, dcs_ref, dcp_ref, dg_ref):
        @pl.when(pl.program_id(0) == 0)
        def _():
            dg_ref[...] = jnp.zeros_like(dg_ref)

        m = m_ref[...]
        r = _rs(m)
        mh = m * r
        dy = dho_ref[...]
        dg_ref[...] += jnp.sum(dy * mh, axis=0, keepdims=True)
        dmh = dy * g_ref[...]
        dm = (r * (dmh - mh * jnp.mean(dmh * mh, axis=-1, keepdims=True))).astype(_MXU)
        dm_ref[...] = dm
        dcs = _dot_nt(dm, ws_ref[...])
        for k in range(n_blk):
            dcs_ref[k] = dcs[:, k * LANES:(k + 1) * LANES]
        dcp_ref[...] = _dot_nt(dm, wp_ref[...])

    row = pl.BlockSpec((tm, d), lambda i: (i, 0))
    half = pl.BlockSpec((tm, width), lambda i: (i, 0))
    return pl.pallas_call(
        body, name=name, grid=(n_rows // tm,),
        out_shape=[jax.ShapeDtypeStruct((n_rows, d), _MXU), jax.ShapeDtypeStruct((n_blk, n_rows, LANES), F32),
                   jax.ShapeDtypeStruct((n_rows, width), F32), jax.ShapeDtypeStruct((1, d), F32)],
        in_specs=[row, row, _full((1, d)), _resident(), _resident()],
        out_specs=[row, _lane_blocks(n_blk, tm), half, _full((1, d))], compiler_params=_params(),
    )(dho, mixed, g, wo_s, wo_p)


def _mix_in_bwd(du_s, du_p, h, dho, g, wi_s, wi_p, name):
    n_rows, d = h.shape
    width = du_p.shape[1]
    n_blk = width // LANES
    tm = _row_tile(n_rows)

    def body(dus_ref, dup_ref, h_ref, dho_ref, g_ref, ws_ref, wp_ref, dh_ref, dp_ref, dg_ref):
        @pl.when(pl.program_id(0) == 0)
        def _():
            dg_ref[...] = jnp.zeros_like(dg_ref)

        for k in range(n_blk):
            dp_ref[:, k * LANES:(k + 1) * LANES] = dus_ref[k].astype(_MXU)
        dup = dup_ref[...].astype(_MXU)
        dp_ref[:, width:2 * width] = dup
        dn = _dot_nt(dp_ref[:, 0:width], ws_ref[...]) + _dot_nt(dup, wp_ref[...])
        hv = h_ref[...]
        r = _rs(hv)
        hh = hv * r
        dg_ref[...] += jnp.sum(dn * hh, axis=0, keepdims=True)
        dhh = dn * g_ref[...]
        dh_ref[...] = dho_ref[...] + r * (dhh - hh * jnp.mean(dhh * hh, axis=-1, keepdims=True))

    row = pl.BlockSpec((tm, d), lambda i: (i, 0))
    half = pl.BlockSpec((tm, width), lambda i: (i, 0))
    return pl.pallas_call(
        body, name=name, grid=(n_rows // tm,),
        out_shape=[jax.ShapeDtypeStruct((n_rows, d), F32), jax.ShapeDtypeStruct((n_rows, 2 * width), _MXU),
                   jax.ShapeDtypeStruct((1, d), F32)],
        in_specs=[_lane_blocks(n_blk, tm), half, row, row, _full((1, d)), _resident(), _resident()],
        out_specs=[row, pl.BlockSpec((tm, 2 * width), lambda i: (i, 0)), _full((1, d))], compiler_params=_params(),
    )(du_s, du_p, h, dho, g, wi_s, wi_p)


def _adamw_update(w_ref, gv, m_ref, v_ref, d_ref, mo_ref, vo_ref):
    mn = ADAM_B1 * m_ref[...] + (1.0 - ADAM_B1) * gv
    vn = ADAM_B2 * v_ref[...] + (1.0 - ADAM_B2) * (gv * gv)
    m_hat = mn / (1.0 - ADAM_B1 ** ADAM_STEP)
    v_hat = vn / (1.0 - ADAM_B2 ** ADAM_STEP)
    d_ref[...] = -ADAM_LR * (m_hat / (jnp.sqrt(v_hat) + ADAM_EPS) + ADAM_WD * w_ref[...])
    mo_ref[...] = mn
    vo_ref[...] = vn


def _adamw(w, g, m, v, name):
    def body(w_ref, g_ref, m_ref, v_ref, d_ref, mo_ref, vo_ref):
        _adamw_update(w_ref, g_ref[...], m_ref, v_ref, d_ref, mo_ref, vo_ref)

    spec = _full(w.shape)
    return pl.pallas_call(
        body, name=name, grid=(1,), out_shape=[jax.ShapeDtypeStruct(w.shape, F32)] * 3,
        in_specs=[spec] * 4, out_specs=[spec] * 3, compiler_params=_params(),
    )(w, g, m, v)


def _adamw_many(ws, gs, ms, vs, name):
    n = len(ws)

    def body(*refs):
        w_refs, g_refs, m_refs, v_refs, d_refs, mo_refs, vo_refs = _split(refs, (n,) * 7)
        for k in range(n):
            _adamw_update(w_refs[k], g_refs[k][...], m_refs[k], v_refs[k], d_refs[k], mo_refs[k], vo_refs[k])

    specs = [_full(w.shape) for w in ws]
    shapes = [jax.ShapeDtypeStruct(w.shape, F32) for w in ws]
    outs = pl.pallas_call(
        body, name=name, grid=(1,), out_shape=shapes * 3, in_specs=specs * 4, out_specs=specs * 3,
        compiler_params=_params(),
    )(*ws, *gs, *ms, *vs)
    return outs[:n], outs[n:2 * n], outs[2 * n:]


def _adamw_slots(w, slots, m, v, name, after=()):
    def body(w_ref, s_ref, m_ref, v_ref, *rest):
        g_ref, d_ref, mo_ref, vo_ref = rest[len(after):]
        gv = s_ref[0].astype(F32)
        for k in range(1, N_DEV):
            gv = gv + s_ref[k].astype(F32)
        g_ref[...] = gv
        _adamw_update(w_ref, gv, m_ref, v_ref, d_ref, mo_ref, vo_ref)

    spec = _full(w.shape)
    return pl.pallas_call(
        body, name=name, grid=(1,), out_shape=[jax.ShapeDtypeStruct(w.shape, F32)] * 4,
        in_specs=[spec, _full(slots.shape), spec, spec] + [pl.BlockSpec(memory_space=pl.ANY)] * len(after),
        out_specs=[spec] * 4, compiler_params=_params(),
    )(w, slots, m, v, *after)


def _discretize(lam_re, lam_im, log_dt, b_re, b_im):
    dt = jnp.exp(log_dt)[:, None]
    decay = jnp.exp(lam_re * dt)
    ang = lam_im * dt
    a_re = decay * jnp.cos(ang)
    a_im = decay * jnp.sin(ang)
    nr = a_re - 1.0
    den = lam_re * lam_re + lam_im * lam_im
    q_re = (nr * lam_re + a_im * lam_im) / den
    q_im = (a_im * lam_re - nr * lam_im) / den
    bb_re = q_re[..., None] * b_re - q_im[..., None] * b_im
    bb_im = q_re[..., None] * b_im + q_im[..., None] * b_re
    return a_re, a_im, bb_re, bb_im


def _block_diag(p, n_blk):
    g, r, c = p.shape
    per = g // n_blk
    eye = jnp.eye(per, dtype=p.dtype)
    return jnp.einsum("jgrc,gk->jgrkc", p.reshape(n_blk, per, r, c), eye).reshape(n_blk, per * r, per * c)


def _block_diag_t(m, g):
    n_blk = m.shape[0]
    per = g // n_blk
    r, c = m.shape[1] // per, m.shape[2] // per
    eye = jnp.eye(per, dtype=m.dtype)
    return jnp.einsum("jgrkc,gk->jgrc", m.reshape(n_blk, per, r, per, c), eye).reshape(g, r, c)


def _pack_rows(parts, cols, multiple):
    flat = jnp.concatenate([p.reshape(-1) for p in parts])
    size = -(-flat.shape[0] // (cols * multiple)) * cols * multiple
    return jnp.pad(flat, (0, size - flat.shape[0])).reshape(-1, cols)


def _unpack(flat, shapes):
    out, pos = [], 0
    flat = flat.reshape(-1)
    for s in shapes:
        n = int(np.prod(s))
        out.append(flat[pos:pos + n].reshape(s))
        pos += n
    return out


def kernel(x, meta_tokens, ffn1_pre_norm, ffn1_post_norm, ffn1_w_gate, ffn1_w_up, ffn1_w_down, mix_pre_norm, mix_post_norm, w_in, ssm_lambda_re, ssm_lambda_im, ssm_log_dt, ssm_b_re, ssm_b_im, ssm_c_re, ssm_c_im, ssm_d, ssm_w_glu, pool_w, pool_scale, ssm_out_norm, pool_out_norm, w_out, ffn2_pre_norm, ffn2_post_norm, ffn2_w_gate, ffn2_w_up, ffn2_w_down, loss_target, m_meta_tokens, m_ffn1_pre_norm, m_ffn1_post_norm, m_ffn1_w_gate, m_ffn1_w_up, m_ffn1_w_down, m_mix_pre_norm, m_mix_post_norm, m_w_in, m_ssm_lambda_re, m_ssm_lambda_im, m_ssm_log_dt, m_ssm_b_re, m_ssm_b_im, m_ssm_c_re, m_ssm_c_im, m_ssm_d, m_ssm_w_glu, m_pool_w, m_pool_scale, m_ssm_out_norm, m_pool_out_norm, m_w_out, m_ffn2_pre_norm, m_ffn2_post_norm, m_ffn2_w_gate, m_ffn2_w_up, m_ffn2_w_down, v_meta_tokens, v_ffn1_pre_norm, v_ffn1_post_norm, v_ffn1_w_gate, v_ffn1_w_up, v_ffn1_w_down, v_mix_pre_norm, v_mix_post_norm, v_w_in, v_ssm_lambda_re, v_ssm_lambda_im, v_ssm_log_dt, v_ssm_b_re, v_ssm_b_im, v_ssm_c_re, v_ssm_c_im, v_ssm_d, v_ssm_w_glu, v_pool_w, v_pool_scale, v_ssm_out_norm, v_pool_out_norm, v_w_out, v_ffn2_pre_norm, v_ffn2_post_norm, v_ffn2_w_gate, v_ffn2_w_up, v_ffn2_w_down):
    args = dict(locals())
    names = ["meta_tokens", "ffn1_pre_norm", "ffn1_post_norm", "ffn1_w_gate", "ffn1_w_up", "ffn1_w_down", "mix_pre_norm",
             "mix_post_norm", "w_in", "ssm_lambda_re", "ssm_lambda_im", "ssm_log_dt", "ssm_b_re", "ssm_b_im", "ssm_c_re",
             "ssm_c_im", "ssm_d", "ssm_w_glu", "pool_w", "pool_scale", "ssm_out_norm", "pool_out_norm", "w_out",
             "ffn2_pre_norm", "ffn2_post_norm", "ffn2_w_gate", "ffn2_w_up", "ffn2_w_down"]
    sharded = ("meta_tokens", "ffn1_w_gate", "ffn1_w_up", "ffn1_w_down", "w_in", "w_out", "ffn2_w_gate", "ffn2_w_up",
               "ffn2_w_down")
    small = [n for n in names if n not in sharded]

    d = x.shape[-1]
    width = d // 2
    n_grp = ssm_lambda_re.shape[1]
    n_blk = width // LANES

    def stacked(gathered):
        return gathered.reshape(-1, d).astype(_MXU)

    def chunked(m):
        return m.reshape(N_DEV, -1, d)

    s_gate1, s_up1, s_down1 = _to_wire([ffn1_w_gate[0].T, ffn1_w_up[0].T, ffn1_w_down[0]], "wire_ffn1")
    s_win, s_wout, s_gate2, s_up2, s_down2 = _to_wire(
        [w_in[0], w_out[0], ffn2_w_gate[0].T, ffn2_w_up[0].T, ffn2_w_down[0]], "wire_rest")
    g_gate1, g_up1, g_down1, meta_all = _gather_two_level([s_gate1, s_up1, s_down1, meta_tokens], "gather_ffn1")
    wgt1, wut1, wd1 = stacked(g_gate1), stacked(g_up1), stacked(g_down1)
    meta_full = jnp.transpose(meta_all, (1, 0, 2)).reshape(N_META, d)

    mixw_started, token = _split_start([s_win, s_wout], ["gather"] * 2, "gather_mix_start")
    ffn2w_started, token = _split_start([s_gate2, s_up2, s_down2], ["gather"] * 3, "gather_ffn2_start")
    (a1, b1, f1, h1, h0), _ = _ffn_fwd(
        None, ffn1_pre_norm, ffn1_post_norm, wgt1, wut1, wd1, "ffn1_fwd", meta=meta_full, tokens=x[0], after=(token,))
    g_win, g_wout = _split_wait(mixw_started, [h1], "gather_mix_wait")
    w_in_f, w_out_f = stacked(g_win), stacked(g_wout)

    a_re, a_im, bb_re, bb_im = _discretize(ssm_lambda_re[0], ssm_lambda_im[0], ssm_log_dt[0], ssm_b_re[0], ssm_b_im[0])
    a_pair = jnp.stack([a_re.reshape(1, -1), a_im.reshape(1, -1)])
    bmat_re = _block_diag(jnp.swapaxes(bb_re, 1, 2), n_blk).astype(_MXU)
    bmat_im = _block_diag(jnp.swapaxes(bb_im, 1, 2), n_blk).astype(_MXU)
    cmat_re = _block_diag(jnp.swapaxes(ssm_c_re[0], 1, 2), n_blk).astype(_MXU)
    cmat_im = _block_diag(jnp.swapaxes(ssm_c_im[0], 1, 2), n_blk).astype(_MXU)
    wz1 = _block_diag(ssm_w_glu[0][:, :, :SSM_GROUP_CH], n_blk).astype(_MXU)
    wz2 = _block_diag(ssm_w_glu[0][:, :, SSM_GROUP_CH:], n_blk).astype(_MXU)
    pool_wm = pool_w[0].astype(_MXU)

    n2, u_s, u_p = _mix_in_fwd(h1, mix_pre_norm, w_in_f, "mix_in_fwd")
    (x_re, x_im, cat_s), _ = _ssm_forward(
        u_s, a_pair, bmat_re, bmat_im, cmat_re, cmat_im, wz1, wz2, ssm_d, ssm_out_norm, "ssm_fwd")
    cat_p = _pool_fwd(u_p, pool_wm, pool_scale, pool_out_norm, "pool_fwd")
    wo_s, wo_p = w_out_f[:width], w_out_f[width:]
    mixed, h2 = _mix_out_fwd(cat_s, cat_p, h1, mix_post_norm, wo_s, wo_p, "mix_out_fwd")

    wgt2, wut2, wd2 = [stacked(t) for t in _split_wait(ffn2w_started, [h2], "gather_ffn2_wait")]
    (a2, b2, f2, dh3, sq_err), _ = _ffn_fwd(
        h2, ffn2_pre_norm, ffn2_post_norm, wgt2, wut2, wd2, "ffn2_fwd", target=loss_target[0])

    g, slots = {}, {}
    (dh2, da2, db2, s2, df2, nf2, g["ffn2_pre_norm"], g["ffn2_post_norm"]), _ = _ffn_bwd(
        dh3, h2, f2, a2, b2, ffn2_pre_norm, ffn2_post_norm, wgt2, wut2, wd2, "ffn2_bwd")
    dgate2, _ = _wgrad(da2, nf2, "ffn2_dgate")
    dup2, _ = _wgrad(db2, nf2, "ffn2_dup")
    ddown2, _ = _wgrad(s2, df2, "ffn2_ddown")

    dmixed, dcat_s, dcat_p, g["mix_post_norm"] = _mix_out_bwd(dh2, mixed, mix_post_norm, wo_s, wo_p, "mix_out_bwd")
    dwout = jnp.concatenate([_wgrad(cat_s, dmixed, "dwout_s")[0], _wgrad(cat_p, dmixed, "dwout_p")[0]], axis=0)
    ((du_s, g["ssm_out_norm"], g["ssm_d"], dwz1, dwz2, dcm_re, dcm_im, dbm_re, dbm_im, acc_re, acc_im),
     (slots["ffn2_w_gate"], slots["ffn2_w_up"], slots["ffn2_w_down"], slots["w_out"])) = _ssm_backward(
        u_s, x_re, x_im, dcat_s, a_pair, bmat_re, bmat_im, cmat_re, cmat_im, wz1, wz2, ssm_d, ssm_out_norm, "ssm_bwd",
        xchg=_Xchg([chunked(dgate2), chunked(dup2), chunked(ddown2), chunked(dwout)], ["scatter"] * 4))
    du_p, g["pool_out_norm"], g["pool_scale"], dpw = _pool_bwd(u_p, dcat_p, pool_wm, pool_scale, pool_out_norm, "pool_bwd")
    wi_s, wi_p = w_in_f[:, :width], w_in_f[:, width:]
    dh1, dproj, g["mix_pre_norm"] = _mix_in_bwd(du_s, du_p, h1, dh2, mix_pre_norm, wi_s, wi_p, "mix_in_bwd")
    dwin, _ = _wgrad(n2, dproj, "dwin")

    g["ssm_c_re"] = jnp.swapaxes(_block_diag_t(jnp.swapaxes(dcm_re, 1, 2), n_grp), 1, 2)[None]
    g["ssm_c_im"] = jnp.swapaxes(_block_diag_t(jnp.swapaxes(dcm_im, 1, 2), n_grp), 1, 2)[None]
    g["ssm_w_glu"] = jnp.concatenate([_block_diag_t(dwz1, n_grp), _block_diag_t(dwz2, n_grp)], axis=-1)[None]
    dbb_re = jnp.swapaxes(_block_diag_t(dbm_re, n_grp), 1, 2)
    dbb_im = jnp.swapaxes(_block_diag_t(dbm_im, n_grp), 1, 2)
    da_re, da_im = acc_re.reshape(a_re.shape), acc_im.reshape(a_re.shape)
    _, disc_vjp = jax.vjp(_discretize, ssm_lambda_re[0], ssm_lambda_im[0], ssm_log_dt[0], ssm_b_re[0], ssm_b_im[0])
    d_lre, d_lim, d_ldt, d_bre, d_bim = disc_vjp((da_re, da_im, dbb_re, dbb_im))
    g["ssm_lambda_re"], g["ssm_lambda_im"], g["ssm_log_dt"] = d_lre[None], d_lim[None], d_ldt[None]
    g["ssm_b_re"], g["ssm_b_im"] = d_bre[None], d_bim[None]
    g["pool_w"] = dpw[None]

    late = ["ffn1_pre_norm", "ffn1_post_norm"]
    early = [n for n in small if n not in late]
    early_vec = _pack_rows([g[n] for n in early] + [sq_err[:, :1]], 1024, SUBLANES)
    mix_started, mix_token = _split_start([chunked(dwin), early_vec], ["scatter", "gather"], "reduce_mix_start")
    (dx, da1, db1, s1, df1, nf1, g["ffn1_pre_norm"], g["ffn1_post_norm"], dmeta_part), _ = _ffn_bwd(
        dh1, h0, f1, a1, b1, ffn1_pre_norm, ffn1_post_norm, wgt1, wut1, wd1, "ffn1_bwd", split_meta=True,
        after=(mix_token,))
    slots["w_in"], recv_early = _split_wait(mix_started, [dmeta_part], "reduce_mix_wait")
    late_vec = _pack_rows([g[n] for n in late] + [dmeta_part], 1024, SUBLANES)
    dgate1, _ = _wgrad(da1, nf1, "ffn1_dgate")
    gate_started, token = _split_start([chunked(dgate1), late_vec], ["scatter", "gather"], "reduce_gate_start")
    dup1, _ = _wgrad(db1, nf1, "ffn1_dup", after=(token,))
    up_started, token = _split_start([chunked(dup1)], ["scatter"], "reduce_up_start")
    ddown1, _ = _wgrad(s1, df1, "ffn1_ddown", after=(token,))
    down_started, token = _split_start([chunked(ddown1)], ["scatter"], "reduce_down_start")
    waits = {"ffn1_w_gate": (gate_started, "reduce_gate_wait"), "ffn1_w_up": (up_started, "reduce_up_wait"),
             "ffn1_w_down": (down_started, "reduce_down_wait")}

    summed = _unpack(_sum_slots(recv_early, "sum_small_grads"), [g[n].shape for n in early] + [(1,)])
    for n, val in zip(early, summed):
        g[n] = val
    loss = (0.5 / d) * summed[-1][0]

    delta, new_m, new_v = {}, {}, {}
    after = (token,)
    for n in sorted(sharded[1:], key=lambda n: list(waits).index(n) if n in waits else -1) + ["meta_tokens"]:
        shape = args[n].shape
        two_d = (-1, shape[-1])
        w2, m2, v2 = args[n].reshape(two_d), args["m_" + n].reshape(two_d), args["v_" + n].reshape(two_d)
        if n in waits:
            slots[n], *rest = _split_wait(waits[n][0], after, waits[n][1])
            if rest:
                g[late[0]], g[late[1]], dmeta = _unpack(_sum_slots(rest[0], "sum_last_grads"), [(1, d), (1, d), (N_META, d)])
                g["meta_tokens"] = lax.dynamic_slice_in_dim(dmeta, _my_slot() * (d // N_DEV), d // N_DEV, axis=1)
        if n == "meta_tokens":
            dl, mn, vn = _adamw(w2, g[n], m2, v2, "adamw_" + n)
        elif n.endswith("gate") or n.endswith("up"):
            raw = _adamw_slots(w2.T, slots[n], m2.T, v2.T, "adamw_" + n, after=after)
            gs, dl, mn, vn = [t.T for t in raw]
        else:
            raw = gs, dl, mn, vn = _adamw_slots(w2, slots[n], m2, v2, "adamw_" + n, after=after)
        if n != "meta_tokens":
            g[n] = gs[None]
            after = (raw[1],)
        delta[n], new_m[n], new_v[n] = dl.reshape(shape), mn.reshape(shape), vn.reshape(shape)
    outs = _adamw_many([args[n] for n in small], [g[n] for n in small], [args["m_" + n] for n in small],
                       [args["v_" + n] for n in small], "adamw_small")
    for store, vals in zip((delta, new_m, new_v), outs):
        for n, val in zip(small, vals):
            store[n] = val

    grad_x = dx[None]
    return (loss, grad_x, *[g[n] for n in names], *[delta[n] for n in names], *[new_m[n] for n in names],
            *[new_v[n] for n in names])
```
